```python
import jax, jax.numpy as jnp
from jax import lax
import numpy as np

D_MODEL = 2048
BATCH = 8
SEQ = 8192
DEPTH = 1

GRID_W = 64
CTX_LEN = 256
N_MOD = 6
SSD_HEAD_DIM = 64
SSD_HEADS = 32
SSD_WIDTH = SSD_HEADS * SSD_HEAD_DIM
SSD_GROUPS = 4
SSD_HPG = SSD_HEADS // SSD_GROUPS
SSD_STATE = 128
SSD_BC = SSD_GROUPS * SSD_STATE
SSD_CHUNK = 128
CONV_W = 4
CONV_PAD_LEFT = 2
LRU_WIDTH = D_MODEL
LRU_HEADS = 16
LRU_HEAD_DIM = LRU_WIDTH // LRU_HEADS
LRU_C = 8.0
FFN_HIDDEN = ((8 * D_MODEL + 2) // 3 + 255) // 256 * 256
IN_SIZES = (SSD_WIDTH, SSD_WIDTH, SSD_BC, SSD_BC, 2 * SSD_HEADS, LRU_WIDTH, LRU_WIDTH)
IN_COLS = sum(IN_SIZES)
IN_SPLITS = tuple(int(s) for s in np.cumsum(IN_SIZES)[:-1])
EPS = 1e-6

kernel_name = "hybrid_ssd_rglru_prefix_dit_block"


def rmsnorm(t, g):
    tf = t.astype(jnp.float32)
    tf = tf * lax.rsqrt(jnp.mean(tf * tf, axis=-1, keepdims=True) + EPS)
    return (tf * g.astype(jnp.float32)).astype(t.dtype)


def modulate(h, shift, scale):
    return h * (1 + scale) + shift


def flip(t):
    return jnp.flip(t, axis=1)


def dwconv_centred(t, w, bias):
    L = t.shape[1]
    tp = jnp.pad(t, ((0, 0), (CONV_PAD_LEFT, CONV_W - 1 - CONV_PAD_LEFT), (0, 0)))
    out = bias
    for k in range(CONV_W):
        out = out + w[k] * tp[:, k:k + L]
    return out


def to_col_major(t, rows):
    b_, L, C = t.shape
    return t.reshape(b_, rows, GRID_W, C).transpose(0, 2, 1, 3).reshape(b_, L, C)


def to_row_major(t, rows):
    b_, L, C = t.shape
    return t.reshape(b_, GRID_W, rows, C).transpose(0, 2, 1, 3).reshape(b_, L, C)


def _ssd_chunks(x, dt, A, B):
    b_, L, G, E, P = x.shape
    nc = L // SSD_CHUNK
    x = x.reshape(b_, nc, SSD_CHUNK, G, E, P)
    dt = dt.reshape(b_, nc, SSD_CHUNK, G, E)
    B = B.reshape(b_, nc, SSD_CHUNK, G, SSD_STATE)
    xdt = x * dt[..., None]
    a_cum = jnp.cumsum(dt * A, axis=2)
    decay_to_end = jnp.exp(a_cum[:, :, -1:] - a_cum)
    states = jnp.einsum('bcqgn,bcqge,bcqgep->bcgepn', B, decay_to_end, xdt)
    chunk_decay = jnp.exp(a_cum[:, :, -1])
    return xdt, B, a_cum, states, chunk_decay


def _carry_states(states, chunk_decay, h0):
    def step(h, inp):
        s, d = inp
        return d[..., None, None] * h + s, h
    final, entering = lax.scan(step, h0, (jnp.moveaxis(states, 1, 0), jnp.moveaxis(chunk_decay, 1, 0)))
    return jnp.moveaxis(entering, 0, 1), final


def ssd_scan(x, dt, A, B, C, h0):
    b_, L, G, E, P = x.shape
    xdt, Bc, a_cum, states, chunk_decay = _ssd_chunks(x, dt, A, B)
    entering, final = _carry_states(states, chunk_decay, h0)
    Cc = C.reshape(Bc.shape)
    idx = jnp.arange(SSD_CHUNK)
    lower = (idx[:, None] >= idx[None, :])[None, None, :, :, None, None]
    seg = a_cum[:, :, :, None] - a_cum[:, :, None, :]
    Lmat = jnp.exp(jnp.where(lower, seg, -jnp.inf))
    CB = jnp.einsum('bcign,bcjgn->bcijg', Cc, Bc)
    y_diag = jnp.einsum('bcijg,bcijge,bcjgep->bcigep', CB, Lmat, xdt)
    y_off = jnp.einsum('bcign,bcige,bcgepn->bcigep', Cc, jnp.exp(a_cum), entering)
    return (y_diag + y_off).reshape(b_, L, G, E, P), final


def ssd_final_state(x, dt, A, B, h0):
    _, _, _, states, chunk_decay = _ssd_chunks(x, dt, A, B)
    _, final = _carry_states(states, chunk_decay, h0)
    return final


def ssd_inputs(xs, bs, cs, dt_raw, lp):
    f32 = jnp.float32
    xbc = jax.nn.silu(dwconv_centred(jnp.concatenate([xs, bs, cs], axis=-1), lp['ssd_conv_w'], lp['ssd_conv_b']))
    xs, bs, cs = jnp.split(xbc, [SSD_WIDTH, SSD_WIDTH + SSD_BC], axis=-1)
    b_, L = xs.shape[:2]
    xs = xs.reshape(b_, L, SSD_GROUPS, SSD_HPG, SSD_HEAD_DIM).astype(f32)
    bs = bs.reshape(b_, L, SSD_GROUPS, SSD_STATE).astype(f32)
    cs = cs.reshape(b_, L, SSD_GROUPS, SSD_STATE).astype(f32)
    dt = jax.nn.softplus(dt_raw.astype(f32).reshape(b_, L, 2, SSD_HEADS) + lp['ssd_dt_bias'].astype(f32))
    dt = dt.reshape(b_, L, 2, SSD_GROUPS, SSD_HPG)
    A = -jnp.exp(lp['ssd_a_log'].astype(f32)).reshape(2, SSD_GROUPS, SSD_HPG)
    return xs, bs, cs, dt, A


def lru_coeffs(xr, w_a, b_a, w_x, b_x, lam):
    b_, L, W = xr.shape
    xh = xr.reshape(b_, L, LRU_HEADS, LRU_HEAD_DIM)
    f32 = jnp.float32
    r = jax.nn.sigmoid(jnp.einsum('blhi,hij->blhj', xh, w_a.astype(f32)).reshape(b_, L, W) + b_a.astype(f32))
    i = jax.nn.sigmoid(jnp.einsum('blhi,hij->blhj', xh, w_x.astype(f32)).reshape(b_, L, W) + b_x.astype(f32))
    log_a = -LRU_C * r * jax.nn.softplus(-lam.astype(f32))
    a = jnp.exp(log_a)
    return a, jnp.sqrt(-jnp.expm1(2 * log_a)) * (i * xr)


def linear_scan(a, u, h0):
    u = u.at[:, 0].add(a[:, 0] * h0)
    def comb(lhs, rhs):
        return lhs[0] * rhs[0], rhs[0] * lhs[1] + rhs[1]
    _, h = lax.associative_scan(comb, (a, u), axis=1)
    return h


def lru_bidir(xr_seq, lp, h0_f, h0_b):
    a_f, u_f = lru_coeffs(xr_seq, lp['lru_w_a'][0], lp['lru_b_a'][0], lp['lru_w_x'][0], lp['lru_b_x'][0], lp['lru_lambda'][0])
    a_b, u_b = lru_coeffs(xr_seq, lp['lru_w_a'][1], lp['lru_b_a'][1], lp['lru_w_x'][1], lp['lru_b_x'][1], lp['lru_lambda'][1])
    h_f = linear_scan(a_f, u_f, h0_f)
    h_b_rev = linear_scan(flip(a_b), flip(u_b), h0_b)
    return h_f + flip(h_b_rev), h_f[:, -1], h_b_rev[:, -1]


def mixer_full(h, lp, init, rows):
    z, xs, bs, cs, dt_raw, xr, yr = jnp.split(h @ lp['w_in'], IN_SPLITS, axis=-1)
    b_, L = h.shape[:2]
    xsh, bsh, csh, dt, A = ssd_inputs(xs, bs, cs, dt_raw, lp)
    y_f, s_f = ssd_scan(xsh, dt[:, :, 0], A[0], bsh, csh, init[0])
    y_b, s_b = ssd_scan(flip(xsh), flip(dt[:, :, 1]), A[1], flip(bsh), flip(csh), init[1])
    d_skip = lp['ssd_d'].astype(jnp.float32).reshape(SSD_GROUPS, SSD_HPG)[..., None]
    y = (y_f + flip(y_b) + d_skip * xsh).reshape(b_, L, SSD_WIDTH)
    y = rmsnorm(y * jax.nn.silu(z.astype(jnp.float32)), lp['ssd_norm'])
    o_s = y.astype(h.dtype) @ lp['w_out_ssd']
    xr_seq = xr if rows is None else to_col_major(xr, rows)
    xr_seq = dwconv_centred(xr_seq, lp['lru_conv_w'], lp['lru_conv_b']).astype(jnp.float32)
    r_out, f_f, f_b = lru_bidir(xr_seq, lp, init[2], init[3])
    if rows is not None:
        r_out = to_row_major(r_out, rows)
    o_r = (r_out * jax.nn.gelu(yr.astype(jnp.float32))).astype(h.dtype) @ lp['w_out_lru']
    g_s, g_r = jnp.split(jax.nn.sigmoid(h @ lp['w_gate'] + lp['b_gate']), 2, axis=-1)
    out = (g_s * o_s + g_r * o_r) @ lp['w_o']
    return out, (s_f, s_b, f_f, f_b)


def mixer_states(hc, lp):
    _, xs, bs, cs, dt_raw, xr, _ = jnp.split(hc @ lp['w_in'], IN_SPLITS, axis=-1)
    b_ = hc.shape[0]
    xsh, bsh, _, dt, A = ssd_inputs(xs, bs, cs, dt_raw, lp)
    s0 = jnp.zeros((b_, SSD_GROUPS, SSD_HPG, SSD_HEAD_DIM, SSD_STATE), jnp.float32)
    s_f = ssd_final_state(xsh, dt[:, :, 0], A[0], bsh, s0)
    s_b = ssd_final_state(flip(xsh), flip(dt[:, :, 1]), A[1], flip(bsh), s0)
    xr_seq = dwconv_centred(xr, lp['lru_conv_w'], lp['lru_conv_b']).astype(jnp.float32)
    h0 = jnp.zeros((b_, LRU_WIDTH), jnp.float32)
    _, f_f, f_b = lru_bidir(xr_seq, lp, h0, h0)
    return (s_f, s_b, f_f, f_b)


def zero_states(b_):
    s0 = jnp.zeros((b_, SSD_GROUPS, SSD_HPG, SSD_HEAD_DIM, SSD_STATE), jnp.float32)
    h0 = jnp.zeros((b_, LRU_WIDTH), jnp.float32)
    return (s0, s0, h0, h0)


def swiglu(h, w13, w2):
    g, u = jnp.split(h @ w13, 2, axis=-1)
    return (jax.nn.silu(g) * u) @ w2


def _fwd_setup_inputs(seed: int = 0) -> dict:
    key = jax.random.key(seed)
    ks = jax.random.split(key, 30)
    D = D_MODEL
    f32 = jnp.float32

    def nrm(k, shape, fan_in, s=1.0):
        return jax.random.normal(k, shape, f32) * (s * fan_in ** -0.5)

    def small(k, shape, s=0.02):
        return jax.random.normal(k, shape, f32) * s

    dt0 = jnp.exp(jax.random.uniform(ks[10], (DEPTH, 2, SSD_HEADS), f32, np.log(1e-3), np.log(1e-1)))
    a_target = jax.random.uniform(ks[21], (DEPTH, 2, LRU_WIDTH), f32, 0.9, 0.999)
    sig_lam = a_target ** (1.0 / LRU_C)
    return {
        "x": jax.random.normal(ks[0], (BATCH, SEQ, D), f32),
        "c": jax.random.normal(ks[1], (BATCH, D), f32),
        "ctx": jax.random.normal(ks[2], (BATCH, CTX_LEN, D), f32),
        "c_ctx": jax.random.normal(ks[3], (D,), f32),
        "w_ada": nrm(ks[4], (DEPTH, D, N_MOD * D), D, 0.5),
        "b_ada": small(ks[5], (DEPTH, N_MOD * D)),
        "norm_mix": 1.0 + small(ks[6], (DEPTH, D)),
        "norm_ffn": 1.0 + small(ks[7], (DEPTH, D)),
        "w_in": nrm(ks[8], (DEPTH, D, IN_COLS), D),
        "ssd_conv_w": nrm(ks[9], (DEPTH, CONV_W, SSD_WIDTH + 2 * SSD_BC), CONV_W),
        "ssd_conv_b": small(ks[11], (DEPTH, SSD_WIDTH + 2 * SSD_BC)),
        "ssd_dt_bias": dt0 + jnp.log(-jnp.expm1(-dt0)),
        "ssd_a_log": jnp.log(jax.random.uniform(ks[12], (DEPTH, 2, SSD_HEADS), f32, 1.0, 16.0)),
        "ssd_d": 1.0 + small(ks[13], (DEPTH, SSD_HEADS)),
        "ssd_norm": 1.0 + small(ks[14], (DEPTH, SSD_WIDTH)),
        "w_out_ssd": nrm(ks[15], (DEPTH, SSD_WIDTH, D), SSD_WIDTH),
        "lru_conv_w": nrm(ks[16], (DEPTH, CONV_W, LRU_WIDTH), CONV_W),
        "lru_conv_b": small(ks[17], (DEPTH, LRU_WIDTH)),
        "lru_w_a": nrm(ks[18], (DEPTH, 2, LRU_HEADS, LRU_HEAD_DIM, LRU_HEAD_DIM), LRU_HEAD_DIM),
        "lru_b_a": small(ks[19], (DEPTH, 2, LRU_WIDTH)),
        "lru_w_x": nrm(ks[20], (DEPTH, 2, LRU_HEADS, LRU_HEAD_DIM, LRU_HEAD_DIM), LRU_HEAD_DIM),
        "lru_b_x": small(ks[22], (DEPTH, 2, LRU_WIDTH)),
        "lru_lambda": jnp.log(sig_lam) - jnp.log1p(-sig_lam),
        "w_out_lru": nrm(ks[23], (DEPTH, LRU_WIDTH, D), LRU_WIDTH),
        "w_gate": nrm(ks[24], (DEPTH, D, 2 * D), D),
        "b_gate": small(ks[25], (DEPTH, 2 * D)),
        "w_o": nrm(ks[26], (DEPTH, D, D), D),
        "ffn_w13": nrm(ks[27], (DEPTH, D, 2 * FFN_HIDDEN), D),
        "ffn_w2": nrm(ks[28], (DEPTH, FFN_HIDDEN, D), FFN_HIDDEN),
        "final_norm": 1.0 + small(ks[29], (D,)),
    }


def _fwd_reference(x, c, ctx, c_ctx, w_ada, b_ada, norm_mix, norm_ffn, w_in, ssd_conv_w, ssd_conv_b,
              ssd_dt_bias, ssd_a_log, ssd_d, ssd_norm, w_out_ssd, lru_conv_w, lru_conv_b,
              lru_w_a, lru_b_a, lru_w_x, lru_b_x, lru_lambda, w_out_lru, w_gate, b_gate, w_o,
              ffn_w13, ffn_w2, final_norm):
    rows = x.shape[1] // GRID_W
    for l in range(DEPTH):
        lp = dict(w_in=w_in[l], ssd_conv_w=ssd_conv_w[l], ssd_conv_b=ssd_conv_b[l],
                  ssd_dt_bias=ssd_dt_bias[l], ssd_a_log=ssd_a_log[l], ssd_d=ssd_d[l],
                  ssd_norm=ssd_norm[l], w_out_ssd=w_out_ssd[l], lru_conv_w=lru_conv_w[l],
                  lru_conv_b=lru_conv_b[l], lru_w_a=lru_w_a[l], lru_b_a=lru_b_a[l],
                  lru_w_x=lru_w_x[l], lru_b_x=lru_b_x[l], lru_lambda=lru_lambda[l],
                  w_out_lru=w_out_lru[l], w_gate=w_gate[l], b_gate=b_gate[l], w_o=w_o[l])
        mod = jax.nn.silu(c) @ w_ada[l] + b_ada[l]
        sh_m, sc_m, g_m, sh_f, sc_f, g_f = jnp.split(mod[:, None, :], N_MOD, axis=-1)
        mod_c = jax.nn.silu(c_ctx) @ w_ada[l] + b_ada[l]
        csh_m, csc_m, cg_m, csh_f, csc_f, cg_f = jnp.split(mod_c, N_MOD, axis=-1)
        hc = modulate(rmsnorm(ctx, norm_mix[l]), csh_m, csc_m)
        if l + 1 < DEPTH:
            mix_c, ctx_states = mixer_full(hc, lp, zero_states(ctx.shape[0]), None)
            ctx = ctx + cg_m * mix_c
            ctx = ctx + cg_f * swiglu(modulate(rmsnorm(ctx, norm_ffn[l]), csh_f, csc_f), ffn_w13[l], ffn_w2[l])
        else:
            ctx_states = mixer_states(hc, lp)
        h = modulate(rmsnorm(x, norm_mix[l]), sh_m, sc_m)
        mix, _ = mixer_full(h, lp, ctx_states, rows)
        x = x + g_m * mix
        h = modulate(rmsnorm(x, norm_ffn[l]), sh_f, sc_f)
        x = x + g_f * swiglu(h, ffn_w13[l], ffn_w2[l])
    return rmsnorm(x, final_norm)


import jax as _jax
import jax.numpy as _jnp

TWIN_FORMAT = 'train_step'
FWD_PARAMS = ['x', 'c', 'ctx', 'c_ctx', 'w_ada', 'b_ada', 'norm_mix', 'norm_ffn', 'w_in', 'ssd_conv_w', 'ssd_conv_b', 'ssd_dt_bias', 'ssd_a_log', 'ssd_d', 'ssd_norm', 'w_out_ssd', 'lru_conv_w', 'lru_conv_b', 'lru_w_a', 'lru_b_a', 'lru_w_x', 'lru_b_x', 'lru_lambda', 'w_out_lru', 'w_gate', 'b_gate', 'w_o', 'ffn_w13', 'ffn_w2', 'final_norm']
TWIN_WEIGHTS = ['c_ctx', 'w_ada', 'b_ada', 'norm_mix', 'norm_ffn', 'w_in', 'ssd_conv_w', 'ssd_conv_b', 'ssd_dt_bias', 'ssd_a_log', 'ssd_d', 'ssd_norm', 'w_out_ssd', 'lru_conv_w', 'lru_conv_b', 'lru_w_a', 'lru_b_a', 'lru_w_x', 'lru_b_x', 'lru_lambda', 'w_out_lru', 'w_gate', 'b_gate', 'w_o', 'ffn_w13', 'ffn_w2', 'final_norm']
TWIN_DIFF_INPUT = 'x'
TWIN_INPUTS = ['x', 'c', 'ctx', 'c_ctx', 'w_ada', 'b_ada', 'norm_mix', 'norm_ffn', 'w_in', 'ssd_conv_w', 'ssd_conv_b', 'ssd_dt_bias', 'ssd_a_log', 'ssd_d', 'ssd_norm', 'w_out_ssd', 'lru_conv_w', 'lru_conv_b', 'lru_w_a', 'lru_b_a', 'lru_w_x', 'lru_b_x', 'lru_lambda', 'w_out_lru', 'w_gate', 'b_gate', 'w_o', 'ffn_w13', 'ffn_w2', 'final_norm', 'loss_target', 'm_c_ctx', 'm_w_ada', 'm_b_ada', 'm_norm_mix', 'm_norm_ffn', 'm_w_in', 'm_ssd_conv_w', 'm_ssd_conv_b', 'm_ssd_dt_bias', 'm_ssd_a_log', 'm_ssd_d', 'm_ssd_norm', 'm_w_out_ssd', 'm_lru_conv_w', 'm_lru_conv_b', 'm_lru_w_a', 'm_lru_b_a', 'm_lru_w_x', 'm_lru_b_x', 'm_lru_lambda', 'm_w_out_lru', 'm_w_gate', 'm_b_gate', 'm_w_o', 'm_ffn_w13', 'm_ffn_w2', 'm_final_norm', 'v_c_ctx', 'v_w_ada', 'v_b_ada', 'v_norm_mix', 'v_norm_ffn', 'v_w_in', 'v_ssd_conv_w', 'v_ssd_conv_b', 'v_ssd_dt_bias', 'v_ssd_a_log', 'v_ssd_d', 'v_ssd_norm', 'v_w_out_ssd', 'v_lru_conv_w', 'v_lru_conv_b', 'v_lru_w_a', 'v_lru_b_a', 'v_lru_w_x', 'v_lru_b_x', 'v_lru_lambda', 'v_w_out_lru', 'v_w_gate', 'v_b_gate', 'v_w_o', 'v_ffn_w13', 'v_ffn_w2', 'v_final_norm']
TWIN_OUTPUTS = ['loss', 'grad_x', 'grad_c_ctx', 'grad_w_ada', 'grad_b_ada', 'grad_norm_mix', 'grad_norm_ffn', 'grad_w_in', 'grad_ssd_conv_w', 'grad_ssd_conv_b', 'grad_ssd_dt_bias', 'grad_ssd_a_log', 'grad_ssd_d', 'grad_ssd_norm', 'grad_w_out_ssd', 'grad_lru_conv_w', 'grad_lru_conv_b', 'grad_lru_w_a', 'grad_lru_b_a', 'grad_lru_w_x', 'grad_lru_b_x', 'grad_lru_lambda', 'grad_w_out_lru', 'grad_w_gate', 'grad_b_gate', 'grad_w_o', 'grad_ffn_w13', 'grad_ffn_w2', 'grad_final_norm', 'delta_c_ctx', 'delta_w_ada', 'delta_b_ada', 'delta_norm_mix', 'delta_norm_ffn', 'delta_w_in', 'delta_ssd_conv_w', 'delta_ssd_conv_b', 'delta_ssd_dt_bias', 'delta_ssd_a_log', 'delta_ssd_d', 'delta_ssd_norm', 'delta_w_out_ssd', 'delta_lru_conv_w', 'delta_lru_conv_b', 'delta_lru_w_a', 'delta_lru_b_a', 'delta_lru_w_x', 'delta_lru_b_x', 'delta_lru_lambda', 'delta_w_out_lru', 'delta_w_gate', 'delta_b_gate', 'delta_w_o', 'delta_ffn_w13', 'delta_ffn_w2', 'delta_final_norm', 'new_m_c_ctx', 'new_m_w_ada', 'new_m_b_ada', 'new_m_norm_mix', 'new_m_norm_ffn', 'new_m_w_in', 'new_m_ssd_conv_w', 'new_m_ssd_conv_b', 'new_m_ssd_dt_bias', 'new_m_ssd_a_log', 'new_m_ssd_d', 'new_m_ssd_norm', 'new_m_w_out_ssd', 'new_m_lru_conv_w', 'new_m_lru_conv_b', 'new_m_lru_w_a', 'new_m_lru_b_a', 'new_m_lru_w_x', 'new_m_lru_b_x', 'new_m_lru_lambda', 'new_m_w_out_lru', 'new_m_w_gate', 'new_m_b_gate', 'new_m_w_o', 'new_m_ffn_w13', 'new_m_ffn_w2', 'new_m_final_norm', 'new_v_c_ctx', 'new_v_w_ada', 'new_v_b_ada', 'new_v_norm_mix', 'new_v_norm_ffn', 'new_v_w_in', 'new_v_ssd_conv_w', 'new_v_ssd_conv_b', 'new_v_ssd_dt_bias', 'new_v_ssd_a_log', 'new_v_ssd_d', 'new_v_ssd_norm', 'new_v_w_out_ssd', 'new_v_lru_conv_w', 'new_v_lru_conv_b', 'new_v_lru_w_a', 'new_v_lru_b_a', 'new_v_lru_w_x', 'new_v_lru_b_x', 'new_v_lru_lambda', 'new_v_w_out_lru', 'new_v_w_gate', 'new_v_b_gate', 'new_v_w_o', 'new_v_ffn_w13', 'new_v_ffn_w2', 'new_v_final_norm']
TWIN_LEAF_KINDS = {'loss': 'loss', 'grad_x': 'grad_x', 'grad_c_ctx': 'grad_w', 'grad_w_ada': 'grad_w', 'grad_b_ada': 'grad_w', 'grad_norm_mix': 'grad_w', 'grad_norm_ffn': 'grad_w', 'grad_w_in': 'grad_w', 'grad_ssd_conv_w': 'grad_w', 'grad_ssd_conv_b': 'grad_w', 'grad_ssd_dt_bias': 'grad_w', 'grad_ssd_a_log': 'grad_w', 'grad_ssd_d': 'grad_w', 'grad_ssd_norm': 'grad_w', 'grad_w_out_ssd': 'grad_w', 'grad_lru_conv_w': 'grad_w', 'grad_lru_conv_b': 'grad_w', 'grad_lru_w_a': 'grad_w', 'grad_lru_b_a': 'grad_w', 'grad_lru_w_x': 'grad_w', 'grad_lru_b_x': 'grad_w', 'grad_lru_lambda': 'grad_w', 'grad_w_out_lru': 'grad_w', 'grad_w_gate': 'grad_w', 'grad_b_gate': 'grad_w', 'grad_w_o': 'grad_w', 'grad_ffn_w13': 'grad_w', 'grad_ffn_w2': 'grad_w', 'grad_final_norm': 'grad_w', 'delta_c_ctx': 'delta_w', 'delta_w_ada': 'delta_w', 'delta_b_ada': 'delta_w', 'delta_norm_mix': 'delta_w', 'delta_norm_ffn': 'delta_w', 'delta_w_in': 'delta_w', 'delta_ssd_conv_w': 'delta_w', 'delta_ssd_conv_b': 'delta_w', 'delta_ssd_dt_bias': 'delta_w', 'delta_ssd_a_log': 'delta_w', 'delta_ssd_d': 'delta_w', 'delta_ssd_norm': 'delta_w', 'delta_w_out_ssd': 'delta_w', 'delta_lru_conv_w': 'delta_w', 'delta_lru_conv_b': 'delta_w', 'delta_lru_w_a': 'delta_w', 'delta_lru_b_a': 'delta_w', 'delta_lru_w_x': 'delta_w', 'delta_lru_b_x': 'delta_w', 'delta_lru_lambda': 'delta_w', 'delta_w_out_lru': 'delta_w', 'delta_w_gate': 'delta_w', 'delta_b_gate': 'delta_w', 'delta_w_o': 'delta_w', 'delta_ffn_w13': 'delta_w', 'delta_ffn_w2': 'delta_w', 'delta_final_norm': 'delta_w', 'new_m_c_ctx': 'new_m', 'new_m_w_ada': 'new_m', 'new_m_b_ada': 'new_m', 'new_m_norm_mix': 'new_m', 'new_m_norm_ffn': 'new_m', 'new_m_w_in': 'new_m', 'new_m_ssd_conv_w': 'new_m', 'new_m_ssd_conv_b': 'new_m', 'new_m_ssd_dt_bias': 'new_m', 'new_m_ssd_a_log': 'new_m', 'new_m_ssd_d': 'new_m', 'new_m_ssd_norm': 'new_m', 'new_m_w_out_ssd': 'new_m', 'new_m_lru_conv_w': 'new_m', 'new_m_lru_conv_b': 'new_m', 'new_m_lru_w_a': 'new_m', 'new_m_lru_b_a': 'new_m', 'new_m_lru_w_x': 'new_m', 'new_m_lru_b_x': 'new_m', 'new_m_lru_lambda': 'new_m', 'new_m_w_out_lru': 'new_m', 'new_m_w_gate': 'new_m', 'new_m_b_gate': 'new_m', 'new_m_w_o': 'new_m', 'new_m_ffn_w13': 'new_m', 'new_m_ffn_w2': 'new_m', 'new_m_final_norm': 'new_m', 'new_v_c_ctx': 'new_v', 'new_v_w_ada': 'new_v', 'new_v_b_ada': 'new_v', 'new_v_norm_mix': 'new_v', 'new_v_norm_ffn': 'new_v', 'new_v_w_in': 'new_v', 'new_v_ssd_conv_w': 'new_v', 'new_v_ssd_conv_b': 'new_v', 'new_v_ssd_dt_bias': 'new_v', 'new_v_ssd_a_log': 'new_v', 'new_v_ssd_d': 'new_v', 'new_v_ssd_norm': 'new_v', 'new_v_w_out_ssd': 'new_v', 'new_v_lru_conv_w': 'new_v', 'new_v_lru_conv_b': 'new_v', 'new_v_lru_w_a': 'new_v', 'new_v_lru_b_a': 'new_v', 'new_v_lru_w_x': 'new_v', 'new_v_lru_b_x': 'new_v', 'new_v_lru_lambda': 'new_v', 'new_v_w_out_lru': 'new_v', 'new_v_w_gate': 'new_v', 'new_v_b_gate': 'new_v', 'new_v_w_o': 'new_v', 'new_v_ffn_w13': 'new_v', 'new_v_ffn_w2': 'new_v', 'new_v_final_norm': 'new_v'}


def _forward(args):
    return _fwd_reference(*[args[k] for k in FWD_PARAMS])


def _output_shape():
    def fwd():
        inp = _fwd_setup_inputs(0)
        return _fwd_reference(*[inp[k] for k in FWD_PARAMS])
    out = _jax.eval_shape(fwd)
    return out.shape, out.dtype

N_MICROBATCH = 1
ADAM_LR = 0.001
ADAM_B1 = 0.9
ADAM_B2 = 0.999
ADAM_EPS = 1e-08
ADAM_WD = 0.01
ADAM_STEP = 10
PER_EXAMPLE_BATCH_AXIS = {'x': 0, 'c': 0, 'ctx': 0, 'loss_target': 0}
SHARED_INPUTS = []
_WEIGHT_DTYPES = {'c_ctx': _jnp.float32, 'w_ada': _jnp.float32, 'b_ada': _jnp.float32, 'norm_mix': _jnp.float32, 'norm_ffn': _jnp.float32, 'w_in': _jnp.float32, 'ssd_conv_w': _jnp.float32, 'ssd_conv_b': _jnp.float32, 'ssd_dt_bias': _jnp.float32, 'ssd_a_log': _jnp.float32, 'ssd_d': _jnp.float32, 'ssd_norm': _jnp.float32, 'w_out_ssd': _jnp.float32, 'lru_conv_w': _jnp.float32, 'lru_conv_b': _jnp.float32, 'lru_w_a': _jnp.float32, 'lru_b_a': _jnp.float32, 'lru_w_x': _jnp.float32, 'lru_b_x': _jnp.float32, 'lru_lambda': _jnp.float32, 'w_out_lru': _jnp.float32, 'w_gate': _jnp.float32, 'b_gate': _jnp.float32, 'w_o': _jnp.float32, 'ffn_w13': _jnp.float32, 'ffn_w2': _jnp.float32, 'final_norm': _jnp.float32}
MOMENT_SCALE = {'c_ctx': 1.369670e-02, 'w_ada': 8.290800e-02, 'b_ada': 1.467694e-01, 'norm_mix': 6.606333e-02, 'norm_ffn': 3.385954e-02, 'w_in': 3.989637e-02, 'ssd_conv_w': 1.761464e-02, 'ssd_conv_b': 2.242102e-02, 'ssd_dt_bias': 3.005956e-02, 'ssd_a_log': 7.198200e-02, 'ssd_d': 8.034696e-02, 'ssd_norm': 2.017816e-02, 'w_out_ssd': 1.958175e-02, 'lru_conv_w': 6.018488e-02, 'lru_conv_b': 1.777537e-01, 'lru_w_a': 3.353825e-03, 'lru_b_a': 4.956211e-03, 'lru_w_x': 6.796482e-03, 'lru_b_x': 1.195432e-02, 'lru_lambda': 1.267696e-02, 'w_out_lru': 6.052865e-02, 'w_gate': 1.749911e-02, 'b_gate': 1.680045e-02, 'w_o': 6.150465e-02, 'ffn_w13': 1.515644e-02, 'ffn_w2': 2.471311e-02, 'final_norm': 3.209160e+01}


def _to_microbatches(a, axis):
    t = _jnp.moveaxis(a, axis, 0)
    t = t.reshape((N_MICROBATCH, t.shape[0] // N_MICROBATCH) + t.shape[1:])
    return _jnp.moveaxis(t, 1, axis + 1)


def setup_inputs(seed: int = 0) -> dict:
    inp = _fwd_setup_inputs(seed)
    key = _jax.random.fold_in(_jax.random.key(seed), 7919)
    shape, _ = _output_shape()
    out = dict(inp)
    out["loss_target"] = _jax.random.normal(_jax.random.fold_in(key, 0), shape, _jnp.float32)
    for i, name in enumerate(TWIN_WEIGHTS):
        w = inp[name].astype(_jnp.float32)
        if MOMENT_SCALE is None:
            s = _jnp.sqrt(_jnp.mean(_jnp.square(w)) + 1e-30)
        else:
            s = MOMENT_SCALE[name]
        km, kv = _jax.random.split(_jax.random.fold_in(key, i + 1))
        out[name] = w
        out["m_" + name] = s * _jax.random.normal(km, w.shape, _jnp.float32)
        out["v_" + name] = (s * s) * _jax.random.uniform(kv, w.shape, _jnp.float32, 0.5, 1.5)
    if N_MICROBATCH > 1:
        for name, axis in PER_EXAMPLE_BATCH_AXIS.items():
            out[name] = _to_microbatches(out[name], axis)
    return {'x': out['x'], 'c': out['c'], 'ctx': out['ctx'], 'c_ctx': out['c_ctx'], 'w_ada': out['w_ada'], 'b_ada': out['b_ada'], 'norm_mix': out['norm_mix'], 'norm_ffn': out['norm_ffn'], 'w_in': out['w_in'], 'ssd_conv_w': out['ssd_conv_w'], 'ssd_conv_b': out['ssd_conv_b'], 'ssd_dt_bias': out['ssd_dt_bias'], 'ssd_a_log': out['ssd_a_log'], 'ssd_d': out['ssd_d'], 'ssd_norm': out['ssd_norm'], 'w_out_ssd': out['w_out_ssd'], 'lru_conv_w': out['lru_conv_w'], 'lru_conv_b': out['lru_conv_b'], 'lru_w_a': out['lru_w_a'], 'lru_b_a': out['lru_b_a'], 'lru_w_x': out['lru_w_x'], 'lru_b_x': out['lru_b_x'], 'lru_lambda': out['lru_lambda'], 'w_out_lru': out['w_out_lru'], 'w_gate': out['w_gate'], 'b_gate': out['b_gate'], 'w_o': out['w_o'], 'ffn_w13': out['ffn_w13'], 'ffn_w2': out['ffn_w2'], 'final_norm': out['final_norm'], 'loss_target': out['loss_target'], 'm_c_ctx': out['m_c_ctx'], 'm_w_ada': out['m_w_ada'], 'm_b_ada': out['m_b_ada'], 'm_norm_mix': out['m_norm_mix'], 'm_norm_ffn': out['m_norm_ffn'], 'm_w_in': out['m_w_in'], 'm_ssd_conv_w': out['m_ssd_conv_w'], 'm_ssd_conv_b': out['m_ssd_conv_b'], 'm_ssd_dt_bias': out['m_ssd_dt_bias'], 'm_ssd_a_log': out['m_ssd_a_log'], 'm_ssd_d': out['m_ssd_d'], 'm_ssd_norm': out['m_ssd_norm'], 'm_w_out_ssd': out['m_w_out_ssd'], 'm_lru_conv_w': out['m_lru_conv_w'], 'm_lru_conv_b': out['m_lru_conv_b'], 'm_lru_w_a': out['m_lru_w_a'], 'm_lru_b_a': out['m_lru_b_a'], 'm_lru_w_x': out['m_lru_w_x'], 'm_lru_b_x': out['m_lru_b_x'], 'm_lru_lambda': out['m_lru_lambda'], 'm_w_out_lru': out['m_w_out_lru'], 'm_w_gate': out['m_w_gate'], 'm_b_gate': out['m_b_gate'], 'm_w_o': out['m_w_o'], 'm_ffn_w13': out['m_ffn_w13'], 'm_ffn_w2': out['m_ffn_w2'], 'm_final_norm': out['m_final_norm'], 'v_c_ctx': out['v_c_ctx'], 'v_w_ada': out['v_w_ada'], 'v_b_ada': out['v_b_ada'], 'v_norm_mix': out['v_norm_mix'], 'v_norm_ffn': out['v_norm_ffn'], 'v_w_in': out['v_w_in'], 'v_ssd_conv_w': out['v_ssd_conv_w'], 'v_ssd_conv_b': out['v_ssd_conv_b'], 'v_ssd_dt_bias': out['v_ssd_dt_bias'], 'v_ssd_a_log': out['v_ssd_a_log'], 'v_ssd_d': out['v_ssd_d'], 'v_ssd_norm': out['v_ssd_norm'], 'v_w_out_ssd': out['v_w_out_ssd'], 'v_lru_conv_w': out['v_lru_conv_w'], 'v_lru_conv_b': out['v_lru_conv_b'], 'v_lru_w_a': out['v_lru_w_a'], 'v_lru_b_a': out['v_lru_b_a'], 'v_lru_w_x': out['v_lru_w_x'], 'v_lru_b_x': out['v_lru_b_x'], 'v_lru_lambda': out['v_lru_lambda'], 'v_w_out_lru': out['v_w_out_lru'], 'v_w_gate': out['v_w_gate'], 'v_b_gate': out['v_b_gate'], 'v_w_o': out['v_w_o'], 'v_ffn_w13': out['v_ffn_w13'], 'v_ffn_w2': out['v_ffn_w2'], 'v_final_norm': out['v_final_norm']}


def _loss(weights, diff, rest, loss_target):
    with _jax.named_scope("forward"):
        args = {**rest, TWIN_DIFF_INPUT: diff, **{k: w.astype(_WEIGHT_DTYPES[k]) for k, w in weights.items()}}
        y = _forward(args)
    with _jax.named_scope("loss_head"):
        err = _jnp.square(y.astype(_jnp.float32) - loss_target)
        return 0.5 * _jnp.sum(_jnp.mean(err, axis=-1)) if err.ndim else 0.5 * err


def _adamw(w, g, m, v):
    m = ADAM_B1 * m + (1.0 - ADAM_B1) * g
    v = ADAM_B2 * v + (1.0 - ADAM_B2) * _jnp.square(g)
    m_hat = m / (1.0 - ADAM_B1 ** ADAM_STEP)
    v_hat = v / (1.0 - ADAM_B2 ** ADAM_STEP)
    delta = -ADAM_LR * (m_hat / (_jnp.sqrt(v_hat) + ADAM_EPS) + ADAM_WD * w)
    return delta, m, v


def reference(x, c, ctx, c_ctx, w_ada, b_ada, norm_mix, norm_ffn, w_in, ssd_conv_w, ssd_conv_b, ssd_dt_bias, ssd_a_log, ssd_d, ssd_norm, w_out_ssd, lru_conv_w, lru_conv_b, lru_w_a, lru_b_a, lru_w_x, lru_b_x, lru_lambda, w_out_lru, w_gate, b_gate, w_o, ffn_w13, ffn_w2, final_norm, loss_target, m_c_ctx, m_w_ada, m_b_ada, m_norm_mix, m_norm_ffn, m_w_in, m_ssd_conv_w, m_ssd_conv_b, m_ssd_dt_bias, m_ssd_a_log, m_ssd_d, m_ssd_norm, m_w_out_ssd, m_lru_conv_w, m_lru_conv_b, m_lru_w_a, m_lru_b_a, m_lru_w_x, m_lru_b_x, m_lru_lambda, m_w_out_lru, m_w_gate, m_b_gate, m_w_o, m_ffn_w13, m_ffn_w2, m_final_norm, v_c_ctx, v_w_ada, v_b_ada, v_norm_mix, v_norm_ffn, v_w_in, v_ssd_conv_w, v_ssd_conv_b, v_ssd_dt_bias, v_ssd_a_log, v_ssd_d, v_ssd_norm, v_w_out_ssd, v_lru_conv_w, v_lru_conv_b, v_lru_w_a, v_lru_b_a, v_lru_w_x, v_lru_b_x, v_lru_lambda, v_w_out_lru, v_w_gate, v_b_gate, v_w_o, v_ffn_w13, v_ffn_w2, v_final_norm):
    given = dict(x=x, c=c, ctx=ctx, c_ctx=c_ctx, w_ada=w_ada, b_ada=b_ada, norm_mix=norm_mix, norm_ffn=norm_ffn, w_in=w_in, ssd_conv_w=ssd_conv_w, ssd_conv_b=ssd_conv_b, ssd_dt_bias=ssd_dt_bias, ssd_a_log=ssd_a_log, ssd_d=ssd_d, ssd_norm=ssd_norm, w_out_ssd=w_out_ssd, lru_conv_w=lru_conv_w, lru_conv_b=lru_conv_b, lru_w_a=lru_w_a, lru_b_a=lru_b_a, lru_w_x=lru_w_x, lru_b_x=lru_b_x, lru_lambda=lru_lambda, w_out_lru=w_out_lru, w_gate=w_gate, b_gate=b_gate, w_o=w_o, ffn_w13=ffn_w13, ffn_w2=ffn_w2, final_norm=final_norm, loss_target=loss_target, m_c_ctx=m_c_ctx, m_w_ada=m_w_ada, m_b_ada=m_b_ada, m_norm_mix=m_norm_mix, m_norm_ffn=m_norm_ffn, m_w_in=m_w_in, m_ssd_conv_w=m_ssd_conv_w, m_ssd_conv_b=m_ssd_conv_b, m_ssd_dt_bias=m_ssd_dt_bias, m_ssd_a_log=m_ssd_a_log, m_ssd_d=m_ssd_d, m_ssd_norm=m_ssd_norm, m_w_out_ssd=m_w_out_ssd, m_lru_conv_w=m_lru_conv_w, m_lru_conv_b=m_lru_conv_b, m_lru_w_a=m_lru_w_a, m_lru_b_a=m_lru_b_a, m_lru_w_x=m_lru_w_x, m_lru_b_x=m_lru_b_x, m_lru_lambda=m_lru_lambda, m_w_out_lru=m_w_out_lru, m_w_gate=m_w_gate, m_b_gate=m_b_gate, m_w_o=m_w_o, m_ffn_w13=m_ffn_w13, m_ffn_w2=m_ffn_w2, m_final_norm=m_final_norm, v_c_ctx=v_c_ctx, v_w_ada=v_w_ada, v_b_ada=v_b_ada, v_norm_mix=v_norm_mix, v_norm_ffn=v_norm_ffn, v_w_in=v_w_in, v_ssd_conv_w=v_ssd_conv_w, v_ssd_conv_b=v_ssd_conv_b, v_ssd_dt_bias=v_ssd_dt_bias, v_ssd_a_log=v_ssd_a_log, v_ssd_d=v_ssd_d, v_ssd_norm=v_ssd_norm, v_w_out_ssd=v_w_out_ssd, v_lru_conv_w=v_lru_conv_w, v_lru_conv_b=v_lru_conv_b, v_lru_w_a=v_lru_w_a, v_lru_b_a=v_lru_b_a, v_lru_w_x=v_lru_w_x, v_lru_b_x=v_lru_b_x, v_lru_lambda=v_lru_lambda, v_w_out_lru=v_w_out_lru, v_w_gate=v_w_gate, v_b_gate=v_b_gate, v_w_o=v_w_o, v_ffn_w13=v_ffn_w13, v_ffn_w2=v_ffn_w2, v_final_norm=v_final_norm)
    weights = {n: given[n] for n in TWIN_WEIGHTS}
    shared = {n: given[n] for n in SHARED_INPUTS}
    per_example = {n: given[n] for n in ['x', 'c', 'ctx']}
    grad_fn = _jax.value_and_grad(_loss, argnums=(0, 1))

    def one_microbatch(ex, loss_target):
        ex = dict(ex)
        diff = ex.pop(TWIN_DIFF_INPUT)
        return grad_fn(weights, diff, {**shared, **ex}, loss_target)

    if N_MICROBATCH == 1:
        loss, (grad_w, grad_x) = one_microbatch(per_example, given["loss_target"])
    else:
        def body(carry, xs):
            loss_sum, grad_sum = carry
            l_k, (gw_k, gx_k) = one_microbatch(xs[0], xs[1])
            with _jax.named_scope("update"):
                return (loss_sum + l_k, _jax.tree.map(_jnp.add, grad_sum, gw_k)), gx_k

        init = (_jnp.zeros((), _jnp.float32), _jax.tree.map(_jnp.zeros_like, weights))
        (loss, grad_w), grad_x = _jax.lax.scan(body, init, (per_example, given["loss_target"]))
    with _jax.named_scope("update"):
        delta_w, new_m, new_v = {}, {}, {}
        for n in TWIN_WEIGHTS:
            delta_w[n], new_m[n], new_v[n] = _adamw(weights[n], grad_w[n], given["m_" + n], given["v_" + n])
    return (loss, grad_x, *[grad_w[n] for n in TWIN_WEIGHTS], *[delta_w[n] for n in TWIN_WEIGHTS],
            *[new_m[n] for n in TWIN_WEIGHTS], *[new_v[n] for n in TWIN_WEIGHTS])
```

```python
import functools

import numpy as np
import jax
import jax.numpy as jnp
from jax import lax
from jax.experimental import pallas as pl
from jax.experimental.pallas import tpu as pltpu

F32, BF16 = jnp.float32, jnp.bfloat16
MESH = pl.DeviceIdType.MESH
N_DEV = 8
VMEM_LIMIT_BYTES = 48 * 1024 * 1024
ROW_BUDGET_BYTES = 12 * 1024 * 1024
LANES, SUBLANES = 128, 8

EPS = 1e-6
GRID_W = 64
SSD_HEAD_DIM = 64
SSD_GROUPS = 4
SSD_STATE = 128
SSD_CHUNK = 128
CONV_W = 4
CONV_LEFT = 2
LRU_HEAD_DIM = 128
LRU_C = 8.0
LRU_BLOCK = 128
N_MOD = 6
ADAM_LR, ADAM_B1, ADAM_B2, ADAM_EPS, ADAM_WD, ADAM_STEP = 0.001, 0.9, 0.999, 1e-08, 0.01, 10
NEG_BIG = -1e30


def _params(sem):
    return pltpu.CompilerParams(dimension_semantics=sem, vmem_limit_bytes=VMEM_LIMIT_BYTES)


def _tile(dim, pref, align):
    if dim <= pref:
        return dim
    t = (pref // align) * align
    while t >= align:
        if dim % t == 0:
            return t
        t -= align
    return dim


def _bdot(a, b):
    return jnp.dot(a.astype(BF16), b.astype(BF16), preferred_element_type=F32)


def _bdot_nt(a, b):
    return lax.dot_general(a.astype(BF16), b.astype(BF16), (((1,), (1,)), ((), ())), preferred_element_type=F32)


def _bdot_tn(a, b):
    return lax.dot_general(a.astype(BF16), b.astype(BF16), (((0,), (0,)), ((), ())), preferred_element_type=F32)


def _hdot(a, b):
    return jnp.dot(a, b, preferred_element_type=F32, precision=lax.Precision.HIGHEST)


def _softplus(v):
    return jnp.maximum(v, 0.0) + jnp.log(1.0 + jnp.exp(-jnp.abs(v)))


def _sigmoid(v):
    return 1.0 / (1.0 + jnp.exp(-v))


def _silu(v):
    return v * _sigmoid(v)


def _dsilu(v):
    s = _sigmoid(v)
    return s * (1.0 + v * (1.0 - s))


GELU_C0 = 0.7978845608028654
GELU_C1 = 0.044715


def _gelu(v):
    return 0.5 * v * (1.0 + jnp.tanh(GELU_C0 * (v + GELU_C1 * v * v * v)))


def _dgelu(v):
    t = jnp.tanh(GELU_C0 * (v + GELU_C1 * v * v * v))
    return 0.5 * (1.0 + t) + 0.5 * v * (1.0 - t * t) * GELU_C0 * (1.0 + 3.0 * GELU_C1 * v * v)


def _mm(pairs, mode, out_dtype, name, tm=1024, tn=1024, tk=512):
    a0, b0 = pairs[0]
    if mode == "nn":
        m, n, ks = a0.shape[0], b0.shape[1], [a.shape[1] for a, _ in pairs]
    elif mode == "nt":
        m, n, ks = a0.shape[0], b0.shape[0], [a.shape[1] for a, _ in pairs]
    else:
        m, n, ks = a0.shape[1], b0.shape[1], [a.shape[0] for a, _ in pairs]
    tm = _tile(m, tm, LANES if mode == "tn" else 16)
    tn = _tile(n, tn, LANES)
    tks = [_tile(k, tk, 16 if mode == "tn" else LANES) for k in ks]
    steps = [k // t for k, t in zip(ks, tks)]
    offs = [sum(steps[:i]) for i in range(len(pairs))]
    total = sum(steps)
    npairs = len(pairs)

    def kk(i):
        return lambda k: jnp.clip(k - offs[i], 0, steps[i] - 1)

    in_specs, operands = [], []
    for i, (a, b) in enumerate(pairs):
        kf = kk(i)
        if mode == "nn":
            in_specs.append(pl.BlockSpec((tm, tks[i]), lambda mi, ni, k, kf=kf: (mi, kf(k))))
            in_specs.append(pl.BlockSpec((tks[i], tn), lambda mi, ni, k, kf=kf: (kf(k), ni)))
        elif mode == "nt":
            in_specs.append(pl.BlockSpec((tm, tks[i]), lambda mi, ni, k, kf=kf: (mi, kf(k))))
            in_specs.append(pl.BlockSpec((tn, tks[i]), lambda mi, ni, k, kf=kf: (ni, kf(k))))
        else:
            in_specs.append(pl.BlockSpec((tks[i], tm), lambda mi, ni, k, kf=kf: (kf(k), mi)))
            in_specs.append(pl.BlockSpec((tks[i], tn), lambda mi, ni, k, kf=kf: (kf(k), ni)))
        operands += [a, b]
    dot = {"nn": _bdot, "nt": _bdot_nt, "tn": _bdot_tn}[mode]

    def body(*refs):
        out_ref, acc_ref = refs[2 * npairs], refs[2 * npairs + 1]
        k = pl.program_id(2)

        @pl.when(k == 0)
        def _():
            acc_ref[...] = jnp.zeros_like(acc_ref)

        for i in range(npairs):
            @pl.when((k >= offs[i]) & (k < offs[i] + steps[i]))
            def _(i=i):
                acc_ref[...] += dot(refs[2 * i][...], refs[2 * i + 1][...])

        @pl.when(k == total - 1)
        def _():
            out_ref[...] = acc_ref[...].astype(out_dtype)

    return pl.pallas_call(
        body, name=name, grid=(m // tm, n // tn, total), in_specs=in_specs,
        out_specs=pl.BlockSpec((tm, tn), lambda mi, ni, k: (mi, ni)),
        out_shape=jax.ShapeDtypeStruct((m, n), out_dtype),
        scratch_shapes=[pltpu.VMEM((tm, tn), F32)],
        compiler_params=_params(("parallel", "parallel", "arbitrary")),
    )(*operands)


def _rowwise(fn, rows, small, outs, accs, name):
    t = rows[0].shape[0]
    per_row = sum(r.shape[1] * r.dtype.itemsize for r in rows) + sum(c * jnp.dtype(d).itemsize for c, d in outs)
    tm = SUBLANES
    while tm * 2 <= 1024 and t % (tm * 2) == 0 and tm * 2 * per_row <= ROW_BUDGET_BYTES:
        tm *= 2
    if t % tm:
        tm = t
    nrow, nsmall, nout, nacc = len(rows), len(small), len(outs), len(accs)

    def body(*refs):
        rv = [r[...] for r in refs[:nrow]]
        sv = [r[...] for r in refs[nrow:nrow + nsmall]]
        o_refs = refs[nrow + nsmall:nrow + nsmall + nout]
        a_refs = refs[nrow + nsmall + nout:]
        ov, av = fn(rv, sv)
        for r, v in zip(o_refs, ov):
            r[...] = v.astype(r.dtype)
        if nacc:
            @pl.when(pl.program_id(0) == 0)
            def _():
                for r in a_refs:
                    r[...] = jnp.zeros_like(r)
            for r, v in zip(a_refs, av):
                r[...] += v

    in_specs = [pl.BlockSpec((tm, r.shape[1]), lambda i: (i, 0)) for r in rows]
    in_specs += [pl.BlockSpec(s.shape, lambda i: (0, 0)) for s in small]
    out_specs = [pl.BlockSpec((tm, c), lambda i: (i, 0)) for c, _ in outs]
    out_specs += [pl.BlockSpec((1, c), lambda i: (0, 0)) for c in accs]
    out_shape = [jax.ShapeDtypeStruct((t, c), d) for c, d in outs]
    out_shape += [jax.ShapeDtypeStruct((1, c), F32) for c in accs]
    res = pl.pallas_call(
        body, name=name, grid=(t // tm,), in_specs=in_specs, out_specs=out_specs, out_shape=out_shape,
        compiler_params=_params(("arbitrary",)),
    )(*rows, *small)
    return res[:nout], res[nout:]


def _colsum(v):
    return jnp.sum(v, axis=0, keepdims=True)


def _rowmean(v):
    return jnp.mean(v, axis=1, keepdims=True)


def _conv_tiles(x):
    t, c = x.shape
    tm = _tile(t, 256, SUBLANES)
    nt = t // tm
    r8 = tm // SUBLANES
    specs = [
        pl.BlockSpec((SUBLANES, c), lambda i: (jnp.maximum(i * r8 - 1, 0), 0)),
        pl.BlockSpec((tm, c), lambda i: (i, 0)),
        pl.BlockSpec((SUBLANES, c), lambda i: (jnp.minimum((i + 1) * r8, t // SUBLANES - 1), 0)),
    ]
    return tm, nt, specs


def _conv_taps(prev_ref, cur_ref, next_ref, tm, nt, left):
    i = pl.program_id(0)
    p = jnp.where(i > 0, prev_ref[...], 0.0)
    q = jnp.where(i < nt - 1, next_ref[...], 0.0)
    ext = jnp.concatenate([p, cur_ref[...], q], axis=0)
    n = tm + 2 * SUBLANES
    taps = []
    for k in range(CONV_W):
        s = (left - k) % n
        r = pltpu.roll(ext, s, axis=0) if s else ext
        taps.append(r[SUBLANES:SUBLANES + tm])
    return taps


def _conv(x, w, b, left, act, name):
    t, c = x.shape
    tm, nt, specs = _conv_tiles(x)

    def body(prev_ref, cur_ref, next_ref, w_ref, b_ref, o_ref):
        taps = _conv_taps(prev_ref, cur_ref, next_ref, tm, nt, left)
        pre = b_ref[...] + sum(w_ref[k:k + 1, :] * taps[k] for k in range(CONV_W))
        o_ref[...] = _silu(pre) if act else pre

    return pl.pallas_call(
        body, name=name, grid=(nt,),
        in_specs=specs + [pl.BlockSpec((CONV_W, c), lambda i: (0, 0)), pl.BlockSpec((1, c), lambda i: (0, 0))],
        out_specs=pl.BlockSpec((tm, c), lambda i: (i, 0)),
        out_shape=jax.ShapeDtypeStruct((t, c), F32), compiler_params=_params(("arbitrary",)),
    )(x, x, x, w, b)


def _conv_bwd(x, dy, w, b, act, name):
    t, c = x.shape
    tm, nt, specs = _conv_tiles(x)

    def body(prev_ref, cur_ref, next_ref, dy_ref, w_ref, b_ref, dpre_ref, dw0, dw1, dw2, dw3, db_ref):
        taps = _conv_taps(prev_ref, cur_ref, next_ref, tm, nt, CONV_LEFT)
        if act:
            pre = b_ref[...] + sum(w_ref[k:k + 1, :] * taps[k] for k in range(CONV_W))
            dpre = dy_ref[...] * _dsilu(pre)
        else:
            dpre = dy_ref[...]
        dpre_ref[...] = dpre

        @pl.when(pl.program_id(0) == 0)
        def _():
            for r in (dw0, dw1, dw2, dw3, db_ref):
                r[...] = jnp.zeros_like(r)

        for k, r in enumerate((dw0, dw1, dw2, dw3)):
            r[...] += _colsum(dpre * taps[k])
        db_ref[...] += _colsum(dpre)

    row = pl.BlockSpec((1, c), lambda i: (0, 0))
    res = pl.pallas_call(
        body, name=name + "_pre", grid=(nt,),
        in_specs=specs + [pl.BlockSpec((tm, c), lambda i: (i, 0)), pl.BlockSpec((CONV_W, c), lambda i: (0, 0)), row],
        out_specs=[pl.BlockSpec((tm, c), lambda i: (i, 0))] + [row] * 5,
        out_shape=[jax.ShapeDtypeStruct((t, c), F32)] + [jax.ShapeDtypeStruct((1, c), F32)] * 5,
        compiler_params=_params(("arbitrary",)),
    )(x, x, x, dy, w, b)
    dpre, dw, db = res[0], jnp.concatenate(res[1:5], axis=0), res[5]
    dx = _conv(dpre, w[::-1], jnp.zeros_like(b), CONV_W - 1 - CONV_LEFT, False, name + "_dx")
    return dx, dw, db


def _ssd_dims(xs, bm):
    t, ws = xs.shape
    heads = ws // SSD_HEAD_DIM
    hpg = heads // SSD_GROUPS
    assert hpg % 2 == 0 and bm.shape[1] == SSD_GROUPS * SSD_STATE and t % SSD_CHUNK == 0
    return t, ws, heads, hpg, heads // 2, t // SSD_CHUNK


def _ssd_chunk_terms(dtr_ref, dtrt_ref, bias_ref, biast_ref, alog_ref, alogt_ref, reverse):
    q = SSD_CHUNK
    a_neg = -jnp.exp(alog_ref[...])
    dt = _softplus(dtr_ref[...] + bias_ref[...])
    dtt = _softplus(dtrt_ref[...] + biast_ref[...])
    a = dt * a_neg
    at = dtt * (-jnp.exp(alogt_ref[...]))
    ii = lax.broadcasted_iota(jnp.int32, (q, q), 0)
    jj = lax.broadcasted_iota(jnp.int32, (q, q), 1)
    mask = (ii <= jj) if reverse else (ii >= jj)
    tri = mask.astype(F32)
    trit = ((jj <= ii) if reverse else (jj >= ii)).astype(F32)
    before = (jj > ii) if reverse else (jj < ii)
    ac = _hdot(tri, a)
    act = _hdot(at, trit)
    tot = _colsum(a)
    return a_neg, dt, a, ac, act, tot, mask, trit, before


def _pair_terms(lo, h1, dt, ac, tot):
    q = SSD_CHUNK

    def colb(v, h):
        return jnp.broadcast_to(v[:, h:h + 1], (v.shape[0], LANES))

    def pairb(v):
        return jnp.where(lo[:v.shape[0]], colb(v, h1), colb(v, h1 + 1))

    dt_p = pairb(dt)
    ac_p = pairb(ac)
    eac_p = jnp.exp(ac_p)
    dte_p = jnp.exp(pairb(jnp.broadcast_to(tot, (q, tot.shape[1]))) - ac_p)
    etot_p = jnp.exp(pairb(tot))
    return dt_p, eac_p, dte_p, etot_p


def _decay_mats(mask, ac, act, cb, h1):
    q = SSD_CHUNK
    out = []
    for h in (h1, h1 + 1):
        diff = jnp.broadcast_to(ac[:, h:h + 1], (q, q)) - jnp.broadcast_to(act[h:h + 1, :], (q, q))
        lmat = jnp.exp(jnp.where(mask, diff, NEG_BIG))
        out.append((lmat, cb * lmat))
    return out


def _ssd_specs(t, ws, heads, npairs, nc, reverse):
    q, gn = SSD_CHUNK, SSD_GROUPS * SSD_STATE
    ci = (lambda k: nc - 1 - k) if reverse else (lambda k: k)
    small = lambda shape: pl.BlockSpec(shape, lambda k: (0,) * len(shape))
    seq = [
        pl.BlockSpec((q, ws), lambda k: (ci(k), 0)),
        pl.BlockSpec((q, gn), lambda k: (ci(k), 0)),
        pl.BlockSpec((q, gn), lambda k: (ci(k), 0)),
        pl.BlockSpec((q, heads), lambda k: (ci(k), 0)),
        pl.BlockSpec((heads, q), lambda k: (0, ci(k))),
        small((1, heads)), small((heads, 1)), small((1, heads)), small((heads, 1)),
    ]
    return ci, small, seq


def _ssd_fwd(xs, bm, cm, dtr, bias, alog, s0, reverse, name):
    t, ws, heads, hpg, npairs, nc = _ssd_dims(xs, bm)
    q, ns = SSD_CHUNK, SSD_STATE
    ci, small, seq = _ssd_specs(t, ws, heads, npairs, nc, reverse)

    def body(xs_ref, b_ref, c_ref, dtr_ref, dtrt_ref, bias_ref, biast_ref, alog_ref, alogt_ref, s0_ref,
             y_ref, se_ref, s_ref):
        @pl.when(pl.program_id(0) == 0)
        def _():
            s_ref[...] = s0_ref[...]

        _, dt, _, ac, act, tot, mask, _, _ = _ssd_chunk_terms(dtr_ref, dtrt_ref, bias_ref, biast_ref, alog_ref, alogt_ref, reverse)
        lo = lax.broadcasted_iota(jnp.int32, (q, LANES), 1) < SSD_HEAD_DIM
        for g in range(SSD_GROUPS):
            bg = b_ref[:, g * ns:(g + 1) * ns]
            cg = c_ref[:, g * ns:(g + 1) * ns]
            cb = _bdot_nt(cg, bg)
            bgt = bg.T
            for pq in range(hpg // 2):
                pp = g * (hpg // 2) + pq
                h1 = 2 * pp
                cs = slice(pp * LANES, (pp + 1) * LANES)
                (_, m1), (_, m2) = _decay_mats(mask, ac, act, cb, h1)
                dt_p, eac_p, dte_p, etot_p = _pair_terms(lo, h1, dt, ac, tot)
                xp = xs_ref[:, cs] * dt_p
                sp = s_ref[pp]
                se_ref[0, pp] = sp
                y_ref[:, cs] = jnp.where(lo, _bdot(m1, xp), _bdot(m2, xp)) + eac_p * _bdot(cg, sp)
                s_ref[pp] = etot_p * sp + _bdot(bgt, xp * dte_p)

    st = (npairs, ns, LANES)
    return pl.pallas_call(
        body, name=name, grid=(nc,), in_specs=seq + [small(st)],
        out_specs=[pl.BlockSpec((q, ws), lambda k: (ci(k), 0)),
                   pl.BlockSpec((1,) + st, lambda k: (ci(k), 0, 0, 0)), small(st)],
        out_shape=[jax.ShapeDtypeStruct((t, ws), F32), jax.ShapeDtypeStruct((nc,) + st, F32), jax.ShapeDtypeStruct(st, F32)],
        compiler_params=_params(("arbitrary",)),
    )(xs, bm, cm, dtr, dtr.T, bias, bias.T, alog, alog.T, s0)


def _ssd_bwd(xs, bm, cm, dtr, bias, alog, dy, s_enter, lam0, reverse, name):
    t, ws, heads, hpg, npairs, nc = _ssd_dims(xs, bm)
    q, ns, gn = SSD_CHUNK, SSD_STATE, SSD_GROUPS * SSD_STATE
    ci, small, seq = _ssd_specs(t, ws, heads, npairs, nc, not reverse)
    st = (npairs, ns, LANES)

    def body(xs_ref, b_ref, c_ref, dtr_ref, dtrt_ref, bias_ref, biast_ref, alog_ref, alogt_ref, dy_ref, se_ref, lam0_ref,
             dxs_ref, db_ref, dc_ref, ddtr_ref, da_ref, dbias_ref, lam_ref):
        @pl.when(pl.program_id(0) == 0)
        def _():
            lam_ref[...] = lam0_ref[...]
            da_ref[...] = jnp.zeros_like(da_ref)
            dbias_ref[...] = jnp.zeros_like(dbias_ref)

        a_neg, dt, _, ac, act, tot, mask, trit, before = _ssd_chunk_terms(dtr_ref, dtrt_ref, bias_ref, biast_ref, alog_ref, alogt_ref, reverse)
        lo = lax.broadcasted_iota(jnp.int32, (q, LANES), 1) < SSD_HEAD_DIM
        hl = lax.broadcasted_iota(jnp.int32, (1, heads), 1)

        def per_head(v, h1):
            s1 = jnp.sum(jnp.where(lo[:v.shape[0]], v, 0.0), axis=1, keepdims=True)
            s2 = jnp.sum(jnp.where(lo[:v.shape[0]], 0.0, v), axis=1, keepdims=True)
            return s1 * (hl == h1).astype(F32) + s2 * (hl == h1 + 1).astype(F32)

        da_tt = jnp.zeros((q, heads), F32)
        da_st = jnp.zeros((q, heads), F32)
        da_ts = jnp.zeros((q, heads), F32)
        da_ss = jnp.zeros((1, heads), F32)
        dxx = jnp.zeros((q, heads), F32)
        for g in range(SSD_GROUPS):
            gs = slice(g * ns, (g + 1) * ns)
            bg, cg = b_ref[:, gs], c_ref[:, gs]
            cb = _bdot_nt(cg, bg)
            cgt = cg.T
            dcb = jnp.zeros((q, q), F32)
            dbg = jnp.zeros((q, ns), F32)
            dcg = jnp.zeros((q, ns), F32)
            for pq in range(hpg // 2):
                pp = g * (hpg // 2) + pq
                h1 = 2 * pp
                cs = slice(pp * LANES, (pp + 1) * LANES)
                (l1, m1), (l2, m2) = _decay_mats(mask, ac, act, cb, h1)
                dt_p, eac_p, dte_p, etot_p = _pair_terms(lo, h1, dt, ac, tot)
                xsp = xs_ref[:, cs]
                xp = xsp * dt_p
                dyp = dy_ref[:, cs]
                lam = lam_ref[pp]
                sp = se_ref[0, pp]
                dx_state = dte_p * _bdot(bg, lam)
                dxdt = jnp.where(lo, _bdot(m1.T, dyp), _bdot(m2.T, dyp)) + dx_state
                dyx1 = _bdot_nt(jnp.where(lo, dyp, 0.0), xp)
                dyx2 = _bdot_nt(jnp.where(lo, 0.0, dyp), xp)
                dcb = dcb + l1 * dyx1 + l2 * dyx2
                for h, pair in ((h1, m1 * dyx1), (h1 + 1, m2 * dyx2)):
                    cross = jnp.sum(jnp.where(before, _bdot(trit, pair), 0.0), axis=1, keepdims=True)
                    da_tt = da_tt + cross * (hl == h).astype(F32)
                da_st = da_st + per_head(dyp * (eac_p * _bdot(cg, sp)), h1)
                da_ts = da_ts + per_head(dx_state * xp, h1)
                da_ss = da_ss + per_head(_colsum(lam * sp) * etot_p, h1)
                dxx = dxx + per_head(dxdt * xsp, h1)
                edy = eac_p * dyp
                xe = dte_p * xp
                dcg = dcg + _bdot_nt(edy, sp)
                dbg = dbg + _bdot_nt(xe, lam)
                lam_ref[pp] = etot_p * lam + _bdot(cgt, edy)
                dxs_ref[:, cs] = dxdt * dt_p
            db_ref[:, gs] = dbg + _bdot(dcb.T, cg)
            dc_ref[:, gs] = dcg + _bdot(dcb, bg)
        da = da_tt + _hdot(trit, da_st) + _hdot(before.astype(F32), da_ts) + da_ss
        ddtr = (da * a_neg + dxx) * _sigmoid(dtr_ref[...] + bias_ref[...])
        ddtr_ref[...] = ddtr
        da_ref[...] += _colsum(da * dt)
        dbias_ref[...] += _colsum(ddtr)

    row = lambda c: pl.BlockSpec((q, c), lambda k: (ci(k), 0))
    return pl.pallas_call(
        body, name=name, grid=(nc,),
        in_specs=seq + [row(ws), pl.BlockSpec((1,) + st, lambda k: (ci(k), 0, 0, 0)), small(st)],
        out_specs=[row(ws), row(gn), row(gn), row(heads), small((1, heads)), small((1, heads)), small(st)],
        out_shape=[jax.ShapeDtypeStruct((t, ws), F32), jax.ShapeDtypeStruct((t, gn), F32), jax.ShapeDtypeStruct((t, gn), F32),
                   jax.ShapeDtypeStruct((t, heads), F32), jax.ShapeDtypeStruct((1, heads), F32),
                   jax.ShapeDtypeStruct((1, heads), F32), jax.ShapeDtypeStruct(st, F32)],
        compiler_params=_params(("arbitrary",)),
    )(xs, bm, cm, dtr, dtr.T, bias, bias.T, alog, alog.T, dy, s_enter, lam0)


def _tile_scan(a_ref, u_ref, out_ref, carry_ref, ntiles, reverse):
    w = a_ref.shape[1]
    row = lax.broadcasted_iota(jnp.int32, (SUBLANES, w), 0)

    def step(j, _):
        r0 = pl.multiple_of((ntiles - 1 - j if reverse else j) * SUBLANES, SUBLANES)
        a = a_ref[pl.ds(r0, SUBLANES), :]
        u = u_ref[pl.ds(r0, SUBLANES), :]
        for d in (1, 2, 4):
            keep = (row < SUBLANES - d) if reverse else (row >= d)
            s = SUBLANES - d if reverse else d
            a_sh = jnp.where(keep, pltpu.roll(a, s, axis=0), 1.0)
            u_sh = jnp.where(keep, pltpu.roll(u, s, axis=0), 0.0)
            u = a * u_sh + u
            a = a * a_sh
        h = a * carry_ref[...] + u
        out_ref[pl.ds(r0, SUBLANES), :] = h
        last = h[0:1, :] if reverse else h[SUBLANES - 1:SUBLANES, :]
        carry_ref[...] = jnp.broadcast_to(last, (SUBLANES, w))
        return 0

    lax.fori_loop(0, ntiles, step, 0)


def _lru_gates(xh, wa, wx, ba, bx, lam):
    r = _sigmoid(_bdot(xh, wa) + ba)
    i = _sigmoid(_bdot(xh, wx) + bx)
    sp = _softplus(-lam)
    a = jnp.exp(-LRU_C * r * sp)
    s = jnp.sqrt(1.0 - a * a)
    return r, i, sp, a, s


def _lru_fwd(xc, wa, wx, ba, bx, lam, h0, addend, reverse, name):
    t, w = xc.shape
    tb = LRU_BLOCK
    nb, nh = t // tb, w // LRU_HEAD_DIM
    bi = (lambda k: nb - 1 - k) if reverse else (lambda k: k)
    has_add = addend is not None

    def body(*refs):
        xc_ref, wa_ref, wx_ref, ba_ref, bx_ref, lam_ref, h0_ref = refs[:7]
        add_ref = refs[7] if has_add else None
        h_ref, hl_ref = refs[7 + has_add], refs[8 + has_add]
        sum_ref = refs[9 + has_add] if has_add else None
        a_s, u_s, carry = refs[-3:]

        @pl.when(pl.program_id(0) == 0)
        def _():
            carry[...] = jnp.broadcast_to(h0_ref[...], carry.shape)

        for hd in range(nh):
            hs = slice(hd * LRU_HEAD_DIM, (hd + 1) * LRU_HEAD_DIM)
            xh = xc_ref[:, hs]
            _, i, _, a, s = _lru_gates(xh, wa_ref[hd], wx_ref[hd], ba_ref[:, hs], bx_ref[:, hs], lam_ref[:, hs])
            a_s[:, hs] = a
            u_s[:, hs] = s * (i * xh)
        _tile_scan(a_s, u_s, h_ref, carry, tb // SUBLANES, reverse)
        hl_ref[...] = carry[0:1, :]
        if has_add:
            sum_ref[...] = h_ref[...] + add_ref[...]

    blk = pl.BlockSpec((tb, w), lambda k: (bi(k), 0))
    vec = pl.BlockSpec((1, w), lambda k: (0, 0))
    wsp = pl.BlockSpec((nh, LRU_HEAD_DIM, LRU_HEAD_DIM), lambda k: (0, 0, 0))
    out_specs = [blk, vec] + ([blk] if has_add else [])
    out_shape = [jax.ShapeDtypeStruct((t, w), F32), jax.ShapeDtypeStruct((1, w), F32)] + ([jax.ShapeDtypeStruct((t, w), F32)] if has_add else [])
    return pl.pallas_call(
        body, name=name, grid=(nb,),
        in_specs=[blk, wsp, wsp, vec, vec, vec, vec] + ([blk] if has_add else []),
        out_specs=out_specs, out_shape=out_shape,
        scratch_shapes=[pltpu.VMEM((tb, w), F32), pltpu.VMEM((tb, w), F32), pltpu.VMEM((SUBLANES, w), F32)],
        compiler_params=_params(("arbitrary",)),
    )(xc, wa, wx, ba, bx, lam, h0, *([addend] if has_add else []))


def _lru_bwd(xc, wa, wx, ba, bx, lam, h0, h, dh, dfin, addend, reverse, name):
    t, w = xc.shape
    tb = LRU_BLOCK
    nb, nh = t // tb, w // LRU_HEAD_DIM
    r8 = tb // SUBLANES
    bi = (lambda k: k) if reverse else (lambda k: nb - 1 - k)
    if reverse:
        halo = lambda k: (jnp.minimum((bi(k) + 1) * r8, t // SUBLANES - 1), 0)
    else:
        halo = lambda k: (jnp.maximum(bi(k) * r8 - 1, 0), 0)
    has_add = addend is not None
    n_ext = tb + SUBLANES

    def body(*refs):
        xc_ref, wa_ref, wx_ref, ba_ref, bx_ref, lam_ref, h0_ref, h_ref, halo_ref, dh_ref, dfin_ref = refs[:11]
        add_ref = refs[11] if has_add else None
        dxc_ref, dwa_ref, dwx_ref, dba_ref, dbx_ref, dsp_ref, dh0_ref = refs[11 + has_add:18 + has_add]
        a_s, v_s, mu_s, carry = refs[-4:]
        k = pl.program_id(0)

        @pl.when(k == 0)
        def _():
            carry[...] = jnp.broadcast_to(dfin_ref[...], carry.shape)
            for r in (dwa_ref, dwx_ref, dba_ref, dbx_ref, dsp_ref):
                r[...] = jnp.zeros_like(r)

        for hd in range(nh):
            hs = slice(hd * LRU_HEAD_DIM, (hd + 1) * LRU_HEAD_DIM)
            _, _, _, a, _ = _lru_gates(xc_ref[:, hs], wa_ref[hd], wx_ref[hd], ba_ref[:, hs], bx_ref[:, hs], lam_ref[:, hs])
            a_s[:, hs] = a
            v_s[:, hs] = a * dh_ref[:, hs]
        mu_in = carry[...]
        _tile_scan(a_s, v_s, mu_s, carry, r8, not reverse)
        dh0_ref[...] = carry[0:1, :]
        first = (bi(k) == nb - 1) if reverse else (bi(k) == 0)
        edge = jnp.where(first, jnp.broadcast_to(h0_ref[...], (SUBLANES, w)), halo_ref[...])
        for hd in range(nh):
            hs = slice(hd * LRU_HEAD_DIM, (hd + 1) * LRU_HEAD_DIM)
            if reverse:
                mu_sh = pltpu.roll(jnp.concatenate([mu_in[:, hs], mu_s[:, hs]], axis=0), 1, axis=0)[SUBLANES:]
                h_sh = pltpu.roll(jnp.concatenate([h_ref[:, hs], edge[:, hs]], axis=0), n_ext - 1, axis=0)[:tb]
            else:
                mu_sh = pltpu.roll(jnp.concatenate([mu_s[:, hs], mu_in[:, hs]], axis=0), n_ext - 1, axis=0)[:tb]
                h_sh = pltpu.roll(jnp.concatenate([edge[:, hs], h_ref[:, hs]], axis=0), 1, axis=0)[SUBLANES:]
            xh = xc_ref[:, hs]
            r, i, sp, a, s = _lru_gates(xh, wa_ref[hd], wx_ref[hd], ba_ref[:, hs], bx_ref[:, hs], lam_ref[:, hs])
            lam_t = dh_ref[:, hs] + mu_sh
            dla = lam_t * h_sh * a - lam_t * (i * xh) * (a * a) / s
            dix = lam_t * s
            dpr = dla * (-LRU_C * sp) * r * (1.0 - r)
            dpi = dix * xh * i * (1.0 - i)
            dsp_ref[:, hs] += _colsum(dla * (-LRU_C * r))
            dba_ref[:, hs] += _colsum(dpr)
            dbx_ref[:, hs] += _colsum(dpi)
            dx = dix * i + _bdot_nt(dpr, wa_ref[hd]) + _bdot_nt(dpi, wx_ref[hd])
            xt = xh.T
            dwa_ref[hd] += _bdot(xt, dpr)
            dwx_ref[hd] += _bdot(xt, dpi)
            dxc_ref[:, hs] = dx + add_ref[:, hs] if has_add else dx

    blk = pl.BlockSpec((tb, w), lambda k: (bi(k), 0))
    vec = pl.BlockSpec((1, w), lambda k: (0, 0))
    wsp = pl.BlockSpec((nh, LRU_HEAD_DIM, LRU_HEAD_DIM), lambda k: (0, 0, 0))
    wshape = jax.ShapeDtypeStruct((nh, LRU_HEAD_DIM, LRU_HEAD_DIM), F32)
    vshape = jax.ShapeDtypeStruct((1, w), F32)
    return pl.pallas_call(
        body, name=name, grid=(nb,),
        in_specs=[blk, wsp, wsp, vec, vec, vec, vec, blk, pl.BlockSpec((SUBLANES, w), halo), blk, vec] + ([blk] if has_add else []),
        out_specs=[blk, wsp, wsp, vec, vec, vec, vec],
        out_shape=[jax.ShapeDtypeStruct((t, w), F32), wshape, wshape, vshape, vshape, vshape, vshape],
        scratch_shapes=[pltpu.VMEM((tb, w), F32), pltpu.VMEM((tb, w), F32), pltpu.VMEM((tb, w), F32), pltpu.VMEM((SUBLANES, w), F32)],
        compiler_params=_params(("arbitrary",)),
    )(xc, wa, wx, ba, bx, lam, h0, h, h, dh, dfin, *([addend] if has_add else []))


def _coords():
    return lax.axis_index("x"), lax.axis_index("y"), lax.axis_index("c")


def _all_gather(shards, name):
    n = len(shards)

    def body(*refs):
        ins, outs = refs[:n], refs[n:2 * n]
        send_sems, recv_sems, local_sems = refs[2 * n:]
        x, y, c = _coords()
        slot = lambda px, py, pc: 4 * px + 2 * py + pc
        me = slot(x, y, c)
        rounds = [
            ((1 - x, y, c), [(x, y, c)]),
            ((x, 1 - y, c), [(x, y, c), (1 - x, y, c)]),
            ((x, y, 1 - c), [(x, y, c), (1 - x, y, c), (x, 1 - y, c), (1 - x, 1 - y, c)]),
        ]

        def copy(i, j, holder, to):
            s = slot(*holder)
            src = ins[i] if j in (0, 1, 3) else outs[i].at[s]
            return pltpu.make_async_remote_copy(src_ref=src, dst_ref=outs[i].at[s], send_sem=send_sems.at[i, j],
                                                recv_sem=recv_sems.at[i, j], device_id=to, device_id_type=MESH)

        local = [pltpu.make_async_copy(ins[i], outs[i].at[me], local_sems.at[i]) for i in range(n)]
        for cp in local:
            cp.start()
        sent, j = [], 0
        for to, holders in rounds:
            first = j
            for holder in holders:
                for i in range(n):
                    cp = copy(i, j, holder, to)
                    cp.start()
                    sent.append(cp)
                j += 1
            dx, dy, dc = to[0] != x, to[1] != y, to[2] != c
            for jj, holder in enumerate(holders):
                theirs = tuple(jnp.where(flip, 1 - v, v) for flip, v in zip((dx, dy, dc), holder))
                for i in range(n):
                    copy(i, first + jj, theirs, to).wait_recv()
        for cp in sent:
            cp.wait_send()
        for cp in local:
            cp.wait()

    any_spec = pl.BlockSpec(memory_space=pl.ANY)
    return pl.pallas_call(
        body, name=name, in_specs=[any_spec] * n, out_specs=[any_spec] * n,
        out_shape=[jax.ShapeDtypeStruct((N_DEV,) + s.shape, s.dtype) for s in shards],
        scratch_shapes=[pltpu.SemaphoreType.DMA((n, 7)), pltpu.SemaphoreType.DMA((n, 7)), pltpu.SemaphoreType.DMA((n,))],
    )(*shards)


def _swap(bufs, axis, name):
    n = len(bufs)

    def body(*refs):
        ins, outs = refs[:n], refs[n:2 * n]
        send_sems, recv_sems = refs[2 * n:]
        x, y, c = _coords()
        to = {"x": (1 - x, y, c), "y": (x, 1 - y, c), "c": (x, y, 1 - c)}[axis]
        cps = [pltpu.make_async_remote_copy(src_ref=ins[i], dst_ref=outs[i], send_sem=send_sems.at[i], recv_sem=recv_sems.at[i],
                                            device_id=to, device_id_type=MESH) for i in range(n)]
        for cp in cps:
            cp.start()
        for cp in cps:
            cp.wait()

    any_spec = pl.BlockSpec(memory_space=pl.ANY)
    return pl.pallas_call(
        body, name=name, in_specs=[any_spec] * n, out_specs=[any_spec] * n,
        out_shape=[jax.ShapeDtypeStruct(b.shape, b.dtype) for b in bufs],
        scratch_shapes=[pltpu.SemaphoreType.DMA((n,)), pltpu.SemaphoreType.DMA((n,))],
    )(*bufs)


def _as2d(a):
    return a.reshape(-1, a.shape[-1])


def _add_f32(a, b, name):
    (out,), _ = _rowwise(lambda r, s: ([r[0].astype(F32) + r[1].astype(F32)], []), [_as2d(a), _as2d(b)], [],
                         [(a.shape[-1], F32)], [], name)
    return out.reshape(a.shape)


def _reduce_scatter(parts, name):
    x, y, c = _coords()
    cur = [p.reshape((2, 2, 2) + p.shape[1:]) for p in parts]
    for rnd, (axis, idx, dim) in enumerate((("c", c, 2), ("y", y, 1), ("x", x, 0))):
        keep = [lax.dynamic_index_in_dim(p, idx, dim, keepdims=False) for p in cur]
        send = [lax.dynamic_index_in_dim(p, 1 - idx, dim, keepdims=False).astype(BF16) for p in cur]
        got = _swap(send, axis, f"{name}_swap_{axis}")
        cur = [_add_f32(k, g, f"{name}_add_{axis}_{i}") for i, (k, g) in enumerate(zip(keep, got))]
    return cur


def _pack(arrs):
    flat = jnp.concatenate([a.reshape(-1).astype(F32) for a in arrs])
    rows = -(-flat.shape[0] // LANES)
    rows = -(-rows // 512) * 512
    return jnp.pad(flat, (0, rows * LANES - flat.shape[0])).reshape(rows, LANES)


def _unpack(buf, like):
    flat, out, off = buf.reshape(-1), [], 0
    for a in like:
        out.append(flat[off:off + a.size].reshape(a.shape))
        off += a.size
    return out


def _all_reduce(arrs, name):
    (g,) = _all_gather([_pack(arrs)], name + "_gather")
    (tot,), _ = _rowwise(lambda r, s: ([functools.reduce(lambda u, v: u + v, r)], []), [g[i] for i in range(N_DEV)], [],
                         [(LANES, F32)], [], name + "_sum")
    return _unpack(tot, arrs)


def _adamw(w, g, m, v, name):
    def fn(r, s):
        w_, g_, m_, v_ = r
        m_ = ADAM_B1 * m_ + (1.0 - ADAM_B1) * g_
        v_ = ADAM_B2 * v_ + (1.0 - ADAM_B2) * jnp.square(g_)
        m_hat = m_ / (1.0 - ADAM_B1 ** ADAM_STEP)
        v_hat = v_ / (1.0 - ADAM_B2 ** ADAM_STEP)
        return [-ADAM_LR * (m_hat / (jnp.sqrt(v_hat) + ADAM_EPS) + ADAM_WD * w_), m_, v_], []

    c = w.shape[-1]
    (d, nm, nv), _ = _rowwise(fn, [_as2d(a) for a in (w, g, m, v)], [], [(c, F32)] * 3, [], name)
    return d.reshape(w.shape), nm.reshape(w.shape), nv.reshape(w.shape)


def _norm_mod(x, gain, sc, sh, name):
    def fn(r, s):
        xv = r[0]
        xn = xv * lax.rsqrt(_rowmean(xv * xv) + EPS)
        return [xn * s[0] * (1.0 + s[1]) + s[2]], []
    (h,), _ = _rowwise(fn, [x], [gain, sc, sh], [(x.shape[1], BF16)], [], name)
    return h


def _norm_mod_bwd(x, dh, dres, gain, sc, name):
    d = x.shape[1]

    def fn(r, s):
        xv, dhv = r[0], r[1]
        rs = lax.rsqrt(_rowmean(xv * xv) + EPS)
        xn = xv * rs
        dxn = dhv * (1.0 + s[1]) * s[0]
        dx = rs * (dxn - xn * _rowmean(dxn * xn))
        if dres is not None:
            dx = dx + r[2]
        return [dx], [_colsum(dhv), _colsum(dhv * xn * s[0]), _colsum(dhv * (1.0 + s[1]) * xn)]
    (dx,), accs = _rowwise(fn, [x, dh] + ([dres] if dres is not None else []), [gain, sc], [(d, F32)], [d, d, d], name)
    return dx, accs


def _ada_rows(mod_row, d):
    return [mod_row[:, i * d:(i + 1) * d] for i in range(N_MOD)]


def _to_col_major(t):
    n, c = t.shape
    return t.reshape(n // GRID_W, GRID_W, c).transpose(1, 0, 2).reshape(n, c)


def _to_row_major(t):
    n, c = t.shape
    return t.reshape(GRID_W, n // GRID_W, c).transpose(1, 0, 2).reshape(n, c)


def kernel(x, c, ctx, c_ctx, w_ada, b_ada, norm_mix, norm_ffn, w_in, ssd_conv_w, ssd_conv_b, ssd_dt_bias, ssd_a_log, ssd_d, ssd_norm, w_out_ssd, lru_conv_w, lru_conv_b, lru_w_a, lru_b_a, lru_w_x, lru_b_x, lru_lambda, w_out_lru, w_gate, b_gate, w_o, ffn_w13, ffn_w2, final_norm, loss_target, m_c_ctx, m_w_ada, m_b_ada, m_norm_mix, m_norm_ffn, m_w_in, m_ssd_conv_w, m_ssd_conv_b, m_ssd_dt_bias, m_ssd_a_log, m_ssd_d, m_ssd_norm, m_w_out_ssd, m_lru_conv_w, m_lru_conv_b, m_lru_w_a, m_lru_b_a, m_lru_w_x, m_lru_b_x, m_lru_lambda, m_w_out_lru, m_w_gate, m_b_gate, m_w_o, m_ffn_w13, m_ffn_w2, m_final_norm, v_c_ctx, v_w_ada, v_b_ada, v_norm_mix, v_norm_ffn, v_w_in, v_ssd_conv_w, v_ssd_conv_b, v_ssd_dt_bias, v_ssd_a_log, v_ssd_d, v_ssd_norm, v_w_out_ssd, v_lru_conv_w, v_lru_conv_b, v_lru_w_a, v_lru_b_a, v_lru_w_x, v_lru_b_x, v_lru_lambda, v_w_out_lru, v_w_gate, v_b_gate, v_w_o, v_ffn_w13, v_ffn_w2, v_final_norm):
    names = ["c_ctx", "w_ada", "b_ada", "norm_mix", "norm_ffn", "w_in", "ssd_conv_w", "ssd_conv_b", "ssd_dt_bias", "ssd_a_log",
             "ssd_d", "ssd_norm", "w_out_ssd", "lru_conv_w", "lru_conv_b", "lru_w_a", "lru_b_a", "lru_w_x", "lru_b_x", "lru_lambda",
             "w_out_lru", "w_gate", "b_gate", "w_o", "ffn_w13", "ffn_w2", "final_norm"]
    wts = dict(zip(names, (c_ctx, w_ada, b_ada, norm_mix, norm_ffn, w_in, ssd_conv_w, ssd_conv_b, ssd_dt_bias, ssd_a_log, ssd_d, ssd_norm, w_out_ssd, lru_conv_w, lru_conv_b, lru_w_a, lru_b_a, lru_w_x, lru_b_x, lru_lambda, w_out_lru, w_gate, b_gate, w_o, ffn_w13, ffn_w2, final_norm)))
    mom1 = dict(zip(names, (m_c_ctx, m_w_ada, m_b_ada, m_norm_mix, m_norm_ffn, m_w_in, m_ssd_conv_w, m_ssd_conv_b, m_ssd_dt_bias, m_ssd_a_log, m_ssd_d, m_ssd_norm, m_w_out_ssd, m_lru_conv_w, m_lru_conv_b, m_lru_w_a, m_lru_b_a, m_lru_w_x, m_lru_b_x, m_lru_lambda, m_w_out_lru, m_w_gate, m_b_gate, m_w_o, m_ffn_w13, m_ffn_w2, m_final_norm)))
    mom2 = dict(zip(names, (v_c_ctx, v_w_ada, v_b_ada, v_norm_mix, v_norm_ffn, v_w_in, v_ssd_conv_w, v_ssd_conv_b, v_ssd_dt_bias, v_ssd_a_log, v_ssd_d, v_ssd_norm, v_w_out_ssd, v_lru_conv_w, v_lru_conv_b, v_lru_w_a, v_lru_b_a, v_lru_w_x, v_lru_b_x, v_lru_lambda, v_w_out_lru, v_w_gate, v_b_gate, v_w_o, v_ffn_w13, v_ffn_w2, v_final_norm)))

    xi, yi, ci_ = _coords()
    me = 4 * xi + 2 * yi + ci_
    x2, tgt, ctx2 = x[0], loss_target[0], ctx[0]
    t, d = x2.shape
    tc = ctx2.shape[0]
    ws = N_DEV * w_out_ssd.shape[1]
    wl = N_DEV * w_out_lru.shape[1]
    heads = ws // SSD_HEAD_DIM
    gn = SSD_GROUPS * SSD_STATE
    nh = wl // LRU_HEAD_DIM
    ff = N_DEV * ffn_w2.shape[1]
    ada_cols = w_ada.shape[2]

    big = [w_in[0], w_out_ssd[0], w_out_lru[0], w_gate[0], w_o[0], ffn_w13[0], ffn_w2[0]]
    smalls = [c, ssd_conv_w[0], lru_conv_w[0], lru_b_a[0], lru_b_x[0], lru_lambda[0]]
    got = _all_gather(smalls + [b.astype(BF16) for b in big], "gather_weights")
    c_all = got[0].reshape(N_DEV, d)
    colcat = lambda g: jnp.moveaxis(g, 0, -2).reshape(g.shape[1:-1] + (N_DEV * g.shape[-1],))
    conv_w_s, conv_w_l, lru_ba, lru_bx, lru_lam = (colcat(g) for g in got[1:6])
    wg_in, wg_os, wg_ol, wg_gate, wg_o, wg_13, wg_2 = got[6:]
    w_in_f = colcat(wg_in)
    o_z, o_xbc, o_dt, o_xr, o_yr = 0, ws, 2 * ws + 2 * gn, 2 * ws + 2 * gn + 2 * heads, 2 * ws + 2 * gn + 2 * heads + wl
    w_z, w_xbc = w_in_f[:, o_z:o_xbc], w_in_f[:, o_xbc:o_dt]
    w_dt = jnp.pad(w_in_f[:, o_dt:o_xr], ((0, 0), (0, LANES - 2 * heads)))
    w_xr, w_yr = w_in_f[:, o_xr:o_yr], w_in_f[:, o_yr:]
    w_os, w_ol, w_oo, w_2 = wg_os.reshape(ws, d), wg_ol.reshape(wl, d), wg_o.reshape(d, d), wg_2.reshape(ff, d)
    w_gf = colcat(wg_gate)
    w_gs, w_gr = w_gf[:, :d], w_gf[:, d:]
    w_13f = colcat(wg_13)
    w_1, w_3 = w_13f[:, :ff], w_13f[:, ff:]

    cond = jnp.concatenate([c_all, c_ctx[None, :], jnp.zeros((16 - N_DEV - 1, d), F32)], axis=0)
    cond_act = _silu(cond)
    mod_l = _mm([(cond_act, w_ada[0])], "nn", F32, "ada_fwd")
    (mod_g,) = _all_gather([mod_l], "gather_mod")
    mod_full = colcat(mod_g) + b_ada
    sh_m, sc_m, g_m, sh_f, sc_f, g_f = _ada_rows(lax.dynamic_slice_in_dim(mod_full, me, 1, 0), d)
    csh_m, csc_m = _ada_rows(mod_full[N_DEV:N_DEV + 1], d)[:2]

    bias_s, alog_s = ssd_dt_bias[0], ssd_a_log[0]
    d_full = jnp.repeat(ssd_d[0], SSD_HEAD_DIM)[None, :]
    zeros_state = jnp.zeros((heads // 2, SSD_STATE, LANES), F32)
    zeros_l = jnp.zeros((1, wl), F32)
    wa, wx = lru_w_a[0], lru_w_x[0]

    def mixer_inputs(hh, tag, full):
        xbc = _mm([(hh, w_xbc)], "nn", F32, f"{tag}_proj_xbc")
        dtr = _mm([(hh, w_dt)], "nn", F32, f"{tag}_proj_dt")[:, :2 * heads]
        xr = _mm([(hh, w_xr)], "nn", F32, f"{tag}_proj_xr")
        cx = _conv(xbc, conv_w_s, ssd_conv_b, CONV_LEFT, True, f"{tag}_ssd_conv")
        xs_, bm_, cm_ = cx[:, :ws], cx[:, ws:ws + gn], cx[:, ws + gn:]
        xr_seq = _to_col_major(xr) if full else xr
        xc = _conv(xr_seq, conv_w_l, lru_conv_b, CONV_LEFT, False, f"{tag}_lru_conv")
        return xbc, dtr, xr_seq, xs_, bm_, cm_, xc

    def lru_args(dirn):
        return wa[dirn], wx[dirn], lru_ba[dirn:dirn + 1], lru_bx[dirn:dirn + 1], lru_lam[dirn:dirn + 1]

    hc = _norm_mod(ctx2, norm_mix, csc_m, csh_m, "ctx_norm")
    c_xbc, c_dtr, c_xr_seq, c_xs, c_bm, c_cm, c_xc = mixer_inputs(hc, "ctx", False)
    c_ssd = []
    for dirn in (0, 1):
        _, se, sf = _ssd_fwd(c_xs, c_bm, c_cm, c_dtr[:, dirn * heads:(dirn + 1) * heads], bias_s[dirn:dirn + 1], alog_s[dirn:dirn + 1],
                             zeros_state, bool(dirn), f"ctx_ssd_fwd{dirn}")
        c_ssd.append((se, sf))
    c_lru = []
    for dirn in (0, 1):
        res = _lru_fwd(c_xc, *lru_args(dirn), zeros_l, None, bool(dirn), f"ctx_lru_fwd{dirn}")
        c_lru.append((res[0], res[1]))

    h = _norm_mod(x2, norm_mix, sc_m, sh_m, "lat_norm")
    z = _mm([(h, w_z)], "nn", F32, "lat_proj_z")
    yr = _mm([(h, w_yr)], "nn", F32, "lat_proj_yr")
    gp_s = _mm([(h, w_gs)], "nn", F32, "lat_proj_gs")
    gp_r = _mm([(h, w_gr)], "nn", F32, "lat_proj_gr")
    l_xbc, l_dtr, l_xr_seq, l_xs, l_bm, l_cm, l_xc = mixer_inputs(h, "lat", True)
    l_ssd = []
    for dirn in (0, 1):
        yv, se, _ = _ssd_fwd(l_xs, l_bm, l_cm, l_dtr[:, dirn * heads:(dirn + 1) * heads], bias_s[dirn:dirn + 1], alog_s[dirn:dirn + 1],
                             c_ssd[dirn][1], bool(dirn), f"lat_ssd_fwd{dirn}")
        l_ssd.append((yv, se))

    def ssd_out(r, s):
        yf, yb, xs_, z_ = r
        yv = yf + yb + s[0] * xs_
        yz = yv * _silu(z_)
        return [yz * lax.rsqrt(_rowmean(yz * yz) + EPS) * s[1]], []
    (ys,), _ = _rowwise(ssd_out, [l_ssd[0][0], l_ssd[1][0], l_xs, z], [d_full, ssd_norm], [(ws, BF16)], [], "lat_ssd_out")
    o_s = _mm([(ys, w_os)], "nn", F32, "lat_out_ssd")

    hf, _ = _lru_fwd(l_xc, *lru_args(0), c_lru[0][1], None, False, "lat_lru_fwd0")
    hb, _, r_seq = _lru_fwd(l_xc, *lru_args(1), c_lru[1][1], hf, True, "lat_lru_fwd1")
    r_out = _to_row_major(r_seq)
    (gr,), _ = _rowwise(lambda r, s: ([r[0] * _gelu(r[1])], []), [r_out, yr], [], [(wl, BF16)], [], "lat_lru_out")
    o_r = _mm([(gr, w_ol)], "nn", F32, "lat_out_lru")

    bg_s, bg_r = b_gate[:, :d], b_gate[:, d:]
    (mix,), _ = _rowwise(lambda r, s: ([_sigmoid(r[0] + s[0]) * r[2] + _sigmoid(r[1] + s[1]) * r[3]], []),
                         [gp_s, gp_r, o_s, o_r], [bg_s, bg_r], [(d, BF16)], [], "lat_merge")
    mixo = _mm([(mix, w_oo)], "nn", F32, "lat_out_o")

    def resid_norm(r, s):
        x1v = r[0] + s[0] * r[1]
        xn = x1v * lax.rsqrt(_rowmean(x1v * x1v) + EPS)
        return [x1v, xn * s[1] * (1.0 + s[2]) + s[3]], []
    (x1, h2), _ = _rowwise(resid_norm, [x2, mixo], [g_m, norm_ffn, sc_f, sh_f], [(d, F32), (d, BF16)], [], "lat_resid1")
    fg = _mm([(h2, w_1)], "nn", F32, "ffn_gate")
    fu = _mm([(h2, w_3)], "nn", F32, "ffn_up")
    (act,), _ = _rowwise(lambda r, s: ([_silu(r[0]) * r[1]], []), [fg, fu], [], [(ff, BF16)], [], "ffn_act")
    fo = _mm([(act, w_2)], "nn", F32, "ffn_down")

    fn_gain = final_norm[None, :]

    def head(r, s):
        x1v, fv, tv = r
        xv = x1v + s[0] * fv
        rs = lax.rsqrt(_rowmean(xv * xv) + EPS)
        xn = xv * rs
        err = xn * s[1] - tv
        dyv = err * (1.0 / d)
        dxn = dyv * s[1]
        dxv = rs * (dxn - xn * _rowmean(dxn * xn))
        return [dxv, dxv * s[0]], [_colsum(err * err), _colsum(dyv * xn), _colsum(dxv * fv)]
    (dx2, dfo), (sq, g_final, dg_f) = _rowwise(head, [x1, fo, tgt], [g_f, fn_gain], [(d, F32), (d, BF16)], [d, d, d], "loss_head")
    loss = lax.psum(0.5 * jnp.sum(sq) / d, ("x", "y", "c"))

    dact = _mm([(dfo, w_2)], "nt", F32, "ffn_down_dx")
    gw_2 = _mm([(act, dfo)], "tn", F32, "ffn_down_dw")
    (dfg, dfu), _ = _rowwise(lambda r, s: ([r[0] * r[2] * _dsilu(r[1]), r[0] * _silu(r[1])], []), [dact, fg, fu], [],
                             [(ff, BF16), (ff, BF16)], [], "ffn_act_bwd")
    gw_1 = _mm([(h2, dfg)], "tn", F32, "ffn_gate_dw")
    gw_3 = _mm([(h2, dfu)], "tn", F32, "ffn_up_dw")
    dh2 = _mm([(dfg, w_1), (dfu, w_3)], "nt", F32, "ffn_dx")
    dx1, (dsh_f, dsc_f, g_norm_ffn) = _norm_mod_bwd(x1, dh2, dx2, norm_ffn, sc_f, "lat_norm_ffn_bwd")

    (dmixo,), (dg_m,) = _rowwise(lambda r, s: ([s[0] * r[0]], [_colsum(r[0] * r[1])]), [dx1, mixo], [g_m], [(d, BF16)], [d], "lat_resid1_bwd")
    dmix = _mm([(dmixo, w_oo)], "nt", F32, "lat_out_o_dx")
    gw_o = _mm([(mix, dmixo)], "tn", F32, "lat_out_o_dw")

    def merge_bwd(r, s):
        dm, gps, gpr, os_, or_ = r
        g1, g2 = _sigmoid(gps + s[0]), _sigmoid(gpr + s[1])
        dg1, dg2 = dm * os_ * g1 * (1.0 - g1), dm * or_ * g2 * (1.0 - g2)
        return [dm * g1, dm * g2, dg1, dg2], [_colsum(dg1), _colsum(dg2)]
    (do_s, do_r, dgp_s, dgp_r), (gb_gs, gb_gr) = _rowwise(merge_bwd, [dmix, gp_s, gp_r, o_s, o_r], [bg_s, bg_r],
                                                            [(d, BF16)] * 4, [d, d], "lat_merge_bwd")
    gw_gs = _mm([(h, dgp_s)], "tn", F32, "lat_gs_dw")
    gw_gr = _mm([(h, dgp_r)], "tn", F32, "lat_gr_dw")

    dgr = _mm([(do_r, w_ol)], "nt", F32, "lat_out_lru_dx")
    gw_ol = _mm([(gr, do_r)], "tn", F32, "lat_out_lru_dw")
    (dr_out, dyr), _ = _rowwise(lambda r, s: ([r[0] * _gelu(r[2]), r[0] * r[1] * _dgelu(r[2])], []), [dgr, r_out, yr], [],
                                [(wl, F32), (wl, BF16)], [], "lat_lru_out_bwd")
    dr_seq = _to_col_major(dr_out)
    lb0 = _lru_bwd(l_xc, *lru_args(0), c_lru[0][1], hf, dr_seq, zeros_l, None, False, "lat_lru_bwd0")
    lb1 = _lru_bwd(l_xc, *lru_args(1), c_lru[1][1], hb, dr_seq, zeros_l, lb0[0], True, "lat_lru_bwd1")
    dxr_seq, g_lconv_w, g_lconv_b = _conv_bwd(l_xr_seq, lb1[0], conv_w_l, lru_conv_b, False, "lat_lru_conv_bwd")
    dxr = _to_row_major(dxr_seq)

    dys = _mm([(do_s, w_os)], "nt", F32, "lat_out_ssd_dx")
    gw_os = _mm([(ys, do_s)], "tn", F32, "lat_out_ssd_dw")

    def ssd_out_bwd(r, s):
        yf, yb, xs_, z_, dyn = r
        yv = yf + yb + s[0] * xs_
        sz = _silu(z_)
        yz = yv * sz
        rs = lax.rsqrt(_rowmean(yz * yz) + EPS)
        yzn = yz * rs
        dyzn = dyn * s[1]
        dyz = rs * (dyzn - yzn * _rowmean(dyzn * yzn))
        dyv = dyz * sz
        return [dyv, dyz * yv * _dsilu(z_), dyv * s[0]], [_colsum(dyv * xs_), _colsum(dyn * yzn)]
    (dy_ssd, dz, dxs_skip), (g_dfull, g_ssd_norm) = _rowwise(ssd_out_bwd, [l_ssd[0][0], l_ssd[1][0], l_xs, z, dys], [d_full, ssd_norm],
                                                              [(ws, F32), (ws, BF16), (ws, F32)], [ws, ws], "lat_ssd_out_bwd")
    sb = [_ssd_bwd(l_xs, l_bm, l_cm, l_dtr[:, dirn * heads:(dirn + 1) * heads], bias_s[dirn:dirn + 1], alog_s[dirn:dirn + 1],
                   dy_ssd, l_ssd[dirn][1], zeros_state, bool(dirn), f"lat_ssd_bwd{dirn}") for dirn in (0, 1)]
    (dcx,), _ = _rowwise(lambda r, s: ([jnp.concatenate([r[0] + r[1] + r[2], r[3] + r[4], r[5] + r[6]], axis=1)], []),
                         [dxs_skip, sb[0][0], sb[1][0], sb[0][1], sb[1][1], sb[0][2], sb[1][2]], [], [(ws + 2 * gn, F32)], [], "lat_ssd_dsum")
    dxbc, g_sconv_w, g_sconv_b = _conv_bwd(l_xbc, dcx, conv_w_s, ssd_conv_b, True, "lat_ssd_conv_bwd")
    ddt = jnp.pad(jnp.concatenate([sb[0][3], sb[1][3]], axis=1), ((0, 0), (0, LANES - 2 * heads)))

    zeros_cy = jnp.zeros((tc, ws), F32)
    csb = [_ssd_bwd(c_xs, c_bm, c_cm, c_dtr[:, dirn * heads:(dirn + 1) * heads], bias_s[dirn:dirn + 1], alog_s[dirn:dirn + 1],
                    zeros_cy, c_ssd[dirn][0], sb[dirn][6], bool(dirn), f"ctx_ssd_bwd{dirn}") for dirn in (0, 1)]
    (c_dcx,), _ = _rowwise(lambda r, s: ([jnp.concatenate([r[0] + r[1], r[2] + r[3], r[4] + r[5]], axis=1)], []),
                           [csb[0][0], csb[1][0], csb[0][1], csb[1][1], csb[0][2], csb[1][2]], [], [(ws + 2 * gn, F32)], [], "ctx_ssd_dsum")
    c_dxbc, cg_sconv_w, cg_sconv_b = _conv_bwd(c_xbc, c_dcx, conv_w_s, ssd_conv_b, True, "ctx_ssd_conv_bwd")
    c_ddt = jnp.pad(jnp.concatenate([csb[0][3], csb[1][3]], axis=1), ((0, 0), (0, LANES - 2 * heads)))
    zeros_ch = jnp.zeros((tc, wl), F32)
    clb0 = _lru_bwd(c_xc, *lru_args(0), zeros_l, c_lru[0][0], zeros_ch, lb0[6], None, False, "ctx_lru_bwd0")
    clb1 = _lru_bwd(c_xc, *lru_args(1), zeros_l, c_lru[1][0], zeros_ch, lb1[6], clb0[0], True, "ctx_lru_bwd1")
    c_dxr, cg_lconv_w, cg_lconv_b = _conv_bwd(c_xr_seq, clb1[0], conv_w_l, lru_conv_b, False, "ctx_lru_conv_bwd")
    dhc = _mm([(c_dxbc, w_xbc), (c_ddt, w_dt), (c_dxr, w_xr)], "nt", F32, "ctx_proj_dx")
    _, (dcsh_m, dcsc_m, cg_norm_mix) = _norm_mod_bwd(ctx2, dhc, None, norm_mix, csc_m, "ctx_norm_bwd")

    gw_z = _mm([(h, dz)], "tn", F32, "lat_z_dw")
    gw_yr = _mm([(h, dyr)], "tn", F32, "lat_yr_dw")
    gw_xbc = _mm([(h, dxbc), (hc, c_dxbc)], "tn", F32, "proj_xbc_dw")
    gw_dt = _mm([(h, ddt), (hc, c_ddt)], "tn", F32, "proj_dt_dw")[:, :2 * heads]
    gw_xr = _mm([(h, dxr), (hc, c_dxr)], "tn", F32, "proj_xr_dw")
    dh = _mm([(dz, w_z), (dxbc, w_xbc), (ddt, w_dt), (dxr, w_xr), (dyr, w_yr), (dgp_s, w_gs), (dgp_r, w_gr)], "nt", F32, "lat_proj_dx")
    grad_x, (dsh_m, dsc_m, g_norm_mix) = _norm_mod_bwd(x2, dh, dx1, norm_mix, sc_m, "lat_norm_bwd")

    zero_d = jnp.zeros((1, d), F32)
    dmod_rows = jnp.concatenate([jnp.concatenate([dsh_m, dsc_m, dg_m, dsh_f, dsc_f, dg_f], axis=1),
                                 jnp.concatenate([dcsh_m, dcsc_m, zero_d, zero_d, zero_d, zero_d], axis=1)], axis=0)
    (dmod_g,) = _all_gather([dmod_rows], "gather_dmod")
    dmod_c = functools.reduce(lambda u, v: u + v, [dmod_g[i, 1] for i in range(N_DEV)])
    dmod9 = jnp.concatenate([dmod_g[:, 0], dmod_c[None, :], jnp.zeros((16 - N_DEV - 1, N_MOD * d), F32)], axis=0)
    g_b_ada = jnp.sum(dmod9, axis=0, keepdims=True)
    dmod_l = lax.dynamic_slice_in_dim(dmod9, me * ada_cols, ada_cols, 1)
    g_w_ada = _mm([(cond_act, dmod_l)], "tn", F32, "ada_dw")
    dcond_l = _mm([(dmod_l, w_ada[0])], "nt", F32, "ada_dx")

    small_parts = [
        dcond_l[N_DEV:N_DEV + 1], g_norm_mix + cg_norm_mix, g_norm_ffn, g_sconv_w + cg_sconv_w, g_sconv_b + cg_sconv_b,
        jnp.concatenate([sb[0][5] + csb[0][5], sb[1][5] + csb[1][5]], axis=0),
        jnp.concatenate([sb[0][4] + csb[0][4], sb[1][4] + csb[1][4]], axis=0),
        g_dfull, g_ssd_norm, g_lconv_w + cg_lconv_w, g_lconv_b + cg_lconv_b,
        jnp.stack([lb0[1] + clb0[1], lb1[1] + clb1[1]]), jnp.concatenate([lb0[3] + clb0[3], lb1[3] + clb1[3]], axis=0),
        jnp.stack([lb0[2] + clb0[2], lb1[2] + clb1[2]]), jnp.concatenate([lb0[4] + clb0[4], lb1[4] + clb1[4]], axis=0),
        jnp.concatenate([lb0[5] + clb0[5], lb1[5] + clb1[5]], axis=0),
        jnp.concatenate([gb_gs, gb_gr], axis=1), g_final,
    ]
    (s_cctx, s_nmix, s_nffn, s_sconv_w, s_sconv_b, s_dtb, s_da, s_dfull, s_snorm, s_lconv_w, s_lconv_b,
     s_wa, s_ba, s_wx, s_bx, s_dsp, s_bgate, s_final) = _all_reduce(small_parts, "small_grads")
    shard = lambda a, n_: lax.dynamic_slice_in_dim(a, me * n_, n_, a.ndim - 1)
    grads = {
        "c_ctx": (s_cctx * _dsilu(c_ctx[None, :]))[0],
        "w_ada": g_w_ada[None], "b_ada": g_b_ada, "norm_mix": s_nmix, "norm_ffn": s_nffn,
        "ssd_conv_w": shard(s_sconv_w, ssd_conv_w.shape[2])[None], "ssd_conv_b": s_sconv_b,
        "ssd_dt_bias": s_dtb[None], "ssd_a_log": (s_da * (-jnp.exp(alog_s)))[None],
        "ssd_d": jnp.sum(s_dfull.reshape(heads, SSD_HEAD_DIM), axis=1)[None], "ssd_norm": s_snorm,
        "lru_conv_w": shard(s_lconv_w, lru_conv_w.shape[2])[None], "lru_conv_b": s_lconv_b,
        "lru_w_a": s_wa[None], "lru_b_a": shard(s_ba, lru_b_a.shape[2])[None],
        "lru_w_x": s_wx[None], "lru_b_x": shard(s_bx, lru_b_x.shape[2])[None],
        "lru_lambda": shard(s_dsp * (-_sigmoid(-lru_lam)), lru_lambda.shape[2])[None],
        "b_gate": s_bgate, "final_norm": s_final[0],
    }

    slots = lambda g: jnp.moveaxis(g.reshape(g.shape[0], N_DEV, g.shape[1] // N_DEV), 1, 0)
    rslots = lambda g: g.reshape(N_DEV, g.shape[0] // N_DEV, g.shape[1])
    gw_in = jnp.concatenate([gw_z, gw_xbc, gw_dt, gw_xr, gw_yr], axis=1)
    parts = [slots(gw_in), rslots(gw_os), rslots(gw_ol), slots(jnp.concatenate([gw_gs, gw_gr], axis=1)), rslots(gw_o),
             slots(jnp.concatenate([gw_1, gw_3], axis=1)), rslots(gw_2)]
    for n_, g in zip(["w_in", "w_out_ssd", "w_out_lru", "w_gate", "w_o", "ffn_w13", "ffn_w2"], _reduce_scatter(parts, "grad_rs")):
        grads[n_] = g[None]

    big_names = ["w_ada", "w_in", "w_out_ssd", "w_out_lru", "w_gate", "w_o", "ffn_w13", "ffn_w2"]
    small_names = [n_ for n_ in names if n_ not in big_names]
    delta, new_m, new_v = {}, {}, {}
    for n_ in big_names:
        delta[n_], new_m[n_], new_v[n_] = _adamw(wts[n_], grads[n_], mom1[n_], mom2[n_], "adamw_" + n_)
    packed = [_pack([src[n_] for n_ in small_names]) for src in (wts, grads, mom1, mom2)]
    outs = _adamw(*packed, "adamw_small")
    like = [wts[n_] for n_ in small_names]
    for dst, buf in zip((delta, new_m, new_v), outs):
        dst.update(zip(small_names, _unpack(buf, like)))

    return (loss, grad_x[None], *[grads[n_].reshape(wts[n_].shape) for n_ in names], *[delta[n_] for n_ in names],
            *[new_m[n_] for n_ in names], *[new_v[n_] for n_ in names])
```

```python
import functools

import numpy as np
import jax
import jax.numpy as jnp
from jax import lax
from jax.experimental import pallas as pl
from jax.experimental.pallas import tpu as pltpu

F32, BF16 = jnp.float32, jnp.bfloat16
MESH = pl.DeviceIdType.MESH
N_DEV = 8
VMEM_LIMIT_BYTES = 48 * 1024 * 1024
ROW_BUDGET_BYTES = 12 * 1024 * 1024
LANES, SUBLANES = 128, 8

EPS = 1e-6
GRID_W = 64
SSD_HEAD_DIM = 64
SSD_GROUPS = 4
SSD_STATE = 128
SSD_CHUNK = 128
CONV_W = 4
CONV_LEFT = 2
LRU_HEAD_DIM = 128
LRU_C = 8.0
LRU_BLOCK = 128
N_MOD = 6
ADAM_LR, ADAM_B1, ADAM_B2, ADAM_EPS, ADAM_WD, ADAM_STEP = 0.001, 0.9, 0.999, 1e-08, 0.01, 10
NEG_BIG = -1e30


def _params(sem):
    return pltpu.CompilerParams(dimension_semantics=sem, vmem_limit_bytes=VMEM_LIMIT_BYTES)


def _tile(dim, pref, align):
    if dim <= pref:
        return dim
    t = (pref // align) * align
    while t >= align:
        if dim % t == 0:
            return t
        t -= align
    return dim


def _bdot(a, b):
    return jnp.dot(a.astype(BF16), b.astype(BF16), preferred_element_type=F32)


def _bdot_nt(a, b):
    return lax.dot_general(a.astype(BF16), b.astype(BF16), (((1,), (1,)), ((), ())), preferred_element_type=F32)


def _bdot_tn(a, b):
    return lax.dot_general(a.astype(BF16), b.astype(BF16), (((0,), (0,)), ((), ())), preferred_element_type=F32)


def _hdot(a, b):
    return jnp.dot(a, b, preferred_element_type=F32, precision=lax.Precision.HIGHEST)


def _softplus(v):
    return jnp.maximum(v, 0.0) + jnp.log(1.0 + jnp.exp(-jnp.abs(v)))


def _sigmoid(v):
    return 1.0 / (1.0 + jnp.exp(-v))


def _silu(v):
    return v * _sigmoid(v)


def _dsilu(v):
    s = _sigmoid(v)
    return s * (1.0 + v * (1.0 - s))


GELU_C0 = 0.7978845608028654
GELU_C1 = 0.044715


def _gelu(v):
    return 0.5 * v * (1.0 + jnp.tanh(GELU_C0 * (v + GELU_C1 * v * v * v)))


def _dgelu(v):
    t = jnp.tanh(GELU_C0 * (v + GELU_C1 * v * v * v))
    return 0.5 * (1.0 + t) + 0.5 * v * (1.0 - t * t) * GELU_C0 * (1.0 + 3.0 * GELU_C1 * v * v)


MM_TM, MM_TN = 1024, 1408
MM_TK = {1: 2048, 2: 1536}


def _mm(pairs, mode, out_dtype, name):
    assert out_dtype == F32
    a0, b0 = pairs[0]
    if mode == "nn":
        m, n, ks = a0.shape[0], b0.shape[1], [a.shape[1] for a, _ in pairs]
    elif mode == "nt":
        m, n, ks = a0.shape[0], b0.shape[0], [a.shape[1] for a, _ in pairs]
    else:
        m, n, ks = a0.shape[1], b0.shape[1], [a.shape[0] for a, _ in pairs]
    tm = _tile(m, MM_TM, LANES if mode == "tn" else 16)
    tn = _tile(n, MM_TN, LANES)
    tk = MM_TK.get(len(pairs), 512)
    tks = [_tile(k, tk, 16 if mode == "tn" else LANES) for k in ks]
    steps = [k // t for k, t in zip(ks, tks)]
    offs = [sum(steps[:i]) for i in range(len(pairs))]
    total = sum(steps)
    npairs = len(pairs)

    def kk(i):
        return lambda k: jnp.clip(k - offs[i], 0, steps[i] - 1)

    in_specs, operands = [], []
    for i, (a, b) in enumerate(pairs):
        kf = kk(i)
        if mode == "nn":
            in_specs.append(pl.BlockSpec((tm, tks[i]), lambda mi, ni, k, kf=kf: (mi, kf(k))))
            in_specs.append(pl.BlockSpec((tks[i], tn), lambda mi, ni, k, kf=kf: (kf(k), ni)))
        elif mode == "nt":
            in_specs.append(pl.BlockSpec((tm, tks[i]), lambda mi, ni, k, kf=kf: (mi, kf(k))))
            in_specs.append(pl.BlockSpec((tn, tks[i]), lambda mi, ni, k, kf=kf: (ni, kf(k))))
        else:
            in_specs.append(pl.BlockSpec((tks[i], tm), lambda mi, ni, k, kf=kf: (kf(k), mi)))
            in_specs.append(pl.BlockSpec((tks[i], tn), lambda mi, ni, k, kf=kf: (kf(k), ni)))
        operands += [a, b]
    dot = {"nn": _bdot, "nt": _bdot_nt, "tn": _bdot_tn}[mode]

    def body(*refs):
        out_ref = refs[2 * npairs]
        k = pl.program_id(2)
        if total == 1:
            out_ref[...] = dot(refs[0][...], refs[1][...])
            return

        @pl.when(k == 0)
        def _():
            out_ref[...] = dot(refs[0][...], refs[1][...])

        for i in range(npairs):
            first = 1 if i == 0 else offs[i]
            if offs[i] + steps[i] > first:
                @pl.when((k >= first) & (k < offs[i] + steps[i]))
                def _(i=i):
                    out_ref[...] += dot(refs[2 * i][...], refs[2 * i + 1][...])

    return pl.pallas_call(
        body, name=name, grid=(m // tm, n // tn, total), in_specs=in_specs,
        out_specs=pl.BlockSpec((tm, tn), lambda mi, ni, k: (mi, ni)),
        out_shape=jax.ShapeDtypeStruct((m, n), F32),
        compiler_params=_params(("parallel", "parallel", "arbitrary")),
    )(*operands)


def _rowwise(fn, rows, small, outs, accs, name):
    t = rows[0].shape[0]
    per_row = sum(r.shape[1] * r.dtype.itemsize for r in rows) + sum(c * jnp.dtype(d).itemsize for c, d in outs)
    tm = SUBLANES
    while tm * 2 <= 1024 and t % (tm * 2) == 0 and tm * 2 * per_row <= ROW_BUDGET_BYTES:
        tm *= 2
    if t % tm:
        tm = t
    nrow, nsmall, nout, nacc = len(rows), len(small), len(outs), len(accs)

    def body(*refs):
        rv = [r[...] for r in refs[:nrow]]
        sv = [r[...] for r in refs[nrow:nrow + nsmall]]
        o_refs = refs[nrow + nsmall:nrow + nsmall + nout]
        a_refs = refs[nrow + nsmall + nout:]
        ov, av = fn(rv, sv)
        for r, v in zip(o_refs, ov):
            r[...] = v.astype(r.dtype)
        if nacc:
            @pl.when(pl.program_id(0) == 0)
            def _():
                for r in a_refs:
                    r[...] = jnp.zeros_like(r)
            for r, v in zip(a_refs, av):
                r[...] += v

    in_specs = [pl.BlockSpec((tm, r.shape[1]), lambda i: (i, 0)) for r in rows]
    in_specs += [pl.BlockSpec(s.shape, lambda i: (0, 0)) for s in small]
    out_specs = [pl.BlockSpec((tm, c), lambda i: (i, 0)) for c, _ in outs]
    out_specs += [pl.BlockSpec((1, c), lambda i: (0, 0)) for c in accs]
    out_shape = [jax.ShapeDtypeStruct((t, c), d) for c, d in outs]
    out_shape += [jax.ShapeDtypeStruct((1, c), F32) for c in accs]
    res = pl.pallas_call(
        body, name=name, grid=(t // tm,), in_specs=in_specs, out_specs=out_specs, out_shape=out_shape,
        compiler_params=_params(("arbitrary",)),
    )(*rows, *small)
    return res[:nout], res[nout:]


def _colsum(v):
    return jnp.sum(v, axis=0, keepdims=True)


def _rowmean(v):
    return jnp.mean(v, axis=1, keepdims=True)


def _conv_tiles(x):
    t, c = x.shape
    tm = _tile(t, 256, SUBLANES)
    nt = t // tm
    r8 = tm // SUBLANES
    specs = [
        pl.BlockSpec((SUBLANES, c), lambda i: (jnp.maximum(i * r8 - 1, 0), 0)),
        pl.BlockSpec((tm, c), lambda i: (i, 0)),
        pl.BlockSpec((SUBLANES, c), lambda i: (jnp.minimum((i + 1) * r8, t // SUBLANES - 1), 0)),
    ]
    return tm, nt, specs


def _conv_taps(prev_ref, cur_ref, next_ref, tm, nt, left):
    i = pl.program_id(0)
    p = jnp.where(i > 0, prev_ref[...], 0.0)
    q = jnp.where(i < nt - 1, next_ref[...], 0.0)
    ext = jnp.concatenate([p, cur_ref[...], q], axis=0)
    n = tm + 2 * SUBLANES
    taps = []
    for k in range(CONV_W):
        s = (left - k) % n
        r = pltpu.roll(ext, s, axis=0) if s else ext
        taps.append(r[SUBLANES:SUBLANES + tm])
    return taps


def _conv(x, w, b, left, act, name, out_dtype=F32):
    t, c = x.shape
    tm, nt, specs = _conv_tiles(x)

    def body(prev_ref, cur_ref, next_ref, w_ref, b_ref, o_ref):
        taps = _conv_taps(prev_ref, cur_ref, next_ref, tm, nt, left)
        pre = b_ref[...] + sum(w_ref[k:k + 1, :] * taps[k] for k in range(CONV_W))
        o_ref[...] = (_silu(pre) if act else pre).astype(out_dtype)

    return pl.pallas_call(
        body, name=name, grid=(nt,),
        in_specs=specs + [pl.BlockSpec((CONV_W, c), lambda i: (0, 0)), pl.BlockSpec((1, c), lambda i: (0, 0))],
        out_specs=pl.BlockSpec((tm, c), lambda i: (i, 0)),
        out_shape=jax.ShapeDtypeStruct((t, c), out_dtype), compiler_params=_params(("arbitrary",)),
    )(x, x, x, w, b)


def _conv_bwd(x, dy, w, b, act, name):
    t, c = x.shape
    tm, nt, specs = _conv_tiles(x)

    def body(prev_ref, cur_ref, next_ref, dy_ref, w_ref, b_ref, dpre_ref, dw0, dw1, dw2, dw3, db_ref):
        taps = _conv_taps(prev_ref, cur_ref, next_ref, tm, nt, CONV_LEFT)
        if act:
            pre = b_ref[...] + sum(w_ref[k:k + 1, :] * taps[k] for k in range(CONV_W))
            dpre = dy_ref[...] * _dsilu(pre)
        else:
            dpre = dy_ref[...]
        dpre_ref[...] = dpre

        @pl.when(pl.program_id(0) == 0)
        def _():
            for r in (dw0, dw1, dw2, dw3, db_ref):
                r[...] = jnp.zeros_like(r)

        for k, r in enumerate((dw0, dw1, dw2, dw3)):
            r[...] += _colsum(dpre * taps[k])
        db_ref[...] += _colsum(dpre)

    row = pl.BlockSpec((1, c), lambda i: (0, 0))
    res = pl.pallas_call(
        body, name=name + "_pre", grid=(nt,),
        in_specs=specs + [pl.BlockSpec((tm, c), lambda i: (i, 0)), pl.BlockSpec((CONV_W, c), lambda i: (0, 0)), row],
        out_specs=[pl.BlockSpec((tm, c), lambda i: (i, 0))] + [row] * 5,
        out_shape=[jax.ShapeDtypeStruct((t, c), F32)] + [jax.ShapeDtypeStruct((1, c), F32)] * 5,
        compiler_params=_params(("arbitrary",)),
    )(x, x, x, dy, w, b)
    dpre, dw, db = res[0], jnp.concatenate(res[1:5], axis=0), res[5]
    dx = _conv(dpre, w[::-1], jnp.zeros_like(b), CONV_W - 1 - CONV_LEFT, False, name + "_dx", BF16)
    return dx, dw, db


def _ssd_dims(xs, bm):
    t, ws = xs.shape
    heads = ws // SSD_HEAD_DIM
    hpg = heads // SSD_GROUPS
    assert hpg % 2 == 0 and bm.shape[1] == SSD_GROUPS * SSD_STATE and t % SSD_CHUNK == 0
    return t, ws, heads, hpg, heads // 2, t // SSD_CHUNK


def _ssd_chunk_terms(dtr_ref, dtrt_ref, bias_ref, biast_ref, alog_ref, alogt_ref, reverse):
    q = SSD_CHUNK
    a_neg = -jnp.exp(alog_ref[...])
    dt = _softplus(dtr_ref[...] + bias_ref[...])
    dtt = _softplus(dtrt_ref[...] + biast_ref[...])
    a = dt * a_neg
    at = dtt * (-jnp.exp(alogt_ref[...]))
    ii = lax.broadcasted_iota(jnp.int32, (q, q), 0)
    jj = lax.broadcasted_iota(jnp.int32, (q, q), 1)
    mask = (ii <= jj) if reverse else (ii >= jj)
    tri = mask.astype(F32)
    trit = ((jj <= ii) if reverse else (jj >= ii)).astype(F32)
    before = (jj > ii) if reverse else (jj < ii)
    ac = _hdot(tri, a)
    act = _hdot(at, trit)
    tot = _colsum(a)
    return a_neg, dt, a, ac, act, tot, mask, trit, before


def _pair_terms(lo, h1, dt, ac, tot):
    q = SSD_CHUNK

    def colb(v, h):
        return jnp.broadcast_to(v[:, h:h + 1], (v.shape[0], LANES))

    def pairb(v):
        return jnp.where(lo[:v.shape[0]], colb(v, h1), colb(v, h1 + 1))

    dt_p = pairb(dt)
    ac_p = pairb(ac)
    eac_p = jnp.exp(ac_p)
    dte_p = jnp.exp(pairb(jnp.broadcast_to(tot, (q, tot.shape[1]))) - ac_p)
    etot_p = jnp.exp(pairb(tot))
    return dt_p, eac_p, dte_p, etot_p


def _decay_mats(mask, ac, act, cb, h1):
    q = SSD_CHUNK
    out = []
    for h in (h1, h1 + 1):
        diff = jnp.broadcast_to(ac[:, h:h + 1], (q, q)) - jnp.broadcast_to(act[h:h + 1, :], (q, q))
        lmat = jnp.exp(jnp.where(mask, diff, NEG_BIG))
        out.append((lmat, cb * lmat))
    return out


def _ssd_specs(t, ws, heads, npairs, nc, reverse):
    q, gn = SSD_CHUNK, SSD_GROUPS * SSD_STATE
    ci = (lambda k: nc - 1 - k) if reverse else (lambda k: k)
    small = lambda shape: pl.BlockSpec(shape, lambda k: (0,) * len(shape))
    seq = [
        pl.BlockSpec((q, ws), lambda k: (ci(k), 0)),
        pl.BlockSpec((q, gn), lambda k: (ci(k), 0)),
        pl.BlockSpec((q, gn), lambda k: (ci(k), 0)),
        pl.BlockSpec((q, heads), lambda k: (ci(k), 0)),
        pl.BlockSpec((heads, q), lambda k: (0, ci(k))),
        small((1, heads)), small((heads, 1)), small((1, heads)), small((heads, 1)),
    ]
    return ci, small, seq


def _ssd_fwd(xs, bm, cm, dtr, bias, alog, s0, reverse, name):
    t, ws, heads, hpg, npairs, nc = _ssd_dims(xs, bm)
    q, ns = SSD_CHUNK, SSD_STATE
    ci, small, seq = _ssd_specs(t, ws, heads, npairs, nc, reverse)

    def body(xs_ref, b_ref, c_ref, dtr_ref, dtrt_ref, bias_ref, biast_ref, alog_ref, alogt_ref, s0_ref,
             y_ref, se_ref, s_ref):
        @pl.when(pl.program_id(0) == 0)
        def _():
            s_ref[...] = s0_ref[...]

        _, dt, _, ac, act, tot, mask, _, _ = _ssd_chunk_terms(dtr_ref, dtrt_ref, bias_ref, biast_ref, alog_ref, alogt_ref, reverse)
        lo = lax.broadcasted_iota(jnp.int32, (q, LANES), 1) < SSD_HEAD_DIM
        for g in range(SSD_GROUPS):
            bg = b_ref[:, g * ns:(g + 1) * ns]
            cg = c_ref[:, g * ns:(g + 1) * ns]
            cb = _bdot_nt(cg, bg)
            bgt = bg.T
            for pq in range(hpg // 2):
                pp = g * (hpg // 2) + pq
                h1 = 2 * pp
                cs = slice(pp * LANES, (pp + 1) * LANES)
                (_, m1), (_, m2) = _decay_mats(mask, ac, act, cb, h1)
                dt_p, eac_p, dte_p, etot_p = _pair_terms(lo, h1, dt, ac, tot)
                xp = xs_ref[:, cs] * dt_p
                sp = s_ref[pp]
                se_ref[0, pp] = sp
                y_ref[:, cs] = jnp.where(lo, _bdot(m1, xp), _bdot(m2, xp)) + eac_p * _bdot(cg, sp)
                s_ref[pp] = etot_p * sp + _bdot(bgt, xp * dte_p)

    st = (npairs, ns, LANES)
    return pl.pallas_call(
        body, name=name, grid=(nc,), in_specs=seq + [small(st)],
        out_specs=[pl.BlockSpec((q, ws), lambda k: (ci(k), 0)),
                   pl.BlockSpec((1,) + st, lambda k: (ci(k), 0, 0, 0)), small(st)],
        out_shape=[jax.ShapeDtypeStruct((t, ws), F32), jax.ShapeDtypeStruct((nc,) + st, F32), jax.ShapeDtypeStruct(st, F32)],
        compiler_params=_params(("arbitrary",)),
    )(xs, bm, cm, dtr, dtr.T, bias, bias.T, alog, alog.T, s0)


def _ssd_bwd(xs, bm, cm, dtr, bias, alog, dy, s_enter, lam0, reverse, name):
    t, ws, heads, hpg, npairs, nc = _ssd_dims(xs, bm)
    q, ns, gn = SSD_CHUNK, SSD_STATE, SSD_GROUPS * SSD_STATE
    ci, small, seq = _ssd_specs(t, ws, heads, npairs, nc, not reverse)
    st = (npairs, ns, LANES)

    def body(xs_ref, b_ref, c_ref, dtr_ref, dtrt_ref, bias_ref, biast_ref, alog_ref, alogt_ref, dy_ref, se_ref, lam0_ref,
             dxs_ref, db_ref, dc_ref, ddtr_ref, da_ref, dbias_ref, lam_ref):
        @pl.when(pl.program_id(0) == 0)
        def _():
            lam_ref[...] = lam0_ref[...]
            da_ref[...] = jnp.zeros_like(da_ref)
            dbias_ref[...] = jnp.zeros_like(dbias_ref)

        a_neg, dt, _, ac, act, tot, mask, trit, before = _ssd_chunk_terms(dtr_ref, dtrt_ref, bias_ref, biast_ref, alog_ref, alogt_ref, reverse)
        lo = lax.broadcasted_iota(jnp.int32, (q, LANES), 1) < SSD_HEAD_DIM
        hl = lax.broadcasted_iota(jnp.int32, (1, heads), 1)

        def per_head(v, h1):
            s1 = jnp.sum(jnp.where(lo[:v.shape[0]], v, 0.0), axis=1, keepdims=True)
            s2 = jnp.sum(jnp.where(lo[:v.shape[0]], 0.0, v), axis=1, keepdims=True)
            return s1 * (hl == h1).astype(F32) + s2 * (hl == h1 + 1).astype(F32)

        da_tt = jnp.zeros((q, heads), F32)
        da_st = jnp.zeros((q, heads), F32)
        da_ts = jnp.zeros((q, heads), F32)
        da_ss = jnp.zeros((1, heads), F32)
        dxx = jnp.zeros((q, heads), F32)
        for g in range(SSD_GROUPS):
            gs = slice(g * ns, (g + 1) * ns)
            bg, cg = b_ref[:, gs], c_ref[:, gs]
            cb = _bdot_nt(cg, bg)
            cgt = cg.T
            dcb = jnp.zeros((q, q), F32)
            dbg = jnp.zeros((q, ns), F32)
            dcg = jnp.zeros((q, ns), F32)
            for pq in range(hpg // 2):
                pp = g * (hpg // 2) + pq
                h1 = 2 * pp
                cs = slice(pp * LANES, (pp + 1) * LANES)
                (l1, m1), (l2, m2) = _decay_mats(mask, ac, act, cb, h1)
                dt_p, eac_p, dte_p, etot_p = _pair_terms(lo, h1, dt, ac, tot)
                xsp = xs_ref[:, cs]
                xp = xsp * dt_p
                dyp = dy_ref[:, cs]
                lam = lam_ref[pp]
                sp = se_ref[0, pp]
                dx_state = dte_p * _bdot(bg, lam)
                dxdt = jnp.where(lo, _bdot(m1.T, dyp), _bdot(m2.T, dyp)) + dx_state
                dyx1 = _bdot_nt(jnp.where(lo, dyp, 0.0), xp)
                dyx2 = _bdot_nt(jnp.where(lo, 0.0, dyp), xp)
                dcb = dcb + l1 * dyx1 + l2 * dyx2
                for h, pair in ((h1, m1 * dyx1), (h1 + 1, m2 * dyx2)):
                    cross = jnp.sum(jnp.where(before, _bdot(trit, pair), 0.0), axis=1, keepdims=True)
                    da_tt = da_tt + cross * (hl == h).astype(F32)
                da_st = da_st + per_head(dyp * (eac_p * _bdot(cg, sp)), h1)
                da_ts = da_ts + per_head(dx_state * xp, h1)
                da_ss = da_ss + per_head(_colsum(lam * sp) * etot_p, h1)
                dxx = dxx + per_head(dxdt * xsp, h1)
                edy = eac_p * dyp
                xe = dte_p * xp
                dcg = dcg + _bdot_nt(edy, sp)
                dbg = dbg + _bdot_nt(xe, lam)
                lam_ref[pp] = etot_p * lam + _bdot(cgt, edy)
                dxs_ref[:, cs] = dxdt * dt_p
            db_ref[:, gs] = dbg + _bdot(dcb.T, cg)
            dc_ref[:, gs] = dcg + _bdot(dcb, bg)
        da = da_tt + _hdot(trit, da_st) + _hdot(before.astype(F32), da_ts) + da_ss
        ddtr = (da * a_neg + dxx) * _sigmoid(dtr_ref[...] + bias_ref[...])
        ddtr_ref[...] = ddtr
        da_ref[...] += _colsum(da * dt)
        dbias_ref[...] += _colsum(ddtr)

    row = lambda c: pl.BlockSpec((q, c), lambda k: (ci(k), 0))
    return pl.pallas_call(
        body, name=name, grid=(nc,),
        in_specs=seq + [row(ws), pl.BlockSpec((1,) + st, lambda k: (ci(k), 0, 0, 0)), small(st)],
        out_specs=[row(ws), row(gn), row(gn), row(heads), small((1, heads)), small((1, heads)), small(st)],
        out_shape=[jax.ShapeDtypeStruct((t, ws), F32), jax.ShapeDtypeStruct((t, gn), F32), jax.ShapeDtypeStruct((t, gn), F32),
                   jax.ShapeDtypeStruct((t, heads), F32), jax.ShapeDtypeStruct((1, heads), F32),
                   jax.ShapeDtypeStruct((1, heads), F32), jax.ShapeDtypeStruct(st, F32)],
        compiler_params=_params(("arbitrary",)),
    )(xs, bm, cm, dtr, dtr.T, bias, bias.T, alog, alog.T, dy, s_enter, lam0)


def _tile_scan(a_ref, u_ref, out_ref, carry_ref, ntiles, reverse):
    w = a_ref.shape[1]
    row = lax.broadcasted_iota(jnp.int32, (SUBLANES, w), 0)

    def step(j, _):
        r0 = pl.multiple_of((ntiles - 1 - j if reverse else j) * SUBLANES, SUBLANES)
        a = a_ref[pl.ds(r0, SUBLANES), :]
        u = u_ref[pl.ds(r0, SUBLANES), :]
        for d in (1, 2, 4):
            keep = (row < SUBLANES - d) if reverse else (row >= d)
            s = SUBLANES - d if reverse else d
            a_sh = jnp.where(keep, pltpu.roll(a, s, axis=0), 1.0)
            u_sh = jnp.where(keep, pltpu.roll(u, s, axis=0), 0.0)
            u = a * u_sh + u
            a = a * a_sh
        h = a * carry_ref[...] + u
        out_ref[pl.ds(r0, SUBLANES), :] = h
        last = h[0:1, :] if reverse else h[SUBLANES - 1:SUBLANES, :]
        carry_ref[...] = jnp.broadcast_to(last, (SUBLANES, w))
        return 0

    lax.fori_loop(0, ntiles, step, 0)


def _lru_gates(xh, wa, wx, ba, bx, lam):
    r = _sigmoid(_bdot(xh, wa) + ba)
    i = _sigmoid(_bdot(xh, wx) + bx)
    sp = _softplus(-lam)
    a = jnp.exp(-LRU_C * r * sp)
    s = jnp.sqrt(1.0 - a * a)
    return r, i, sp, a, s


def _lru_fwd(xc, wa, wx, ba, bx, lam, h0, addend, reverse, name):
    t, w = xc.shape
    tb = LRU_BLOCK
    nb, nh = t // tb, w // LRU_HEAD_DIM
    bi = (lambda k: nb - 1 - k) if reverse else (lambda k: k)
    has_add = addend is not None

    def body(*refs):
        xc_ref, wa_ref, wx_ref, ba_ref, bx_ref, lam_ref, h0_ref = refs[:7]
        add_ref = refs[7] if has_add else None
        h_ref, hl_ref = refs[7 + has_add], refs[8 + has_add]
        sum_ref = refs[9 + has_add] if has_add else None
        a_s, u_s, carry = refs[-3:]

        @pl.when(pl.program_id(0) == 0)
        def _():
            carry[...] = jnp.broadcast_to(h0_ref[...], carry.shape)

        for hd in range(nh):
            hs = slice(hd * LRU_HEAD_DIM, (hd + 1) * LRU_HEAD_DIM)
            xh = xc_ref[:, hs]
            _, i, _, a, s = _lru_gates(xh, wa_ref[hd], wx_ref[hd], ba_ref[:, hs], bx_ref[:, hs], lam_ref[:, hs])
            a_s[:, hs] = a
            u_s[:, hs] = s * (i * xh)
        _tile_scan(a_s, u_s, h_ref, carry, tb // SUBLANES, reverse)
        hl_ref[...] = carry[0:1, :]
        if has_add:
            sum_ref[...] = h_ref[...] + add_ref[...]

    blk = pl.BlockSpec((tb, w), lambda k: (bi(k), 0))
    vec = pl.BlockSpec((1, w), lambda k: (0, 0))
    wsp = pl.BlockSpec((nh, LRU_HEAD_DIM, LRU_HEAD_DIM), lambda k: (0, 0, 0))
    out_specs = [blk, vec] + ([blk] if has_add else [])
    out_shape = [jax.ShapeDtypeStruct((t, w), F32), jax.ShapeDtypeStruct((1, w), F32)] + ([jax.ShapeDtypeStruct((t, w), F32)] if has_add else [])
    return pl.pallas_call(
        body, name=name, grid=(nb,),
        in_specs=[blk, wsp, wsp, vec, vec, vec, vec] + ([blk] if has_add else []),
        out_specs=out_specs, out_shape=out_shape,
        scratch_shapes=[pltpu.VMEM((tb, w), F32), pltpu.VMEM((tb, w), F32), pltpu.VMEM((SUBLANES, w), F32)],
        compiler_params=_params(("arbitrary",)),
    )(xc, wa, wx, ba, bx, lam, h0, *([addend] if has_add else []))


def _lru_bwd(xc, wa, wx, ba, bx, lam, h0, h, dh, dfin, addend, reverse, name):
    t, w = xc.shape
    tb = LRU_BLOCK
    nb, nh = t // tb, w // LRU_HEAD_DIM
    r8 = tb // SUBLANES
    bi = (lambda k: k) if reverse else (lambda k: nb - 1 - k)
    if reverse:
        halo = lambda k: (jnp.minimum((bi(k) + 1) * r8, t // SUBLANES - 1), 0)
    else:
        halo = lambda k: (jnp.maximum(bi(k) * r8 - 1, 0), 0)
    has_add = addend is not None
    n_ext = tb + SUBLANES

    def body(*refs):
        xc_ref, wa_ref, wx_ref, ba_ref, bx_ref, lam_ref, h0_ref, h_ref, halo_ref, dh_ref, dfin_ref = refs[:11]
        add_ref = refs[11] if has_add else None
        dxc_ref, dwa_ref, dwx_ref, dba_ref, dbx_ref, dsp_ref, dh0_ref = refs[11 + has_add:18 + has_add]
        a_s, v_s, mu_s, carry = refs[-4:]
        k = pl.program_id(0)

        @pl.when(k == 0)
        def _():
            carry[...] = jnp.broadcast_to(dfin_ref[...], carry.shape)
            for r in (dwa_ref, dwx_ref, dba_ref, dbx_ref, dsp_ref):
                r[...] = jnp.zeros_like(r)

        for hd in range(nh):
            hs = slice(hd * LRU_HEAD_DIM, (hd + 1) * LRU_HEAD_DIM)
            _, _, _, a, _ = _lru_gates(xc_ref[:, hs], wa_ref[hd], wx_ref[hd], ba_ref[:, hs], bx_ref[:, hs], lam_ref[:, hs])
            a_s[:, hs] = a
            v_s[:, hs] = a * dh_ref[:, hs]
        mu_in = carry[...]
        _tile_scan(a_s, v_s, mu_s, carry, r8, not reverse)
        dh0_ref[...] = carry[0:1, :]
        first = (bi(k) == nb - 1) if reverse else (bi(k) == 0)
        edge = jnp.where(first, jnp.broadcast_to(h0_ref[...], (SUBLANES, w)), halo_ref[...])
        for hd in range(nh):
            hs = slice(hd * LRU_HEAD_DIM, (hd + 1) * LRU_HEAD_DIM)
            if reverse:
                mu_sh = pltpu.roll(jnp.concatenate([mu_in[:, hs], mu_s[:, hs]], axis=0), 1, axis=0)[SUBLANES:]
                h_sh = pltpu.roll(jnp.concatenate([h_ref[:, hs], edge[:, hs]], axis=0), n_ext - 1, axis=0)[:tb]
            else:
                mu_sh = pltpu.roll(jnp.concatenate([mu_s[:, hs], mu_in[:, hs]], axis=0), n_ext - 1, axis=0)[:tb]
                h_sh = pltpu.roll(jnp.concatenate([edge[:, hs], h_ref[:, hs]], axis=0), 1, axis=0)[SUBLANES:]
            xh = xc_ref[:, hs]
            r, i, sp, a, s = _lru_gates(xh, wa_ref[hd], wx_ref[hd], ba_ref[:, hs], bx_ref[:, hs], lam_ref[:, hs])
            lam_t = dh_ref[:, hs] + mu_sh
            dla = lam_t * h_sh * a - lam_t * (i * xh) * (a * a) / s
            dix = lam_t * s
            dpr = dla * (-LRU_C * sp) * r * (1.0 - r)
            dpi = dix * xh * i * (1.0 - i)
            dsp_ref[:, hs] += _colsum(dla * (-LRU_C * r))
            dba_ref[:, hs] += _colsum(dpr)
            dbx_ref[:, hs] += _colsum(dpi)
            dx = dix * i + _bdot_nt(dpr, wa_ref[hd]) + _bdot_nt(dpi, wx_ref[hd])
            xt = xh.T
            dwa_ref[hd] += _bdot(xt, dpr)
            dwx_ref[hd] += _bdot(xt, dpi)
            dxc_ref[:, hs] = dx + add_ref[:, hs] if has_add else dx

    blk = pl.BlockSpec((tb, w), lambda k: (bi(k), 0))
    vec = pl.BlockSpec((1, w), lambda k: (0, 0))
    wsp = pl.BlockSpec((nh, LRU_HEAD_DIM, LRU_HEAD_DIM), lambda k: (0, 0, 0))
    wshape = jax.ShapeDtypeStruct((nh, LRU_HEAD_DIM, LRU_HEAD_DIM), F32)
    vshape = jax.ShapeDtypeStruct((1, w), F32)
    return pl.pallas_call(
        body, name=name, grid=(nb,),
        in_specs=[blk, wsp, wsp, vec, vec, vec, vec, blk, pl.BlockSpec((SUBLANES, w), halo), blk, vec] + ([blk] if has_add else []),
        out_specs=[blk, wsp, wsp, vec, vec, vec, vec],
        out_shape=[jax.ShapeDtypeStruct((t, w), F32), wshape, wshape, vshape, vshape, vshape, vshape],
        scratch_shapes=[pltpu.VMEM((tb, w), F32), pltpu.VMEM((tb, w), F32), pltpu.VMEM((tb, w), F32), pltpu.VMEM((SUBLANES, w), F32)],
        compiler_params=_params(("arbitrary",)),
    )(xc, wa, wx, ba, bx, lam, h0, h, h, dh, dfin, *([addend] if has_add else []))


def _coords():
    return lax.axis_index("x"), lax.axis_index("y"), lax.axis_index("c")


HALF_ROWS_ALIGN = 32


def _all_gather(shards, name):
    n = len(shards)
    split = [s.ndim >= 2 and s.shape[0] % HALF_ROWS_ALIGN == 0 for s in shards]
    n_copies = 10

    def body(*refs):
        ins, outs = refs[:n], refs[n:2 * n]
        send_sems, recv_sems, local_sems = refs[2 * n:]
        x, y, c = _coords()
        slot = lambda p: 4 * p[0] + 2 * p[1] + p[2]
        me, xp, yp, dg = (x, y, c), (1 - x, y, c), (x, 1 - y, c), (1 - x, 1 - y, c)
        other_core = lambda p: (p[0], p[1], 1 - p[2])

        def rows(ref, i, half):
            r = shards[i].shape[0]
            if half == "all" or not split[i]:
                return ref
            return ref.at[pl.ds(0, r // 2)] if half == "a" else ref.at[pl.ds(r // 2, r // 2)]

        def copy(i, j, holder, half, to, own=False):
            dst = rows(outs[i].at[slot(holder)], i, half)
            src = rows(ins[i], i, half) if own else dst
            return pltpu.make_async_remote_copy(src_ref=src, dst_ref=dst, send_sem=send_sems.at[i, j],
                                                recv_sem=recv_sems.at[i, j], device_id=to, device_id_type=MESH)

        rounds = [
            ([(0, me, "a", xp, True), (1, me, "b", yp, True)],
             [(0, xp, "a"), (1, yp, "b")]),
            ([(2, me, "a", yp, True), (3, xp, "a", yp, False), (4, me, "b", xp, True), (5, yp, "b", xp, False)],
             [(2, yp, "a"), (3, dg, "a"), (4, xp, "b"), (5, dg, "b")]),
            ([(6, me, "all", other_core(me), True)] + [(7 + q, p, "all", other_core(me), False) for q, p in enumerate((xp, yp, dg))],
             [(6 + q, other_core(p), "all") for q, p in enumerate((me, xp, yp, dg))]),
        ]
        local = [pltpu.make_async_copy(ins[i], outs[i].at[slot(me)], local_sems.at[i]) for i in range(n)]
        for cp in local:
            cp.start()
        sent = []
        for sends, lands in rounds:
            for j, holder, half, to, own in sends:
                for i in range(n):
                    if half == "b" and not split[i]:
                        continue
                    cp = copy(i, j, holder, half, to, own)
                    cp.start()
                    sent.append(cp)
            for j, holder, half in lands:
                for i in range(n):
                    if half == "b" and not split[i]:
                        continue
                    copy(i, j, holder, half, me).wait_recv()
        for cp in sent:
            cp.wait_send()
        for cp in local:
            cp.wait()

    any_spec = pl.BlockSpec(memory_space=pl.ANY)
    return pl.pallas_call(
        body, name=name, in_specs=[any_spec] * n, out_specs=[any_spec] * n,
        out_shape=[jax.ShapeDtypeStruct((N_DEV,) + s.shape, s.dtype) for s in shards],
        scratch_shapes=[pltpu.SemaphoreType.DMA((n, n_copies)), pltpu.SemaphoreType.DMA((n, n_copies)), pltpu.SemaphoreType.DMA((n,))],
    )(*shards)


def _swap(bufs, axes, name):
    n = len(bufs)

    def body(*refs):
        ins, outs = refs[:n], refs[n:2 * n]
        send_sems, recv_sems = refs[2 * n:]
        x, y, c = _coords()
        to = {"x": (1 - x, y, c), "y": (x, 1 - y, c), "c": (x, y, 1 - c)}
        cps = [pltpu.make_async_remote_copy(src_ref=ins[i], dst_ref=outs[i], send_sem=send_sems.at[i], recv_sem=recv_sems.at[i],
                                            device_id=to[axes[i]], device_id_type=MESH) for i in range(n)]
        for cp in cps:
            cp.start()
        for cp in cps:
            cp.wait()

    any_spec = pl.BlockSpec(memory_space=pl.ANY)
    return pl.pallas_call(
        body, name=name, in_specs=[any_spec] * n, out_specs=[any_spec] * n,
        out_shape=[jax.ShapeDtypeStruct(b.shape, b.dtype) for b in bufs],
        scratch_shapes=[pltpu.SemaphoreType.DMA((n,)), pltpu.SemaphoreType.DMA((n,))],
    )(*bufs)


def _as2d(a):
    return a.reshape(-1, a.shape[-1])


def _add_f32(a, b, name):
    (out,), _ = _rowwise(lambda r, s: ([r[0].astype(F32) + r[1].astype(F32)], []), [_as2d(a), _as2d(b)], [],
                         [(a.shape[-1], F32)], [], name)
    return out.reshape(a.shape)


def _reduce_scatter(parts, name):
    x, y, c = _coords()
    n = len(parts)
    pick = lambda p, idx, dim: lax.dynamic_index_in_dim(p, idx, dim, keepdims=False)

    def exchange(bufs, idxs, dims, axes, tag):
        keep = [pick(p, i_, d_) for p, i_, d_ in zip(bufs, idxs, dims)]
        send = [pick(p, 1 - i_, d_).astype(BF16) for p, i_, d_ in zip(bufs, idxs, dims)]
        got = _swap(send, axes, f"{name}_swap_{tag}")
        return [_add_f32(k, g, f"{name}_add_{tag}_{i}") for i, (k, g) in enumerate(zip(keep, got))]

    cur = exchange([p.reshape((2, 2, 2) + p.shape[1:]) for p in parts], [c] * n, [2] * n, ["c"] * n, "c")
    halves = [p[:, :, :p.shape[2] // 2] for p in cur] + [p[:, :, p.shape[2] // 2:] for p in cur]
    cur = exchange(halves, [y] * n + [x] * n, [1] * n + [0] * n, ["y"] * n + ["x"] * n, "yx")
    cur = exchange(cur, [x] * n + [y] * n, [0] * (2 * n), ["x"] * n + ["y"] * n, "xy")
    return [jnp.concatenate([cur[i], cur[n + i]], axis=0) for i in range(n)]


def _pack(arrs):
    flat = jnp.concatenate([a.reshape(-1).astype(F32) for a in arrs])
    rows = -(-flat.shape[0] // LANES)
    rows = -(-rows // 512) * 512
    return jnp.pad(flat, (0, rows * LANES - flat.shape[0])).reshape(rows, LANES)


def _unpack(buf, like):
    flat, out, off = buf.reshape(-1), [], 0
    for a in like:
        out.append(flat[off:off + a.size].reshape(a.shape))
        off += a.size
    return out


def _all_reduce(arrs, name):
    (g,) = _all_gather([_pack(arrs)], name + "_gather")
    (tot,), _ = _rowwise(lambda r, s: ([functools.reduce(lambda u, v: u + v, r)], []), [g[i] for i in range(N_DEV)], [],
                         [(LANES, F32)], [], name + "_sum")
    return _unpack(tot, arrs)


def _adamw(w, g, m, v, name):
    def fn(r, s):
        w_, g_, m_, v_ = r
        m_ = ADAM_B1 * m_ + (1.0 - ADAM_B1) * g_
        v_ = ADAM_B2 * v_ + (1.0 - ADAM_B2) * jnp.square(g_)
        m_hat = m_ / (1.0 - ADAM_B1 ** ADAM_STEP)
        v_hat = v_ / (1.0 - ADAM_B2 ** ADAM_STEP)
        return [-ADAM_LR * (m_hat / (jnp.sqrt(v_hat) + ADAM_EPS) + ADAM_WD * w_), m_, v_], []

    c = w.shape[-1]
    (d, nm, nv), _ = _rowwise(fn, [_as2d(a) for a in (w, g, m, v)], [], [(c, F32)] * 3, [], name)
    return d.reshape(w.shape), nm.reshape(w.shape), nv.reshape(w.shape)


def _norm_mod(x, gain, sc, sh, name):
    def fn(r, s):
        xv = r[0]
        xn = xv * lax.rsqrt(_rowmean(xv * xv) + EPS)
        return [xn * s[0] * (1.0 + s[1]) + s[2]], []
    (h,), _ = _rowwise(fn, [x], [gain, sc, sh], [(x.shape[1], BF16)], [], name)
    return h


def _norm_mod_bwd(x, dh, dres, gain, sc, name):
    d = x.shape[1]

    def fn(r, s):
        xv, dhv = r[0], r[1]
        rs = lax.rsqrt(_rowmean(xv * xv) + EPS)
        xn = xv * rs
        dxn = dhv * (1.0 + s[1]) * s[0]
        dx = rs * (dxn - xn * _rowmean(dxn * xn))
        if dres is not None:
            dx = dx + r[2]
        return [dx], [_colsum(dhv), _colsum(dhv * xn * s[0]), _colsum(dhv * (1.0 + s[1]) * xn)]
    (dx,), accs = _rowwise(fn, [x, dh] + ([dres] if dres is not None else []), [gain, sc], [(d, F32)], [d, d, d], name)
    return dx, accs


def _ada_rows(mod_row, d):
    return [mod_row[:, i * d:(i + 1) * d] for i in range(N_MOD)]


def _to_col_major(t):
    n, c = t.shape
    return t.reshape(n // GRID_W, GRID_W, c).transpose(1, 0, 2).reshape(n, c)


def _to_row_major(t):
    n, c = t.shape
    return t.reshape(GRID_W, n // GRID_W, c).transpose(1, 0, 2).reshape(n, c)


def kernel(x, c, ctx, c_ctx, w_ada, b_ada, norm_mix, norm_ffn, w_in, ssd_conv_w, ssd_conv_b, ssd_dt_bias, ssd_a_log, ssd_d, ssd_norm, w_out_ssd, lru_conv_w, lru_conv_b, lru_w_a, lru_b_a, lru_w_x, lru_b_x, lru_lambda, w_out_lru, w_gate, b_gate, w_o, ffn_w13, ffn_w2, final_norm, loss_target, m_c_ctx, m_w_ada, m_b_ada, m_norm_mix, m_norm_ffn, m_w_in, m_ssd_conv_w, m_ssd_conv_b, m_ssd_dt_bias, m_ssd_a_log, m_ssd_d, m_ssd_norm, m_w_out_ssd, m_lru_conv_w, m_lru_conv_b, m_lru_w_a, m_lru_b_a, m_lru_w_x, m_lru_b_x, m_lru_lambda, m_w_out_lru, m_w_gate, m_b_gate, m_w_o, m_ffn_w13, m_ffn_w2, m_final_norm, v_c_ctx, v_w_ada, v_b_ada, v_norm_mix, v_norm_ffn, v_w_in, v_ssd_conv_w, v_ssd_conv_b, v_ssd_dt_bias, v_ssd_a_log, v_ssd_d, v_ssd_norm, v_w_out_ssd, v_lru_conv_w, v_lru_conv_b, v_lru_w_a, v_lru_b_a, v_lru_w_x, v_lru_b_x, v_lru_lambda, v_w_out_lru, v_w_gate, v_b_gate, v_w_o, v_ffn_w13, v_ffn_w2, v_final_norm):
    names = ["c_ctx", "w_ada", "b_ada", "norm_mix", "norm_ffn", "w_in", "ssd_conv_w", "ssd_conv_b", "ssd_dt_bias", "ssd_a_log",
             "ssd_d", "ssd_norm", "w_out_ssd", "lru_conv_w", "lru_conv_b", "lru_w_a", "lru_b_a", "lru_w_x", "lru_b_x", "lru_lambda",
             "w_out_lru", "w_gate", "b_gate", "w_o", "ffn_w13", "ffn_w2", "final_norm"]
    wts = dict(zip(names, (c_ctx, w_ada, b_ada, norm_mix, norm_ffn, w_in, ssd_conv_w, ssd_conv_b, ssd_dt_bias, ssd_a_log, ssd_d, ssd_norm, w_out_ssd, lru_conv_w, lru_conv_b, lru_w_a, lru_b_a, lru_w_x, lru_b_x, lru_lambda, w_out_lru, w_gate, b_gate, w_o, ffn_w13, ffn_w2, final_norm)))
    mom1 = dict(zip(names, (m_c_ctx, m_w_ada, m_b_ada, m_norm_mix, m_norm_ffn, m_w_in, m_ssd_conv_w, m_ssd_conv_b, m_ssd_dt_bias, m_ssd_a_log, m_ssd_d, m_ssd_norm, m_w_out_ssd, m_lru_conv_w, m_lru_conv_b, m_lru_w_a, m_lru_b_a, m_lru_w_x, m_lru_b_x, m_lru_lambda, m_w_out_lru, m_w_gate, m_b_gate, m_w_o, m_ffn_w13, m_ffn_w2, m_final_norm)))
    mom2 = dict(zip(names, (v_c_ctx, v_w_ada, v_b_ada, v_norm_mix, v_norm_ffn, v_w_in, v_ssd_conv_w, v_ssd_conv_b, v_ssd_dt_bias, v_ssd_a_log, v_ssd_d, v_ssd_norm, v_w_out_ssd, v_lru_conv_w, v_lru_conv_b, v_lru_w_a, v_lru_b_a, v_lru_w_x, v_lru_b_x, v_lru_lambda, v_w_out_lru, v_w_gate, v_b_gate, v_w_o, v_ffn_w13, v_ffn_w2, v_final_norm)))

    xi, yi, ci_ = _coords()
    me = 4 * xi + 2 * yi + ci_
    x2, tgt, ctx2 = x[0], loss_target[0], ctx[0]
    t, d = x2.shape
    tc = ctx2.shape[0]
    ws = N_DEV * w_out_ssd.shape[1]
    wl = N_DEV * w_out_lru.shape[1]
    heads = ws // SSD_HEAD_DIM
    gn = SSD_GROUPS * SSD_STATE
    nh = wl // LRU_HEAD_DIM
    ff = N_DEV * ffn_w2.shape[1]
    ada_cols = w_ada.shape[2]

    big = [w_in[0], w_out_ssd[0], w_out_lru[0], w_gate[0], w_o[0], ffn_w13[0], ffn_w2[0]]
    smalls = [c, ssd_conv_w[0], lru_conv_w[0], lru_b_a[0], lru_b_x[0], lru_lambda[0]]
    got = _all_gather(smalls + [b.astype(BF16) for b in big], "gather_weights")
    c_all = got[0].reshape(N_DEV, d)
    colcat = lambda g: jnp.moveaxis(g, 0, -2).reshape(g.shape[1:-1] + (N_DEV * g.shape[-1],))
    conv_w_s, conv_w_l, lru_ba, lru_bx, lru_lam = (colcat(g) for g in got[1:6])
    wg_in, wg_os, wg_ol, wg_gate, wg_o, wg_13, wg_2 = got[6:]
    w_in_f = colcat(wg_in)
    o_z, o_xbc, o_dt, o_xr, o_yr = 0, ws, 2 * ws + 2 * gn, 2 * ws + 2 * gn + 2 * heads, 2 * ws + 2 * gn + 2 * heads + wl
    w_z, w_xbc = w_in_f[:, o_z:o_xbc], w_in_f[:, o_xbc:o_dt]
    w_dt = jnp.pad(w_in_f[:, o_dt:o_xr], ((0, 0), (0, LANES - 2 * heads)))
    w_xr, w_yr = w_in_f[:, o_xr:o_yr], w_in_f[:, o_yr:]
    w_os, w_ol, w_oo, w_2 = wg_os.reshape(ws, d), wg_ol.reshape(wl, d), wg_o.reshape(d, d), wg_2.reshape(ff, d)
    w_gf = colcat(wg_gate)
    w_gs, w_gr = w_gf[:, :d], w_gf[:, d:]
    w_13f = colcat(wg_13)
    w_1, w_3 = w_13f[:, :ff], w_13f[:, ff:]

    cond = jnp.concatenate([c_all, c_ctx[None, :], jnp.zeros((16 - N_DEV - 1, d), F32)], axis=0)
    cond_act = _silu(cond)
    mod_l = _mm([(cond_act, w_ada[0])], "nn", F32, "ada_fwd")
    (mod_g,) = _all_gather([mod_l], "gather_mod")
    mod_full = colcat(mod_g) + b_ada
    sh_m, sc_m, g_m, sh_f, sc_f, g_f = _ada_rows(lax.dynamic_slice_in_dim(mod_full, me, 1, 0), d)
    csh_m, csc_m = _ada_rows(mod_full[N_DEV:N_DEV + 1], d)[:2]

    bias_s, alog_s = ssd_dt_bias[0], ssd_a_log[0]
    d_full = jnp.repeat(ssd_d[0], SSD_HEAD_DIM)[None, :]
    zeros_state = jnp.zeros((heads // 2, SSD_STATE, LANES), F32)
    zeros_l = jnp.zeros((1, wl), F32)
    wa, wx = lru_w_a[0], lru_w_x[0]

    def mixer_inputs(hh, tag, full):
        xbc = _mm([(hh, w_xbc)], "nn", F32, f"{tag}_proj_xbc")
        dtr = _mm([(hh, w_dt)], "nn", F32, f"{tag}_proj_dt")[:, :2 * heads]
        xr = _mm([(hh, w_xr)], "nn", F32, f"{tag}_proj_xr")
        cx = _conv(xbc, conv_w_s, ssd_conv_b, CONV_LEFT, True, f"{tag}_ssd_conv")
        xs_, bm_, cm_ = cx[:, :ws], cx[:, ws:ws + gn], cx[:, ws + gn:]
        xr_seq = _to_col_major(xr) if full else xr
        xc = _conv(xr_seq, conv_w_l, lru_conv_b, CONV_LEFT, False, f"{tag}_lru_conv")
        return xbc, dtr, xr_seq, xs_, bm_, cm_, xc

    def lru_args(dirn):
        return wa[dirn], wx[dirn], lru_ba[dirn:dirn + 1], lru_bx[dirn:dirn + 1], lru_lam[dirn:dirn + 1]

    hc = _norm_mod(ctx2, norm_mix, csc_m, csh_m, "ctx_norm")
    c_xbc, c_dtr, c_xr_seq, c_xs, c_bm, c_cm, c_xc = mixer_inputs(hc, "ctx", False)
    c_ssd = []
    for dirn in (0, 1):
        _, se, sf = _ssd_fwd(c_xs, c_bm, c_cm, c_dtr[:, dirn * heads:(dirn + 1) * heads], bias_s[dirn:dirn + 1], alog_s[dirn:dirn + 1],
                             zeros_state, bool(dirn), f"ctx_ssd_fwd{dirn}")
        c_ssd.append((se, sf))
    c_lru = []
    for dirn in (0, 1):
        res = _lru_fwd(c_xc, *lru_args(dirn), zeros_l, None, bool(dirn), f"ctx_lru_fwd{dirn}")
        c_lru.append((res[0], res[1]))

    h = _norm_mod(x2, norm_mix, sc_m, sh_m, "lat_norm")
    z = _mm([(h, w_z)], "nn", F32, "lat_proj_z")
    yr = _mm([(h, w_yr)], "nn", F32, "lat_proj_yr")
    gp_s = _mm([(h, w_gs)], "nn", F32, "lat_proj_gs")
    gp_r = _mm([(h, w_gr)], "nn", F32, "lat_proj_gr")
    l_xbc, l_dtr, l_xr_seq, l_xs, l_bm, l_cm, l_xc = mixer_inputs(h, "lat", True)
    l_ssd = []
    for dirn in (0, 1):
        yv, se, _ = _ssd_fwd(l_xs, l_bm, l_cm, l_dtr[:, dirn * heads:(dirn + 1) * heads], bias_s[dirn:dirn + 1], alog_s[dirn:dirn + 1],
                             c_ssd[dirn][1], bool(dirn), f"lat_ssd_fwd{dirn}")
        l_ssd.append((yv, se))

    def ssd_out(r, s):
        yf, yb, xs_, z_ = r
        yv = yf + yb + s[0] * xs_
        yz = yv * _silu(z_)
        return [yz * lax.rsqrt(_rowmean(yz * yz) + EPS) * s[1]], []
    (ys,), _ = _rowwise(ssd_out, [l_ssd[0][0], l_ssd[1][0], l_xs, z], [d_full, ssd_norm], [(ws, BF16)], [], "lat_ssd_out")
    o_s = _mm([(ys, w_os)], "nn", F32, "lat_out_ssd")

    hf, _ = _lru_fwd(l_xc, *lru_args(0), c_lru[0][1], None, False, "lat_lru_fwd0")
    hb, _, r_seq = _lru_fwd(l_xc, *lru_args(1), c_lru[1][1], hf, True, "lat_lru_fwd1")
    r_out = _to_row_major(r_seq)
    (gr,), _ = _rowwise(lambda r, s: ([r[0] * _gelu(r[1])], []), [r_out, yr], [], [(wl, BF16)], [], "lat_lru_out")
    o_r = _mm([(gr, w_ol)], "nn", F32, "lat_out_lru")

    bg_s, bg_r = b_gate[:, :d], b_gate[:, d:]
    (mix,), _ = _rowwise(lambda r, s: ([_sigmoid(r[0] + s[0]) * r[2] + _sigmoid(r[1] + s[1]) * r[3]], []),
                         [gp_s, gp_r, o_s, o_r], [bg_s, bg_r], [(d, BF16)], [], "lat_merge")
    mixo = _mm([(mix, w_oo)], "nn", F32, "lat_out_o")

    def resid_norm(r, s):
        x1v = r[0] + s[0] * r[1]
        xn = x1v * lax.rsqrt(_rowmean(x1v * x1v) + EPS)
        return [x1v, xn * s[1] * (1.0 + s[2]) + s[3]], []
    (x1, h2), _ = _rowwise(resid_norm, [x2, mixo], [g_m, norm_ffn, sc_f, sh_f], [(d, F32), (d, BF16)], [], "lat_resid1")
    fg = _mm([(h2, w_1)], "nn", F32, "ffn_gate")
    fu = _mm([(h2, w_3)], "nn", F32, "ffn_up")
    (act,), _ = _rowwise(lambda r, s: ([_silu(r[0]) * r[1]], []), [fg, fu], [], [(ff, BF16)], [], "ffn_act")
    fo = _mm([(act, w_2)], "nn", F32, "ffn_down")

    fn_gain = final_norm[None, :]

    def head(r, s):
        x1v, fv, tv = r
        xv = x1v + s[0] * fv
        rs = lax.rsqrt(_rowmean(xv * xv) + EPS)
        xn = xv * rs
        err = xn * s[1] - tv
        dyv = err * (1.0 / d)
        dxn = dyv * s[1]
        dxv = rs * (dxn - xn * _rowmean(dxn * xn))
        return [dxv, dxv * s[0]], [_colsum(err * err), _colsum(dyv * xn), _colsum(dxv * fv)]
    (dx2, dfo), (sq, g_final, dg_f) = _rowwise(head, [x1, fo, tgt], [g_f, fn_gain], [(d, F32), (d, BF16)], [d, d, d], "loss_head")
    loss = lax.psum(0.5 * jnp.sum(sq) / d, ("x", "y", "c"))

    dact = _mm([(dfo, w_2)], "nt", F32, "ffn_down_dx")
    gw_2 = _mm([(act, dfo)], "tn", F32, "ffn_down_dw")
    (dfg, dfu), _ = _rowwise(lambda r, s: ([r[0] * r[2] * _dsilu(r[1]), r[0] * _silu(r[1])], []), [dact, fg, fu], [],
                             [(ff, BF16), (ff, BF16)], [], "ffn_act_bwd")
    gw_1 = _mm([(h2, dfg)], "tn", F32, "ffn_gate_dw")
    gw_3 = _mm([(h2, dfu)], "tn", F32, "ffn_up_dw")
    dh2 = _mm([(dfg, w_1), (dfu, w_3)], "nt", F32, "ffn_dx")
    dx1, (dsh_f, dsc_f, g_norm_ffn) = _norm_mod_bwd(x1, dh2, dx2, norm_ffn, sc_f, "lat_norm_ffn_bwd")

    (dmixo,), (dg_m,) = _rowwise(lambda r, s: ([s[0] * r[0]], [_colsum(r[0] * r[1])]), [dx1, mixo], [g_m], [(d, BF16)], [d], "lat_resid1_bwd")
    dmix = _mm([(dmixo, w_oo)], "nt", F32, "lat_out_o_dx")
    gw_o = _mm([(mix, dmixo)], "tn", F32, "lat_out_o_dw")

    def merge_bwd(r, s):
        dm, gps, gpr, os_, or_ = r
        g1, g2 = _sigmoid(gps + s[0]), _sigmoid(gpr + s[1])
        dg1, dg2 = dm * os_ * g1 * (1.0 - g1), dm * or_ * g2 * (1.0 - g2)
        return [dm * g1, dm * g2, dg1, dg2], [_colsum(dg1), _colsum(dg2)]
    (do_s, do_r, dgp_s, dgp_r), (gb_gs, gb_gr) = _rowwise(merge_bwd, [dmix, gp_s, gp_r, o_s, o_r], [bg_s, bg_r],
                                                            [(d, BF16)] * 4, [d, d], "lat_merge_bwd")
    gw_gs = _mm([(h, dgp_s)], "tn", F32, "lat_gs_dw")
    gw_gr = _mm([(h, dgp_r)], "tn", F32, "lat_gr_dw")

    dgr = _mm([(do_r, w_ol)], "nt", F32, "lat_out_lru_dx")
    gw_ol = _mm([(gr, do_r)], "tn", F32, "lat_out_lru_dw")
    (dr_out, dyr), _ = _rowwise(lambda r, s: ([r[0] * _gelu(r[2]), r[0] * r[1] * _dgelu(r[2])], []), [dgr, r_out, yr], [],
                                [(wl, F32), (wl, BF16)], [], "lat_lru_out_bwd")
    dr_seq = _to_col_major(dr_out)
    lb0 = _lru_bwd(l_xc, *lru_args(0), c_lru[0][1], hf, dr_seq, zeros_l, None, False, "lat_lru_bwd0")
    lb1 = _lru_bwd(l_xc, *lru_args(1), c_lru[1][1], hb, dr_seq, zeros_l, lb0[0], True, "lat_lru_bwd1")
    dxr_seq, g_lconv_w, g_lconv_b = _conv_bwd(l_xr_seq, lb1[0], conv_w_l, lru_conv_b, False, "lat_lru_conv_bwd")
    dxr = _to_row_major(dxr_seq)

    dys = _mm([(do_s, w_os)], "nt", F32, "lat_out_ssd_dx")
    gw_os = _mm([(ys, do_s)], "tn", F32, "lat_out_ssd_dw")

    def ssd_out_bwd(r, s):
        yf, yb, xs_, z_, dyn = r
        yv = yf + yb + s[0] * xs_
        sz = _silu(z_)
        yz = yv * sz
        rs = lax.rsqrt(_rowmean(yz * yz) + EPS)
        yzn = yz * rs
        dyzn = dyn * s[1]
        dyz = rs * (dyzn - yzn * _rowmean(dyzn * yzn))
        dyv = dyz * sz
        return [dyv, dyz * yv * _dsilu(z_), dyv * s[0]], [_colsum(dyv * xs_), _colsum(dyn * yzn)]
    (dy_ssd, dz, dxs_skip), (g_dfull, g_ssd_norm) = _rowwise(ssd_out_bwd, [l_ssd[0][0], l_ssd[1][0], l_xs, z, dys], [d_full, ssd_norm],
                                                              [(ws, F32), (ws, BF16), (ws, F32)], [ws, ws], "lat_ssd_out_bwd")
    sb = [_ssd_bwd(l_xs, l_bm, l_cm, l_dtr[:, dirn * heads:(dirn + 1) * heads], bias_s[dirn:dirn + 1], alog_s[dirn:dirn + 1],
                   dy_ssd, l_ssd[dirn][1], zeros_state, bool(dirn), f"lat_ssd_bwd{dirn}") for dirn in (0, 1)]
    (dcx,), _ = _rowwise(lambda r, s: ([jnp.concatenate([r[0] + r[1] + r[2], r[3] + r[4], r[5] + r[6]], axis=1)], []),
                         [dxs_skip, sb[0][0], sb[1][0], sb[0][1], sb[1][1], sb[0][2], sb[1][2]], [], [(ws + 2 * gn, F32)], [], "lat_ssd_dsum")
    dxbc, g_sconv_w, g_sconv_b = _conv_bwd(l_xbc, dcx, conv_w_s, ssd_conv_b, True, "lat_ssd_conv_bwd")
    ddt = jnp.pad(jnp.concatenate([sb[0][3], sb[1][3]], axis=1), ((0, 0), (0, LANES - 2 * heads)))

    zeros_cy = jnp.zeros((tc, ws), F32)
    csb = [_ssd_bwd(c_xs, c_bm, c_cm, c_dtr[:, dirn * heads:(dirn + 1) * heads], bias_s[dirn:dirn + 1], alog_s[dirn:dirn + 1],
                    zeros_cy, c_ssd[dirn][0], sb[dirn][6], bool(dirn), f"ctx_ssd_bwd{dirn}") for dirn in (0, 1)]
    (c_dcx,), _ = _rowwise(lambda r, s: ([jnp.concatenate([r[0] + r[1], r[2] + r[3], r[4] + r[5]], axis=1)], []),
                           [csb[0][0], csb[1][0], csb[0][1], csb[1][1], csb[0][2], csb[1][2]], [], [(ws + 2 * gn, F32)], [], "ctx_ssd_dsum")
    c_dxbc, cg_sconv_w, cg_sconv_b = _conv_bwd(c_xbc, c_dcx, conv_w_s, ssd_conv_b, True, "ctx_ssd_conv_bwd")
    c_ddt = jnp.pad(jnp.concatenate([csb[0][3], csb[1][3]], axis=1), ((0, 0), (0, LANES - 2 * heads)))
    zeros_ch = jnp.zeros((tc, wl), F32)
    clb0 = _lru_bwd(c_xc, *lru_args(0), zeros_l, c_lru[0][0], zeros_ch, lb0[6], None, False, "ctx_lru_bwd0")
    clb1 = _lru_bwd(c_xc, *lru_args(1), zeros_l, c_lru[1][0], zeros_ch, lb1[6], clb0[0], True, "ctx_lru_bwd1")
    c_dxr, cg_lconv_w, cg_lconv_b = _conv_bwd(c_xr_seq, clb1[0], conv_w_l, lru_conv_b, False, "ctx_lru_conv_bwd")
    dhc = _mm([(c_dxbc, w_xbc), (c_ddt, w_dt), (c_dxr, w_xr)], "nt", F32, "ctx_proj_dx")
    _, (dcsh_m, dcsc_m, cg_norm_mix) = _norm_mod_bwd(ctx2, dhc, None, norm_mix, csc_m, "ctx_norm_bwd")

    gw_z = _mm([(h, dz)], "tn", F32, "lat_z_dw")
    gw_yr = _mm([(h, dyr)], "tn", F32, "lat_yr_dw")
    gw_xbc = _mm([(h, dxbc), (hc, c_dxbc)], "tn", F32, "proj_xbc_dw")
    gw_dt = _mm([(h, ddt), (hc, c_ddt)], "tn", F32, "proj_dt_dw")[:, :2 * heads]
    gw_xr = _mm([(h, dxr), (hc, c_dxr)], "tn", F32, "proj_xr_dw")
    dh = _mm([(dz, w_z), (dxbc, w_xbc), (ddt, w_dt), (dxr, w_xr), (dyr, w_yr), (dgp_s, w_gs), (dgp_r, w_gr)], "nt", F32, "lat_proj_dx")
    grad_x, (dsh_m, dsc_m, g_norm_mix) = _norm_mod_bwd(x2, dh, dx1, norm_mix, sc_m, "lat_norm_bwd")

    zero_d = jnp.zeros((1, d), F32)
    dmod_rows = jnp.concatenate([jnp.concatenate([dsh_m, dsc_m, dg_m, dsh_f, dsc_f, dg_f], axis=1),
                                 jnp.concatenate([dcsh_m, dcsc_m, zero_d, zero_d, zero_d, zero_d], axis=1)], axis=0)
    (dmod_g,) = _all_gather([dmod_rows], "gather_dmod")
    dmod_c = functools.reduce(lambda u, v: u + v, [dmod_g[i, 1] for i in range(N_DEV)])
    dmod9 = jnp.concatenate([dmod_g[:, 0], dmod_c[None, :], jnp.zeros((16 - N_DEV - 1, N_MOD * d), F32)], axis=0)
    g_b_ada = jnp.sum(dmod9, axis=0, keepdims=True)
    dmod_l = lax.dynamic_slice_in_dim(dmod9, me * ada_cols, ada_cols, 1)
    g_w_ada = _mm([(cond_act, dmod_l)], "tn", F32, "ada_dw")
    dcond_l = _mm([(dmod_l, w_ada[0])], "nt", F32, "ada_dx")

    small_parts = [
        dcond_l[N_DEV:N_DEV + 1], g_norm_mix + cg_norm_mix, g_norm_ffn, g_sconv_w + cg_sconv_w, g_sconv_b + cg_sconv_b,
        jnp.concatenate([sb[0][5] + csb[0][5], sb[1][5] + csb[1][5]], axis=0),
        jnp.concatenate([sb[0][4] + csb[0][4], sb[1][4] + csb[1][4]], axis=0),
        g_dfull, g_ssd_norm, g_lconv_w + cg_lconv_w, g_lconv_b + cg_lconv_b,
        jnp.stack([lb0[1] + clb0[1], lb1[1] + clb1[1]]), jnp.concatenate([lb0[3] + clb0[3], lb1[3] + clb1[3]], axis=0),
        jnp.stack([lb0[2] + clb0[2], lb1[2] + clb1[2]]), jnp.concatenate([lb0[4] + clb0[4], lb1[4] + clb1[4]], axis=0),
        jnp.concatenate([lb0[5] + clb0[5], lb1[5] + clb1[5]], axis=0),
        jnp.concatenate([gb_gs, gb_gr], axis=1), g_final,
    ]
    (s_cctx, s_nmix, s_nffn, s_sconv_w, s_sconv_b, s_dtb, s_da, s_dfull, s_snorm, s_lconv_w, s_lconv_b,
     s_wa, s_ba, s_wx, s_bx, s_dsp, s_bgate, s_final) = _all_reduce(small_parts, "small_grads")
    shard = lambda a, n_: lax.dynamic_slice_in_dim(a, me * n_, n_, a.ndim - 1)
    grads = {
        "c_ctx": (s_cctx * _dsilu(c_ctx[None, :]))[0],
        "w_ada": g_w_ada[None], "b_ada": g_b_ada, "norm_mix": s_nmix, "norm_ffn": s_nffn,
        "ssd_conv_w": shard(s_sconv_w, ssd_conv_w.shape[2])[None], "ssd_conv_b": s_sconv_b,
        "ssd_dt_bias": s_dtb[None], "ssd_a_log": (s_da * (-jnp.exp(alog_s)))[None],
        "ssd_d": jnp.sum(s_dfull.reshape(heads, SSD_HEAD_DIM), axis=1)[None], "ssd_norm": s_snorm,
        "lru_conv_w": shard(s_lconv_w, lru_conv_w.shape[2])[None], "lru_conv_b": s_lconv_b,
        "lru_w_a": s_wa[None], "lru_b_a": shard(s_ba, lru_b_a.shape[2])[None],
        "lru_w_x": s_wx[None], "lru_b_x": shard(s_bx, lru_b_x.shape[2])[None],
        "lru_lambda": shard(s_dsp * (-_sigmoid(-lru_lam)), lru_lambda.shape[2])[None],
        "b_gate": s_bgate, "final_norm": s_final[0],
    }

    slots = lambda g: jnp.moveaxis(g.reshape(g.shape[0], N_DEV, g.shape[1] // N_DEV), 1, 0)
    rslots = lambda g: g.reshape(N_DEV, g.shape[0] // N_DEV, g.shape[1])
    gw_in = jnp.concatenate([gw_z, gw_xbc, gw_dt, gw_xr, gw_yr], axis=1)
    parts = [slots(gw_in), rslots(gw_os), rslots(gw_ol), slots(jnp.concatenate([gw_gs, gw_gr], axis=1)), rslots(gw_o),
             slots(jnp.concatenate([gw_1, gw_3], axis=1)), rslots(gw_2)]
    for n_, g in zip(["w_in", "w_out_ssd", "w_out_lru", "w_gate", "w_o", "ffn_w13", "ffn_w2"], _reduce_scatter(parts, "grad_rs")):
        grads[n_] = g[None]

    big_names = ["w_ada", "w_in", "w_out_ssd", "w_out_lru", "w_gate", "w_o", "ffn_w13", "ffn_w2"]
    small_names = [n_ for n_ in names if n_ not in big_names]
    delta, new_m, new_v = {}, {}, {}
    for n_ in big_names:
        delta[n_], new_m[n_], new_v[n_] = _adamw(wts[n_], grads[n_], mom1[n_], mom2[n_], "adamw_" + n_)
    packed = [_pack([src[n_] for n_ in small_names]) for src in (wts, grads, mom1, mom2)]
    outs = _adamw(*packed, "adamw_small")
    like = [wts[n_] for n_ in small_names]
    for dst, buf in zip((delta, new_m, new_v), outs):
        dst.update(zip(small_names, _unpack(buf, like)))

    return (loss, grad_x[None], *[grads[n_].reshape(wts[n_].shape) for n_ in names], *[delta[n_] for n_ in names],
            *[new_m[n_] for n_ in names], *[new_v[n_] for n_ in names])
```

```python
import functools

import numpy as np
import jax
import jax.numpy as jnp
from jax import lax
from jax.experimental import pallas as pl
from jax.experimental.pallas import tpu as pltpu

F32, BF16 = jnp.float32, jnp.bfloat16
MESH = pl.DeviceIdType.MESH
N_DEV = 8
VMEM_LIMIT_BYTES = 48 * 1024 * 1024
ROW_BUDGET_BYTES = 12 * 1024 * 1024
LANES, SUBLANES = 128, 8

EPS = 1e-6
GRID_W = 64
SSD_HEAD_DIM = 64
SSD_GROUPS = 4
SSD_STATE = 128
SSD_CHUNK = 128
CONV_W = 4
CONV_LEFT = 2
LRU_HEAD_DIM = 128
LRU_C = 8.0
LRU_BLOCK = 128
N_MOD = 6
ADAM_LR, ADAM_B1, ADAM_B2, ADAM_EPS, ADAM_WD, ADAM_STEP = 0.001, 0.9, 0.999, 1e-08, 0.01, 10
NEG_BIG = -1e30


def _params(sem):
    return pltpu.CompilerParams(dimension_semantics=sem, vmem_limit_bytes=VMEM_LIMIT_BYTES)


def _tile(dim, pref, align):
    if dim <= pref:
        return dim
    t = (pref // align) * align
    while t >= align:
        if dim % t == 0:
            return t
        t -= align
    return dim


def _bdot(a, b):
    return jnp.dot(a.astype(BF16), b.astype(BF16), preferred_element_type=F32)


def _bdot_nt(a, b):
    return lax.dot_general(a.astype(BF16), b.astype(BF16), (((1,), (1,)), ((), ())), preferred_element_type=F32)


def _bdot_tn(a, b):
    return lax.dot_general(a.astype(BF16), b.astype(BF16), (((0,), (0,)), ((), ())), preferred_element_type=F32)


def _hdot(a, b):
    return jnp.dot(a, b, preferred_element_type=F32, precision=lax.Precision.HIGHEST)


def _softplus(v):
    return jnp.maximum(v, 0.0) + jnp.log(1.0 + jnp.exp(-jnp.abs(v)))


def _sigmoid(v):
    return 0.5 * jnp.tanh(0.5 * v) + 0.5


def _silu(v):
    return v * _sigmoid(v)


def _dsilu(v):
    s = _sigmoid(v)
    return s * (1.0 + v * (1.0 - s))


GELU_C0 = 0.7978845608028654
GELU_C1 = 0.044715


def _gelu(v):
    return 0.5 * v * (1.0 + jnp.tanh(GELU_C0 * (v + GELU_C1 * v * v * v)))


def _dgelu(v):
    t = jnp.tanh(GELU_C0 * (v + GELU_C1 * v * v * v))
    return 0.5 * (1.0 + t) + 0.5 * v * (1.0 - t * t) * GELU_C0 * (1.0 + 3.0 * GELU_C1 * v * v)


MM_TM, MM_TN = 1024, 1408
MM_TK = {1: 2048, 2: 1536}


def _mm(pairs, mode, out_dtype, name, init=None):
    a0, b0 = pairs[0]
    if mode == "nn":
        m, n, ks = a0.shape[0], b0.shape[1], [a.shape[1] for a, _ in pairs]
    elif mode == "nt":
        m, n, ks = a0.shape[0], b0.shape[0], [a.shape[1] for a, _ in pairs]
    else:
        m, n, ks = a0.shape[1], b0.shape[1], [a.shape[0] for a, _ in pairs]
    tm = _tile(m, MM_TM, LANES if mode == "tn" else 16)
    tn = _tile(n, MM_TN, LANES)
    tk = MM_TK.get(sum(k > 512 for k in ks), 512)
    tks = [_tile(k, tk, 16 if mode == "tn" else LANES) for k in ks]
    steps = [k // t for k, t in zip(ks, tks)]
    offs = [sum(steps[:i]) for i in range(len(pairs))]
    total = sum(steps)
    npairs = len(pairs)

    def kk(i):
        return lambda k: jnp.clip(k - offs[i], 0, steps[i] - 1)

    in_specs, operands = [], []
    for i, (a, b) in enumerate(pairs):
        kf = kk(i)
        if mode == "nn":
            in_specs.append(pl.BlockSpec((tm, tks[i]), lambda mi, ni, k, kf=kf: (mi, kf(k))))
            in_specs.append(pl.BlockSpec((tks[i], tn), lambda mi, ni, k, kf=kf: (kf(k), ni)))
        elif mode == "nt":
            in_specs.append(pl.BlockSpec((tm, tks[i]), lambda mi, ni, k, kf=kf: (mi, kf(k))))
            in_specs.append(pl.BlockSpec((tn, tks[i]), lambda mi, ni, k, kf=kf: (ni, kf(k))))
        else:
            in_specs.append(pl.BlockSpec((tks[i], tm), lambda mi, ni, k, kf=kf: (kf(k), mi)))
            in_specs.append(pl.BlockSpec((tks[i], tn), lambda mi, ni, k, kf=kf: (kf(k), ni)))
        operands += [a, b]
    dot = {"nn": _bdot, "nt": _bdot_nt, "tn": _bdot_tn}[mode]
    if init is not None:
        in_specs.append(pl.BlockSpec((tm, tn), lambda mi, ni, k: (mi, ni)))
        operands.append(init)
    assert out_dtype == F32 or total == 1

    def body(*refs):
        out_ref = refs[-1]
        k = pl.program_id(2)

        def first_step():
            p = dot(refs[0][...], refs[1][...])
            return p if init is None else p + refs[2 * npairs][...]

        if total == 1:
            out_ref[...] = first_step().astype(out_dtype)
            return

        @pl.when(k == 0)
        def _():
            out_ref[...] = first_step()

        for i in range(npairs):
            first = 1 if i == 0 else offs[i]
            if offs[i] + steps[i] > first:
                @pl.when((k >= first) & (k < offs[i] + steps[i]))
                def _(i=i):
                    out_ref[...] += dot(refs[2 * i][...], refs[2 * i + 1][...])

    return pl.pallas_call(
        body, name=name, grid=(m // tm, n // tn, total), in_specs=in_specs,
        out_specs=pl.BlockSpec((tm, tn), lambda mi, ni, k: (mi, ni)),
        out_shape=jax.ShapeDtypeStruct((m, n), out_dtype),
        compiler_params=_params(("parallel", "parallel", "arbitrary")),
    )(*operands)


def _rowwise(fn, rows, small, outs, accs, name):
    t = rows[0].shape[0]
    per_row = sum(r.shape[1] * r.dtype.itemsize for r in rows) + sum(c * jnp.dtype(d).itemsize for c, d in outs)
    tm = SUBLANES
    while tm * 2 <= 1024 and t % (tm * 2) == 0 and tm * 2 * per_row <= ROW_BUDGET_BYTES:
        tm *= 2
    if t % tm:
        tm = t
    nrow, nsmall, nout, nacc = len(rows), len(small), len(outs), len(accs)

    def body(*refs):
        rv = [r[...].astype(F32) for r in refs[:nrow]]
        sv = [r[...] for r in refs[nrow:nrow + nsmall]]
        o_refs = refs[nrow + nsmall:nrow + nsmall + nout]
        a_refs = refs[nrow + nsmall + nout:]
        ov, av = fn(rv, sv)
        for r, v in zip(o_refs, ov):
            r[...] = v.astype(r.dtype)
        if nacc:
            @pl.when(pl.program_id(0) == 0)
            def _():
                for r in a_refs:
                    r[...] = jnp.zeros_like(r)
            for r, v in zip(a_refs, av):
                r[...] += v

    in_specs = [pl.BlockSpec((tm, r.shape[1]), lambda i: (i, 0)) for r in rows]
    in_specs += [pl.BlockSpec(s.shape, lambda i: (0, 0)) for s in small]
    out_specs = [pl.BlockSpec((tm, c), lambda i: (i, 0)) for c, _ in outs]
    out_specs += [pl.BlockSpec((1, c), lambda i: (0, 0)) for c in accs]
    out_shape = [jax.ShapeDtypeStruct((t, c), d) for c, d in outs]
    out_shape += [jax.ShapeDtypeStruct((1, c), F32) for c in accs]
    res = pl.pallas_call(
        body, name=name, grid=(t // tm,), in_specs=in_specs, out_specs=out_specs, out_shape=out_shape,
        compiler_params=_params(("arbitrary",)),
    )(*rows, *small)
    return res[:nout], res[nout:]


def _colsum(v):
    return jnp.sum(v, axis=0, keepdims=True)


def _rowmean(v):
    return jnp.mean(v, axis=1, keepdims=True)


def _conv_tiles(x):
    t, c = x.shape
    tm = _tile(t, 256, SUBLANES)
    nt = t // tm
    r8 = tm // SUBLANES
    specs = [
        pl.BlockSpec((SUBLANES, c), lambda i: (jnp.maximum(i * r8 - 1, 0), 0)),
        pl.BlockSpec((tm, c), lambda i: (i, 0)),
        pl.BlockSpec((SUBLANES, c), lambda i: (jnp.minimum((i + 1) * r8, t // SUBLANES - 1), 0)),
    ]
    return tm, nt, specs


def _conv_taps(prev_ref, cur_ref, next_ref, tm, nt, left):
    i = pl.program_id(0)
    p = jnp.where(i > 0, prev_ref[...], 0.0)
    q = jnp.where(i < nt - 1, next_ref[...], 0.0)
    ext = jnp.concatenate([p, cur_ref[...], q], axis=0)
    n = tm + 2 * SUBLANES
    taps = []
    for k in range(CONV_W):
        s = (left - k) % n
        r = pltpu.roll(ext, s, axis=0) if s else ext
        taps.append(r[SUBLANES:SUBLANES + tm])
    return taps


def _conv(x, w, b, left, act, name, out_dtype=F32):
    t, c = x.shape
    tm, nt, specs = _conv_tiles(x)

    def body(prev_ref, cur_ref, next_ref, w_ref, b_ref, o_ref):
        taps = _conv_taps(prev_ref, cur_ref, next_ref, tm, nt, left)
        pre = b_ref[...] + sum(w_ref[k:k + 1, :] * taps[k] for k in range(CONV_W))
        o_ref[...] = (_silu(pre) if act else pre).astype(out_dtype)

    return pl.pallas_call(
        body, name=name, grid=(nt,),
        in_specs=specs + [pl.BlockSpec((CONV_W, c), lambda i: (0, 0)), pl.BlockSpec((1, c), lambda i: (0, 0))],
        out_specs=pl.BlockSpec((tm, c), lambda i: (i, 0)),
        out_shape=jax.ShapeDtypeStruct((t, c), out_dtype), compiler_params=_params(("arbitrary",)),
    )(x, x, x, w, b)


def _conv_bwd(x, dy, w, b, act, name):
    t, c = x.shape
    tm, nt, specs = _conv_tiles(x)

    def body(prev_ref, cur_ref, next_ref, dy_ref, w_ref, b_ref, dpre_ref, dw0, dw1, dw2, dw3, db_ref):
        taps = _conv_taps(prev_ref, cur_ref, next_ref, tm, nt, CONV_LEFT)
        if act:
            pre = b_ref[...] + sum(w_ref[k:k + 1, :] * taps[k] for k in range(CONV_W))
            dpre = dy_ref[...] * _dsilu(pre)
        else:
            dpre = dy_ref[...]
        dpre_ref[...] = dpre

        @pl.when(pl.program_id(0) == 0)
        def _():
            for r in (dw0, dw1, dw2, dw3, db_ref):
                r[...] = jnp.zeros_like(r)

        for k, r in enumerate((dw0, dw1, dw2, dw3)):
            r[...] += _colsum(dpre * taps[k])
        db_ref[...] += _colsum(dpre)

    row = pl.BlockSpec((1, c), lambda i: (0, 0))
    res = pl.pallas_call(
        body, name=name + "_pre", grid=(nt,),
        in_specs=specs + [pl.BlockSpec((tm, c), lambda i: (i, 0)), pl.BlockSpec((CONV_W, c), lambda i: (0, 0)), row],
        out_specs=[pl.BlockSpec((tm, c), lambda i: (i, 0))] + [row] * 5,
        out_shape=[jax.ShapeDtypeStruct((t, c), F32)] + [jax.ShapeDtypeStruct((1, c), F32)] * 5,
        compiler_params=_params(("arbitrary",)),
    )(x, x, x, dy, w, b)
    dpre, dw, db = res[0], jnp.concatenate(res[1:5], axis=0), res[5]
    dx = _conv(dpre, w[::-1], jnp.zeros_like(b), CONV_W - 1 - CONV_LEFT, False, name + "_dx", BF16)
    return dx, dw, db


def _ssd_dims(xs, bm):
    t, ws = xs.shape
    heads = ws // SSD_HEAD_DIM
    hpg = heads // SSD_GROUPS
    assert hpg % 2 == 0 and bm.shape[1] == SSD_GROUPS * SSD_STATE and t % SSD_CHUNK == 0
    return t, ws, heads, hpg, heads // 2, t // SSD_CHUNK


def _ssd_chunk_terms(dtr_ref, dtrt_ref, bias_ref, biast_ref, alog_ref, alogt_ref, reverse):
    q = SSD_CHUNK
    a_neg = -jnp.exp(alog_ref[...])
    dt = _softplus(dtr_ref[...] + bias_ref[...])
    dtt = _softplus(dtrt_ref[...] + biast_ref[...])
    a = dt * a_neg
    at = dtt * (-jnp.exp(alogt_ref[...]))
    ii = lax.broadcasted_iota(jnp.int32, (q, q), 0)
    jj = lax.broadcasted_iota(jnp.int32, (q, q), 1)
    mask = (ii <= jj) if reverse else (ii >= jj)
    tri = mask.astype(F32)
    trit = ((jj <= ii) if reverse else (jj >= ii)).astype(F32)
    before = (jj > ii) if reverse else (jj < ii)
    ac = _hdot(tri, a)
    act = _hdot(at, trit)
    tot = _colsum(a)
    return a_neg, dt, a, ac, act, tot, mask, trit, before


def _pair_terms(lo, h1, dt, ac, tot):
    q = SSD_CHUNK

    def colb(v, h):
        return jnp.broadcast_to(v[:, h:h + 1], (v.shape[0], LANES))

    def pairb(v):
        return jnp.where(lo[:v.shape[0]], colb(v, h1), colb(v, h1 + 1))

    dt_p = pairb(dt)
    ac_p = pairb(ac)
    eac_p = jnp.exp(ac_p)
    dte_p = jnp.exp(pairb(jnp.broadcast_to(tot, (q, tot.shape[1]))) - ac_p)
    etot_p = jnp.exp(pairb(tot))
    return dt_p, eac_p, dte_p, etot_p


def _decay_mats(mask, ac, act, cb, h1):
    q = SSD_CHUNK
    out = []
    for h in (h1, h1 + 1):
        diff = jnp.broadcast_to(ac[:, h:h + 1], (q, q)) - jnp.broadcast_to(act[h:h + 1, :], (q, q))
        lmat = jnp.exp(jnp.where(mask, diff, NEG_BIG))
        out.append((lmat, cb * lmat))
    return out


def _ssd_specs(t, ws, heads, npairs, nc, reverse):
    q, gn = SSD_CHUNK, SSD_GROUPS * SSD_STATE
    ci = (lambda k: nc - 1 - k) if reverse else (lambda k: k)
    small = lambda shape: pl.BlockSpec(shape, lambda k: (0,) * len(shape))
    seq = [
        pl.BlockSpec((q, ws), lambda k: (ci(k), 0)),
        pl.BlockSpec((q, gn), lambda k: (ci(k), 0)),
        pl.BlockSpec((q, gn), lambda k: (ci(k), 0)),
        pl.BlockSpec((q, heads), lambda k: (ci(k), 0)),
        pl.BlockSpec((heads, q), lambda k: (0, ci(k))),
        small((1, heads)), small((heads, 1)), small((1, heads)), small((heads, 1)),
    ]
    return ci, small, seq


def _ssd_fwd(xs, bm, cm, dtr, bias, alog, s0, reverse, name):
    t, ws, heads, hpg, npairs, nc = _ssd_dims(xs, bm)
    q, ns = SSD_CHUNK, SSD_STATE
    ci, small, seq = _ssd_specs(t, ws, heads, npairs, nc, reverse)

    def body(xs_ref, b_ref, c_ref, dtr_ref, dtrt_ref, bias_ref, biast_ref, alog_ref, alogt_ref, s0_ref,
             y_ref, se_ref, s_ref):
        @pl.when(pl.program_id(0) == 0)
        def _():
            s_ref[...] = s0_ref[...]

        _, dt, _, ac, act, tot, mask, _, _ = _ssd_chunk_terms(dtr_ref, dtrt_ref, bias_ref, biast_ref, alog_ref, alogt_ref, reverse)
        lo = lax.broadcasted_iota(jnp.int32, (q, LANES), 1) < SSD_HEAD_DIM
        for g in range(SSD_GROUPS):
            bg = b_ref[:, g * ns:(g + 1) * ns]
            cg = c_ref[:, g * ns:(g + 1) * ns]
            cb = _bdot_nt(cg, bg)
            bgt = bg.T
            for pq in range(hpg // 2):
                pp = g * (hpg // 2) + pq
                h1 = 2 * pp
                cs = slice(pp * LANES, (pp + 1) * LANES)
                (_, m1), (_, m2) = _decay_mats(mask, ac, act, cb, h1)
                dt_p, eac_p, dte_p, etot_p = _pair_terms(lo, h1, dt, ac, tot)
                xp = xs_ref[:, cs] * dt_p
                sp = s_ref[pp]
                se_ref[0, pp] = sp
                y_ref[:, cs] = jnp.where(lo, _bdot(m1, xp), _bdot(m2, xp)) + eac_p * _bdot(cg, sp)
                s_ref[pp] = etot_p * sp + _bdot(bgt, xp * dte_p)

    st = (npairs, ns, LANES)
    return pl.pallas_call(
        body, name=name, grid=(nc,), in_specs=seq + [small(st)],
        out_specs=[pl.BlockSpec((q, ws), lambda k: (ci(k), 0)),
                   pl.BlockSpec((1,) + st, lambda k: (ci(k), 0, 0, 0)), small(st)],
        out_shape=[jax.ShapeDtypeStruct((t, ws), F32), jax.ShapeDtypeStruct((nc,) + st, F32), jax.ShapeDtypeStruct(st, F32)],
        compiler_params=_params(("arbitrary",)),
    )(xs, bm, cm, dtr, dtr.T, bias, bias.T, alog, alog.T, s0)


def _ssd_bwd(xs, bm, cm, dtr, bias, alog, dy, s_enter, lam0, reverse, name):
    t, ws, heads, hpg, npairs, nc = _ssd_dims(xs, bm)
    q, ns, gn = SSD_CHUNK, SSD_STATE, SSD_GROUPS * SSD_STATE
    ci, small, seq = _ssd_specs(t, ws, heads, npairs, nc, not reverse)
    st = (npairs, ns, LANES)

    e_heads = jnp.asarray(np.arange(ws)[:, None] // SSD_HEAD_DIM == np.arange(heads)[None, :], BF16)

    def body(xs_ref, b_ref, c_ref, dtr_ref, dtrt_ref, bias_ref, biast_ref, alog_ref, alogt_ref, dy_ref, se_ref, lam0_ref, e_ref,
             dxs_ref, db_ref, dc_ref, ddtr_ref, da_ref, dbias_ref, lam_ref, st_s, ts_s, ss_s, xx_s):
        @pl.when(pl.program_id(0) == 0)
        def _():
            lam_ref[...] = lam0_ref[...]
            da_ref[...] = jnp.zeros_like(da_ref)
            dbias_ref[...] = jnp.zeros_like(dbias_ref)

        a_neg, dt, _, ac, act, tot, mask, trit, before = _ssd_chunk_terms(dtr_ref, dtrt_ref, bias_ref, biast_ref, alog_ref, alogt_ref, reverse)
        lo = lax.broadcasted_iota(jnp.int32, (q, LANES), 1) < SSD_HEAD_DIM
        head_col = lax.broadcasted_iota(jnp.int32, (heads, 1), 0)
        ii = lax.broadcasted_iota(jnp.int32, (q, q), 0)
        jj = lax.broadcasted_iota(jnp.int32, (q, q), 1)
        before_t = ((ii > jj) if reverse else (ii < jj)).astype(F32)

        def per_head(v):
            hi = v.astype(BF16)
            lo = (v - hi.astype(F32)).astype(BF16)
            sel = e_ref[...]
            return jnp.dot(hi, sel, preferred_element_type=F32) + jnp.dot(lo, sel, preferred_element_type=F32)

        da_tt = jnp.zeros((heads, q), F32)
        for g in range(SSD_GROUPS):
            gs = slice(g * ns, (g + 1) * ns)
            bg, cg = b_ref[:, gs], c_ref[:, gs]
            cb = _bdot_nt(cg, bg)
            cgt = cg.T
            dcb = jnp.zeros((q, q), F32)
            dbg = jnp.zeros((q, ns), F32)
            dcg = jnp.zeros((q, ns), F32)
            for pq in range(hpg // 2):
                pp = g * (hpg // 2) + pq
                h1 = 2 * pp
                cs = slice(pp * LANES, (pp + 1) * LANES)
                (l1, m1), (l2, m2) = _decay_mats(mask, ac, act, cb, h1)
                dt_p, eac_p, dte_p, etot_p = _pair_terms(lo, h1, dt, ac, tot)
                xsp = xs_ref[:, cs]
                xp = xsp * dt_p
                dyp = dy_ref[:, cs]
                lam = lam_ref[pp]
                sp = se_ref[0, pp]
                dx_state = dte_p * _bdot(bg, lam)
                dxdt = jnp.where(lo, _bdot(m1.T, dyp), _bdot(m2.T, dyp)) + dx_state
                dyx1 = _bdot_nt(jnp.where(lo, dyp, 0.0), xp)
                dyx2 = _bdot_nt(jnp.where(lo, 0.0, dyp), xp)
                dcb = dcb + l1 * dyx1 + l2 * dyx2
                for h, pair in ((h1, m1 * dyx1), (h1 + 1, m2 * dyx2)):
                    cross = _colsum(jnp.where(mask, _bdot(pair, before_t), 0.0))
                    da_tt = da_tt + (head_col == h).astype(F32) * cross
                st_s[:, cs] = dyp * (eac_p * _bdot(cg, sp))
                ts_s[:, cs] = dx_state * xp
                ss_s[:, cs] = jnp.broadcast_to(_colsum(lam * sp) * etot_p, (SUBLANES, LANES))
                xx_s[:, cs] = dxdt * xsp
                edy = eac_p * dyp
                xe = dte_p * xp
                dcg = dcg + _bdot_nt(edy, sp)
                dbg = dbg + _bdot_nt(xe, lam)
                lam_ref[pp] = etot_p * lam + _bdot(cgt, edy)
                dxs_ref[:, cs] = dxdt * dt_p
            db_ref[:, gs] = dbg + _bdot(dcb.T, cg)
            dc_ref[:, gs] = dcg + _bdot(dcb, bg)
        da = da_tt.T + _hdot(trit, per_head(st_s[...])) + _hdot(before.astype(F32), per_head(ts_s[...])) + per_head(ss_s[...])[0:1]
        ddtr = (da * a_neg + per_head(xx_s[...])) * _sigmoid(dtr_ref[...] + bias_ref[...])
        ddtr_ref[...] = ddtr
        da_ref[...] += _colsum(da * dt)
        dbias_ref[...] += _colsum(ddtr)

    row = lambda c: pl.BlockSpec((q, c), lambda k: (ci(k), 0))
    return pl.pallas_call(
        body, name=name, grid=(nc,),
        in_specs=seq + [row(ws), pl.BlockSpec((1,) + st, lambda k: (ci(k), 0, 0, 0)), small(st), small((ws, heads))],
        out_specs=[row(ws), row(gn), row(gn), row(heads), small((1, heads)), small((1, heads)), small(st)],
        out_shape=[jax.ShapeDtypeStruct((t, ws), F32), jax.ShapeDtypeStruct((t, gn), F32), jax.ShapeDtypeStruct((t, gn), F32),
                   jax.ShapeDtypeStruct((t, heads), F32), jax.ShapeDtypeStruct((1, heads), F32),
                   jax.ShapeDtypeStruct((1, heads), F32), jax.ShapeDtypeStruct(st, F32)],
        scratch_shapes=[pltpu.VMEM((q, ws), F32), pltpu.VMEM((q, ws), F32), pltpu.VMEM((SUBLANES, ws), F32), pltpu.VMEM((q, ws), F32)],
        compiler_params=_params(("arbitrary",)),
    )(xs, bm, cm, dtr, dtr.T, bias, bias.T, alog, alog.T, dy, s_enter, lam0, e_heads)


def _tile_scan(a_ref, u_ref, out_ref, carry_ref, ntiles, reverse):
    w = a_ref.shape[1]
    row = lax.broadcasted_iota(jnp.int32, (SUBLANES, w), 0)

    def step(j, _):
        r0 = pl.multiple_of((ntiles - 1 - j if reverse else j) * SUBLANES, SUBLANES)
        a = a_ref[pl.ds(r0, SUBLANES), :]
        u = u_ref[pl.ds(r0, SUBLANES), :]
        for d in (1, 2, 4):
            keep = (row < SUBLANES - d) if reverse else (row >= d)
            s = SUBLANES - d if reverse else d
            a_sh = jnp.where(keep, pltpu.roll(a, s, axis=0), 1.0)
            u_sh = jnp.where(keep, pltpu.roll(u, s, axis=0), 0.0)
            u = a * u_sh + u
            a = a * a_sh
        h = a * carry_ref[...] + u
        out_ref[pl.ds(r0, SUBLANES), :] = h
        last = h[0:1, :] if reverse else h[SUBLANES - 1:SUBLANES, :]
        carry_ref[...] = jnp.broadcast_to(last, (SUBLANES, w))
        return 0

    lax.fori_loop(0, ntiles, step, 0)


def _lru_gates(xh, wa, wx, ba, bx, lam):
    r = _sigmoid(_bdot(xh, wa) + ba)
    i = _sigmoid(_bdot(xh, wx) + bx)
    sp = _softplus(-lam)
    a = jnp.exp(-LRU_C * r * sp)
    return r, i, sp, a


def _lru_fwd(xc, wa, wx, ba, bx, lam, h0, addend, reverse, name):
    t, w = xc.shape
    tb = LRU_BLOCK
    nb, nh = t // tb, w // LRU_HEAD_DIM
    bi = (lambda k: nb - 1 - k) if reverse else (lambda k: k)
    has_add = addend is not None

    def body(*refs):
        xc_ref, wa_ref, wx_ref, ba_ref, bx_ref, lam_ref, h0_ref = refs[:7]
        add_ref = refs[7] if has_add else None
        h_ref, hl_ref = refs[7 + has_add], refs[8 + has_add]
        sum_ref = refs[9 + has_add] if has_add else None
        a_s, u_s, carry = refs[-3:]

        @pl.when(pl.program_id(0) == 0)
        def _():
            carry[...] = jnp.broadcast_to(h0_ref[...], carry.shape)

        for hd in range(nh):
            hs = slice(hd * LRU_HEAD_DIM, (hd + 1) * LRU_HEAD_DIM)
            xh = xc_ref[:, hs]
            _, i, _, a = _lru_gates(xh, wa_ref[hd], wx_ref[hd], ba_ref[:, hs], bx_ref[:, hs], lam_ref[:, hs])
            a_s[:, hs] = a
            u_s[:, hs] = jnp.sqrt(1.0 - a * a) * (i * xh)
        _tile_scan(a_s, u_s, h_ref, carry, tb // SUBLANES, reverse)
        hl_ref[...] = carry[0:1, :]
        if has_add:
            sum_ref[...] = h_ref[...] + add_ref[...]

    blk = pl.BlockSpec((tb, w), lambda k: (bi(k), 0))
    vec = pl.BlockSpec((1, w), lambda k: (0, 0))
    wsp = pl.BlockSpec((nh, LRU_HEAD_DIM, LRU_HEAD_DIM), lambda k: (0, 0, 0))
    out_specs = [blk, vec] + ([blk] if has_add else [])
    out_shape = [jax.ShapeDtypeStruct((t, w), F32), jax.ShapeDtypeStruct((1, w), F32)] + ([jax.ShapeDtypeStruct((t, w), F32)] if has_add else [])
    return pl.pallas_call(
        body, name=name, grid=(nb,),
        in_specs=[blk, wsp, wsp, vec, vec, vec, vec] + ([blk] if has_add else []),
        out_specs=out_specs, out_shape=out_shape,
        scratch_shapes=[pltpu.VMEM((tb, w), F32), pltpu.VMEM((tb, w), F32), pltpu.VMEM((SUBLANES, w), F32)],
        compiler_params=_params(("arbitrary",)),
    )(xc, wa, wx, ba, bx, lam, h0, *([addend] if has_add else []))


def _lru_bwd(xc, wa, wx, ba, bx, lam, h0, h, dh, dfin, addend, reverse, name):
    t, w = xc.shape
    tb = LRU_BLOCK
    nb, nh = t // tb, w // LRU_HEAD_DIM
    r8 = tb // SUBLANES
    bi = (lambda k: k) if reverse else (lambda k: nb - 1 - k)
    if reverse:
        halo = lambda k: (jnp.minimum((bi(k) + 1) * r8, t // SUBLANES - 1), 0)
    else:
        halo = lambda k: (jnp.maximum(bi(k) * r8 - 1, 0), 0)
    has_add = addend is not None
    n_ext = tb + SUBLANES

    def body(*refs):
        xc_ref, wa_ref, wx_ref, ba_ref, bx_ref, lam_ref, h0_ref, h_ref, halo_ref, dh_ref, dfin_ref = refs[:11]
        add_ref = refs[11] if has_add else None
        dxc_ref, dwa_ref, dwx_ref, dba_ref, dbx_ref, dsp_ref, dh0_ref = refs[11 + has_add:18 + has_add]
        a_s, v_s, mu_s, r_s, i_s, carry = refs[-6:]
        k = pl.program_id(0)

        @pl.when(k == 0)
        def _():
            carry[...] = jnp.broadcast_to(dfin_ref[...], carry.shape)
            for r in (dwa_ref, dwx_ref, dba_ref, dbx_ref, dsp_ref):
                r[...] = jnp.zeros_like(r)

        for hd in range(nh):
            hs = slice(hd * LRU_HEAD_DIM, (hd + 1) * LRU_HEAD_DIM)
            r, i, _, a = _lru_gates(xc_ref[:, hs], wa_ref[hd], wx_ref[hd], ba_ref[:, hs], bx_ref[:, hs], lam_ref[:, hs])
            a_s[:, hs] = a
            r_s[:, hs] = r
            i_s[:, hs] = i
            v_s[:, hs] = a * dh_ref[:, hs]
        mu_in = carry[...]
        _tile_scan(a_s, v_s, mu_s, carry, r8, not reverse)
        dh0_ref[...] = carry[0:1, :]
        first = (bi(k) == nb - 1) if reverse else (bi(k) == 0)
        edge = jnp.where(first, jnp.broadcast_to(h0_ref[...], (SUBLANES, w)), halo_ref[...])
        for hd in range(nh):
            hs = slice(hd * LRU_HEAD_DIM, (hd + 1) * LRU_HEAD_DIM)
            if reverse:
                mu_sh = pltpu.roll(jnp.concatenate([mu_in[:, hs], mu_s[:, hs]], axis=0), 1, axis=0)[SUBLANES:]
                h_sh = pltpu.roll(jnp.concatenate([h_ref[:, hs], edge[:, hs]], axis=0), n_ext - 1, axis=0)[:tb]
            else:
                mu_sh = pltpu.roll(jnp.concatenate([mu_s[:, hs], mu_in[:, hs]], axis=0), n_ext - 1, axis=0)[:tb]
                h_sh = pltpu.roll(jnp.concatenate([edge[:, hs], h_ref[:, hs]], axis=0), 1, axis=0)[SUBLANES:]
            xh = xc_ref[:, hs]
            r, i, a = r_s[:, hs], i_s[:, hs], a_s[:, hs]
            sp = _softplus(-lam_ref[:, hs])
            om = 1.0 - a * a
            rs = lax.rsqrt(om)
            lam_t = dh_ref[:, hs] + mu_sh
            dla = lam_t * h_sh * a - lam_t * (i * xh) * (a * a) * rs
            dix = lam_t * (om * rs)
            dpr = dla * (-LRU_C * sp) * r * (1.0 - r)
            dpi = dix * xh * i * (1.0 - i)
            dsp_ref[:, hs] += _colsum(dla * (-LRU_C * r))
            dba_ref[:, hs] += _colsum(dpr)
            dbx_ref[:, hs] += _colsum(dpi)
            dx = dix * i + _bdot_nt(dpr, wa_ref[hd]) + _bdot_nt(dpi, wx_ref[hd])
            xt = xh.T
            dwa_ref[hd] += _bdot(xt, dpr)
            dwx_ref[hd] += _bdot(xt, dpi)
            dxc_ref[:, hs] = dx + add_ref[:, hs] if has_add else dx

    blk = pl.BlockSpec((tb, w), lambda k: (bi(k), 0))
    vec = pl.BlockSpec((1, w), lambda k: (0, 0))
    wsp = pl.BlockSpec((nh, LRU_HEAD_DIM, LRU_HEAD_DIM), lambda k: (0, 0, 0))
    wshape = jax.ShapeDtypeStruct((nh, LRU_HEAD_DIM, LRU_HEAD_DIM), F32)
    vshape = jax.ShapeDtypeStruct((1, w), F32)
    return pl.pallas_call(
        body, name=name, grid=(nb,),
        in_specs=[blk, wsp, wsp, vec, vec, vec, vec, blk, pl.BlockSpec((SUBLANES, w), halo), blk, vec] + ([blk] if has_add else []),
        out_specs=[blk, wsp, wsp, vec, vec, vec, vec],
        out_shape=[jax.ShapeDtypeStruct((t, w), F32), wshape, wshape, vshape, vshape, vshape, vshape],
        scratch_shapes=[pltpu.VMEM((tb, w), F32)] * 5 + [pltpu.VMEM((SUBLANES, w), F32)],
        compiler_params=_params(("arbitrary",)),
    )(xc, wa, wx, ba, bx, lam, h0, h, h, dh, dfin, *([addend] if has_add else []))


def _coords():
    return lax.axis_index("x"), lax.axis_index("y"), lax.axis_index("c")


HALF_ROWS_ALIGN = 32


def _all_gather(shards, name):
    n = len(shards)
    split = [s.ndim >= 2 and s.shape[0] % HALF_ROWS_ALIGN == 0 for s in shards]
    n_copies = 10

    def body(*refs):
        ins, outs = refs[:n], refs[n:2 * n]
        send_sems, recv_sems, local_sems = refs[2 * n:]
        x, y, c = _coords()
        slot = lambda p: 4 * p[0] + 2 * p[1] + p[2]
        me, xp, yp, dg = (x, y, c), (1 - x, y, c), (x, 1 - y, c), (1 - x, 1 - y, c)
        other_core = lambda p: (p[0], p[1], 1 - p[2])

        def rows(ref, i, half):
            r = shards[i].shape[0]
            if half == "all" or not split[i]:
                return ref
            return ref.at[pl.ds(0, r // 2)] if half == "a" else ref.at[pl.ds(r // 2, r // 2)]

        def copy(i, j, holder, half, to, own=False):
            dst = rows(outs[i].at[slot(holder)], i, half)
            src = rows(ins[i], i, half) if own else dst
            return pltpu.make_async_remote_copy(src_ref=src, dst_ref=dst, send_sem=send_sems.at[i, j],
                                                recv_sem=recv_sems.at[i, j], device_id=to, device_id_type=MESH)

        rounds = [
            ([(0, me, "a", xp, True), (1, me, "b", yp, True)],
             [(0, xp, "a"), (1, yp, "b")]),
            ([(2, me, "a", yp, True), (3, xp, "a", yp, False), (4, me, "b", xp, True), (5, yp, "b", xp, False)],
             [(2, yp, "a"), (3, dg, "a"), (4, xp, "b"), (5, dg, "b")]),
            ([(6, me, "all", other_core(me), True)] + [(7 + q, p, "all", other_core(me), False) for q, p in enumerate((xp, yp, dg))],
             [(6 + q, other_core(p), "all") for q, p in enumerate((me, xp, yp, dg))]),
        ]
        local = [pltpu.make_async_copy(ins[i], outs[i].at[slot(me)], local_sems.at[i]) for i in range(n)]
        for cp in local:
            cp.start()
        sent = []
        for sends, lands in rounds:
            for j, holder, half, to, own in sends:
                for i in range(n):
                    if half == "b" and not split[i]:
                        continue
                    cp = copy(i, j, holder, half, to, own)
                    cp.start()
                    sent.append(cp)
            for j, holder, half in lands:
                for i in range(n):
                    if half == "b" and not split[i]:
                        continue
                    copy(i, j, holder, half, me).wait_recv()
        for cp in sent:
            cp.wait_send()
        for cp in local:
            cp.wait()

    any_spec = pl.BlockSpec(memory_space=pl.ANY)
    return pl.pallas_call(
        body, name=name, in_specs=[any_spec] * n, out_specs=[any_spec] * n,
        out_shape=[jax.ShapeDtypeStruct((N_DEV,) + s.shape, s.dtype) for s in shards],
        scratch_shapes=[pltpu.SemaphoreType.DMA((n, n_copies)), pltpu.SemaphoreType.DMA((n, n_copies)), pltpu.SemaphoreType.DMA((n,))],
    )(*shards)


def _swap(bufs, axes, name):
    n = len(bufs)

    def body(*refs):
        ins, outs = refs[:n], refs[n:2 * n]
        send_sems, recv_sems = refs[2 * n:]
        x, y, c = _coords()
        to = {"x": (1 - x, y, c), "y": (x, 1 - y, c), "c": (x, y, 1 - c)}
        cps = [pltpu.make_async_remote_copy(src_ref=ins[i], dst_ref=outs[i], send_sem=send_sems.at[i], recv_sem=recv_sems.at[i],
                                            device_id=to[axes[i]], device_id_type=MESH) for i in range(n)]
        for cp in cps:
            cp.start()
        for cp in cps:
            cp.wait()

    any_spec = pl.BlockSpec(memory_space=pl.ANY)
    return pl.pallas_call(
        body, name=name, in_specs=[any_spec] * n, out_specs=[any_spec] * n,
        out_shape=[jax.ShapeDtypeStruct(b.shape, b.dtype) for b in bufs],
        scratch_shapes=[pltpu.SemaphoreType.DMA((n,)), pltpu.SemaphoreType.DMA((n,))],
    )(*bufs)


def _as2d(a):
    return a.reshape(-1, a.shape[-1])


def _add_f32(a, b, name):
    (out,), _ = _rowwise(lambda r, s: ([r[0].astype(F32) + r[1].astype(F32)], []), [_as2d(a), _as2d(b)], [],
                         [(a.shape[-1], F32)], [], name)
    return out.reshape(a.shape)


def _reduce_scatter(parts, name):
    x, y, c = _coords()
    n = len(parts)
    pick = lambda p, idx, dim: lax.dynamic_index_in_dim(p, idx, dim, keepdims=False)

    def exchange(bufs, idxs, dims, axes, tag):
        keep = [pick(p, i_, d_) for p, i_, d_ in zip(bufs, idxs, dims)]
        send = [pick(p, 1 - i_, d_).astype(BF16) for p, i_, d_ in zip(bufs, idxs, dims)]
        got = _swap(send, axes, f"{name}_swap_{tag}")
        return [_add_f32(k, g, f"{name}_add_{tag}_{i}") for i, (k, g) in enumerate(zip(keep, got))]

    cur = exchange([p.reshape((2, 2, 2) + p.shape[1:]) for p in parts], [c] * n, [2] * n, ["c"] * n, "c")
    halves = [p[:, :, :p.shape[2] // 2] for p in cur] + [p[:, :, p.shape[2] // 2:] for p in cur]
    cur = exchange(halves, [y] * n + [x] * n, [1] * n + [0] * n, ["y"] * n + ["x"] * n, "yx")
    cur = exchange(cur, [x] * n + [y] * n, [0] * (2 * n), ["x"] * n + ["y"] * n, "xy")
    return [jnp.concatenate([cur[i], cur[n + i]], axis=0) for i in range(n)]


def _pack(arrs):
    flat = jnp.concatenate([a.reshape(-1).astype(F32) for a in arrs])
    rows = -(-flat.shape[0] // LANES)
    rows = -(-rows // 512) * 512
    return jnp.pad(flat, (0, rows * LANES - flat.shape[0])).reshape(rows, LANES)


def _unpack(buf, like):
    flat, out, off = buf.reshape(-1), [], 0
    for a in like:
        out.append(flat[off:off + a.size].reshape(a.shape))
        off += a.size
    return out


def _all_reduce(arrs, name):
    (g,) = _all_gather([_pack(arrs)], name + "_gather")
    (tot,), _ = _rowwise(lambda r, s: ([functools.reduce(lambda u, v: u + v, r)], []), [g[i] for i in range(N_DEV)], [],
                         [(LANES, F32)], [], name + "_sum")
    return _unpack(tot, arrs)


def _adamw(w, g, m, v, name):
    def fn(r, s):
        w_, g_, m_, v_ = r
        m_ = ADAM_B1 * m_ + (1.0 - ADAM_B1) * g_
        v_ = ADAM_B2 * v_ + (1.0 - ADAM_B2) * jnp.square(g_)
        m_hat = m_ / (1.0 - ADAM_B1 ** ADAM_STEP)
        v_hat = v_ / (1.0 - ADAM_B2 ** ADAM_STEP)
        return [-ADAM_LR * (m_hat / (jnp.sqrt(v_hat) + ADAM_EPS) + ADAM_WD * w_), m_, v_], []

    c = w.shape[-1]
    (d, nm, nv), _ = _rowwise(fn, [_as2d(a) for a in (w, g, m, v)], [], [(c, F32)] * 3, [], name)
    return d.reshape(w.shape), nm.reshape(w.shape), nv.reshape(w.shape)


def _norm_mod(x, gain, sc, sh, name):
    def fn(r, s):
        xv = r[0]
        xn = xv * lax.rsqrt(_rowmean(xv * xv) + EPS)
        return [xn * s[0] * (1.0 + s[1]) + s[2]], []
    (h,), _ = _rowwise(fn, [x], [gain, sc, sh], [(x.shape[1], BF16)], [], name)
    return h


def _norm_mod_bwd(x, dh, dres, gain, sc, name):
    d = x.shape[1]

    def fn(r, s):
        xv, dhv = r[0], r[1]
        rs = lax.rsqrt(_rowmean(xv * xv) + EPS)
        xn = xv * rs
        dxn = dhv * (1.0 + s[1]) * s[0]
        dx = rs * (dxn - xn * _rowmean(dxn * xn))
        if dres is not None:
            dx = dx + r[2]
        return [dx], [_colsum(dhv), _colsum(dhv * xn * s[0]), _colsum(dhv * (1.0 + s[1]) * xn)]
    (dx,), accs = _rowwise(fn, [x, dh] + ([dres] if dres is not None else []), [gain, sc], [(d, F32)], [d, d, d], name)
    return dx, accs


def _ada_rows(mod_row, d):
    return [mod_row[:, i * d:(i + 1) * d] for i in range(N_MOD)]


def _to_col_major(t):
    n, c = t.shape
    return t.reshape(n // GRID_W, GRID_W, c).transpose(1, 0, 2).reshape(n, c)


def _to_row_major(t):
    n, c = t.shape
    return t.reshape(GRID_W, n // GRID_W, c).transpose(1, 0, 2).reshape(n, c)


def kernel(x, c, ctx, c_ctx, w_ada, b_ada, norm_mix, norm_ffn, w_in, ssd_conv_w, ssd_conv_b, ssd_dt_bias, ssd_a_log, ssd_d, ssd_norm, w_out_ssd, lru_conv_w, lru_conv_b, lru_w_a, lru_b_a, lru_w_x, lru_b_x, lru_lambda, w_out_lru, w_gate, b_gate, w_o, ffn_w13, ffn_w2, final_norm, loss_target, m_c_ctx, m_w_ada, m_b_ada, m_norm_mix, m_norm_ffn, m_w_in, m_ssd_conv_w, m_ssd_conv_b, m_ssd_dt_bias, m_ssd_a_log, m_ssd_d, m_ssd_norm, m_w_out_ssd, m_lru_conv_w, m_lru_conv_b, m_lru_w_a, m_lru_b_a, m_lru_w_x, m_lru_b_x, m_lru_lambda, m_w_out_lru, m_w_gate, m_b_gate, m_w_o, m_ffn_w13, m_ffn_w2, m_final_norm, v_c_ctx, v_w_ada, v_b_ada, v_norm_mix, v_norm_ffn, v_w_in, v_ssd_conv_w, v_ssd_conv_b, v_ssd_dt_bias, v_ssd_a_log, v_ssd_d, v_ssd_norm, v_w_out_ssd, v_lru_conv_w, v_lru_conv_b, v_lru_w_a, v_lru_b_a, v_lru_w_x, v_lru_b_x, v_lru_lambda, v_w_out_lru, v_w_gate, v_b_gate, v_w_o, v_ffn_w13, v_ffn_w2, v_final_norm):
    names = ["c_ctx", "w_ada", "b_ada", "norm_mix", "norm_ffn", "w_in", "ssd_conv_w", "ssd_conv_b", "ssd_dt_bias", "ssd_a_log",
             "ssd_d", "ssd_norm", "w_out_ssd", "lru_conv_w", "lru_conv_b", "lru_w_a", "lru_b_a", "lru_w_x", "lru_b_x", "lru_lambda",
             "w_out_lru", "w_gate", "b_gate", "w_o", "ffn_w13", "ffn_w2", "final_norm"]
    wts = dict(zip(names, (c_ctx, w_ada, b_ada, norm_mix, norm_ffn, w_in, ssd_conv_w, ssd_conv_b, ssd_dt_bias, ssd_a_log, ssd_d, ssd_norm, w_out_ssd, lru_conv_w, lru_conv_b, lru_w_a, lru_b_a, lru_w_x, lru_b_x, lru_lambda, w_out_lru, w_gate, b_gate, w_o, ffn_w13, ffn_w2, final_norm)))
    mom1 = dict(zip(names, (m_c_ctx, m_w_ada, m_b_ada, m_norm_mix, m_norm_ffn, m_w_in, m_ssd_conv_w, m_ssd_conv_b, m_ssd_dt_bias, m_ssd_a_log, m_ssd_d, m_ssd_norm, m_w_out_ssd, m_lru_conv_w, m_lru_conv_b, m_lru_w_a, m_lru_b_a, m_lru_w_x, m_lru_b_x, m_lru_lambda, m_w_out_lru, m_w_gate, m_b_gate, m_w_o, m_ffn_w13, m_ffn_w2, m_final_norm)))
    mom2 = dict(zip(names, (v_c_ctx, v_w_ada, v_b_ada, v_norm_mix, v_norm_ffn, v_w_in, v_ssd_conv_w, v_ssd_conv_b, v_ssd_dt_bias, v_ssd_a_log, v_ssd_d, v_ssd_norm, v_w_out_ssd, v_lru_conv_w, v_lru_conv_b, v_lru_w_a, v_lru_b_a, v_lru_w_x, v_lru_b_x, v_lru_lambda, v_w_out_lru, v_w_gate, v_b_gate, v_w_o, v_ffn_w13, v_ffn_w2, v_final_norm)))

    xi, yi, ci_ = _coords()
    me = 4 * xi + 2 * yi + ci_
    x2, tgt, ctx2 = x[0], loss_target[0], ctx[0]
    t, d = x2.shape
    tc = ctx2.shape[0]
    ws = N_DEV * w_out_ssd.shape[1]
    wl = N_DEV * w_out_lru.shape[1]
    heads = ws // SSD_HEAD_DIM
    gn = SSD_GROUPS * SSD_STATE
    nh = wl // LRU_HEAD_DIM
    ff = N_DEV * ffn_w2.shape[1]
    ada_cols = w_ada.shape[2]

    big = [w_in[0], w_out_ssd[0], w_out_lru[0], w_gate[0], w_o[0], ffn_w13[0], ffn_w2[0]]
    smalls = [c, ssd_conv_w[0], lru_conv_w[0], lru_b_a[0], lru_b_x[0], lru_lambda[0]]
    got = _all_gather(smalls + [b.astype(BF16) for b in big], "gather_weights")
    c_all = got[0].reshape(N_DEV, d)
    colcat = lambda g: jnp.moveaxis(g, 0, -2).reshape(g.shape[1:-1] + (N_DEV * g.shape[-1],))
    conv_w_s, conv_w_l, lru_ba, lru_bx, lru_lam = (colcat(g) for g in got[1:6])
    wg_in, wg_os, wg_ol, wg_gate, wg_o, wg_13, wg_2 = got[6:]
    w_in_f = colcat(wg_in)
    o_z, o_xbc, o_dt, o_xr, o_yr = 0, ws, 2 * ws + 2 * gn, 2 * ws + 2 * gn + 2 * heads, 2 * ws + 2 * gn + 2 * heads + wl
    w_z, w_xbc = w_in_f[:, o_z:o_xbc], w_in_f[:, o_xbc:o_dt]
    w_dt = jnp.pad(w_in_f[:, o_dt:o_xr], ((0, 0), (0, LANES - 2 * heads)))
    w_xr, w_yr = w_in_f[:, o_xr:o_yr], w_in_f[:, o_yr:]
    w_os, w_ol, w_oo, w_2 = wg_os.reshape(ws, d), wg_ol.reshape(wl, d), wg_o.reshape(d, d), wg_2.reshape(ff, d)
    w_gf = colcat(wg_gate)
    w_gs, w_gr = w_gf[:, :d], w_gf[:, d:]
    w_13f = colcat(wg_13)
    w_1, w_3 = w_13f[:, :ff], w_13f[:, ff:]

    cond = jnp.concatenate([c_all, c_ctx[None, :], jnp.zeros((16 - N_DEV - 1, d), F32)], axis=0)
    cond_act = _silu(cond)
    mod_l = _mm([(cond_act, w_ada[0])], "nn", F32, "ada_fwd")
    (mod_g,) = _all_gather([mod_l], "gather_mod")
    mod_full = colcat(mod_g) + b_ada
    sh_m, sc_m, g_m, sh_f, sc_f, g_f = _ada_rows(lax.dynamic_slice_in_dim(mod_full, me, 1, 0), d)
    csh_m, csc_m = _ada_rows(mod_full[N_DEV:N_DEV + 1], d)[:2]

    bias_s, alog_s = ssd_dt_bias[0], ssd_a_log[0]
    d_full = jnp.repeat(ssd_d[0], SSD_HEAD_DIM)[None, :]
    zeros_state = jnp.zeros((heads // 2, SSD_STATE, LANES), F32)
    zeros_l = jnp.zeros((1, wl), F32)
    wa, wx = lru_w_a[0], lru_w_x[0]

    def mixer_inputs(hh, tag, full):
        xbc = _mm([(hh, w_xbc)], "nn", F32, f"{tag}_proj_xbc")
        dtr = _mm([(hh, w_dt)], "nn", F32, f"{tag}_proj_dt")[:, :2 * heads]
        xr = _mm([(hh, w_xr)], "nn", F32, f"{tag}_proj_xr")
        cx = _conv(xbc, conv_w_s, ssd_conv_b, CONV_LEFT, True, f"{tag}_ssd_conv")
        xs_, bm_, cm_ = cx[:, :ws], cx[:, ws:ws + gn], cx[:, ws + gn:]
        xr_seq = _to_col_major(xr) if full else xr
        xc = _conv(xr_seq, conv_w_l, lru_conv_b, CONV_LEFT, False, f"{tag}_lru_conv")
        return xbc, dtr, xr_seq, xs_, bm_, cm_, xc

    def lru_args(dirn):
        return wa[dirn], wx[dirn], lru_ba[dirn:dirn + 1], lru_bx[dirn:dirn + 1], lru_lam[dirn:dirn + 1]

    hc = _norm_mod(ctx2, norm_mix, csc_m, csh_m, "ctx_norm")
    c_xbc, c_dtr, c_xr_seq, c_xs, c_bm, c_cm, c_xc = mixer_inputs(hc, "ctx", False)
    c_ssd = []
    for dirn in (0, 1):
        _, se, sf = _ssd_fwd(c_xs, c_bm, c_cm, c_dtr[:, dirn * heads:(dirn + 1) * heads], bias_s[dirn:dirn + 1], alog_s[dirn:dirn + 1],
                             zeros_state, bool(dirn), f"ctx_ssd_fwd{dirn}")
        c_ssd.append((se, sf))
    c_lru = []
    for dirn in (0, 1):
        res = _lru_fwd(c_xc, *lru_args(dirn), zeros_l, None, bool(dirn), f"ctx_lru_fwd{dirn}")
        c_lru.append((res[0], res[1]))

    h = _norm_mod(x2, norm_mix, sc_m, sh_m, "lat_norm")
    z = _mm([(h, w_z)], "nn", BF16, "lat_proj_z")
    yr = _mm([(h, w_yr)], "nn", BF16, "lat_proj_yr")
    gp_s = _mm([(h, w_gs)], "nn", BF16, "lat_proj_gs")
    gp_r = _mm([(h, w_gr)], "nn", BF16, "lat_proj_gr")
    l_xbc, l_dtr, l_xr_seq, l_xs, l_bm, l_cm, l_xc = mixer_inputs(h, "lat", True)
    l_ssd = []
    for dirn in (0, 1):
        yv, se, _ = _ssd_fwd(l_xs, l_bm, l_cm, l_dtr[:, dirn * heads:(dirn + 1) * heads], bias_s[dirn:dirn + 1], alog_s[dirn:dirn + 1],
                             c_ssd[dirn][1], bool(dirn), f"lat_ssd_fwd{dirn}")
        l_ssd.append((yv, se))

    def ssd_out(r, s):
        yf, yb, xs_, z_ = r
        yv = yf + yb + s[0] * xs_
        yz = yv * _silu(z_)
        return [yz * lax.rsqrt(_rowmean(yz * yz) + EPS) * s[1]], []
    (ys,), _ = _rowwise(ssd_out, [l_ssd[0][0], l_ssd[1][0], l_xs, z], [d_full, ssd_norm], [(ws, BF16)], [], "lat_ssd_out")
    o_s = _mm([(ys, w_os)], "nn", BF16, "lat_out_ssd")

    hf, _ = _lru_fwd(l_xc, *lru_args(0), c_lru[0][1], None, False, "lat_lru_fwd0")
    hb, _, r_seq = _lru_fwd(l_xc, *lru_args(1), c_lru[1][1], hf, True, "lat_lru_fwd1")
    r_out = _to_row_major(r_seq)
    (gr,), _ = _rowwise(lambda r, s: ([r[0] * _gelu(r[1])], []), [r_out, yr], [], [(wl, BF16)], [], "lat_lru_out")
    o_r = _mm([(gr, w_ol)], "nn", BF16, "lat_out_lru")

    bg_s, bg_r = b_gate[:, :d], b_gate[:, d:]
    (mix,), _ = _rowwise(lambda r, s: ([_sigmoid(r[0] + s[0]) * r[2] + _sigmoid(r[1] + s[1]) * r[3]], []),
                         [gp_s, gp_r, o_s, o_r], [bg_s, bg_r], [(d, BF16)], [], "lat_merge")
    mixo = _mm([(mix, w_oo)], "nn", F32, "lat_out_o")

    def resid_norm(r, s):
        x1v = r[0] + s[0] * r[1]
        xn = x1v * lax.rsqrt(_rowmean(x1v * x1v) + EPS)
        return [x1v, xn * s[1] * (1.0 + s[2]) + s[3]], []
    (x1, h2), _ = _rowwise(resid_norm, [x2, mixo], [g_m, norm_ffn, sc_f, sh_f], [(d, F32), (d, BF16)], [], "lat_resid1")
    fg = _mm([(h2, w_1)], "nn", BF16, "ffn_gate")
    fu = _mm([(h2, w_3)], "nn", BF16, "ffn_up")
    (act,), _ = _rowwise(lambda r, s: ([_silu(r[0]) * r[1]], []), [fg, fu], [], [(ff, BF16)], [], "ffn_act")
    fo = _mm([(act, w_2)], "nn", F32, "ffn_down")

    fn_gain = final_norm[None, :]

    def head(r, s):
        x1v, fv, tv = r
        xv = x1v + s[0] * fv
        rs = lax.rsqrt(_rowmean(xv * xv) + EPS)
        xn = xv * rs
        err = xn * s[1] - tv
        dyv = err * (1.0 / d)
        dxn = dyv * s[1]
        dxv = rs * (dxn - xn * _rowmean(dxn * xn))
        return [dxv, dxv * s[0]], [_colsum(err * err), _colsum(dyv * xn), _colsum(dxv * fv)]
    (dx2, dfo), (sq, g_final, dg_f) = _rowwise(head, [x1, fo, tgt], [g_f, fn_gain], [(d, F32), (d, BF16)], [d, d, d], "loss_head")
    loss = lax.psum(0.5 * jnp.sum(sq) / d, ("x", "y", "c"))

    dact = _mm([(dfo, w_2)], "nt", BF16, "ffn_down_dx")
    gw_2 = _mm([(act, dfo)], "tn", F32, "ffn_down_dw")
    (dfg, dfu), _ = _rowwise(lambda r, s: ([r[0] * r[2] * _dsilu(r[1]), r[0] * _silu(r[1])], []), [dact, fg, fu], [],
                             [(ff, BF16), (ff, BF16)], [], "ffn_act_bwd")
    gw_1 = _mm([(h2, dfg)], "tn", F32, "ffn_gate_dw")
    gw_3 = _mm([(h2, dfu)], "tn", F32, "ffn_up_dw")
    dh2 = _mm([(dfg, w_1), (dfu, w_3)], "nt", F32, "ffn_dx")
    dx1, (dsh_f, dsc_f, g_norm_ffn) = _norm_mod_bwd(x1, dh2, dx2, norm_ffn, sc_f, "lat_norm_ffn_bwd")

    (dmixo,), (dg_m,) = _rowwise(lambda r, s: ([s[0] * r[0]], [_colsum(r[0] * r[1])]), [dx1, mixo], [g_m], [(d, BF16)], [d], "lat_resid1_bwd")
    dmix = _mm([(dmixo, w_oo)], "nt", BF16, "lat_out_o_dx")
    gw_o = _mm([(mix, dmixo)], "tn", F32, "lat_out_o_dw")

    def merge_bwd(r, s):
        dm, gps, gpr, os_, or_ = r
        g1, g2 = _sigmoid(gps + s[0]), _sigmoid(gpr + s[1])
        dg1, dg2 = dm * os_ * g1 * (1.0 - g1), dm * or_ * g2 * (1.0 - g2)
        return [dm * g1, dm * g2, dg1, dg2], [_colsum(dg1), _colsum(dg2)]
    (do_s, do_r, dgp_s, dgp_r), (gb_gs, gb_gr) = _rowwise(merge_bwd, [dmix, gp_s, gp_r, o_s, o_r], [bg_s, bg_r],
                                                            [(d, BF16)] * 4, [d, d], "lat_merge_bwd")
    gw_gs = _mm([(h, dgp_s)], "tn", F32, "lat_gs_dw")
    gw_gr = _mm([(h, dgp_r)], "tn", F32, "lat_gr_dw")

    dgr = _mm([(do_r, w_ol)], "nt", BF16, "lat_out_lru_dx")
    gw_ol = _mm([(gr, do_r)], "tn", F32, "lat_out_lru_dw")
    (dr_out, dyr), _ = _rowwise(lambda r, s: ([r[0] * _gelu(r[2]), r[0] * r[1] * _dgelu(r[2])], []), [dgr, r_out, yr], [],
                                [(wl, F32), (wl, BF16)], [], "lat_lru_out_bwd")
    dr_seq = _to_col_major(dr_out)
    lb0 = _lru_bwd(l_xc, *lru_args(0), c_lru[0][1], hf, dr_seq, zeros_l, None, False, "lat_lru_bwd0")
    lb1 = _lru_bwd(l_xc, *lru_args(1), c_lru[1][1], hb, dr_seq, zeros_l, lb0[0], True, "lat_lru_bwd1")
    dxr_seq, g_lconv_w, g_lconv_b = _conv_bwd(l_xr_seq, lb1[0], conv_w_l, lru_conv_b, False, "lat_lru_conv_bwd")
    dxr = _to_row_major(dxr_seq)

    dys = _mm([(do_s, w_os)], "nt", BF16, "lat_out_ssd_dx")
    gw_os = _mm([(ys, do_s)], "tn", F32, "lat_out_ssd_dw")

    def ssd_out_bwd(r, s):
        yf, yb, xs_, z_, dyn = r
        yv = yf + yb + s[0] * xs_
        sz = _silu(z_)
        yz = yv * sz
        rs = lax.rsqrt(_rowmean(yz * yz) + EPS)
        yzn = yz * rs
        dyzn = dyn * s[1]
        dyz = rs * (dyzn - yzn * _rowmean(dyzn * yzn))
        dyv = dyz * sz
        return [dyv, dyz * yv * _dsilu(z_), dyv * s[0]], [_colsum(dyv * xs_), _colsum(dyn * yzn)]
    (dy_ssd, dz, dxs_skip), (g_dfull, g_ssd_norm) = _rowwise(ssd_out_bwd, [l_ssd[0][0], l_ssd[1][0], l_xs, z, dys], [d_full, ssd_norm],
                                                              [(ws, F32), (ws, BF16), (ws, F32)], [ws, ws], "lat_ssd_out_bwd")
    sb = [_ssd_bwd(l_xs, l_bm, l_cm, l_dtr[:, dirn * heads:(dirn + 1) * heads], bias_s[dirn:dirn + 1], alog_s[dirn:dirn + 1],
                   dy_ssd, l_ssd[dirn][1], zeros_state, bool(dirn), f"lat_ssd_bwd{dirn}") for dirn in (0, 1)]
    (dcx,), _ = _rowwise(lambda r, s: ([jnp.concatenate([r[0] + r[1] + r[2], r[3] + r[4], r[5] + r[6]], axis=1)], []),
                         [dxs_skip, sb[0][0], sb[1][0], sb[0][1], sb[1][1], sb[0][2], sb[1][2]], [], [(ws + 2 * gn, F32)], [], "lat_ssd_dsum")
    dxbc, g_sconv_w, g_sconv_b = _conv_bwd(l_xbc, dcx, conv_w_s, ssd_conv_b, True, "lat_ssd_conv_bwd")
    ddt = jnp.pad(jnp.concatenate([sb[0][3], sb[1][3]], axis=1), ((0, 0), (0, LANES - 2 * heads)))

    zeros_cy = jnp.zeros((tc, ws), F32)
    csb = [_ssd_bwd(c_xs, c_bm, c_cm, c_dtr[:, dirn * heads:(dirn + 1) * heads], bias_s[dirn:dirn + 1], alog_s[dirn:dirn + 1],
                    zeros_cy, c_ssd[dirn][0], sb[dirn][6], bool(dirn), f"ctx_ssd_bwd{dirn}") for dirn in (0, 1)]
    (c_dcx,), _ = _rowwise(lambda r, s: ([jnp.concatenate([r[0] + r[1], r[2] + r[3], r[4] + r[5]], axis=1)], []),
                           [csb[0][0], csb[1][0], csb[0][1], csb[1][1], csb[0][2], csb[1][2]], [], [(ws + 2 * gn, F32)], [], "ctx_ssd_dsum")
    c_dxbc, cg_sconv_w, cg_sconv_b = _conv_bwd(c_xbc, c_dcx, conv_w_s, ssd_conv_b, True, "ctx_ssd_conv_bwd")
    c_ddt = jnp.pad(jnp.concatenate([csb[0][3], csb[1][3]], axis=1), ((0, 0), (0, LANES - 2 * heads)))
    zeros_ch = jnp.zeros((tc, wl), F32)
    clb0 = _lru_bwd(c_xc, *lru_args(0), zeros_l, c_lru[0][0], zeros_ch, lb0[6], None, False, "ctx_lru_bwd0")
    clb1 = _lru_bwd(c_xc, *lru_args(1), zeros_l, c_lru[1][0], zeros_ch, lb1[6], clb0[0], True, "ctx_lru_bwd1")
    c_dxr, cg_lconv_w, cg_lconv_b = _conv_bwd(c_xr_seq, clb1[0], conv_w_l, lru_conv_b, False, "ctx_lru_conv_bwd")
    dhc = _mm([(c_dxbc, w_xbc), (c_ddt, w_dt), (c_dxr, w_xr)], "nt", F32, "ctx_proj_dx")
    _, (dcsh_m, dcsc_m, cg_norm_mix) = _norm_mod_bwd(ctx2, dhc, None, norm_mix, csc_m, "ctx_norm_bwd")

    gw_z = _mm([(h, dz)], "tn", F32, "lat_z_dw")
    gw_yr = _mm([(h, dyr)], "tn", F32, "lat_yr_dw")
    gw_xbc = _mm([(h, dxbc), (hc, c_dxbc)], "tn", F32, "proj_xbc_dw")
    gw_dt = _mm([(h, ddt), (hc, c_ddt)], "tn", F32, "proj_dt_dw")[:, :2 * heads]
    gw_xr = _mm([(h, dxr), (hc, c_dxr)], "tn", F32, "proj_xr_dw")
    dh = _mm([(dz, w_z), (dxbc, w_xbc), (ddt, w_dt)], "nt", F32, "lat_proj_dx0")
    dh = _mm([(dxr, w_xr), (dyr, w_yr)], "nt", F32, "lat_proj_dx1", init=dh)
    dh = _mm([(dgp_s, w_gs), (dgp_r, w_gr)], "nt", F32, "lat_proj_dx2", init=dh)
    grad_x, (dsh_m, dsc_m, g_norm_mix) = _norm_mod_bwd(x2, dh, dx1, norm_mix, sc_m, "lat_norm_bwd")

    zero_d = jnp.zeros((1, d), F32)
    dmod_rows = jnp.concatenate([jnp.concatenate([dsh_m, dsc_m, dg_m, dsh_f, dsc_f, dg_f], axis=1),
                                 jnp.concatenate([dcsh_m, dcsc_m, zero_d, zero_d, zero_d, zero_d], axis=1)], axis=0)
    (dmod_g,) = _all_gather([dmod_rows], "gather_dmod")
    dmod_c = functools.reduce(lambda u, v: u + v, [dmod_g[i, 1] for i in range(N_DEV)])
    dmod9 = jnp.concatenate([dmod_g[:, 0], dmod_c[None, :], jnp.zeros((16 - N_DEV - 1, N_MOD * d), F32)], axis=0)
    g_b_ada = jnp.sum(dmod9, axis=0, keepdims=True)
    dmod_l = lax.dynamic_slice_in_dim(dmod9, me * ada_cols, ada_cols, 1)
    g_w_ada = _mm([(cond_act, dmod_l)], "tn", F32, "ada_dw")
    dcond_l = _mm([(dmod_l, w_ada[0])], "nt", F32, "ada_dx")

    small_parts = [
        dcond_l[N_DEV:N_DEV + 1], g_norm_mix + cg_norm_mix, g_norm_ffn, g_sconv_w + cg_sconv_w, g_sconv_b + cg_sconv_b,
        jnp.concatenate([sb[0][5] + csb[0][5], sb[1][5] + csb[1][5]], axis=0),
        jnp.concatenate([sb[0][4] + csb[0][4], sb[1][4] + csb[1][4]], axis=0),
        g_dfull, g_ssd_norm, g_lconv_w + cg_lconv_w, g_lconv_b + cg_lconv_b,
        jnp.stack([lb0[1] + clb0[1], lb1[1] + clb1[1]]), jnp.concatenate([lb0[3] + clb0[3], lb1[3] + clb1[3]], axis=0),
        jnp.stack([lb0[2] + clb0[2], lb1[2] + clb1[2]]), jnp.concatenate([lb0[4] + clb0[4], lb1[4] + clb1[4]], axis=0),
        jnp.concatenate([lb0[5] + clb0[5], lb1[5] + clb1[5]], axis=0),
        jnp.concatenate([gb_gs, gb_gr], axis=1), g_final,
    ]
    (s_cctx, s_nmix, s_nffn, s_sconv_w, s_sconv_b, s_dtb, s_da, s_dfull, s_snorm, s_lconv_w, s_lconv_b,
     s_wa, s_ba, s_wx, s_bx, s_dsp, s_bgate, s_final) = _all_reduce(small_parts, "small_grads")
    shard = lambda a, n_: lax.dynamic_slice_in_dim(a, me * n_, n_, a.ndim - 1)
    grads = {
        "c_ctx": (s_cctx * _dsilu(c_ctx[None, :]))[0],
        "w_ada": g_w_ada[None], "b_ada": g_b_ada, "norm_mix": s_nmix, "norm_ffn": s_nffn,
        "ssd_conv_w": shard(s_sconv_w, ssd_conv_w.shape[2])[None], "ssd_conv_b": s_sconv_b,
        "ssd_dt_bias": s_dtb[None], "ssd_a_log": (s_da * (-jnp.exp(alog_s)))[None],
        "ssd_d": jnp.sum(s_dfull.reshape(heads, SSD_HEAD_DIM), axis=1)[None], "ssd_norm": s_snorm,
        "lru_conv_w": shard(s_lconv_w, lru_conv_w.shape[2])[None], "lru_conv_b": s_lconv_b,
        "lru_w_a": s_wa[None], "lru_b_a": shard(s_ba, lru_b_a.shape[2])[None],
        "lru_w_x": s_wx[None], "lru_b_x": shard(s_bx, lru_b_x.shape[2])[None],
        "lru_lambda": shard(s_dsp * (-_sigmoid(-lru_lam)), lru_lambda.shape[2])[None],
        "b_gate": s_bgate, "final_norm": s_final[0],
    }

    slots = lambda g: jnp.moveaxis(g.reshape(g.shape[0], N_DEV, g.shape[1] // N_DEV), 1, 0)
    rslots = lambda g: g.reshape(N_DEV, g.shape[0] // N_DEV, g.shape[1])
    gw_in = jnp.concatenate([gw_z, gw_xbc, gw_dt, gw_xr, gw_yr], axis=1)
    parts = [slots(gw_in), rslots(gw_os), rslots(gw_ol), slots(jnp.concatenate([gw_gs, gw_gr], axis=1)), rslots(gw_o),
             slots(jnp.concatenate([gw_1, gw_3], axis=1)), rslots(gw_2)]
    for n_, g in zip(["w_in", "w_out_ssd", "w_out_lru", "w_gate", "w_o", "ffn_w13", "ffn_w2"], _reduce_scatter(parts, "grad_rs")):
        grads[n_] = g[None]

    big_names = ["w_ada", "w_in", "w_out_ssd", "w_out_lru", "w_gate", "w_o", "ffn_w13", "ffn_w2"]
    small_names = [n_ for n_ in names if n_ not in big_names]
    delta, new_m, new_v = {}, {}, {}
    for n_ in big_names:
        delta[n_], new_m[n_], new_v[n_] = _adamw(wts[n_], grads[n_], mom1[n_], mom2[n_], "adamw_" + n_)
    packed = [_pack([src[n_] for n_ in small_names]) for src in (wts, grads, mom1, mom2)]
    outs = _adamw(*packed, "adamw_small")
    like = [wts[n_] for n_ in small_names]
    for dst, buf in zip((delta, new_m, new_v), outs):
        dst.update(zip(small_names, _unpack(buf, like)))

    return (loss, grad_x[None], *[grads[n_].reshape(wts[n_].shape) for n_ in names], *[delta[n_] for n_ in names],
            *[new_m[n_] for n_ in names], *[new_v[n_] for n_ in names])
```

```python
import functools

import numpy as np
import jax
import jax.numpy as jnp
from jax import lax
from jax.experimental import pallas as pl
from jax.experimental.pallas import tpu as pltpu

F32, BF16 = jnp.float32, jnp.bfloat16
MESH = pl.DeviceIdType.MESH
N_DEV = 8
VMEM_LIMIT_BYTES = 48 * 1024 * 1024
ROW_BUDGET_BYTES = 12 * 1024 * 1024
LANES, SUBLANES = 128, 8

EPS = 1e-6
GRID_W = 64
SSD_HEAD_DIM = 64
SSD_GROUPS = 4
SSD_STATE = 128
SSD_CHUNK = 128
CONV_W = 4
CONV_LEFT = 2
LRU_HEAD_DIM = 128
LRU_C = 8.0
LRU_BLOCK = 128
N_MOD = 6
ADAM_LR, ADAM_B1, ADAM_B2, ADAM_EPS, ADAM_WD, ADAM_STEP = 0.001, 0.9, 0.999, 1e-08, 0.01, 10
NEG_BIG = -1e30


def _params(sem):
    return pltpu.CompilerParams(dimension_semantics=sem, vmem_limit_bytes=VMEM_LIMIT_BYTES)


def _tile(dim, pref, align):
    if dim <= pref:
        return dim
    t = (pref // align) * align
    while t >= align:
        if dim % t == 0:
            return t
        t -= align
    return dim


class _Rider:
    def __init__(self, ins, out_shape, aliases, sems, start, wait):
        self.ins, self.out_shape, self.aliases, self.sems, self.start, self.wait = ins, out_shape, aliases, sems, start, wait


def _call(body, name, grid, in_specs, out_specs, out_shape, scratch_shapes, sem, operands, rider=None):
    if rider is None:
        return pl.pallas_call(body, name=name, grid=grid, in_specs=in_specs, out_specs=out_specs, out_shape=out_shape,
                              scratch_shapes=scratch_shapes, compiler_params=_params(sem))(*operands)
    n_in, n_out, n_scr = len(in_specs), len(out_specs), len(scratch_shapes)
    ri, ro = len(rider.ins), len(rider.out_shape)
    a, b = n_in, n_in + ri
    c, d = b + n_out, b + n_out + ro
    e = d + n_scr

    def carried(*refs):
        ids = [pl.program_id(i) for i in range(len(grid))]
        first = functools.reduce(lambda u, v: u & v, [i == 0 for i in ids])
        last = functools.reduce(lambda u, v: u & v, [i == g - 1 for i, g in zip(ids, grid)])

        @pl.when(first)
        def _():
            rider.start(refs[a:b], refs[c:d], refs[e:])

        body(*refs[:a], *refs[b:c], *refs[d:e])

        @pl.when(last)
        def _():
            rider.wait(refs[a:b], refs[c:d], refs[e:])

    any_spec = pl.BlockSpec(memory_space=pl.ANY)
    res = pl.pallas_call(
        carried, name=name, grid=grid, in_specs=list(in_specs) + [any_spec] * ri, out_specs=list(out_specs) + [any_spec] * ro,
        out_shape=list(out_shape) + list(rider.out_shape), scratch_shapes=list(scratch_shapes) + list(rider.sems),
        input_output_aliases={n_in + i: n_out + o for i, o in rider.aliases.items()},
        compiler_params=_params(("arbitrary",) * len(grid)),
    )(*operands, *rider.ins)
    return res[:n_out], res[n_out:]


def _bdot(a, b):
    return jnp.dot(a.astype(BF16), b.astype(BF16), preferred_element_type=F32)


def _bdot_nt(a, b):
    return lax.dot_general(a.astype(BF16), b.astype(BF16), (((1,), (1,)), ((), ())), preferred_element_type=F32)


def _bdot_tn(a, b):
    return lax.dot_general(a.astype(BF16), b.astype(BF16), (((0,), (0,)), ((), ())), preferred_element_type=F32)


def _hdot(a, b):
    return jnp.dot(a, b, preferred_element_type=F32, precision=lax.Precision.HIGHEST)


def _softplus(v):
    return jnp.maximum(v, 0.0) + jnp.log(1.0 + jnp.exp(-jnp.abs(v)))


def _sigmoid(v):
    return 0.5 * jnp.tanh(0.5 * v) + 0.5


def _silu(v):
    return v * _sigmoid(v)


def _dsilu(v):
    s = _sigmoid(v)
    return s * (1.0 + v * (1.0 - s))


GELU_C0 = 0.7978845608028654
GELU_C1 = 0.044715


def _gelu(v):
    return 0.5 * v * (1.0 + jnp.tanh(GELU_C0 * (v + GELU_C1 * v * v * v)))


def _dgelu(v):
    t = jnp.tanh(GELU_C0 * (v + GELU_C1 * v * v * v))
    return 0.5 * (1.0 + t) + 0.5 * v * (1.0 - t * t) * GELU_C0 * (1.0 + 3.0 * GELU_C1 * v * v)


MM_TM, MM_TN = 1024, 1408
MM_TK = {1: 2048, 2: 1536}


def _mm(pairs, mode, out_dtype, name, init=None, rider=None):
    a0, b0 = pairs[0]
    if mode == "nn":
        m, n, ks = a0.shape[0], b0.shape[1], [a.shape[1] for a, _ in pairs]
    elif mode == "nt":
        m, n, ks = a0.shape[0], b0.shape[0], [a.shape[1] for a, _ in pairs]
    else:
        m, n, ks = a0.shape[1], b0.shape[1], [a.shape[0] for a, _ in pairs]
    tm = _tile(m, MM_TM, LANES if mode == "tn" else 16)
    tn = _tile(n, MM_TN, LANES)
    tk = MM_TK.get(sum(k > 512 for k in ks), 512)
    tks = [_tile(k, tk, 16 if mode == "tn" else LANES) for k in ks]
    steps = [k // t for k, t in zip(ks, tks)]
    offs = [sum(steps[:i]) for i in range(len(pairs))]
    total = sum(steps)
    npairs = len(pairs)

    def kk(i):
        return lambda k: jnp.clip(k - offs[i], 0, steps[i] - 1)

    in_specs, operands = [], []
    for i, (a, b) in enumerate(pairs):
        kf = kk(i)
        if mode == "nn":
            in_specs.append(pl.BlockSpec((tm, tks[i]), lambda mi, ni, k, kf=kf: (mi, kf(k))))
            in_specs.append(pl.BlockSpec((tks[i], tn), lambda mi, ni, k, kf=kf: (kf(k), ni)))
        elif mode == "nt":
            in_specs.append(pl.BlockSpec((tm, tks[i]), lambda mi, ni, k, kf=kf: (mi, kf(k))))
            in_specs.append(pl.BlockSpec((tn, tks[i]), lambda mi, ni, k, kf=kf: (ni, kf(k))))
        else:
            in_specs.append(pl.BlockSpec((tks[i], tm), lambda mi, ni, k, kf=kf: (kf(k), mi)))
            in_specs.append(pl.BlockSpec((tks[i], tn), lambda mi, ni, k, kf=kf: (kf(k), ni)))
        operands += [a, b]
    dot = {"nn": _bdot, "nt": _bdot_nt, "tn": _bdot_tn}[mode]
    if init is not None:
        in_specs.append(pl.BlockSpec((tm, tn), lambda mi, ni, k: (mi, ni)))
        operands.append(init)
    assert out_dtype == F32 or total == 1

    def body(*refs):
        out_ref = refs[-1]
        k = pl.program_id(2)

        def first_step():
            p = dot(refs[0][...], refs[1][...])
            return p if init is None else p + refs[2 * npairs][...]

        if total == 1:
            out_ref[...] = first_step().astype(out_dtype)
            return

        @pl.when(k == 0)
        def _():
            out_ref[...] = first_step()

        for i in range(npairs):
            first = 1 if i == 0 else offs[i]
            if offs[i] + steps[i] > first:
                @pl.when((k >= first) & (k < offs[i] + steps[i]))
                def _(i=i):
                    out_ref[...] += dot(refs[2 * i][...], refs[2 * i + 1][...])

    res = _call(body, name, (m // tm, n // tn, total), in_specs, [pl.BlockSpec((tm, tn), lambda mi, ni, k: (mi, ni))],
                [jax.ShapeDtypeStruct((m, n), out_dtype)], [], ("parallel", "parallel", "arbitrary"), operands, rider)
    return res[0] if rider is None else (res[0][0], res[1])


def _rowwise(fn, rows, small, outs, accs, name):
    t = rows[0].shape[0]
    per_row = sum(r.shape[1] * r.dtype.itemsize for r in rows) + sum(c * jnp.dtype(d).itemsize for c, d in outs)
    tm = SUBLANES
    while tm * 2 <= 1024 and t % (tm * 2) == 0 and tm * 2 * per_row <= ROW_BUDGET_BYTES:
        tm *= 2
    if t % tm:
        tm = t
    nrow, nsmall, nout, nacc = len(rows), len(small), len(outs), len(accs)

    def body(*refs):
        rv = [r[...].astype(F32) for r in refs[:nrow]]
        sv = [r[...] for r in refs[nrow:nrow + nsmall]]
        o_refs = refs[nrow + nsmall:nrow + nsmall + nout]
        a_refs = refs[nrow + nsmall + nout:]
        ov, av = fn(rv, sv)
        for r, v in zip(o_refs, ov):
            r[...] = v.astype(r.dtype)
        if nacc:
            @pl.when(pl.program_id(0) == 0)
            def _():
                for r in a_refs:
                    r[...] = jnp.zeros_like(r)
            for r, v in zip(a_refs, av):
                r[...] += v

    in_specs = [pl.BlockSpec((tm, r.shape[1]), lambda i: (i, 0)) for r in rows]
    in_specs += [pl.BlockSpec(s.shape, lambda i: (0, 0)) for s in small]
    out_specs = [pl.BlockSpec((tm, c), lambda i: (i, 0)) for c, _ in outs]
    out_specs += [pl.BlockSpec((1, c), lambda i: (0, 0)) for c in accs]
    out_shape = [jax.ShapeDtypeStruct((t, c), d) for c, d in outs]
    out_shape += [jax.ShapeDtypeStruct((1, c), F32) for c in accs]
    res = pl.pallas_call(
        body, name=name, grid=(t // tm,), in_specs=in_specs, out_specs=out_specs, out_shape=out_shape,
        compiler_params=_params(("arbitrary",)),
    )(*rows, *small)
    return res[:nout], res[nout:]


def _colsum(v):
    return jnp.sum(v, axis=0, keepdims=True)


def _rowmean(v):
    return jnp.mean(v, axis=1, keepdims=True)


def _conv_tiles(x):
    t, c = x.shape
    tm = _tile(t, 256, SUBLANES)
    nt = t // tm
    r8 = tm // SUBLANES
    specs = [
        pl.BlockSpec((SUBLANES, c), lambda i: (jnp.maximum(i * r8 - 1, 0), 0)),
        pl.BlockSpec((tm, c), lambda i: (i, 0)),
        pl.BlockSpec((SUBLANES, c), lambda i: (jnp.minimum((i + 1) * r8, t // SUBLANES - 1), 0)),
    ]
    return tm, nt, specs


def _conv_taps(prev_ref, cur_ref, next_ref, tm, nt, left):
    i = pl.program_id(0)
    p = jnp.where(i > 0, prev_ref[...], 0.0)
    q = jnp.where(i < nt - 1, next_ref[...], 0.0)
    ext = jnp.concatenate([p, cur_ref[...], q], axis=0)
    n = tm + 2 * SUBLANES
    taps = []
    for k in range(CONV_W):
        s = (left - k) % n
        r = pltpu.roll(ext, s, axis=0) if s else ext
        taps.append(r[SUBLANES:SUBLANES + tm])
    return taps


def _conv(x, w, b, left, act, name, out_dtype=F32):
    t, c = x.shape
    tm, nt, specs = _conv_tiles(x)

    def body(prev_ref, cur_ref, next_ref, w_ref, b_ref, o_ref):
        taps = _conv_taps(prev_ref, cur_ref, next_ref, tm, nt, left)
        pre = b_ref[...] + sum(w_ref[k:k + 1, :] * taps[k] for k in range(CONV_W))
        o_ref[...] = (_silu(pre) if act else pre).astype(out_dtype)

    return pl.pallas_call(
        body, name=name, grid=(nt,),
        in_specs=specs + [pl.BlockSpec((CONV_W, c), lambda i: (0, 0)), pl.BlockSpec((1, c), lambda i: (0, 0))],
        out_specs=pl.BlockSpec((tm, c), lambda i: (i, 0)),
        out_shape=jax.ShapeDtypeStruct((t, c), out_dtype), compiler_params=_params(("arbitrary",)),
    )(x, x, x, w, b)


def _conv_bwd(x, dy, w, b, act, name):
    t, c = x.shape
    tm, nt, specs = _conv_tiles(x)

    def body(prev_ref, cur_ref, next_ref, dy_ref, w_ref, b_ref, dpre_ref, dw0, dw1, dw2, dw3, db_ref):
        taps = _conv_taps(prev_ref, cur_ref, next_ref, tm, nt, CONV_LEFT)
        if act:
            pre = b_ref[...] + sum(w_ref[k:k + 1, :] * taps[k] for k in range(CONV_W))
            dpre = dy_ref[...] * _dsilu(pre)
        else:
            dpre = dy_ref[...]
        dpre_ref[...] = dpre

        @pl.when(pl.program_id(0) == 0)
        def _():
            for r in (dw0, dw1, dw2, dw3, db_ref):
                r[...] = jnp.zeros_like(r)

        for k, r in enumerate((dw0, dw1, dw2, dw3)):
            r[...] += _colsum(dpre * taps[k])
        db_ref[...] += _colsum(dpre)

    row = pl.BlockSpec((1, c), lambda i: (0, 0))
    res = pl.pallas_call(
        body, name=name + "_pre", grid=(nt,),
        in_specs=specs + [pl.BlockSpec((tm, c), lambda i: (i, 0)), pl.BlockSpec((CONV_W, c), lambda i: (0, 0)), row],
        out_specs=[pl.BlockSpec((tm, c), lambda i: (i, 0))] + [row] * 5,
        out_shape=[jax.ShapeDtypeStruct((t, c), F32)] + [jax.ShapeDtypeStruct((1, c), F32)] * 5,
        compiler_params=_params(("arbitrary",)),
    )(x, x, x, dy, w, b)
    dpre, dw, db = res[0], jnp.concatenate(res[1:5], axis=0), res[5]
    dx = _conv(dpre, w[::-1], jnp.zeros_like(b), CONV_W - 1 - CONV_LEFT, False, name + "_dx", BF16)
    return dx, dw, db


def _ssd_dims(xs, bm):
    t, ws = xs.shape
    heads = ws // SSD_HEAD_DIM
    hpg = heads // SSD_GROUPS
    assert hpg % 2 == 0 and bm.shape[1] == SSD_GROUPS * SSD_STATE and t % SSD_CHUNK == 0
    return t, ws, heads, hpg, heads // 2, t // SSD_CHUNK


def _ssd_chunk_terms(dtr_ref, dtrt_ref, bias_ref, biast_ref, alog_ref, alogt_ref, reverse):
    q = SSD_CHUNK
    a_neg = -jnp.exp(alog_ref[...])
    dt = _softplus(dtr_ref[...] + bias_ref[...])
    dtt = _softplus(dtrt_ref[...] + biast_ref[...])
    a = dt * a_neg
    at = dtt * (-jnp.exp(alogt_ref[...]))
    ii = lax.broadcasted_iota(jnp.int32, (q, q), 0)
    jj = lax.broadcasted_iota(jnp.int32, (q, q), 1)
    mask = (ii <= jj) if reverse else (ii >= jj)
    tri = mask.astype(F32)
    trit = ((jj <= ii) if reverse else (jj >= ii)).astype(F32)
    before = (jj > ii) if reverse else (jj < ii)
    ac = _hdot(tri, a)
    act = _hdot(at, trit)
    tot = _colsum(a)
    return a_neg, dt, a, ac, act, tot, mask, trit, before


def _pair_terms(lo, h1, dt, ac, tot):
    q = SSD_CHUNK

    def colb(v, h):
        return jnp.broadcast_to(v[:, h:h + 1], (v.shape[0], LANES))

    def pairb(v):
        return jnp.where(lo[:v.shape[0]], colb(v, h1), colb(v, h1 + 1))

    dt_p = pairb(dt)
    ac_p = pairb(ac)
    eac_p = jnp.exp(ac_p)
    dte_p = jnp.exp(pairb(jnp.broadcast_to(tot, (q, tot.shape[1]))) - ac_p)
    etot_p = jnp.exp(pairb(tot))
    return dt_p, eac_p, dte_p, etot_p


def _decay_mats(mask, ac, act, cb, h1):
    q = SSD_CHUNK
    out = []
    for h in (h1, h1 + 1):
        diff = jnp.broadcast_to(ac[:, h:h + 1], (q, q)) - jnp.broadcast_to(act[h:h + 1, :], (q, q))
        lmat = jnp.exp(jnp.where(mask, diff, NEG_BIG))
        out.append((lmat, cb * lmat))
    return out


def _ssd_specs(t, ws, heads, npairs, nc, reverse):
    q, gn = SSD_CHUNK, SSD_GROUPS * SSD_STATE
    ci = (lambda k: nc - 1 - k) if reverse else (lambda k: k)
    small = lambda shape: pl.BlockSpec(shape, lambda k: (0,) * len(shape))
    seq = [
        pl.BlockSpec((q, ws), lambda k: (ci(k), 0)),
        pl.BlockSpec((q, gn), lambda k: (ci(k), 0)),
        pl.BlockSpec((q, gn), lambda k: (ci(k), 0)),
        pl.BlockSpec((q, heads), lambda k: (ci(k), 0)),
        pl.BlockSpec((heads, q), lambda k: (0, ci(k))),
        small((1, heads)), small((heads, 1)), small((1, heads)), small((heads, 1)),
    ]
    return ci, small, seq


def _ssd_fwd(xs, bm, cm, dtr, bias, alog, s0, reverse, name, rider=None):
    t, ws, heads, hpg, npairs, nc = _ssd_dims(xs, bm)
    q, ns = SSD_CHUNK, SSD_STATE
    ci, small, seq = _ssd_specs(t, ws, heads, npairs, nc, reverse)

    def body(xs_ref, b_ref, c_ref, dtr_ref, dtrt_ref, bias_ref, biast_ref, alog_ref, alogt_ref, s0_ref,
             y_ref, se_ref, s_ref):
        @pl.when(pl.program_id(0) == 0)
        def _():
            s_ref[...] = s0_ref[...]

        _, dt, _, ac, act, tot, mask, _, _ = _ssd_chunk_terms(dtr_ref, dtrt_ref, bias_ref, biast_ref, alog_ref, alogt_ref, reverse)
        lo = lax.broadcasted_iota(jnp.int32, (q, LANES), 1) < SSD_HEAD_DIM
        for g in range(SSD_GROUPS):
            bg = b_ref[:, g * ns:(g + 1) * ns]
            cg = c_ref[:, g * ns:(g + 1) * ns]
            cb = _bdot_nt(cg, bg)
            bgt = bg.T
            for pq in range(hpg // 2):
                pp = g * (hpg // 2) + pq
                h1 = 2 * pp
                cs = slice(pp * LANES, (pp + 1) * LANES)
                (_, m1), (_, m2) = _decay_mats(mask, ac, act, cb, h1)
                dt_p, eac_p, dte_p, etot_p = _pair_terms(lo, h1, dt, ac, tot)
                xp = xs_ref[:, cs] * dt_p
                sp = s_ref[pp]
                se_ref[0, pp] = sp
                y_ref[:, cs] = jnp.where(lo, _bdot(m1, xp), _bdot(m2, xp)) + eac_p * _bdot(cg, sp)
                s_ref[pp] = etot_p * sp + _bdot(bgt, xp * dte_p)

    st = (npairs, ns, LANES)
    return _call(
        body, name, (nc,), seq + [small(st)],
        [pl.BlockSpec((q, ws), lambda k: (ci(k), 0)), pl.BlockSpec((1,) + st, lambda k: (ci(k), 0, 0, 0)), small(st)],
        [jax.ShapeDtypeStruct((t, ws), F32), jax.ShapeDtypeStruct((nc,) + st, F32), jax.ShapeDtypeStruct(st, F32)],
        [], ("arbitrary",), (xs, bm, cm, dtr, dtr.T, bias, bias.T, alog, alog.T, s0), rider)


def _ssd_bwd(xs, bm, cm, dtr, bias, alog, dy, s_enter, lam0, reverse, name):
    t, ws, heads, hpg, npairs, nc = _ssd_dims(xs, bm)
    q, ns, gn = SSD_CHUNK, SSD_STATE, SSD_GROUPS * SSD_STATE
    ci, small, seq = _ssd_specs(t, ws, heads, npairs, nc, not reverse)
    st = (npairs, ns, LANES)

    e_heads = jnp.asarray(np.arange(ws)[:, None] // SSD_HEAD_DIM == np.arange(heads)[None, :], BF16)

    def body(xs_ref, b_ref, c_ref, dtr_ref, dtrt_ref, bias_ref, biast_ref, alog_ref, alogt_ref, dy_ref, se_ref, lam0_ref, e_ref,
             dxs_ref, db_ref, dc_ref, ddtr_ref, da_ref, dbias_ref, lam_ref, st_s, ts_s, ss_s, xx_s):
        @pl.when(pl.program_id(0) == 0)
        def _():
            lam_ref[...] = lam0_ref[...]
            da_ref[...] = jnp.zeros_like(da_ref)
            dbias_ref[...] = jnp.zeros_like(dbias_ref)

        a_neg, dt, _, ac, act, tot, mask, trit, before = _ssd_chunk_terms(dtr_ref, dtrt_ref, bias_ref, biast_ref, alog_ref, alogt_ref, reverse)
        lo = lax.broadcasted_iota(jnp.int32, (q, LANES), 1) < SSD_HEAD_DIM
        head_col = lax.broadcasted_iota(jnp.int32, (heads, 1), 0)
        ii = lax.broadcasted_iota(jnp.int32, (q, q), 0)
        jj = lax.broadcasted_iota(jnp.int32, (q, q), 1)
        before_t = ((ii > jj) if reverse else (ii < jj)).astype(F32)

        def per_head(v):
            hi = v.astype(BF16)
            lo = (v - hi.astype(F32)).astype(BF16)
            sel = e_ref[...]
            return jnp.dot(hi, sel, preferred_element_type=F32) + jnp.dot(lo, sel, preferred_element_type=F32)

        da_tt = jnp.zeros((heads, q), F32)
        for g in range(SSD_GROUPS):
            gs = slice(g * ns, (g + 1) * ns)
            bg, cg = b_ref[:, gs], c_ref[:, gs]
            cb = _bdot_nt(cg, bg)
            cgt = cg.T
            dcb = jnp.zeros((q, q), F32)
            dbg = jnp.zeros((q, ns), F32)
            dcg = jnp.zeros((q, ns), F32)
            for pq in range(hpg // 2):
                pp = g * (hpg // 2) + pq
                h1 = 2 * pp
                cs = slice(pp * LANES, (pp + 1) * LANES)
                (l1, m1), (l2, m2) = _decay_mats(mask, ac, act, cb, h1)
                dt_p, eac_p, dte_p, etot_p = _pair_terms(lo, h1, dt, ac, tot)
                xsp = xs_ref[:, cs]
                xp = xsp * dt_p
                dyp = dy_ref[:, cs]
                lam = lam_ref[pp]
                sp = se_ref[0, pp]
                dx_state = dte_p * _bdot(bg, lam)
                dxdt = jnp.where(lo, _bdot(m1.T, dyp), _bdot(m2.T, dyp)) + dx_state
                dyx1 = _bdot_nt(jnp.where(lo, dyp, 0.0), xp)
                dyx2 = _bdot_nt(jnp.where(lo, 0.0, dyp), xp)
                dcb = dcb + l1 * dyx1 + l2 * dyx2
                for h, pair in ((h1, m1 * dyx1), (h1 + 1, m2 * dyx2)):
                    cross = _colsum(jnp.where(mask, _bdot(pair, before_t), 0.0))
                    da_tt = da_tt + (head_col == h).astype(F32) * cross
                st_s[:, cs] = dyp * (eac_p * _bdot(cg, sp))
                ts_s[:, cs] = dx_state * xp
                ss_s[:, cs] = jnp.broadcast_to(_colsum(lam * sp) * etot_p, (SUBLANES, LANES))
                xx_s[:, cs] = dxdt * xsp
                edy = eac_p * dyp
                xe = dte_p * xp
                dcg = dcg + _bdot_nt(edy, sp)
                dbg = dbg + _bdot_nt(xe, lam)
                lam_ref[pp] = etot_p * lam + _bdot(cgt, edy)
                dxs_ref[:, cs] = dxdt * dt_p
            db_ref[:, gs] = dbg + _bdot(dcb.T, cg)
            dc_ref[:, gs] = dcg + _bdot(dcb, bg)
        da = da_tt.T + _hdot(trit, per_head(st_s[...])) + _hdot(before.astype(F32), per_head(ts_s[...])) + per_head(ss_s[...])[0:1]
        ddtr = (da * a_neg + per_head(xx_s[...])) * _sigmoid(dtr_ref[...] + bias_ref[...])
        ddtr_ref[...] = ddtr
        da_ref[...] += _colsum(da * dt)
        dbias_ref[...] += _colsum(ddtr)

    row = lambda c: pl.BlockSpec((q, c), lambda k: (ci(k), 0))
    return pl.pallas_call(
        body, name=name, grid=(nc,),
        in_specs=seq + [row(ws), pl.BlockSpec((1,) + st, lambda k: (ci(k), 0, 0, 0)), small(st), small((ws, heads))],
        out_specs=[row(ws), row(gn), row(gn), row(heads), small((1, heads)), small((1, heads)), small(st)],
        out_shape=[jax.ShapeDtypeStruct((t, ws), F32), jax.ShapeDtypeStruct((t, gn), F32), jax.ShapeDtypeStruct((t, gn), F32),
                   jax.ShapeDtypeStruct((t, heads), F32), jax.ShapeDtypeStruct((1, heads), F32),
                   jax.ShapeDtypeStruct((1, heads), F32), jax.ShapeDtypeStruct(st, F32)],
        scratch_shapes=[pltpu.VMEM((q, ws), F32), pltpu.VMEM((q, ws), F32), pltpu.VMEM((SUBLANES, ws), F32), pltpu.VMEM((q, ws), F32)],
        compiler_params=_params(("arbitrary",)),
    )(xs, bm, cm, dtr, dtr.T, bias, bias.T, alog, alog.T, dy, s_enter, lam0, e_heads)


def _tile_scan(a_ref, u_ref, out_ref, carry_ref, ntiles, reverse):
    w = a_ref.shape[1]
    row = lax.broadcasted_iota(jnp.int32, (SUBLANES, w), 0)

    def step(j, _):
        r0 = pl.multiple_of((ntiles - 1 - j if reverse else j) * SUBLANES, SUBLANES)
        a = a_ref[pl.ds(r0, SUBLANES), :]
        u = u_ref[pl.ds(r0, SUBLANES), :]
        for d in (1, 2, 4):
            keep = (row < SUBLANES - d) if reverse else (row >= d)
            s = SUBLANES - d if reverse else d
            a_sh = jnp.where(keep, pltpu.roll(a, s, axis=0), 1.0)
            u_sh = jnp.where(keep, pltpu.roll(u, s, axis=0), 0.0)
            u = a * u_sh + u
            a = a * a_sh
        h = a * carry_ref[...] + u
        out_ref[pl.ds(r0, SUBLANES), :] = h
        last = h[0:1, :] if reverse else h[SUBLANES - 1:SUBLANES, :]
        carry_ref[...] = jnp.broadcast_to(last, (SUBLANES, w))
        return 0

    lax.fori_loop(0, ntiles, step, 0)


def _lru_gates(xh, wa, wx, ba, bx, lam):
    r = _sigmoid(_bdot(xh, wa) + ba)
    i = _sigmoid(_bdot(xh, wx) + bx)
    sp = _softplus(-lam)
    a = jnp.exp(-LRU_C * r * sp)
    return r, i, sp, a


def _lru_fwd(xc, wa, wx, ba, bx, lam, h0, addend, reverse, name, rider=None):
    t, w = xc.shape
    tb = LRU_BLOCK
    nb, nh = t // tb, w // LRU_HEAD_DIM
    bi = (lambda k: nb - 1 - k) if reverse else (lambda k: k)
    has_add = addend is not None

    def body(*refs):
        xc_ref, wa_ref, wx_ref, ba_ref, bx_ref, lam_ref, h0_ref = refs[:7]
        add_ref = refs[7] if has_add else None
        h_ref, hl_ref = refs[7 + has_add], refs[8 + has_add]
        sum_ref = refs[9 + has_add] if has_add else None
        a_s, u_s, carry = refs[-3:]

        @pl.when(pl.program_id(0) == 0)
        def _():
            carry[...] = jnp.broadcast_to(h0_ref[...], carry.shape)

        for hd in range(nh):
            hs = slice(hd * LRU_HEAD_DIM, (hd + 1) * LRU_HEAD_DIM)
            xh = xc_ref[:, hs]
            _, i, _, a = _lru_gates(xh, wa_ref[hd], wx_ref[hd], ba_ref[:, hs], bx_ref[:, hs], lam_ref[:, hs])
            a_s[:, hs] = a
            u_s[:, hs] = jnp.sqrt(1.0 - a * a) * (i * xh)
        _tile_scan(a_s, u_s, h_ref, carry, tb // SUBLANES, reverse)
        hl_ref[...] = carry[0:1, :]
        if has_add:
            sum_ref[...] = h_ref[...] + add_ref[...]

    blk = pl.BlockSpec((tb, w), lambda k: (bi(k), 0))
    vec = pl.BlockSpec((1, w), lambda k: (0, 0))
    wsp = pl.BlockSpec((nh, LRU_HEAD_DIM, LRU_HEAD_DIM), lambda k: (0, 0, 0))
    out_specs = [blk, vec] + ([blk] if has_add else [])
    out_shape = [jax.ShapeDtypeStruct((t, w), F32), jax.ShapeDtypeStruct((1, w), F32)] + ([jax.ShapeDtypeStruct((t, w), F32)] if has_add else [])
    return _call(
        body, name, (nb,), [blk, wsp, wsp, vec, vec, vec, vec] + ([blk] if has_add else []), out_specs, out_shape,
        [pltpu.VMEM((tb, w), F32), pltpu.VMEM((tb, w), F32), pltpu.VMEM((SUBLANES, w), F32)], ("arbitrary",),
        (xc, wa, wx, ba, bx, lam, h0, *([addend] if has_add else [])), rider)


def _lru_bwd(xc, wa, wx, ba, bx, lam, h0, h, dh, dfin, addend, reverse, name, rider=None):
    t, w = xc.shape
    tb = LRU_BLOCK
    nb, nh = t // tb, w // LRU_HEAD_DIM
    r8 = tb // SUBLANES
    bi = (lambda k: k) if reverse else (lambda k: nb - 1 - k)
    if reverse:
        halo = lambda k: (jnp.minimum((bi(k) + 1) * r8, t // SUBLANES - 1), 0)
    else:
        halo = lambda k: (jnp.maximum(bi(k) * r8 - 1, 0), 0)
    has_add = addend is not None
    n_ext = tb + SUBLANES

    def body(*refs):
        xc_ref, wa_ref, wx_ref, ba_ref, bx_ref, lam_ref, h0_ref, h_ref, halo_ref, dh_ref, dfin_ref = refs[:11]
        add_ref = refs[11] if has_add else None
        dxc_ref, dwa_ref, dwx_ref, dba_ref, dbx_ref, dsp_ref, dh0_ref = refs[11 + has_add:18 + has_add]
        a_s, v_s, mu_s, r_s, i_s, carry = refs[-6:]
        k = pl.program_id(0)

        @pl.when(k == 0)
        def _():
            carry[...] = jnp.broadcast_to(dfin_ref[...], carry.shape)
            for r in (dwa_ref, dwx_ref, dba_ref, dbx_ref, dsp_ref):
                r[...] = jnp.zeros_like(r)

        for hd in range(nh):
            hs = slice(hd * LRU_HEAD_DIM, (hd + 1) * LRU_HEAD_DIM)
            r, i, _, a = _lru_gates(xc_ref[:, hs], wa_ref[hd], wx_ref[hd], ba_ref[:, hs], bx_ref[:, hs], lam_ref[:, hs])
            a_s[:, hs] = a
            r_s[:, hs] = r
            i_s[:, hs] = i
            v_s[:, hs] = a * dh_ref[:, hs]
        mu_in = carry[...]
        _tile_scan(a_s, v_s, mu_s, carry, r8, not reverse)
        dh0_ref[...] = carry[0:1, :]
        first = (bi(k) == nb - 1) if reverse else (bi(k) == 0)
        edge = jnp.where(first, jnp.broadcast_to(h0_ref[...], (SUBLANES, w)), halo_ref[...])
        for hd in range(nh):
            hs = slice(hd * LRU_HEAD_DIM, (hd + 1) * LRU_HEAD_DIM)
            if reverse:
                mu_sh = pltpu.roll(jnp.concatenate([mu_in[:, hs], mu_s[:, hs]], axis=0), 1, axis=0)[SUBLANES:]
                h_sh = pltpu.roll(jnp.concatenate([h_ref[:, hs], edge[:, hs]], axis=0), n_ext - 1, axis=0)[:tb]
            else:
                mu_sh = pltpu.roll(jnp.concatenate([mu_s[:, hs], mu_in[:, hs]], axis=0), n_ext - 1, axis=0)[:tb]
                h_sh = pltpu.roll(jnp.concatenate([edge[:, hs], h_ref[:, hs]], axis=0), 1, axis=0)[SUBLANES:]
            xh = xc_ref[:, hs]
            r, i, a = r_s[:, hs], i_s[:, hs], a_s[:, hs]
            sp = _softplus(-lam_ref[:, hs])
            om = 1.0 - a * a
            rs = lax.rsqrt(om)
            lam_t = dh_ref[:, hs] + mu_sh
            dla = lam_t * h_sh * a - lam_t * (i * xh) * (a * a) * rs
            dix = lam_t * (om * rs)
            dpr = dla * (-LRU_C * sp) * r * (1.0 - r)
            dpi = dix * xh * i * (1.0 - i)
            dsp_ref[:, hs] += _colsum(dla * (-LRU_C * r))
            dba_ref[:, hs] += _colsum(dpr)
            dbx_ref[:, hs] += _colsum(dpi)
            dx = dix * i + _bdot_nt(dpr, wa_ref[hd]) + _bdot_nt(dpi, wx_ref[hd])
            xt = xh.T
            dwa_ref[hd] += _bdot(xt, dpr)
            dwx_ref[hd] += _bdot(xt, dpi)
            dxc_ref[:, hs] = dx + add_ref[:, hs] if has_add else dx

    blk = pl.BlockSpec((tb, w), lambda k: (bi(k), 0))
    vec = pl.BlockSpec((1, w), lambda k: (0, 0))
    wsp = pl.BlockSpec((nh, LRU_HEAD_DIM, LRU_HEAD_DIM), lambda k: (0, 0, 0))
    wshape = jax.ShapeDtypeStruct((nh, LRU_HEAD_DIM, LRU_HEAD_DIM), F32)
    vshape = jax.ShapeDtypeStruct((1, w), F32)
    return _call(
        body, name, (nb,),
        [blk, wsp, wsp, vec, vec, vec, vec, blk, pl.BlockSpec((SUBLANES, w), halo), blk, vec] + ([blk] if has_add else []),
        [blk, wsp, wsp, vec, vec, vec, vec],
        [jax.ShapeDtypeStruct((t, w), F32), wshape, wshape, vshape, vshape, vshape, vshape],
        [pltpu.VMEM((tb, w), F32)] * 5 + [pltpu.VMEM((SUBLANES, w), F32)], ("arbitrary",),
        (xc, wa, wx, ba, bx, lam, h0, h, h, dh, dfin, *([addend] if has_add else [])), rider)


def _coords():
    return lax.axis_index("x"), lax.axis_index("y"), lax.axis_index("c")


HALF_ROWS_ALIGN = 32


def _all_gather(shards, name):
    n = len(shards)
    start, wait, sems = _gather_rounds(shards)

    def body(*refs):
        for rnd in range(3):
            start(rnd, refs[:n], refs[n:2 * n], refs[2 * n:])
            wait(rnd, refs[:n], refs[n:2 * n], refs[2 * n:])

    any_spec = pl.BlockSpec(memory_space=pl.ANY)
    return pl.pallas_call(
        body, name=name, in_specs=[any_spec] * n, out_specs=[any_spec] * n,
        out_shape=[jax.ShapeDtypeStruct((N_DEV,) + s.shape, s.dtype) for s in shards], scratch_shapes=sems,
    )(*shards)


def _gather_riders(shards):
    n = len(shards)
    start, wait, sems = _gather_rounds(shards)
    out_shape = [jax.ShapeDtypeStruct((N_DEV,) + s.shape, s.dtype) for s in shards]

    def make(rnd, bufs=()):
        return _Rider(list(shards) + list(bufs), out_shape, {n + i: i for i in range(len(bufs))}, sems,
                      lambda r_in, r_out, s: start(rnd, r_in[:n], r_out, s), lambda r_in, r_out, s: wait(rnd, r_in[:n], r_out, s))
    return make


def _gather_rounds(shards):
    n = len(shards)
    split = [s.ndim >= 2 and s.shape[0] % HALF_ROWS_ALIGN == 0 for s in shards]
    n_copies = 10

    def plan(ins, outs, sem_refs):
        send_sems, recv_sems, local_sems = sem_refs
        x, y, c = _coords()
        slot = lambda p: 4 * p[0] + 2 * p[1] + p[2]
        me, xp, yp, dg = (x, y, c), (1 - x, y, c), (x, 1 - y, c), (1 - x, 1 - y, c)
        other_core = lambda p: (p[0], p[1], 1 - p[2])

        def rows(ref, i, half):
            r = shards[i].shape[0]
            if half == "all" or not split[i]:
                return ref
            return ref.at[pl.ds(0, r // 2)] if half == "a" else ref.at[pl.ds(r // 2, r // 2)]

        def copy(i, j, holder, half, to, own=False):
            dst = rows(outs[i].at[slot(holder)], i, half)
            src = rows(ins[i], i, half) if own else dst
            return pltpu.make_async_remote_copy(src_ref=src, dst_ref=dst, send_sem=send_sems.at[i, j],
                                                recv_sem=recv_sems.at[i, j], device_id=to, device_id_type=MESH)

        rounds = [
            ([(0, me, "a", xp, True), (1, me, "b", yp, True)],
             [(0, xp, "a"), (1, yp, "b")]),
            ([(2, me, "a", yp, True), (3, xp, "a", yp, False), (4, me, "b", xp, True), (5, yp, "b", xp, False)],
             [(2, yp, "a"), (3, dg, "a"), (4, xp, "b"), (5, dg, "b")]),
            ([(6, me, "all", other_core(me), True)] + [(7 + q, p, "all", other_core(me), False) for q, p in enumerate((xp, yp, dg))],
             [(6 + q, other_core(p), "all") for q, p in enumerate((me, xp, yp, dg))]),
        ]
        local = lambda: [pltpu.make_async_copy(ins[i], outs[i].at[slot(me)], local_sems.at[i]) for i in range(n)]
        travels = lambda i, half: half != "b" or split[i]
        return rounds, copy, local, travels, me

    def start(rnd, ins, outs, sem_refs):
        rounds, copy, local, travels, _ = plan(ins, outs, sem_refs)
        if rnd == 0:
            for cp in local():
                cp.start()
        for j, holder, half, to, own in rounds[rnd][0]:
            for i in range(n):
                if travels(i, half):
                    copy(i, j, holder, half, to, own).start()

    def wait(rnd, ins, outs, sem_refs):
        rounds, copy, local, travels, me = plan(ins, outs, sem_refs)
        for j, holder, half in rounds[rnd][1]:
            for i in range(n):
                if travels(i, half):
                    copy(i, j, holder, half, me).wait_recv()
        for j, holder, half, to, own in rounds[rnd][0]:
            for i in range(n):
                if travels(i, half):
                    copy(i, j, holder, half, to, own).wait_send()
        if rnd == 0:
            for cp in local():
                cp.wait()

    sems = [pltpu.SemaphoreType.DMA((n, n_copies)), pltpu.SemaphoreType.DMA((n, n_copies)), pltpu.SemaphoreType.DMA((n,))]
    return start, wait, sems


def _swap_rider(bufs, axes):
    n = len(bufs)

    def copies(ins, outs, sem_refs):
        x, y, c = _coords()
        to = {"x": (1 - x, y, c), "y": (x, 1 - y, c), "c": (x, y, 1 - c)}
        return [pltpu.make_async_remote_copy(src_ref=ins[i], dst_ref=outs[i], send_sem=sem_refs[0].at[i], recv_sem=sem_refs[1].at[i],
                                             device_id=to[axes[i]], device_id_type=MESH) for i in range(n)]

    def start(ins, outs, sem_refs):
        for cp in copies(ins, outs, sem_refs):
            cp.start()

    def wait(ins, outs, sem_refs):
        for cp in copies(ins, outs, sem_refs):
            cp.wait()

    return _Rider(list(bufs), [jax.ShapeDtypeStruct(b.shape, b.dtype) for b in bufs], {},
                  [pltpu.SemaphoreType.DMA((n,)), pltpu.SemaphoreType.DMA((n,))], start, wait)


def _swap(bufs, axes, name):
    n = len(bufs)
    rider = _swap_rider(bufs, axes)

    def body(*refs):
        rider.start(refs[:n], refs[n:2 * n], refs[2 * n:])
        rider.wait(refs[:n], refs[n:2 * n], refs[2 * n:])

    any_spec = pl.BlockSpec(memory_space=pl.ANY)
    return pl.pallas_call(body, name=name, in_specs=[any_spec] * n, out_specs=[any_spec] * n, out_shape=rider.out_shape,
                          scratch_shapes=rider.sems)(*bufs)


def _as2d(a):
    return a.reshape(-1, a.shape[-1])


def _add_f32(a, b, name):
    (out,), _ = _rowwise(lambda r, s: ([r[0].astype(F32) + r[1].astype(F32)], []), [_as2d(a), _as2d(b)], [],
                         [(a.shape[-1], F32)], [], name)
    return out.reshape(a.shape)


class _ReduceScatter:
    def __init__(self, parts, name):
        self.n, self.name, self.rnd = len(parts), name, 0
        self.cur = [p.reshape((2, 2, 2) + p.shape[1:]) for p in parts]

    def stage(self):
        x, y, c = _coords()
        n = self.n
        if self.rnd == 0:
            bufs, idxs, dims, axes = self.cur, [c] * n, [2] * n, ["c"] * n
        elif self.rnd == 1:
            bufs = [p[:, :, :p.shape[2] // 2] for p in self.cur] + [p[:, :, p.shape[2] // 2:] for p in self.cur]
            idxs, dims, axes = [y] * n + [x] * n, [1] * n + [0] * n, ["y"] * n + ["x"] * n
        else:
            bufs, idxs, dims, axes = self.cur, [x] * n + [y] * n, [0] * (2 * n), ["x"] * n + ["y"] * n
        pick = lambda p, idx, dim: lax.dynamic_index_in_dim(p, idx, dim, keepdims=False)
        self.keep = [pick(p, i_, d_) for p, i_, d_ in zip(bufs, idxs, dims)]
        return [pick(p, 1 - i_, d_).astype(BF16) for p, i_, d_ in zip(bufs, idxs, dims)], axes

    def absorb(self, got):
        self.cur = [_add_f32(k, g, f"{self.name}_add{self.rnd}_{i}") for i, (k, g) in enumerate(zip(self.keep, got))]
        self.rnd += 1

    def result(self):
        return [jnp.concatenate([self.cur[i], self.cur[self.n + i]], axis=0) for i in range(self.n)]


def _reduce_scatter(parts, name):
    rs = _ReduceScatter(parts, name)
    for rnd in range(3):
        send, axes = rs.stage()
        rs.absorb(_swap(send, axes, f"{name}_swap{rnd}"))
    return rs.result()


def _pack(arrs):
    flat = jnp.concatenate([a.reshape(-1).astype(F32) for a in arrs])
    rows = -(-flat.shape[0] // LANES)
    rows = -(-rows // 512) * 512
    return jnp.pad(flat, (0, rows * LANES - flat.shape[0])).reshape(rows, LANES)


def _unpack(buf, like):
    flat, out, off = buf.reshape(-1), [], 0
    for a in like:
        out.append(flat[off:off + a.size].reshape(a.shape))
        off += a.size
    return out


def _all_reduce(arrs, name):
    (g,) = _all_gather([_pack(arrs)], name + "_gather")
    (tot,), _ = _rowwise(lambda r, s: ([functools.reduce(lambda u, v: u + v, r)], []), [g[i] for i in range(N_DEV)], [],
                         [(LANES, F32)], [], name + "_sum")
    return _unpack(tot, arrs)


def _adamw(w, g, m, v, name):
    def fn(r, s):
        w_, g_, m_, v_ = r
        m_ = ADAM_B1 * m_ + (1.0 - ADAM_B1) * g_
        v_ = ADAM_B2 * v_ + (1.0 - ADAM_B2) * jnp.square(g_)
        m_hat = m_ / (1.0 - ADAM_B1 ** ADAM_STEP)
        v_hat = v_ / (1.0 - ADAM_B2 ** ADAM_STEP)
        return [-ADAM_LR * (m_hat / (jnp.sqrt(v_hat) + ADAM_EPS) + ADAM_WD * w_), m_, v_], []

    c = w.shape[-1]
    (d, nm, nv), _ = _rowwise(fn, [_as2d(a) for a in (w, g, m, v)], [], [(c, F32)] * 3, [], name)
    return d.reshape(w.shape), nm.reshape(w.shape), nv.reshape(w.shape)


def _norm_mod(x, gain, sc, sh, name):
    def fn(r, s):
        xv = r[0]
        xn = xv * lax.rsqrt(_rowmean(xv * xv) + EPS)
        return [xn * s[0] * (1.0 + s[1]) + s[2]], []
    (h,), _ = _rowwise(fn, [x], [gain, sc, sh], [(x.shape[1], BF16)], [], name)
    return h


def _norm_mod_bwd(x, dh, dres, gain, sc, name):
    d = x.shape[1]

    def fn(r, s):
        xv, dhv = r[0], r[1]
        rs = lax.rsqrt(_rowmean(xv * xv) + EPS)
        xn = xv * rs
        dxn = dhv * (1.0 + s[1]) * s[0]
        dx = rs * (dxn - xn * _rowmean(dxn * xn))
        if dres is not None:
            dx = dx + r[2]
        return [dx], [_colsum(dhv), _colsum(dhv * xn * s[0]), _colsum(dhv * (1.0 + s[1]) * xn)]
    (dx,), accs = _rowwise(fn, [x, dh] + ([dres] if dres is not None else []), [gain, sc], [(d, F32)], [d, d, d], name)
    return dx, accs


def _ada_rows(mod_row, d):
    return [mod_row[:, i * d:(i + 1) * d] for i in range(N_MOD)]


def _to_col_major(t):
    n, c = t.shape
    return t.reshape(n // GRID_W, GRID_W, c).transpose(1, 0, 2).reshape(n, c)


def _to_row_major(t):
    n, c = t.shape
    return t.reshape(GRID_W, n // GRID_W, c).transpose(1, 0, 2).reshape(n, c)


def kernel(x, c, ctx, c_ctx, w_ada, b_ada, norm_mix, norm_ffn, w_in, ssd_conv_w, ssd_conv_b, ssd_dt_bias, ssd_a_log, ssd_d, ssd_norm, w_out_ssd, lru_conv_w, lru_conv_b, lru_w_a, lru_b_a, lru_w_x, lru_b_x, lru_lambda, w_out_lru, w_gate, b_gate, w_o, ffn_w13, ffn_w2, final_norm, loss_target, m_c_ctx, m_w_ada, m_b_ada, m_norm_mix, m_norm_ffn, m_w_in, m_ssd_conv_w, m_ssd_conv_b, m_ssd_dt_bias, m_ssd_a_log, m_ssd_d, m_ssd_norm, m_w_out_ssd, m_lru_conv_w, m_lru_conv_b, m_lru_w_a, m_lru_b_a, m_lru_w_x, m_lru_b_x, m_lru_lambda, m_w_out_lru, m_w_gate, m_b_gate, m_w_o, m_ffn_w13, m_ffn_w2, m_final_norm, v_c_ctx, v_w_ada, v_b_ada, v_norm_mix, v_norm_ffn, v_w_in, v_ssd_conv_w, v_ssd_conv_b, v_ssd_dt_bias, v_ssd_a_log, v_ssd_d, v_ssd_norm, v_w_out_ssd, v_lru_conv_w, v_lru_conv_b, v_lru_w_a, v_lru_b_a, v_lru_w_x, v_lru_b_x, v_lru_lambda, v_w_out_lru, v_w_gate, v_b_gate, v_w_o, v_ffn_w13, v_ffn_w2, v_final_norm):
    names = ["c_ctx", "w_ada", "b_ada", "norm_mix", "norm_ffn", "w_in", "ssd_conv_w", "ssd_conv_b", "ssd_dt_bias", "ssd_a_log",
             "ssd_d", "ssd_norm", "w_out_ssd", "lru_conv_w", "lru_conv_b", "lru_w_a", "lru_b_a", "lru_w_x", "lru_b_x", "lru_lambda",
             "w_out_lru", "w_gate", "b_gate", "w_o", "ffn_w13", "ffn_w2", "final_norm"]
    wts = dict(zip(names, (c_ctx, w_ada, b_ada, norm_mix, norm_ffn, w_in, ssd_conv_w, ssd_conv_b, ssd_dt_bias, ssd_a_log, ssd_d, ssd_norm, w_out_ssd, lru_conv_w, lru_conv_b, lru_w_a, lru_b_a, lru_w_x, lru_b_x, lru_lambda, w_out_lru, w_gate, b_gate, w_o, ffn_w13, ffn_w2, final_norm)))
    mom1 = dict(zip(names, (m_c_ctx, m_w_ada, m_b_ada, m_norm_mix, m_norm_ffn, m_w_in, m_ssd_conv_w, m_ssd_conv_b, m_ssd_dt_bias, m_ssd_a_log, m_ssd_d, m_ssd_norm, m_w_out_ssd, m_lru_conv_w, m_lru_conv_b, m_lru_w_a, m_lru_b_a, m_lru_w_x, m_lru_b_x, m_lru_lambda, m_w_out_lru, m_w_gate, m_b_gate, m_w_o, m_ffn_w13, m_ffn_w2, m_final_norm)))
    mom2 = dict(zip(names, (v_c_ctx, v_w_ada, v_b_ada, v_norm_mix, v_norm_ffn, v_w_in, v_ssd_conv_w, v_ssd_conv_b, v_ssd_dt_bias, v_ssd_a_log, v_ssd_d, v_ssd_norm, v_w_out_ssd, v_lru_conv_w, v_lru_conv_b, v_lru_w_a, v_lru_b_a, v_lru_w_x, v_lru_b_x, v_lru_lambda, v_w_out_lru, v_w_gate, v_b_gate, v_w_o, v_ffn_w13, v_ffn_w2, v_final_norm)))

    xi, yi, ci_ = _coords()
    me = 4 * xi + 2 * yi + ci_
    x2, tgt, ctx2 = x[0], loss_target[0], ctx[0]
    t, d = x2.shape
    tc = ctx2.shape[0]
    ws = N_DEV * w_out_ssd.shape[1]
    wl = N_DEV * w_out_lru.shape[1]
    heads = ws // SSD_HEAD_DIM
    gn = SSD_GROUPS * SSD_STATE
    nh = wl // LRU_HEAD_DIM
    ff = N_DEV * ffn_w2.shape[1]
    ada_cols = w_ada.shape[2]

    smalls = [c, ssd_conv_w[0], lru_conv_w[0], lru_b_a[0], lru_b_x[0], lru_lambda[0]]
    got = _all_gather(smalls + [w_in[0].astype(BF16), w_gate[0].astype(BF16)], "gather_weights")
    late_gather = _gather_riders([w.astype(BF16) for w in (w_out_ssd[0], w_out_lru[0], w_o[0], ffn_w13[0], ffn_w2[0])])
    c_all = got[0].reshape(N_DEV, d)
    colcat = lambda g: jnp.moveaxis(g, 0, -2).reshape(g.shape[1:-1] + (N_DEV * g.shape[-1],))
    conv_w_s, conv_w_l, lru_ba, lru_bx, lru_lam = (colcat(g) for g in got[1:6])
    wg_in, wg_gate = got[6:]
    w_in_f = colcat(wg_in)
    o_z, o_xbc, o_dt, o_xr, o_yr = 0, ws, 2 * ws + 2 * gn, 2 * ws + 2 * gn + 2 * heads, 2 * ws + 2 * gn + 2 * heads + wl
    w_z, w_xbc = w_in_f[:, o_z:o_xbc], w_in_f[:, o_xbc:o_dt]
    w_dt = jnp.pad(w_in_f[:, o_dt:o_xr], ((0, 0), (0, LANES - 2 * heads)))
    w_xr, w_yr = w_in_f[:, o_xr:o_yr], w_in_f[:, o_yr:]
    w_gf = colcat(wg_gate)
    w_gs, w_gr = w_gf[:, :d], w_gf[:, d:]

    cond = jnp.concatenate([c_all, c_ctx[None, :], jnp.zeros((16 - N_DEV - 1, d), F32)], axis=0)
    cond_act = _silu(cond)
    mod_l = _mm([(cond_act, w_ada[0])], "nn", F32, "ada_fwd")
    (mod_g,) = _all_gather([mod_l], "gather_mod")
    mod_full = colcat(mod_g) + b_ada
    sh_m, sc_m, g_m, sh_f, sc_f, g_f = _ada_rows(lax.dynamic_slice_in_dim(mod_full, me, 1, 0), d)
    csh_m, csc_m = _ada_rows(mod_full[N_DEV:N_DEV + 1], d)[:2]

    bias_s, alog_s = ssd_dt_bias[0], ssd_a_log[0]
    d_full = jnp.repeat(ssd_d[0], SSD_HEAD_DIM)[None, :]
    zeros_state = jnp.zeros((heads // 2, SSD_STATE, LANES), F32)
    zeros_l = jnp.zeros((1, wl), F32)
    wa, wx = lru_w_a[0], lru_w_x[0]

    def mixer_inputs(hh, tag, full):
        xbc = _mm([(hh, w_xbc)], "nn", F32, f"{tag}_proj_xbc")
        dtr = _mm([(hh, w_dt)], "nn", F32, f"{tag}_proj_dt")[:, :2 * heads]
        xr = _mm([(hh, w_xr)], "nn", F32, f"{tag}_proj_xr")
        cx = _conv(xbc, conv_w_s, ssd_conv_b, CONV_LEFT, True, f"{tag}_ssd_conv")
        xs_, bm_, cm_ = cx[:, :ws], cx[:, ws:ws + gn], cx[:, ws + gn:]
        xr_seq = _to_col_major(xr) if full else xr
        xc = _conv(xr_seq, conv_w_l, lru_conv_b, CONV_LEFT, False, f"{tag}_lru_conv")
        return xbc, dtr, xr_seq, xs_, bm_, cm_, xc

    def lru_args(dirn):
        return wa[dirn], wx[dirn], lru_ba[dirn:dirn + 1], lru_bx[dirn:dirn + 1], lru_lam[dirn:dirn + 1]

    hc = _norm_mod(ctx2, norm_mix, csc_m, csh_m, "ctx_norm")
    c_xbc, c_dtr, c_xr_seq, c_xs, c_bm, c_cm, c_xc = mixer_inputs(hc, "ctx", False)
    c_ssd = []
    for dirn in (0, 1):
        _, se, sf = _ssd_fwd(c_xs, c_bm, c_cm, c_dtr[:, dirn * heads:(dirn + 1) * heads], bias_s[dirn:dirn + 1], alog_s[dirn:dirn + 1],
                             zeros_state, bool(dirn), f"ctx_ssd_fwd{dirn}")
        c_ssd.append((se, sf))
    c_lru = []
    for dirn in (0, 1):
        res = _lru_fwd(c_xc, *lru_args(dirn), zeros_l, None, bool(dirn), f"ctx_lru_fwd{dirn}")
        c_lru.append((res[0], res[1]))

    h = _norm_mod(x2, norm_mix, sc_m, sh_m, "lat_norm")
    z = _mm([(h, w_z)], "nn", BF16, "lat_proj_z")
    yr = _mm([(h, w_yr)], "nn", BF16, "lat_proj_yr")
    gp_s = _mm([(h, w_gs)], "nn", BF16, "lat_proj_gs")
    gp_r = _mm([(h, w_gr)], "nn", BF16, "lat_proj_gr")
    l_xbc, l_dtr, l_xr_seq, l_xs, l_bm, l_cm, l_xc = mixer_inputs(h, "lat", True)
    l_ssd, late = [], ()
    for dirn in (0, 1):
        (yv, se, _), late = _ssd_fwd(l_xs, l_bm, l_cm, l_dtr[:, dirn * heads:(dirn + 1) * heads], bias_s[dirn:dirn + 1], alog_s[dirn:dirn + 1],
                                     c_ssd[dirn][1], bool(dirn), f"lat_ssd_fwd{dirn}", rider=late_gather(dirn, late))
        l_ssd.append((yv, se))
    (hf, _), late = _lru_fwd(l_xc, *lru_args(0), c_lru[0][1], None, False, "lat_lru_fwd0", rider=late_gather(2, late))
    hb, _, r_seq = _lru_fwd(l_xc, *lru_args(1), c_lru[1][1], hf, True, "lat_lru_fwd1")
    wg_os, wg_ol, wg_o, wg_13, wg_2 = late
    w_os, w_ol, w_oo, w_2 = wg_os.reshape(ws, d), wg_ol.reshape(wl, d), wg_o.reshape(d, d), wg_2.reshape(ff, d)
    w_13f = colcat(wg_13)
    w_1, w_3 = w_13f[:, :ff], w_13f[:, ff:]

    def ssd_out(r, s):
        yf, yb, xs_, z_ = r
        yv = yf + yb + s[0] * xs_
        yz = yv * _silu(z_)
        return [yz * lax.rsqrt(_rowmean(yz * yz) + EPS) * s[1]], []
    (ys,), _ = _rowwise(ssd_out, [l_ssd[0][0], l_ssd[1][0], l_xs, z], [d_full, ssd_norm], [(ws, BF16)], [], "lat_ssd_out")
    o_s = _mm([(ys, w_os)], "nn", BF16, "lat_out_ssd")

    r_out = _to_row_major(r_seq)
    (gr,), _ = _rowwise(lambda r, s: ([r[0] * _gelu(r[1])], []), [r_out, yr], [], [(wl, BF16)], [], "lat_lru_out")
    o_r = _mm([(gr, w_ol)], "nn", BF16, "lat_out_lru")

    bg_s, bg_r = b_gate[:, :d], b_gate[:, d:]
    (mix,), _ = _rowwise(lambda r, s: ([_sigmoid(r[0] + s[0]) * r[2] + _sigmoid(r[1] + s[1]) * r[3]], []),
                         [gp_s, gp_r, o_s, o_r], [bg_s, bg_r], [(d, BF16)], [], "lat_merge")
    mixo = _mm([(mix, w_oo)], "nn", F32, "lat_out_o")

    def resid_norm(r, s):
        x1v = r[0] + s[0] * r[1]
        xn = x1v * lax.rsqrt(_rowmean(x1v * x1v) + EPS)
        return [x1v, xn * s[1] * (1.0 + s[2]) + s[3]], []
    (x1, h2), _ = _rowwise(resid_norm, [x2, mixo], [g_m, norm_ffn, sc_f, sh_f], [(d, F32), (d, BF16)], [], "lat_resid1")
    fg = _mm([(h2, w_1)], "nn", BF16, "ffn_gate")
    fu = _mm([(h2, w_3)], "nn", BF16, "ffn_up")
    (act,), _ = _rowwise(lambda r, s: ([_silu(r[0]) * r[1]], []), [fg, fu], [], [(ff, BF16)], [], "ffn_act")
    fo = _mm([(act, w_2)], "nn", F32, "ffn_down")

    fn_gain = final_norm[None, :]

    def head(r, s):
        x1v, fv, tv = r
        xv = x1v + s[0] * fv
        rs = lax.rsqrt(_rowmean(xv * xv) + EPS)
        xn = xv * rs
        err = xn * s[1] - tv
        dyv = err * (1.0 / d)
        dxn = dyv * s[1]
        dxv = rs * (dxn - xn * _rowmean(dxn * xn))
        return [dxv, dxv * s[0]], [_colsum(err * err), _colsum(dyv * xn), _colsum(dxv * fv)]
    (dx2, dfo), (sq, g_final, dg_f) = _rowwise(head, [x1, fo, tgt], [g_f, fn_gain], [(d, F32), (d, BF16)], [d, d, d], "loss_head")
    loss = lax.psum(0.5 * jnp.sum(sq) / d, ("x", "y", "c"))

    dact = _mm([(dfo, w_2)], "nt", BF16, "ffn_down_dx")
    gw_2 = _mm([(act, dfo)], "tn", F32, "ffn_down_dw")
    (dfg, dfu), _ = _rowwise(lambda r, s: ([r[0] * r[2] * _dsilu(r[1]), r[0] * _silu(r[1])], []), [dact, fg, fu], [],
                             [(ff, BF16), (ff, BF16)], [], "ffn_act_bwd")
    gw_1 = _mm([(h2, dfg)], "tn", F32, "ffn_gate_dw")
    gw_3 = _mm([(h2, dfu)], "tn", F32, "ffn_up_dw")
    slots = lambda g: jnp.moveaxis(g.reshape(g.shape[0], N_DEV, g.shape[1] // N_DEV), 1, 0)
    rslots = lambda g: g.reshape(N_DEV, g.shape[0] // N_DEV, g.shape[1])
    ffn_rs = _ReduceScatter([slots(jnp.concatenate([gw_1, gw_3], axis=1)), rslots(gw_2)], "ffn_rs")
    dh2, got = _mm([(dfg, w_1), (dfu, w_3)], "nt", F32, "ffn_dx", rider=_swap_rider(*ffn_rs.stage()))
    ffn_rs.absorb(got)
    dx1, (dsh_f, dsc_f, g_norm_ffn) = _norm_mod_bwd(x1, dh2, dx2, norm_ffn, sc_f, "lat_norm_ffn_bwd")

    (dmixo,), (dg_m,) = _rowwise(lambda r, s: ([s[0] * r[0]], [_colsum(r[0] * r[1])]), [dx1, mixo], [g_m], [(d, BF16)], [d], "lat_resid1_bwd")
    dmix = _mm([(dmixo, w_oo)], "nt", BF16, "lat_out_o_dx")
    gw_o = _mm([(mix, dmixo)], "tn", F32, "lat_out_o_dw")

    def merge_bwd(r, s):
        dm, gps, gpr, os_, or_ = r
        g1, g2 = _sigmoid(gps + s[0]), _sigmoid(gpr + s[1])
        dg1, dg2 = dm * os_ * g1 * (1.0 - g1), dm * or_ * g2 * (1.0 - g2)
        return [dm * g1, dm * g2, dg1, dg2], [_colsum(dg1), _colsum(dg2)]
    (do_s, do_r, dgp_s, dgp_r), (gb_gs, gb_gr) = _rowwise(merge_bwd, [dmix, gp_s, gp_r, o_s, o_r], [bg_s, bg_r],
                                                            [(d, BF16)] * 4, [d, d], "lat_merge_bwd")
    gw_gs = _mm([(h, dgp_s)], "tn", F32, "lat_gs_dw")
    gw_gr = _mm([(h, dgp_r)], "tn", F32, "lat_gr_dw")

    dgr = _mm([(do_r, w_ol)], "nt", BF16, "lat_out_lru_dx")
    gw_ol = _mm([(gr, do_r)], "tn", F32, "lat_out_lru_dw")
    (dr_out, dyr), _ = _rowwise(lambda r, s: ([r[0] * _gelu(r[2]), r[0] * r[1] * _dgelu(r[2])], []), [dgr, r_out, yr], [],
                                [(wl, F32), (wl, BF16)], [], "lat_lru_out_bwd")
    dr_seq = _to_col_major(dr_out)
    lb0, got = _lru_bwd(l_xc, *lru_args(0), c_lru[0][1], hf, dr_seq, zeros_l, None, False, "lat_lru_bwd0", rider=_swap_rider(*ffn_rs.stage()))
    ffn_rs.absorb(got)
    lb1, got = _lru_bwd(l_xc, *lru_args(1), c_lru[1][1], hb, dr_seq, zeros_l, lb0[0], True, "lat_lru_bwd1", rider=_swap_rider(*ffn_rs.stage()))
    ffn_rs.absorb(got)
    dxr_seq, g_lconv_w, g_lconv_b = _conv_bwd(l_xr_seq, lb1[0], conv_w_l, lru_conv_b, False, "lat_lru_conv_bwd")
    dxr = _to_row_major(dxr_seq)

    dys = _mm([(do_s, w_os)], "nt", BF16, "lat_out_ssd_dx")
    gw_os = _mm([(ys, do_s)], "tn", F32, "lat_out_ssd_dw")

    def ssd_out_bwd(r, s):
        yf, yb, xs_, z_, dyn = r
        yv = yf + yb + s[0] * xs_
        sz = _silu(z_)
        yz = yv * sz
        rs = lax.rsqrt(_rowmean(yz * yz) + EPS)
        yzn = yz * rs
        dyzn = dyn * s[1]
        dyz = rs * (dyzn - yzn * _rowmean(dyzn * yzn))
        dyv = dyz * sz
        return [dyv, dyz * yv * _dsilu(z_), dyv * s[0]], [_colsum(dyv * xs_), _colsum(dyn * yzn)]
    (dy_ssd, dz, dxs_skip), (g_dfull, g_ssd_norm) = _rowwise(ssd_out_bwd, [l_ssd[0][0], l_ssd[1][0], l_xs, z, dys], [d_full, ssd_norm],
                                                              [(ws, F32), (ws, BF16), (ws, F32)], [ws, ws], "lat_ssd_out_bwd")
    sb = [_ssd_bwd(l_xs, l_bm, l_cm, l_dtr[:, dirn * heads:(dirn + 1) * heads], bias_s[dirn:dirn + 1], alog_s[dirn:dirn + 1],
                   dy_ssd, l_ssd[dirn][1], zeros_state, bool(dirn), f"lat_ssd_bwd{dirn}") for dirn in (0, 1)]
    (dcx,), _ = _rowwise(lambda r, s: ([jnp.concatenate([r[0] + r[1] + r[2], r[3] + r[4], r[5] + r[6]], axis=1)], []),
                         [dxs_skip, sb[0][0], sb[1][0], sb[0][1], sb[1][1], sb[0][2], sb[1][2]], [], [(ws + 2 * gn, F32)], [], "lat_ssd_dsum")
    dxbc, g_sconv_w, g_sconv_b = _conv_bwd(l_xbc, dcx, conv_w_s, ssd_conv_b, True, "lat_ssd_conv_bwd")
    ddt = jnp.pad(jnp.concatenate([sb[0][3], sb[1][3]], axis=1), ((0, 0), (0, LANES - 2 * heads)))

    zeros_cy = jnp.zeros((tc, ws), F32)
    csb = [_ssd_bwd(c_xs, c_bm, c_cm, c_dtr[:, dirn * heads:(dirn + 1) * heads], bias_s[dirn:dirn + 1], alog_s[dirn:dirn + 1],
                    zeros_cy, c_ssd[dirn][0], sb[dirn][6], bool(dirn), f"ctx_ssd_bwd{dirn}") for dirn in (0, 1)]
    (c_dcx,), _ = _rowwise(lambda r, s: ([jnp.concatenate([r[0] + r[1], r[2] + r[3], r[4] + r[5]], axis=1)], []),
                           [csb[0][0], csb[1][0], csb[0][1], csb[1][1], csb[0][2], csb[1][2]], [], [(ws + 2 * gn, F32)], [], "ctx_ssd_dsum")
    c_dxbc, cg_sconv_w, cg_sconv_b = _conv_bwd(c_xbc, c_dcx, conv_w_s, ssd_conv_b, True, "ctx_ssd_conv_bwd")
    c_ddt = jnp.pad(jnp.concatenate([csb[0][3], csb[1][3]], axis=1), ((0, 0), (0, LANES - 2 * heads)))
    zeros_ch = jnp.zeros((tc, wl), F32)
    clb0 = _lru_bwd(c_xc, *lru_args(0), zeros_l, c_lru[0][0], zeros_ch, lb0[6], None, False, "ctx_lru_bwd0")
    clb1 = _lru_bwd(c_xc, *lru_args(1), zeros_l, c_lru[1][0], zeros_ch, lb1[6], clb0[0], True, "ctx_lru_bwd1")
    c_dxr, cg_lconv_w, cg_lconv_b = _conv_bwd(c_xr_seq, clb1[0], conv_w_l, lru_conv_b, False, "ctx_lru_conv_bwd")
    dhc = _mm([(c_dxbc, w_xbc), (c_ddt, w_dt), (c_dxr, w_xr)], "nt", F32, "ctx_proj_dx")
    _, (dcsh_m, dcsc_m, cg_norm_mix) = _norm_mod_bwd(ctx2, dhc, None, norm_mix, csc_m, "ctx_norm_bwd")

    gw_z = _mm([(h, dz)], "tn", F32, "lat_z_dw")
    gw_yr = _mm([(h, dyr)], "tn", F32, "lat_yr_dw")
    gw_xbc = _mm([(h, dxbc), (hc, c_dxbc)], "tn", F32, "proj_xbc_dw")
    gw_dt = _mm([(h, ddt), (hc, c_ddt)], "tn", F32, "proj_dt_dw")[:, :2 * heads]
    gw_xr = _mm([(h, dxr), (hc, c_dxr)], "tn", F32, "proj_xr_dw")
    dh = _mm([(dz, w_z), (dxbc, w_xbc), (ddt, w_dt)], "nt", F32, "lat_proj_dx0")
    dh = _mm([(dxr, w_xr), (dyr, w_yr)], "nt", F32, "lat_proj_dx1", init=dh)
    dh = _mm([(dgp_s, w_gs), (dgp_r, w_gr)], "nt", F32, "lat_proj_dx2", init=dh)
    grad_x, (dsh_m, dsc_m, g_norm_mix) = _norm_mod_bwd(x2, dh, dx1, norm_mix, sc_m, "lat_norm_bwd")

    zero_d = jnp.zeros((1, d), F32)
    dmod_rows = jnp.concatenate([jnp.concatenate([dsh_m, dsc_m, dg_m, dsh_f, dsc_f, dg_f], axis=1),
                                 jnp.concatenate([dcsh_m, dcsc_m, zero_d, zero_d, zero_d, zero_d], axis=1)], axis=0)
    (dmod_g,) = _all_gather([dmod_rows], "gather_dmod")
    dmod_c = functools.reduce(lambda u, v: u + v, [dmod_g[i, 1] for i in range(N_DEV)])
    dmod9 = jnp.concatenate([dmod_g[:, 0], dmod_c[None, :], jnp.zeros((16 - N_DEV - 1, N_MOD * d), F32)], axis=0)
    g_b_ada = jnp.sum(dmod9, axis=0, keepdims=True)
    dmod_l = lax.dynamic_slice_in_dim(dmod9, me * ada_cols, ada_cols, 1)
    g_w_ada = _mm([(cond_act, dmod_l)], "tn", F32, "ada_dw")
    dcond_l = _mm([(dmod_l, w_ada[0])], "nt", F32, "ada_dx")

    small_parts = [
        dcond_l[N_DEV:N_DEV + 1], g_norm_mix + cg_norm_mix, g_norm_ffn, g_sconv_w + cg_sconv_w, g_sconv_b + cg_sconv_b,
        jnp.concatenate([sb[0][5] + csb[0][5], sb[1][5] + csb[1][5]], axis=0),
        jnp.concatenate([sb[0][4] + csb[0][4], sb[1][4] + csb[1][4]], axis=0),
        g_dfull, g_ssd_norm, g_lconv_w + cg_lconv_w, g_lconv_b + cg_lconv_b,
        jnp.stack([lb0[1] + clb0[1], lb1[1] + clb1[1]]), jnp.concatenate([lb0[3] + clb0[3], lb1[3] + clb1[3]], axis=0),
        jnp.stack([lb0[2] + clb0[2], lb1[2] + clb1[2]]), jnp.concatenate([lb0[4] + clb0[4], lb1[4] + clb1[4]], axis=0),
        jnp.concatenate([lb0[5] + clb0[5], lb1[5] + clb1[5]], axis=0),
        jnp.concatenate([gb_gs, gb_gr], axis=1), g_final,
    ]
    (s_cctx, s_nmix, s_nffn, s_sconv_w, s_sconv_b, s_dtb, s_da, s_dfull, s_snorm, s_lconv_w, s_lconv_b,
     s_wa, s_ba, s_wx, s_bx, s_dsp, s_bgate, s_final) = _all_reduce(small_parts, "small_grads")
    shard = lambda a, n_: lax.dynamic_slice_in_dim(a, me * n_, n_, a.ndim - 1)
    grads = {
        "c_ctx": (s_cctx * _dsilu(c_ctx[None, :]))[0],
        "w_ada": g_w_ada[None], "b_ada": g_b_ada, "norm_mix": s_nmix, "norm_ffn": s_nffn,
        "ssd_conv_w": shard(s_sconv_w, ssd_conv_w.shape[2])[None], "ssd_conv_b": s_sconv_b,
        "ssd_dt_bias": s_dtb[None], "ssd_a_log": (s_da * (-jnp.exp(alog_s)))[None],
        "ssd_d": jnp.sum(s_dfull.reshape(heads, SSD_HEAD_DIM), axis=1)[None], "ssd_norm": s_snorm,
        "lru_conv_w": shard(s_lconv_w, lru_conv_w.shape[2])[None], "lru_conv_b": s_lconv_b,
        "lru_w_a": s_wa[None], "lru_b_a": shard(s_ba, lru_b_a.shape[2])[None],
        "lru_w_x": s_wx[None], "lru_b_x": shard(s_bx, lru_b_x.shape[2])[None],
        "lru_lambda": shard(s_dsp * (-_sigmoid(-lru_lam)), lru_lambda.shape[2])[None],
        "b_gate": s_bgate, "final_norm": s_final[0],
    }

    gw_in = jnp.concatenate([gw_z, gw_xbc, gw_dt, gw_xr, gw_yr], axis=1)
    parts = [slots(gw_in), rslots(gw_os), rslots(gw_ol), slots(jnp.concatenate([gw_gs, gw_gr], axis=1)), rslots(gw_o)]
    for n_, g in zip(["w_in", "w_out_ssd", "w_out_lru", "w_gate", "w_o", "ffn_w13", "ffn_w2"],
                     _reduce_scatter(parts, "grad_rs") + ffn_rs.result()):
        grads[n_] = g[None]

    big_names = ["w_ada", "w_in", "w_out_ssd", "w_out_lru", "w_gate", "w_o", "ffn_w13", "ffn_w2"]
    small_names = [n_ for n_ in names if n_ not in big_names]
    delta, new_m, new_v = {}, {}, {}
    for n_ in big_names:
        delta[n_], new_m[n_], new_v[n_] = _adamw(wts[n_], grads[n_], mom1[n_], mom2[n_], "adamw_" + n_)
    packed = [_pack([src[n_] for n_ in small_names]) for src in (wts, grads, mom1, mom2)]
    outs = _adamw(*packed, "adamw_small")
    like = [wts[n_] for n_ in small_names]
    for dst, buf in zip((delta, new_m, new_v), outs):
        dst.update(zip(small_names, _unpack(buf, like)))

    return (loss, grad_x[None], *[grads[n_].reshape(wts[n_].shape) for n_ in names], *[delta[n_] for n_ in names],
            *[new_m[n_] for n_ in names], *[new_v[n_] for n_ in names])
```

```python
import functools

import numpy as np
import jax
import jax.numpy as jnp
from jax import lax
from jax.experimental import pallas as pl
from jax.experimental.pallas import tpu as pltpu

F32, BF16 = jnp.float32, jnp.bfloat16
MESH = pl.DeviceIdType.MESH
N_DEV = 8
VMEM_LIMIT_BYTES = 48 * 1024 * 1024
ROW_BUDGET_BYTES = 12 * 1024 * 1024
LANES, SUBLANES = 128, 8

EPS = 1e-6
GRID_W = 64
SSD_HEAD_DIM = 64
SSD_GROUPS = 4
SSD_STATE = 128
SSD_CHUNK = 128
CONV_W = 4
CONV_LEFT = 2
LRU_HEAD_DIM = 128
LRU_C = 8.0
LRU_BLOCK = 128
N_MOD = 6
ADAM_LR, ADAM_B1, ADAM_B2, ADAM_EPS, ADAM_WD, ADAM_STEP = 0.001, 0.9, 0.999, 1e-08, 0.01, 10
NEG_BIG = -1e30


def _params(sem):
    return pltpu.CompilerParams(dimension_semantics=sem, vmem_limit_bytes=VMEM_LIMIT_BYTES)


def _tile(dim, pref, align):
    if dim <= pref:
        return dim
    t = (pref // align) * align
    while t >= align:
        if dim % t == 0:
            return t
        t -= align
    return dim


class _Rider:
    def __init__(self, ins, out_shape, aliases, sems, start, wait):
        self.ins, self.out_shape, self.aliases, self.sems, self.start, self.wait = ins, out_shape, aliases, sems, start, wait


def _call(body, name, grid, in_specs, out_specs, out_shape, scratch_shapes, sem, operands, rider=None):
    if rider is None:
        return pl.pallas_call(body, name=name, grid=grid, in_specs=in_specs, out_specs=out_specs, out_shape=out_shape,
                              scratch_shapes=scratch_shapes, compiler_params=_params(sem))(*operands)
    n_in, n_out, n_scr = len(in_specs), len(out_specs), len(scratch_shapes)
    ri, ro = len(rider.ins), len(rider.out_shape)
    a, b = n_in, n_in + ri
    c, d = b + n_out, b + n_out + ro
    e = d + n_scr

    def carried(*refs):
        ids = [pl.program_id(i) for i in range(len(grid))]
        first = functools.reduce(lambda u, v: u & v, [i == 0 for i in ids])
        last = functools.reduce(lambda u, v: u & v, [i == g - 1 for i, g in zip(ids, grid)])

        @pl.when(first)
        def _():
            rider.start(refs[a:b], refs[c:d], refs[e:])

        body(*refs[:a], *refs[b:c], *refs[d:e])

        @pl.when(last)
        def _():
            rider.wait(refs[a:b], refs[c:d], refs[e:])

    any_spec = pl.BlockSpec(memory_space=pl.ANY)
    res = pl.pallas_call(
        carried, name=name, grid=grid, in_specs=list(in_specs) + [any_spec] * ri, out_specs=list(out_specs) + [any_spec] * ro,
        out_shape=list(out_shape) + list(rider.out_shape), scratch_shapes=list(scratch_shapes) + list(rider.sems),
        input_output_aliases={n_in + i: n_out + o for i, o in rider.aliases.items()},
        compiler_params=_params(("arbitrary",) * len(grid)),
    )(*operands, *rider.ins)
    return res[:n_out], res[n_out:]


def _bdot(a, b):
    return jnp.dot(a.astype(BF16), b.astype(BF16), preferred_element_type=F32)


def _bdot_nt(a, b):
    return lax.dot_general(a.astype(BF16), b.astype(BF16), (((1,), (1,)), ((), ())), preferred_element_type=F32)


def _bdot_tn(a, b):
    return lax.dot_general(a.astype(BF16), b.astype(BF16), (((0,), (0,)), ((), ())), preferred_element_type=F32)


def _hdot(a, b):
    return jnp.dot(a, b, preferred_element_type=F32, precision=lax.Precision.HIGHEST)


def _softplus(v):
    return jnp.maximum(v, 0.0) + jnp.log(1.0 + jnp.exp(-jnp.abs(v)))


def _sigmoid(v):
    return 0.5 * jnp.tanh(0.5 * v) + 0.5


def _silu(v):
    return v * _sigmoid(v)


def _dsilu(v):
    s = _sigmoid(v)
    return s * (1.0 + v * (1.0 - s))


GELU_C0 = 0.7978845608028654
GELU_C1 = 0.044715


def _gelu(v):
    return 0.5 * v * (1.0 + jnp.tanh(GELU_C0 * (v + GELU_C1 * v * v * v)))


def _dgelu(v):
    t = jnp.tanh(GELU_C0 * (v + GELU_C1 * v * v * v))
    return 0.5 * (1.0 + t) + 0.5 * v * (1.0 - t * t) * GELU_C0 * (1.0 + 3.0 * GELU_C1 * v * v)


MM_TM, MM_TN = 1024, 1408
MM_TK = {1: 2048, 2: 1536}


def _mm(pairs, mode, out_dtype, name, init=None, rider=None, slots=None):
    a0, b0 = pairs[0]
    if mode == "nn":
        m, n, ks = a0.shape[0], b0.shape[1], [a.shape[1] for a, _ in pairs]
    elif mode == "nt":
        m, n, ks = a0.shape[0], b0.shape[0], [a.shape[1] for a, _ in pairs]
    else:
        m, n, ks = a0.shape[1], b0.shape[1], [a.shape[0] for a, _ in pairs]
    tm = _tile(m, MM_TM, LANES if mode == "tn" else 16)
    tn = _tile(n, MM_TN, LANES) if slots is None else slots[0]
    tk = MM_TK.get(sum(k > 512 for k in ks), 512)
    tks = [_tile(k, tk, 16 if mode == "tn" else LANES) for k in ks]
    steps = [k // t for k, t in zip(ks, tks)]
    offs = [sum(steps[:i]) for i in range(len(pairs))]
    total = sum(steps)
    npairs = len(pairs)

    def kk(i):
        return lambda k: jnp.clip(k - offs[i], 0, steps[i] - 1)

    in_specs, operands = [], []
    for i, (a, b) in enumerate(pairs):
        kf = kk(i)
        if mode == "nn":
            in_specs.append(pl.BlockSpec((tm, tks[i]), lambda mi, ni, k, kf=kf: (mi, kf(k))))
            in_specs.append(pl.BlockSpec((tks[i], tn), lambda mi, ni, k, kf=kf: (kf(k), ni)))
        elif mode == "nt":
            in_specs.append(pl.BlockSpec((tm, tks[i]), lambda mi, ni, k, kf=kf: (mi, kf(k))))
            in_specs.append(pl.BlockSpec((tn, tks[i]), lambda mi, ni, k, kf=kf: (ni, kf(k))))
        else:
            in_specs.append(pl.BlockSpec((tks[i], tm), lambda mi, ni, k, kf=kf: (kf(k), mi)))
            in_specs.append(pl.BlockSpec((tks[i], tn), lambda mi, ni, k, kf=kf: (kf(k), ni)))
        operands += [a, b]
    dot = {"nn": _bdot, "nt": _bdot_nt, "tn": _bdot_tn}[mode]
    if init is not None:
        in_specs.append(pl.BlockSpec((tm, tn), lambda mi, ni, k: (mi, ni)))
        operands.append(init)
    assert out_dtype == F32 or total == 1

    def body(*refs):
        out_ref = refs[-1]
        k = pl.program_id(2)

        def first_step():
            p = dot(refs[0][...], refs[1][...])
            return p if init is None else p + refs[2 * npairs][...]

        if total == 1:
            out_ref[...] = first_step().astype(out_dtype)
            return

        @pl.when(k == 0)
        def _():
            out_ref[...] = first_step()

        for i in range(npairs):
            first = 1 if i == 0 else offs[i]
            if offs[i] + steps[i] > first:
                @pl.when((k >= first) & (k < offs[i] + steps[i]))
                def _(i=i):
                    out_ref[...] += dot(refs[2 * i][...], refs[2 * i + 1][...])

    if slots is not None:
        width, n_slots, first_slot, into = slots
        assert rider is None and init is None and n % width == 0
        aliases = {}
        if into is not None:
            in_specs.append(pl.BlockSpec(memory_space=pl.ANY))
            operands.append(into)
            aliases = {len(operands) - 1: 0}
        return pl.pallas_call(
            body, name=name, grid=(m // tm, n // tn, total), in_specs=in_specs,
            out_specs=pl.BlockSpec((None, tm, tn), lambda mi, ni, k: (ni + first_slot, mi, 0)),
            out_shape=jax.ShapeDtypeStruct((n_slots, m, width), out_dtype), input_output_aliases=aliases,
            compiler_params=_params(("parallel", "parallel", "arbitrary")))(*operands)
    res = _call(body, name, (m // tm, n // tn, total), in_specs, [pl.BlockSpec((tm, tn), lambda mi, ni, k: (mi, ni))],
                [jax.ShapeDtypeStruct((m, n), out_dtype)], [], ("parallel", "parallel", "arbitrary"), operands, rider)
    return res[0] if rider is None else (res[0][0], res[1])


def _rowwise(fn, rows, small, outs, accs, name):
    t = rows[0].shape[0]
    per_row = sum(r.shape[1] * r.dtype.itemsize for r in rows) + sum(c * jnp.dtype(d).itemsize for c, d in outs)
    tm = SUBLANES
    while tm * 2 <= 1024 and t % (tm * 2) == 0 and tm * 2 * per_row <= ROW_BUDGET_BYTES:
        tm *= 2
    if t % tm:
        tm = t
    nrow, nsmall, nout, nacc = len(rows), len(small), len(outs), len(accs)

    def body(*refs):
        rv = [r[...].astype(F32) for r in refs[:nrow]]
        sv = [r[...] for r in refs[nrow:nrow + nsmall]]
        o_refs = refs[nrow + nsmall:nrow + nsmall + nout]
        a_refs = refs[nrow + nsmall + nout:]
        ov, av = fn(rv, sv)
        for r, v in zip(o_refs, ov):
            r[...] = v.astype(r.dtype)
        if nacc:
            @pl.when(pl.program_id(0) == 0)
            def _():
                for r in a_refs:
                    r[...] = jnp.zeros_like(r)
            for r, v in zip(a_refs, av):
                r[...] += v

    in_specs = [pl.BlockSpec((tm, r.shape[1]), lambda i: (i, 0)) for r in rows]
    in_specs += [pl.BlockSpec(s.shape, lambda i: (0, 0)) for s in small]
    out_specs = [pl.BlockSpec((tm, c), lambda i: (i, 0)) for c, _ in outs]
    out_specs += [pl.BlockSpec((1, c), lambda i: (0, 0)) for c in accs]
    out_shape = [jax.ShapeDtypeStruct((t, c), d) for c, d in outs]
    out_shape += [jax.ShapeDtypeStruct((1, c), F32) for c in accs]
    res = pl.pallas_call(
        body, name=name, grid=(t // tm,), in_specs=in_specs, out_specs=out_specs, out_shape=out_shape,
        compiler_params=_params(("arbitrary",)),
    )(*rows, *small)
    return res[:nout], res[nout:]


def _colsum(v):
    return jnp.sum(v, axis=0, keepdims=True)


def _rowmean(v):
    return jnp.mean(v, axis=1, keepdims=True)


def _conv_tiles(x):
    t, c = x.shape
    tm = _tile(t, 256, SUBLANES)
    nt = t // tm
    r8 = tm // SUBLANES
    specs = [
        pl.BlockSpec((SUBLANES, c), lambda i: (jnp.maximum(i * r8 - 1, 0), 0)),
        pl.BlockSpec((tm, c), lambda i: (i, 0)),
        pl.BlockSpec((SUBLANES, c), lambda i: (jnp.minimum((i + 1) * r8, t // SUBLANES - 1), 0)),
    ]
    return tm, nt, specs


def _conv_taps(prev_ref, cur_ref, next_ref, tm, nt, left):
    i = pl.program_id(0)
    p = jnp.where(i > 0, prev_ref[...], 0.0)
    q = jnp.where(i < nt - 1, next_ref[...], 0.0)
    ext = jnp.concatenate([p, cur_ref[...], q], axis=0)
    n = tm + 2 * SUBLANES
    taps = []
    for k in range(CONV_W):
        s = (left - k) % n
        r = pltpu.roll(ext, s, axis=0) if s else ext
        taps.append(r[SUBLANES:SUBLANES + tm])
    return taps


def _conv(x, w, b, left, act, name, out_dtype=F32):
    t, c = x.shape
    tm, nt, specs = _conv_tiles(x)

    def body(prev_ref, cur_ref, next_ref, w_ref, b_ref, o_ref):
        taps = _conv_taps(prev_ref, cur_ref, next_ref, tm, nt, left)
        pre = b_ref[...] + sum(w_ref[k:k + 1, :] * taps[k] for k in range(CONV_W))
        o_ref[...] = (_silu(pre) if act else pre).astype(out_dtype)

    return pl.pallas_call(
        body, name=name, grid=(nt,),
        in_specs=specs + [pl.BlockSpec((CONV_W, c), lambda i: (0, 0)), pl.BlockSpec((1, c), lambda i: (0, 0))],
        out_specs=pl.BlockSpec((tm, c), lambda i: (i, 0)),
        out_shape=jax.ShapeDtypeStruct((t, c), out_dtype), compiler_params=_params(("arbitrary",)),
    )(x, x, x, w, b)


def _conv_bwd(x, dys, combine, w, b, act, name):
    t, c = x.shape
    tm, nt, specs = _conv_tiles(x)
    nd = len(dys)

    def body(*refs):
        prev_ref, cur_ref, next_ref = refs[:3]
        w_ref, b_ref, dpre_ref, dw0, dw1, dw2, dw3, db_ref = refs[3 + nd:]
        dy = combine([r[...] for r in refs[3:3 + nd]])
        taps = _conv_taps(prev_ref, cur_ref, next_ref, tm, nt, CONV_LEFT)
        if act:
            pre = b_ref[...] + sum(w_ref[k:k + 1, :] * taps[k] for k in range(CONV_W))
            dpre = dy * _dsilu(pre)
        else:
            dpre = dy
        dpre_ref[...] = dpre

        @pl.when(pl.program_id(0) == 0)
        def _():
            for r in (dw0, dw1, dw2, dw3, db_ref):
                r[...] = jnp.zeros_like(r)

        for k, r in enumerate((dw0, dw1, dw2, dw3)):
            r[...] += _colsum(dpre * taps[k])
        db_ref[...] += _colsum(dpre)

    row = pl.BlockSpec((1, c), lambda i: (0, 0))
    res = pl.pallas_call(
        body, name=name + "_pre", grid=(nt,),
        in_specs=specs + [pl.BlockSpec((tm, a.shape[1]), lambda i: (i, 0)) for a in dys] + [pl.BlockSpec((CONV_W, c), lambda i: (0, 0)), row],
        out_specs=[pl.BlockSpec((tm, c), lambda i: (i, 0))] + [row] * 5,
        out_shape=[jax.ShapeDtypeStruct((t, c), F32)] + [jax.ShapeDtypeStruct((1, c), F32)] * 5,
        compiler_params=_params(("arbitrary",)),
    )(x, x, x, *dys, w, b)
    dpre, dw, db = res[0], jnp.concatenate(res[1:5], axis=0), res[5]
    dx = _conv(dpre, w[::-1], jnp.zeros_like(b), CONV_W - 1 - CONV_LEFT, False, name + "_dx", BF16)
    return dx, dw, db


def _ssd_dims(xbc):
    t = xbc.shape[0]
    gn = SSD_GROUPS * SSD_STATE
    ws = xbc.shape[1] - 2 * gn
    heads = ws // SSD_HEAD_DIM
    hpg = heads // SSD_GROUPS
    assert hpg % 2 == 0 and ws % gn == 0 and t % SSD_CHUNK == 0
    return t, ws, heads, hpg, heads // 2, t // SSD_CHUNK


def _ssd_chunk_terms(dtr_ref, dtrt_ref, bias_ref, biast_ref, alog_ref, alogt_ref, reverse):
    q = SSD_CHUNK
    a_neg = -jnp.exp(alog_ref[...])
    dt = _softplus(dtr_ref[...] + bias_ref[...])
    dtt = _softplus(dtrt_ref[...] + biast_ref[...])
    a = dt * a_neg
    at = dtt * (-jnp.exp(alogt_ref[...]))
    ii = lax.broadcasted_iota(jnp.int32, (q, q), 0)
    jj = lax.broadcasted_iota(jnp.int32, (q, q), 1)
    mask = (ii <= jj) if reverse else (ii >= jj)
    tri = mask.astype(F32)
    trit = ((jj <= ii) if reverse else (jj >= ii)).astype(F32)
    before = (jj > ii) if reverse else (jj < ii)
    ac = _hdot(tri, a)
    act = _hdot(at, trit)
    tot = _colsum(a)
    return a_neg, dt, a, ac, act, tot, mask, trit, before


def _pair_terms(lo, h1, dt, ac, tot):
    q = SSD_CHUNK

    def colb(v, h):
        return jnp.broadcast_to(v[:, h:h + 1], (v.shape[0], LANES))

    def pairb(v):
        return jnp.where(lo[:v.shape[0]], colb(v, h1), colb(v, h1 + 1))

    dt_p = pairb(dt)
    ac_p = pairb(ac)
    eac_p = jnp.exp(ac_p)
    dte_p = jnp.exp(pairb(jnp.broadcast_to(tot, (q, tot.shape[1]))) - ac_p)
    etot_p = jnp.exp(pairb(tot))
    return dt_p, eac_p, dte_p, etot_p


def _decay_mats(mask, ac, act, cb, h1):
    q = SSD_CHUNK
    out = []
    for h in (h1, h1 + 1):
        diff = jnp.broadcast_to(ac[:, h:h + 1], (q, q)) - jnp.broadcast_to(act[h:h + 1, :], (q, q))
        lmat = jnp.exp(jnp.where(mask, diff, NEG_BIG))
        out.append((lmat, cb * lmat))
    return out


def _ssd_specs(t, ws, heads, npairs, nc, reverse):
    q, gn = SSD_CHUNK, SSD_GROUPS * SSD_STATE
    ci = (lambda k: nc - 1 - k) if reverse else (lambda k: k)
    small = lambda shape: pl.BlockSpec(shape, lambda k: (0,) * len(shape))
    seq = [
        pl.BlockSpec((q, ws), lambda k: (ci(k), 0)),
        pl.BlockSpec((q, gn), lambda k: (ci(k), ws // gn)),
        pl.BlockSpec((q, gn), lambda k: (ci(k), ws // gn + 1)),
        pl.BlockSpec((q, heads), lambda k: (ci(k), 0)),
        pl.BlockSpec((heads, q), lambda k: (0, ci(k))),
        small((1, heads)), small((heads, 1)), small((1, heads)), small((heads, 1)),
    ]
    return ci, small, seq


def _ssd_fwd(xbc, dtr, bias, alog, s0, reverse, name, rider=None):
    t, ws, heads, hpg, npairs, nc = _ssd_dims(xbc)
    q, ns = SSD_CHUNK, SSD_STATE
    ci, small, seq = _ssd_specs(t, ws, heads, npairs, nc, reverse)

    def body(xs_ref, b_ref, c_ref, dtr_ref, dtrt_ref, bias_ref, biast_ref, alog_ref, alogt_ref, s0_ref,
             y_ref, se_ref, s_ref):
        @pl.when(pl.program_id(0) == 0)
        def _():
            s_ref[...] = s0_ref[...]

        _, dt, _, ac, act, tot, mask, _, _ = _ssd_chunk_terms(dtr_ref, dtrt_ref, bias_ref, biast_ref, alog_ref, alogt_ref, reverse)
        lo = lax.broadcasted_iota(jnp.int32, (q, LANES), 1) < SSD_HEAD_DIM
        for g in range(SSD_GROUPS):
            bg = b_ref[:, g * ns:(g + 1) * ns]
            cg = c_ref[:, g * ns:(g + 1) * ns]
            cb = _bdot_nt(cg, bg)
            bgt = bg.T
            for pq in range(hpg // 2):
                pp = g * (hpg // 2) + pq
                h1 = 2 * pp
                cs = slice(pp * LANES, (pp + 1) * LANES)
                (_, m1), (_, m2) = _decay_mats(mask, ac, act, cb, h1)
                dt_p, eac_p, dte_p, etot_p = _pair_terms(lo, h1, dt, ac, tot)
                xp = xs_ref[:, cs] * dt_p
                sp = s_ref[pp]
                se_ref[0, pp] = sp
                y_ref[:, cs] = jnp.where(lo, _bdot(m1, xp), _bdot(m2, xp)) + eac_p * _bdot(cg, sp)
                s_ref[pp] = etot_p * sp + _bdot(bgt, xp * dte_p)

    st = (npairs, ns, LANES)
    return _call(
        body, name, (nc,), seq + [small(st)],
        [pl.BlockSpec((q, ws), lambda k: (ci(k), 0)), pl.BlockSpec((1,) + st, lambda k: (ci(k), 0, 0, 0)), small(st)],
        [jax.ShapeDtypeStruct((t, ws), F32), jax.ShapeDtypeStruct((nc,) + st, F32), jax.ShapeDtypeStruct(st, F32)],
        [], ("arbitrary",), (xbc, xbc, xbc, dtr, dtr.T, bias, bias.T, alog, alog.T, s0), rider)


def _ssd_bwd(xbc, dtr, bias, alog, dy, s_enter, lam0, reverse, name):
    t, ws, heads, hpg, npairs, nc = _ssd_dims(xbc)
    q, ns, gn = SSD_CHUNK, SSD_STATE, SSD_GROUPS * SSD_STATE
    ci, small, seq = _ssd_specs(t, ws, heads, npairs, nc, not reverse)
    st = (npairs, ns, LANES)

    e_heads = jnp.asarray(np.arange(ws)[:, None] // SSD_HEAD_DIM == np.arange(heads)[None, :], BF16)

    def body(xs_ref, b_ref, c_ref, dtr_ref, dtrt_ref, bias_ref, biast_ref, alog_ref, alogt_ref, dy_ref, se_ref, lam0_ref, e_ref,
             dxs_ref, db_ref, dc_ref, ddtr_ref, da_ref, dbias_ref, lam_ref, st_s, ts_s, ss_s, xx_s):
        @pl.when(pl.program_id(0) == 0)
        def _():
            lam_ref[...] = lam0_ref[...]
            da_ref[...] = jnp.zeros_like(da_ref)
            dbias_ref[...] = jnp.zeros_like(dbias_ref)

        a_neg, dt, _, ac, act, tot, mask, trit, before = _ssd_chunk_terms(dtr_ref, dtrt_ref, bias_ref, biast_ref, alog_ref, alogt_ref, reverse)
        lo = lax.broadcasted_iota(jnp.int32, (q, LANES), 1) < SSD_HEAD_DIM
        head_col = lax.broadcasted_iota(jnp.int32, (heads, 1), 0)
        ii = lax.broadcasted_iota(jnp.int32, (q, q), 0)
        jj = lax.broadcasted_iota(jnp.int32, (q, q), 1)
        before_t = ((ii > jj) if reverse else (ii < jj)).astype(F32)

        def per_head(v):
            hi = v.astype(BF16)
            lo = (v - hi.astype(F32)).astype(BF16)
            sel = e_ref[...]
            return jnp.dot(hi, sel, preferred_element_type=F32) + jnp.dot(lo, sel, preferred_element_type=F32)

        da_tt = jnp.zeros((heads, q), F32)
        for g in range(SSD_GROUPS):
            gs = slice(g * ns, (g + 1) * ns)
            bg, cg = b_ref[:, gs], c_ref[:, gs]
            cb = _bdot_nt(cg, bg)
            cgt = cg.T
            dcb = jnp.zeros((q, q), F32)
            dbg = jnp.zeros((q, ns), F32)
            dcg = jnp.zeros((q, ns), F32)
            for pq in range(hpg // 2):
                pp = g * (hpg // 2) + pq
                h1 = 2 * pp
                cs = slice(pp * LANES, (pp + 1) * LANES)
                (l1, m1), (l2, m2) = _decay_mats(mask, ac, act, cb, h1)
                dt_p, eac_p, dte_p, etot_p = _pair_terms(lo, h1, dt, ac, tot)
                xsp = xs_ref[:, cs]
                xp = xsp * dt_p
                dyp = dy_ref[:, cs]
                lam = lam_ref[pp]
                sp = se_ref[0, pp]
                dx_state = dte_p * _bdot(bg, lam)
                dxdt = jnp.where(lo, _bdot(m1.T, dyp), _bdot(m2.T, dyp)) + dx_state
                dyx1 = _bdot_nt(jnp.where(lo, dyp, 0.0), xp)
                dyx2 = _bdot_nt(jnp.where(lo, 0.0, dyp), xp)
                dcb = dcb + l1 * dyx1 + l2 * dyx2
                for h, pair in ((h1, m1 * dyx1), (h1 + 1, m2 * dyx2)):
                    cross = _colsum(jnp.where(mask, _bdot(pair, before_t), 0.0))
                    da_tt = da_tt + (head_col == h).astype(F32) * cross
                st_s[:, cs] = dyp * (eac_p * _bdot(cg, sp))
                ts_s[:, cs] = dx_state * xp
                ss_s[:, cs] = jnp.broadcast_to(_colsum(lam * sp) * etot_p, (SUBLANES, LANES))
                xx_s[:, cs] = dxdt * xsp
                edy = eac_p * dyp
                xe = dte_p * xp
                dcg = dcg + _bdot_nt(edy, sp)
                dbg = dbg + _bdot_nt(xe, lam)
                lam_ref[pp] = etot_p * lam + _bdot(cgt, edy)
                dxs_ref[:, cs] = dxdt * dt_p
            db_ref[:, gs] = dbg + _bdot(dcb.T, cg)
            dc_ref[:, gs] = dcg + _bdot(dcb, bg)
        da = da_tt.T + _hdot(trit, per_head(st_s[...])) + _hdot(before.astype(F32), per_head(ts_s[...])) + per_head(ss_s[...])[0:1]
        ddtr = (da * a_neg + per_head(xx_s[...])) * _sigmoid(dtr_ref[...] + bias_ref[...])
        ddtr_ref[...] = ddtr
        da_ref[...] += _colsum(da * dt)
        dbias_ref[...] += _colsum(ddtr)

    row = lambda c: pl.BlockSpec((q, c), lambda k: (ci(k), 0))
    return pl.pallas_call(
        body, name=name, grid=(nc,),
        in_specs=seq + [row(ws), pl.BlockSpec((1,) + st, lambda k: (ci(k), 0, 0, 0)), small(st), small((ws, heads))],
        out_specs=[row(ws), row(gn), row(gn), row(heads), small((1, heads)), small((1, heads)), small(st)],
        out_shape=[jax.ShapeDtypeStruct((t, ws), F32), jax.ShapeDtypeStruct((t, gn), F32), jax.ShapeDtypeStruct((t, gn), F32),
                   jax.ShapeDtypeStruct((t, heads), F32), jax.ShapeDtypeStruct((1, heads), F32),
                   jax.ShapeDtypeStruct((1, heads), F32), jax.ShapeDtypeStruct(st, F32)],
        scratch_shapes=[pltpu.VMEM((q, ws), F32), pltpu.VMEM((q, ws), F32), pltpu.VMEM((SUBLANES, ws), F32), pltpu.VMEM((q, ws), F32)],
        compiler_params=_params(("arbitrary",)),
    )(xbc, xbc, xbc, dtr, dtr.T, bias, bias.T, alog, alog.T, dy, s_enter, lam0, e_heads)


def _tile_scan(a_ref, u_ref, out_ref, carry_ref, ntiles, reverse):
    w = a_ref.shape[1]
    row = lax.broadcasted_iota(jnp.int32, (SUBLANES, w), 0)

    def step(j, _):
        r0 = pl.multiple_of((ntiles - 1 - j if reverse else j) * SUBLANES, SUBLANES)
        a = a_ref[pl.ds(r0, SUBLANES), :]
        u = u_ref[pl.ds(r0, SUBLANES), :]
        for d in (1, 2, 4):
            keep = (row < SUBLANES - d) if reverse else (row >= d)
            s = SUBLANES - d if reverse else d
            a_sh = jnp.where(keep, pltpu.roll(a, s, axis=0), 1.0)
            u_sh = jnp.where(keep, pltpu.roll(u, s, axis=0), 0.0)
            u = a * u_sh + u
            a = a * a_sh
        h = a * carry_ref[...] + u
        out_ref[pl.ds(r0, SUBLANES), :] = h
        last = h[0:1, :] if reverse else h[SUBLANES - 1:SUBLANES, :]
        carry_ref[...] = jnp.broadcast_to(last, (SUBLANES, w))
        return 0

    lax.fori_loop(0, ntiles, step, 0)


def _lru_gates(xh, wa, wx, ba, bx, lam):
    r = _sigmoid(_bdot(xh, wa) + ba)
    i = _sigmoid(_bdot(xh, wx) + bx)
    sp = _softplus(-lam)
    a = jnp.exp(-LRU_C * r * sp)
    return r, i, sp, a


def _lru_fwd(xc, wa, wx, ba, bx, lam, h0, addend, reverse, name, rider=None):
    t, w = xc.shape
    tb = LRU_BLOCK
    nb, nh = t // tb, w // LRU_HEAD_DIM
    bi = (lambda k: nb - 1 - k) if reverse else (lambda k: k)
    has_add = addend is not None

    def body(*refs):
        xc_ref, wa_ref, wx_ref, ba_ref, bx_ref, lam_ref, h0_ref = refs[:7]
        add_ref = refs[7] if has_add else None
        h_ref, hl_ref = refs[7 + has_add], refs[8 + has_add]
        sum_ref = refs[9 + has_add] if has_add else None
        a_s, u_s, carry = refs[-3:]

        @pl.when(pl.program_id(0) == 0)
        def _():
            carry[...] = jnp.broadcast_to(h0_ref[...], carry.shape)

        for hd in range(nh):
            hs = slice(hd * LRU_HEAD_DIM, (hd + 1) * LRU_HEAD_DIM)
            xh = xc_ref[:, hs]
            _, i, _, a = _lru_gates(xh, wa_ref[hd], wx_ref[hd], ba_ref[:, hs], bx_ref[:, hs], lam_ref[:, hs])
            a_s[:, hs] = a
            u_s[:, hs] = jnp.sqrt(1.0 - a * a) * (i * xh)
        _tile_scan(a_s, u_s, h_ref, carry, tb // SUBLANES, reverse)
        hl_ref[...] = carry[0:1, :]
        if has_add:
            sum_ref[...] = h_ref[...] + add_ref[...]

    blk = pl.BlockSpec((tb, w), lambda k: (bi(k), 0))
    vec = pl.BlockSpec((1, w), lambda k: (0, 0))
    wsp = pl.BlockSpec((nh, LRU_HEAD_DIM, LRU_HEAD_DIM), lambda k: (0, 0, 0))
    out_specs = [blk, vec] + ([blk] if has_add else [])
    out_shape = [jax.ShapeDtypeStruct((t, w), F32), jax.ShapeDtypeStruct((1, w), F32)] + ([jax.ShapeDtypeStruct((t, w), F32)] if has_add else [])
    return _call(
        body, name, (nb,), [blk, wsp, wsp, vec, vec, vec, vec] + ([blk] if has_add else []), out_specs, out_shape,
        [pltpu.VMEM((tb, w), F32), pltpu.VMEM((tb, w), F32), pltpu.VMEM((SUBLANES, w), F32)], ("arbitrary",),
        (xc, wa, wx, ba, bx, lam, h0, *([addend] if has_add else [])), rider)


def _lru_bwd(xc, wa, wx, ba, bx, lam, h0, h, dh, dfin, addend, reverse, name, rider=None):
    t, w = xc.shape
    tb = LRU_BLOCK
    nb, nh = t // tb, w // LRU_HEAD_DIM
    r8 = tb // SUBLANES
    bi = (lambda k: k) if reverse else (lambda k: nb - 1 - k)
    if reverse:
        halo = lambda k: (jnp.minimum((bi(k) + 1) * r8, t // SUBLANES - 1), 0)
    else:
        halo = lambda k: (jnp.maximum(bi(k) * r8 - 1, 0), 0)
    has_add = addend is not None
    n_ext = tb + SUBLANES

    def body(*refs):
        xc_ref, wa_ref, wx_ref, ba_ref, bx_ref, lam_ref, h0_ref, h_ref, halo_ref, dh_ref, dfin_ref = refs[:11]
        add_ref = refs[11] if has_add else None
        dxc_ref, dwa_ref, dwx_ref, dba_ref, dbx_ref, dsp_ref, dh0_ref = refs[11 + has_add:18 + has_add]
        a_s, v_s, mu_s, r_s, i_s, carry = refs[-6:]
        k = pl.program_id(0)

        @pl.when(k == 0)
        def _():
            carry[...] = jnp.broadcast_to(dfin_ref[...], carry.shape)
            for r in (dwa_ref, dwx_ref, dba_ref, dbx_ref, dsp_ref):
                r[...] = jnp.zeros_like(r)

        for hd in range(nh):
            hs = slice(hd * LRU_HEAD_DIM, (hd + 1) * LRU_HEAD_DIM)
            r, i, _, a = _lru_gates(xc_ref[:, hs], wa_ref[hd], wx_ref[hd], ba_ref[:, hs], bx_ref[:, hs], lam_ref[:, hs])
            a_s[:, hs] = a
            r_s[:, hs] = r
            i_s[:, hs] = i
            v_s[:, hs] = a * dh_ref[:, hs]
        mu_in = carry[...]
        _tile_scan(a_s, v_s, mu_s, carry, r8, not reverse)
        dh0_ref[...] = carry[0:1, :]
        first = (bi(k) == nb - 1) if reverse else (bi(k) == 0)
        edge = jnp.where(first, jnp.broadcast_to(h0_ref[...], (SUBLANES, w)), halo_ref[...])
        for hd in range(nh):
            hs = slice(hd * LRU_HEAD_DIM, (hd + 1) * LRU_HEAD_DIM)
            if reverse:
                mu_sh = pltpu.roll(jnp.concatenate([mu_in[:, hs], mu_s[:, hs]], axis=0), 1, axis=0)[SUBLANES:]
                h_sh = pltpu.roll(jnp.concatenate([h_ref[:, hs], edge[:, hs]], axis=0), n_ext - 1, axis=0)[:tb]
            else:
                mu_sh = pltpu.roll(jnp.concatenate([mu_s[:, hs], mu_in[:, hs]], axis=0), n_ext - 1, axis=0)[:tb]
                h_sh = pltpu.roll(jnp.concatenate([edge[:, hs], h_ref[:, hs]], axis=0), 1, axis=0)[SUBLANES:]
            xh = xc_ref[:, hs]
            r, i, a = r_s[:, hs], i_s[:, hs], a_s[:, hs]
            sp = _softplus(-lam_ref[:, hs])
            om = 1.0 - a * a
            rs = lax.rsqrt(om)
            lam_t = dh_ref[:, hs] + mu_sh
            dla = lam_t * h_sh * a - lam_t * (i * xh) * (a * a) * rs
            dix = lam_t * (om * rs)
            dpr = dla * (-LRU_C * sp) * r * (1.0 - r)
            dpi = dix * xh * i * (1.0 - i)
            dsp_ref[:, hs] += _colsum(dla * (-LRU_C * r))
            dba_ref[:, hs] += _colsum(dpr)
            dbx_ref[:, hs] += _colsum(dpi)
            dx = dix * i + _bdot_nt(dpr, wa_ref[hd]) + _bdot_nt(dpi, wx_ref[hd])
            xt = xh.T
            dwa_ref[hd] += _bdot(xt, dpr)
            dwx_ref[hd] += _bdot(xt, dpi)
            dxc_ref[:, hs] = dx + add_ref[:, hs] if has_add else dx

    blk = pl.BlockSpec((tb, w), lambda k: (bi(k), 0))
    vec = pl.BlockSpec((1, w), lambda k: (0, 0))
    wsp = pl.BlockSpec((nh, LRU_HEAD_DIM, LRU_HEAD_DIM), lambda k: (0, 0, 0))
    wshape = jax.ShapeDtypeStruct((nh, LRU_HEAD_DIM, LRU_HEAD_DIM), F32)
    vshape = jax.ShapeDtypeStruct((1, w), F32)
    return _call(
        body, name, (nb,),
        [blk, wsp, wsp, vec, vec, vec, vec, blk, pl.BlockSpec((SUBLANES, w), halo), blk, vec] + ([blk] if has_add else []),
        [blk, wsp, wsp, vec, vec, vec, vec],
        [jax.ShapeDtypeStruct((t, w), F32), wshape, wshape, vshape, vshape, vshape, vshape],
        [pltpu.VMEM((tb, w), F32)] * 5 + [pltpu.VMEM((SUBLANES, w), F32)], ("arbitrary",),
        (xc, wa, wx, ba, bx, lam, h0, h, h, dh, dfin, *([addend] if has_add else [])), rider)


def _coords():
    return lax.axis_index("x"), lax.axis_index("y"), lax.axis_index("c")


HALF_ROWS_ALIGN = 32


def _all_gather(shards, name):
    n = len(shards)
    start, wait, sems = _gather_rounds(shards)

    def body(*refs):
        for rnd in range(3):
            start(rnd, refs[:n], refs[n:2 * n], refs[2 * n:])
            wait(rnd, refs[:n], refs[n:2 * n], refs[2 * n:])

    any_spec = pl.BlockSpec(memory_space=pl.ANY)
    return pl.pallas_call(
        body, name=name, in_specs=[any_spec] * n, out_specs=[any_spec] * n,
        out_shape=[jax.ShapeDtypeStruct((N_DEV,) + s.shape, s.dtype) for s in shards], scratch_shapes=sems,
    )(*shards)


def _gather_riders(shards):
    n = len(shards)
    start, wait, sems = _gather_rounds(shards)
    out_shape = [jax.ShapeDtypeStruct((N_DEV,) + s.shape, s.dtype) for s in shards]

    def make(rnd, bufs=()):
        return _Rider(list(shards) + list(bufs), out_shape, {n + i: i for i in range(len(bufs))}, sems,
                      lambda r_in, r_out, s: start(rnd, r_in[:n], r_out, s), lambda r_in, r_out, s: wait(rnd, r_in[:n], r_out, s))
    return make


def _gather_rounds(shards):
    n = len(shards)
    split = [s.ndim >= 2 and s.shape[0] % HALF_ROWS_ALIGN == 0 for s in shards]
    n_copies = 10

    def plan(ins, outs, sem_refs):
        send_sems, recv_sems, local_sems = sem_refs
        x, y, c = _coords()
        slot = lambda p: 4 * p[0] + 2 * p[1] + p[2]
        me, xp, yp, dg = (x, y, c), (1 - x, y, c), (x, 1 - y, c), (1 - x, 1 - y, c)
        other_core = lambda p: (p[0], p[1], 1 - p[2])

        def rows(ref, i, half):
            r = shards[i].shape[0]
            if half == "all" or not split[i]:
                return ref
            return ref.at[pl.ds(0, r // 2)] if half == "a" else ref.at[pl.ds(r // 2, r // 2)]

        def copy(i, j, holder, half, to, own=False):
            dst = rows(outs[i].at[slot(holder)], i, half)
            src = rows(ins[i], i, half) if own else dst
            return pltpu.make_async_remote_copy(src_ref=src, dst_ref=dst, send_sem=send_sems.at[i, j],
                                                recv_sem=recv_sems.at[i, j], device_id=to, device_id_type=MESH)

        rounds = [
            ([(0, me, "a", xp, True), (1, me, "b", yp, True)],
             [(0, xp, "a"), (1, yp, "b")]),
            ([(2, me, "a", yp, True), (3, xp, "a", yp, False), (4, me, "b", xp, True), (5, yp, "b", xp, False)],
             [(2, yp, "a"), (3, dg, "a"), (4, xp, "b"), (5, dg, "b")]),
            ([(6, me, "all", other_core(me), True)] + [(7 + q, p, "all", other_core(me), False) for q, p in enumerate((xp, yp, dg))],
             [(6 + q, other_core(p), "all") for q, p in enumerate((me, xp, yp, dg))]),
        ]
        local = lambda: [pltpu.make_async_copy(ins[i], outs[i].at[slot(me)], local_sems.at[i]) for i in range(n)]
        travels = lambda i, half: half != "b" or split[i]
        return rounds, copy, local, travels, me

    def start(rnd, ins, outs, sem_refs):
        rounds, copy, local, travels, _ = plan(ins, outs, sem_refs)
        if rnd == 0:
            for cp in local():
                cp.start()
        for j, holder, half, to, own in rounds[rnd][0]:
            for i in range(n):
                if travels(i, half):
                    copy(i, j, holder, half, to, own).start()

    def wait(rnd, ins, outs, sem_refs):
        rounds, copy, local, travels, me = plan(ins, outs, sem_refs)
        for j, holder, half in rounds[rnd][1]:
            for i in range(n):
                if travels(i, half):
                    copy(i, j, holder, half, me).wait_recv()
        for j, holder, half, to, own in rounds[rnd][0]:
            for i in range(n):
                if travels(i, half):
                    copy(i, j, holder, half, to, own).wait_send()
        if rnd == 0:
            for cp in local():
                cp.wait()

    sems = [pltpu.SemaphoreType.DMA((n, n_copies)), pltpu.SemaphoreType.DMA((n, n_copies)), pltpu.SemaphoreType.DMA((n,))]
    return start, wait, sems


def _swap_rider(bufs, axes):
    n = len(bufs)

    def copies(ins, outs, sem_refs):
        x, y, c = _coords()
        to = {"x": (1 - x, y, c), "y": (x, 1 - y, c), "c": (x, y, 1 - c)}
        return [pltpu.make_async_remote_copy(src_ref=ins[i], dst_ref=outs[i], send_sem=sem_refs[0].at[i], recv_sem=sem_refs[1].at[i],
                                             device_id=to[axes[i]], device_id_type=MESH) for i in range(n)]

    def start(ins, outs, sem_refs):
        for cp in copies(ins, outs, sem_refs):
            cp.start()

    def wait(ins, outs, sem_refs):
        for cp in copies(ins, outs, sem_refs):
            cp.wait()

    return _Rider(list(bufs), [jax.ShapeDtypeStruct(b.shape, b.dtype) for b in bufs], {},
                  [pltpu.SemaphoreType.DMA((n,)), pltpu.SemaphoreType.DMA((n,))], start, wait)


def _swap(bufs, axes, name):
    n = len(bufs)
    rider = _swap_rider(bufs, axes)

    def body(*refs):
        rider.start(refs[:n], refs[n:2 * n], refs[2 * n:])
        rider.wait(refs[:n], refs[n:2 * n], refs[2 * n:])

    any_spec = pl.BlockSpec(memory_space=pl.ANY)
    return pl.pallas_call(body, name=name, in_specs=[any_spec] * n, out_specs=[any_spec] * n, out_shape=rider.out_shape,
                          scratch_shapes=rider.sems)(*bufs)


def _as2d(a):
    return a.reshape(-1, a.shape[-1])


def _add_f32(a, b, name):
    (out,), _ = _rowwise(lambda r, s: ([r[0].astype(F32) + r[1].astype(F32)], []), [_as2d(a), _as2d(b)], [],
                         [(a.shape[-1], F32)], [], name)
    return out.reshape(a.shape)


class _ReduceScatter:
    def __init__(self, parts, name):
        self.n, self.name, self.rnd = len(parts), name, 0
        self.cur = [p.reshape((2, 2, 2) + p.shape[1:]) for p in parts]

    def stage(self):
        x, y, c = _coords()
        n = self.n
        if self.rnd == 0:
            bufs, idxs, dims, axes = self.cur, [c] * n, [2] * n, ["c"] * n
        elif self.rnd == 1:
            bufs = [p[:, :, :p.shape[2] // 2] for p in self.cur] + [p[:, :, p.shape[2] // 2:] for p in self.cur]
            idxs, dims, axes = [y] * n + [x] * n, [1] * n + [0] * n, ["y"] * n + ["x"] * n
        else:
            bufs, idxs, dims, axes = self.cur, [x] * n + [y] * n, [0] * (2 * n), ["x"] * n + ["y"] * n
        pick = lambda p, idx, dim: lax.dynamic_index_in_dim(p, idx, dim, keepdims=False)
        self.keep = [pick(p, i_, d_) for p, i_, d_ in zip(bufs, idxs, dims)]
        return [pick(p, 1 - i_, d_).astype(BF16) for p, i_, d_ in zip(bufs, idxs, dims)], axes

    def absorb(self, got):
        self.cur = [_add_f32(k, g, f"{self.name}_add{self.rnd}_{i}") for i, (k, g) in enumerate(zip(self.keep, got))]
        self.rnd += 1

    def result(self):
        return [jnp.concatenate([self.cur[i], self.cur[self.n + i]], axis=0) for i in range(self.n)]


def _reduce_scatter(parts, name):
    rs = _ReduceScatter(parts, name)
    for rnd in range(3):
        send, axes = rs.stage()
        rs.absorb(_swap(send, axes, f"{name}_swap{rnd}"))
    return rs.result()


def _pack(arrs):
    flat = jnp.concatenate([a.reshape(-1).astype(F32) for a in arrs])
    rows = -(-flat.shape[0] // LANES)
    rows = -(-rows // 512) * 512
    return jnp.pad(flat, (0, rows * LANES - flat.shape[0])).reshape(rows, LANES)


def _unpack(buf, like):
    flat, out, off = buf.reshape(-1), [], 0
    for a in like:
        out.append(flat[off:off + a.size].reshape(a.shape))
        off += a.size
    return out


def _all_reduce(arrs, name, narrow=()):
    wide = [a for i, a in enumerate(arrs) if i not in narrow]
    thin = [a for i, a in enumerate(arrs) if i in narrow]
    packs = [_pack(wide)] + ([_pack(thin).astype(BF16)] if thin else [])
    gathered = _all_gather(packs, name + "_gather")
    sums = []
    for j, g in enumerate(gathered):
        (tot,), _ = _rowwise(lambda r, s: ([functools.reduce(lambda u, v: u + v, r)], []), [g[i] for i in range(N_DEV)], [],
                             [(LANES, F32)], [], f"{name}_sum{j}")
        sums.append(tot)
    out_wide, out_thin = iter(_unpack(sums[0], wide)), iter(_unpack(sums[1], thin) if thin else [])
    return [next(out_thin) if i in narrow else next(out_wide) for i in range(len(arrs))]


def _adamw(w, g, m, v, name):
    def fn(r, s):
        w_, g_, m_, v_ = r
        m_ = ADAM_B1 * m_ + (1.0 - ADAM_B1) * g_
        v_ = ADAM_B2 * v_ + (1.0 - ADAM_B2) * jnp.square(g_)
        m_hat = m_ / (1.0 - ADAM_B1 ** ADAM_STEP)
        v_hat = v_ / (1.0 - ADAM_B2 ** ADAM_STEP)
        return [-ADAM_LR * (m_hat / (jnp.sqrt(v_hat) + ADAM_EPS) + ADAM_WD * w_), m_, v_], []

    c = w.shape[-1]
    (d, nm, nv), _ = _rowwise(fn, [_as2d(a) for a in (w, g, m, v)], [], [(c, F32)] * 3, [], name)
    return d.reshape(w.shape), nm.reshape(w.shape), nv.reshape(w.shape)


def _norm_mod(x, gain, sc, sh, name):
    def fn(r, s):
        xv = r[0]
        xn = xv * lax.rsqrt(_rowmean(xv * xv) + EPS)
        return [xn * s[0] * (1.0 + s[1]) + s[2]], []
    (h,), _ = _rowwise(fn, [x], [gain, sc, sh], [(x.shape[1], BF16)], [], name)
    return h


def _norm_mod_bwd(x, dh, dres, gain, sc, name):
    d = x.shape[1]

    def fn(r, s):
        xv, dhv = r[0], r[1]
        rs = lax.rsqrt(_rowmean(xv * xv) + EPS)
        xn = xv * rs
        dxn = dhv * (1.0 + s[1]) * s[0]
        dx = rs * (dxn - xn * _rowmean(dxn * xn))
        if dres is not None:
            dx = dx + r[2]
        return [dx], [_colsum(dhv), _colsum(dhv * xn * s[0]), _colsum(dhv * (1.0 + s[1]) * xn)]
    (dx,), accs = _rowwise(fn, [x, dh] + ([dres] if dres is not None else []), [gain, sc], [(d, F32)], [d, d, d], name)
    return dx, accs


def _ada_rows(mod_row, d):
    return [mod_row[:, i * d:(i + 1) * d] for i in range(N_MOD)]


def _to_col_major(t):
    n, c = t.shape
    return t.reshape(n // GRID_W, GRID_W, c).transpose(1, 0, 2).reshape(n, c)


def _to_row_major(t):
    n, c = t.shape
    return t.reshape(GRID_W, n // GRID_W, c).transpose(1, 0, 2).reshape(n, c)


def kernel(x, c, ctx, c_ctx, w_ada, b_ada, norm_mix, norm_ffn, w_in, ssd_conv_w, ssd_conv_b, ssd_dt_bias, ssd_a_log, ssd_d, ssd_norm, w_out_ssd, lru_conv_w, lru_conv_b, lru_w_a, lru_b_a, lru_w_x, lru_b_x, lru_lambda, w_out_lru, w_gate, b_gate, w_o, ffn_w13, ffn_w2, final_norm, loss_target, m_c_ctx, m_w_ada, m_b_ada, m_norm_mix, m_norm_ffn, m_w_in, m_ssd_conv_w, m_ssd_conv_b, m_ssd_dt_bias, m_ssd_a_log, m_ssd_d, m_ssd_norm, m_w_out_ssd, m_lru_conv_w, m_lru_conv_b, m_lru_w_a, m_lru_b_a, m_lru_w_x, m_lru_b_x, m_lru_lambda, m_w_out_lru, m_w_gate, m_b_gate, m_w_o, m_ffn_w13, m_ffn_w2, m_final_norm, v_c_ctx, v_w_ada, v_b_ada, v_norm_mix, v_norm_ffn, v_w_in, v_ssd_conv_w, v_ssd_conv_b, v_ssd_dt_bias, v_ssd_a_log, v_ssd_d, v_ssd_norm, v_w_out_ssd, v_lru_conv_w, v_lru_conv_b, v_lru_w_a, v_lru_b_a, v_lru_w_x, v_lru_b_x, v_lru_lambda, v_w_out_lru, v_w_gate, v_b_gate, v_w_o, v_ffn_w13, v_ffn_w2, v_final_norm):
    names = ["c_ctx", "w_ada", "b_ada", "norm_mix", "norm_ffn", "w_in", "ssd_conv_w", "ssd_conv_b", "ssd_dt_bias", "ssd_a_log",
             "ssd_d", "ssd_norm", "w_out_ssd", "lru_conv_w", "lru_conv_b", "lru_w_a", "lru_b_a", "lru_w_x", "lru_b_x", "lru_lambda",
             "w_out_lru", "w_gate", "b_gate", "w_o", "ffn_w13", "ffn_w2", "final_norm"]
    wts = dict(zip(names, (c_ctx, w_ada, b_ada, norm_mix, norm_ffn, w_in, ssd_conv_w, ssd_conv_b, ssd_dt_bias, ssd_a_log, ssd_d, ssd_norm, w_out_ssd, lru_conv_w, lru_conv_b, lru_w_a, lru_b_a, lru_w_x, lru_b_x, lru_lambda, w_out_lru, w_gate, b_gate, w_o, ffn_w13, ffn_w2, final_norm)))
    mom1 = dict(zip(names, (m_c_ctx, m_w_ada, m_b_ada, m_norm_mix, m_norm_ffn, m_w_in, m_ssd_conv_w, m_ssd_conv_b, m_ssd_dt_bias, m_ssd_a_log, m_ssd_d, m_ssd_norm, m_w_out_ssd, m_lru_conv_w, m_lru_conv_b, m_lru_w_a, m_lru_b_a, m_lru_w_x, m_lru_b_x, m_lru_lambda, m_w_out_lru, m_w_gate, m_b_gate, m_w_o, m_ffn_w13, m_ffn_w2, m_final_norm)))
    mom2 = dict(zip(names, (v_c_ctx, v_w_ada, v_b_ada, v_norm_mix, v_norm_ffn, v_w_in, v_ssd_conv_w, v_ssd_conv_b, v_ssd_dt_bias, v_ssd_a_log, v_ssd_d, v_ssd_norm, v_w_out_ssd, v_lru_conv_w, v_lru_conv_b, v_lru_w_a, v_lru_b_a, v_lru_w_x, v_lru_b_x, v_lru_lambda, v_w_out_lru, v_w_gate, v_b_gate, v_w_o, v_ffn_w13, v_ffn_w2, v_final_norm)))

    xi, yi, ci_ = _coords()
    me = 4 * xi + 2 * yi + ci_
    x2, tgt, ctx2 = x[0], loss_target[0], ctx[0]
    t, d = x2.shape
    tc = ctx2.shape[0]
    ws = N_DEV * w_out_ssd.shape[1]
    wl = N_DEV * w_out_lru.shape[1]
    heads = ws // SSD_HEAD_DIM
    gn = SSD_GROUPS * SSD_STATE
    nh = wl // LRU_HEAD_DIM
    ff = N_DEV * ffn_w2.shape[1]
    ada_cols = w_ada.shape[2]

    smalls = [c, ssd_conv_w[0], lru_conv_w[0], lru_b_a[0], lru_b_x[0], lru_lambda[0]]
    got = _all_gather(smalls + [w_in[0].astype(BF16), w_gate[0].astype(BF16)], "gather_weights")
    late_gather = _gather_riders([w.astype(BF16) for w in (w_out_ssd[0], w_out_lru[0], w_o[0], ffn_w13[0], ffn_w2[0])])
    c_all = got[0].reshape(N_DEV, d)
    colcat = lambda g: jnp.moveaxis(g, 0, -2).reshape(g.shape[1:-1] + (N_DEV * g.shape[-1],))
    conv_w_s, conv_w_l, lru_ba, lru_bx, lru_lam = (colcat(g) for g in got[1:6])
    wg_in, wg_gate = got[6:]
    w_in_f = colcat(wg_in)
    o_z, o_xbc, o_dt, o_xr, o_yr = 0, ws, 2 * ws + 2 * gn, 2 * ws + 2 * gn + 2 * heads, 2 * ws + 2 * gn + 2 * heads + wl
    w_z, w_xbc = w_in_f[:, o_z:o_xbc], w_in_f[:, o_xbc:o_dt]
    w_dt = jnp.pad(w_in_f[:, o_dt:o_xr], ((0, 0), (0, LANES - 2 * heads)))
    w_xr, w_yr = w_in_f[:, o_xr:o_yr], w_in_f[:, o_yr:]
    w_gf = colcat(wg_gate)
    w_gs, w_gr = w_gf[:, :d], w_gf[:, d:]

    cond = jnp.concatenate([c_all, c_ctx[None, :], jnp.zeros((16 - N_DEV - 1, d), F32)], axis=0)
    cond_act = _silu(cond)
    mod_l = _mm([(cond_act, w_ada[0])], "nn", F32, "ada_fwd")
    (mod_g,) = _all_gather([mod_l], "gather_mod")
    mod_full = colcat(mod_g) + b_ada
    sh_m, sc_m, g_m, sh_f, sc_f, g_f = _ada_rows(lax.dynamic_slice_in_dim(mod_full, me, 1, 0), d)
    csh_m, csc_m = _ada_rows(mod_full[N_DEV:N_DEV + 1], d)[:2]

    bias_s, alog_s = ssd_dt_bias[0], ssd_a_log[0]
    d_full = jnp.repeat(ssd_d[0], SSD_HEAD_DIM)[None, :]
    zeros_state = jnp.zeros((heads // 2, SSD_STATE, LANES), F32)
    zeros_l = jnp.zeros((1, wl), F32)
    wa, wx = lru_w_a[0], lru_w_x[0]

    def mixer_inputs(hh, tag, full):
        xbc = _mm([(hh, w_xbc)], "nn", F32, f"{tag}_proj_xbc")
        dtr = _mm([(hh, w_dt)], "nn", F32, f"{tag}_proj_dt")[:, :2 * heads]
        xr = _mm([(hh, w_xr)], "nn", F32, f"{tag}_proj_xr")
        cx = _conv(xbc, conv_w_s, ssd_conv_b, CONV_LEFT, True, f"{tag}_ssd_conv")
        xr_seq = _to_col_major(xr) if full else xr
        xc = _conv(xr_seq, conv_w_l, lru_conv_b, CONV_LEFT, False, f"{tag}_lru_conv")
        return xbc, dtr, xr_seq, cx, xc

    def lru_args(dirn):
        return wa[dirn], wx[dirn], lru_ba[dirn:dirn + 1], lru_bx[dirn:dirn + 1], lru_lam[dirn:dirn + 1]

    hc = _norm_mod(ctx2, norm_mix, csc_m, csh_m, "ctx_norm")
    c_xbc, c_dtr, c_xr_seq, c_cx, c_xc = mixer_inputs(hc, "ctx", False)
    c_ssd = []
    for dirn in (0, 1):
        _, se, sf = _ssd_fwd(c_cx, c_dtr[:, dirn * heads:(dirn + 1) * heads], bias_s[dirn:dirn + 1], alog_s[dirn:dirn + 1],
                             zeros_state, bool(dirn), f"ctx_ssd_fwd{dirn}")
        c_ssd.append((se, sf))
    c_lru = []
    for dirn in (0, 1):
        res = _lru_fwd(c_xc, *lru_args(dirn), zeros_l, None, bool(dirn), f"ctx_lru_fwd{dirn}")
        c_lru.append((res[0], res[1]))

    h = _norm_mod(x2, norm_mix, sc_m, sh_m, "lat_norm")
    z = _mm([(h, w_z)], "nn", BF16, "lat_proj_z")
    yr = _mm([(h, w_yr)], "nn", BF16, "lat_proj_yr")
    gp_s = _mm([(h, w_gs)], "nn", BF16, "lat_proj_gs")
    gp_r = _mm([(h, w_gr)], "nn", BF16, "lat_proj_gr")
    l_xbc, l_dtr, l_xr_seq, l_cx, l_xc = mixer_inputs(h, "lat", True)
    l_ssd, late = [], ()
    for dirn in (0, 1):
        (yv, se, _), late = _ssd_fwd(l_cx, l_dtr[:, dirn * heads:(dirn + 1) * heads], bias_s[dirn:dirn + 1], alog_s[dirn:dirn + 1],
                                     c_ssd[dirn][1], bool(dirn), f"lat_ssd_fwd{dirn}", rider=late_gather(dirn, late))
        l_ssd.append((yv, se))
    (hf, _), late = _lru_fwd(l_xc, *lru_args(0), c_lru[0][1], None, False, "lat_lru_fwd0", rider=late_gather(2, late))
    hb, _, r_seq = _lru_fwd(l_xc, *lru_args(1), c_lru[1][1], hf, True, "lat_lru_fwd1")
    wg_os, wg_ol, wg_o, wg_13, wg_2 = late
    w_os, w_ol, w_oo, w_2 = wg_os.reshape(ws, d), wg_ol.reshape(wl, d), wg_o.reshape(d, d), wg_2.reshape(ff, d)
    w_13f = colcat(wg_13)
    w_1, w_3 = w_13f[:, :ff], w_13f[:, ff:]

    def ssd_out(r, s):
        yf, yb, cx_, z_ = r
        yv = yf + yb + s[0] * cx_[:, :ws]
        yz = yv * _silu(z_)
        return [yz * lax.rsqrt(_rowmean(yz * yz) + EPS) * s[1]], []
    (ys,), _ = _rowwise(ssd_out, [l_ssd[0][0], l_ssd[1][0], l_cx, z], [d_full, ssd_norm], [(ws, BF16)], [], "lat_ssd_out")
    o_s = _mm([(ys, w_os)], "nn", BF16, "lat_out_ssd")

    r_out = _to_row_major(r_seq)
    (gr,), _ = _rowwise(lambda r, s: ([r[0] * _gelu(r[1])], []), [r_out, yr], [], [(wl, BF16)], [], "lat_lru_out")
    o_r = _mm([(gr, w_ol)], "nn", BF16, "lat_out_lru")

    bg_s, bg_r = b_gate[:, :d], b_gate[:, d:]
    (mix,), _ = _rowwise(lambda r, s: ([_sigmoid(r[0] + s[0]) * r[2] + _sigmoid(r[1] + s[1]) * r[3]], []),
                         [gp_s, gp_r, o_s, o_r], [bg_s, bg_r], [(d, BF16)], [], "lat_merge")
    mixo = _mm([(mix, w_oo)], "nn", F32, "lat_out_o")

    def resid_norm(r, s):
        x1v = r[0] + s[0] * r[1]
        xn = x1v * lax.rsqrt(_rowmean(x1v * x1v) + EPS)
        return [x1v, xn * s[1] * (1.0 + s[2]) + s[3]], []
    (x1, h2), _ = _rowwise(resid_norm, [x2, mixo], [g_m, norm_ffn, sc_f, sh_f], [(d, F32), (d, BF16)], [], "lat_resid1")
    fg = _mm([(h2, w_1)], "nn", BF16, "ffn_gate")
    fu = _mm([(h2, w_3)], "nn", BF16, "ffn_up")
    (act,), _ = _rowwise(lambda r, s: ([_silu(r[0]) * r[1]], []), [fg, fu], [], [(ff, BF16)], [], "ffn_act")
    fo = _mm([(act, w_2)], "nn", F32, "ffn_down")

    fn_gain = final_norm[None, :]

    def head(r, s):
        x1v, fv, tv = r
        xv = x1v + s[0] * fv
        rs = lax.rsqrt(_rowmean(xv * xv) + EPS)
        xn = xv * rs
        err = xn * s[1] - tv
        dyv = err * (1.0 / d)
        dxn = dyv * s[1]
        dxv = rs * (dxn - xn * _rowmean(dxn * xn))
        return [dxv, dxv * s[0]], [_colsum(err * err), _colsum(dyv * xn), _colsum(dxv * fv)]
    (dx2, dfo), (sq, g_final, dg_f) = _rowwise(head, [x1, fo, tgt], [g_f, fn_gain], [(d, F32), (d, BF16)], [d, d, d], "loss_head")
    loss = lax.psum(0.5 * jnp.sum(sq) / d, ("x", "y", "c"))

    dact = _mm([(dfo, w_2)], "nt", BF16, "ffn_down_dx")
    gw_2 = _mm([(act, dfo)], "tn", F32, "ffn_down_dw")
    (dfg, dfu), _ = _rowwise(lambda r, s: ([r[0] * r[2] * _dsilu(r[1]), r[0] * _silu(r[1])], []), [dact, fg, fu], [],
                             [(ff, BF16), (ff, BF16)], [], "ffn_act_bwd")
    gw_13 = _mm([(h2, dfg)], "tn", F32, "ffn_gate_dw", slots=(ff // 4, N_DEV, 0, None))
    gw_13 = _mm([(h2, dfu)], "tn", F32, "ffn_up_dw", slots=(ff // 4, N_DEV, N_DEV // 2, gw_13))
    slots = lambda g: jnp.moveaxis(g.reshape(g.shape[0], N_DEV, g.shape[1] // N_DEV), 1, 0)
    rslots = lambda g: g.reshape(N_DEV, g.shape[0] // N_DEV, g.shape[1])
    ffn_rs = _ReduceScatter([gw_13, rslots(gw_2)], "ffn_rs")
    dh2, got = _mm([(dfg, w_1), (dfu, w_3)], "nt", F32, "ffn_dx", rider=_swap_rider(*ffn_rs.stage()))
    ffn_rs.absorb(got)
    dx1, (dsh_f, dsc_f, g_norm_ffn) = _norm_mod_bwd(x1, dh2, dx2, norm_ffn, sc_f, "lat_norm_ffn_bwd")

    (dmixo,), (dg_m,) = _rowwise(lambda r, s: ([s[0] * r[0]], [_colsum(r[0] * r[1])]), [dx1, mixo], [g_m], [(d, BF16)], [d], "lat_resid1_bwd")
    dmix = _mm([(dmixo, w_oo)], "nt", BF16, "lat_out_o_dx")
    gw_o = _mm([(mix, dmixo)], "tn", F32, "lat_out_o_dw")

    def merge_bwd(r, s):
        dm, gps, gpr, os_, or_ = r
        g1, g2 = _sigmoid(gps + s[0]), _sigmoid(gpr + s[1])
        dg1, dg2 = dm * os_ * g1 * (1.0 - g1), dm * or_ * g2 * (1.0 - g2)
        return [dm * g1, dm * g2, dg1, dg2], [_colsum(dg1), _colsum(dg2)]
    (do_s, do_r, dgp_s, dgp_r), (gb_gs, gb_gr) = _rowwise(merge_bwd, [dmix, gp_s, gp_r, o_s, o_r], [bg_s, bg_r],
                                                            [(d, BF16)] * 4, [d, d], "lat_merge_bwd")
    gw_gate = _mm([(h, dgp_s)], "tn", F32, "lat_gs_dw", slots=(d // 4, N_DEV, 0, None))
    gw_gate = _mm([(h, dgp_r)], "tn", F32, "lat_gr_dw", slots=(d // 4, N_DEV, N_DEV // 2, gw_gate))

    dgr = _mm([(do_r, w_ol)], "nt", BF16, "lat_out_lru_dx")
    gw_ol = _mm([(gr, do_r)], "tn", F32, "lat_out_lru_dw")
    (dr_out, dyr), _ = _rowwise(lambda r, s: ([r[0] * _gelu(r[2]), r[0] * r[1] * _dgelu(r[2])], []), [dgr, r_out, yr], [],
                                [(wl, F32), (wl, BF16)], [], "lat_lru_out_bwd")
    dr_seq = _to_col_major(dr_out)
    lb0, got = _lru_bwd(l_xc, *lru_args(0), c_lru[0][1], hf, dr_seq, zeros_l, None, False, "lat_lru_bwd0", rider=_swap_rider(*ffn_rs.stage()))
    ffn_rs.absorb(got)
    lb1, got = _lru_bwd(l_xc, *lru_args(1), c_lru[1][1], hb, dr_seq, zeros_l, lb0[0], True, "lat_lru_bwd1", rider=_swap_rider(*ffn_rs.stage()))
    ffn_rs.absorb(got)
    dxr_seq, g_lconv_w, g_lconv_b = _conv_bwd(l_xr_seq, [lb1[0]], lambda v: v[0], conv_w_l, lru_conv_b, False, "lat_lru_conv_bwd")
    dxr = _to_row_major(dxr_seq)

    dys = _mm([(do_s, w_os)], "nt", BF16, "lat_out_ssd_dx")
    gw_os = _mm([(ys, do_s)], "tn", F32, "lat_out_ssd_dw")

    def ssd_out_bwd(r, s):
        yf, yb, cx_, z_, dyn = r
        xs_ = cx_[:, :ws]
        yv = yf + yb + s[0] * xs_
        sz = _silu(z_)
        yz = yv * sz
        rs = lax.rsqrt(_rowmean(yz * yz) + EPS)
        yzn = yz * rs
        dyzn = dyn * s[1]
        dyz = rs * (dyzn - yzn * _rowmean(dyzn * yzn))
        dyv = dyz * sz
        return [dyv, dyz * yv * _dsilu(z_), dyv * s[0]], [_colsum(dyv * xs_), _colsum(dyn * yzn)]
    (dy_ssd, dz, dxs_skip), (g_dfull, g_ssd_norm) = _rowwise(ssd_out_bwd, [l_ssd[0][0], l_ssd[1][0], l_cx, z, dys], [d_full, ssd_norm],
                                                              [(ws, F32), (ws, BF16), (ws, F32)], [ws, ws], "lat_ssd_out_bwd")
    sb = [_ssd_bwd(l_cx, l_dtr[:, dirn * heads:(dirn + 1) * heads], bias_s[dirn:dirn + 1], alog_s[dirn:dirn + 1],
                   dy_ssd, l_ssd[dirn][1], zeros_state, bool(dirn), f"lat_ssd_bwd{dirn}") for dirn in (0, 1)]
    dxbc, g_sconv_w, g_sconv_b = _conv_bwd(
        l_xbc, [dxs_skip, sb[0][0], sb[1][0], sb[0][1], sb[1][1], sb[0][2], sb[1][2]],
        lambda v: jnp.concatenate([v[0] + v[1] + v[2], v[3] + v[4], v[5] + v[6]], axis=1), conv_w_s, ssd_conv_b, True, "lat_ssd_conv_bwd")
    ddt = jnp.pad(jnp.concatenate([sb[0][3], sb[1][3]], axis=1), ((0, 0), (0, LANES - 2 * heads)))

    zeros_cy = jnp.zeros((tc, ws), F32)
    csb = [_ssd_bwd(c_cx, c_dtr[:, dirn * heads:(dirn + 1) * heads], bias_s[dirn:dirn + 1], alog_s[dirn:dirn + 1],
                    zeros_cy, c_ssd[dirn][0], sb[dirn][6], bool(dirn), f"ctx_ssd_bwd{dirn}") for dirn in (0, 1)]
    c_dxbc, cg_sconv_w, cg_sconv_b = _conv_bwd(
        c_xbc, [csb[0][0], csb[1][0], csb[0][1], csb[1][1], csb[0][2], csb[1][2]],
        lambda v: jnp.concatenate([v[0] + v[1], v[2] + v[3], v[4] + v[5]], axis=1), conv_w_s, ssd_conv_b, True, "ctx_ssd_conv_bwd")
    c_ddt = jnp.pad(jnp.concatenate([csb[0][3], csb[1][3]], axis=1), ((0, 0), (0, LANES - 2 * heads)))
    zeros_ch = jnp.zeros((tc, wl), F32)
    clb0 = _lru_bwd(c_xc, *lru_args(0), zeros_l, c_lru[0][0], zeros_ch, lb0[6], None, False, "ctx_lru_bwd0")
    clb1 = _lru_bwd(c_xc, *lru_args(1), zeros_l, c_lru[1][0], zeros_ch, lb1[6], clb0[0], True, "ctx_lru_bwd1")
    c_dxr, cg_lconv_w, cg_lconv_b = _conv_bwd(c_xr_seq, [clb1[0]], lambda v: v[0], conv_w_l, lru_conv_b, False, "ctx_lru_conv_bwd")
    dhc = _mm([(c_dxbc, w_xbc), (c_ddt, w_dt), (c_dxr, w_xr)], "nt", F32, "ctx_proj_dx")
    _, (dcsh_m, dcsc_m, cg_norm_mix) = _norm_mod_bwd(ctx2, dhc, None, norm_mix, csc_m, "ctx_norm_bwd")

    gw_z = _mm([(h, dz)], "tn", F32, "lat_z_dw")
    gw_yr = _mm([(h, dyr)], "tn", F32, "lat_yr_dw")
    gw_xbc = _mm([(h, dxbc), (hc, c_dxbc)], "tn", F32, "proj_xbc_dw")
    gw_dt = _mm([(h, ddt), (hc, c_ddt)], "tn", F32, "proj_dt_dw")[:, :2 * heads]
    gw_xr = _mm([(h, dxr), (hc, c_dxr)], "tn", F32, "proj_xr_dw")
    dh = _mm([(dz, w_z), (dxbc, w_xbc), (ddt, w_dt)], "nt", F32, "lat_proj_dx0")
    dh = _mm([(dxr, w_xr), (dyr, w_yr)], "nt", F32, "lat_proj_dx1", init=dh)
    dh = _mm([(dgp_s, w_gs), (dgp_r, w_gr)], "nt", F32, "lat_proj_dx2", init=dh)
    grad_x, (dsh_m, dsc_m, g_norm_mix) = _norm_mod_bwd(x2, dh, dx1, norm_mix, sc_m, "lat_norm_bwd")

    zero_d = jnp.zeros((1, d), F32)
    dmod_rows = jnp.concatenate([jnp.concatenate([dsh_m, dsc_m, dg_m, dsh_f, dsc_f, dg_f], axis=1),
                                 jnp.concatenate([dcsh_m, dcsc_m, zero_d, zero_d, zero_d, zero_d], axis=1)], axis=0)
    (dmod_g,) = _all_gather([dmod_rows], "gather_dmod")
    dmod_c = functools.reduce(lambda u, v: u + v, [dmod_g[i, 1] for i in range(N_DEV)])
    dmod9 = jnp.concatenate([dmod_g[:, 0], dmod_c[None, :], jnp.zeros((16 - N_DEV - 1, N_MOD * d), F32)], axis=0)
    g_b_ada = jnp.sum(dmod9, axis=0, keepdims=True)
    dmod_l = lax.dynamic_slice_in_dim(dmod9, me * ada_cols, ada_cols, 1)
    g_w_ada = _mm([(cond_act, dmod_l)], "tn", F32, "ada_dw")
    dcond_l = _mm([(dmod_l, w_ada[0])], "nt", F32, "ada_dx")

    small_parts = [
        dcond_l[N_DEV:N_DEV + 1], g_norm_mix + cg_norm_mix, g_norm_ffn, g_sconv_w + cg_sconv_w, g_sconv_b + cg_sconv_b,
        jnp.concatenate([sb[0][5] + csb[0][5], sb[1][5] + csb[1][5]], axis=0),
        jnp.concatenate([sb[0][4] + csb[0][4], sb[1][4] + csb[1][4]], axis=0),
        g_dfull, g_ssd_norm, g_lconv_w + cg_lconv_w, g_lconv_b + cg_lconv_b,
        jnp.stack([lb0[1] + clb0[1], lb1[1] + clb1[1]]), jnp.concatenate([lb0[3] + clb0[3], lb1[3] + clb1[3]], axis=0),
        jnp.stack([lb0[2] + clb0[2], lb1[2] + clb1[2]]), jnp.concatenate([lb0[4] + clb0[4], lb1[4] + clb1[4]], axis=0),
        jnp.concatenate([lb0[5] + clb0[5], lb1[5] + clb1[5]], axis=0),
        jnp.concatenate([gb_gs, gb_gr], axis=1), g_final,
    ]
    (s_cctx, s_nmix, s_nffn, s_sconv_w, s_sconv_b, s_dtb, s_da, s_dfull, s_snorm, s_lconv_w, s_lconv_b,
     s_wa, s_ba, s_wx, s_bx, s_dsp, s_bgate, s_final) = _all_reduce(small_parts, "small_grads", narrow=(11, 13))
    shard = lambda a, n_: lax.dynamic_slice_in_dim(a, me * n_, n_, a.ndim - 1)
    grads = {
        "c_ctx": (s_cctx * _dsilu(c_ctx[None, :]))[0],
        "w_ada": g_w_ada[None], "b_ada": g_b_ada, "norm_mix": s_nmix, "norm_ffn": s_nffn,
        "ssd_conv_w": shard(s_sconv_w, ssd_conv_w.shape[2])[None], "ssd_conv_b": s_sconv_b,
        "ssd_dt_bias": s_dtb[None], "ssd_a_log": (s_da * (-jnp.exp(alog_s)))[None],
        "ssd_d": jnp.sum(s_dfull.reshape(heads, SSD_HEAD_DIM), axis=1)[None], "ssd_norm": s_snorm,
        "lru_conv_w": shard(s_lconv_w, lru_conv_w.shape[2])[None], "lru_conv_b": s_lconv_b,
        "lru_w_a": s_wa[None], "lru_b_a": shard(s_ba, lru_b_a.shape[2])[None],
        "lru_w_x": s_wx[None], "lru_b_x": shard(s_bx, lru_b_x.shape[2])[None],
        "lru_lambda": shard(s_dsp * (-_sigmoid(-lru_lam)), lru_lambda.shape[2])[None],
        "b_gate": s_bgate, "final_norm": s_final[0],
    }

    gw_in = jnp.concatenate([gw_z, gw_xbc, gw_dt, gw_xr, gw_yr], axis=1)
    parts = [slots(gw_in), rslots(gw_os), rslots(gw_ol), gw_gate, rslots(gw_o)]
    for n_, g in zip(["w_in", "w_out_ssd", "w_out_lru", "w_gate", "w_o", "ffn_w13", "ffn_w2"],
                     _reduce_scatter(parts, "grad_rs") + ffn_rs.result()):
        grads[n_] = g[None]

    big_names = ["w_ada", "w_in", "w_out_ssd", "w_out_lru", "w_gate", "w_o", "ffn_w13", "ffn_w2"]
    small_names = [n_ for n_ in names if n_ not in big_names]
    delta, new_m, new_v = {}, {}, {}
    for n_ in big_names:
        delta[n_], new_m[n_], new_v[n_] = _adamw(wts[n_], grads[n_], mom1[n_], mom2[n_], "adamw_" + n_)
    packed = [_pack([src[n_] for n_ in small_names]) for src in (wts, grads, mom1, mom2)]
    outs = _adamw(*packed, "adamw_small")
    like = [wts[n_] for n_ in small_names]
    for dst, buf in zip((delta, new_m, new_v), outs):
        dst.update(zip(small_names, _unpack(buf, like)))

    return (loss, grad_x[None], *[grads[n_].reshape(wts[n_].shape) for n_ in names], *[delta[n_] for n_ in names],
            *[new_m[n_] for n_ in names], *[new_v[n_] for n_ in names])
```

```python
import functools

import numpy as np
import jax
import jax.numpy as jnp
from jax import lax
from jax.experimental import pallas as pl
from jax.experimental.pallas import tpu as pltpu

F32, BF16 = jnp.float32, jnp.bfloat16
MESH = pl.DeviceIdType.MESH
N_DEV = 8
VMEM_LIMIT_BYTES = 48 * 1024 * 1024
ROW_BUDGET_BYTES = 12 * 1024 * 1024
LANES, SUBLANES = 128, 8

EPS = 1e-6
GRID_W = 64
SSD_HEAD_DIM = 64
SSD_GROUPS = 4
SSD_STATE = 128
SSD_CHUNK = 128
CONV_W = 4
CONV_LEFT = 2
LRU_HEAD_DIM = 128
LRU_C = 8.0
LRU_BLOCK = 128
N_MOD = 6
ADAM_LR, ADAM_B1, ADAM_B2, ADAM_EPS, ADAM_WD, ADAM_STEP = 0.001, 0.9, 0.999, 1e-08, 0.01, 10
NEG_BIG = -1e30


def _params(sem):
    return pltpu.CompilerParams(dimension_semantics=sem, vmem_limit_bytes=VMEM_LIMIT_BYTES)


def _tile(dim, pref, align):
    if dim <= pref:
        return dim
    t = (pref // align) * align
    while t >= align:
        if dim % t == 0:
            return t
        t -= align
    return dim


class _Rider:
    def __init__(self, ins, out_shape, aliases, sems, start, wait):
        self.ins, self.out_shape, self.aliases, self.sems, self.start, self.wait = ins, out_shape, aliases, sems, start, wait


def _merge_riders(a, b):
    ai, ao, asem = len(a.ins), len(a.out_shape), len(a.sems)
    aliases = dict(a.aliases)
    aliases.update({ai + i: ao + o for i, o in b.aliases.items()})

    def both(fa, fb):
        def run(ins, outs, sems):
            fa(ins[:ai], outs[:ao], sems[:asem])
            fb(ins[ai:], outs[ao:], sems[asem:])
        return run

    return _Rider(list(a.ins) + list(b.ins), list(a.out_shape) + list(b.out_shape), aliases, list(a.sems) + list(b.sems),
                  both(a.start, b.start), both(a.wait, b.wait))


def _call(body, name, grid, in_specs, out_specs, out_shape, scratch_shapes, sem, operands, rider=None):
    if rider is None:
        return pl.pallas_call(body, name=name, grid=grid, in_specs=in_specs, out_specs=out_specs, out_shape=out_shape,
                              scratch_shapes=scratch_shapes, compiler_params=_params(sem))(*operands)
    n_in, n_out, n_scr = len(in_specs), len(out_specs), len(scratch_shapes)
    ri, ro = len(rider.ins), len(rider.out_shape)
    a, b = n_in, n_in + ri
    c, d = b + n_out, b + n_out + ro
    e = d + n_scr

    def carried(*refs):
        ids = [pl.program_id(i) for i in range(len(grid))]
        first = functools.reduce(lambda u, v: u & v, [i == 0 for i in ids])
        last = functools.reduce(lambda u, v: u & v, [i == g - 1 for i, g in zip(ids, grid)])

        @pl.when(first)
        def _():
            rider.start(refs[a:b], refs[c:d], refs[e:])

        body(*refs[:a], *refs[b:c], *refs[d:e])

        @pl.when(last)
        def _():
            rider.wait(refs[a:b], refs[c:d], refs[e:])

    any_spec = pl.BlockSpec(memory_space=pl.ANY)
    res = pl.pallas_call(
        carried, name=name, grid=grid, in_specs=list(in_specs) + [any_spec] * ri, out_specs=list(out_specs) + [any_spec] * ro,
        out_shape=list(out_shape) + list(rider.out_shape), scratch_shapes=list(scratch_shapes) + list(rider.sems),
        input_output_aliases={n_in + i: n_out + o for i, o in rider.aliases.items()},
        compiler_params=_params(("arbitrary",) * len(grid)),
    )(*operands, *rider.ins)
    return res[:n_out], res[n_out:]


def _bdot(a, b):
    return jnp.dot(a.astype(BF16), b.astype(BF16), preferred_element_type=F32)


def _bdot_nt(a, b):
    return lax.dot_general(a.astype(BF16), b.astype(BF16), (((1,), (1,)), ((), ())), preferred_element_type=F32)


def _bdot_tn(a, b):
    return lax.dot_general(a.astype(BF16), b.astype(BF16), (((0,), (0,)), ((), ())), preferred_element_type=F32)


def _hdot(a, b):
    return jnp.dot(a, b, preferred_element_type=F32, precision=lax.Precision.HIGHEST)


def _softplus(v):
    return jnp.maximum(v, 0.0) + jnp.log(1.0 + jnp.exp(-jnp.abs(v)))


def _sigmoid(v):
    return 0.5 * jnp.tanh(0.5 * v) + 0.5


def _silu(v):
    return v * _sigmoid(v)


def _dsilu(v):
    s = _sigmoid(v)
    return s * (1.0 + v * (1.0 - s))


GELU_C0 = 0.7978845608028654
GELU_C1 = 0.044715


def _gelu(v):
    return 0.5 * v * (1.0 + jnp.tanh(GELU_C0 * (v + GELU_C1 * v * v * v)))


def _dgelu(v):
    t = jnp.tanh(GELU_C0 * (v + GELU_C1 * v * v * v))
    return 0.5 * (1.0 + t) + 0.5 * v * (1.0 - t * t) * GELU_C0 * (1.0 + 3.0 * GELU_C1 * v * v)


MM_TM, MM_TN = 1024, 1408
MM_TK = {1: 2048, 2: 1536}


def _mm(pairs, mode, out_dtype, name, init=None, rider=None, slots=None):
    a0, b0 = pairs[0]
    if mode == "nn":
        m, n, ks = a0.shape[0], b0.shape[1], [a.shape[1] for a, _ in pairs]
    elif mode == "nt":
        m, n, ks = a0.shape[0], b0.shape[0], [a.shape[1] for a, _ in pairs]
    else:
        m, n, ks = a0.shape[1], b0.shape[1], [a.shape[0] for a, _ in pairs]
    tm = _tile(m, MM_TM, LANES if mode == "tn" else 16)
    tn = _tile(n, MM_TN, LANES) if slots is None else slots[0]
    tk = MM_TK.get(sum(k > 512 for k in ks), 512)
    tks = [_tile(k, tk, 16 if mode == "tn" else LANES) for k in ks]
    steps = [k // t for k, t in zip(ks, tks)]
    offs = [sum(steps[:i]) for i in range(len(pairs))]
    total = sum(steps)
    npairs = len(pairs)

    def kk(i):
        return lambda k: jnp.clip(k - offs[i], 0, steps[i] - 1)

    in_specs, operands = [], []
    for i, (a, b) in enumerate(pairs):
        kf = kk(i)
        if mode == "nn":
            in_specs.append(pl.BlockSpec((tm, tks[i]), lambda mi, ni, k, kf=kf: (mi, kf(k))))
            in_specs.append(pl.BlockSpec((tks[i], tn), lambda mi, ni, k, kf=kf: (kf(k), ni)))
        elif mode == "nt":
            in_specs.append(pl.BlockSpec((tm, tks[i]), lambda mi, ni, k, kf=kf: (mi, kf(k))))
            in_specs.append(pl.BlockSpec((tn, tks[i]), lambda mi, ni, k, kf=kf: (ni, kf(k))))
        else:
            in_specs.append(pl.BlockSpec((tks[i], tm), lambda mi, ni, k, kf=kf: (kf(k), mi)))
            in_specs.append(pl.BlockSpec((tks[i], tn), lambda mi, ni, k, kf=kf: (kf(k), ni)))
        operands += [a, b]
    dot = {"nn": _bdot, "nt": _bdot_nt, "tn": _bdot_tn}[mode]
    if init is not None:
        in_specs.append(pl.BlockSpec((tm, tn), lambda mi, ni, k: (mi, ni)))
        operands.append(init)
    assert out_dtype == F32 or total == 1

    def body(*refs):
        out_ref = refs[-1]
        k = pl.program_id(2)

        def first_step():
            p = dot(refs[0][...], refs[1][...])
            return p if init is None else p + refs[2 * npairs][...]

        if total == 1:
            out_ref[...] = first_step().astype(out_dtype)
            return

        @pl.when(k == 0)
        def _():
            out_ref[...] = first_step()

        for i in range(npairs):
            first = 1 if i == 0 else offs[i]
            if offs[i] + steps[i] > first:
                @pl.when((k >= first) & (k < offs[i] + steps[i]))
                def _(i=i):
                    out_ref[...] += dot(refs[2 * i][...], refs[2 * i + 1][...])

    if slots is not None:
        width, n_slots, first_slot, into = slots
        assert rider is None and init is None and n % width == 0
        aliases = {}
        if into is not None:
            in_specs.append(pl.BlockSpec(memory_space=pl.ANY))
            operands.append(into)
            aliases = {len(operands) - 1: 0}
        return pl.pallas_call(
            body, name=name, grid=(m // tm, n // tn, total), in_specs=in_specs,
            out_specs=pl.BlockSpec((None, tm, tn), lambda mi, ni, k: (ni + first_slot, mi, 0)),
            out_shape=jax.ShapeDtypeStruct((n_slots, m, width), out_dtype), input_output_aliases=aliases,
            compiler_params=_params(("parallel", "parallel", "arbitrary")))(*operands)
    res = _call(body, name, (m // tm, n // tn, total), in_specs, [pl.BlockSpec((tm, tn), lambda mi, ni, k: (mi, ni))],
                [jax.ShapeDtypeStruct((m, n), out_dtype)], [], ("parallel", "parallel", "arbitrary"), operands, rider)
    return res[0] if rider is None else (res[0][0], res[1])


def _rowwise(fn, rows, small, outs, accs, name):
    t = rows[0].shape[0]
    per_row = sum(r.shape[1] * r.dtype.itemsize for r in rows) + sum(c * jnp.dtype(d).itemsize for c, d in outs)
    tm = SUBLANES
    while tm * 2 <= 1024 and t % (tm * 2) == 0 and tm * 2 * per_row <= ROW_BUDGET_BYTES:
        tm *= 2
    if t % tm:
        tm = t
    nrow, nsmall, nout, nacc = len(rows), len(small), len(outs), len(accs)

    def body(*refs):
        rv = [r[...].astype(F32) for r in refs[:nrow]]
        sv = [r[...] for r in refs[nrow:nrow + nsmall]]
        o_refs = refs[nrow + nsmall:nrow + nsmall + nout]
        a_refs = refs[nrow + nsmall + nout:]
        ov, av = fn(rv, sv)
        for r, v in zip(o_refs, ov):
            r[...] = v.astype(r.dtype)
        if nacc:
            @pl.when(pl.program_id(0) == 0)
            def _():
                for r in a_refs:
                    r[...] = jnp.zeros_like(r)
            for r, v in zip(a_refs, av):
                r[...] += v

    in_specs = [pl.BlockSpec((tm, r.shape[1]), lambda i: (i, 0)) for r in rows]
    in_specs += [pl.BlockSpec(s.shape, lambda i: (0, 0)) for s in small]
    out_specs = [pl.BlockSpec((tm, c), lambda i: (i, 0)) for c, _ in outs]
    out_specs += [pl.BlockSpec((1, c), lambda i: (0, 0)) for c in accs]
    out_shape = [jax.ShapeDtypeStruct((t, c), d) for c, d in outs]
    out_shape += [jax.ShapeDtypeStruct((1, c), F32) for c in accs]
    res = pl.pallas_call(
        body, name=name, grid=(t // tm,), in_specs=in_specs, out_specs=out_specs, out_shape=out_shape,
        compiler_params=_params(("arbitrary",)),
    )(*rows, *small)
    return res[:nout], res[nout:]


def _colsum(v):
    return jnp.sum(v, axis=0, keepdims=True)


def _rowmean(v):
    return jnp.mean(v, axis=1, keepdims=True)


def _conv_tiles(x):
    t, c = x.shape
    tm = _tile(t, 256, SUBLANES)
    nt = t // tm
    r8 = tm // SUBLANES
    specs = [
        pl.BlockSpec((SUBLANES, c), lambda i: (jnp.maximum(i * r8 - 1, 0), 0)),
        pl.BlockSpec((tm, c), lambda i: (i, 0)),
        pl.BlockSpec((SUBLANES, c), lambda i: (jnp.minimum((i + 1) * r8, t // SUBLANES - 1), 0)),
    ]
    return tm, nt, specs


def _conv_taps(prev_ref, cur_ref, next_ref, tm, nt, left):
    i = pl.program_id(0)
    p = jnp.where(i > 0, prev_ref[...], 0.0)
    q = jnp.where(i < nt - 1, next_ref[...], 0.0)
    ext = jnp.concatenate([p, cur_ref[...], q], axis=0)
    n = tm + 2 * SUBLANES
    taps = []
    for k in range(CONV_W):
        s = (left - k) % n
        r = pltpu.roll(ext, s, axis=0) if s else ext
        taps.append(r[SUBLANES:SUBLANES + tm])
    return taps


def _conv(x, w, b, left, act, name, out_dtype=F32):
    t, c = x.shape
    tm, nt, specs = _conv_tiles(x)

    def body(prev_ref, cur_ref, next_ref, w_ref, b_ref, o_ref):
        taps = _conv_taps(prev_ref, cur_ref, next_ref, tm, nt, left)
        pre = b_ref[...] + sum(w_ref[k:k + 1, :] * taps[k] for k in range(CONV_W))
        o_ref[...] = (_silu(pre) if act else pre).astype(out_dtype)

    return pl.pallas_call(
        body, name=name, grid=(nt,),
        in_specs=specs + [pl.BlockSpec((CONV_W, c), lambda i: (0, 0)), pl.BlockSpec((1, c), lambda i: (0, 0))],
        out_specs=pl.BlockSpec((tm, c), lambda i: (i, 0)),
        out_shape=jax.ShapeDtypeStruct((t, c), out_dtype), compiler_params=_params(("arbitrary",)),
    )(x, x, x, w, b)


def _conv_bwd(x, dys, combine, w, b, act, name):
    t, c = x.shape
    tm, nt, specs = _conv_tiles(x)
    nd = len(dys)

    def body(*refs):
        prev_ref, cur_ref, next_ref = refs[:3]
        w_ref, b_ref, dpre_ref, dw0, dw1, dw2, dw3, db_ref = refs[3 + nd:]
        dy = combine([r[...] for r in refs[3:3 + nd]])
        taps = _conv_taps(prev_ref, cur_ref, next_ref, tm, nt, CONV_LEFT)
        if act:
            pre = b_ref[...] + sum(w_ref[k:k + 1, :] * taps[k] for k in range(CONV_W))
            dpre = dy * _dsilu(pre)
        else:
            dpre = dy
        dpre_ref[...] = dpre

        @pl.when(pl.program_id(0) == 0)
        def _():
            for r in (dw0, dw1, dw2, dw3, db_ref):
                r[...] = jnp.zeros_like(r)

        for k, r in enumerate((dw0, dw1, dw2, dw3)):
            r[...] += _colsum(dpre * taps[k])
        db_ref[...] += _colsum(dpre)

    row = pl.BlockSpec((1, c), lambda i: (0, 0))
    res = pl.pallas_call(
        body, name=name + "_pre", grid=(nt,),
        in_specs=specs + [pl.BlockSpec((tm, a.shape[1]), lambda i: (i, 0)) for a in dys] + [pl.BlockSpec((CONV_W, c), lambda i: (0, 0)), row],
        out_specs=[pl.BlockSpec((tm, c), lambda i: (i, 0))] + [row] * 5,
        out_shape=[jax.ShapeDtypeStruct((t, c), F32)] + [jax.ShapeDtypeStruct((1, c), F32)] * 5,
        compiler_params=_params(("arbitrary",)),
    )(x, x, x, *dys, w, b)
    dpre, dw, db = res[0], jnp.concatenate(res[1:5], axis=0), res[5]
    dx = _conv(dpre, w[::-1], jnp.zeros_like(b), CONV_W - 1 - CONV_LEFT, False, name + "_dx", BF16)
    return dx, dw, db


def _ssd_dims(xbc):
    t = xbc.shape[0]
    gn = SSD_GROUPS * SSD_STATE
    ws = xbc.shape[1] - 2 * gn
    heads = ws // SSD_HEAD_DIM
    hpg = heads // SSD_GROUPS
    assert hpg % 2 == 0 and ws % gn == 0 and t % SSD_CHUNK == 0
    return t, ws, heads, hpg, heads // 2, t // SSD_CHUNK


def _ssd_chunk_terms(dtr_ref, dtrt_ref, bias_ref, biast_ref, alog_ref, alogt_ref, reverse):
    q = SSD_CHUNK
    a_neg = -jnp.exp(alog_ref[...])
    dt = _softplus(dtr_ref[...] + bias_ref[...])
    dtt = _softplus(dtrt_ref[...] + biast_ref[...])
    a = dt * a_neg
    at = dtt * (-jnp.exp(alogt_ref[...]))
    ii = lax.broadcasted_iota(jnp.int32, (q, q), 0)
    jj = lax.broadcasted_iota(jnp.int32, (q, q), 1)
    mask = (ii <= jj) if reverse else (ii >= jj)
    tri = mask.astype(F32)
    trit = ((jj <= ii) if reverse else (jj >= ii)).astype(F32)
    before = (jj > ii) if reverse else (jj < ii)
    ac = _hdot(tri, a)
    act = _hdot(at, trit)
    tot = _colsum(a)
    return a_neg, dt, a, ac, act, tot, mask, trit, before


def _pair_terms(lo, h1, dt, ac, tot):
    q = SSD_CHUNK

    def colb(v, h):
        return jnp.broadcast_to(v[:, h:h + 1], (v.shape[0], LANES))

    def pairb(v):
        return jnp.where(lo[:v.shape[0]], colb(v, h1), colb(v, h1 + 1))

    dt_p = pairb(dt)
    ac_p = pairb(ac)
    eac_p = jnp.exp(ac_p)
    dte_p = jnp.exp(pairb(jnp.broadcast_to(tot, (q, tot.shape[1]))) - ac_p)
    etot_p = jnp.exp(pairb(tot))
    return dt_p, eac_p, dte_p, etot_p


def _decay_mats(mask, ac, act, cb, h1):
    q = SSD_CHUNK
    out = []
    for h in (h1, h1 + 1):
        diff = jnp.broadcast_to(ac[:, h:h + 1], (q, q)) - jnp.broadcast_to(act[h:h + 1, :], (q, q))
        lmat = jnp.exp(jnp.where(mask, diff, NEG_BIG))
        out.append((lmat, cb * lmat))
    return out


def _ssd_specs(t, ws, heads, npairs, nc, reverse):
    q, gn = SSD_CHUNK, SSD_GROUPS * SSD_STATE
    ci = (lambda k: nc - 1 - k) if reverse else (lambda k: k)
    small = lambda shape: pl.BlockSpec(shape, lambda k: (0,) * len(shape))
    seq = [
        pl.BlockSpec((q, ws), lambda k: (ci(k), 0)),
        pl.BlockSpec((q, gn), lambda k: (ci(k), ws // gn)),
        pl.BlockSpec((q, gn), lambda k: (ci(k), ws // gn + 1)),
        pl.BlockSpec((q, heads), lambda k: (ci(k), 0)),
        pl.BlockSpec((heads, q), lambda k: (0, ci(k))),
        small((1, heads)), small((heads, 1)), small((1, heads)), small((heads, 1)),
    ]
    return ci, small, seq


def _ssd_fwd(xbc, dtr, bias, alog, s0, reverse, name, rider=None):
    t, ws, heads, hpg, npairs, nc = _ssd_dims(xbc)
    q, ns = SSD_CHUNK, SSD_STATE
    ci, small, seq = _ssd_specs(t, ws, heads, npairs, nc, reverse)

    def body(xs_ref, b_ref, c_ref, dtr_ref, dtrt_ref, bias_ref, biast_ref, alog_ref, alogt_ref, s0_ref,
             y_ref, se_ref, s_ref):
        @pl.when(pl.program_id(0) == 0)
        def _():
            s_ref[...] = s0_ref[...]

        _, dt, _, ac, act, tot, mask, _, _ = _ssd_chunk_terms(dtr_ref, dtrt_ref, bias_ref, biast_ref, alog_ref, alogt_ref, reverse)
        lo = lax.broadcasted_iota(jnp.int32, (q, LANES), 1) < SSD_HEAD_DIM
        for g in range(SSD_GROUPS):
            bg = b_ref[:, g * ns:(g + 1) * ns]
            cg = c_ref[:, g * ns:(g + 1) * ns]
            cb = _bdot_nt(cg, bg)
            bgt = bg.T
            for pq in range(hpg // 2):
                pp = g * (hpg // 2) + pq
                h1 = 2 * pp
                cs = slice(pp * LANES, (pp + 1) * LANES)
                (_, m1), (_, m2) = _decay_mats(mask, ac, act, cb, h1)
                dt_p, eac_p, dte_p, etot_p = _pair_terms(lo, h1, dt, ac, tot)
                xp = xs_ref[:, cs] * dt_p
                sp = s_ref[pp]
                se_ref[0, pp] = sp
                y_ref[:, cs] = jnp.where(lo, _bdot(m1, xp), _bdot(m2, xp)) + eac_p * _bdot(cg, sp)
                s_ref[pp] = etot_p * sp + _bdot(bgt, xp * dte_p)

    st = (npairs, ns, LANES)
    return _call(
        body, name, (nc,), seq + [small(st)],
        [pl.BlockSpec((q, ws), lambda k: (ci(k), 0)), pl.BlockSpec((1,) + st, lambda k: (ci(k), 0, 0, 0)), small(st)],
        [jax.ShapeDtypeStruct((t, ws), F32), jax.ShapeDtypeStruct((nc,) + st, F32), jax.ShapeDtypeStruct(st, F32)],
        [], ("arbitrary",), (xbc, xbc, xbc, dtr, dtr.T, bias, bias.T, alog, alog.T, s0), rider)


def _ssd_bwd(xbc, dtr, bias, alog, dy, s_enter, lam0, reverse, name, rider=None):
    t, ws, heads, hpg, npairs, nc = _ssd_dims(xbc)
    q, ns, gn = SSD_CHUNK, SSD_STATE, SSD_GROUPS * SSD_STATE
    ci, small, seq = _ssd_specs(t, ws, heads, npairs, nc, not reverse)
    st = (npairs, ns, LANES)

    e_heads = jnp.asarray(np.arange(ws)[:, None] // SSD_HEAD_DIM == np.arange(heads)[None, :], BF16)

    def body(xs_ref, b_ref, c_ref, dtr_ref, dtrt_ref, bias_ref, biast_ref, alog_ref, alogt_ref, dy_ref, se_ref, lam0_ref, e_ref,
             dxs_ref, db_ref, dc_ref, ddtr_ref, da_ref, dbias_ref, lam_ref, st_s, ts_s, ss_s, xx_s):
        @pl.when(pl.program_id(0) == 0)
        def _():
            lam_ref[...] = lam0_ref[...]
            da_ref[...] = jnp.zeros_like(da_ref)
            dbias_ref[...] = jnp.zeros_like(dbias_ref)

        a_neg, dt, _, ac, act, tot, mask, trit, before = _ssd_chunk_terms(dtr_ref, dtrt_ref, bias_ref, biast_ref, alog_ref, alogt_ref, reverse)
        lo = lax.broadcasted_iota(jnp.int32, (q, LANES), 1) < SSD_HEAD_DIM
        head_col = lax.broadcasted_iota(jnp.int32, (heads, 1), 0)
        ii = lax.broadcasted_iota(jnp.int32, (q, q), 0)
        jj = lax.broadcasted_iota(jnp.int32, (q, q), 1)
        before_t = ((ii > jj) if reverse else (ii < jj)).astype(F32)

        def per_head(v):
            hi = v.astype(BF16)
            lo = (v - hi.astype(F32)).astype(BF16)
            sel = e_ref[...]
            return jnp.dot(hi, sel, preferred_element_type=F32) + jnp.dot(lo, sel, preferred_element_type=F32)

        da_tt = jnp.zeros((heads, q), F32)
        for g in range(SSD_GROUPS):
            gs = slice(g * ns, (g + 1) * ns)
            bg, cg = b_ref[:, gs], c_ref[:, gs]
            cb = _bdot_nt(cg, bg)
            cgt = cg.T
            dcb = jnp.zeros((q, q), F32)
            dbg = jnp.zeros((q, ns), F32)
            dcg = jnp.zeros((q, ns), F32)
            for pq in range(hpg // 2):
                pp = g * (hpg // 2) + pq
                h1 = 2 * pp
                cs = slice(pp * LANES, (pp + 1) * LANES)
                (l1, m1), (l2, m2) = _decay_mats(mask, ac, act, cb, h1)
                dt_p, eac_p, dte_p, etot_p = _pair_terms(lo, h1, dt, ac, tot)
                xsp = xs_ref[:, cs]
                xp = xsp * dt_p
                dyp = dy_ref[:, cs]
                lam = lam_ref[pp]
                sp = se_ref[0, pp]
                dx_state = dte_p * _bdot(bg, lam)
                dxdt = jnp.where(lo, _bdot(m1.T, dyp), _bdot(m2.T, dyp)) + dx_state
                dyx1 = _bdot_nt(jnp.where(lo, dyp, 0.0), xp)
                dyx2 = _bdot_nt(jnp.where(lo, 0.0, dyp), xp)
                dcb = dcb + l1 * dyx1 + l2 * dyx2
                for h, pair in ((h1, m1 * dyx1), (h1 + 1, m2 * dyx2)):
                    cross = _colsum(jnp.where(mask, _bdot(pair, before_t), 0.0))
                    da_tt = da_tt + (head_col == h).astype(F32) * cross
                st_s[:, cs] = dyp * (eac_p * _bdot(cg, sp))
                ts_s[:, cs] = dx_state * xp
                ss_s[:, cs] = jnp.broadcast_to(_colsum(lam * sp) * etot_p, (SUBLANES, LANES))
                xx_s[:, cs] = dxdt * xsp
                edy = eac_p * dyp
                xe = dte_p * xp
                dcg = dcg + _bdot_nt(edy, sp)
                dbg = dbg + _bdot_nt(xe, lam)
                lam_ref[pp] = etot_p * lam + _bdot(cgt, edy)
                dxs_ref[:, cs] = dxdt * dt_p
            db_ref[:, gs] = dbg + _bdot(dcb.T, cg)
            dc_ref[:, gs] = dcg + _bdot(dcb, bg)
        da = da_tt.T + _hdot(trit, per_head(st_s[...])) + _hdot(before.astype(F32), per_head(ts_s[...])) + per_head(ss_s[...])[0:1]
        ddtr = (da * a_neg + per_head(xx_s[...])) * _sigmoid(dtr_ref[...] + bias_ref[...])
        ddtr_ref[...] = ddtr
        da_ref[...] += _colsum(da * dt)
        dbias_ref[...] += _colsum(ddtr)

    row = lambda c: pl.BlockSpec((q, c), lambda k: (ci(k), 0))
    return _call(
        body, name, (nc,),
        seq + [row(ws), pl.BlockSpec((1,) + st, lambda k: (ci(k), 0, 0, 0)), small(st), small((ws, heads))],
        [row(ws), row(gn), row(gn), row(heads), small((1, heads)), small((1, heads)), small(st)],
        [jax.ShapeDtypeStruct((t, ws), F32), jax.ShapeDtypeStruct((t, gn), F32), jax.ShapeDtypeStruct((t, gn), F32),
         jax.ShapeDtypeStruct((t, heads), F32), jax.ShapeDtypeStruct((1, heads), F32),
         jax.ShapeDtypeStruct((1, heads), F32), jax.ShapeDtypeStruct(st, F32)],
        [pltpu.VMEM((q, ws), F32), pltpu.VMEM((q, ws), F32), pltpu.VMEM((SUBLANES, ws), F32), pltpu.VMEM((q, ws), F32)],
        ("arbitrary",), (xbc, xbc, xbc, dtr, dtr.T, bias, bias.T, alog, alog.T, dy, s_enter, lam0, e_heads), rider)


def _tile_scan(a_ref, u_ref, out_ref, carry_ref, ntiles, reverse):
    w = a_ref.shape[1]
    row = lax.broadcasted_iota(jnp.int32, (SUBLANES, w), 0)

    def step(j, _):
        r0 = pl.multiple_of((ntiles - 1 - j if reverse else j) * SUBLANES, SUBLANES)
        a = a_ref[pl.ds(r0, SUBLANES), :]
        u = u_ref[pl.ds(r0, SUBLANES), :]
        for d in (1, 2, 4):
            keep = (row < SUBLANES - d) if reverse else (row >= d)
            s = SUBLANES - d if reverse else d
            a_sh = jnp.where(keep, pltpu.roll(a, s, axis=0), 1.0)
            u_sh = jnp.where(keep, pltpu.roll(u, s, axis=0), 0.0)
            u = a * u_sh + u
            a = a * a_sh
        h = a * carry_ref[...] + u
        out_ref[pl.ds(r0, SUBLANES), :] = h
        last = h[0:1, :] if reverse else h[SUBLANES - 1:SUBLANES, :]
        carry_ref[...] = jnp.broadcast_to(last, (SUBLANES, w))
        return 0

    lax.fori_loop(0, ntiles, step, 0)


def _lru_gates(xh, wa, wx, ba, bx, lam):
    r = _sigmoid(_bdot(xh, wa) + ba)
    i = _sigmoid(_bdot(xh, wx) + bx)
    sp = _softplus(-lam)
    a = jnp.exp(-LRU_C * r * sp)
    return r, i, sp, a


def _lru_fwd(xc, wa, wx, ba, bx, lam, h0, addend, reverse, name, rider=None):
    t, w = xc.shape
    tb = LRU_BLOCK
    nb, nh = t // tb, w // LRU_HEAD_DIM
    bi = (lambda k: nb - 1 - k) if reverse else (lambda k: k)
    has_add = addend is not None

    def body(*refs):
        xc_ref, wa_ref, wx_ref, ba_ref, bx_ref, lam_ref, h0_ref = refs[:7]
        add_ref = refs[7] if has_add else None
        h_ref, hl_ref = refs[7 + has_add], refs[8 + has_add]
        sum_ref = refs[9 + has_add] if has_add else None
        a_s, u_s, carry = refs[-3:]

        @pl.when(pl.program_id(0) == 0)
        def _():
            carry[...] = jnp.broadcast_to(h0_ref[...], carry.shape)

        for hd in range(nh):
            hs = slice(hd * LRU_HEAD_DIM, (hd + 1) * LRU_HEAD_DIM)
            xh = xc_ref[:, hs]
            _, i, _, a = _lru_gates(xh, wa_ref[hd], wx_ref[hd], ba_ref[:, hs], bx_ref[:, hs], lam_ref[:, hs])
            a_s[:, hs] = a
            u_s[:, hs] = jnp.sqrt(1.0 - a * a) * (i * xh)
        _tile_scan(a_s, u_s, h_ref, carry, tb // SUBLANES, reverse)
        hl_ref[...] = carry[0:1, :]
        if has_add:
            sum_ref[...] = h_ref[...] + add_ref[...]

    blk = pl.BlockSpec((tb, w), lambda k: (bi(k), 0))
    vec = pl.BlockSpec((1, w), lambda k: (0, 0))
    wsp = pl.BlockSpec((nh, LRU_HEAD_DIM, LRU_HEAD_DIM), lambda k: (0, 0, 0))
    out_specs = [blk, vec] + ([blk] if has_add else [])
    out_shape = [jax.ShapeDtypeStruct((t, w), F32), jax.ShapeDtypeStruct((1, w), F32)] + ([jax.ShapeDtypeStruct((t, w), F32)] if has_add else [])
    return _call(
        body, name, (nb,), [blk, wsp, wsp, vec, vec, vec, vec] + ([blk] if has_add else []), out_specs, out_shape,
        [pltpu.VMEM((tb, w), F32), pltpu.VMEM((tb, w), F32), pltpu.VMEM((SUBLANES, w), F32)], ("arbitrary",),
        (xc, wa, wx, ba, bx, lam, h0, *([addend] if has_add else [])), rider)


def _lru_bwd(xc, wa, wx, ba, bx, lam, h0, h, dh, dfin, addend, reverse, name, rider=None):
    t, w = xc.shape
    tb = LRU_BLOCK
    nb, nh = t // tb, w // LRU_HEAD_DIM
    r8 = tb // SUBLANES
    bi = (lambda k: k) if reverse else (lambda k: nb - 1 - k)
    if reverse:
        halo = lambda k: (jnp.minimum((bi(k) + 1) * r8, t // SUBLANES - 1), 0)
    else:
        halo = lambda k: (jnp.maximum(bi(k) * r8 - 1, 0), 0)
    has_add = addend is not None
    n_ext = tb + SUBLANES

    def body(*refs):
        xc_ref, wa_ref, wx_ref, ba_ref, bx_ref, lam_ref, h0_ref, h_ref, halo_ref, dh_ref, dfin_ref = refs[:11]
        add_ref = refs[11] if has_add else None
        dxc_ref, dwa_ref, dwx_ref, dba_ref, dbx_ref, dsp_ref, dh0_ref = refs[11 + has_add:18 + has_add]
        a_s, v_s, mu_s, r_s, i_s, carry = refs[-6:]
        k = pl.program_id(0)

        @pl.when(k == 0)
        def _():
            carry[...] = jnp.broadcast_to(dfin_ref[...], carry.shape)
            for r in (dwa_ref, dwx_ref, dba_ref, dbx_ref, dsp_ref):
                r[...] = jnp.zeros_like(r)

        for hd in range(nh):
            hs = slice(hd * LRU_HEAD_DIM, (hd + 1) * LRU_HEAD_DIM)
            r, i, _, a = _lru_gates(xc_ref[:, hs], wa_ref[hd], wx_ref[hd], ba_ref[:, hs], bx_ref[:, hs], lam_ref[:, hs])
            a_s[:, hs] = a
            r_s[:, hs] = r
            i_s[:, hs] = i
            v_s[:, hs] = a * dh_ref[:, hs]
        mu_in = carry[...]
        _tile_scan(a_s, v_s, mu_s, carry, r8, not reverse)
        dh0_ref[...] = carry[0:1, :]
        first = (bi(k) == nb - 1) if reverse else (bi(k) == 0)
        edge = jnp.where(first, jnp.broadcast_to(h0_ref[...], (SUBLANES, w)), halo_ref[...])
        for hd in range(nh):
            hs = slice(hd * LRU_HEAD_DIM, (hd + 1) * LRU_HEAD_DIM)
            if reverse:
                mu_sh = pltpu.roll(jnp.concatenate([mu_in[:, hs], mu_s[:, hs]], axis=0), 1, axis=0)[SUBLANES:]
                h_sh = pltpu.roll(jnp.concatenate([h_ref[:, hs], edge[:, hs]], axis=0), n_ext - 1, axis=0)[:tb]
            else:
                mu_sh = pltpu.roll(jnp.concatenate([mu_s[:, hs], mu_in[:, hs]], axis=0), n_ext - 1, axis=0)[:tb]
                h_sh = pltpu.roll(jnp.concatenate([edge[:, hs], h_ref[:, hs]], axis=0), 1, axis=0)[SUBLANES:]
            xh = xc_ref[:, hs]
            r, i, a = r_s[:, hs], i_s[:, hs], a_s[:, hs]
            sp = _softplus(-lam_ref[:, hs])
            om = 1.0 - a * a
            rs = lax.rsqrt(om)
            lam_t = dh_ref[:, hs] + mu_sh
            dla = lam_t * h_sh * a - lam_t * (i * xh) * (a * a) * rs
            dix = lam_t * (om * rs)
            dpr = dla * (-LRU_C * sp) * r * (1.0 - r)
            dpi = dix * xh * i * (1.0 - i)
            dsp_ref[:, hs] += _colsum(dla * (-LRU_C * r))
            dba_ref[:, hs] += _colsum(dpr)
            dbx_ref[:, hs] += _colsum(dpi)
            dx = dix * i + _bdot_nt(dpr, wa_ref[hd]) + _bdot_nt(dpi, wx_ref[hd])
            xt = xh.T
            dwa_ref[hd] += _bdot(xt, dpr)
            dwx_ref[hd] += _bdot(xt, dpi)
            dxc_ref[:, hs] = dx + add_ref[:, hs] if has_add else dx

    blk = pl.BlockSpec((tb, w), lambda k: (bi(k), 0))
    vec = pl.BlockSpec((1, w), lambda k: (0, 0))
    wsp = pl.BlockSpec((nh, LRU_HEAD_DIM, LRU_HEAD_DIM), lambda k: (0, 0, 0))
    wshape = jax.ShapeDtypeStruct((nh, LRU_HEAD_DIM, LRU_HEAD_DIM), F32)
    vshape = jax.ShapeDtypeStruct((1, w), F32)
    return _call(
        body, name, (nb,),
        [blk, wsp, wsp, vec, vec, vec, vec, blk, pl.BlockSpec((SUBLANES, w), halo), blk, vec] + ([blk] if has_add else []),
        [blk, wsp, wsp, vec, vec, vec, vec],
        [jax.ShapeDtypeStruct((t, w), F32), wshape, wshape, vshape, vshape, vshape, vshape],
        [pltpu.VMEM((tb, w), F32)] * 5 + [pltpu.VMEM((SUBLANES, w), F32)], ("arbitrary",),
        (xc, wa, wx, ba, bx, lam, h0, h, h, dh, dfin, *([addend] if has_add else [])), rider)


def _coords():
    return lax.axis_index("x"), lax.axis_index("y"), lax.axis_index("c")


HALF_ROWS_ALIGN = 32


def _all_gather(shards, name):
    n = len(shards)
    start, wait, sems = _gather_rounds(shards)

    def body(*refs):
        for rnd in range(3):
            start(rnd, refs[:n], refs[n:2 * n], refs[2 * n:])
            wait(rnd, refs[:n], refs[n:2 * n], refs[2 * n:])

    any_spec = pl.BlockSpec(memory_space=pl.ANY)
    return pl.pallas_call(
        body, name=name, in_specs=[any_spec] * n, out_specs=[any_spec] * n,
        out_shape=[jax.ShapeDtypeStruct((N_DEV,) + s.shape, s.dtype) for s in shards], scratch_shapes=sems,
    )(*shards)


def _gather_riders(shards):
    n = len(shards)
    start, wait, sems = _gather_rounds(shards)
    out_shape = [jax.ShapeDtypeStruct((N_DEV,) + s.shape, s.dtype) for s in shards]

    def make(rnd, bufs=()):
        return _Rider(list(shards) + list(bufs), out_shape, {n + i: i for i in range(len(bufs))}, sems,
                      lambda r_in, r_out, s: start(rnd, r_in[:n], r_out, s), lambda r_in, r_out, s: wait(rnd, r_in[:n], r_out, s))
    return make


def _gather_rounds(shards):
    n = len(shards)
    split = [s.ndim >= 2 and s.shape[0] % HALF_ROWS_ALIGN == 0 for s in shards]
    n_copies = 10

    def plan(ins, outs, sem_refs):
        send_sems, recv_sems, local_sems = sem_refs
        x, y, c = _coords()
        slot = lambda p: 4 * p[0] + 2 * p[1] + p[2]
        me, xp, yp, dg = (x, y, c), (1 - x, y, c), (x, 1 - y, c), (1 - x, 1 - y, c)
        other_core = lambda p: (p[0], p[1], 1 - p[2])

        def rows(ref, i, half):
            r = shards[i].shape[0]
            if half == "all" or not split[i]:
                return ref
            return ref.at[pl.ds(0, r // 2)] if half == "a" else ref.at[pl.ds(r // 2, r // 2)]

        def copy(i, j, holder, half, to, own=False):
            dst = rows(outs[i].at[slot(holder)], i, half)
            src = rows(ins[i], i, half) if own else dst
            return pltpu.make_async_remote_copy(src_ref=src, dst_ref=dst, send_sem=send_sems.at[i, j],
                                                recv_sem=recv_sems.at[i, j], device_id=to, device_id_type=MESH)

        rounds = [
            ([(0, me, "a", xp, True), (1, me, "b", yp, True)],
             [(0, xp, "a"), (1, yp, "b")]),
            ([(2, me, "a", yp, True), (3, xp, "a", yp, False), (4, me, "b", xp, True), (5, yp, "b", xp, False)],
             [(2, yp, "a"), (3, dg, "a"), (4, xp, "b"), (5, dg, "b")]),
            ([(6, me, "all", other_core(me), True)] + [(7 + q, p, "all", other_core(me), False) for q, p in enumerate((xp, yp, dg))],
             [(6 + q, other_core(p), "all") for q, p in enumerate((me, xp, yp, dg))]),
        ]
        local = lambda: [pltpu.make_async_copy(ins[i], outs[i].at[slot(me)], local_sems.at[i]) for i in range(n)]
        travels = lambda i, half: half != "b" or split[i]
        return rounds, copy, local, travels, me

    def start(rnd, ins, outs, sem_refs):
        rounds, copy, local, travels, _ = plan(ins, outs, sem_refs)
        if rnd == 0:
            for cp in local():
                cp.start()
        for j, holder, half, to, own in rounds[rnd][0]:
            for i in range(n):
                if travels(i, half):
                    copy(i, j, holder, half, to, own).start()

    def wait(rnd, ins, outs, sem_refs):
        rounds, copy, local, travels, me = plan(ins, outs, sem_refs)
        for j, holder, half in rounds[rnd][1]:
            for i in range(n):
                if travels(i, half):
                    copy(i, j, holder, half, me).wait_recv()
        for j, holder, half, to, own in rounds[rnd][0]:
            for i in range(n):
                if travels(i, half):
                    copy(i, j, holder, half, to, own).wait_send()
        if rnd == 0:
            for cp in local():
                cp.wait()

    sems = [pltpu.SemaphoreType.DMA((n, n_copies)), pltpu.SemaphoreType.DMA((n, n_copies)), pltpu.SemaphoreType.DMA((n,))]
    return start, wait, sems


def _swap_rider(bufs, axes):
    n = len(bufs)

    def copies(ins, outs, sem_refs):
        x, y, c = _coords()
        to = {"x": (1 - x, y, c), "y": (x, 1 - y, c), "c": (x, y, 1 - c)}
        return [pltpu.make_async_remote_copy(src_ref=ins[i], dst_ref=outs[i], send_sem=sem_refs[0].at[i], recv_sem=sem_refs[1].at[i],
                                             device_id=to[axes[i]], device_id_type=MESH) for i in range(n)]

    def start(ins, outs, sem_refs):
        for cp in copies(ins, outs, sem_refs):
            cp.start()

    def wait(ins, outs, sem_refs):
        for cp in copies(ins, outs, sem_refs):
            cp.wait()

    return _Rider(list(bufs), [jax.ShapeDtypeStruct(b.shape, b.dtype) for b in bufs], {},
                  [pltpu.SemaphoreType.DMA((n,)), pltpu.SemaphoreType.DMA((n,))], start, wait)


def _swap(bufs, axes, name):
    n = len(bufs)
    rider = _swap_rider(bufs, axes)

    def body(*refs):
        rider.start(refs[:n], refs[n:2 * n], refs[2 * n:])
        rider.wait(refs[:n], refs[n:2 * n], refs[2 * n:])

    any_spec = pl.BlockSpec(memory_space=pl.ANY)
    return pl.pallas_call(body, name=name, in_specs=[any_spec] * n, out_specs=[any_spec] * n, out_shape=rider.out_shape,
                          scratch_shapes=rider.sems)(*bufs)


def _as2d(a):
    return a.reshape(-1, a.shape[-1])


def _add_f32(a, b, name):
    (out,), _ = _rowwise(lambda r, s: ([r[0].astype(F32) + r[1].astype(F32)], []), [_as2d(a), _as2d(b)], [],
                         [(a.shape[-1], F32)], [], name)
    return out.reshape(a.shape)


class _ReduceScatter:
    def __init__(self, parts, name):
        self.n, self.name, self.rnd = len(parts), name, 0
        self.cur = [p.reshape((2, 2, 2) + p.shape[1:]) for p in parts]

    def stage(self):
        x, y, c = _coords()
        n = self.n
        if self.rnd == 0:
            bufs, idxs, dims, axes = self.cur, [c] * n, [2] * n, ["c"] * n
        elif self.rnd == 1:
            bufs = [p[:, :, :p.shape[2] // 2] for p in self.cur] + [p[:, :, p.shape[2] // 2:] for p in self.cur]
            idxs, dims, axes = [y] * n + [x] * n, [1] * n + [0] * n, ["y"] * n + ["x"] * n
        else:
            bufs, idxs, dims, axes = self.cur, [x] * n + [y] * n, [0] * (2 * n), ["x"] * n + ["y"] * n
        pick = lambda p, idx, dim: lax.dynamic_index_in_dim(p, idx, dim, keepdims=False)
        self.keep = [pick(p, i_, d_) for p, i_, d_ in zip(bufs, idxs, dims)]
        return [pick(p, 1 - i_, d_).astype(BF16) for p, i_, d_ in zip(bufs, idxs, dims)], axes

    def absorb(self, got):
        self.cur = [_add_f32(k, g, f"{self.name}_add{self.rnd}_{i}") for i, (k, g) in enumerate(zip(self.keep, got))]
        self.rnd += 1

    def result(self):
        return [jnp.concatenate([self.cur[i], self.cur[self.n + i]], axis=0) for i in range(self.n)]


def _reduce_scatter(parts, name):
    rs = _ReduceScatter(parts, name)
    for rnd in range(3):
        send, axes = rs.stage()
        rs.absorb(_swap(send, axes, f"{name}_swap{rnd}"))
    return rs.result()


def _pack(arrs):
    flat = jnp.concatenate([a.reshape(-1).astype(F32) for a in arrs])
    rows = -(-flat.shape[0] // LANES)
    rows = -(-rows // 512) * 512
    return jnp.pad(flat, (0, rows * LANES - flat.shape[0])).reshape(rows, LANES)


def _unpack(buf, like):
    flat, out, off = buf.reshape(-1), [], 0
    for a in like:
        out.append(flat[off:off + a.size].reshape(a.shape))
        off += a.size
    return out


def _all_reduce(arrs, name, narrow=()):
    wide = [a for i, a in enumerate(arrs) if i not in narrow]
    thin = [a for i, a in enumerate(arrs) if i in narrow]
    packs = [_pack(wide)] + ([_pack(thin).astype(BF16)] if thin else [])
    gathered = _all_gather(packs, name + "_gather")
    sums = []
    for j, g in enumerate(gathered):
        (tot,), _ = _rowwise(lambda r, s: ([functools.reduce(lambda u, v: u + v, r)], []), [g[i] for i in range(N_DEV)], [],
                             [(LANES, F32)], [], f"{name}_sum{j}")
        sums.append(tot)
    out_wide, out_thin = iter(_unpack(sums[0], wide)), iter(_unpack(sums[1], thin) if thin else [])
    return [next(out_thin) if i in narrow else next(out_wide) for i in range(len(arrs))]


def _adamw(w, g, m, v, name):
    def fn(r, s):
        w_, g_, m_, v_ = r
        m_ = ADAM_B1 * m_ + (1.0 - ADAM_B1) * g_
        v_ = ADAM_B2 * v_ + (1.0 - ADAM_B2) * jnp.square(g_)
        m_hat = m_ / (1.0 - ADAM_B1 ** ADAM_STEP)
        v_hat = v_ / (1.0 - ADAM_B2 ** ADAM_STEP)
        return [-ADAM_LR * (m_hat / (jnp.sqrt(v_hat) + ADAM_EPS) + ADAM_WD * w_), m_, v_], []

    c = w.shape[-1]
    (d, nm, nv), _ = _rowwise(fn, [_as2d(a) for a in (w, g, m, v)], [], [(c, F32)] * 3, [], name)
    return d.reshape(w.shape), nm.reshape(w.shape), nv.reshape(w.shape)


def _norm_mod(x, gain, sc, sh, name):
    def fn(r, s):
        xv = r[0]
        xn = xv * lax.rsqrt(_rowmean(xv * xv) + EPS)
        return [xn * s[0] * (1.0 + s[1]) + s[2]], []
    (h,), _ = _rowwise(fn, [x], [gain, sc, sh], [(x.shape[1], BF16)], [], name)
    return h


def _norm_mod_bwd(x, dh, dres, gain, sc, name):
    d = x.shape[1]

    def fn(r, s):
        xv, dhv = r[0], r[1]
        rs = lax.rsqrt(_rowmean(xv * xv) + EPS)
        xn = xv * rs
        dxn = dhv * (1.0 + s[1]) * s[0]
        dx = rs * (dxn - xn * _rowmean(dxn * xn))
        if dres is not None:
            dx = dx + r[2]
        return [dx], [_colsum(dhv), _colsum(dhv * xn * s[0]), _colsum(dhv * (1.0 + s[1]) * xn)]
    (dx,), accs = _rowwise(fn, [x, dh] + ([dres] if dres is not None else []), [gain, sc], [(d, F32)], [d, d, d], name)
    return dx, accs


def _ada_rows(mod_row, d):
    return [mod_row[:, i * d:(i + 1) * d] for i in range(N_MOD)]


def _to_col_major(t):
    n, c = t.shape
    return t.reshape(n // GRID_W, GRID_W, c).transpose(1, 0, 2).reshape(n, c)


def _to_row_major(t):
    n, c = t.shape
    return t.reshape(GRID_W, n // GRID_W, c).transpose(1, 0, 2).reshape(n, c)


def kernel(x, c, ctx, c_ctx, w_ada, b_ada, norm_mix, norm_ffn, w_in, ssd_conv_w, ssd_conv_b, ssd_dt_bias, ssd_a_log, ssd_d, ssd_norm, w_out_ssd, lru_conv_w, lru_conv_b, lru_w_a, lru_b_a, lru_w_x, lru_b_x, lru_lambda, w_out_lru, w_gate, b_gate, w_o, ffn_w13, ffn_w2, final_norm, loss_target, m_c_ctx, m_w_ada, m_b_ada, m_norm_mix, m_norm_ffn, m_w_in, m_ssd_conv_w, m_ssd_conv_b, m_ssd_dt_bias, m_ssd_a_log, m_ssd_d, m_ssd_norm, m_w_out_ssd, m_lru_conv_w, m_lru_conv_b, m_lru_w_a, m_lru_b_a, m_lru_w_x, m_lru_b_x, m_lru_lambda, m_w_out_lru, m_w_gate, m_b_gate, m_w_o, m_ffn_w13, m_ffn_w2, m_final_norm, v_c_ctx, v_w_ada, v_b_ada, v_norm_mix, v_norm_ffn, v_w_in, v_ssd_conv_w, v_ssd_conv_b, v_ssd_dt_bias, v_ssd_a_log, v_ssd_d, v_ssd_norm, v_w_out_ssd, v_lru_conv_w, v_lru_conv_b, v_lru_w_a, v_lru_b_a, v_lru_w_x, v_lru_b_x, v_lru_lambda, v_w_out_lru, v_w_gate, v_b_gate, v_w_o, v_ffn_w13, v_ffn_w2, v_final_norm):
    names = ["c_ctx", "w_ada", "b_ada", "norm_mix", "norm_ffn", "w_in", "ssd_conv_w", "ssd_conv_b", "ssd_dt_bias", "ssd_a_log",
             "ssd_d", "ssd_norm", "w_out_ssd", "lru_conv_w", "lru_conv_b", "lru_w_a", "lru_b_a", "lru_w_x", "lru_b_x", "lru_lambda",
             "w_out_lru", "w_gate", "b_gate", "w_o", "ffn_w13", "ffn_w2", "final_norm"]
    wts = dict(zip(names, (c_ctx, w_ada, b_ada, norm_mix, norm_ffn, w_in, ssd_conv_w, ssd_conv_b, ssd_dt_bias, ssd_a_log, ssd_d, ssd_norm, w_out_ssd, lru_conv_w, lru_conv_b, lru_w_a, lru_b_a, lru_w_x, lru_b_x, lru_lambda, w_out_lru, w_gate, b_gate, w_o, ffn_w13, ffn_w2, final_norm)))
    mom1 = dict(zip(names, (m_c_ctx, m_w_ada, m_b_ada, m_norm_mix, m_norm_ffn, m_w_in, m_ssd_conv_w, m_ssd_conv_b, m_ssd_dt_bias, m_ssd_a_log, m_ssd_d, m_ssd_norm, m_w_out_ssd, m_lru_conv_w, m_lru_conv_b, m_lru_w_a, m_lru_b_a, m_lru_w_x, m_lru_b_x, m_lru_lambda, m_w_out_lru, m_w_gate, m_b_gate, m_w_o, m_ffn_w13, m_ffn_w2, m_final_norm)))
    mom2 = dict(zip(names, (v_c_ctx, v_w_ada, v_b_ada, v_norm_mix, v_norm_ffn, v_w_in, v_ssd_conv_w, v_ssd_conv_b, v_ssd_dt_bias, v_ssd_a_log, v_ssd_d, v_ssd_norm, v_w_out_ssd, v_lru_conv_w, v_lru_conv_b, v_lru_w_a, v_lru_b_a, v_lru_w_x, v_lru_b_x, v_lru_lambda, v_w_out_lru, v_w_gate, v_b_gate, v_w_o, v_ffn_w13, v_ffn_w2, v_final_norm)))

    xi, yi, ci_ = _coords()
    me = 4 * xi + 2 * yi + ci_
    x2, tgt, ctx2 = x[0], loss_target[0], ctx[0]
    t, d = x2.shape
    tc = ctx2.shape[0]
    ws = N_DEV * w_out_ssd.shape[1]
    wl = N_DEV * w_out_lru.shape[1]
    heads = ws // SSD_HEAD_DIM
    gn = SSD_GROUPS * SSD_STATE
    nh = wl // LRU_HEAD_DIM
    ff = N_DEV * ffn_w2.shape[1]
    ada_cols = w_ada.shape[2]

    smalls = [c, ssd_conv_w[0], lru_conv_w[0], lru_b_a[0], lru_b_x[0], lru_lambda[0]]
    got = _all_gather(smalls + [w_in[0].astype(BF16), w_gate[0].astype(BF16)], "gather_weights")
    late_gather = _gather_riders([w.astype(BF16) for w in (w_out_ssd[0], w_out_lru[0], w_o[0], ffn_w13[0], ffn_w2[0])])
    c_all = got[0].reshape(N_DEV, d)
    colcat = lambda g: jnp.moveaxis(g, 0, -2).reshape(g.shape[1:-1] + (N_DEV * g.shape[-1],))
    conv_w_s, conv_w_l, lru_ba, lru_bx, lru_lam = (colcat(g) for g in got[1:6])
    wg_in, wg_gate = got[6:]
    w_in_f = colcat(wg_in)
    o_z, o_xbc, o_dt, o_xr, o_yr = 0, ws, 2 * ws + 2 * gn, 2 * ws + 2 * gn + 2 * heads, 2 * ws + 2 * gn + 2 * heads + wl
    w_z, w_xbc = w_in_f[:, o_z:o_xbc], w_in_f[:, o_xbc:o_dt]
    w_dt = jnp.pad(w_in_f[:, o_dt:o_xr], ((0, 0), (0, LANES - 2 * heads)))
    w_xr, w_yr = w_in_f[:, o_xr:o_yr], w_in_f[:, o_yr:]
    w_gf = colcat(wg_gate)
    w_gs, w_gr = w_gf[:, :d], w_gf[:, d:]

    cond = jnp.concatenate([c_all, c_ctx[None, :], jnp.zeros((16 - N_DEV - 1, d), F32)], axis=0)
    cond_act = _silu(cond)
    mod_l = _mm([(cond_act, w_ada[0])], "nn", F32, "ada_fwd")
    (mod_g,) = _all_gather([mod_l], "gather_mod")
    mod_full = colcat(mod_g) + b_ada
    sh_m, sc_m, g_m, sh_f, sc_f, g_f = _ada_rows(lax.dynamic_slice_in_dim(mod_full, me, 1, 0), d)
    csh_m, csc_m = _ada_rows(mod_full[N_DEV:N_DEV + 1], d)[:2]

    bias_s, alog_s = ssd_dt_bias[0], ssd_a_log[0]
    d_full = jnp.repeat(ssd_d[0], SSD_HEAD_DIM)[None, :]
    zeros_state = jnp.zeros((heads // 2, SSD_STATE, LANES), F32)
    zeros_l = jnp.zeros((1, wl), F32)
    wa, wx = lru_w_a[0], lru_w_x[0]

    def mixer_inputs(hh, tag, full):
        xbc = _mm([(hh, w_xbc)], "nn", F32, f"{tag}_proj_xbc")
        dtr = _mm([(hh, w_dt)], "nn", F32, f"{tag}_proj_dt")[:, :2 * heads]
        xr = _mm([(hh, w_xr)], "nn", F32, f"{tag}_proj_xr")
        cx = _conv(xbc, conv_w_s, ssd_conv_b, CONV_LEFT, True, f"{tag}_ssd_conv")
        xr_seq = _to_col_major(xr) if full else xr
        xc = _conv(xr_seq, conv_w_l, lru_conv_b, CONV_LEFT, False, f"{tag}_lru_conv")
        return xbc, dtr, xr_seq, cx, xc

    def lru_args(dirn):
        return wa[dirn], wx[dirn], lru_ba[dirn:dirn + 1], lru_bx[dirn:dirn + 1], lru_lam[dirn:dirn + 1]

    hc = _norm_mod(ctx2, norm_mix, csc_m, csh_m, "ctx_norm")
    c_xbc, c_dtr, c_xr_seq, c_cx, c_xc = mixer_inputs(hc, "ctx", False)
    c_ssd = []
    for dirn in (0, 1):
        _, se, sf = _ssd_fwd(c_cx, c_dtr[:, dirn * heads:(dirn + 1) * heads], bias_s[dirn:dirn + 1], alog_s[dirn:dirn + 1],
                             zeros_state, bool(dirn), f"ctx_ssd_fwd{dirn}")
        c_ssd.append((se, sf))
    c_lru = []
    for dirn in (0, 1):
        res = _lru_fwd(c_xc, *lru_args(dirn), zeros_l, None, bool(dirn), f"ctx_lru_fwd{dirn}")
        c_lru.append((res[0], res[1]))

    h = _norm_mod(x2, norm_mix, sc_m, sh_m, "lat_norm")
    z = _mm([(h, w_z)], "nn", BF16, "lat_proj_z")
    yr = _mm([(h, w_yr)], "nn", BF16, "lat_proj_yr")
    gp_s = _mm([(h, w_gs)], "nn", BF16, "lat_proj_gs")
    gp_r = _mm([(h, w_gr)], "nn", BF16, "lat_proj_gr")
    l_xbc, l_dtr, l_xr_seq, l_cx, l_xc = mixer_inputs(h, "lat", True)
    l_ssd, late = [], ()
    for dirn in (0, 1):
        (yv, se, _), late = _ssd_fwd(l_cx, l_dtr[:, dirn * heads:(dirn + 1) * heads], bias_s[dirn:dirn + 1], alog_s[dirn:dirn + 1],
                                     c_ssd[dirn][1], bool(dirn), f"lat_ssd_fwd{dirn}", rider=late_gather(dirn, late))
        l_ssd.append((yv, se))
    (hf, _), late = _lru_fwd(l_xc, *lru_args(0), c_lru[0][1], None, False, "lat_lru_fwd0", rider=late_gather(2, late))
    hb, _, r_seq = _lru_fwd(l_xc, *lru_args(1), c_lru[1][1], hf, True, "lat_lru_fwd1")
    wg_os, wg_ol, wg_o, wg_13, wg_2 = late
    w_os, w_ol, w_oo, w_2 = wg_os.reshape(ws, d), wg_ol.reshape(wl, d), wg_o.reshape(d, d), wg_2.reshape(ff, d)
    w_13f = colcat(wg_13)
    w_1, w_3 = w_13f[:, :ff], w_13f[:, ff:]

    def ssd_out(r, s):
        yf, yb, cx_, z_ = r
        yv = yf + yb + s[0] * cx_[:, :ws]
        yz = yv * _silu(z_)
        return [yz * lax.rsqrt(_rowmean(yz * yz) + EPS) * s[1]], []
    (ys,), _ = _rowwise(ssd_out, [l_ssd[0][0], l_ssd[1][0], l_cx, z], [d_full, ssd_norm], [(ws, BF16)], [], "lat_ssd_out")
    o_s = _mm([(ys, w_os)], "nn", BF16, "lat_out_ssd")

    r_out = _to_row_major(r_seq)
    (gr,), _ = _rowwise(lambda r, s: ([r[0] * _gelu(r[1])], []), [r_out, yr], [], [(wl, BF16)], [], "lat_lru_out")
    o_r = _mm([(gr, w_ol)], "nn", BF16, "lat_out_lru")

    bg_s, bg_r = b_gate[:, :d], b_gate[:, d:]
    (mix,), _ = _rowwise(lambda r, s: ([_sigmoid(r[0] + s[0]) * r[2] + _sigmoid(r[1] + s[1]) * r[3]], []),
                         [gp_s, gp_r, o_s, o_r], [bg_s, bg_r], [(d, BF16)], [], "lat_merge")
    mixo = _mm([(mix, w_oo)], "nn", F32, "lat_out_o")

    def resid_norm(r, s):
        x1v = r[0] + s[0] * r[1]
        xn = x1v * lax.rsqrt(_rowmean(x1v * x1v) + EPS)
        return [x1v, xn * s[1] * (1.0 + s[2]) + s[3]], []
    (x1, h2), _ = _rowwise(resid_norm, [x2, mixo], [g_m, norm_ffn, sc_f, sh_f], [(d, F32), (d, BF16)], [], "lat_resid1")
    fg = _mm([(h2, w_1)], "nn", BF16, "ffn_gate")
    fu = _mm([(h2, w_3)], "nn", BF16, "ffn_up")
    (act,), _ = _rowwise(lambda r, s: ([_silu(r[0]) * r[1]], []), [fg, fu], [], [(ff, BF16)], [], "ffn_act")
    fo = _mm([(act, w_2)], "nn", F32, "ffn_down")

    fn_gain = final_norm[None, :]

    def head(r, s):
        x1v, fv, tv = r
        xv = x1v + s[0] * fv
        rs = lax.rsqrt(_rowmean(xv * xv) + EPS)
        xn = xv * rs
        err = xn * s[1] - tv
        dyv = err * (1.0 / d)
        dxn = dyv * s[1]
        dxv = rs * (dxn - xn * _rowmean(dxn * xn))
        return [dxv, dxv * s[0]], [_colsum(err * err), _colsum(dyv * xn), _colsum(dxv * fv)]
    (dx2, dfo), (sq, g_final, dg_f) = _rowwise(head, [x1, fo, tgt], [g_f, fn_gain], [(d, F32), (d, BF16)], [d, d, d], "loss_head")
    loss = lax.psum(0.5 * jnp.sum(sq) / d, ("x", "y", "c"))

    dact = _mm([(dfo, w_2)], "nt", BF16, "ffn_down_dx")
    gw_2 = _mm([(act, dfo)], "tn", F32, "ffn_down_dw")
    (dfg, dfu), _ = _rowwise(lambda r, s: ([r[0] * r[2] * _dsilu(r[1]), r[0] * _silu(r[1])], []), [dact, fg, fu], [],
                             [(ff, BF16), (ff, BF16)], [], "ffn_act_bwd")
    gw_13 = _mm([(h2, dfg)], "tn", F32, "ffn_gate_dw", slots=(ff // 4, N_DEV, 0, None))
    gw_13 = _mm([(h2, dfu)], "tn", F32, "ffn_up_dw", slots=(ff // 4, N_DEV, N_DEV // 2, gw_13))
    slots = lambda g: jnp.moveaxis(g.reshape(g.shape[0], N_DEV, g.shape[1] // N_DEV), 1, 0)
    rslots = lambda g: g.reshape(N_DEV, g.shape[0] // N_DEV, g.shape[1])
    ffn_rs = _ReduceScatter([gw_13, rslots(gw_2)], "ffn_rs")
    dh2, got = _mm([(dfg, w_1), (dfu, w_3)], "nt", F32, "ffn_dx", rider=_swap_rider(*ffn_rs.stage()))
    ffn_rs.absorb(got)
    dx1, (dsh_f, dsc_f, g_norm_ffn) = _norm_mod_bwd(x1, dh2, dx2, norm_ffn, sc_f, "lat_norm_ffn_bwd")

    (dmixo,), (dg_m,) = _rowwise(lambda r, s: ([s[0] * r[0]], [_colsum(r[0] * r[1])]), [dx1, mixo], [g_m], [(d, BF16)], [d], "lat_resid1_bwd")
    dmix = _mm([(dmixo, w_oo)], "nt", BF16, "lat_out_o_dx")
    gw_o = _mm([(mix, dmixo)], "tn", F32, "lat_out_o_dw")

    def merge_bwd(r, s):
        dm, gps, gpr, os_, or_ = r
        g1, g2 = _sigmoid(gps + s[0]), _sigmoid(gpr + s[1])
        dg1, dg2 = dm * os_ * g1 * (1.0 - g1), dm * or_ * g2 * (1.0 - g2)
        return [dm * g1, dm * g2, dg1, dg2], [_colsum(dg1), _colsum(dg2)]
    (do_s, do_r, dgp_s, dgp_r), (gb_gs, gb_gr) = _rowwise(merge_bwd, [dmix, gp_s, gp_r, o_s, o_r], [bg_s, bg_r],
                                                            [(d, BF16)] * 4, [d, d], "lat_merge_bwd")
    gw_gate = _mm([(h, dgp_s)], "tn", F32, "lat_gs_dw", slots=(d // 4, N_DEV, 0, None))
    gw_gate = _mm([(h, dgp_r)], "tn", F32, "lat_gr_dw", slots=(d // 4, N_DEV, N_DEV // 2, gw_gate))

    dgr = _mm([(do_r, w_ol)], "nt", BF16, "lat_out_lru_dx")
    gw_ol = _mm([(gr, do_r)], "tn", F32, "lat_out_lru_dw")
    (dr_out, dyr), _ = _rowwise(lambda r, s: ([r[0] * _gelu(r[2]), r[0] * r[1] * _dgelu(r[2])], []), [dgr, r_out, yr], [],
                                [(wl, F32), (wl, BF16)], [], "lat_lru_out_bwd")
    dr_seq = _to_col_major(dr_out)
    dys = _mm([(do_s, w_os)], "nt", BF16, "lat_out_ssd_dx")
    gw_os = _mm([(ys, do_s)], "tn", F32, "lat_out_ssd_dw")
    mid_rs = _ReduceScatter([rslots(gw_os), rslots(gw_ol), gw_gate, rslots(gw_o)], "mid_rs")
    n_ffn = 2 * ffn_rs.n
    lb0, got = _lru_bwd(l_xc, *lru_args(0), c_lru[0][1], hf, dr_seq, zeros_l, None, False, "lat_lru_bwd0",
                        rider=_merge_riders(_swap_rider(*ffn_rs.stage()), _swap_rider(*mid_rs.stage())))
    ffn_rs.absorb(got[:n_ffn])
    mid_rs.absorb(got[n_ffn:])
    lb1, got = _lru_bwd(l_xc, *lru_args(1), c_lru[1][1], hb, dr_seq, zeros_l, lb0[0], True, "lat_lru_bwd1",
                        rider=_merge_riders(_swap_rider(*ffn_rs.stage()), _swap_rider(*mid_rs.stage())))
    ffn_rs.absorb(got[:n_ffn])
    mid_rs.absorb(got[n_ffn:])
    dxr_seq, g_lconv_w, g_lconv_b = _conv_bwd(l_xr_seq, [lb1[0]], lambda v: v[0], conv_w_l, lru_conv_b, False, "lat_lru_conv_bwd")
    dxr = _to_row_major(dxr_seq)


    def ssd_out_bwd(r, s):
        yf, yb, cx_, z_, dyn = r
        xs_ = cx_[:, :ws]
        yv = yf + yb + s[0] * xs_
        sz = _silu(z_)
        yz = yv * sz
        rs = lax.rsqrt(_rowmean(yz * yz) + EPS)
        yzn = yz * rs
        dyzn = dyn * s[1]
        dyz = rs * (dyzn - yzn * _rowmean(dyzn * yzn))
        dyv = dyz * sz
        return [dyv, dyz * yv * _dsilu(z_), dyv * s[0]], [_colsum(dyv * xs_), _colsum(dyn * yzn)]
    (dy_ssd, dz, dxs_skip), (g_dfull, g_ssd_norm) = _rowwise(ssd_out_bwd, [l_ssd[0][0], l_ssd[1][0], l_cx, z, dys], [d_full, ssd_norm],
                                                              [(ws, F32), (ws, BF16), (ws, F32)], [ws, ws], "lat_ssd_out_bwd")
    sb0, got = _ssd_bwd(l_cx, l_dtr[:, :heads], bias_s[0:1], alog_s[0:1], dy_ssd, l_ssd[0][1], zeros_state, False, "lat_ssd_bwd0",
                        rider=_swap_rider(*mid_rs.stage()))
    mid_rs.absorb(got)
    sb = [sb0, _ssd_bwd(l_cx, l_dtr[:, heads:], bias_s[1:2], alog_s[1:2], dy_ssd, l_ssd[1][1], zeros_state, True, "lat_ssd_bwd1")]
    dxbc, g_sconv_w, g_sconv_b = _conv_bwd(
        l_xbc, [dxs_skip, sb[0][0], sb[1][0], sb[0][1], sb[1][1], sb[0][2], sb[1][2]],
        lambda v: jnp.concatenate([v[0] + v[1] + v[2], v[3] + v[4], v[5] + v[6]], axis=1), conv_w_s, ssd_conv_b, True, "lat_ssd_conv_bwd")
    ddt = jnp.pad(jnp.concatenate([sb[0][3], sb[1][3]], axis=1), ((0, 0), (0, LANES - 2 * heads)))

    zeros_cy = jnp.zeros((tc, ws), F32)
    csb = [_ssd_bwd(c_cx, c_dtr[:, dirn * heads:(dirn + 1) * heads], bias_s[dirn:dirn + 1], alog_s[dirn:dirn + 1],
                    zeros_cy, c_ssd[dirn][0], sb[dirn][6], bool(dirn), f"ctx_ssd_bwd{dirn}") for dirn in (0, 1)]
    c_dxbc, cg_sconv_w, cg_sconv_b = _conv_bwd(
        c_xbc, [csb[0][0], csb[1][0], csb[0][1], csb[1][1], csb[0][2], csb[1][2]],
        lambda v: jnp.concatenate([v[0] + v[1], v[2] + v[3], v[4] + v[5]], axis=1), conv_w_s, ssd_conv_b, True, "ctx_ssd_conv_bwd")
    c_ddt = jnp.pad(jnp.concatenate([csb[0][3], csb[1][3]], axis=1), ((0, 0), (0, LANES - 2 * heads)))
    zeros_ch = jnp.zeros((tc, wl), F32)
    clb0 = _lru_bwd(c_xc, *lru_args(0), zeros_l, c_lru[0][0], zeros_ch, lb0[6], None, False, "ctx_lru_bwd0")
    clb1 = _lru_bwd(c_xc, *lru_args(1), zeros_l, c_lru[1][0], zeros_ch, lb1[6], clb0[0], True, "ctx_lru_bwd1")
    c_dxr, cg_lconv_w, cg_lconv_b = _conv_bwd(c_xr_seq, [clb1[0]], lambda v: v[0], conv_w_l, lru_conv_b, False, "ctx_lru_conv_bwd")
    dhc = _mm([(c_dxbc, w_xbc), (c_ddt, w_dt), (c_dxr, w_xr)], "nt", F32, "ctx_proj_dx")
    _, (dcsh_m, dcsc_m, cg_norm_mix) = _norm_mod_bwd(ctx2, dhc, None, norm_mix, csc_m, "ctx_norm_bwd")

    gw_z = _mm([(h, dz)], "tn", F32, "lat_z_dw")
    gw_yr = _mm([(h, dyr)], "tn", F32, "lat_yr_dw")
    gw_xbc = _mm([(h, dxbc), (hc, c_dxbc)], "tn", F32, "proj_xbc_dw")
    gw_dt = _mm([(h, ddt), (hc, c_ddt)], "tn", F32, "proj_dt_dw")[:, :2 * heads]
    gw_xr = _mm([(h, dxr), (hc, c_dxr)], "tn", F32, "proj_xr_dw")
    dh = _mm([(dz, w_z), (dxbc, w_xbc), (ddt, w_dt)], "nt", F32, "lat_proj_dx0")
    dh = _mm([(dxr, w_xr), (dyr, w_yr)], "nt", F32, "lat_proj_dx1", init=dh)
    dh = _mm([(dgp_s, w_gs), (dgp_r, w_gr)], "nt", F32, "lat_proj_dx2", init=dh)
    grad_x, (dsh_m, dsc_m, g_norm_mix) = _norm_mod_bwd(x2, dh, dx1, norm_mix, sc_m, "lat_norm_bwd")

    zero_d = jnp.zeros((1, d), F32)
    dmod_rows = jnp.concatenate([jnp.concatenate([dsh_m, dsc_m, dg_m, dsh_f, dsc_f, dg_f], axis=1),
                                 jnp.concatenate([dcsh_m, dcsc_m, zero_d, zero_d, zero_d, zero_d], axis=1)], axis=0)
    (dmod_g,) = _all_gather([dmod_rows], "gather_dmod")
    dmod_c = functools.reduce(lambda u, v: u + v, [dmod_g[i, 1] for i in range(N_DEV)])
    dmod9 = jnp.concatenate([dmod_g[:, 0], dmod_c[None, :], jnp.zeros((16 - N_DEV - 1, N_MOD * d), F32)], axis=0)
    g_b_ada = jnp.sum(dmod9, axis=0, keepdims=True)
    dmod_l = lax.dynamic_slice_in_dim(dmod9, me * ada_cols, ada_cols, 1)
    g_w_ada = _mm([(cond_act, dmod_l)], "tn", F32, "ada_dw")
    dcond_l = _mm([(dmod_l, w_ada[0])], "nt", F32, "ada_dx")

    small_parts = [
        dcond_l[N_DEV:N_DEV + 1], g_norm_mix + cg_norm_mix, g_norm_ffn, g_sconv_w + cg_sconv_w, g_sconv_b + cg_sconv_b,
        jnp.concatenate([sb[0][5] + csb[0][5], sb[1][5] + csb[1][5]], axis=0),
        jnp.concatenate([sb[0][4] + csb[0][4], sb[1][4] + csb[1][4]], axis=0),
        g_dfull, g_ssd_norm, g_lconv_w + cg_lconv_w, g_lconv_b + cg_lconv_b,
        jnp.stack([lb0[1] + clb0[1], lb1[1] + clb1[1]]), jnp.concatenate([lb0[3] + clb0[3], lb1[3] + clb1[3]], axis=0),
        jnp.stack([lb0[2] + clb0[2], lb1[2] + clb1[2]]), jnp.concatenate([lb0[4] + clb0[4], lb1[4] + clb1[4]], axis=0),
        jnp.concatenate([lb0[5] + clb0[5], lb1[5] + clb1[5]], axis=0),
        jnp.concatenate([gb_gs, gb_gr], axis=1), g_final,
    ]
    (s_cctx, s_nmix, s_nffn, s_sconv_w, s_sconv_b, s_dtb, s_da, s_dfull, s_snorm, s_lconv_w, s_lconv_b,
     s_wa, s_ba, s_wx, s_bx, s_dsp, s_bgate, s_final) = _all_reduce(small_parts, "small_grads", narrow=(11, 13))
    shard = lambda a, n_: lax.dynamic_slice_in_dim(a, me * n_, n_, a.ndim - 1)
    grads = {
        "c_ctx": (s_cctx * _dsilu(c_ctx[None, :]))[0],
        "w_ada": g_w_ada[None], "b_ada": g_b_ada, "norm_mix": s_nmix, "norm_ffn": s_nffn,
        "ssd_conv_w": shard(s_sconv_w, ssd_conv_w.shape[2])[None], "ssd_conv_b": s_sconv_b,
        "ssd_dt_bias": s_dtb[None], "ssd_a_log": (s_da * (-jnp.exp(alog_s)))[None],
        "ssd_d": jnp.sum(s_dfull.reshape(heads, SSD_HEAD_DIM), axis=1)[None], "ssd_norm": s_snorm,
        "lru_conv_w": shard(s_lconv_w, lru_conv_w.shape[2])[None], "lru_conv_b": s_lconv_b,
        "lru_w_a": s_wa[None], "lru_b_a": shard(s_ba, lru_b_a.shape[2])[None],
        "lru_w_x": s_wx[None], "lru_b_x": shard(s_bx, lru_b_x.shape[2])[None],
        "lru_lambda": shard(s_dsp * (-_sigmoid(-lru_lam)), lru_lambda.shape[2])[None],
        "b_gate": s_bgate, "final_norm": s_final[0],
    }

    gw_in = jnp.concatenate([gw_z, gw_xbc, gw_dt, gw_xr, gw_yr], axis=1)
    for n_, g in zip(["w_in", "w_out_ssd", "w_out_lru", "w_gate", "w_o", "ffn_w13", "ffn_w2"],
                     _reduce_scatter([slots(gw_in)], "grad_rs") + mid_rs.result() + ffn_rs.result()):
        grads[n_] = g[None]

    big_names = ["w_ada", "w_in", "w_out_ssd", "w_out_lru", "w_gate", "w_o", "ffn_w13", "ffn_w2", "lru_w_a", "lru_w_x"]
    small_names = [n_ for n_ in names if n_ not in big_names]
    delta, new_m, new_v = {}, {}, {}
    for n_ in big_names:
        delta[n_], new_m[n_], new_v[n_] = _adamw(wts[n_], grads[n_], mom1[n_], mom2[n_], "adamw_" + n_)
    packed = [_pack([src[n_] for n_ in small_names]) for src in (wts, grads, mom1, mom2)]
    outs = _adamw(*packed, "adamw_small")
    like = [wts[n_] for n_ in small_names]
    for dst, buf in zip((delta, new_m, new_v), outs):
        dst.update(zip(small_names, _unpack(buf, like)))

    return (loss, grad_x[None], *[grads[n_].reshape(wts[n_].shape) for n_ in names], *[delta[n_] for n_ in names],
            *[new_m[n_] for n_ in names], *[new_v[n_] for n_ in names])
```

```python
import functools

import numpy as np
import jax
import jax.numpy as jnp
from jax import lax
from jax.experimental import pallas as pl
from jax.experimental.pallas import tpu as pltpu

F32, BF16 = jnp.float32, jnp.bfloat16
MESH = pl.DeviceIdType.MESH
N_DEV = 8
VMEM_LIMIT_BYTES = 48 * 1024 * 1024
ROW_BUDGET_BYTES = 12 * 1024 * 1024
LANES, SUBLANES = 128, 8

EPS = 1e-6
GRID_W = 64
SSD_HEAD_DIM = 64
SSD_GROUPS = 4
SSD_STATE = 128
SSD_CHUNK = 128
CONV_W = 4
CONV_LEFT = 2
LRU_HEAD_DIM = 128
LRU_C = 8.0
LRU_BLOCK = 128
N_MOD = 6
ADAM_LR, ADAM_B1, ADAM_B2, ADAM_EPS, ADAM_WD, ADAM_STEP = 0.001, 0.9, 0.999, 1e-08, 0.01, 10
NEG_BIG = -1e30


def _params(sem):
    return pltpu.CompilerParams(dimension_semantics=sem, vmem_limit_bytes=VMEM_LIMIT_BYTES)


def _tile(dim, pref, align):
    if dim <= pref:
        return dim
    t = (pref // align) * align
    while t >= align:
        if dim % t == 0:
            return t
        t -= align
    return dim


class _Rider:
    def __init__(self, ins, out_shape, aliases, sems, start, wait):
        self.ins, self.out_shape, self.aliases, self.sems, self.start, self.wait = ins, out_shape, aliases, sems, start, wait


def _merge_riders(a, b):
    ai, ao, asem = len(a.ins), len(a.out_shape), len(a.sems)
    aliases = dict(a.aliases)
    aliases.update({ai + i: ao + o for i, o in b.aliases.items()})

    def both(fa, fb):
        def run(ins, outs, sems):
            fa(ins[:ai], outs[:ao], sems[:asem])
            fb(ins[ai:], outs[ao:], sems[asem:])
        return run

    return _Rider(list(a.ins) + list(b.ins), list(a.out_shape) + list(b.out_shape), aliases, list(a.sems) + list(b.sems),
                  both(a.start, b.start), both(a.wait, b.wait))


def _call(body, name, grid, in_specs, out_specs, out_shape, scratch_shapes, sem, operands, rider=None):
    if rider is None:
        return pl.pallas_call(body, name=name, grid=grid, in_specs=in_specs, out_specs=out_specs, out_shape=out_shape,
                              scratch_shapes=scratch_shapes, compiler_params=_params(sem))(*operands)
    n_in, n_out, n_scr = len(in_specs), len(out_specs), len(scratch_shapes)
    ri, ro = len(rider.ins), len(rider.out_shape)
    a, b = n_in, n_in + ri
    c, d = b + n_out, b + n_out + ro
    e = d + n_scr

    def carried(*refs):
        ids = [pl.program_id(i) for i in range(len(grid))]
        first = functools.reduce(lambda u, v: u & v, [i == 0 for i in ids])
        last = functools.reduce(lambda u, v: u & v, [i == g - 1 for i, g in zip(ids, grid)])

        @pl.when(first)
        def _():
            rider.start(refs[a:b], refs[c:d], refs[e:])

        body(*refs[:a], *refs[b:c], *refs[d:e])

        @pl.when(last)
        def _():
            rider.wait(refs[a:b], refs[c:d], refs[e:])

    any_spec = pl.BlockSpec(memory_space=pl.ANY)
    res = pl.pallas_call(
        carried, name=name, grid=grid, in_specs=list(in_specs) + [any_spec] * ri, out_specs=list(out_specs) + [any_spec] * ro,
        out_shape=list(out_shape) + list(rider.out_shape), scratch_shapes=list(scratch_shapes) + list(rider.sems),
        input_output_aliases={n_in + i: n_out + o for i, o in rider.aliases.items()},
        compiler_params=_params(("arbitrary",) * len(grid)),
    )(*operands, *rider.ins)
    return res[:n_out], res[n_out:]


def _bdot(a, b):
    return jnp.dot(a.astype(BF16), b.astype(BF16), preferred_element_type=F32)


def _bdot_nt(a, b):
    return lax.dot_general(a.astype(BF16), b.astype(BF16), (((1,), (1,)), ((), ())), preferred_element_type=F32)


def _bdot_tn(a, b):
    return lax.dot_general(a.astype(BF16), b.astype(BF16), (((0,), (0,)), ((), ())), preferred_element_type=F32)


def _hdot(a, b):
    return jnp.dot(a, b, preferred_element_type=F32, precision=lax.Precision.HIGHEST)


def _softplus(v):
    return jnp.maximum(v, 0.0) + jnp.log(1.0 + jnp.exp(-jnp.abs(v)))


def _sigmoid(v):
    return 0.5 * jnp.tanh(0.5 * v) + 0.5


def _silu(v):
    return v * _sigmoid(v)


def _dsilu(v):
    s = _sigmoid(v)
    return s * (1.0 + v * (1.0 - s))


GELU_C0 = 0.7978845608028654
GELU_C1 = 0.044715


def _gelu(v):
    return 0.5 * v * (1.0 + jnp.tanh(GELU_C0 * (v + GELU_C1 * v * v * v)))


def _dgelu(v):
    t = jnp.tanh(GELU_C0 * (v + GELU_C1 * v * v * v))
    return 0.5 * (1.0 + t) + 0.5 * v * (1.0 - t * t) * GELU_C0 * (1.0 + 3.0 * GELU_C1 * v * v)


MM_TM, MM_TN = 1024, 1408
MM_TK = {1: 2048, 2: 1536}


def _mm(pairs, mode, out_dtype, name, init=None, rider=None, slots=None):
    a0, b0 = pairs[0]
    if mode == "nn":
        m, n, ks = a0.shape[0], b0.shape[1], [a.shape[1] for a, _ in pairs]
    elif mode == "nt":
        m, n, ks = a0.shape[0], b0.shape[0], [a.shape[1] for a, _ in pairs]
    else:
        m, n, ks = a0.shape[1], b0.shape[1], [a.shape[0] for a, _ in pairs]
    tm = _tile(m, MM_TM, LANES if mode == "tn" else 16)
    tn = _tile(n, MM_TN, LANES) if slots is None else slots[0]
    tk = MM_TK.get(sum(k > 512 for k in ks), 512)
    tks = [_tile(k, tk, 16 if mode == "tn" else LANES) for k in ks]
    steps = [k // t for k, t in zip(ks, tks)]
    offs = [sum(steps[:i]) for i in range(len(pairs))]
    total = sum(steps)
    npairs = len(pairs)

    def kk(i):
        return lambda k: jnp.clip(k - offs[i], 0, steps[i] - 1)

    in_specs, operands = [], []
    for i, (a, b) in enumerate(pairs):
        kf = kk(i)
        if mode == "nn":
            in_specs.append(pl.BlockSpec((tm, tks[i]), lambda mi, ni, k, kf=kf: (mi, kf(k))))
            in_specs.append(pl.BlockSpec((tks[i], tn), lambda mi, ni, k, kf=kf: (kf(k), ni)))
        elif mode == "nt":
            in_specs.append(pl.BlockSpec((tm, tks[i]), lambda mi, ni, k, kf=kf: (mi, kf(k))))
            in_specs.append(pl.BlockSpec((tn, tks[i]), lambda mi, ni, k, kf=kf: (ni, kf(k))))
        else:
            in_specs.append(pl.BlockSpec((tks[i], tm), lambda mi, ni, k, kf=kf: (kf(k), mi)))
            in_specs.append(pl.BlockSpec((tks[i], tn), lambda mi, ni, k, kf=kf: (kf(k), ni)))
        operands += [a, b]
    dot = {"nn": _bdot, "nt": _bdot_nt, "tn": _bdot_tn}[mode]
    if init is not None:
        in_specs.append(pl.BlockSpec((tm, tn), lambda mi, ni, k: (mi, ni)))
        operands.append(init)
    assert out_dtype == F32 or total == 1

    def body(*refs):
        out_ref = refs[-1]
        k = pl.program_id(2)

        def first_step():
            p = dot(refs[0][...], refs[1][...])
            return p if init is None else p + refs[2 * npairs][...]

        if total == 1:
            out_ref[...] = first_step().astype(out_dtype)
            return

        @pl.when(k == 0)
        def _():
            out_ref[...] = first_step()

        for i in range(npairs):
            first = 1 if i == 0 else offs[i]
            if offs[i] + steps[i] > first:
                @pl.when((k >= first) & (k < offs[i] + steps[i]))
                def _(i=i):
                    out_ref[...] += dot(refs[2 * i][...], refs[2 * i + 1][...])

    if slots is not None:
        width, n_slots, first_slot, into = slots
        assert rider is None and init is None and n % width == 0
        aliases = {}
        if into is not None:
            in_specs.append(pl.BlockSpec(memory_space=pl.ANY))
            operands.append(into)
            aliases = {len(operands) - 1: 0}
        return pl.pallas_call(
            body, name=name, grid=(m // tm, n // tn, total), in_specs=in_specs,
            out_specs=pl.BlockSpec((None, tm, tn), lambda mi, ni, k: (ni + first_slot, mi, 0)),
            out_shape=jax.ShapeDtypeStruct((n_slots, m, width), out_dtype), input_output_aliases=aliases,
            compiler_params=_params(("parallel", "parallel", "arbitrary")))(*operands)
    res = _call(body, name, (m // tm, n // tn, total), in_specs, [pl.BlockSpec((tm, tn), lambda mi, ni, k: (mi, ni))],
                [jax.ShapeDtypeStruct((m, n), out_dtype)], [], ("parallel", "parallel", "arbitrary"), operands, rider)
    return res[0] if rider is None else (res[0][0], res[1])


def _rowwise(fn, rows, small, outs, accs, name):
    t = rows[0].shape[0]
    per_row = sum(r.shape[1] * r.dtype.itemsize for r in rows) + sum(c * jnp.dtype(d).itemsize for c, d in outs)
    tm = SUBLANES
    while tm * 2 <= 1024 and t % (tm * 2) == 0 and tm * 2 * per_row <= ROW_BUDGET_BYTES:
        tm *= 2
    if t % tm:
        tm = t
    nrow, nsmall, nout, nacc = len(rows), len(small), len(outs), len(accs)

    def body(*refs):
        rv = [r[...].astype(F32) for r in refs[:nrow]]
        sv = [r[...] for r in refs[nrow:nrow + nsmall]]
        o_refs = refs[nrow + nsmall:nrow + nsmall + nout]
        a_refs = refs[nrow + nsmall + nout:]
        ov, av = fn(rv, sv)
        for r, v in zip(o_refs, ov):
            r[...] = v.astype(r.dtype)
        if nacc:
            @pl.when(pl.program_id(0) == 0)
            def _():
                for r in a_refs:
                    r[...] = jnp.zeros_like(r)
            for r, v in zip(a_refs, av):
                r[...] += v

    in_specs = [pl.BlockSpec((tm, r.shape[1]), lambda i: (i, 0)) for r in rows]
    in_specs += [pl.BlockSpec(s.shape, lambda i: (0, 0)) for s in small]
    out_specs = [pl.BlockSpec((tm, c), lambda i: (i, 0)) for c, _ in outs]
    out_specs += [pl.BlockSpec((1, c), lambda i: (0, 0)) for c in accs]
    out_shape = [jax.ShapeDtypeStruct((t, c), d) for c, d in outs]
    out_shape += [jax.ShapeDtypeStruct((1, c), F32) for c in accs]
    res = pl.pallas_call(
        body, name=name, grid=(t // tm,), in_specs=in_specs, out_specs=out_specs, out_shape=out_shape,
        compiler_params=_params(("arbitrary",)),
    )(*rows, *small)
    return res[:nout], res[nout:]


def _colsum(v):
    return jnp.sum(v, axis=0, keepdims=True)


def _rowmean(v):
    return jnp.mean(v, axis=1, keepdims=True)


def _conv_tiles(x):
    t, c = x.shape
    tm = _tile(t, 256, SUBLANES)
    nt = t // tm
    r8 = tm // SUBLANES
    specs = [
        pl.BlockSpec((SUBLANES, c), lambda i: (jnp.maximum(i * r8 - 1, 0), 0)),
        pl.BlockSpec((tm, c), lambda i: (i, 0)),
        pl.BlockSpec((SUBLANES, c), lambda i: (jnp.minimum((i + 1) * r8, t // SUBLANES - 1), 0)),
    ]
    return tm, nt, specs


def _conv_taps(prev_ref, cur_ref, next_ref, tm, nt, left):
    i = pl.program_id(0)
    p = jnp.where(i > 0, prev_ref[...], 0.0)
    q = jnp.where(i < nt - 1, next_ref[...], 0.0)
    ext = jnp.concatenate([p, cur_ref[...], q], axis=0)
    n = tm + 2 * SUBLANES
    taps = []
    for k in range(CONV_W):
        s = (left - k) % n
        r = pltpu.roll(ext, s, axis=0) if s else ext
        taps.append(r[SUBLANES:SUBLANES + tm])
    return taps


def _conv(x, w, b, left, act, name, out_dtype=F32):
    t, c = x.shape
    tm, nt, specs = _conv_tiles(x)

    def body(prev_ref, cur_ref, next_ref, w_ref, b_ref, o_ref):
        taps = _conv_taps(prev_ref, cur_ref, next_ref, tm, nt, left)
        pre = b_ref[...] + sum(w_ref[k:k + 1, :] * taps[k] for k in range(CONV_W))
        o_ref[...] = (_silu(pre) if act else pre).astype(out_dtype)

    return pl.pallas_call(
        body, name=name, grid=(nt,),
        in_specs=specs + [pl.BlockSpec((CONV_W, c), lambda i: (0, 0)), pl.BlockSpec((1, c), lambda i: (0, 0))],
        out_specs=pl.BlockSpec((tm, c), lambda i: (i, 0)),
        out_shape=jax.ShapeDtypeStruct((t, c), out_dtype), compiler_params=_params(("arbitrary",)),
    )(x, x, x, w, b)


def _conv_bwd(x, dys, combine, w, b, act, name):
    t, c = x.shape
    tm, nt, specs = _conv_tiles(x)
    nd = len(dys)

    def body(*refs):
        prev_ref, cur_ref, next_ref = refs[:3]
        w_ref, b_ref, dpre_ref, dw0, dw1, dw2, dw3, db_ref = refs[3 + nd:]
        dy = combine([r[...] for r in refs[3:3 + nd]])
        taps = _conv_taps(prev_ref, cur_ref, next_ref, tm, nt, CONV_LEFT)
        if act:
            pre = b_ref[...] + sum(w_ref[k:k + 1, :] * taps[k] for k in range(CONV_W))
            dpre = dy * _dsilu(pre)
        else:
            dpre = dy
        dpre_ref[...] = dpre

        @pl.when(pl.program_id(0) == 0)
        def _():
            for r in (dw0, dw1, dw2, dw3, db_ref):
                r[...] = jnp.zeros_like(r)

        for k, r in enumerate((dw0, dw1, dw2, dw3)):
            r[...] += _colsum(dpre * taps[k])
        db_ref[...] += _colsum(dpre)

    row = pl.BlockSpec((1, c), lambda i: (0, 0))
    res = pl.pallas_call(
        body, name=name + "_pre", grid=(nt,),
        in_specs=specs + [pl.BlockSpec((tm, a.shape[1]), lambda i: (i, 0)) for a in dys] + [pl.BlockSpec((CONV_W, c), lambda i: (0, 0)), row],
        out_specs=[pl.BlockSpec((tm, c), lambda i: (i, 0))] + [row] * 5,
        out_shape=[jax.ShapeDtypeStruct((t, c), F32)] + [jax.ShapeDtypeStruct((1, c), F32)] * 5,
        compiler_params=_params(("arbitrary",)),
    )(x, x, x, *dys, w, b)
    dpre, dw, db = res[0], jnp.concatenate(res[1:5], axis=0), res[5]
    dx = _conv(dpre, w[::-1], jnp.zeros_like(b), CONV_W - 1 - CONV_LEFT, False, name + "_dx", BF16)
    return dx, dw, db


def _ssd_dims(xbc):
    t = xbc.shape[0]
    gn = SSD_GROUPS * SSD_STATE
    ws = xbc.shape[1] - 2 * gn
    heads = ws // SSD_HEAD_DIM
    hpg = heads // SSD_GROUPS
    assert hpg % 2 == 0 and ws % gn == 0 and t % SSD_CHUNK == 0
    return t, ws, heads, hpg, heads // 2, t // SSD_CHUNK


def _ssd_chunk_terms(dtr_ref, dtrt_ref, bias_ref, biast_ref, alog_ref, alogt_ref, reverse):
    q = SSD_CHUNK
    a_neg = -jnp.exp(alog_ref[...])
    dt = _softplus(dtr_ref[...] + bias_ref[...])
    dtt = _softplus(dtrt_ref[...] + biast_ref[...])
    a = dt * a_neg
    at = dtt * (-jnp.exp(alogt_ref[...]))
    ii = lax.broadcasted_iota(jnp.int32, (q, q), 0)
    jj = lax.broadcasted_iota(jnp.int32, (q, q), 1)
    mask = (ii <= jj) if reverse else (ii >= jj)
    tri = mask.astype(F32)
    trit = ((jj <= ii) if reverse else (jj >= ii)).astype(F32)
    before = (jj > ii) if reverse else (jj < ii)
    ac = _hdot(tri, a)
    act = _hdot(at, trit)
    tot = _colsum(a)
    return a_neg, dt, a, ac, act, tot, mask, trit, before


def _pair_terms(lo, h1, dt, ac, tot):
    q = SSD_CHUNK

    def colb(v, h):
        return jnp.broadcast_to(v[:, h:h + 1], (v.shape[0], LANES))

    def pairb(v):
        return jnp.where(lo[:v.shape[0]], colb(v, h1), colb(v, h1 + 1))

    dt_p = pairb(dt)
    ac_p = pairb(ac)
    eac_p = jnp.exp(ac_p)
    dte_p = jnp.exp(pairb(jnp.broadcast_to(tot, (q, tot.shape[1]))) - ac_p)
    etot_p = jnp.exp(pairb(tot))
    return dt_p, eac_p, dte_p, etot_p


def _decay_mats(mask, ac, act, cb, h1):
    q = SSD_CHUNK
    out = []
    for h in (h1, h1 + 1):
        diff = jnp.broadcast_to(ac[:, h:h + 1], (q, q)) - jnp.broadcast_to(act[h:h + 1, :], (q, q))
        lmat = jnp.exp(jnp.where(mask, diff, NEG_BIG))
        out.append((lmat, cb * lmat))
    return out


def _ssd_specs(t, ws, heads, npairs, nc, reverse):
    q, gn = SSD_CHUNK, SSD_GROUPS * SSD_STATE
    ci = (lambda k: nc - 1 - k) if reverse else (lambda k: k)
    small = lambda shape: pl.BlockSpec(shape, lambda k: (0,) * len(shape))
    seq = [
        pl.BlockSpec((q, ws), lambda k: (ci(k), 0)),
        pl.BlockSpec((q, gn), lambda k: (ci(k), ws // gn)),
        pl.BlockSpec((q, gn), lambda k: (ci(k), ws // gn + 1)),
        pl.BlockSpec((q, heads), lambda k: (ci(k), 0)),
        pl.BlockSpec((heads, q), lambda k: (0, ci(k))),
        small((1, heads)), small((heads, 1)), small((1, heads)), small((heads, 1)),
    ]
    return ci, small, seq


def _ssd_fwd(xbc, dtr, bias, alog, s0, reverse, name, rider=None):
    t, ws, heads, hpg, npairs, nc = _ssd_dims(xbc)
    q, ns = SSD_CHUNK, SSD_STATE
    ci, small, seq = _ssd_specs(t, ws, heads, npairs, nc, reverse)

    def body(xs_ref, b_ref, c_ref, dtr_ref, dtrt_ref, bias_ref, biast_ref, alog_ref, alogt_ref, s0_ref,
             y_ref, se_ref, s_ref):
        @pl.when(pl.program_id(0) == 0)
        def _():
            s_ref[...] = s0_ref[...]

        _, dt, _, ac, act, tot, mask, _, _ = _ssd_chunk_terms(dtr_ref, dtrt_ref, bias_ref, biast_ref, alog_ref, alogt_ref, reverse)
        lo = lax.broadcasted_iota(jnp.int32, (q, LANES), 1) < SSD_HEAD_DIM
        for g in range(SSD_GROUPS):
            bg = b_ref[:, g * ns:(g + 1) * ns]
            cg = c_ref[:, g * ns:(g + 1) * ns]
            cb = _bdot_nt(cg, bg)
            bgt = bg.T
            for pq in range(hpg // 2):
                pp = g * (hpg // 2) + pq
                h1 = 2 * pp
                cs = slice(pp * LANES, (pp + 1) * LANES)
                (_, m1), (_, m2) = _decay_mats(mask, ac, act, cb, h1)
                dt_p, eac_p, dte_p, etot_p = _pair_terms(lo, h1, dt, ac, tot)
                xp = xs_ref[:, cs] * dt_p
                sp = s_ref[pp]
                se_ref[0, pp] = sp
                y_ref[:, cs] = jnp.where(lo, _bdot(m1, xp), _bdot(m2, xp)) + eac_p * _bdot(cg, sp)
                s_ref[pp] = etot_p * sp + _bdot(bgt, xp * dte_p)

    st = (npairs, ns, LANES)
    return _call(
        body, name, (nc,), seq + [small(st)],
        [pl.BlockSpec((q, ws), lambda k: (ci(k), 0)), pl.BlockSpec((1,) + st, lambda k: (ci(k), 0, 0, 0)), small(st)],
        [jax.ShapeDtypeStruct((t, ws), F32), jax.ShapeDtypeStruct((nc,) + st, F32), jax.ShapeDtypeStruct(st, F32)],
        [], ("arbitrary",), (xbc, xbc, xbc, dtr, dtr.T, bias, bias.T, alog, alog.T, s0), rider)


def _ssd_bwd(xbc, dtr, bias, alog, dy, s_enter, lam0, reverse, name, rider=None):
    t, ws, heads, hpg, npairs, nc = _ssd_dims(xbc)
    q, ns, gn = SSD_CHUNK, SSD_STATE, SSD_GROUPS * SSD_STATE
    ci, small, seq = _ssd_specs(t, ws, heads, npairs, nc, not reverse)
    st = (npairs, ns, LANES)

    e_heads = jnp.asarray(np.arange(ws)[:, None] // SSD_HEAD_DIM == np.arange(heads)[None, :], BF16)

    def body(xs_ref, b_ref, c_ref, dtr_ref, dtrt_ref, bias_ref, biast_ref, alog_ref, alogt_ref, dy_ref, se_ref, lam0_ref, e_ref,
             dxs_ref, db_ref, dc_ref, ddtr_ref, da_ref, dbias_ref, lam_ref, st_s, ts_s, ss_s, xx_s):
        @pl.when(pl.program_id(0) == 0)
        def _():
            lam_ref[...] = lam0_ref[...]
            da_ref[...] = jnp.zeros_like(da_ref)
            dbias_ref[...] = jnp.zeros_like(dbias_ref)

        a_neg, dt, _, ac, act, tot, mask, trit, before = _ssd_chunk_terms(dtr_ref, dtrt_ref, bias_ref, biast_ref, alog_ref, alogt_ref, reverse)
        lo = lax.broadcasted_iota(jnp.int32, (q, LANES), 1) < SSD_HEAD_DIM
        head_col = lax.broadcasted_iota(jnp.int32, (heads, 1), 0)
        ii = lax.broadcasted_iota(jnp.int32, (q, q), 0)
        jj = lax.broadcasted_iota(jnp.int32, (q, q), 1)
        before_t = ((ii > jj) if reverse else (ii < jj)).astype(F32)

        def per_head(v):
            hi = v.astype(BF16)
            lo = (v - hi.astype(F32)).astype(BF16)
            sel = e_ref[...]
            return jnp.dot(hi, sel, preferred_element_type=F32) + jnp.dot(lo, sel, preferred_element_type=F32)

        da_tt = jnp.zeros((heads, q), F32)
        for g in range(SSD_GROUPS):
            gs = slice(g * ns, (g + 1) * ns)
            bg, cg = b_ref[:, gs], c_ref[:, gs]
            cb = _bdot_nt(cg, bg)
            cgt = cg.T
            dcb = jnp.zeros((q, q), F32)
            dbg = jnp.zeros((q, ns), F32)
            dcg = jnp.zeros((q, ns), F32)
            for pq in range(hpg // 2):
                pp = g * (hpg // 2) + pq
                h1 = 2 * pp
                cs = slice(pp * LANES, (pp + 1) * LANES)
                (l1, m1), (l2, m2) = _decay_mats(mask, ac, act, cb, h1)
                dt_p, eac_p, dte_p, etot_p = _pair_terms(lo, h1, dt, ac, tot)
                xsp = xs_ref[:, cs]
                xp = xsp * dt_p
                dyp = dy_ref[:, cs]
                lam = lam_ref[pp]
                sp = se_ref[0, pp]
                dx_state = dte_p * _bdot(bg, lam)
                dxdt = jnp.where(lo, _bdot(m1.T, dyp), _bdot(m2.T, dyp)) + dx_state
                dyx1 = _bdot_nt(jnp.where(lo, dyp, 0.0), xp)
                dyx2 = _bdot_nt(jnp.where(lo, 0.0, dyp), xp)
                dcb = dcb + l1 * dyx1 + l2 * dyx2
                for h, pair in ((h1, m1 * dyx1), (h1 + 1, m2 * dyx2)):
                    cross = _colsum(jnp.where(mask, _bdot(pair, before_t), 0.0))
                    da_tt = da_tt + (head_col == h).astype(F32) * cross
                st_s[:, cs] = dyp * (eac_p * _bdot(cg, sp))
                ts_s[:, cs] = dx_state * xp
                ss_s[:, cs] = jnp.broadcast_to(_colsum(lam * sp) * etot_p, (SUBLANES, LANES))
                xx_s[:, cs] = dxdt * xsp
                edy = eac_p * dyp
                xe = dte_p * xp
                dcg = dcg + _bdot_nt(edy, sp)
                dbg = dbg + _bdot_nt(xe, lam)
                lam_ref[pp] = etot_p * lam + _bdot(cgt, edy)
                dxs_ref[:, cs] = dxdt * dt_p
            db_ref[:, gs] = dbg + _bdot(dcb.T, cg)
            dc_ref[:, gs] = dcg + _bdot(dcb, bg)
        da = da_tt.T + _hdot(trit, per_head(st_s[...])) + _hdot(before.astype(F32), per_head(ts_s[...])) + per_head(ss_s[...])[0:1]
        ddtr = (da * a_neg + per_head(xx_s[...])) * _sigmoid(dtr_ref[...] + bias_ref[...])
        ddtr_ref[...] = ddtr
        da_ref[...] += _colsum(da * dt)
        dbias_ref[...] += _colsum(ddtr)

    row = lambda c: pl.BlockSpec((q, c), lambda k: (ci(k), 0))
    return _call(
        body, name, (nc,),
        seq + [row(ws), pl.BlockSpec((1,) + st, lambda k: (ci(k), 0, 0, 0)), small(st), small((ws, heads))],
        [row(ws), row(gn), row(gn), row(heads), small((1, heads)), small((1, heads)), small(st)],
        [jax.ShapeDtypeStruct((t, ws), F32), jax.ShapeDtypeStruct((t, gn), F32), jax.ShapeDtypeStruct((t, gn), F32),
         jax.ShapeDtypeStruct((t, heads), F32), jax.ShapeDtypeStruct((1, heads), F32),
         jax.ShapeDtypeStruct((1, heads), F32), jax.ShapeDtypeStruct(st, F32)],
        [pltpu.VMEM((q, ws), F32), pltpu.VMEM((q, ws), F32), pltpu.VMEM((SUBLANES, ws), F32), pltpu.VMEM((q, ws), F32)],
        ("arbitrary",), (xbc, xbc, xbc, dtr, dtr.T, bias, bias.T, alog, alog.T, dy, s_enter, lam0, e_heads), rider)


def _tile_scan(a_ref, u_ref, out_ref, carry_ref, ntiles, reverse):
    w = a_ref.shape[1]
    row = lax.broadcasted_iota(jnp.int32, (SUBLANES, w), 0)

    def step(j, _):
        r0 = pl.multiple_of((ntiles - 1 - j if reverse else j) * SUBLANES, SUBLANES)
        a = a_ref[pl.ds(r0, SUBLANES), :]
        u = u_ref[pl.ds(r0, SUBLANES), :]
        for d in (1, 2, 4):
            keep = (row < SUBLANES - d) if reverse else (row >= d)
            s = SUBLANES - d if reverse else d
            a_sh = jnp.where(keep, pltpu.roll(a, s, axis=0), 1.0)
            u_sh = jnp.where(keep, pltpu.roll(u, s, axis=0), 0.0)
            u = a * u_sh + u
            a = a * a_sh
        h = a * carry_ref[...] + u
        out_ref[pl.ds(r0, SUBLANES), :] = h
        last = h[0:1, :] if reverse else h[SUBLANES - 1:SUBLANES, :]
        carry_ref[...] = jnp.broadcast_to(last, (SUBLANES, w))
        return 0

    lax.fori_loop(0, ntiles, step, 0)


def _lru_gates(xh, wa, wx, ba, bx, lam):
    r = _sigmoid(_bdot(xh, wa) + ba)
    i = _sigmoid(_bdot(xh, wx) + bx)
    sp = _softplus(-lam)
    a = jnp.exp(-LRU_C * r * sp)
    return r, i, sp, a


def _lru_fwd(xc, wa, wx, ba, bx, lam, h0, addend, reverse, name, rider=None):
    t, w = xc.shape
    tb = LRU_BLOCK
    nb, nh = t // tb, w // LRU_HEAD_DIM
    bi = (lambda k: nb - 1 - k) if reverse else (lambda k: k)
    has_add = addend is not None

    def body(*refs):
        xc_ref, wa_ref, wx_ref, ba_ref, bx_ref, lam_ref, h0_ref = refs[:7]
        add_ref = refs[7] if has_add else None
        h_ref, hl_ref = refs[7 + has_add], refs[8 + has_add]
        sum_ref = refs[9 + has_add] if has_add else None
        a_s, u_s, carry = refs[-3:]

        @pl.when(pl.program_id(0) == 0)
        def _():
            carry[...] = jnp.broadcast_to(h0_ref[...], carry.shape)

        for hd in range(nh):
            hs = slice(hd * LRU_HEAD_DIM, (hd + 1) * LRU_HEAD_DIM)
            xh = xc_ref[:, hs]
            _, i, _, a = _lru_gates(xh, wa_ref[hd], wx_ref[hd], ba_ref[:, hs], bx_ref[:, hs], lam_ref[:, hs])
            a_s[:, hs] = a
            u_s[:, hs] = jnp.sqrt(1.0 - a * a) * (i * xh)
        _tile_scan(a_s, u_s, h_ref, carry, tb // SUBLANES, reverse)
        hl_ref[...] = carry[0:1, :]
        if has_add:
            sum_ref[...] = h_ref[...] + add_ref[...]

    blk = pl.BlockSpec((tb, w), lambda k: (bi(k), 0))
    vec = pl.BlockSpec((1, w), lambda k: (0, 0))
    wsp = pl.BlockSpec((nh, LRU_HEAD_DIM, LRU_HEAD_DIM), lambda k: (0, 0, 0))
    out_specs = [blk, vec] + ([blk] if has_add else [])
    out_shape = [jax.ShapeDtypeStruct((t, w), F32), jax.ShapeDtypeStruct((1, w), F32)] + ([jax.ShapeDtypeStruct((t, w), F32)] if has_add else [])
    return _call(
        body, name, (nb,), [blk, wsp, wsp, vec, vec, vec, vec] + ([blk] if has_add else []), out_specs, out_shape,
        [pltpu.VMEM((tb, w), F32), pltpu.VMEM((tb, w), F32), pltpu.VMEM((SUBLANES, w), F32)], ("arbitrary",),
        (xc, wa, wx, ba, bx, lam, h0, *([addend] if has_add else [])), rider)


def _lru_bwd(xc, wa, wx, ba, bx, lam, h0, h, dh, dfin, addend, reverse, name, rider=None):
    t, w = xc.shape
    tb = LRU_BLOCK
    nb, nh = t // tb, w // LRU_HEAD_DIM
    r8 = tb // SUBLANES
    bi = (lambda k: k) if reverse else (lambda k: nb - 1 - k)
    if reverse:
        halo = lambda k: (jnp.minimum((bi(k) + 1) * r8, t // SUBLANES - 1), 0)
    else:
        halo = lambda k: (jnp.maximum(bi(k) * r8 - 1, 0), 0)
    has_add = addend is not None
    n_ext = tb + SUBLANES

    def body(*refs):
        xc_ref, wa_ref, wx_ref, ba_ref, bx_ref, lam_ref, h0_ref, h_ref, halo_ref, dh_ref, dfin_ref = refs[:11]
        add_ref = refs[11] if has_add else None
        dxc_ref, dwa_ref, dwx_ref, dba_ref, dbx_ref, dsp_ref, dh0_ref = refs[11 + has_add:18 + has_add]
        a_s, v_s, mu_s, r_s, i_s, carry = refs[-6:]
        k = pl.program_id(0)

        @pl.when(k == 0)
        def _():
            carry[...] = jnp.broadcast_to(dfin_ref[...], carry.shape)
            for r in (dwa_ref, dwx_ref, dba_ref, dbx_ref, dsp_ref):
                r[...] = jnp.zeros_like(r)

        for hd in range(nh):
            hs = slice(hd * LRU_HEAD_DIM, (hd + 1) * LRU_HEAD_DIM)
            r, i, _, a = _lru_gates(xc_ref[:, hs], wa_ref[hd], wx_ref[hd], ba_ref[:, hs], bx_ref[:, hs], lam_ref[:, hs])
            a_s[:, hs] = a
            r_s[:, hs] = r
            i_s[:, hs] = i
            v_s[:, hs] = a * dh_ref[:, hs]
        mu_in = carry[...]
        _tile_scan(a_s, v_s, mu_s, carry, r8, not reverse)
        dh0_ref[...] = carry[0:1, :]
        first = (bi(k) == nb - 1) if reverse else (bi(k) == 0)
        edge = jnp.where(first, jnp.broadcast_to(h0_ref[...], (SUBLANES, w)), halo_ref[...])
        for hd in range(nh):
            hs = slice(hd * LRU_HEAD_DIM, (hd + 1) * LRU_HEAD_DIM)
            if reverse:
                mu_sh = pltpu.roll(jnp.concatenate([mu_in[:, hs], mu_s[:, hs]], axis=0), 1, axis=0)[SUBLANES:]
                h_sh = pltpu.roll(jnp.concatenate([h_ref[:, hs], edge[:, hs]], axis=0), n_ext - 1, axis=0)[:tb]
            else:
                mu_sh = pltpu.roll(jnp.concatenate([mu_s[:, hs], mu_in[:, hs]], axis=0), n_ext - 1, axis=0)[:tb]
                h_sh = pltpu.roll(jnp.concatenate([edge[:, hs], h_ref[:, hs]], axis=0), 1, axis=0)[SUBLANES:]
            xh = xc_ref[:, hs]
            r, i, a = r_s[:, hs], i_s[:, hs], a_s[:, hs]
            sp = _softplus(-lam_ref[:, hs])
            om = 1.0 - a * a
            rs = lax.rsqrt(om)
            lam_t = dh_ref[:, hs] + mu_sh
            dla = lam_t * h_sh * a - lam_t * (i * xh) * (a * a) * rs
            dix = lam_t * (om * rs)
            dpr = dla * (-LRU_C * sp) * r * (1.0 - r)
            dpi = dix * xh * i * (1.0 - i)
            dsp_ref[:, hs] += _colsum(dla * (-LRU_C * r))
            dba_ref[:, hs] += _colsum(dpr)
            dbx_ref[:, hs] += _colsum(dpi)
            dx = dix * i + _bdot_nt(dpr, wa_ref[hd]) + _bdot_nt(dpi, wx_ref[hd])
            xt = xh.T
            dwa_ref[hd] += _bdot(xt, dpr)
            dwx_ref[hd] += _bdot(xt, dpi)
            dxc_ref[:, hs] = dx + add_ref[:, hs] if has_add else dx

    blk = pl.BlockSpec((tb, w), lambda k: (bi(k), 0))
    vec = pl.BlockSpec((1, w), lambda k: (0, 0))
    wsp = pl.BlockSpec((nh, LRU_HEAD_DIM, LRU_HEAD_DIM), lambda k: (0, 0, 0))
    wshape = jax.ShapeDtypeStruct((nh, LRU_HEAD_DIM, LRU_HEAD_DIM), F32)
    vshape = jax.ShapeDtypeStruct((1, w), F32)
    return _call(
        body, name, (nb,),
        [blk, wsp, wsp, vec, vec, vec, vec, blk, pl.BlockSpec((SUBLANES, w), halo), blk, vec] + ([blk] if has_add else []),
        [blk, wsp, wsp, vec, vec, vec, vec],
        [jax.ShapeDtypeStruct((t, w), F32), wshape, wshape, vshape, vshape, vshape, vshape],
        [pltpu.VMEM((tb, w), F32)] * 5 + [pltpu.VMEM((SUBLANES, w), F32)], ("arbitrary",),
        (xc, wa, wx, ba, bx, lam, h0, h, h, dh, dfin, *([addend] if has_add else [])), rider)


def _coords():
    return lax.axis_index("x"), lax.axis_index("y"), lax.axis_index("c")


HALF_ROWS_ALIGN = 32


def _all_gather(shards, name):
    n = len(shards)
    start, wait, sems = _gather_rounds(shards)

    def body(*refs):
        for rnd in range(3):
            start(rnd, refs[:n], refs[n:2 * n], refs[2 * n:])
            wait(rnd, refs[:n], refs[n:2 * n], refs[2 * n:])

    any_spec = pl.BlockSpec(memory_space=pl.ANY)
    return pl.pallas_call(
        body, name=name, in_specs=[any_spec] * n, out_specs=[any_spec] * n,
        out_shape=[jax.ShapeDtypeStruct((N_DEV,) + s.shape, s.dtype) for s in shards], scratch_shapes=sems,
    )(*shards)


def _gather_riders(shards):
    n = len(shards)
    start, wait, sems = _gather_rounds(shards)
    out_shape = [jax.ShapeDtypeStruct((N_DEV,) + s.shape, s.dtype) for s in shards]

    def make(rnd, bufs=()):
        return _Rider(list(shards) + list(bufs), out_shape, {n + i: i for i in range(len(bufs))}, sems,
                      lambda r_in, r_out, s: start(rnd, r_in[:n], r_out, s), lambda r_in, r_out, s: wait(rnd, r_in[:n], r_out, s))
    return make


def _gather_rounds(shards):
    n = len(shards)
    split = [s.ndim >= 2 and s.shape[0] % HALF_ROWS_ALIGN == 0 for s in shards]
    n_copies = 10

    def plan(ins, outs, sem_refs):
        send_sems, recv_sems, local_sems = sem_refs
        x, y, c = _coords()
        slot = lambda p: 4 * p[0] + 2 * p[1] + p[2]
        me, xp, yp, dg = (x, y, c), (1 - x, y, c), (x, 1 - y, c), (1 - x, 1 - y, c)
        other_core = lambda p: (p[0], p[1], 1 - p[2])

        def rows(ref, i, half):
            r = shards[i].shape[0]
            if half == "all" or not split[i]:
                return ref
            return ref.at[pl.ds(0, r // 2)] if half == "a" else ref.at[pl.ds(r // 2, r // 2)]

        def copy(i, j, holder, half, to, own=False):
            dst = rows(outs[i].at[slot(holder)], i, half)
            src = rows(ins[i], i, half) if own else dst
            return pltpu.make_async_remote_copy(src_ref=src, dst_ref=dst, send_sem=send_sems.at[i, j],
                                                recv_sem=recv_sems.at[i, j], device_id=to, device_id_type=MESH)

        rounds = [
            ([(0, me, "a", xp, True), (1, me, "b", yp, True)],
             [(0, xp, "a"), (1, yp, "b")]),
            ([(2, me, "a", yp, True), (3, xp, "a", yp, False), (4, me, "b", xp, True), (5, yp, "b", xp, False)],
             [(2, yp, "a"), (3, dg, "a"), (4, xp, "b"), (5, dg, "b")]),
            ([(6, me, "all", other_core(me), True)] + [(7 + q, p, "all", other_core(me), False) for q, p in enumerate((xp, yp, dg))],
             [(6 + q, other_core(p), "all") for q, p in enumerate((me, xp, yp, dg))]),
        ]
        local = lambda: [pltpu.make_async_copy(ins[i], outs[i].at[slot(me)], local_sems.at[i]) for i in range(n)]
        travels = lambda i, half: half != "b" or split[i]
        return rounds, copy, local, travels, me

    def start(rnd, ins, outs, sem_refs):
        rounds, copy, local, travels, _ = plan(ins, outs, sem_refs)
        if rnd == 0:
            for cp in local():
                cp.start()
        for j, holder, half, to, own in rounds[rnd][0]:
            for i in range(n):
                if travels(i, half):
                    copy(i, j, holder, half, to, own).start()

    def wait(rnd, ins, outs, sem_refs):
        rounds, copy, local, travels, me = plan(ins, outs, sem_refs)
        for j, holder, half in rounds[rnd][1]:
            for i in range(n):
                if travels(i, half):
                    copy(i, j, holder, half, me).wait_recv()
        for j, holder, half, to, own in rounds[rnd][0]:
            for i in range(n):
                if travels(i, half):
                    copy(i, j, holder, half, to, own).wait_send()
        if rnd == 0:
            for cp in local():
                cp.wait()

    sems = [pltpu.SemaphoreType.DMA((n, n_copies)), pltpu.SemaphoreType.DMA((n, n_copies)), pltpu.SemaphoreType.DMA((n,))]
    return start, wait, sems


def _swap_rider(bufs, axes):
    n = len(bufs)

    def copies(ins, outs, sem_refs):
        x, y, c = _coords()
        to = {"x": (1 - x, y, c), "y": (x, 1 - y, c), "c": (x, y, 1 - c)}
        return [pltpu.make_async_remote_copy(src_ref=ins[i], dst_ref=outs[i], send_sem=sem_refs[0].at[i], recv_sem=sem_refs[1].at[i],
                                             device_id=to[axes[i]], device_id_type=MESH) for i in range(n)]

    def start(ins, outs, sem_refs):
        for cp in copies(ins, outs, sem_refs):
            cp.start()

    def wait(ins, outs, sem_refs):
        for cp in copies(ins, outs, sem_refs):
            cp.wait()

    return _Rider(list(bufs), [jax.ShapeDtypeStruct(b.shape, b.dtype) for b in bufs], {},
                  [pltpu.SemaphoreType.DMA((n,)), pltpu.SemaphoreType.DMA((n,))], start, wait)


def _swap(bufs, axes, name):
    n = len(bufs)
    rider = _swap_rider(bufs, axes)

    def body(*refs):
        rider.start(refs[:n], refs[n:2 * n], refs[2 * n:])
        rider.wait(refs[:n], refs[n:2 * n], refs[2 * n:])

    any_spec = pl.BlockSpec(memory_space=pl.ANY)
    return pl.pallas_call(body, name=name, in_specs=[any_spec] * n, out_specs=[any_spec] * n, out_shape=rider.out_shape,
                          scratch_shapes=rider.sems)(*bufs)


def _as2d(a):
    return a.reshape(-1, a.shape[-1])


def _add_f32(a, b, name):
    (out,), _ = _rowwise(lambda r, s: ([r[0].astype(F32) + r[1].astype(F32)], []), [_as2d(a), _as2d(b)], [],
                         [(a.shape[-1], F32)], [], name)
    return out.reshape(a.shape)


class _ReduceScatter:
    def __init__(self, parts, name):
        self.n, self.name, self.rnd = len(parts), name, 0
        self.cur = [p.reshape((2, 2, 2) + p.shape[1:]) for p in parts]

    def stage(self):
        x, y, c = _coords()
        n = self.n
        if self.rnd == 0:
            bufs, idxs, dims, axes = self.cur, [c] * n, [2] * n, ["c"] * n
        elif self.rnd == 1:
            bufs = [p[:, :, :p.shape[2] // 2] for p in self.cur] + [p[:, :, p.shape[2] // 2:] for p in self.cur]
            idxs, dims, axes = [y] * n + [x] * n, [1] * n + [0] * n, ["y"] * n + ["x"] * n
        else:
            bufs, idxs, dims, axes = self.cur, [x] * n + [y] * n, [0] * (2 * n), ["x"] * n + ["y"] * n
        pick = lambda p, idx, dim: lax.dynamic_index_in_dim(p, idx, dim, keepdims=False)
        self.keep = [pick(p, i_, d_) for p, i_, d_ in zip(bufs, idxs, dims)]
        return [pick(p, 1 - i_, d_).astype(BF16) for p, i_, d_ in zip(bufs, idxs, dims)], axes

    def absorb(self, got):
        self.cur = [_add_f32(k, g, f"{self.name}_add{self.rnd}_{i}") for i, (k, g) in enumerate(zip(self.keep, got))]
        self.rnd += 1

    def result(self):
        return [jnp.concatenate([self.cur[i], self.cur[self.n + i]], axis=0) for i in range(self.n)]


def _reduce_scatter(parts, name):
    rs = _ReduceScatter(parts, name)
    for rnd in range(3):
        send, axes = rs.stage()
        rs.absorb(_swap(send, axes, f"{name}_swap{rnd}"))
    return rs.result()


def _pack(arrs):
    flat = jnp.concatenate([a.reshape(-1).astype(F32) for a in arrs])
    rows = -(-flat.shape[0] // LANES)
    rows = -(-rows // 512) * 512
    return jnp.pad(flat, (0, rows * LANES - flat.shape[0])).reshape(rows, LANES)


def _unpack(buf, like):
    flat, out, off = buf.reshape(-1), [], 0
    for a in like:
        out.append(flat[off:off + a.size].reshape(a.shape))
        off += a.size
    return out


class _AllReduce:
    def __init__(self, arrs, name, narrow=()):
        self.n, self.name, self.narrow = len(arrs), name, narrow
        self.wide = [a for i, a in enumerate(arrs) if i not in narrow]
        self.thin = [a for i, a in enumerate(arrs) if i in narrow]
        self.packs = [_pack(self.wide)] + ([_pack(self.thin).astype(BF16)] if self.thin else [])
        self.make = _gather_riders(self.packs)

    def finish(self, gathered):
        sums = []
        for j, g in enumerate(gathered):
            (tot,), _ = _rowwise(lambda r, s: ([functools.reduce(lambda u, v: u + v, r)], []), [g[i] for i in range(N_DEV)], [],
                                 [(LANES, F32)], [], f"{self.name}_sum{j}")
            sums.append(tot)
        out_wide = iter(_unpack(sums[0], self.wide))
        out_thin = iter(_unpack(sums[1], self.thin) if self.thin else [])
        return [next(out_thin) if i in self.narrow else next(out_wide) for i in range(self.n)]


def _all_reduce(arrs, name, narrow=()):
    ar = _AllReduce(arrs, name, narrow)
    return ar.finish(_all_gather(ar.packs, name + "_gather"))


def _adamw(w, g, m, v, name):
    def fn(r, s):
        w_, g_, m_, v_ = r
        m_ = ADAM_B1 * m_ + (1.0 - ADAM_B1) * g_
        v_ = ADAM_B2 * v_ + (1.0 - ADAM_B2) * jnp.square(g_)
        m_hat = m_ / (1.0 - ADAM_B1 ** ADAM_STEP)
        v_hat = v_ / (1.0 - ADAM_B2 ** ADAM_STEP)
        return [-ADAM_LR * (m_hat / (jnp.sqrt(v_hat) + ADAM_EPS) + ADAM_WD * w_), m_, v_], []

    c = w.shape[-1]
    (d, nm, nv), _ = _rowwise(fn, [_as2d(a) for a in (w, g, m, v)], [], [(c, F32)] * 3, [], name)
    return d.reshape(w.shape), nm.reshape(w.shape), nv.reshape(w.shape)


def _norm_mod(x, gain, sc, sh, name):
    def fn(r, s):
        xv = r[0]
        xn = xv * lax.rsqrt(_rowmean(xv * xv) + EPS)
        return [xn * s[0] * (1.0 + s[1]) + s[2]], []
    (h,), _ = _rowwise(fn, [x], [gain, sc, sh], [(x.shape[1], BF16)], [], name)
    return h


def _norm_mod_bwd(x, dh, dres, gain, sc, name):
    d = x.shape[1]

    def fn(r, s):
        xv, dhv = r[0], r[1]
        rs = lax.rsqrt(_rowmean(xv * xv) + EPS)
        xn = xv * rs
        dxn = dhv * (1.0 + s[1]) * s[0]
        dx = rs * (dxn - xn * _rowmean(dxn * xn))
        if dres is not None:
            dx = dx + r[2]
        return [dx], [_colsum(dhv), _colsum(dhv * xn * s[0]), _colsum(dhv * (1.0 + s[1]) * xn)]
    (dx,), accs = _rowwise(fn, [x, dh] + ([dres] if dres is not None else []), [gain, sc], [(d, F32)], [d, d, d], name)
    return dx, accs


def _ada_rows(mod_row, d):
    return [mod_row[:, i * d:(i + 1) * d] for i in range(N_MOD)]


def _to_col_major(t):
    n, c = t.shape
    return t.reshape(n // GRID_W, GRID_W, c).transpose(1, 0, 2).reshape(n, c)


def _to_row_major(t):
    n, c = t.shape
    return t.reshape(GRID_W, n // GRID_W, c).transpose(1, 0, 2).reshape(n, c)


def kernel(x, c, ctx, c_ctx, w_ada, b_ada, norm_mix, norm_ffn, w_in, ssd_conv_w, ssd_conv_b, ssd_dt_bias, ssd_a_log, ssd_d, ssd_norm, w_out_ssd, lru_conv_w, lru_conv_b, lru_w_a, lru_b_a, lru_w_x, lru_b_x, lru_lambda, w_out_lru, w_gate, b_gate, w_o, ffn_w13, ffn_w2, final_norm, loss_target, m_c_ctx, m_w_ada, m_b_ada, m_norm_mix, m_norm_ffn, m_w_in, m_ssd_conv_w, m_ssd_conv_b, m_ssd_dt_bias, m_ssd_a_log, m_ssd_d, m_ssd_norm, m_w_out_ssd, m_lru_conv_w, m_lru_conv_b, m_lru_w_a, m_lru_b_a, m_lru_w_x, m_lru_b_x, m_lru_lambda, m_w_out_lru, m_w_gate, m_b_gate, m_w_o, m_ffn_w13, m_ffn_w2, m_final_norm, v_c_ctx, v_w_ada, v_b_ada, v_norm_mix, v_norm_ffn, v_w_in, v_ssd_conv_w, v_ssd_conv_b, v_ssd_dt_bias, v_ssd_a_log, v_ssd_d, v_ssd_norm, v_w_out_ssd, v_lru_conv_w, v_lru_conv_b, v_lru_w_a, v_lru_b_a, v_lru_w_x, v_lru_b_x, v_lru_lambda, v_w_out_lru, v_w_gate, v_b_gate, v_w_o, v_ffn_w13, v_ffn_w2, v_final_norm):
    names = ["c_ctx", "w_ada", "b_ada", "norm_mix", "norm_ffn", "w_in", "ssd_conv_w", "ssd_conv_b", "ssd_dt_bias", "ssd_a_log",
             "ssd_d", "ssd_norm", "w_out_ssd", "lru_conv_w", "lru_conv_b", "lru_w_a", "lru_b_a", "lru_w_x", "lru_b_x", "lru_lambda",
             "w_out_lru", "w_gate", "b_gate", "w_o", "ffn_w13", "ffn_w2", "final_norm"]
    wts = dict(zip(names, (c_ctx, w_ada, b_ada, norm_mix, norm_ffn, w_in, ssd_conv_w, ssd_conv_b, ssd_dt_bias, ssd_a_log, ssd_d, ssd_norm, w_out_ssd, lru_conv_w, lru_conv_b, lru_w_a, lru_b_a, lru_w_x, lru_b_x, lru_lambda, w_out_lru, w_gate, b_gate, w_o, ffn_w13, ffn_w2, final_norm)))
    mom1 = dict(zip(names, (m_c_ctx, m_w_ada, m_b_ada, m_norm_mix, m_norm_ffn, m_w_in, m_ssd_conv_w, m_ssd_conv_b, m_ssd_dt_bias, m_ssd_a_log, m_ssd_d, m_ssd_norm, m_w_out_ssd, m_lru_conv_w, m_lru_conv_b, m_lru_w_a, m_lru_b_a, m_lru_w_x, m_lru_b_x, m_lru_lambda, m_w_out_lru, m_w_gate, m_b_gate, m_w_o, m_ffn_w13, m_ffn_w2, m_final_norm)))
    mom2 = dict(zip(names, (v_c_ctx, v_w_ada, v_b_ada, v_norm_mix, v_norm_ffn, v_w_in, v_ssd_conv_w, v_ssd_conv_b, v_ssd_dt_bias, v_ssd_a_log, v_ssd_d, v_ssd_norm, v_w_out_ssd, v_lru_conv_w, v_lru_conv_b, v_lru_w_a, v_lru_b_a, v_lru_w_x, v_lru_b_x, v_lru_lambda, v_w_out_lru, v_w_gate, v_b_gate, v_w_o, v_ffn_w13, v_ffn_w2, v_final_norm)))

    xi, yi, ci_ = _coords()
    me = 4 * xi + 2 * yi + ci_
    x2, tgt, ctx2 = x[0], loss_target[0], ctx[0]
    t, d = x2.shape
    tc = ctx2.shape[0]
    ws = N_DEV * w_out_ssd.shape[1]
    wl = N_DEV * w_out_lru.shape[1]
    heads = ws // SSD_HEAD_DIM
    gn = SSD_GROUPS * SSD_STATE
    nh = wl // LRU_HEAD_DIM
    ff = N_DEV * ffn_w2.shape[1]
    ada_cols = w_ada.shape[2]

    smalls = [c, ssd_conv_w[0], lru_conv_w[0], lru_b_a[0], lru_b_x[0], lru_lambda[0]]
    got = _all_gather(smalls + [w_in[0].astype(BF16), w_gate[0].astype(BF16)], "gather_weights")
    late_a = _gather_riders([ffn_w13[0].astype(BF16)])
    late_b = _gather_riders([w.astype(BF16) for w in (ffn_w2[0], w_out_ssd[0], w_out_lru[0], w_o[0])])
    c_all = got[0].reshape(N_DEV, d)
    colcat = lambda g: jnp.moveaxis(g, 0, -2).reshape(g.shape[1:-1] + (N_DEV * g.shape[-1],))
    conv_w_s, conv_w_l, lru_ba, lru_bx, lru_lam = (colcat(g) for g in got[1:6])
    wg_in, wg_gate = got[6:]
    w_in_f = colcat(wg_in)
    o_z, o_xbc, o_dt, o_xr, o_yr = 0, ws, 2 * ws + 2 * gn, 2 * ws + 2 * gn + 2 * heads, 2 * ws + 2 * gn + 2 * heads + wl
    w_z, w_xbc = w_in_f[:, o_z:o_xbc], w_in_f[:, o_xbc:o_dt]
    w_dt = jnp.pad(w_in_f[:, o_dt:o_xr], ((0, 0), (0, LANES - 2 * heads)))
    w_xr, w_yr = w_in_f[:, o_xr:o_yr], w_in_f[:, o_yr:]
    w_gf = colcat(wg_gate)
    w_gs, w_gr = w_gf[:, :d], w_gf[:, d:]

    cond = jnp.concatenate([c_all, c_ctx[None, :], jnp.zeros((16 - N_DEV - 1, d), F32)], axis=0)
    cond_act = _silu(cond)
    mod_l = _mm([(cond_act, w_ada[0])], "nn", F32, "ada_fwd")
    (mod_g,) = _all_gather([mod_l], "gather_mod")
    mod_full = colcat(mod_g) + b_ada
    sh_m, sc_m, g_m, sh_f, sc_f, g_f = _ada_rows(lax.dynamic_slice_in_dim(mod_full, me, 1, 0), d)
    csh_m, csc_m = _ada_rows(mod_full[N_DEV:N_DEV + 1], d)[:2]

    bias_s, alog_s = ssd_dt_bias[0], ssd_a_log[0]
    d_full = jnp.repeat(ssd_d[0], SSD_HEAD_DIM)[None, :]
    zeros_state = jnp.zeros((heads // 2, SSD_STATE, LANES), F32)
    zeros_l = jnp.zeros((1, wl), F32)
    wa, wx = lru_w_a[0], lru_w_x[0]

    def mixer_inputs(hh, tag, full):
        xbc = _mm([(hh, w_xbc)], "nn", F32, f"{tag}_proj_xbc")
        dtr = _mm([(hh, w_dt)], "nn", F32, f"{tag}_proj_dt")[:, :2 * heads]
        xr = _mm([(hh, w_xr)], "nn", F32, f"{tag}_proj_xr")
        cx = _conv(xbc, conv_w_s, ssd_conv_b, CONV_LEFT, True, f"{tag}_ssd_conv")
        xr_seq = _to_col_major(xr) if full else xr
        xc = _conv(xr_seq, conv_w_l, lru_conv_b, CONV_LEFT, False, f"{tag}_lru_conv")
        return xbc, dtr, xr_seq, cx, xc

    def lru_args(dirn):
        return wa[dirn], wx[dirn], lru_ba[dirn:dirn + 1], lru_bx[dirn:dirn + 1], lru_lam[dirn:dirn + 1]

    hc = _norm_mod(ctx2, norm_mix, csc_m, csh_m, "ctx_norm")
    c_xbc, c_dtr, c_xr_seq, c_cx, c_xc = mixer_inputs(hc, "ctx", False)
    c_ssd = []
    for dirn in (0, 1):
        _, se, sf = _ssd_fwd(c_cx, c_dtr[:, dirn * heads:(dirn + 1) * heads], bias_s[dirn:dirn + 1], alog_s[dirn:dirn + 1],
                             zeros_state, bool(dirn), f"ctx_ssd_fwd{dirn}")
        c_ssd.append((se, sf))
    c_lru = []
    for dirn in (0, 1):
        res = _lru_fwd(c_xc, *lru_args(dirn), zeros_l, None, bool(dirn), f"ctx_lru_fwd{dirn}")
        c_lru.append((res[0], res[1]))

    h = _norm_mod(x2, norm_mix, sc_m, sh_m, "lat_norm")
    z = _mm([(h, w_z)], "nn", BF16, "lat_proj_z")
    yr = _mm([(h, w_yr)], "nn", BF16, "lat_proj_yr")
    gp_s = _mm([(h, w_gs)], "nn", BF16, "lat_proj_gs")
    gp_r = _mm([(h, w_gr)], "nn", BF16, "lat_proj_gr")
    l_xbc, l_dtr, l_xr_seq, l_cx, l_xc = mixer_inputs(h, "lat", True)
    ssd_args = lambda dirn: (l_cx, l_dtr[:, dirn * heads:(dirn + 1) * heads], bias_s[dirn:dirn + 1], alog_s[dirn:dirn + 1], c_ssd[dirn][1], bool(dirn))
    (yv0, se0, _), late = _ssd_fwd(*ssd_args(0), "lat_ssd_fwd0", rider=_merge_riders(late_a(0), late_b(0)))
    got_a, got_b = late[:1], late[1:]
    (yv1, se1, _), got_a = _ssd_fwd(*ssd_args(1), "lat_ssd_fwd1", rider=late_a(1, got_a))
    l_ssd = [(yv0, se0), (yv1, se1)]
    (hf, _), late = _lru_fwd(l_xc, *lru_args(0), c_lru[0][1], None, False, "lat_lru_fwd0", rider=_merge_riders(late_a(2, got_a), late_b(1, got_b)))
    (wg_13,), got_b = late[:1], late[1:]
    (hb, _, r_seq), (wg_2, wg_os, wg_ol, wg_o) = _lru_fwd(l_xc, *lru_args(1), c_lru[1][1], hf, True, "lat_lru_fwd1", rider=late_b(2, got_b))
    w_os, w_ol, w_oo, w_2 = wg_os.reshape(ws, d), wg_ol.reshape(wl, d), wg_o.reshape(d, d), wg_2.reshape(ff, d)
    w_13f = colcat(wg_13)
    w_1, w_3 = w_13f[:, :ff], w_13f[:, ff:]

    def ssd_out(r, s):
        yf, yb, cx_, z_ = r
        yv = yf + yb + s[0] * cx_[:, :ws]
        yz = yv * _silu(z_)
        return [yz * lax.rsqrt(_rowmean(yz * yz) + EPS) * s[1]], []
    (ys,), _ = _rowwise(ssd_out, [l_ssd[0][0], l_ssd[1][0], l_cx, z], [d_full, ssd_norm], [(ws, BF16)], [], "lat_ssd_out")
    o_s = _mm([(ys, w_os)], "nn", BF16, "lat_out_ssd")

    r_out = _to_row_major(r_seq)
    (gr,), _ = _rowwise(lambda r, s: ([r[0] * _gelu(r[1])], []), [r_out, yr], [], [(wl, BF16)], [], "lat_lru_out")
    o_r = _mm([(gr, w_ol)], "nn", BF16, "lat_out_lru")

    bg_s, bg_r = b_gate[:, :d], b_gate[:, d:]
    (mix,), _ = _rowwise(lambda r, s: ([_sigmoid(r[0] + s[0]) * r[2] + _sigmoid(r[1] + s[1]) * r[3]], []),
                         [gp_s, gp_r, o_s, o_r], [bg_s, bg_r], [(d, BF16)], [], "lat_merge")
    mixo = _mm([(mix, w_oo)], "nn", F32, "lat_out_o")

    def resid_norm(r, s):
        x1v = r[0] + s[0] * r[1]
        xn = x1v * lax.rsqrt(_rowmean(x1v * x1v) + EPS)
        return [x1v, xn * s[1] * (1.0 + s[2]) + s[3]], []
    (x1, h2), _ = _rowwise(resid_norm, [x2, mixo], [g_m, norm_ffn, sc_f, sh_f], [(d, F32), (d, BF16)], [], "lat_resid1")
    fg = _mm([(h2, w_1)], "nn", BF16, "ffn_gate")
    fu = _mm([(h2, w_3)], "nn", BF16, "ffn_up")
    (act,), _ = _rowwise(lambda r, s: ([_silu(r[0]) * r[1]], []), [fg, fu], [], [(ff, BF16)], [], "ffn_act")
    fo = _mm([(act, w_2)], "nn", F32, "ffn_down")

    fn_gain = final_norm[None, :]

    def head(r, s):
        x1v, fv, tv = r
        xv = x1v + s[0] * fv
        rs = lax.rsqrt(_rowmean(xv * xv) + EPS)
        xn = xv * rs
        err = xn * s[1] - tv
        dyv = err * (1.0 / d)
        dxn = dyv * s[1]
        dxv = rs * (dxn - xn * _rowmean(dxn * xn))
        return [dxv, dxv * s[0]], [_colsum(err * err), _colsum(dyv * xn), _colsum(dxv * fv)]
    (dx2, dfo), (sq, g_final, dg_f) = _rowwise(head, [x1, fo, tgt], [g_f, fn_gain], [(d, F32), (d, BF16)], [d, d, d], "loss_head")
    loss = lax.psum(0.5 * jnp.sum(sq) / d, ("x", "y", "c"))

    dact = _mm([(dfo, w_2)], "nt", BF16, "ffn_down_dx")
    gw_2 = _mm([(act, dfo)], "tn", F32, "ffn_down_dw")
    (dfg, dfu), _ = _rowwise(lambda r, s: ([r[0] * r[2] * _dsilu(r[1]), r[0] * _silu(r[1])], []), [dact, fg, fu], [],
                             [(ff, BF16), (ff, BF16)], [], "ffn_act_bwd")
    gw_13 = _mm([(h2, dfg)], "tn", F32, "ffn_gate_dw", slots=(ff // 4, N_DEV, 0, None))
    gw_13 = _mm([(h2, dfu)], "tn", F32, "ffn_up_dw", slots=(ff // 4, N_DEV, N_DEV // 2, gw_13))
    slots = lambda g: jnp.moveaxis(g.reshape(g.shape[0], N_DEV, g.shape[1] // N_DEV), 1, 0)
    rslots = lambda g: g.reshape(N_DEV, g.shape[0] // N_DEV, g.shape[1])
    ffn_rs = _ReduceScatter([gw_13, rslots(gw_2)], "ffn_rs")
    dh2, got = _mm([(dfg, w_1), (dfu, w_3)], "nt", F32, "ffn_dx", rider=_swap_rider(*ffn_rs.stage()))
    ffn_rs.absorb(got)
    dx1, (dsh_f, dsc_f, g_norm_ffn) = _norm_mod_bwd(x1, dh2, dx2, norm_ffn, sc_f, "lat_norm_ffn_bwd")

    (dmixo,), (dg_m,) = _rowwise(lambda r, s: ([s[0] * r[0]], [_colsum(r[0] * r[1])]), [dx1, mixo], [g_m], [(d, BF16)], [d], "lat_resid1_bwd")
    dmix = _mm([(dmixo, w_oo)], "nt", BF16, "lat_out_o_dx")
    gw_o = _mm([(mix, dmixo)], "tn", F32, "lat_out_o_dw")

    def merge_bwd(r, s):
        dm, gps, gpr, os_, or_ = r
        g1, g2 = _sigmoid(gps + s[0]), _sigmoid(gpr + s[1])
        dg1, dg2 = dm * os_ * g1 * (1.0 - g1), dm * or_ * g2 * (1.0 - g2)
        return [dm * g1, dm * g2, dg1, dg2], [_colsum(dg1), _colsum(dg2)]
    (do_s, do_r, dgp_s, dgp_r), (gb_gs, gb_gr) = _rowwise(merge_bwd, [dmix, gp_s, gp_r, o_s, o_r], [bg_s, bg_r],
                                                            [(d, BF16)] * 4, [d, d], "lat_merge_bwd")
    gw_gate = _mm([(h, dgp_s)], "tn", F32, "lat_gs_dw", slots=(d // 4, N_DEV, 0, None))
    gw_gate = _mm([(h, dgp_r)], "tn", F32, "lat_gr_dw", slots=(d // 4, N_DEV, N_DEV // 2, gw_gate))

    dgr = _mm([(do_r, w_ol)], "nt", BF16, "lat_out_lru_dx")
    gw_ol = _mm([(gr, do_r)], "tn", F32, "lat_out_lru_dw")
    (dr_out, dyr), _ = _rowwise(lambda r, s: ([r[0] * _gelu(r[2]), r[0] * r[1] * _dgelu(r[2])], []), [dgr, r_out, yr], [],
                                [(wl, F32), (wl, BF16)], [], "lat_lru_out_bwd")
    dr_seq = _to_col_major(dr_out)
    dys = _mm([(do_s, w_os)], "nt", BF16, "lat_out_ssd_dx")
    gw_os = _mm([(ys, do_s)], "tn", F32, "lat_out_ssd_dw")
    mid_rs = _ReduceScatter([rslots(gw_os), rslots(gw_ol), gw_gate, rslots(gw_o)], "mid_rs")
    n_ffn = 2 * ffn_rs.n
    lb0, got = _lru_bwd(l_xc, *lru_args(0), c_lru[0][1], hf, dr_seq, zeros_l, None, False, "lat_lru_bwd0",
                        rider=_merge_riders(_swap_rider(*ffn_rs.stage()), _swap_rider(*mid_rs.stage())))
    ffn_rs.absorb(got[:n_ffn])
    mid_rs.absorb(got[n_ffn:])
    lb1, got = _lru_bwd(l_xc, *lru_args(1), c_lru[1][1], hb, dr_seq, zeros_l, lb0[0], True, "lat_lru_bwd1",
                        rider=_merge_riders(_swap_rider(*ffn_rs.stage()), _swap_rider(*mid_rs.stage())))
    ffn_rs.absorb(got[:n_ffn])
    mid_rs.absorb(got[n_ffn:])
    dxr_seq, g_lconv_w, g_lconv_b = _conv_bwd(l_xr_seq, [lb1[0]], lambda v: v[0], conv_w_l, lru_conv_b, False, "lat_lru_conv_bwd")
    dxr = _to_row_major(dxr_seq)


    def ssd_out_bwd(r, s):
        yf, yb, cx_, z_, dyn = r
        xs_ = cx_[:, :ws]
        yv = yf + yb + s[0] * xs_
        sz = _silu(z_)
        yz = yv * sz
        rs = lax.rsqrt(_rowmean(yz * yz) + EPS)
        yzn = yz * rs
        dyzn = dyn * s[1]
        dyz = rs * (dyzn - yzn * _rowmean(dyzn * yzn))
        dyv = dyz * sz
        return [dyv, dyz * yv * _dsilu(z_), dyv * s[0]], [_colsum(dyv * xs_), _colsum(dyn * yzn)]
    (dy_ssd, dz, dxs_skip), (g_dfull, g_ssd_norm) = _rowwise(ssd_out_bwd, [l_ssd[0][0], l_ssd[1][0], l_cx, z, dys], [d_full, ssd_norm],
                                                              [(ws, F32), (ws, BF16), (ws, F32)], [ws, ws], "lat_ssd_out_bwd")
    sb0, got = _ssd_bwd(l_cx, l_dtr[:, :heads], bias_s[0:1], alog_s[0:1], dy_ssd, l_ssd[0][1], zeros_state, False, "lat_ssd_bwd0",
                        rider=_swap_rider(*mid_rs.stage()))
    mid_rs.absorb(got)
    sb = [sb0, _ssd_bwd(l_cx, l_dtr[:, heads:], bias_s[1:2], alog_s[1:2], dy_ssd, l_ssd[1][1], zeros_state, True, "lat_ssd_bwd1")]
    dxbc, g_sconv_w, g_sconv_b = _conv_bwd(
        l_xbc, [dxs_skip, sb[0][0], sb[1][0], sb[0][1], sb[1][1], sb[0][2], sb[1][2]],
        lambda v: jnp.concatenate([v[0] + v[1] + v[2], v[3] + v[4], v[5] + v[6]], axis=1), conv_w_s, ssd_conv_b, True, "lat_ssd_conv_bwd")
    ddt = jnp.pad(jnp.concatenate([sb[0][3], sb[1][3]], axis=1), ((0, 0), (0, LANES - 2 * heads)))

    zeros_cy = jnp.zeros((tc, ws), F32)
    csb = [_ssd_bwd(c_cx, c_dtr[:, dirn * heads:(dirn + 1) * heads], bias_s[dirn:dirn + 1], alog_s[dirn:dirn + 1],
                    zeros_cy, c_ssd[dirn][0], sb[dirn][6], bool(dirn), f"ctx_ssd_bwd{dirn}") for dirn in (0, 1)]
    c_dxbc, cg_sconv_w, cg_sconv_b = _conv_bwd(
        c_xbc, [csb[0][0], csb[1][0], csb[0][1], csb[1][1], csb[0][2], csb[1][2]],
        lambda v: jnp.concatenate([v[0] + v[1], v[2] + v[3], v[4] + v[5]], axis=1), conv_w_s, ssd_conv_b, True, "ctx_ssd_conv_bwd")
    c_ddt = jnp.pad(jnp.concatenate([csb[0][3], csb[1][3]], axis=1), ((0, 0), (0, LANES - 2 * heads)))
    zeros_ch = jnp.zeros((tc, wl), F32)
    clb0 = _lru_bwd(c_xc, *lru_args(0), zeros_l, c_lru[0][0], zeros_ch, lb0[6], None, False, "ctx_lru_bwd0")
    clb1 = _lru_bwd(c_xc, *lru_args(1), zeros_l, c_lru[1][0], zeros_ch, lb1[6], clb0[0], True, "ctx_lru_bwd1")
    c_dxr, cg_lconv_w, cg_lconv_b = _conv_bwd(c_xr_seq, [clb1[0]], lambda v: v[0], conv_w_l, lru_conv_b, False, "ctx_lru_conv_bwd")
    dhc = _mm([(c_dxbc, w_xbc), (c_ddt, w_dt), (c_dxr, w_xr)], "nt", F32, "ctx_proj_dx")
    _, (dcsh_m, dcsc_m, cg_norm_mix) = _norm_mod_bwd(ctx2, dhc, None, norm_mix, csc_m, "ctx_norm_bwd")

    gw_z = _mm([(h, dz)], "tn", F32, "lat_z_dw")
    gw_yr = _mm([(h, dyr)], "tn", F32, "lat_yr_dw")
    gw_xbc = _mm([(h, dxbc), (hc, c_dxbc)], "tn", F32, "proj_xbc_dw")
    gw_dt = _mm([(h, ddt), (hc, c_ddt)], "tn", F32, "proj_dt_dw")[:, :2 * heads]
    gw_xr = _mm([(h, dxr), (hc, c_dxr)], "tn", F32, "proj_xr_dw")
    in_rs = _ReduceScatter([slots(jnp.concatenate([gw_z, gw_xbc, gw_dt, gw_xr, gw_yr], axis=1))], "in_rs")
    small_parts = [
        g_norm_ffn, g_sconv_w + cg_sconv_w, g_sconv_b + cg_sconv_b,
        jnp.concatenate([sb[0][5] + csb[0][5], sb[1][5] + csb[1][5]], axis=0),
        jnp.concatenate([sb[0][4] + csb[0][4], sb[1][4] + csb[1][4]], axis=0),
        g_dfull, g_ssd_norm, g_lconv_w + cg_lconv_w, g_lconv_b + cg_lconv_b,
        jnp.stack([lb0[1] + clb0[1], lb1[1] + clb1[1]]), jnp.concatenate([lb0[3] + clb0[3], lb1[3] + clb1[3]], axis=0),
        jnp.stack([lb0[2] + clb0[2], lb1[2] + clb1[2]]), jnp.concatenate([lb0[4] + clb0[4], lb1[4] + clb1[4]], axis=0),
        jnp.concatenate([lb0[5] + clb0[5], lb1[5] + clb1[5]], axis=0),
        jnp.concatenate([gb_gs, gb_gr], axis=1), g_final,
    ]
    small_ar = _AllReduce(small_parts, "small_grads", narrow=(9, 11))
    dh, bufs = None, ()
    for rnd, pairs in enumerate(([(dz, w_z), (dxbc, w_xbc), (ddt, w_dt)], [(dxr, w_xr), (dyr, w_yr)], [(dgp_s, w_gs), (dgp_r, w_gr)])):
        send, axes = in_rs.stage()
        dh, got = _mm(pairs, "nt", F32, f"lat_proj_dx{rnd}", init=dh, rider=_merge_riders(_swap_rider(send, axes), small_ar.make(rnd, bufs)))
        in_rs.absorb(got[:len(send)])
        bufs = got[len(send):]
    grad_x, (dsh_m, dsc_m, g_norm_mix) = _norm_mod_bwd(x2, dh, dx1, norm_mix, sc_m, "lat_norm_bwd")

    zero_d = jnp.zeros((1, d), F32)
    dmod_rows = jnp.concatenate([jnp.concatenate([dsh_m, dsc_m, dg_m, dsh_f, dsc_f, dg_f], axis=1),
                                 jnp.concatenate([dcsh_m, dcsc_m, zero_d, zero_d, zero_d, zero_d], axis=1)], axis=0)
    (dmod_g,) = _all_gather([dmod_rows], "gather_dmod")
    dmod_c = functools.reduce(lambda u, v: u + v, [dmod_g[i, 1] for i in range(N_DEV)])
    dmod9 = jnp.concatenate([dmod_g[:, 0], dmod_c[None, :], jnp.zeros((16 - N_DEV - 1, N_MOD * d), F32)], axis=0)
    g_b_ada = jnp.sum(dmod9, axis=0, keepdims=True)
    dmod_l = lax.dynamic_slice_in_dim(dmod9, me * ada_cols, ada_cols, 1)
    g_w_ada = _mm([(cond_act, dmod_l)], "tn", F32, "ada_dw")
    dcond_l = _mm([(dmod_l, w_ada[0])], "nt", F32, "ada_dx")

    (s_nffn, s_sconv_w, s_sconv_b, s_dtb, s_da, s_dfull, s_snorm, s_lconv_w, s_lconv_b,
     s_wa, s_ba, s_wx, s_bx, s_dsp, s_bgate, s_final) = small_ar.finish(bufs)
    s_cctx, s_nmix = _all_reduce([dcond_l[N_DEV:N_DEV + 1], g_norm_mix + cg_norm_mix], "last_grads")
    shard = lambda a, n_: lax.dynamic_slice_in_dim(a, me * n_, n_, a.ndim - 1)
    grads = {
        "c_ctx": (s_cctx * _dsilu(c_ctx[None, :]))[0],
        "w_ada": g_w_ada[None], "b_ada": g_b_ada, "norm_mix": s_nmix, "norm_ffn": s_nffn,
        "ssd_conv_w": shard(s_sconv_w, ssd_conv_w.shape[2])[None], "ssd_conv_b": s_sconv_b,
        "ssd_dt_bias": s_dtb[None], "ssd_a_log": (s_da * (-jnp.exp(alog_s)))[None],
        "ssd_d": jnp.sum(s_dfull.reshape(heads, SSD_HEAD_DIM), axis=1)[None], "ssd_norm": s_snorm,
        "lru_conv_w": shard(s_lconv_w, lru_conv_w.shape[2])[None], "lru_conv_b": s_lconv_b,
        "lru_w_a": s_wa[None], "lru_b_a": shard(s_ba, lru_b_a.shape[2])[None],
        "lru_w_x": s_wx[None], "lru_b_x": shard(s_bx, lru_b_x.shape[2])[None],
        "lru_lambda": shard(s_dsp * (-_sigmoid(-lru_lam)), lru_lambda.shape[2])[None],
        "b_gate": s_bgate, "final_norm": s_final[0],
    }

    for n_, g in zip(["w_in", "w_out_ssd", "w_out_lru", "w_gate", "w_o", "ffn_w13", "ffn_w2"],
                     in_rs.result() + mid_rs.result() + ffn_rs.result()):
        grads[n_] = g[None]

    big_names = ["w_ada", "w_in", "w_out_ssd", "w_out_lru", "w_gate", "w_o", "ffn_w13", "ffn_w2", "lru_w_a", "lru_w_x"]
    small_names = [n_ for n_ in names if n_ not in big_names]
    delta, new_m, new_v = {}, {}, {}
    for n_ in big_names:
        delta[n_], new_m[n_], new_v[n_] = _adamw(wts[n_], grads[n_], mom1[n_], mom2[n_], "adamw_" + n_)
    packed = [_pack([src[n_] for n_ in small_names]) for src in (wts, grads, mom1, mom2)]
    outs = _adamw(*packed, "adamw_small")
    like = [wts[n_] for n_ in small_names]
    for dst, buf in zip((delta, new_m, new_v), outs):
        dst.update(zip(small_names, _unpack(buf, like)))

    return (loss, grad_x[None], *[grads[n_].reshape(wts[n_].shape) for n_ in names], *[delta[n_] for n_ in names],
            *[new_m[n_] for n_ in names], *[new_v[n_] for n_ in names])
```

```python
import functools

import numpy as np
import jax
import jax.numpy as jnp
from jax import lax
from jax.experimental import pallas as pl
from jax.experimental.pallas import tpu as pltpu

F32, BF16 = jnp.float32, jnp.bfloat16
MESH = pl.DeviceIdType.MESH
N_DEV = 8
VMEM_LIMIT_BYTES = 48 * 1024 * 1024
ROW_BUDGET_BYTES = 12 * 1024 * 1024
LANES, SUBLANES = 128, 8

EPS = 1e-6
GRID_W = 64
SSD_HEAD_DIM = 64
SSD_GROUPS = 4
SSD_STATE = 128
SSD_CHUNK = 128
CONV_W = 4
CONV_LEFT = 2
LRU_HEAD_DIM = 128
LRU_C = 8.0
LRU_BLOCK = 128
N_MOD = 6
ADAM_LR, ADAM_B1, ADAM_B2, ADAM_EPS, ADAM_WD, ADAM_STEP = 0.001, 0.9, 0.999, 1e-08, 0.01, 10
NEG_BIG = -1e30


def _params(sem):
    return pltpu.CompilerParams(dimension_semantics=sem, vmem_limit_bytes=VMEM_LIMIT_BYTES)


def _tile(dim, pref, align):
    if dim <= pref:
        return dim
    t = (pref // align) * align
    while t >= align:
        if dim % t == 0:
            return t
        t -= align
    return dim


class _Rider:
    def __init__(self, ins, out_shape, aliases, sems, start, wait):
        self.ins, self.out_shape, self.aliases, self.sems, self.start, self.wait = ins, out_shape, aliases, sems, start, wait


def _merge_riders(a, b):
    ai, ao, asem = len(a.ins), len(a.out_shape), len(a.sems)
    aliases = dict(a.aliases)
    aliases.update({ai + i: ao + o for i, o in b.aliases.items()})

    def both(fa, fb):
        def run(ins, outs, sems):
            fa(ins[:ai], outs[:ao], sems[:asem])
            fb(ins[ai:], outs[ao:], sems[asem:])
        return run

    return _Rider(list(a.ins) + list(b.ins), list(a.out_shape) + list(b.out_shape), aliases, list(a.sems) + list(b.sems),
                  both(a.start, b.start), both(a.wait, b.wait))


def _call(body, name, grid, in_specs, out_specs, out_shape, scratch_shapes, sem, operands, rider=None):
    if rider is None:
        return pl.pallas_call(body, name=name, grid=grid, in_specs=in_specs, out_specs=out_specs, out_shape=out_shape,
                              scratch_shapes=scratch_shapes, compiler_params=_params(sem))(*operands)
    n_in, n_out, n_scr = len(in_specs), len(out_specs), len(scratch_shapes)
    ri, ro = len(rider.ins), len(rider.out_shape)
    a, b = n_in, n_in + ri
    c, d = b + n_out, b + n_out + ro
    e = d + n_scr

    def carried(*refs):
        ids = [pl.program_id(i) for i in range(len(grid))]
        first = functools.reduce(lambda u, v: u & v, [i == 0 for i in ids])
        last = functools.reduce(lambda u, v: u & v, [i == g - 1 for i, g in zip(ids, grid)])

        @pl.when(first)
        def _():
            rider.start(refs[a:b], refs[c:d], refs[e:])

        body(*refs[:a], *refs[b:c], *refs[d:e])

        @pl.when(last)
        def _():
            rider.wait(refs[a:b], refs[c:d], refs[e:])

    any_spec = pl.BlockSpec(memory_space=pl.ANY)
    res = pl.pallas_call(
        carried, name=name, grid=grid, in_specs=list(in_specs) + [any_spec] * ri, out_specs=list(out_specs) + [any_spec] * ro,
        out_shape=list(out_shape) + list(rider.out_shape), scratch_shapes=list(scratch_shapes) + list(rider.sems),
        input_output_aliases={n_in + i: n_out + o for i, o in rider.aliases.items()},
        compiler_params=_params(("arbitrary",) * len(grid)),
    )(*operands, *rider.ins)
    return res[:n_out], res[n_out:]


def _bdot(a, b):
    return jnp.dot(a.astype(BF16), b.astype(BF16), preferred_element_type=F32)


def _bdot_nt(a, b):
    return lax.dot_general(a.astype(BF16), b.astype(BF16), (((1,), (1,)), ((), ())), preferred_element_type=F32)


def _bdot_tn(a, b):
    return lax.dot_general(a.astype(BF16), b.astype(BF16), (((0,), (0,)), ((), ())), preferred_element_type=F32)


def _hdot(a, b):
    return jnp.dot(a, b, preferred_element_type=F32, precision=lax.Precision.HIGHEST)


def _softplus(v):
    return jnp.maximum(v, 0.0) + jnp.log(1.0 + jnp.exp(-jnp.abs(v)))


def _sigmoid(v):
    return 0.5 * jnp.tanh(0.5 * v) + 0.5


def _silu(v):
    return v * _sigmoid(v)


def _dsilu(v):
    s = _sigmoid(v)
    return s * (1.0 + v * (1.0 - s))


GELU_C0 = 0.7978845608028654
GELU_C1 = 0.044715


def _gelu(v):
    return 0.5 * v * (1.0 + jnp.tanh(GELU_C0 * (v + GELU_C1 * v * v * v)))


def _dgelu(v):
    t = jnp.tanh(GELU_C0 * (v + GELU_C1 * v * v * v))
    return 0.5 * (1.0 + t) + 0.5 * v * (1.0 - t * t) * GELU_C0 * (1.0 + 3.0 * GELU_C1 * v * v)


MM_TM, MM_TN = 1024, 1408
MM_TK = {1: 2048, 2: 1536}


def _mm(pairs, mode, out_dtype, name, init=None, rider=None, slots=None):
    a0, b0 = pairs[0]
    if mode == "nn":
        m, n, ks = a0.shape[0], b0.shape[1], [a.shape[1] for a, _ in pairs]
    elif mode == "nt":
        m, n, ks = a0.shape[0], b0.shape[0], [a.shape[1] for a, _ in pairs]
    else:
        m, n, ks = a0.shape[1], b0.shape[1], [a.shape[0] for a, _ in pairs]
    tm = _tile(m, MM_TM, LANES if mode == "tn" else 16)
    tn = _tile(n, MM_TN, LANES) if slots is None else slots[0]
    tk = MM_TK.get(sum(k > 512 for k in ks), 512)
    tks = [_tile(k, tk, 16 if mode == "tn" else LANES) for k in ks]
    steps = [k // t for k, t in zip(ks, tks)]
    offs = [sum(steps[:i]) for i in range(len(pairs))]
    total = sum(steps)
    npairs = len(pairs)

    def kk(i):
        return lambda k: jnp.clip(k - offs[i], 0, steps[i] - 1)

    in_specs, operands = [], []
    for i, (a, b) in enumerate(pairs):
        kf = kk(i)
        if mode == "nn":
            in_specs.append(pl.BlockSpec((tm, tks[i]), lambda mi, ni, k, kf=kf: (mi, kf(k))))
            in_specs.append(pl.BlockSpec((tks[i], tn), lambda mi, ni, k, kf=kf: (kf(k), ni)))
        elif mode == "nt":
            in_specs.append(pl.BlockSpec((tm, tks[i]), lambda mi, ni, k, kf=kf: (mi, kf(k))))
            in_specs.append(pl.BlockSpec((tn, tks[i]), lambda mi, ni, k, kf=kf: (ni, kf(k))))
        else:
            in_specs.append(pl.BlockSpec((tks[i], tm), lambda mi, ni, k, kf=kf: (kf(k), mi)))
            in_specs.append(pl.BlockSpec((tks[i], tn), lambda mi, ni, k, kf=kf: (kf(k), ni)))
        operands += [a, b]
    dot = {"nn": _bdot, "nt": _bdot_nt, "tn": _bdot_tn}[mode]
    if init is not None:
        in_specs.append(pl.BlockSpec((tm, tn), lambda mi, ni, k: (mi, ni)))
        operands.append(init)
    assert out_dtype == F32 or total == 1

    def body(*refs):
        out_ref = refs[-1]
        k = pl.program_id(2)

        def first_step():
            p = dot(refs[0][...], refs[1][...])
            return p if init is None else p + refs[2 * npairs][...]

        if total == 1:
            out_ref[...] = first_step().astype(out_dtype)
            return

        @pl.when(k == 0)
        def _():
            out_ref[...] = first_step()

        for i in range(npairs):
            first = 1 if i == 0 else offs[i]
            if offs[i] + steps[i] > first:
                @pl.when((k >= first) & (k < offs[i] + steps[i]))
                def _(i=i):
                    out_ref[...] += dot(refs[2 * i][...], refs[2 * i + 1][...])

    if slots is not None:
        width, n_slots, first_slot, into = slots
        assert rider is None and init is None and n % width == 0
        aliases = {}
        if into is not None:
            in_specs.append(pl.BlockSpec(memory_space=pl.ANY))
            operands.append(into)
            aliases = {len(operands) - 1: 0}
        return pl.pallas_call(
            body, name=name, grid=(m // tm, n // tn, total), in_specs=in_specs,
            out_specs=pl.BlockSpec((None, tm, tn), lambda mi, ni, k: (ni + first_slot, mi, 0)),
            out_shape=jax.ShapeDtypeStruct((n_slots, m, width), out_dtype), input_output_aliases=aliases,
            compiler_params=_params(("parallel", "parallel", "arbitrary")))(*operands)
    res = _call(body, name, (m // tm, n // tn, total), in_specs, [pl.BlockSpec((tm, tn), lambda mi, ni, k: (mi, ni))],
                [jax.ShapeDtypeStruct((m, n), out_dtype)], [], ("parallel", "parallel", "arbitrary"), operands, rider)
    return res[0] if rider is None else (res[0][0], res[1])


def _rowwise(fn, rows, small, outs, accs, name):
    t = rows[0].shape[0]
    per_row = sum(r.shape[1] * r.dtype.itemsize for r in rows) + sum(c * jnp.dtype(d).itemsize for c, d in outs)
    tm = SUBLANES
    while tm * 2 <= 1024 and t % (tm * 2) == 0 and tm * 2 * per_row <= ROW_BUDGET_BYTES:
        tm *= 2
    if t % tm:
        tm = t
    nrow, nsmall, nout, nacc = len(rows), len(small), len(outs), len(accs)

    def body(*refs):
        rv = [r[...].astype(F32) for r in refs[:nrow]]
        sv = [r[...] for r in refs[nrow:nrow + nsmall]]
        o_refs = refs[nrow + nsmall:nrow + nsmall + nout]
        a_refs = refs[nrow + nsmall + nout:]
        ov, av = fn(rv, sv)
        for r, v in zip(o_refs, ov):
            r[...] = v.astype(r.dtype)
        if nacc:
            @pl.when(pl.program_id(0) == 0)
            def _():
                for r in a_refs:
                    r[...] = jnp.zeros_like(r)
            for r, v in zip(a_refs, av):
                r[...] += v

    in_specs = [pl.BlockSpec((tm, r.shape[1]), lambda i: (i, 0)) for r in rows]
    in_specs += [pl.BlockSpec(s.shape, lambda i: (0, 0)) for s in small]
    out_specs = [pl.BlockSpec((tm, c), lambda i: (i, 0)) for c, _ in outs]
    out_specs += [pl.BlockSpec((1, c), lambda i: (0, 0)) for c in accs]
    out_shape = [jax.ShapeDtypeStruct((t, c), d) for c, d in outs]
    out_shape += [jax.ShapeDtypeStruct((1, c), F32) for c in accs]
    res = pl.pallas_call(
        body, name=name, grid=(t // tm,), in_specs=in_specs, out_specs=out_specs, out_shape=out_shape,
        compiler_params=_params(("arbitrary",)),
    )(*rows, *small)
    return res[:nout], res[nout:]


def _colsum(v):
    return jnp.sum(v, axis=0, keepdims=True)


def _rowmean(v):
    return jnp.mean(v, axis=1, keepdims=True)


def _conv_tiles(x):
    t, c = x.shape
    tm = _tile(t, 256, SUBLANES)
    nt = t // tm
    r8 = tm // SUBLANES
    specs = [
        pl.BlockSpec((SUBLANES, c), lambda i: (jnp.maximum(i * r8 - 1, 0), 0)),
        pl.BlockSpec((tm, c), lambda i: (i, 0)),
        pl.BlockSpec((SUBLANES, c), lambda i: (jnp.minimum((i + 1) * r8, t // SUBLANES - 1), 0)),
    ]
    return tm, nt, specs


def _conv_taps(prev_ref, cur_ref, next_ref, tm, nt, left):
    i = pl.program_id(0)
    p = jnp.where(i > 0, prev_ref[...], 0.0)
    q = jnp.where(i < nt - 1, next_ref[...], 0.0)
    ext = jnp.concatenate([p, cur_ref[...], q], axis=0)
    n = tm + 2 * SUBLANES
    taps = []
    for k in range(CONV_W):
        s = (left - k) % n
        r = pltpu.roll(ext, s, axis=0) if s else ext
        taps.append(r[SUBLANES:SUBLANES + tm])
    return taps


def _conv(x, w, b, left, act, name, out_dtype=F32):
    t, c = x.shape
    tm, nt, specs = _conv_tiles(x)

    def body(prev_ref, cur_ref, next_ref, w_ref, b_ref, o_ref):
        taps = _conv_taps(prev_ref, cur_ref, next_ref, tm, nt, left)
        pre = b_ref[...] + sum(w_ref[k:k + 1, :] * taps[k] for k in range(CONV_W))
        o_ref[...] = (_silu(pre) if act else pre).astype(out_dtype)

    return pl.pallas_call(
        body, name=name, grid=(nt,),
        in_specs=specs + [pl.BlockSpec((CONV_W, c), lambda i: (0, 0)), pl.BlockSpec((1, c), lambda i: (0, 0))],
        out_specs=pl.BlockSpec((tm, c), lambda i: (i, 0)),
        out_shape=jax.ShapeDtypeStruct((t, c), out_dtype), compiler_params=_params(("arbitrary",)),
    )(x, x, x, w, b)


def _conv_bwd(x, dys, combine, w, b, act, name):
    t, c = x.shape
    tm, nt, specs = _conv_tiles(x)
    nd = len(dys)

    def body(*refs):
        prev_ref, cur_ref, next_ref = refs[:3]
        w_ref, b_ref, dpre_ref, dw0, dw1, dw2, dw3, db_ref = refs[3 + nd:]
        dy = combine([r[...] for r in refs[3:3 + nd]])
        taps = _conv_taps(prev_ref, cur_ref, next_ref, tm, nt, CONV_LEFT)
        if act:
            pre = b_ref[...] + sum(w_ref[k:k + 1, :] * taps[k] for k in range(CONV_W))
            dpre = dy * _dsilu(pre)
        else:
            dpre = dy
        dpre_ref[...] = dpre

        @pl.when(pl.program_id(0) == 0)
        def _():
            for r in (dw0, dw1, dw2, dw3, db_ref):
                r[...] = jnp.zeros_like(r)

        for k, r in enumerate((dw0, dw1, dw2, dw3)):
            r[...] += _colsum(dpre * taps[k])
        db_ref[...] += _colsum(dpre)

    row = pl.BlockSpec((1, c), lambda i: (0, 0))
    res = pl.pallas_call(
        body, name=name + "_pre", grid=(nt,),
        in_specs=specs + [pl.BlockSpec((tm, a.shape[1]), lambda i: (i, 0)) for a in dys] + [pl.BlockSpec((CONV_W, c), lambda i: (0, 0)), row],
        out_specs=[pl.BlockSpec((tm, c), lambda i: (i, 0))] + [row] * 5,
        out_shape=[jax.ShapeDtypeStruct((t, c), F32)] + [jax.ShapeDtypeStruct((1, c), F32)] * 5,
        compiler_params=_params(("arbitrary",)),
    )(x, x, x, *dys, w, b)
    dpre, dw, db = res[0], jnp.concatenate(res[1:5], axis=0), res[5]
    dx = _conv(dpre, w[::-1], jnp.zeros_like(b), CONV_W - 1 - CONV_LEFT, False, name + "_dx", BF16)
    return dx, dw, db


def _ssd_dims(xbc):
    t = xbc.shape[0]
    gn = SSD_GROUPS * SSD_STATE
    ws = xbc.shape[1] - 2 * gn
    heads = ws // SSD_HEAD_DIM
    hpg = heads // SSD_GROUPS
    assert hpg % 2 == 0 and ws % gn == 0 and t % SSD_CHUNK == 0
    return t, ws, heads, hpg, heads // 2, t // SSD_CHUNK


def _ssd_chunk_terms(dtr_ref, dtrt_ref, bias_ref, biast_ref, alog_ref, alogt_ref, reverse):
    q = SSD_CHUNK
    a_neg = -jnp.exp(alog_ref[...])
    dt = _softplus(dtr_ref[...] + bias_ref[...])
    dtt = _softplus(dtrt_ref[...] + biast_ref[...])
    a = dt * a_neg
    at = dtt * (-jnp.exp(alogt_ref[...]))
    ii = lax.broadcasted_iota(jnp.int32, (q, q), 0)
    jj = lax.broadcasted_iota(jnp.int32, (q, q), 1)
    mask = (ii <= jj) if reverse else (ii >= jj)
    tri = mask.astype(F32)
    trit = ((jj <= ii) if reverse else (jj >= ii)).astype(F32)
    before = (jj > ii) if reverse else (jj < ii)
    ac = _hdot(tri, a)
    act = _hdot(at, trit)
    tot = _colsum(a)
    return a_neg, dt, a, ac, act, tot, mask, trit, before


def _pair_terms(lo, h1, dt, ac, tot):
    q = SSD_CHUNK

    def colb(v, h):
        return jnp.broadcast_to(v[:, h:h + 1], (v.shape[0], LANES))

    def pairb(v):
        return jnp.where(lo[:v.shape[0]], colb(v, h1), colb(v, h1 + 1))

    dt_p = pairb(dt)
    ac_p = pairb(ac)
    eac_p = jnp.exp(ac_p)
    dte_p = jnp.exp(pairb(jnp.broadcast_to(tot, (q, tot.shape[1]))) - ac_p)
    etot_p = jnp.exp(pairb(tot))
    return dt_p, eac_p, dte_p, etot_p


def _decay_mats(mask, ac, act, cb, h1):
    q = SSD_CHUNK
    out = []
    for h in (h1, h1 + 1):
        diff = jnp.broadcast_to(ac[:, h:h + 1], (q, q)) - jnp.broadcast_to(act[h:h + 1, :], (q, q))
        lmat = jnp.exp(jnp.where(mask, diff, NEG_BIG))
        out.append((lmat, cb * lmat))
    return out


def _ssd_specs(t, ws, heads, npairs, nc, reverse):
    q, gn = SSD_CHUNK, SSD_GROUPS * SSD_STATE
    ci = (lambda k: nc - 1 - k) if reverse else (lambda k: k)
    small = lambda shape: pl.BlockSpec(shape, lambda k: (0,) * len(shape))
    seq = [
        pl.BlockSpec((q, ws), lambda k: (ci(k), 0)),
        pl.BlockSpec((q, gn), lambda k: (ci(k), ws // gn)),
        pl.BlockSpec((q, gn), lambda k: (ci(k), ws // gn + 1)),
        pl.BlockSpec((q, heads), lambda k: (ci(k), 0)),
        pl.BlockSpec((heads, q), lambda k: (0, ci(k))),
        small((1, heads)), small((heads, 1)), small((1, heads)), small((heads, 1)),
    ]
    return ci, small, seq


def _ssd_fwd(xbc, dtr, bias, alog, s0, reverse, name, rider=None):
    t, ws, heads, hpg, npairs, nc = _ssd_dims(xbc)
    q, ns = SSD_CHUNK, SSD_STATE
    ci, small, seq = _ssd_specs(t, ws, heads, npairs, nc, reverse)

    def body(xs_ref, b_ref, c_ref, dtr_ref, dtrt_ref, bias_ref, biast_ref, alog_ref, alogt_ref, s0_ref,
             y_ref, se_ref, s_ref):
        @pl.when(pl.program_id(0) == 0)
        def _():
            s_ref[...] = s0_ref[...]

        _, dt, _, ac, act, tot, mask, _, _ = _ssd_chunk_terms(dtr_ref, dtrt_ref, bias_ref, biast_ref, alog_ref, alogt_ref, reverse)
        lo = lax.broadcasted_iota(jnp.int32, (q, LANES), 1) < SSD_HEAD_DIM
        for g in range(SSD_GROUPS):
            bg = b_ref[:, g * ns:(g + 1) * ns]
            cg = c_ref[:, g * ns:(g + 1) * ns]
            cb = _bdot_nt(cg, bg)
            bgt = bg.T
            for pq in range(hpg // 2):
                pp = g * (hpg // 2) + pq
                h1 = 2 * pp
                cs = slice(pp * LANES, (pp + 1) * LANES)
                (_, m1), (_, m2) = _decay_mats(mask, ac, act, cb, h1)
                dt_p, eac_p, dte_p, etot_p = _pair_terms(lo, h1, dt, ac, tot)
                xp = xs_ref[:, cs] * dt_p
                sp = s_ref[pp]
                se_ref[0, pp] = sp
                y_ref[:, cs] = jnp.where(lo, _bdot(m1, xp), _bdot(m2, xp)) + eac_p * _bdot(cg, sp)
                s_ref[pp] = etot_p * sp + _bdot(bgt, xp * dte_p)

    st = (npairs, ns, LANES)
    return _call(
        body, name, (nc,), seq + [small(st)],
        [pl.BlockSpec((q, ws), lambda k: (ci(k), 0)), pl.BlockSpec((1,) + st, lambda k: (ci(k), 0, 0, 0)), small(st)],
        [jax.ShapeDtypeStruct((t, ws), F32), jax.ShapeDtypeStruct((nc,) + st, F32), jax.ShapeDtypeStruct(st, F32)],
        [], ("arbitrary",), (xbc, xbc, xbc, dtr, dtr.T, bias, bias.T, alog, alog.T, s0), rider)


def _ssd_bwd(xbc, dtr, bias, alog, dy, s_enter, lam0, reverse, name, rider=None):
    t, ws, heads, hpg, npairs, nc = _ssd_dims(xbc)
    q, ns, gn = SSD_CHUNK, SSD_STATE, SSD_GROUPS * SSD_STATE
    ci, small, seq = _ssd_specs(t, ws, heads, npairs, nc, not reverse)
    st = (npairs, ns, LANES)

    e_heads = jnp.asarray(np.arange(ws)[:, None] // SSD_HEAD_DIM == np.arange(heads)[None, :], BF16)

    def body(xs_ref, b_ref, c_ref, dtr_ref, dtrt_ref, bias_ref, biast_ref, alog_ref, alogt_ref, dy_ref, se_ref, lam0_ref, e_ref,
             dxs_ref, db_ref, dc_ref, ddtr_ref, da_ref, dbias_ref, lam_ref, st_s, ts_s, ss_s, xx_s):
        @pl.when(pl.program_id(0) == 0)
        def _():
            lam_ref[...] = lam0_ref[...]
            da_ref[...] = jnp.zeros_like(da_ref)
            dbias_ref[...] = jnp.zeros_like(dbias_ref)

        a_neg, dt, _, ac, act, tot, mask, trit, before = _ssd_chunk_terms(dtr_ref, dtrt_ref, bias_ref, biast_ref, alog_ref, alogt_ref, reverse)
        lo = lax.broadcasted_iota(jnp.int32, (q, LANES), 1) < SSD_HEAD_DIM
        head_col = lax.broadcasted_iota(jnp.int32, (heads, 1), 0)
        ii = lax.broadcasted_iota(jnp.int32, (q, q), 0)
        jj = lax.broadcasted_iota(jnp.int32, (q, q), 1)
        before_t = ((ii > jj) if reverse else (ii < jj)).astype(F32)

        def per_head(v):
            hi = v.astype(BF16)
            lo = (v - hi.astype(F32)).astype(BF16)
            sel = e_ref[...]
            return jnp.dot(hi, sel, preferred_element_type=F32) + jnp.dot(lo, sel, preferred_element_type=F32)

        da_tt = jnp.zeros((heads, q), F32)
        for g in range(SSD_GROUPS):
            gs = slice(g * ns, (g + 1) * ns)
            bg, cg = b_ref[:, gs], c_ref[:, gs]
            cb = _bdot_nt(cg, bg)
            cgt = cg.T
            dcb = jnp.zeros((q, q), F32)
            dbg = jnp.zeros((q, ns), F32)
            dcg = jnp.zeros((q, ns), F32)
            for pq in range(hpg // 2):
                pp = g * (hpg // 2) + pq
                h1 = 2 * pp
                cs = slice(pp * LANES, (pp + 1) * LANES)
                (l1, m1), (l2, m2) = _decay_mats(mask, ac, act, cb, h1)
                dt_p, eac_p, dte_p, etot_p = _pair_terms(lo, h1, dt, ac, tot)
                xsp = xs_ref[:, cs]
                xp = xsp * dt_p
                dyp = dy_ref[:, cs]
                lam = lam_ref[pp]
                sp = se_ref[0, pp]
                dx_state = dte_p * _bdot(bg, lam)
                dxdt = jnp.where(lo, _bdot(m1.T, dyp), _bdot(m2.T, dyp)) + dx_state
                dyx1 = _bdot_nt(jnp.where(lo, dyp, 0.0), xp)
                dyx2 = _bdot_nt(jnp.where(lo, 0.0, dyp), xp)
                dcb = dcb + l1 * dyx1 + l2 * dyx2
                for h, pair in ((h1, m1 * dyx1), (h1 + 1, m2 * dyx2)):
                    cross = _colsum(jnp.where(mask, _bdot(pair, before_t), 0.0))
                    da_tt = da_tt + (head_col == h).astype(F32) * cross
                st_s[:, cs] = dyp * (eac_p * _bdot(cg, sp))
                ts_s[:, cs] = dx_state * xp
                ss_s[:, cs] = jnp.broadcast_to(_colsum(lam * sp) * etot_p, (SUBLANES, LANES))
                xx_s[:, cs] = dxdt * xsp
                edy = eac_p * dyp
                xe = dte_p * xp
                dcg = dcg + _bdot_nt(edy, sp)
                dbg = dbg + _bdot_nt(xe, lam)
                lam_ref[pp] = etot_p * lam + _bdot(cgt, edy)
                dxs_ref[:, cs] = dxdt * dt_p
            db_ref[:, gs] = dbg + _bdot(dcb.T, cg)
            dc_ref[:, gs] = dcg + _bdot(dcb, bg)
        da = da_tt.T + _hdot(trit, per_head(st_s[...])) + _hdot(before.astype(F32), per_head(ts_s[...])) + per_head(ss_s[...])[0:1]
        ddtr = (da * a_neg + per_head(xx_s[...])) * _sigmoid(dtr_ref[...] + bias_ref[...])
        ddtr_ref[...] = ddtr
        da_ref[...] += _colsum(da * dt)
        dbias_ref[...] += _colsum(ddtr)

    row = lambda c: pl.BlockSpec((q, c), lambda k: (ci(k), 0))
    return _call(
        body, name, (nc,),
        seq + [row(ws), pl.BlockSpec((1,) + st, lambda k: (ci(k), 0, 0, 0)), small(st), small((ws, heads))],
        [row(ws), row(gn), row(gn), row(heads), small((1, heads)), small((1, heads)), small(st)],
        [jax.ShapeDtypeStruct((t, ws), F32), jax.ShapeDtypeStruct((t, gn), F32), jax.ShapeDtypeStruct((t, gn), F32),
         jax.ShapeDtypeStruct((t, heads), F32), jax.ShapeDtypeStruct((1, heads), F32),
         jax.ShapeDtypeStruct((1, heads), F32), jax.ShapeDtypeStruct(st, F32)],
        [pltpu.VMEM((q, ws), F32), pltpu.VMEM((q, ws), F32), pltpu.VMEM((SUBLANES, ws), F32), pltpu.VMEM((q, ws), F32)],
        ("arbitrary",), (xbc, xbc, xbc, dtr, dtr.T, bias, bias.T, alog, alog.T, dy, s_enter, lam0, e_heads), rider)


def _tile_scan(a_ref, u_ref, out_ref, carry_ref, ntiles, reverse):
    w = a_ref.shape[1]
    row = lax.broadcasted_iota(jnp.int32, (SUBLANES, w), 0)

    def step(j, _):
        r0 = pl.multiple_of((ntiles - 1 - j if reverse else j) * SUBLANES, SUBLANES)
        a = a_ref[pl.ds(r0, SUBLANES), :]
        u = u_ref[pl.ds(r0, SUBLANES), :]
        for d in (1, 2, 4):
            keep = (row < SUBLANES - d) if reverse else (row >= d)
            s = SUBLANES - d if reverse else d
            a_sh = jnp.where(keep, pltpu.roll(a, s, axis=0), 1.0)
            u_sh = jnp.where(keep, pltpu.roll(u, s, axis=0), 0.0)
            u = a * u_sh + u
            a = a * a_sh
        h = a * carry_ref[...] + u
        out_ref[pl.ds(r0, SUBLANES), :] = h
        last = h[0:1, :] if reverse else h[SUBLANES - 1:SUBLANES, :]
        carry_ref[...] = jnp.broadcast_to(last, (SUBLANES, w))
        return 0

    lax.fori_loop(0, ntiles, step, 0)


def _lru_gates(xh, wa, wx, ba, bx, lam):
    r = _sigmoid(_bdot(xh, wa) + ba)
    i = _sigmoid(_bdot(xh, wx) + bx)
    sp = _softplus(-lam)
    a = jnp.exp(-LRU_C * r * sp)
    return r, i, sp, a


def _lru_fwd(xc, wa, wx, ba, bx, lam, h0, addend, reverse, name, rider=None):
    t, w = xc.shape
    tb = LRU_BLOCK
    nb, nh = t // tb, w // LRU_HEAD_DIM
    bi = (lambda k: nb - 1 - k) if reverse else (lambda k: k)
    has_add = addend is not None

    def body(*refs):
        xc_ref, wa_ref, wx_ref, ba_ref, bx_ref, lam_ref, h0_ref = refs[:7]
        add_ref = refs[7] if has_add else None
        h_ref, hl_ref = refs[7 + has_add], refs[8 + has_add]
        sum_ref = refs[9 + has_add] if has_add else None
        a_s, u_s, carry = refs[-3:]

        @pl.when(pl.program_id(0) == 0)
        def _():
            carry[...] = jnp.broadcast_to(h0_ref[...], carry.shape)

        for hd in range(nh):
            hs = slice(hd * LRU_HEAD_DIM, (hd + 1) * LRU_HEAD_DIM)
            xh = xc_ref[:, hs]
            _, i, _, a = _lru_gates(xh, wa_ref[hd], wx_ref[hd], ba_ref[:, hs], bx_ref[:, hs], lam_ref[:, hs])
            a_s[:, hs] = a
            u_s[:, hs] = jnp.sqrt(1.0 - a * a) * (i * xh)
        _tile_scan(a_s, u_s, h_ref, carry, tb // SUBLANES, reverse)
        hl_ref[...] = carry[0:1, :]
        if has_add:
            sum_ref[...] = h_ref[...] + add_ref[...]

    blk = pl.BlockSpec((tb, w), lambda k: (bi(k), 0))
    vec = pl.BlockSpec((1, w), lambda k: (0, 0))
    wsp = pl.BlockSpec((nh, LRU_HEAD_DIM, LRU_HEAD_DIM), lambda k: (0, 0, 0))
    out_specs = [blk, vec] + ([blk] if has_add else [])
    out_shape = [jax.ShapeDtypeStruct((t, w), F32), jax.ShapeDtypeStruct((1, w), F32)] + ([jax.ShapeDtypeStruct((t, w), F32)] if has_add else [])
    return _call(
        body, name, (nb,), [blk, wsp, wsp, vec, vec, vec, vec] + ([blk] if has_add else []), out_specs, out_shape,
        [pltpu.VMEM((tb, w), F32), pltpu.VMEM((tb, w), F32), pltpu.VMEM((SUBLANES, w), F32)], ("arbitrary",),
        (xc, wa, wx, ba, bx, lam, h0, *([addend] if has_add else [])), rider)


def _lru_bwd(xc, wa, wx, ba, bx, lam, h0, h, dh, dfin, addend, reverse, name, rider=None):
    t, w = xc.shape
    tb = LRU_BLOCK
    nb, nh = t // tb, w // LRU_HEAD_DIM
    r8 = tb // SUBLANES
    bi = (lambda k: k) if reverse else (lambda k: nb - 1 - k)
    if reverse:
        halo = lambda k: (jnp.minimum((bi(k) + 1) * r8, t // SUBLANES - 1), 0)
    else:
        halo = lambda k: (jnp.maximum(bi(k) * r8 - 1, 0), 0)
    has_add = addend is not None
    n_ext = tb + SUBLANES

    def body(*refs):
        xc_ref, wa_ref, wx_ref, ba_ref, bx_ref, lam_ref, h0_ref, h_ref, halo_ref, dh_ref, dfin_ref = refs[:11]
        add_ref = refs[11] if has_add else None
        dxc_ref, dwa_ref, dwx_ref, dba_ref, dbx_ref, dsp_ref, dh0_ref = refs[11 + has_add:18 + has_add]
        a_s, v_s, mu_s, r_s, i_s, carry = refs[-6:]
        k = pl.program_id(0)

        @pl.when(k == 0)
        def _():
            carry[...] = jnp.broadcast_to(dfin_ref[...], carry.shape)
            for r in (dwa_ref, dwx_ref, dba_ref, dbx_ref, dsp_ref):
                r[...] = jnp.zeros_like(r)

        for hd in range(nh):
            hs = slice(hd * LRU_HEAD_DIM, (hd + 1) * LRU_HEAD_DIM)
            r, i, _, a = _lru_gates(xc_ref[:, hs], wa_ref[hd], wx_ref[hd], ba_ref[:, hs], bx_ref[:, hs], lam_ref[:, hs])
            a_s[:, hs] = a
            r_s[:, hs] = r
            i_s[:, hs] = i
            v_s[:, hs] = a * dh_ref[:, hs]
        mu_in = carry[...]
        _tile_scan(a_s, v_s, mu_s, carry, r8, not reverse)
        dh0_ref[...] = carry[0:1, :]
        first = (bi(k) == nb - 1) if reverse else (bi(k) == 0)
        edge = jnp.where(first, jnp.broadcast_to(h0_ref[...], (SUBLANES, w)), halo_ref[...])
        for hd in range(nh):
            hs = slice(hd * LRU_HEAD_DIM, (hd + 1) * LRU_HEAD_DIM)
            if reverse:
                mu_sh = pltpu.roll(jnp.concatenate([mu_in[:, hs], mu_s[:, hs]], axis=0), 1, axis=0)[SUBLANES:]
                h_sh = pltpu.roll(jnp.concatenate([h_ref[:, hs], edge[:, hs]], axis=0), n_ext - 1, axis=0)[:tb]
            else:
                mu_sh = pltpu.roll(jnp.concatenate([mu_s[:, hs], mu_in[:, hs]], axis=0), n_ext - 1, axis=0)[:tb]
                h_sh = pltpu.roll(jnp.concatenate([edge[:, hs], h_ref[:, hs]], axis=0), 1, axis=0)[SUBLANES:]
            xh = xc_ref[:, hs]
            r, i, a = r_s[:, hs], i_s[:, hs], a_s[:, hs]
            sp = _softplus(-lam_ref[:, hs])
            om = 1.0 - a * a
            rs = lax.rsqrt(om)
            lam_t = dh_ref[:, hs] + mu_sh
            dla = lam_t * h_sh * a - lam_t * (i * xh) * (a * a) * rs
            dix = lam_t * (om * rs)
            dpr = dla * (-LRU_C * sp) * r * (1.0 - r)
            dpi = dix * xh * i * (1.0 - i)
            dsp_ref[:, hs] += _colsum(dla * (-LRU_C * r))
            dba_ref[:, hs] += _colsum(dpr)
            dbx_ref[:, hs] += _colsum(dpi)
            dx = dix * i + _bdot_nt(dpr, wa_ref[hd]) + _bdot_nt(dpi, wx_ref[hd])
            xt = xh.T
            dwa_ref[hd] += _bdot(xt, dpr)
            dwx_ref[hd] += _bdot(xt, dpi)
            dxc_ref[:, hs] = dx + add_ref[:, hs] if has_add else dx

    blk = pl.BlockSpec((tb, w), lambda k: (bi(k), 0))
    vec = pl.BlockSpec((1, w), lambda k: (0, 0))
    wsp = pl.BlockSpec((nh, LRU_HEAD_DIM, LRU_HEAD_DIM), lambda k: (0, 0, 0))
    wshape = jax.ShapeDtypeStruct((nh, LRU_HEAD_DIM, LRU_HEAD_DIM), F32)
    vshape = jax.ShapeDtypeStruct((1, w), F32)
    return _call(
        body, name, (nb,),
        [blk, wsp, wsp, vec, vec, vec, vec, blk, pl.BlockSpec((SUBLANES, w), halo), blk, vec] + ([blk] if has_add else []),
        [blk, wsp, wsp, vec, vec, vec, vec],
        [jax.ShapeDtypeStruct((t, w), F32), wshape, wshape, vshape, vshape, vshape, vshape],
        [pltpu.VMEM((tb, w), F32)] * 5 + [pltpu.VMEM((SUBLANES, w), F32)], ("arbitrary",),
        (xc, wa, wx, ba, bx, lam, h0, h, h, dh, dfin, *([addend] if has_add else [])), rider)


def _coords():
    return lax.axis_index("x"), lax.axis_index("y"), lax.axis_index("c")


HALF_ROWS_ALIGN = 32


def _all_gather(shards, name):
    n = len(shards)
    start, wait, sems = _gather_rounds(shards)

    def body(*refs):
        for rnd in range(3):
            start(rnd, refs[:n], refs[n:2 * n], refs[2 * n:])
            wait(rnd, refs[:n], refs[n:2 * n], refs[2 * n:])

    any_spec = pl.BlockSpec(memory_space=pl.ANY)
    return pl.pallas_call(
        body, name=name, in_specs=[any_spec] * n, out_specs=[any_spec] * n,
        out_shape=[jax.ShapeDtypeStruct((N_DEV,) + s.shape, s.dtype) for s in shards], scratch_shapes=sems,
    )(*shards)


def _gather_riders(shards):
    n = len(shards)
    start, wait, sems = _gather_rounds(shards)
    out_shape = [jax.ShapeDtypeStruct((N_DEV,) + s.shape, s.dtype) for s in shards]

    def make(rnd, bufs=()):
        return _Rider(list(shards) + list(bufs), out_shape, {n + i: i for i in range(len(bufs))}, sems,
                      lambda r_in, r_out, s: start(rnd, r_in[:n], r_out, s), lambda r_in, r_out, s: wait(rnd, r_in[:n], r_out, s))
    return make


def _gather_rounds(shards):
    n = len(shards)
    split = [s.ndim >= 2 and s.shape[0] % HALF_ROWS_ALIGN == 0 for s in shards]
    n_copies = 10

    def plan(ins, outs, sem_refs):
        send_sems, recv_sems, local_sems = sem_refs
        x, y, c = _coords()
        slot = lambda p: 4 * p[0] + 2 * p[1] + p[2]
        me, xp, yp, dg = (x, y, c), (1 - x, y, c), (x, 1 - y, c), (1 - x, 1 - y, c)
        other_core = lambda p: (p[0], p[1], 1 - p[2])

        def rows(ref, i, half):
            r = shards[i].shape[0]
            if half == "all" or not split[i]:
                return ref
            return ref.at[pl.ds(0, r // 2)] if half == "a" else ref.at[pl.ds(r // 2, r // 2)]

        def copy(i, j, holder, half, to, own=False):
            dst = rows(outs[i].at[slot(holder)], i, half)
            src = rows(ins[i], i, half) if own else dst
            return pltpu.make_async_remote_copy(src_ref=src, dst_ref=dst, send_sem=send_sems.at[i, j],
                                                recv_sem=recv_sems.at[i, j], device_id=to, device_id_type=MESH)

        rounds = [
            ([(0, me, "a", xp, True), (1, me, "b", yp, True)],
             [(0, xp, "a"), (1, yp, "b")]),
            ([(2, me, "a", yp, True), (3, xp, "a", yp, False), (4, me, "b", xp, True), (5, yp, "b", xp, False)],
             [(2, yp, "a"), (3, dg, "a"), (4, xp, "b"), (5, dg, "b")]),
            ([(6, me, "all", other_core(me), True)] + [(7 + q, p, "all", other_core(me), False) for q, p in enumerate((xp, yp, dg))],
             [(6 + q, other_core(p), "all") for q, p in enumerate((me, xp, yp, dg))]),
        ]
        local = lambda: [pltpu.make_async_copy(ins[i], outs[i].at[slot(me)], local_sems.at[i]) for i in range(n)]
        travels = lambda i, half: half != "b" or split[i]
        return rounds, copy, local, travels, me

    def start(rnd, ins, outs, sem_refs):
        rounds, copy, local, travels, _ = plan(ins, outs, sem_refs)
        if rnd == 0:
            for cp in local():
                cp.start()
        for j, holder, half, to, own in rounds[rnd][0]:
            for i in range(n):
                if travels(i, half):
                    copy(i, j, holder, half, to, own).start()

    def wait(rnd, ins, outs, sem_refs):
        rounds, copy, local, travels, me = plan(ins, outs, sem_refs)
        for j, holder, half in rounds[rnd][1]:
            for i in range(n):
                if travels(i, half):
                    copy(i, j, holder, half, me).wait_recv()
        for j, holder, half, to, own in rounds[rnd][0]:
            for i in range(n):
                if travels(i, half):
                    copy(i, j, holder, half, to, own).wait_send()
        if rnd == 0:
            for cp in local():
                cp.wait()

    sems = [pltpu.SemaphoreType.DMA((n, n_copies)), pltpu.SemaphoreType.DMA((n, n_copies)), pltpu.SemaphoreType.DMA((n,))]
    return start, wait, sems


def _swap_rider(bufs, axes):
    n = len(bufs)

    def copies(ins, outs, sem_refs):
        x, y, c = _coords()
        to = {"x": (1 - x, y, c), "y": (x, 1 - y, c), "c": (x, y, 1 - c)}
        return [pltpu.make_async_remote_copy(src_ref=ins[i], dst_ref=outs[i], send_sem=sem_refs[0].at[i], recv_sem=sem_refs[1].at[i],
                                             device_id=to[axes[i]], device_id_type=MESH) for i in range(n)]

    def start(ins, outs, sem_refs):
        for cp in copies(ins, outs, sem_refs):
            cp.start()

    def wait(ins, outs, sem_refs):
        for cp in copies(ins, outs, sem_refs):
            cp.wait()

    return _Rider(list(bufs), [jax.ShapeDtypeStruct(b.shape, b.dtype) for b in bufs], {},
                  [pltpu.SemaphoreType.DMA((n,)), pltpu.SemaphoreType.DMA((n,))], start, wait)


def _swap(bufs, axes, name):
    n = len(bufs)
    rider = _swap_rider(bufs, axes)

    def body(*refs):
        rider.start(refs[:n], refs[n:2 * n], refs[2 * n:])
        rider.wait(refs[:n], refs[n:2 * n], refs[2 * n:])

    any_spec = pl.BlockSpec(memory_space=pl.ANY)
    return pl.pallas_call(body, name=name, in_specs=[any_spec] * n, out_specs=[any_spec] * n, out_shape=rider.out_shape,
                          scratch_shapes=rider.sems)(*bufs)


def _as2d(a):
    return a.reshape(-1, a.shape[-1])


def _add_f32(a, b, name):
    (out,), _ = _rowwise(lambda r, s: ([r[0].astype(F32) + r[1].astype(F32)], []), [_as2d(a), _as2d(b)], [],
                         [(a.shape[-1], F32)], [], name)
    return out.reshape(a.shape)


class _ReduceScatter:
    def __init__(self, parts, name):
        self.n, self.name, self.rnd = len(parts), name, 0
        self.cur = [p.reshape((2, 2, 2) + p.shape[1:]) for p in parts]

    def stage(self):
        x, y, c = _coords()
        n = self.n
        if self.rnd == 0:
            bufs, idxs, dims, axes = self.cur, [c] * n, [2] * n, ["c"] * n
        elif self.rnd == 1:
            bufs = [p[:, :, :p.shape[2] // 2] for p in self.cur] + [p[:, :, p.shape[2] // 2:] for p in self.cur]
            idxs, dims, axes = [y] * n + [x] * n, [1] * n + [0] * n, ["y"] * n + ["x"] * n
        else:
            bufs, idxs, dims, axes = self.cur, [x] * n + [y] * n, [0] * (2 * n), ["x"] * n + ["y"] * n
        pick = lambda p, idx, dim: lax.dynamic_index_in_dim(p, idx, dim, keepdims=False)
        self.keep = [pick(p, i_, d_) for p, i_, d_ in zip(bufs, idxs, dims)]
        return [pick(p, 1 - i_, d_).astype(BF16) for p, i_, d_ in zip(bufs, idxs, dims)], axes

    def absorb(self, got):
        self.cur = [_add_f32(k, g, f"{self.name}_add{self.rnd}_{i}") for i, (k, g) in enumerate(zip(self.keep, got))]
        self.rnd += 1

    def result(self):
        return [jnp.concatenate([self.cur[i], self.cur[self.n + i]], axis=0) for i in range(self.n)]


def _reduce_scatter(parts, name):
    rs = _ReduceScatter(parts, name)
    for rnd in range(3):
        send, axes = rs.stage()
        rs.absorb(_swap(send, axes, f"{name}_swap{rnd}"))
    return rs.result()


def _pack(arrs):
    flat = jnp.concatenate([a.reshape(-1).astype(F32) for a in arrs])
    rows = -(-flat.shape[0] // LANES)
    rows = -(-rows // 512) * 512
    return jnp.pad(flat, (0, rows * LANES - flat.shape[0])).reshape(rows, LANES)


def _unpack(buf, like):
    flat, out, off = buf.reshape(-1), [], 0
    for a in like:
        out.append(flat[off:off + a.size].reshape(a.shape))
        off += a.size
    return out


class _AllReduce:
    def __init__(self, arrs, name, narrow=()):
        self.n, self.name, self.narrow = len(arrs), name, narrow
        self.wide = [a for i, a in enumerate(arrs) if i not in narrow]
        self.thin = [a for i, a in enumerate(arrs) if i in narrow]
        self.packs = [_pack(self.wide)] + ([_pack(self.thin).astype(BF16)] if self.thin else [])
        self.make = _gather_riders(self.packs)

    def finish(self, gathered):
        sums = []
        for j, g in enumerate(gathered):
            (tot,), _ = _rowwise(lambda r, s: ([functools.reduce(lambda u, v: u + v, r)], []), [g[i] for i in range(N_DEV)], [],
                                 [(LANES, F32)], [], f"{self.name}_sum{j}")
            sums.append(tot)
        out_wide = iter(_unpack(sums[0], self.wide))
        out_thin = iter(_unpack(sums[1], self.thin) if self.thin else [])
        return [next(out_thin) if i in self.narrow else next(out_wide) for i in range(self.n)]


def _all_reduce(arrs, name, narrow=()):
    ar = _AllReduce(arrs, name, narrow)
    return ar.finish(_all_gather(ar.packs, name + "_gather"))


def _adamw(w, g, m, v, name):
    def fn(r, s):
        w_, g_, m_, v_ = r
        m_ = ADAM_B1 * m_ + (1.0 - ADAM_B1) * g_
        v_ = ADAM_B2 * v_ + (1.0 - ADAM_B2) * jnp.square(g_)
        m_hat = m_ / (1.0 - ADAM_B1 ** ADAM_STEP)
        v_hat = v_ / (1.0 - ADAM_B2 ** ADAM_STEP)
        return [-ADAM_LR * (m_hat / (jnp.sqrt(v_hat) + ADAM_EPS) + ADAM_WD * w_), m_, v_], []

    c = w.shape[-1]
    (d, nm, nv), _ = _rowwise(fn, [_as2d(a) for a in (w, g, m, v)], [], [(c, F32)] * 3, [], name)
    return d.reshape(w.shape), nm.reshape(w.shape), nv.reshape(w.shape)


def _norm_mod(x, gain, sc, sh, name):
    def fn(r, s):
        xv = r[0]
        xn = xv * lax.rsqrt(_rowmean(xv * xv) + EPS)
        return [xn * s[0] * (1.0 + s[1]) + s[2]], []
    (h,), _ = _rowwise(fn, [x], [gain, sc, sh], [(x.shape[1], BF16)], [], name)
    return h


def _norm_mod_bwd(x, dh, dres, gain, sc, name):
    d = x.shape[1]

    def fn(r, s):
        xv, dhv = r[0], r[1]
        rs = lax.rsqrt(_rowmean(xv * xv) + EPS)
        xn = xv * rs
        dxn = dhv * (1.0 + s[1]) * s[0]
        dx = rs * (dxn - xn * _rowmean(dxn * xn))
        if dres is not None:
            dx = dx + r[2]
        return [dx], [_colsum(dhv), _colsum(dhv * xn * s[0]), _colsum(dhv * (1.0 + s[1]) * xn)]
    (dx,), accs = _rowwise(fn, [x, dh] + ([dres] if dres is not None else []), [gain, sc], [(d, F32)], [d, d, d], name)
    return dx, accs


def _ada_rows(mod_row, d):
    return [mod_row[:, i * d:(i + 1) * d] for i in range(N_MOD)]


def _to_col_major(t):
    n, c = t.shape
    return t.reshape(n // GRID_W, GRID_W, c).transpose(1, 0, 2).reshape(n, c)


def _to_row_major(t):
    n, c = t.shape
    return t.reshape(GRID_W, n // GRID_W, c).transpose(1, 0, 2).reshape(n, c)


def kernel(x, c, ctx, c_ctx, w_ada, b_ada, norm_mix, norm_ffn, w_in, ssd_conv_w, ssd_conv_b, ssd_dt_bias, ssd_a_log, ssd_d, ssd_norm, w_out_ssd, lru_conv_w, lru_conv_b, lru_w_a, lru_b_a, lru_w_x, lru_b_x, lru_lambda, w_out_lru, w_gate, b_gate, w_o, ffn_w13, ffn_w2, final_norm, loss_target, m_c_ctx, m_w_ada, m_b_ada, m_norm_mix, m_norm_ffn, m_w_in, m_ssd_conv_w, m_ssd_conv_b, m_ssd_dt_bias, m_ssd_a_log, m_ssd_d, m_ssd_norm, m_w_out_ssd, m_lru_conv_w, m_lru_conv_b, m_lru_w_a, m_lru_b_a, m_lru_w_x, m_lru_b_x, m_lru_lambda, m_w_out_lru, m_w_gate, m_b_gate, m_w_o, m_ffn_w13, m_ffn_w2, m_final_norm, v_c_ctx, v_w_ada, v_b_ada, v_norm_mix, v_norm_ffn, v_w_in, v_ssd_conv_w, v_ssd_conv_b, v_ssd_dt_bias, v_ssd_a_log, v_ssd_d, v_ssd_norm, v_w_out_ssd, v_lru_conv_w, v_lru_conv_b, v_lru_w_a, v_lru_b_a, v_lru_w_x, v_lru_b_x, v_lru_lambda, v_w_out_lru, v_w_gate, v_b_gate, v_w_o, v_ffn_w13, v_ffn_w2, v_final_norm):
    names = ["c_ctx", "w_ada", "b_ada", "norm_mix", "norm_ffn", "w_in", "ssd_conv_w", "ssd_conv_b", "ssd_dt_bias", "ssd_a_log",
             "ssd_d", "ssd_norm", "w_out_ssd", "lru_conv_w", "lru_conv_b", "lru_w_a", "lru_b_a", "lru_w_x", "lru_b_x", "lru_lambda",
             "w_out_lru", "w_gate", "b_gate", "w_o", "ffn_w13", "ffn_w2", "final_norm"]
    wts = dict(zip(names, (c_ctx, w_ada, b_ada, norm_mix, norm_ffn, w_in, ssd_conv_w, ssd_conv_b, ssd_dt_bias, ssd_a_log, ssd_d, ssd_norm, w_out_ssd, lru_conv_w, lru_conv_b, lru_w_a, lru_b_a, lru_w_x, lru_b_x, lru_lambda, w_out_lru, w_gate, b_gate, w_o, ffn_w13, ffn_w2, final_norm)))
    mom1 = dict(zip(names, (m_c_ctx, m_w_ada, m_b_ada, m_norm_mix, m_norm_ffn, m_w_in, m_ssd_conv_w, m_ssd_conv_b, m_ssd_dt_bias, m_ssd_a_log, m_ssd_d, m_ssd_norm, m_w_out_ssd, m_lru_conv_w, m_lru_conv_b, m_lru_w_a, m_lru_b_a, m_lru_w_x, m_lru_b_x, m_lru_lambda, m_w_out_lru, m_w_gate, m_b_gate, m_w_o, m_ffn_w13, m_ffn_w2, m_final_norm)))
    mom2 = dict(zip(names, (v_c_ctx, v_w_ada, v_b_ada, v_norm_mix, v_norm_ffn, v_w_in, v_ssd_conv_w, v_ssd_conv_b, v_ssd_dt_bias, v_ssd_a_log, v_ssd_d, v_ssd_norm, v_w_out_ssd, v_lru_conv_w, v_lru_conv_b, v_lru_w_a, v_lru_b_a, v_lru_w_x, v_lru_b_x, v_lru_lambda, v_w_out_lru, v_w_gate, v_b_gate, v_w_o, v_ffn_w13, v_ffn_w2, v_final_norm)))

    xi, yi, ci_ = _coords()
    me = 4 * xi + 2 * yi + ci_
    x2, tgt, ctx2 = x[0], loss_target[0], ctx[0]
    t, d = x2.shape
    tc = ctx2.shape[0]
    ws = N_DEV * w_out_ssd.shape[1]
    wl = N_DEV * w_out_lru.shape[1]
    heads = ws // SSD_HEAD_DIM
    gn = SSD_GROUPS * SSD_STATE
    nh = wl // LRU_HEAD_DIM
    ff = N_DEV * ffn_w2.shape[1]
    ada_cols = w_ada.shape[2]

    smalls = [c, ssd_conv_w[0], lru_conv_w[0], lru_b_a[0], lru_b_x[0], lru_lambda[0]]
    got = _all_gather(smalls + [w_in[0].astype(BF16)], "gather_weights")
    gate_g = _gather_riders([w_gate[0].astype(BF16)])
    late_a = _gather_riders([ffn_w13[0].astype(BF16)])
    late_b = _gather_riders([w.astype(BF16) for w in (ffn_w2[0], w_out_ssd[0], w_out_lru[0], w_o[0])])
    c_all = got[0].reshape(N_DEV, d)
    colcat = lambda g: jnp.moveaxis(g, 0, -2).reshape(g.shape[1:-1] + (N_DEV * g.shape[-1],))
    conv_w_s, conv_w_l, lru_ba, lru_bx, lru_lam = (colcat(g) for g in got[1:6])
    w_in_f = colcat(got[6])
    o_z, o_xbc, o_dt, o_xr, o_yr = 0, ws, 2 * ws + 2 * gn, 2 * ws + 2 * gn + 2 * heads, 2 * ws + 2 * gn + 2 * heads + wl
    w_z, w_xbc = w_in_f[:, o_z:o_xbc], w_in_f[:, o_xbc:o_dt]
    w_dt = jnp.pad(w_in_f[:, o_dt:o_xr], ((0, 0), (0, LANES - 2 * heads)))
    w_xr, w_yr = w_in_f[:, o_xr:o_yr], w_in_f[:, o_yr:]

    cond = jnp.concatenate([c_all, c_ctx[None, :], jnp.zeros((16 - N_DEV - 1, d), F32)], axis=0)
    cond_act = _silu(cond)
    mod_l = _mm([(cond_act, w_ada[0])], "nn", F32, "ada_fwd")
    (mod_g,) = _all_gather([mod_l], "gather_mod")
    mod_full = colcat(mod_g) + b_ada
    sh_m, sc_m, g_m, sh_f, sc_f, g_f = _ada_rows(lax.dynamic_slice_in_dim(mod_full, me, 1, 0), d)
    csh_m, csc_m = _ada_rows(mod_full[N_DEV:N_DEV + 1], d)[:2]

    bias_s, alog_s = ssd_dt_bias[0], ssd_a_log[0]
    d_full = jnp.repeat(ssd_d[0], SSD_HEAD_DIM)[None, :]
    zeros_state = jnp.zeros((heads // 2, SSD_STATE, LANES), F32)
    zeros_l = jnp.zeros((1, wl), F32)
    wa, wx = lru_w_a[0], lru_w_x[0]

    def mixer_inputs(hh, tag, full, rider=None):
        xbc = _mm([(hh, w_xbc)], "nn", F32, f"{tag}_proj_xbc", rider=rider)
        if rider is not None:
            xbc, rode = xbc
        dtr = _mm([(hh, w_dt)], "nn", F32, f"{tag}_proj_dt")[:, :2 * heads]
        xr = _mm([(hh, w_xr)], "nn", F32, f"{tag}_proj_xr")
        cx = _conv(xbc, conv_w_s, ssd_conv_b, CONV_LEFT, True, f"{tag}_ssd_conv")
        xr_seq = _to_col_major(xr) if full else xr
        xc = _conv(xr_seq, conv_w_l, lru_conv_b, CONV_LEFT, False, f"{tag}_lru_conv")
        return (xbc, dtr, xr_seq, cx, xc) if rider is None else ((xbc, dtr, xr_seq, cx, xc), rode)

    def lru_args(dirn):
        return wa[dirn], wx[dirn], lru_ba[dirn:dirn + 1], lru_bx[dirn:dirn + 1], lru_lam[dirn:dirn + 1]

    hc = _norm_mod(ctx2, norm_mix, csc_m, csh_m, "ctx_norm")
    c_xbc, c_dtr, c_xr_seq, c_cx, c_xc = mixer_inputs(hc, "ctx", False)
    c_ssd, gate_bufs = [], None
    for dirn in (0, 1):
        res = _ssd_fwd(c_cx, c_dtr[:, dirn * heads:(dirn + 1) * heads], bias_s[dirn:dirn + 1], alog_s[dirn:dirn + 1],
                       zeros_state, bool(dirn), f"ctx_ssd_fwd{dirn}", rider=None if dirn else gate_g(0))
        (_, se, sf), gate_bufs = res if dirn == 0 else (res, gate_bufs)
        c_ssd.append((se, sf))
    c_lru = []
    for dirn in (0, 1):
        res = _lru_fwd(c_xc, *lru_args(dirn), zeros_l, None, bool(dirn), f"ctx_lru_fwd{dirn}")
        c_lru.append((res[0], res[1]))

    h = _norm_mod(x2, norm_mix, sc_m, sh_m, "lat_norm")
    z, gate_bufs = _mm([(h, w_z)], "nn", BF16, "lat_proj_z", rider=gate_g(1, gate_bufs))
    yr, (wg_gate,) = _mm([(h, w_yr)], "nn", BF16, "lat_proj_yr", rider=gate_g(2, gate_bufs))
    w_gf = colcat(wg_gate)
    w_gs, w_gr = w_gf[:, :d], w_gf[:, d:]
    gp_s = _mm([(h, w_gs)], "nn", BF16, "lat_proj_gs")
    gp_r = _mm([(h, w_gr)], "nn", BF16, "lat_proj_gr")
    (l_xbc, l_dtr, l_xr_seq, l_cx, l_xc), late = mixer_inputs(h, "lat", True, rider=_merge_riders(late_a(0), late_b(0)))
    got_a, got_b = late[:1], late[1:]
    ssd_args = lambda dirn: (l_cx, l_dtr[:, dirn * heads:(dirn + 1) * heads], bias_s[dirn:dirn + 1], alog_s[dirn:dirn + 1], c_ssd[dirn][1], bool(dirn))
    (yv0, se0, _), got_a = _ssd_fwd(*ssd_args(0), "lat_ssd_fwd0", rider=late_a(1, got_a))
    (yv1, se1, _), got_b = _ssd_fwd(*ssd_args(1), "lat_ssd_fwd1", rider=late_b(1, got_b))
    l_ssd = [(yv0, se0), (yv1, se1)]
    (hf, _), late = _lru_fwd(l_xc, *lru_args(0), c_lru[0][1], None, False, "lat_lru_fwd0", rider=_merge_riders(late_a(2, got_a), late_b(2, got_b)))
    (wg_13,), (wg_2, wg_os, wg_ol, wg_o) = late[:1], late[1:]
    hb, _, r_seq = _lru_fwd(l_xc, *lru_args(1), c_lru[1][1], hf, True, "lat_lru_fwd1")
    w_os, w_ol, w_oo, w_2 = wg_os.reshape(ws, d), wg_ol.reshape(wl, d), wg_o.reshape(d, d), wg_2.reshape(ff, d)
    w_13f = colcat(wg_13)
    w_1, w_3 = w_13f[:, :ff], w_13f[:, ff:]

    def ssd_out(r, s):
        yf, yb, cx_, z_ = r
        yv = yf + yb + s[0] * cx_[:, :ws]
        yz = yv * _silu(z_)
        return [yz * lax.rsqrt(_rowmean(yz * yz) + EPS) * s[1]], []
    (ys,), _ = _rowwise(ssd_out, [l_ssd[0][0], l_ssd[1][0], l_cx, z], [d_full, ssd_norm], [(ws, BF16)], [], "lat_ssd_out")
    o_s = _mm([(ys, w_os)], "nn", BF16, "lat_out_ssd")

    r_out = _to_row_major(r_seq)
    (gr,), _ = _rowwise(lambda r, s: ([r[0] * _gelu(r[1])], []), [r_out, yr], [], [(wl, BF16)], [], "lat_lru_out")
    o_r = _mm([(gr, w_ol)], "nn", BF16, "lat_out_lru")

    bg_s, bg_r = b_gate[:, :d], b_gate[:, d:]
    (mix,), _ = _rowwise(lambda r, s: ([_sigmoid(r[0] + s[0]) * r[2] + _sigmoid(r[1] + s[1]) * r[3]], []),
                         [gp_s, gp_r, o_s, o_r], [bg_s, bg_r], [(d, BF16)], [], "lat_merge")
    mixo = _mm([(mix, w_oo)], "nn", F32, "lat_out_o")

    def resid_norm(r, s):
        x1v = r[0] + s[0] * r[1]
        xn = x1v * lax.rsqrt(_rowmean(x1v * x1v) + EPS)
        return [x1v, xn * s[1] * (1.0 + s[2]) + s[3]], []
    (x1, h2), _ = _rowwise(resid_norm, [x2, mixo], [g_m, norm_ffn, sc_f, sh_f], [(d, F32), (d, BF16)], [], "lat_resid1")
    fg = _mm([(h2, w_1)], "nn", BF16, "ffn_gate")
    fu = _mm([(h2, w_3)], "nn", BF16, "ffn_up")
    (act,), _ = _rowwise(lambda r, s: ([_silu(r[0]) * r[1]], []), [fg, fu], [], [(ff, BF16)], [], "ffn_act")
    fo = _mm([(act, w_2)], "nn", F32, "ffn_down")

    fn_gain = final_norm[None, :]

    def head(r, s):
        x1v, fv, tv = r
        xv = x1v + s[0] * fv
        rs = lax.rsqrt(_rowmean(xv * xv) + EPS)
        xn = xv * rs
        err = xn * s[1] - tv
        dyv = err * (1.0 / d)
        dxn = dyv * s[1]
        dxv = rs * (dxn - xn * _rowmean(dxn * xn))
        return [dxv, dxv * s[0]], [_colsum(err * err), _colsum(dyv * xn), _colsum(dxv * fv)]
    (dx2, dfo), (sq, g_final, dg_f) = _rowwise(head, [x1, fo, tgt], [g_f, fn_gain], [(d, F32), (d, BF16)], [d, d, d], "loss_head")
    loss = lax.psum(0.5 * jnp.sum(sq) / d, ("x", "y", "c"))

    dact = _mm([(dfo, w_2)], "nt", BF16, "ffn_down_dx")
    gw_2 = _mm([(act, dfo)], "tn", F32, "ffn_down_dw")
    (dfg, dfu), _ = _rowwise(lambda r, s: ([r[0] * r[2] * _dsilu(r[1]), r[0] * _silu(r[1])], []), [dact, fg, fu], [],
                             [(ff, BF16), (ff, BF16)], [], "ffn_act_bwd")
    gw_13 = _mm([(h2, dfg)], "tn", F32, "ffn_gate_dw", slots=(ff // 4, N_DEV, 0, None))
    gw_13 = _mm([(h2, dfu)], "tn", F32, "ffn_up_dw", slots=(ff // 4, N_DEV, N_DEV // 2, gw_13))
    slots = lambda g: jnp.moveaxis(g.reshape(g.shape[0], N_DEV, g.shape[1] // N_DEV), 1, 0)
    rslots = lambda g: g.reshape(N_DEV, g.shape[0] // N_DEV, g.shape[1])
    ffn_rs = _ReduceScatter([gw_13, rslots(gw_2)], "ffn_rs")
    dh2, got = _mm([(dfg, w_1), (dfu, w_3)], "nt", F32, "ffn_dx", rider=_swap_rider(*ffn_rs.stage()))
    ffn_rs.absorb(got)
    dx1, (dsh_f, dsc_f, g_norm_ffn) = _norm_mod_bwd(x1, dh2, dx2, norm_ffn, sc_f, "lat_norm_ffn_bwd")

    (dmixo,), (dg_m,) = _rowwise(lambda r, s: ([s[0] * r[0]], [_colsum(r[0] * r[1])]), [dx1, mixo], [g_m], [(d, BF16)], [d], "lat_resid1_bwd")
    dmix = _mm([(dmixo, w_oo)], "nt", BF16, "lat_out_o_dx")
    gw_o = _mm([(mix, dmixo)], "tn", F32, "lat_out_o_dw")

    def merge_bwd(r, s):
        dm, gps, gpr, os_, or_ = r
        g1, g2 = _sigmoid(gps + s[0]), _sigmoid(gpr + s[1])
        dg1, dg2 = dm * os_ * g1 * (1.0 - g1), dm * or_ * g2 * (1.0 - g2)
        return [dm * g1, dm * g2, dg1, dg2], [_colsum(dg1), _colsum(dg2)]
    (do_s, do_r, dgp_s, dgp_r), (gb_gs, gb_gr) = _rowwise(merge_bwd, [dmix, gp_s, gp_r, o_s, o_r], [bg_s, bg_r],
                                                            [(d, BF16)] * 4, [d, d], "lat_merge_bwd")
    gw_gate = _mm([(h, dgp_s)], "tn", F32, "lat_gs_dw", slots=(d // 4, N_DEV, 0, None))
    gw_gate = _mm([(h, dgp_r)], "tn", F32, "lat_gr_dw", slots=(d // 4, N_DEV, N_DEV // 2, gw_gate))

    dgr = _mm([(do_r, w_ol)], "nt", BF16, "lat_out_lru_dx")
    gw_ol = _mm([(gr, do_r)], "tn", F32, "lat_out_lru_dw")
    (dr_out, dyr), _ = _rowwise(lambda r, s: ([r[0] * _gelu(r[2]), r[0] * r[1] * _dgelu(r[2])], []), [dgr, r_out, yr], [],
                                [(wl, F32), (wl, BF16)], [], "lat_lru_out_bwd")
    dr_seq = _to_col_major(dr_out)
    dys = _mm([(do_s, w_os)], "nt", BF16, "lat_out_ssd_dx")
    gw_os = _mm([(ys, do_s)], "tn", F32, "lat_out_ssd_dw")
    mid_rs = _ReduceScatter([rslots(gw_os), rslots(gw_ol), gw_gate, rslots(gw_o)], "mid_rs")
    n_ffn = 2 * ffn_rs.n
    lb0, got = _lru_bwd(l_xc, *lru_args(0), c_lru[0][1], hf, dr_seq, zeros_l, None, False, "lat_lru_bwd0",
                        rider=_merge_riders(_swap_rider(*ffn_rs.stage()), _swap_rider(*mid_rs.stage())))
    ffn_rs.absorb(got[:n_ffn])
    mid_rs.absorb(got[n_ffn:])
    lb1, got = _lru_bwd(l_xc, *lru_args(1), c_lru[1][1], hb, dr_seq, zeros_l, lb0[0], True, "lat_lru_bwd1",
                        rider=_merge_riders(_swap_rider(*ffn_rs.stage()), _swap_rider(*mid_rs.stage())))
    ffn_rs.absorb(got[:n_ffn])
    mid_rs.absorb(got[n_ffn:])
    dxr_seq, g_lconv_w, g_lconv_b = _conv_bwd(l_xr_seq, [lb1[0]], lambda v: v[0], conv_w_l, lru_conv_b, False, "lat_lru_conv_bwd")
    dxr = _to_row_major(dxr_seq)


    def ssd_out_bwd(r, s):
        yf, yb, cx_, z_, dyn = r
        xs_ = cx_[:, :ws]
        yv = yf + yb + s[0] * xs_
        sz = _silu(z_)
        yz = yv * sz
        rs = lax.rsqrt(_rowmean(yz * yz) + EPS)
        yzn = yz * rs
        dyzn = dyn * s[1]
        dyz = rs * (dyzn - yzn * _rowmean(dyzn * yzn))
        dyv = dyz * sz
        return [dyv, dyz * yv * _dsilu(z_), dyv * s[0]], [_colsum(dyv * xs_), _colsum(dyn * yzn)]
    (dy_ssd, dz, dxs_skip), (g_dfull, g_ssd_norm) = _rowwise(ssd_out_bwd, [l_ssd[0][0], l_ssd[1][0], l_cx, z, dys], [d_full, ssd_norm],
                                                              [(ws, F32), (ws, BF16), (ws, F32)], [ws, ws], "lat_ssd_out_bwd")
    sb0, got = _ssd_bwd(l_cx, l_dtr[:, :heads], bias_s[0:1], alog_s[0:1], dy_ssd, l_ssd[0][1], zeros_state, False, "lat_ssd_bwd0",
                        rider=_swap_rider(*mid_rs.stage()))
    mid_rs.absorb(got)
    sb = [sb0, _ssd_bwd(l_cx, l_dtr[:, heads:], bias_s[1:2], alog_s[1:2], dy_ssd, l_ssd[1][1], zeros_state, True, "lat_ssd_bwd1")]
    dxbc, g_sconv_w, g_sconv_b = _conv_bwd(
        l_xbc, [dxs_skip, sb[0][0], sb[1][0], sb[0][1], sb[1][1], sb[0][2], sb[1][2]],
        lambda v: jnp.concatenate([v[0] + v[1] + v[2], v[3] + v[4], v[5] + v[6]], axis=1), conv_w_s, ssd_conv_b, True, "lat_ssd_conv_bwd")
    ddt = jnp.pad(jnp.concatenate([sb[0][3], sb[1][3]], axis=1), ((0, 0), (0, LANES - 2 * heads)))

    zeros_cy = jnp.zeros((tc, ws), F32)
    csb = [_ssd_bwd(c_cx, c_dtr[:, dirn * heads:(dirn + 1) * heads], bias_s[dirn:dirn + 1], alog_s[dirn:dirn + 1],
                    zeros_cy, c_ssd[dirn][0], sb[dirn][6], bool(dirn), f"ctx_ssd_bwd{dirn}") for dirn in (0, 1)]
    c_dxbc, cg_sconv_w, cg_sconv_b = _conv_bwd(
        c_xbc, [csb[0][0], csb[1][0], csb[0][1], csb[1][1], csb[0][2], csb[1][2]],
        lambda v: jnp.concatenate([v[0] + v[1], v[2] + v[3], v[4] + v[5]], axis=1), conv_w_s, ssd_conv_b, True, "ctx_ssd_conv_bwd")
    c_ddt = jnp.pad(jnp.concatenate([csb[0][3], csb[1][3]], axis=1), ((0, 0), (0, LANES - 2 * heads)))
    zeros_ch = jnp.zeros((tc, wl), F32)
    clb0 = _lru_bwd(c_xc, *lru_args(0), zeros_l, c_lru[0][0], zeros_ch, lb0[6], None, False, "ctx_lru_bwd0")
    clb1 = _lru_bwd(c_xc, *lru_args(1), zeros_l, c_lru[1][0], zeros_ch, lb1[6], clb0[0], True, "ctx_lru_bwd1")
    c_dxr, cg_lconv_w, cg_lconv_b = _conv_bwd(c_xr_seq, [clb1[0]], lambda v: v[0], conv_w_l, lru_conv_b, False, "ctx_lru_conv_bwd")
    dhc = _mm([(c_dxbc, w_xbc), (c_ddt, w_dt), (c_dxr, w_xr)], "nt", F32, "ctx_proj_dx")
    _, (dcsh_m, dcsc_m, cg_norm_mix) = _norm_mod_bwd(ctx2, dhc, None, norm_mix, csc_m, "ctx_norm_bwd")

    gw_z = _mm([(h, dz)], "tn", F32, "lat_z_dw")
    gw_yr = _mm([(h, dyr)], "tn", F32, "lat_yr_dw")
    gw_xbc = _mm([(h, dxbc), (hc, c_dxbc)], "tn", F32, "proj_xbc_dw")
    gw_dt = _mm([(h, ddt), (hc, c_ddt)], "tn", F32, "proj_dt_dw")[:, :2 * heads]
    gw_xr = _mm([(h, dxr), (hc, c_dxr)], "tn", F32, "proj_xr_dw")
    in_rs = _ReduceScatter([slots(jnp.concatenate([gw_z, gw_xbc, gw_dt, gw_xr, gw_yr], axis=1))], "in_rs")
    small_parts = [
        g_norm_ffn, g_sconv_w + cg_sconv_w, g_sconv_b + cg_sconv_b,
        jnp.concatenate([sb[0][5] + csb[0][5], sb[1][5] + csb[1][5]], axis=0),
        jnp.concatenate([sb[0][4] + csb[0][4], sb[1][4] + csb[1][4]], axis=0),
        g_dfull, g_ssd_norm, g_lconv_w + cg_lconv_w, g_lconv_b + cg_lconv_b,
        jnp.stack([lb0[1] + clb0[1], lb1[1] + clb1[1]]), jnp.concatenate([lb0[3] + clb0[3], lb1[3] + clb1[3]], axis=0),
        jnp.stack([lb0[2] + clb0[2], lb1[2] + clb1[2]]), jnp.concatenate([lb0[4] + clb0[4], lb1[4] + clb1[4]], axis=0),
        jnp.concatenate([lb0[5] + clb0[5], lb1[5] + clb1[5]], axis=0),
        jnp.concatenate([gb_gs, gb_gr], axis=1), g_final,
    ]
    small_ar = _AllReduce(small_parts, "small_grads", narrow=(9, 11))
    dh, bufs = None, ()
    for rnd, pairs in enumerate(([(dz, w_z), (dxbc, w_xbc), (ddt, w_dt)], [(dxr, w_xr), (dyr, w_yr)], [(dgp_s, w_gs), (dgp_r, w_gr)])):
        send, axes = in_rs.stage()
        dh, got = _mm(pairs, "nt", F32, f"lat_proj_dx{rnd}", init=dh, rider=_merge_riders(_swap_rider(send, axes), small_ar.make(rnd, bufs)))
        in_rs.absorb(got[:len(send)])
        bufs = got[len(send):]
    grad_x, (dsh_m, dsc_m, g_norm_mix) = _norm_mod_bwd(x2, dh, dx1, norm_mix, sc_m, "lat_norm_bwd")

    zero_d = jnp.zeros((1, d), F32)
    dmod_rows = jnp.concatenate([jnp.concatenate([dsh_m, dsc_m, dg_m, dsh_f, dsc_f, dg_f], axis=1),
                                 jnp.concatenate([dcsh_m, dcsc_m, zero_d, zero_d, zero_d, zero_d], axis=1)], axis=0)
    (dmod_g,) = _all_gather([dmod_rows], "gather_dmod")
    dmod_c = functools.reduce(lambda u, v: u + v, [dmod_g[i, 1] for i in range(N_DEV)])
    dmod9 = jnp.concatenate([dmod_g[:, 0], dmod_c[None, :], jnp.zeros((16 - N_DEV - 1, N_MOD * d), F32)], axis=0)
    g_b_ada = jnp.sum(dmod9, axis=0, keepdims=True)
    dmod_l = lax.dynamic_slice_in_dim(dmod9, me * ada_cols, ada_cols, 1)
    g_w_ada = _mm([(cond_act, dmod_l)], "tn", F32, "ada_dw")
    dcond_l = _mm([(dmod_l, w_ada[0])], "nt", F32, "ada_dx")

    (s_nffn, s_sconv_w, s_sconv_b, s_dtb, s_da, s_dfull, s_snorm, s_lconv_w, s_lconv_b,
     s_wa, s_ba, s_wx, s_bx, s_dsp, s_bgate, s_final) = small_ar.finish(bufs)
    s_cctx, s_nmix = _all_reduce([dcond_l[N_DEV:N_DEV + 1], g_norm_mix + cg_norm_mix], "last_grads")
    shard = lambda a, n_: lax.dynamic_slice_in_dim(a, me * n_, n_, a.ndim - 1)
    grads = {
        "c_ctx": (s_cctx * _dsilu(c_ctx[None, :]))[0],
        "w_ada": g_w_ada[None], "b_ada": g_b_ada, "norm_mix": s_nmix, "norm_ffn": s_nffn,
        "ssd_conv_w": shard(s_sconv_w, ssd_conv_w.shape[2])[None], "ssd_conv_b": s_sconv_b,
        "ssd_dt_bias": s_dtb[None], "ssd_a_log": (s_da * (-jnp.exp(alog_s)))[None],
        "ssd_d": jnp.sum(s_dfull.reshape(heads, SSD_HEAD_DIM), axis=1)[None], "ssd_norm": s_snorm,
        "lru_conv_w": shard(s_lconv_w, lru_conv_w.shape[2])[None], "lru_conv_b": s_lconv_b,
        "lru_w_a": s_wa[None], "lru_b_a": shard(s_ba, lru_b_a.shape[2])[None],
        "lru_w_x": s_wx[None], "lru_b_x": shard(s_bx, lru_b_x.shape[2])[None],
        "lru_lambda": shard(s_dsp * (-_sigmoid(-lru_lam)), lru_lambda.shape[2])[None],
        "b_gate": s_bgate, "final_norm": s_final[0],
    }

    for n_, g in zip(["w_in", "w_out_ssd", "w_out_lru", "w_gate", "w_o", "ffn_w13", "ffn_w2"],
                     in_rs.result() + mid_rs.result() + ffn_rs.result()):
        grads[n_] = g[None]

    big_names = ["w_ada", "w_in", "w_out_ssd", "w_out_lru", "w_gate", "w_o", "ffn_w13", "ffn_w2", "lru_w_a", "lru_w_x"]
    small_names = [n_ for n_ in names if n_ not in big_names]
    delta, new_m, new_v = {}, {}, {}
    for n_ in big_names:
        delta[n_], new_m[n_], new_v[n_] = _adamw(wts[n_], grads[n_], mom1[n_], mom2[n_], "adamw_" + n_)
    packed = [_pack([src[n_] for n_ in small_names]) for src in (wts, grads, mom1, mom2)]
    outs = _adamw(*packed, "adamw_small")
    like = [wts[n_] for n_ in small_names]
    for dst, buf in zip((delta, new_m, new_v), outs):
        dst.update(zip(small_names, _unpack(buf, like)))

    return (loss, grad_x[None], *[grads[n_].reshape(wts[n_].shape) for n_ in names], *[delta[n_] for n_ in names],
            *[new_m[n_] for n_ in names], *[new_v[n_] for n_ in names])
```

```python
import functools

import numpy as np
import jax
import jax.numpy as jnp
from jax import lax
from jax.experimental import pallas as pl
from jax.experimental.pallas import tpu as pltpu

F32, BF16 = jnp.float32, jnp.bfloat16
MESH = pl.DeviceIdType.MESH
N_DEV = 8
VMEM_LIMIT_BYTES = 48 * 1024 * 1024
ROW_BUDGET_BYTES = 12 * 1024 * 1024
LANES, SUBLANES = 128, 8

EPS = 1e-6
GRID_W = 64
SSD_HEAD_DIM = 64
SSD_GROUPS = 4
SSD_STATE = 128
SSD_CHUNK = 128
CONV_W = 4
CONV_LEFT = 2
LRU_HEAD_DIM = 128
LRU_C = 8.0
LRU_BLOCK = 128
N_MOD = 6
ADAM_LR, ADAM_B1, ADAM_B2, ADAM_EPS, ADAM_WD, ADAM_STEP = 0.001, 0.9, 0.999, 1e-08, 0.01, 10
NEG_BIG = -1e30


def _params(sem):
    return pltpu.CompilerParams(dimension_semantics=sem, vmem_limit_bytes=VMEM_LIMIT_BYTES)


def _tile(dim, pref, align):
    if dim <= pref:
        return dim
    t = (pref // align) * align
    while t >= align:
        if dim % t == 0:
            return t
        t -= align
    return dim


class _Rider:
    def __init__(self, ins, out_shape, aliases, sems, start, wait):
        self.ins, self.out_shape, self.aliases, self.sems, self.start, self.wait = ins, out_shape, aliases, sems, start, wait


def _merge_riders(a, b):
    ai, ao, asem = len(a.ins), len(a.out_shape), len(a.sems)
    aliases = dict(a.aliases)
    aliases.update({ai + i: ao + o for i, o in b.aliases.items()})

    def both(fa, fb):
        def run(ins, outs, sems):
            fa(ins[:ai], outs[:ao], sems[:asem])
            fb(ins[ai:], outs[ao:], sems[asem:])
        return run

    return _Rider(list(a.ins) + list(b.ins), list(a.out_shape) + list(b.out_shape), aliases, list(a.sems) + list(b.sems),
                  both(a.start, b.start), both(a.wait, b.wait))


def _call(body, name, grid, in_specs, out_specs, out_shape, scratch_shapes, sem, operands, rider=None):
    if rider is None:
        return pl.pallas_call(body, name=name, grid=grid, in_specs=in_specs, out_specs=out_specs, out_shape=out_shape,
                              scratch_shapes=scratch_shapes, compiler_params=_params(sem))(*operands)
    n_in, n_out, n_scr = len(in_specs), len(out_specs), len(scratch_shapes)
    ri, ro = len(rider.ins), len(rider.out_shape)
    a, b = n_in, n_in + ri
    c, d = b + n_out, b + n_out + ro
    e = d + n_scr

    def carried(*refs):
        ids = [pl.program_id(i) for i in range(len(grid))]
        first = functools.reduce(lambda u, v: u & v, [i == 0 for i in ids])
        last = functools.reduce(lambda u, v: u & v, [i == g - 1 for i, g in zip(ids, grid)])

        @pl.when(first)
        def _():
            rider.start(refs[a:b], refs[c:d], refs[e:])

        body(*refs[:a], *refs[b:c], *refs[d:e])

        @pl.when(last)
        def _():
            rider.wait(refs[a:b], refs[c:d], refs[e:])

    any_spec = pl.BlockSpec(memory_space=pl.ANY)
    res = pl.pallas_call(
        carried, name=name, grid=grid, in_specs=list(in_specs) + [any_spec] * ri, out_specs=list(out_specs) + [any_spec] * ro,
        out_shape=list(out_shape) + list(rider.out_shape), scratch_shapes=list(scratch_shapes) + list(rider.sems),
        input_output_aliases={n_in + i: n_out + o for i, o in rider.aliases.items()},
        compiler_params=_params(("arbitrary",) * len(grid)),
    )(*operands, *rider.ins)
    return res[:n_out], res[n_out:]


def _bdot(a, b):
    return jnp.dot(a.astype(BF16), b.astype(BF16), preferred_element_type=F32)


def _bdot_nt(a, b):
    return lax.dot_general(a.astype(BF16), b.astype(BF16), (((1,), (1,)), ((), ())), preferred_element_type=F32)


def _bdot_tn(a, b):
    return lax.dot_general(a.astype(BF16), b.astype(BF16), (((0,), (0,)), ((), ())), preferred_element_type=F32)


def _hdot(a, b):
    return jnp.dot(a, b, preferred_element_type=F32, precision=lax.Precision.HIGHEST)


def _softplus(v):
    return jnp.maximum(v, 0.0) + jnp.log(1.0 + jnp.exp(-jnp.abs(v)))


def _sigmoid(v):
    return 0.5 * jnp.tanh(0.5 * v) + 0.5


def _silu(v):
    return v * _sigmoid(v)


def _dsilu(v):
    s = _sigmoid(v)
    return s * (1.0 + v * (1.0 - s))


GELU_C0 = 0.7978845608028654
GELU_C1 = 0.044715


def _gelu(v):
    return 0.5 * v * (1.0 + jnp.tanh(GELU_C0 * (v + GELU_C1 * v * v * v)))


def _dgelu(v):
    t = jnp.tanh(GELU_C0 * (v + GELU_C1 * v * v * v))
    return 0.5 * (1.0 + t) + 0.5 * v * (1.0 - t * t) * GELU_C0 * (1.0 + 3.0 * GELU_C1 * v * v)


MM_TM, MM_TN = 1024, 1408
MM_TK = {1: 2816, 2: 1536}


def _mm(pairs, mode, out_dtype, name, init=None, rider=None, slots=None):
    a0, b0 = pairs[0]
    if mode == "nn":
        m, n, ks = a0.shape[0], b0.shape[1], [a.shape[1] for a, _ in pairs]
    elif mode == "nt":
        m, n, ks = a0.shape[0], b0.shape[0], [a.shape[1] for a, _ in pairs]
    else:
        m, n, ks = a0.shape[1], b0.shape[1], [a.shape[0] for a, _ in pairs]
    tm = _tile(m, MM_TM, LANES if mode == "tn" else 16)
    tn = _tile(n, MM_TN, LANES) if slots is None else slots[0]
    tk = MM_TK.get(sum(k > 512 for k in ks), 512)
    tks = [_tile(k, tk, 16 if mode == "tn" else LANES) for k in ks]
    steps = [k // t for k, t in zip(ks, tks)]
    offs = [sum(steps[:i]) for i in range(len(pairs))]
    total = sum(steps)
    npairs = len(pairs)

    def kk(i):
        return lambda k: jnp.clip(k - offs[i], 0, steps[i] - 1)

    in_specs, operands = [], []
    for i, (a, b) in enumerate(pairs):
        kf = kk(i)
        if mode == "nn":
            in_specs.append(pl.BlockSpec((tm, tks[i]), lambda mi, ni, k, kf=kf: (mi, kf(k))))
            in_specs.append(pl.BlockSpec((tks[i], tn), lambda mi, ni, k, kf=kf: (kf(k), ni)))
        elif mode == "nt":
            in_specs.append(pl.BlockSpec((tm, tks[i]), lambda mi, ni, k, kf=kf: (mi, kf(k))))
            in_specs.append(pl.BlockSpec((tn, tks[i]), lambda mi, ni, k, kf=kf: (ni, kf(k))))
        else:
            in_specs.append(pl.BlockSpec((tks[i], tm), lambda mi, ni, k, kf=kf: (kf(k), mi)))
            in_specs.append(pl.BlockSpec((tks[i], tn), lambda mi, ni, k, kf=kf: (kf(k), ni)))
        operands += [a, b]
    dot = {"nn": _bdot, "nt": _bdot_nt, "tn": _bdot_tn}[mode]
    if init is not None:
        in_specs.append(pl.BlockSpec((tm, tn), lambda mi, ni, k: (mi, ni)))
        operands.append(init)
    assert out_dtype == F32 or total == 1

    def body(*refs):
        out_ref = refs[-1]
        k = pl.program_id(2)

        def first_step():
            p = dot(refs[0][...], refs[1][...])
            return p if init is None else p + refs[2 * npairs][...]

        if total == 1:
            out_ref[...] = first_step().astype(out_dtype)
            return

        @pl.when(k == 0)
        def _():
            out_ref[...] = first_step()

        for i in range(npairs):
            first = 1 if i == 0 else offs[i]
            if offs[i] + steps[i] > first:
                @pl.when((k >= first) & (k < offs[i] + steps[i]))
                def _(i=i):
                    out_ref[...] += dot(refs[2 * i][...], refs[2 * i + 1][...])

    if slots is not None:
        width, n_slots, first_slot, into = slots
        assert rider is None and init is None and n % width == 0
        aliases = {}
        if into is not None:
            in_specs.append(pl.BlockSpec(memory_space=pl.ANY))
            operands.append(into)
            aliases = {len(operands) - 1: 0}
        return pl.pallas_call(
            body, name=name, grid=(m // tm, n // tn, total), in_specs=in_specs,
            out_specs=pl.BlockSpec((None, tm, tn), lambda mi, ni, k: (ni + first_slot, mi, 0)),
            out_shape=jax.ShapeDtypeStruct((n_slots, m, width), out_dtype), input_output_aliases=aliases,
            compiler_params=_params(("parallel", "parallel", "arbitrary")))(*operands)
    res = _call(body, name, (m // tm, n // tn, total), in_specs, [pl.BlockSpec((tm, tn), lambda mi, ni, k: (mi, ni))],
                [jax.ShapeDtypeStruct((m, n), out_dtype)], [], ("parallel", "parallel", "arbitrary"), operands, rider)
    return res[0] if rider is None else (res[0][0], res[1])


def _rowwise(fn, rows, small, outs, accs, name):
    t = rows[0].shape[0]
    per_row = sum(r.shape[1] * r.dtype.itemsize for r in rows) + sum(c * jnp.dtype(d).itemsize for c, d in outs)
    tm = SUBLANES
    while tm * 2 <= 1024 and t % (tm * 2) == 0 and tm * 2 * per_row <= ROW_BUDGET_BYTES:
        tm *= 2
    if t % tm:
        tm = t
    nrow, nsmall, nout, nacc = len(rows), len(small), len(outs), len(accs)

    def body(*refs):
        rv = [r[...].astype(F32) for r in refs[:nrow]]
        sv = [r[...] for r in refs[nrow:nrow + nsmall]]
        o_refs = refs[nrow + nsmall:nrow + nsmall + nout]
        a_refs = refs[nrow + nsmall + nout:]
        ov, av = fn(rv, sv)
        for r, v in zip(o_refs, ov):
            r[...] = v.astype(r.dtype)
        if nacc:
            @pl.when(pl.program_id(0) == 0)
            def _():
                for r in a_refs:
                    r[...] = jnp.zeros_like(r)
            for r, v in zip(a_refs, av):
                r[...] += v

    in_specs = [pl.BlockSpec((tm, r.shape[1]), lambda i: (i, 0)) for r in rows]
    in_specs += [pl.BlockSpec(s.shape, lambda i: (0, 0)) for s in small]
    out_specs = [pl.BlockSpec((tm, c), lambda i: (i, 0)) for c, _ in outs]
    out_specs += [pl.BlockSpec((1, c), lambda i: (0, 0)) for c in accs]
    out_shape = [jax.ShapeDtypeStruct((t, c), d) for c, d in outs]
    out_shape += [jax.ShapeDtypeStruct((1, c), F32) for c in accs]
    res = pl.pallas_call(
        body, name=name, grid=(t // tm,), in_specs=in_specs, out_specs=out_specs, out_shape=out_shape,
        compiler_params=_params(("arbitrary",)),
    )(*rows, *small)
    return res[:nout], res[nout:]


def _colsum(v):
    return jnp.sum(v, axis=0, keepdims=True)


def _rowmean(v):
    return jnp.mean(v, axis=1, keepdims=True)


def _conv_tiles(x):
    t, c = x.shape
    tm = _tile(t, 256, SUBLANES)
    nt = t // tm
    r8 = tm // SUBLANES
    specs = [
        pl.BlockSpec((SUBLANES, c), lambda i: (jnp.maximum(i * r8 - 1, 0), 0)),
        pl.BlockSpec((tm, c), lambda i: (i, 0)),
        pl.BlockSpec((SUBLANES, c), lambda i: (jnp.minimum((i + 1) * r8, t // SUBLANES - 1), 0)),
    ]
    return tm, nt, specs


def _conv_taps(prev_ref, cur_ref, next_ref, tm, nt, left):
    i = pl.program_id(0)
    p = jnp.where(i > 0, prev_ref[...], 0.0)
    q = jnp.where(i < nt - 1, next_ref[...], 0.0)
    ext = jnp.concatenate([p, cur_ref[...], q], axis=0)
    n = tm + 2 * SUBLANES
    taps = []
    for k in range(CONV_W):
        s = (left - k) % n
        r = pltpu.roll(ext, s, axis=0) if s else ext
        taps.append(r[SUBLANES:SUBLANES + tm])
    return taps


def _conv(x, w, b, left, act, name, out_dtype=F32):
    t, c = x.shape
    tm, nt, specs = _conv_tiles(x)

    def body(prev_ref, cur_ref, next_ref, w_ref, b_ref, o_ref):
        taps = _conv_taps(prev_ref, cur_ref, next_ref, tm, nt, left)
        pre = b_ref[...] + sum(w_ref[k:k + 1, :] * taps[k] for k in range(CONV_W))
        o_ref[...] = (_silu(pre) if act else pre).astype(out_dtype)

    return pl.pallas_call(
        body, name=name, grid=(nt,),
        in_specs=specs + [pl.BlockSpec((CONV_W, c), lambda i: (0, 0)), pl.BlockSpec((1, c), lambda i: (0, 0))],
        out_specs=pl.BlockSpec((tm, c), lambda i: (i, 0)),
        out_shape=jax.ShapeDtypeStruct((t, c), out_dtype), compiler_params=_params(("arbitrary",)),
    )(x, x, x, w, b)


def _conv_bwd(x, dys, combine, w, b, act, name):
    t, c = x.shape
    tm, nt, specs = _conv_tiles(x)
    nd = len(dys)

    def body(*refs):
        prev_ref, cur_ref, next_ref = refs[:3]
        w_ref, b_ref, dpre_ref, dw0, dw1, dw2, dw3, db_ref = refs[3 + nd:]
        dy = combine([r[...] for r in refs[3:3 + nd]])
        taps = _conv_taps(prev_ref, cur_ref, next_ref, tm, nt, CONV_LEFT)
        if act:
            pre = b_ref[...] + sum(w_ref[k:k + 1, :] * taps[k] for k in range(CONV_W))
            dpre = dy * _dsilu(pre)
        else:
            dpre = dy
        dpre_ref[...] = dpre

        @pl.when(pl.program_id(0) == 0)
        def _():
            for r in (dw0, dw1, dw2, dw3, db_ref):
                r[...] = jnp.zeros_like(r)

        for k, r in enumerate((dw0, dw1, dw2, dw3)):
            r[...] += _colsum(dpre * taps[k])
        db_ref[...] += _colsum(dpre)

    row = pl.BlockSpec((1, c), lambda i: (0, 0))
    res = pl.pallas_call(
        body, name=name + "_pre", grid=(nt,),
        in_specs=specs + [pl.BlockSpec((tm, a.shape[1]), lambda i: (i, 0)) for a in dys] + [pl.BlockSpec((CONV_W, c), lambda i: (0, 0)), row],
        out_specs=[pl.BlockSpec((tm, c), lambda i: (i, 0))] + [row] * 5,
        out_shape=[jax.ShapeDtypeStruct((t, c), F32)] + [jax.ShapeDtypeStruct((1, c), F32)] * 5,
        compiler_params=_params(("arbitrary",)),
    )(x, x, x, *dys, w, b)
    dpre, dw, db = res[0], jnp.concatenate(res[1:5], axis=0), res[5]
    dx = _conv(dpre, w[::-1], jnp.zeros_like(b), CONV_W - 1 - CONV_LEFT, False, name + "_dx", BF16)
    return dx, dw, db


def _ssd_dims(xbc):
    t = xbc.shape[0]
    gn = SSD_GROUPS * SSD_STATE
    ws = xbc.shape[1] - 2 * gn
    heads = ws // SSD_HEAD_DIM
    hpg = heads // SSD_GROUPS
    assert hpg % 2 == 0 and ws % gn == 0 and t % SSD_CHUNK == 0
    return t, ws, heads, hpg, heads // 2, t // SSD_CHUNK


def _ssd_chunk_terms(dtr_ref, dtrt_ref, bias_ref, biast_ref, alog_ref, alogt_ref, reverse):
    q = SSD_CHUNK
    a_neg = -jnp.exp(alog_ref[...])
    dt = _softplus(dtr_ref[...] + bias_ref[...])
    dtt = _softplus(dtrt_ref[...] + biast_ref[...])
    a = dt * a_neg
    at = dtt * (-jnp.exp(alogt_ref[...]))
    ii = lax.broadcasted_iota(jnp.int32, (q, q), 0)
    jj = lax.broadcasted_iota(jnp.int32, (q, q), 1)
    mask = (ii <= jj) if reverse else (ii >= jj)
    tri = mask.astype(F32)
    trit = ((jj <= ii) if reverse else (jj >= ii)).astype(F32)
    before = (jj > ii) if reverse else (jj < ii)
    ac = _hdot(tri, a)
    act = _hdot(at, trit)
    tot = _colsum(a)
    return a_neg, dt, a, ac, act, tot, mask, trit, before


def _pair_terms(lo, h1, dt, ac, tot):
    q = SSD_CHUNK

    def colb(v, h):
        return jnp.broadcast_to(v[:, h:h + 1], (v.shape[0], LANES))

    def pairb(v):
        return jnp.where(lo[:v.shape[0]], colb(v, h1), colb(v, h1 + 1))

    dt_p = pairb(dt)
    ac_p = pairb(ac)
    eac_p = jnp.exp(ac_p)
    dte_p = jnp.exp(pairb(jnp.broadcast_to(tot, (q, tot.shape[1]))) - ac_p)
    etot_p = jnp.exp(pairb(tot))
    return dt_p, eac_p, dte_p, etot_p


def _decay_mats(mask, ac, act, cb, h1):
    q = SSD_CHUNK
    out = []
    for h in (h1, h1 + 1):
        diff = jnp.broadcast_to(ac[:, h:h + 1], (q, q)) - jnp.broadcast_to(act[h:h + 1, :], (q, q))
        lmat = jnp.exp(jnp.where(mask, diff, NEG_BIG))
        out.append((lmat, cb * lmat))
    return out


def _ssd_specs(t, ws, heads, npairs, nc, reverse):
    q, gn = SSD_CHUNK, SSD_GROUPS * SSD_STATE
    ci = (lambda k: nc - 1 - k) if reverse else (lambda k: k)
    small = lambda shape: pl.BlockSpec(shape, lambda k: (0,) * len(shape))
    seq = [
        pl.BlockSpec((q, ws), lambda k: (ci(k), 0)),
        pl.BlockSpec((q, gn), lambda k: (ci(k), ws // gn)),
        pl.BlockSpec((q, gn), lambda k: (ci(k), ws // gn + 1)),
        pl.BlockSpec((q, heads), lambda k: (ci(k), 0)),
        pl.BlockSpec((heads, q), lambda k: (0, ci(k))),
        small((1, heads)), small((heads, 1)), small((1, heads)), small((heads, 1)),
    ]
    return ci, small, seq


def _ssd_fwd(xbc, dtr, bias, alog, s0, reverse, name, rider=None):
    t, ws, heads, hpg, npairs, nc = _ssd_dims(xbc)
    q, ns = SSD_CHUNK, SSD_STATE
    ci, small, seq = _ssd_specs(t, ws, heads, npairs, nc, reverse)

    def body(xs_ref, b_ref, c_ref, dtr_ref, dtrt_ref, bias_ref, biast_ref, alog_ref, alogt_ref, s0_ref,
             y_ref, se_ref, s_ref):
        @pl.when(pl.program_id(0) == 0)
        def _():
            s_ref[...] = s0_ref[...]

        _, dt, _, ac, act, tot, mask, _, _ = _ssd_chunk_terms(dtr_ref, dtrt_ref, bias_ref, biast_ref, alog_ref, alogt_ref, reverse)
        lo = lax.broadcasted_iota(jnp.int32, (q, LANES), 1) < SSD_HEAD_DIM
        for g in range(SSD_GROUPS):
            bg = b_ref[:, g * ns:(g + 1) * ns]
            cg = c_ref[:, g * ns:(g + 1) * ns]
            cb = _bdot_nt(cg, bg)
            bgt = bg.T
            for pq in range(hpg // 2):
                pp = g * (hpg // 2) + pq
                h1 = 2 * pp
                cs = slice(pp * LANES, (pp + 1) * LANES)
                (_, m1), (_, m2) = _decay_mats(mask, ac, act, cb, h1)
                dt_p, eac_p, dte_p, etot_p = _pair_terms(lo, h1, dt, ac, tot)
                xp = xs_ref[:, cs] * dt_p
                sp = s_ref[pp]
                se_ref[0, pp] = sp
                y_ref[:, cs] = jnp.where(lo, _bdot(m1, xp), _bdot(m2, xp)) + eac_p * _bdot(cg, sp)
                s_ref[pp] = etot_p * sp + _bdot(bgt, xp * dte_p)

    st = (npairs, ns, LANES)
    return _call(
        body, name, (nc,), seq + [small(st)],
        [pl.BlockSpec((q, ws), lambda k: (ci(k), 0)), pl.BlockSpec((1,) + st, lambda k: (ci(k), 0, 0, 0)), small(st)],
        [jax.ShapeDtypeStruct((t, ws), F32), jax.ShapeDtypeStruct((nc,) + st, F32), jax.ShapeDtypeStruct(st, F32)],
        [], ("arbitrary",), (xbc, xbc, xbc, dtr, dtr.T, bias, bias.T, alog, alog.T, s0), rider)


def _ssd_bwd(xbc, dtr, bias, alog, dy, s_enter, lam0, reverse, name, rider=None):
    t, ws, heads, hpg, npairs, nc = _ssd_dims(xbc)
    q, ns, gn = SSD_CHUNK, SSD_STATE, SSD_GROUPS * SSD_STATE
    ci, small, seq = _ssd_specs(t, ws, heads, npairs, nc, not reverse)
    st = (npairs, ns, LANES)

    e_heads = jnp.asarray(np.arange(ws)[:, None] // SSD_HEAD_DIM == np.arange(heads)[None, :], BF16)

    def body(xs_ref, b_ref, c_ref, dtr_ref, dtrt_ref, bias_ref, biast_ref, alog_ref, alogt_ref, dy_ref, se_ref, lam0_ref, e_ref,
             dxs_ref, db_ref, dc_ref, ddtr_ref, da_ref, dbias_ref, lam_ref, st_s, ts_s, ss_s, xx_s):
        @pl.when(pl.program_id(0) == 0)
        def _():
            lam_ref[...] = lam0_ref[...]
            da_ref[...] = jnp.zeros_like(da_ref)
            dbias_ref[...] = jnp.zeros_like(dbias_ref)

        a_neg, dt, _, ac, act, tot, mask, trit, before = _ssd_chunk_terms(dtr_ref, dtrt_ref, bias_ref, biast_ref, alog_ref, alogt_ref, reverse)
        lo = lax.broadcasted_iota(jnp.int32, (q, LANES), 1) < SSD_HEAD_DIM
        head_col = lax.broadcasted_iota(jnp.int32, (heads, 1), 0)
        ii = lax.broadcasted_iota(jnp.int32, (q, q), 0)
        jj = lax.broadcasted_iota(jnp.int32, (q, q), 1)
        before_t = ((ii > jj) if reverse else (ii < jj)).astype(F32)

        def per_head(v):
            hi = v.astype(BF16)
            lo = (v - hi.astype(F32)).astype(BF16)
            sel = e_ref[...]
            return jnp.dot(hi, sel, preferred_element_type=F32) + jnp.dot(lo, sel, preferred_element_type=F32)

        da_tt = jnp.zeros((heads, q), F32)
        for g in range(SSD_GROUPS):
            gs = slice(g * ns, (g + 1) * ns)
            bg, cg = b_ref[:, gs], c_ref[:, gs]
            cb = _bdot_nt(cg, bg)
            cgt = cg.T
            dcb = jnp.zeros((q, q), F32)
            dbg = jnp.zeros((q, ns), F32)
            dcg = jnp.zeros((q, ns), F32)
            for pq in range(hpg // 2):
                pp = g * (hpg // 2) + pq
                h1 = 2 * pp
                cs = slice(pp * LANES, (pp + 1) * LANES)
                (l1, m1), (l2, m2) = _decay_mats(mask, ac, act, cb, h1)
                dt_p, eac_p, dte_p, etot_p = _pair_terms(lo, h1, dt, ac, tot)
                xsp = xs_ref[:, cs]
                xp = xsp * dt_p
                dyp = dy_ref[:, cs]
                lam = lam_ref[pp]
                sp = se_ref[0, pp]
                dx_state = dte_p * _bdot(bg, lam)
                dxdt = jnp.where(lo, _bdot(m1.T, dyp), _bdot(m2.T, dyp)) + dx_state
                dyx1 = _bdot_nt(jnp.where(lo, dyp, 0.0), xp)
                dyx2 = _bdot_nt(jnp.where(lo, 0.0, dyp), xp)
                dcb = dcb + l1 * dyx1 + l2 * dyx2
                for h, pair in ((h1, m1 * dyx1), (h1 + 1, m2 * dyx2)):
                    cross = _colsum(jnp.where(mask, _bdot(pair, before_t), 0.0))
                    da_tt = da_tt + (head_col == h).astype(F32) * cross
                st_s[:, cs] = dyp * (eac_p * _bdot(cg, sp))
                ts_s[:, cs] = dx_state * xp
                ss_s[:, cs] = jnp.broadcast_to(_colsum(lam * sp) * etot_p, (SUBLANES, LANES))
                xx_s[:, cs] = dxdt * xsp
                edy = eac_p * dyp
                xe = dte_p * xp
                dcg = dcg + _bdot_nt(edy, sp)
                dbg = dbg + _bdot_nt(xe, lam)
                lam_ref[pp] = etot_p * lam + _bdot(cgt, edy)
                dxs_ref[:, cs] = dxdt * dt_p
            db_ref[:, gs] = dbg + _bdot(dcb.T, cg)
            dc_ref[:, gs] = dcg + _bdot(dcb, bg)
        da = da_tt.T + _hdot(trit, per_head(st_s[...])) + _hdot(before.astype(F32), per_head(ts_s[...])) + per_head(ss_s[...])[0:1]
        ddtr = (da * a_neg + per_head(xx_s[...])) * _sigmoid(dtr_ref[...] + bias_ref[...])
        ddtr_ref[...] = ddtr
        da_ref[...] += _colsum(da * dt)
        dbias_ref[...] += _colsum(ddtr)

    row = lambda c: pl.BlockSpec((q, c), lambda k: (ci(k), 0))
    return _call(
        body, name, (nc,),
        seq + [row(ws), pl.BlockSpec((1,) + st, lambda k: (ci(k), 0, 0, 0)), small(st), small((ws, heads))],
        [row(ws), row(gn), row(gn), row(heads), small((1, heads)), small((1, heads)), small(st)],
        [jax.ShapeDtypeStruct((t, ws), F32), jax.ShapeDtypeStruct((t, gn), F32), jax.ShapeDtypeStruct((t, gn), F32),
         jax.ShapeDtypeStruct((t, heads), F32), jax.ShapeDtypeStruct((1, heads), F32),
         jax.ShapeDtypeStruct((1, heads), F32), jax.ShapeDtypeStruct(st, F32)],
        [pltpu.VMEM((q, ws), F32), pltpu.VMEM((q, ws), F32), pltpu.VMEM((SUBLANES, ws), F32), pltpu.VMEM((q, ws), F32)],
        ("arbitrary",), (xbc, xbc, xbc, dtr, dtr.T, bias, bias.T, alog, alog.T, dy, s_enter, lam0, e_heads), rider)


def _tile_scan(a_ref, u_ref, out_ref, carry_ref, ntiles, reverse):
    w = a_ref.shape[1]
    row = lax.broadcasted_iota(jnp.int32, (SUBLANES, w), 0)

    def step(j, _):
        r0 = pl.multiple_of((ntiles - 1 - j if reverse else j) * SUBLANES, SUBLANES)
        a = a_ref[pl.ds(r0, SUBLANES), :]
        u = u_ref[pl.ds(r0, SUBLANES), :]
        for d in (1, 2, 4):
            keep = (row < SUBLANES - d) if reverse else (row >= d)
            s = SUBLANES - d if reverse else d
            a_sh = jnp.where(keep, pltpu.roll(a, s, axis=0), 1.0)
            u_sh = jnp.where(keep, pltpu.roll(u, s, axis=0), 0.0)
            u = a * u_sh + u
            a = a * a_sh
        h = a * carry_ref[...] + u
        out_ref[pl.ds(r0, SUBLANES), :] = h
        last = h[0:1, :] if reverse else h[SUBLANES - 1:SUBLANES, :]
        carry_ref[...] = jnp.broadcast_to(last, (SUBLANES, w))
        return 0

    lax.fori_loop(0, ntiles, step, 0)


def _lru_gates(xh, wa, wx, ba, bx, lam):
    r = _sigmoid(_bdot(xh, wa) + ba)
    i = _sigmoid(_bdot(xh, wx) + bx)
    sp = _softplus(-lam)
    a = jnp.exp(-LRU_C * r * sp)
    return r, i, sp, a


def _lru_fwd(xc, wa, wx, ba, bx, lam, h0, addend, reverse, name, rider=None):
    t, w = xc.shape
    tb = LRU_BLOCK
    nb, nh = t // tb, w // LRU_HEAD_DIM
    bi = (lambda k: nb - 1 - k) if reverse else (lambda k: k)
    has_add = addend is not None

    def body(*refs):
        xc_ref, wa_ref, wx_ref, ba_ref, bx_ref, lam_ref, h0_ref = refs[:7]
        add_ref = refs[7] if has_add else None
        h_ref, hl_ref = refs[7 + has_add], refs[8 + has_add]
        sum_ref = refs[9 + has_add] if has_add else None
        a_s, u_s, carry = refs[-3:]

        @pl.when(pl.program_id(0) == 0)
        def _():
            carry[...] = jnp.broadcast_to(h0_ref[...], carry.shape)

        for hd in range(nh):
            hs = slice(hd * LRU_HEAD_DIM, (hd + 1) * LRU_HEAD_DIM)
            xh = xc_ref[:, hs]
            _, i, _, a = _lru_gates(xh, wa_ref[hd], wx_ref[hd], ba_ref[:, hs], bx_ref[:, hs], lam_ref[:, hs])
            a_s[:, hs] = a
            u_s[:, hs] = jnp.sqrt(1.0 - a * a) * (i * xh)
        _tile_scan(a_s, u_s, h_ref, carry, tb // SUBLANES, reverse)
        hl_ref[...] = carry[0:1, :]
        if has_add:
            sum_ref[...] = h_ref[...] + add_ref[...]

    blk = pl.BlockSpec((tb, w), lambda k: (bi(k), 0))
    vec = pl.BlockSpec((1, w), lambda k: (0, 0))
    wsp = pl.BlockSpec((nh, LRU_HEAD_DIM, LRU_HEAD_DIM), lambda k: (0, 0, 0))
    out_specs = [blk, vec] + ([blk] if has_add else [])
    out_shape = [jax.ShapeDtypeStruct((t, w), F32), jax.ShapeDtypeStruct((1, w), F32)] + ([jax.ShapeDtypeStruct((t, w), F32)] if has_add else [])
    return _call(
        body, name, (nb,), [blk, wsp, wsp, vec, vec, vec, vec] + ([blk] if has_add else []), out_specs, out_shape,
        [pltpu.VMEM((tb, w), F32), pltpu.VMEM((tb, w), F32), pltpu.VMEM((SUBLANES, w), F32)], ("arbitrary",),
        (xc, wa, wx, ba, bx, lam, h0, *([addend] if has_add else [])), rider)


def _lru_bwd(xc, wa, wx, ba, bx, lam, h0, h, dh, dfin, addend, reverse, name, rider=None):
    t, w = xc.shape
    tb = LRU_BLOCK
    nb, nh = t // tb, w // LRU_HEAD_DIM
    r8 = tb // SUBLANES
    bi = (lambda k: k) if reverse else (lambda k: nb - 1 - k)
    if reverse:
        halo = lambda k: (jnp.minimum((bi(k) + 1) * r8, t // SUBLANES - 1), 0)
    else:
        halo = lambda k: (jnp.maximum(bi(k) * r8 - 1, 0), 0)
    has_add = addend is not None
    n_ext = tb + SUBLANES

    def body(*refs):
        xc_ref, wa_ref, wx_ref, ba_ref, bx_ref, lam_ref, h0_ref, h_ref, halo_ref, dh_ref, dfin_ref = refs[:11]
        add_ref = refs[11] if has_add else None
        dxc_ref, dwa_ref, dwx_ref, dba_ref, dbx_ref, dsp_ref, dh0_ref = refs[11 + has_add:18 + has_add]
        a_s, v_s, mu_s, r_s, i_s, carry = refs[-6:]
        k = pl.program_id(0)

        @pl.when(k == 0)
        def _():
            carry[...] = jnp.broadcast_to(dfin_ref[...], carry.shape)
            for r in (dwa_ref, dwx_ref, dba_ref, dbx_ref, dsp_ref):
                r[...] = jnp.zeros_like(r)

        for hd in range(nh):
            hs = slice(hd * LRU_HEAD_DIM, (hd + 1) * LRU_HEAD_DIM)
            r, i, _, a = _lru_gates(xc_ref[:, hs], wa_ref[hd], wx_ref[hd], ba_ref[:, hs], bx_ref[:, hs], lam_ref[:, hs])
            a_s[:, hs] = a
            r_s[:, hs] = r
            i_s[:, hs] = i
            v_s[:, hs] = a * dh_ref[:, hs]
        mu_in = carry[...]
        _tile_scan(a_s, v_s, mu_s, carry, r8, not reverse)
        dh0_ref[...] = carry[0:1, :]
        first = (bi(k) == nb - 1) if reverse else (bi(k) == 0)
        edge = jnp.where(first, jnp.broadcast_to(h0_ref[...], (SUBLANES, w)), halo_ref[...])
        for hd in range(nh):
            hs = slice(hd * LRU_HEAD_DIM, (hd + 1) * LRU_HEAD_DIM)
            if reverse:
                mu_sh = pltpu.roll(jnp.concatenate([mu_in[:, hs], mu_s[:, hs]], axis=0), 1, axis=0)[SUBLANES:]
                h_sh = pltpu.roll(jnp.concatenate([h_ref[:, hs], edge[:, hs]], axis=0), n_ext - 1, axis=0)[:tb]
            else:
                mu_sh = pltpu.roll(jnp.concatenate([mu_s[:, hs], mu_in[:, hs]], axis=0), n_ext - 1, axis=0)[:tb]
                h_sh = pltpu.roll(jnp.concatenate([edge[:, hs], h_ref[:, hs]], axis=0), 1, axis=0)[SUBLANES:]
            xh = xc_ref[:, hs]
            r, i, a = r_s[:, hs], i_s[:, hs], a_s[:, hs]
            sp = _softplus(-lam_ref[:, hs])
            om = 1.0 - a * a
            rs = lax.rsqrt(om)
            lam_t = dh_ref[:, hs] + mu_sh
            dla = lam_t * h_sh * a - lam_t * (i * xh) * (a * a) * rs
            dix = lam_t * (om * rs)
            dpr = dla * (-LRU_C * sp) * r * (1.0 - r)
            dpi = dix * xh * i * (1.0 - i)
            dsp_ref[:, hs] += _colsum(dla * (-LRU_C * r))
            dba_ref[:, hs] += _colsum(dpr)
            dbx_ref[:, hs] += _colsum(dpi)
            dx = dix * i + _bdot_nt(dpr, wa_ref[hd]) + _bdot_nt(dpi, wx_ref[hd])
            xt = xh.T
            dwa_ref[hd] += _bdot(xt, dpr)
            dwx_ref[hd] += _bdot(xt, dpi)
            dxc_ref[:, hs] = dx + add_ref[:, hs] if has_add else dx

    blk = pl.BlockSpec((tb, w), lambda k: (bi(k), 0))
    vec = pl.BlockSpec((1, w), lambda k: (0, 0))
    wsp = pl.BlockSpec((nh, LRU_HEAD_DIM, LRU_HEAD_DIM), lambda k: (0, 0, 0))
    wshape = jax.ShapeDtypeStruct((nh, LRU_HEAD_DIM, LRU_HEAD_DIM), F32)
    vshape = jax.ShapeDtypeStruct((1, w), F32)
    return _call(
        body, name, (nb,),
        [blk, wsp, wsp, vec, vec, vec, vec, blk, pl.BlockSpec((SUBLANES, w), halo), blk, vec] + ([blk] if has_add else []),
        [blk, wsp, wsp, vec, vec, vec, vec],
        [jax.ShapeDtypeStruct((t, w), F32), wshape, wshape, vshape, vshape, vshape, vshape],
        [pltpu.VMEM((tb, w), F32)] * 5 + [pltpu.VMEM((SUBLANES, w), F32)], ("arbitrary",),
        (xc, wa, wx, ba, bx, lam, h0, h, h, dh, dfin, *([addend] if has_add else [])), rider)


def _coords():
    return lax.axis_index("x"), lax.axis_index("y"), lax.axis_index("c")


HALF_ROWS_ALIGN = 32


def _all_gather(shards, name, begin_only=()):
    n, m = len(shards), len(begin_only)
    start, wait, sems = _gather_rounds(shards)
    start2, wait2, sems2 = _gather_rounds(begin_only) if m else (None, None, [])

    def body(*refs):
        ins, ins2 = refs[:n], refs[n:n + m]
        outs, outs2 = refs[n + m:2 * n + m], refs[2 * n + m:2 * (n + m)]
        s1, s2 = refs[2 * (n + m):2 * (n + m) + len(sems)], refs[2 * (n + m) + len(sems):]
        start(0, ins, outs, s1)
        if m:
            start2(0, ins2, outs2, s2)
        wait(0, ins, outs, s1)
        if m:
            wait2(0, ins2, outs2, s2)
        for rnd in (1, 2):
            start(rnd, ins, outs, s1)
            wait(rnd, ins, outs, s1)

    any_spec = pl.BlockSpec(memory_space=pl.ANY)
    return pl.pallas_call(
        body, name=name, in_specs=[any_spec] * (n + m), out_specs=[any_spec] * (n + m),
        out_shape=[jax.ShapeDtypeStruct((N_DEV,) + s.shape, s.dtype) for s in list(shards) + list(begin_only)],
        scratch_shapes=list(sems) + list(sems2),
    )(*shards, *begin_only)


def _gather_riders(shards):
    n = len(shards)
    start, wait, sems = _gather_rounds(shards)
    out_shape = [jax.ShapeDtypeStruct((N_DEV,) + s.shape, s.dtype) for s in shards]

    def make(rnd, bufs=()):
        return _Rider(list(shards) + list(bufs), out_shape, {n + i: i for i in range(len(bufs))}, sems,
                      lambda r_in, r_out, s: start(rnd, r_in[:n], r_out, s), lambda r_in, r_out, s: wait(rnd, r_in[:n], r_out, s))
    return make


def _gather_rounds(shards):
    n = len(shards)
    split = [s.ndim >= 2 and s.shape[0] % HALF_ROWS_ALIGN == 0 for s in shards]
    n_copies = 10

    def plan(ins, outs, sem_refs):
        send_sems, recv_sems, local_sems = sem_refs
        x, y, c = _coords()
        slot = lambda p: 4 * p[0] + 2 * p[1] + p[2]
        me, xp, yp, dg = (x, y, c), (1 - x, y, c), (x, 1 - y, c), (1 - x, 1 - y, c)
        other_core = lambda p: (p[0], p[1], 1 - p[2])

        def rows(ref, i, half):
            r = shards[i].shape[0]
            if half == "all" or not split[i]:
                return ref
            return ref.at[pl.ds(0, r // 2)] if half == "a" else ref.at[pl.ds(r // 2, r // 2)]

        def copy(i, j, holder, half, to, own=False):
            dst = rows(outs[i].at[slot(holder)], i, half)
            src = rows(ins[i], i, half) if own else dst
            return pltpu.make_async_remote_copy(src_ref=src, dst_ref=dst, send_sem=send_sems.at[i, j],
                                                recv_sem=recv_sems.at[i, j], device_id=to, device_id_type=MESH)

        rounds = [
            ([(0, me, "a", xp, True), (1, me, "b", yp, True)],
             [(0, xp, "a"), (1, yp, "b")]),
            ([(2, me, "a", yp, True), (3, xp, "a", yp, False), (4, me, "b", xp, True), (5, yp, "b", xp, False)],
             [(2, yp, "a"), (3, dg, "a"), (4, xp, "b"), (5, dg, "b")]),
            ([(6, me, "all", other_core(me), True)] + [(7 + q, p, "all", other_core(me), False) for q, p in enumerate((xp, yp, dg))],
             [(6 + q, other_core(p), "all") for q, p in enumerate((me, xp, yp, dg))]),
        ]
        local = lambda: [pltpu.make_async_copy(ins[i], outs[i].at[slot(me)], local_sems.at[i]) for i in range(n)]
        travels = lambda i, half: half != "b" or split[i]
        return rounds, copy, local, travels, me

    def start(rnd, ins, outs, sem_refs):
        rounds, copy, local, travels, _ = plan(ins, outs, sem_refs)
        if rnd == 0:
            for cp in local():
                cp.start()
        for j, holder, half, to, own in rounds[rnd][0]:
            for i in range(n):
                if travels(i, half):
                    copy(i, j, holder, half, to, own).start()

    def wait(rnd, ins, outs, sem_refs):
        rounds, copy, local, travels, me = plan(ins, outs, sem_refs)
        for j, holder, half in rounds[rnd][1]:
            for i in range(n):
                if travels(i, half):
                    copy(i, j, holder, half, me).wait_recv()
        for j, holder, half, to, own in rounds[rnd][0]:
            for i in range(n):
                if travels(i, half):
                    copy(i, j, holder, half, to, own).wait_send()
        if rnd == 0:
            for cp in local():
                cp.wait()

    sems = [pltpu.SemaphoreType.DMA((n, n_copies)), pltpu.SemaphoreType.DMA((n, n_copies)), pltpu.SemaphoreType.DMA((n,))]
    return start, wait, sems


def _swap_rider(bufs, axes):
    n = len(bufs)

    def copies(ins, outs, sem_refs):
        x, y, c = _coords()
        to = {"x": (1 - x, y, c), "y": (x, 1 - y, c), "c": (x, y, 1 - c)}
        return [pltpu.make_async_remote_copy(src_ref=ins[i], dst_ref=outs[i], send_sem=sem_refs[0].at[i], recv_sem=sem_refs[1].at[i],
                                             device_id=to[axes[i]], device_id_type=MESH) for i in range(n)]

    def start(ins, outs, sem_refs):
        for cp in copies(ins, outs, sem_refs):
            cp.start()

    def wait(ins, outs, sem_refs):
        for cp in copies(ins, outs, sem_refs):
            cp.wait()

    return _Rider(list(bufs), [jax.ShapeDtypeStruct(b.shape, b.dtype) for b in bufs], {},
                  [pltpu.SemaphoreType.DMA((n,)), pltpu.SemaphoreType.DMA((n,))], start, wait)


def _swap(bufs, axes, name):
    n = len(bufs)
    rider = _swap_rider(bufs, axes)

    def body(*refs):
        rider.start(refs[:n], refs[n:2 * n], refs[2 * n:])
        rider.wait(refs[:n], refs[n:2 * n], refs[2 * n:])

    any_spec = pl.BlockSpec(memory_space=pl.ANY)
    return pl.pallas_call(body, name=name, in_specs=[any_spec] * n, out_specs=[any_spec] * n, out_shape=rider.out_shape,
                          scratch_shapes=rider.sems)(*bufs)


def _as2d(a):
    return a.reshape(-1, a.shape[-1])


def _add_f32(a, b, name):
    (out,), _ = _rowwise(lambda r, s: ([r[0].astype(F32) + r[1].astype(F32)], []), [_as2d(a), _as2d(b)], [],
                         [(a.shape[-1], F32)], [], name)
    return out.reshape(a.shape)


class _ReduceScatter:
    def __init__(self, parts, name):
        self.n, self.name, self.rnd = len(parts), name, 0
        self.cur = [p.reshape((2, 2, 2) + p.shape[1:]) for p in parts]

    def stage(self):
        x, y, c = _coords()
        n = self.n
        if self.rnd == 0:
            bufs, idxs, dims, axes = self.cur, [c] * n, [2] * n, ["c"] * n
        elif self.rnd == 1:
            bufs = [p[:, :, :p.shape[2] // 2] for p in self.cur] + [p[:, :, p.shape[2] // 2:] for p in self.cur]
            idxs, dims, axes = [y] * n + [x] * n, [1] * n + [0] * n, ["y"] * n + ["x"] * n
        else:
            bufs, idxs, dims, axes = self.cur, [x] * n + [y] * n, [0] * (2 * n), ["x"] * n + ["y"] * n
        pick = lambda p, idx, dim: lax.dynamic_index_in_dim(p, idx, dim, keepdims=False)
        self.keep = [pick(p, i_, d_) for p, i_, d_ in zip(bufs, idxs, dims)]
        return [pick(p, 1 - i_, d_).astype(BF16) for p, i_, d_ in zip(bufs, idxs, dims)], axes

    def absorb(self, got):
        self.cur = [_add_f32(k, g, f"{self.name}_add{self.rnd}_{i}") for i, (k, g) in enumerate(zip(self.keep, got))]
        self.rnd += 1

    def result(self):
        return [jnp.concatenate([self.cur[i], self.cur[self.n + i]], axis=0) for i in range(self.n)]


def _reduce_scatter(parts, name):
    rs = _ReduceScatter(parts, name)
    for rnd in range(3):
        send, axes = rs.stage()
        rs.absorb(_swap(send, axes, f"{name}_swap{rnd}"))
    return rs.result()


def _pack(arrs):
    flat = jnp.concatenate([a.reshape(-1).astype(F32) for a in arrs])
    rows = -(-flat.shape[0] // LANES)
    rows = -(-rows // 512) * 512
    return jnp.pad(flat, (0, rows * LANES - flat.shape[0])).reshape(rows, LANES)


def _unpack(buf, like):
    flat, out, off = buf.reshape(-1), [], 0
    for a in like:
        out.append(flat[off:off + a.size].reshape(a.shape))
        off += a.size
    return out


class _AllReduce:
    def __init__(self, arrs, name, narrow=()):
        self.n, self.name, self.narrow = len(arrs), name, narrow
        self.wide = [a for i, a in enumerate(arrs) if i not in narrow]
        self.thin = [a for i, a in enumerate(arrs) if i in narrow]
        self.packs = [_pack(self.wide)] + ([_pack(self.thin).astype(BF16)] if self.thin else [])
        self.make = _gather_riders(self.packs)

    def finish(self, gathered):
        sums = []
        for j, g in enumerate(gathered):
            (tot,), _ = _rowwise(lambda r, s: ([functools.reduce(lambda u, v: u + v, r)], []), [g[i] for i in range(N_DEV)], [],
                                 [(LANES, F32)], [], f"{self.name}_sum{j}")
            sums.append(tot)
        out_wide = iter(_unpack(sums[0], self.wide))
        out_thin = iter(_unpack(sums[1], self.thin) if self.thin else [])
        return [next(out_thin) if i in self.narrow else next(out_wide) for i in range(self.n)]


def _all_reduce(arrs, name, narrow=()):
    ar = _AllReduce(arrs, name, narrow)
    return ar.finish(_all_gather(ar.packs, name + "_gather"))


def _adamw(w, g, m, v, name):
    def fn(r, s):
        w_, g_, m_, v_ = r
        m_ = ADAM_B1 * m_ + (1.0 - ADAM_B1) * g_
        v_ = ADAM_B2 * v_ + (1.0 - ADAM_B2) * jnp.square(g_)
        m_hat = m_ / (1.0 - ADAM_B1 ** ADAM_STEP)
        v_hat = v_ / (1.0 - ADAM_B2 ** ADAM_STEP)
        return [-ADAM_LR * (m_hat / (jnp.sqrt(v_hat) + ADAM_EPS) + ADAM_WD * w_), m_, v_], []

    c = w.shape[-1]
    (d, nm, nv), _ = _rowwise(fn, [_as2d(a) for a in (w, g, m, v)], [], [(c, F32)] * 3, [], name)
    return d.reshape(w.shape), nm.reshape(w.shape), nv.reshape(w.shape)


def _norm_mod(x, gain, sc, sh, name):
    def fn(r, s):
        xv = r[0]
        xn = xv * lax.rsqrt(_rowmean(xv * xv) + EPS)
        return [xn * s[0] * (1.0 + s[1]) + s[2]], []
    (h,), _ = _rowwise(fn, [x], [gain, sc, sh], [(x.shape[1], BF16)], [], name)
    return h


def _norm_mod_bwd(x, dh, dres, gain, sc, name):
    d = x.shape[1]

    def fn(r, s):
        xv, dhv = r[0], r[1]
        rs = lax.rsqrt(_rowmean(xv * xv) + EPS)
        xn = xv * rs
        dxn = dhv * (1.0 + s[1]) * s[0]
        dx = rs * (dxn - xn * _rowmean(dxn * xn))
        if dres is not None:
            dx = dx + r[2]
        return [dx], [_colsum(dhv), _colsum(dhv * xn * s[0]), _colsum(dhv * (1.0 + s[1]) * xn)]
    (dx,), accs = _rowwise(fn, [x, dh] + ([dres] if dres is not None else []), [gain, sc], [(d, F32)], [d, d, d], name)
    return dx, accs


def _ada_rows(mod_row, d):
    return [mod_row[:, i * d:(i + 1) * d] for i in range(N_MOD)]


def _to_col_major(t):
    n, c = t.shape
    return t.reshape(n // GRID_W, GRID_W, c).transpose(1, 0, 2).reshape(n, c)


def _to_row_major(t):
    n, c = t.shape
    return t.reshape(GRID_W, n // GRID_W, c).transpose(1, 0, 2).reshape(n, c)


def kernel(x, c, ctx, c_ctx, w_ada, b_ada, norm_mix, norm_ffn, w_in, ssd_conv_w, ssd_conv_b, ssd_dt_bias, ssd_a_log, ssd_d, ssd_norm, w_out_ssd, lru_conv_w, lru_conv_b, lru_w_a, lru_b_a, lru_w_x, lru_b_x, lru_lambda, w_out_lru, w_gate, b_gate, w_o, ffn_w13, ffn_w2, final_norm, loss_target, m_c_ctx, m_w_ada, m_b_ada, m_norm_mix, m_norm_ffn, m_w_in, m_ssd_conv_w, m_ssd_conv_b, m_ssd_dt_bias, m_ssd_a_log, m_ssd_d, m_ssd_norm, m_w_out_ssd, m_lru_conv_w, m_lru_conv_b, m_lru_w_a, m_lru_b_a, m_lru_w_x, m_lru_b_x, m_lru_lambda, m_w_out_lru, m_w_gate, m_b_gate, m_w_o, m_ffn_w13, m_ffn_w2, m_final_norm, v_c_ctx, v_w_ada, v_b_ada, v_norm_mix, v_norm_ffn, v_w_in, v_ssd_conv_w, v_ssd_conv_b, v_ssd_dt_bias, v_ssd_a_log, v_ssd_d, v_ssd_norm, v_w_out_ssd, v_lru_conv_w, v_lru_conv_b, v_lru_w_a, v_lru_b_a, v_lru_w_x, v_lru_b_x, v_lru_lambda, v_w_out_lru, v_w_gate, v_b_gate, v_w_o, v_ffn_w13, v_ffn_w2, v_final_norm):
    names = ["c_ctx", "w_ada", "b_ada", "norm_mix", "norm_ffn", "w_in", "ssd_conv_w", "ssd_conv_b", "ssd_dt_bias", "ssd_a_log",
             "ssd_d", "ssd_norm", "w_out_ssd", "lru_conv_w", "lru_conv_b", "lru_w_a", "lru_b_a", "lru_w_x", "lru_b_x", "lru_lambda",
             "w_out_lru", "w_gate", "b_gate", "w_o", "ffn_w13", "ffn_w2", "final_norm"]
    wts = dict(zip(names, (c_ctx, w_ada, b_ada, norm_mix, norm_ffn, w_in, ssd_conv_w, ssd_conv_b, ssd_dt_bias, ssd_a_log, ssd_d, ssd_norm, w_out_ssd, lru_conv_w, lru_conv_b, lru_w_a, lru_b_a, lru_w_x, lru_b_x, lru_lambda, w_out_lru, w_gate, b_gate, w_o, ffn_w13, ffn_w2, final_norm)))
    mom1 = dict(zip(names, (m_c_ctx, m_w_ada, m_b_ada, m_norm_mix, m_norm_ffn, m_w_in, m_ssd_conv_w, m_ssd_conv_b, m_ssd_dt_bias, m_ssd_a_log, m_ssd_d, m_ssd_norm, m_w_out_ssd, m_lru_conv_w, m_lru_conv_b, m_lru_w_a, m_lru_b_a, m_lru_w_x, m_lru_b_x, m_lru_lambda, m_w_out_lru, m_w_gate, m_b_gate, m_w_o, m_ffn_w13, m_ffn_w2, m_final_norm)))
    mom2 = dict(zip(names, (v_c_ctx, v_w_ada, v_b_ada, v_norm_mix, v_norm_ffn, v_w_in, v_ssd_conv_w, v_ssd_conv_b, v_ssd_dt_bias, v_ssd_a_log, v_ssd_d, v_ssd_norm, v_w_out_ssd, v_lru_conv_w, v_lru_conv_b, v_lru_w_a, v_lru_b_a, v_lru_w_x, v_lru_b_x, v_lru_lambda, v_w_out_lru, v_w_gate, v_b_gate, v_w_o, v_ffn_w13, v_ffn_w2, v_final_norm)))

    xi, yi, ci_ = _coords()
    me = 4 * xi + 2 * yi + ci_
    x2, tgt, ctx2 = x[0], loss_target[0], ctx[0]
    t, d = x2.shape
    tc = ctx2.shape[0]
    ws = N_DEV * w_out_ssd.shape[1]
    wl = N_DEV * w_out_lru.shape[1]
    heads = ws // SSD_HEAD_DIM
    gn = SSD_GROUPS * SSD_STATE
    nh = wl // LRU_HEAD_DIM
    ff = N_DEV * ffn_w2.shape[1]
    ada_cols = w_ada.shape[2]

    smalls = [c, ssd_conv_w[0], lru_conv_w[0], lru_b_a[0], lru_b_x[0], lru_lambda[0]]
    gate_shard = [w_gate[0].astype(BF16)]
    got = _all_gather(smalls + [w_in[0].astype(BF16)], "gather_weights", begin_only=gate_shard)
    gate_g, gate_bufs = _gather_riders(gate_shard), got[7:]
    late_a = _gather_riders([ffn_w13[0].astype(BF16)])
    late_b = _gather_riders([w.astype(BF16) for w in (ffn_w2[0], w_out_ssd[0], w_out_lru[0], w_o[0])])
    c_all = got[0].reshape(N_DEV, d)
    colcat = lambda g: jnp.moveaxis(g, 0, -2).reshape(g.shape[1:-1] + (N_DEV * g.shape[-1],))
    conv_w_s, conv_w_l, lru_ba, lru_bx, lru_lam = (colcat(g) for g in got[1:6])
    w_in_f = colcat(got[6])
    o_z, o_xbc, o_dt, o_xr, o_yr = 0, ws, 2 * ws + 2 * gn, 2 * ws + 2 * gn + 2 * heads, 2 * ws + 2 * gn + 2 * heads + wl
    w_z, w_xbc = w_in_f[:, o_z:o_xbc], w_in_f[:, o_xbc:o_dt]
    w_dt = jnp.pad(w_in_f[:, o_dt:o_xr], ((0, 0), (0, LANES - 2 * heads)))
    w_xr, w_yr = w_in_f[:, o_xr:o_yr], w_in_f[:, o_yr:]

    cond = jnp.concatenate([c_all, c_ctx[None, :], jnp.zeros((16 - N_DEV - 1, d), F32)], axis=0)
    cond_act = _silu(cond)
    mod_l = _mm([(cond_act, w_ada[0])], "nn", F32, "ada_fwd")
    (mod_g,) = _all_gather([mod_l], "gather_mod")
    mod_full = colcat(mod_g) + b_ada
    sh_m, sc_m, g_m, sh_f, sc_f, g_f = _ada_rows(lax.dynamic_slice_in_dim(mod_full, me, 1, 0), d)
    csh_m, csc_m = _ada_rows(mod_full[N_DEV:N_DEV + 1], d)[:2]

    bias_s, alog_s = ssd_dt_bias[0], ssd_a_log[0]
    d_full = jnp.repeat(ssd_d[0], SSD_HEAD_DIM)[None, :]
    zeros_state = jnp.zeros((heads // 2, SSD_STATE, LANES), F32)
    zeros_l = jnp.zeros((1, wl), F32)
    wa, wx = lru_w_a[0], lru_w_x[0]

    def mixer_inputs(hh, tag, full, riders=None):
        xbc = _mm([(hh, w_xbc)], "nn", F32, f"{tag}_proj_xbc", rider=riders and riders[0])
        dtr = _mm([(hh, w_dt)], "nn", F32, f"{tag}_proj_dt")[:, :2 * heads]
        xr = _mm([(hh, w_xr)], "nn", F32, f"{tag}_proj_xr", rider=riders and riders[1])
        if riders:
            (xbc, rode0), (xr, rode1) = xbc, xr
            rode = list(rode0) + list(rode1)
        cx = _conv(xbc, conv_w_s, ssd_conv_b, CONV_LEFT, True, f"{tag}_ssd_conv")
        xr_seq = _to_col_major(xr) if full else xr
        xc = _conv(xr_seq, conv_w_l, lru_conv_b, CONV_LEFT, False, f"{tag}_lru_conv")
        return ((xbc, dtr, xr_seq, cx, xc), rode) if riders else (xbc, dtr, xr_seq, cx, xc)

    def lru_args(dirn):
        return wa[dirn], wx[dirn], lru_ba[dirn:dirn + 1], lru_bx[dirn:dirn + 1], lru_lam[dirn:dirn + 1]

    hc = _norm_mod(ctx2, norm_mix, csc_m, csh_m, "ctx_norm")
    c_xbc, c_dtr, c_xr_seq, c_cx, c_xc = mixer_inputs(hc, "ctx", False)
    c_ssd = []
    for dirn in (0, 1):
        _, se, sf = _ssd_fwd(c_cx, c_dtr[:, dirn * heads:(dirn + 1) * heads], bias_s[dirn:dirn + 1], alog_s[dirn:dirn + 1],
                             zeros_state, bool(dirn), f"ctx_ssd_fwd{dirn}")
        c_ssd.append((se, sf))
    c_lru = []
    for dirn in (0, 1):
        res = _lru_fwd(c_xc, *lru_args(dirn), zeros_l, None, bool(dirn), f"ctx_lru_fwd{dirn}")
        c_lru.append((res[0], res[1]))

    h = _norm_mod(x2, norm_mix, sc_m, sh_m, "lat_norm")
    z, gate_bufs = _mm([(h, w_z)], "nn", BF16, "lat_proj_z", rider=gate_g(1, gate_bufs))
    yr, (wg_gate,) = _mm([(h, w_yr)], "nn", BF16, "lat_proj_yr", rider=gate_g(2, gate_bufs))
    w_gf = colcat(wg_gate)
    w_gs, w_gr = w_gf[:, :d], w_gf[:, d:]
    gp_s = _mm([(h, w_gs)], "nn", BF16, "lat_proj_gs")
    gp_r = _mm([(h, w_gr)], "nn", BF16, "lat_proj_gr")
    (l_xbc, l_dtr, l_xr_seq, l_cx, l_xc), late = mixer_inputs(h, "lat", True, riders=(late_a(0), late_b(0)))
    got_a, got_b = late[:1], late[1:]
    ssd_args = lambda dirn: (l_cx, l_dtr[:, dirn * heads:(dirn + 1) * heads], bias_s[dirn:dirn + 1], alog_s[dirn:dirn + 1], c_ssd[dirn][1], bool(dirn))
    (yv0, se0, _), got_a = _ssd_fwd(*ssd_args(0), "lat_ssd_fwd0", rider=late_a(1, got_a))
    (yv1, se1, _), got_b = _ssd_fwd(*ssd_args(1), "lat_ssd_fwd1", rider=late_b(1, got_b))
    l_ssd = [(yv0, se0), (yv1, se1)]
    (hf, _), late = _lru_fwd(l_xc, *lru_args(0), c_lru[0][1], None, False, "lat_lru_fwd0", rider=_merge_riders(late_a(2, got_a), late_b(2, got_b)))
    (wg_13,), (wg_2, wg_os, wg_ol, wg_o) = late[:1], late[1:]
    hb, _, r_seq = _lru_fwd(l_xc, *lru_args(1), c_lru[1][1], hf, True, "lat_lru_fwd1")
    w_os, w_ol, w_oo, w_2 = wg_os.reshape(ws, d), wg_ol.reshape(wl, d), wg_o.reshape(d, d), wg_2.reshape(ff, d)
    w_13f = colcat(wg_13)
    w_1, w_3 = w_13f[:, :ff], w_13f[:, ff:]

    def ssd_out(r, s):
        yf, yb, cx_, z_ = r
        yv = yf + yb + s[0] * cx_[:, :ws]
        yz = yv * _silu(z_)
        return [yz * lax.rsqrt(_rowmean(yz * yz) + EPS) * s[1]], []
    (ys,), _ = _rowwise(ssd_out, [l_ssd[0][0], l_ssd[1][0], l_cx, z], [d_full, ssd_norm], [(ws, BF16)], [], "lat_ssd_out")
    o_s = _mm([(ys, w_os)], "nn", BF16, "lat_out_ssd")

    r_out = _to_row_major(r_seq)
    (gr,), _ = _rowwise(lambda r, s: ([r[0] * _gelu(r[1])], []), [r_out, yr], [], [(wl, BF16)], [], "lat_lru_out")
    o_r = _mm([(gr, w_ol)], "nn", BF16, "lat_out_lru")

    bg_s, bg_r = b_gate[:, :d], b_gate[:, d:]
    (mix,), _ = _rowwise(lambda r, s: ([_sigmoid(r[0] + s[0]) * r[2] + _sigmoid(r[1] + s[1]) * r[3]], []),
                         [gp_s, gp_r, o_s, o_r], [bg_s, bg_r], [(d, BF16)], [], "lat_merge")
    mixo = _mm([(mix, w_oo)], "nn", F32, "lat_out_o")

    def resid_norm(r, s):
        x1v = r[0] + s[0] * r[1]
        xn = x1v * lax.rsqrt(_rowmean(x1v * x1v) + EPS)
        return [x1v, xn * s[1] * (1.0 + s[2]) + s[3]], []
    (x1, h2), _ = _rowwise(resid_norm, [x2, mixo], [g_m, norm_ffn, sc_f, sh_f], [(d, F32), (d, BF16)], [], "lat_resid1")
    fg = _mm([(h2, w_1)], "nn", BF16, "ffn_gate")
    fu = _mm([(h2, w_3)], "nn", BF16, "ffn_up")
    (act,), _ = _rowwise(lambda r, s: ([_silu(r[0]) * r[1]], []), [fg, fu], [], [(ff, BF16)], [], "ffn_act")
    fo = _mm([(act, w_2)], "nn", F32, "ffn_down")

    fn_gain = final_norm[None, :]

    def head(r, s):
        x1v, fv, tv = r
        xv = x1v + s[0] * fv
        rs = lax.rsqrt(_rowmean(xv * xv) + EPS)
        xn = xv * rs
        err = xn * s[1] - tv
        dyv = err * (1.0 / d)
        dxn = dyv * s[1]
        dxv = rs * (dxn - xn * _rowmean(dxn * xn))
        return [dxv, dxv * s[0]], [_colsum(err * err), _colsum(dyv * xn), _colsum(dxv * fv)]
    (dx2, dfo), (sq, g_final, dg_f) = _rowwise(head, [x1, fo, tgt], [g_f, fn_gain], [(d, F32), (d, BF16)], [d, d, d], "loss_head")
    loss = lax.psum(0.5 * jnp.sum(sq) / d, ("x", "y", "c"))

    dact = _mm([(dfo, w_2)], "nt", BF16, "ffn_down_dx")
    gw_2 = _mm([(act, dfo)], "tn", F32, "ffn_down_dw")
    (dfg, dfu), _ = _rowwise(lambda r, s: ([r[0] * r[2] * _dsilu(r[1]), r[0] * _silu(r[1])], []), [dact, fg, fu], [],
                             [(ff, BF16), (ff, BF16)], [], "ffn_act_bwd")
    gw_13 = _mm([(h2, dfg)], "tn", F32, "ffn_gate_dw", slots=(ff // 4, N_DEV, 0, None))
    gw_13 = _mm([(h2, dfu)], "tn", F32, "ffn_up_dw", slots=(ff // 4, N_DEV, N_DEV // 2, gw_13))
    slots = lambda g: jnp.moveaxis(g.reshape(g.shape[0], N_DEV, g.shape[1] // N_DEV), 1, 0)
    rslots = lambda g: g.reshape(N_DEV, g.shape[0] // N_DEV, g.shape[1])
    ffn_rs = _ReduceScatter([gw_13, rslots(gw_2)], "ffn_rs")
    dh2 = _mm([(dfg, w_1)], "nt", F32, "ffn_dx0")
    dh2, got = _mm([(dfu, w_3)], "nt", F32, "ffn_dx1", init=dh2, rider=_swap_rider(*ffn_rs.stage()))
    ffn_rs.absorb(got)
    dx1, (dsh_f, dsc_f, g_norm_ffn) = _norm_mod_bwd(x1, dh2, dx2, norm_ffn, sc_f, "lat_norm_ffn_bwd")

    (dmixo,), (dg_m,) = _rowwise(lambda r, s: ([s[0] * r[0]], [_colsum(r[0] * r[1])]), [dx1, mixo], [g_m], [(d, BF16)], [d], "lat_resid1_bwd")
    dmix = _mm([(dmixo, w_oo)], "nt", BF16, "lat_out_o_dx")
    gw_o = _mm([(mix, dmixo)], "tn", F32, "lat_out_o_dw")

    def merge_bwd(r, s):
        dm, gps, gpr, os_, or_ = r
        g1, g2 = _sigmoid(gps + s[0]), _sigmoid(gpr + s[1])
        dg1, dg2 = dm * os_ * g1 * (1.0 - g1), dm * or_ * g2 * (1.0 - g2)
        return [dm * g1, dm * g2, dg1, dg2], [_colsum(dg1), _colsum(dg2)]
    (do_s, do_r, dgp_s, dgp_r), (gb_gs, gb_gr) = _rowwise(merge_bwd, [dmix, gp_s, gp_r, o_s, o_r], [bg_s, bg_r],
                                                            [(d, BF16)] * 4, [d, d], "lat_merge_bwd")
    gw_gate = _mm([(h, dgp_s)], "tn", F32, "lat_gs_dw", slots=(d // 4, N_DEV, 0, None))
    gw_gate = _mm([(h, dgp_r)], "tn", F32, "lat_gr_dw", slots=(d // 4, N_DEV, N_DEV // 2, gw_gate))

    dgr = _mm([(do_r, w_ol)], "nt", BF16, "lat_out_lru_dx")
    gw_ol = _mm([(gr, do_r)], "tn", F32, "lat_out_lru_dw")
    (dr_out, dyr), _ = _rowwise(lambda r, s: ([r[0] * _gelu(r[2]), r[0] * r[1] * _dgelu(r[2])], []), [dgr, r_out, yr], [],
                                [(wl, F32), (wl, BF16)], [], "lat_lru_out_bwd")
    dr_seq = _to_col_major(dr_out)
    dys = _mm([(do_s, w_os)], "nt", BF16, "lat_out_ssd_dx")
    gw_os = _mm([(ys, do_s)], "tn", F32, "lat_out_ssd_dw")
    mid_rs = _ReduceScatter([rslots(gw_os), rslots(gw_ol), gw_gate, rslots(gw_o)], "mid_rs")
    n_ffn = 2 * ffn_rs.n
    lb0, got = _lru_bwd(l_xc, *lru_args(0), c_lru[0][1], hf, dr_seq, zeros_l, None, False, "lat_lru_bwd0",
                        rider=_merge_riders(_swap_rider(*ffn_rs.stage()), _swap_rider(*mid_rs.stage())))
    ffn_rs.absorb(got[:n_ffn])
    mid_rs.absorb(got[n_ffn:])
    lb1, got = _lru_bwd(l_xc, *lru_args(1), c_lru[1][1], hb, dr_seq, zeros_l, lb0[0], True, "lat_lru_bwd1",
                        rider=_merge_riders(_swap_rider(*ffn_rs.stage()), _swap_rider(*mid_rs.stage())))
    ffn_rs.absorb(got[:n_ffn])
    mid_rs.absorb(got[n_ffn:])
    dxr_seq, g_lconv_w, g_lconv_b = _conv_bwd(l_xr_seq, [lb1[0]], lambda v: v[0], conv_w_l, lru_conv_b, False, "lat_lru_conv_bwd")
    dxr = _to_row_major(dxr_seq)


    def ssd_out_bwd(r, s):
        yf, yb, cx_, z_, dyn = r
        xs_ = cx_[:, :ws]
        yv = yf + yb + s[0] * xs_
        sz = _silu(z_)
        yz = yv * sz
        rs = lax.rsqrt(_rowmean(yz * yz) + EPS)
        yzn = yz * rs
        dyzn = dyn * s[1]
        dyz = rs * (dyzn - yzn * _rowmean(dyzn * yzn))
        dyv = dyz * sz
        return [dyv, dyz * yv * _dsilu(z_), dyv * s[0]], [_colsum(dyv * xs_), _colsum(dyn * yzn)]
    (dy_ssd, dz, dxs_skip), (g_dfull, g_ssd_norm) = _rowwise(ssd_out_bwd, [l_ssd[0][0], l_ssd[1][0], l_cx, z, dys], [d_full, ssd_norm],
                                                              [(ws, F32), (ws, BF16), (ws, F32)], [ws, ws], "lat_ssd_out_bwd")
    sb0, got = _ssd_bwd(l_cx, l_dtr[:, :heads], bias_s[0:1], alog_s[0:1], dy_ssd, l_ssd[0][1], zeros_state, False, "lat_ssd_bwd0",
                        rider=_swap_rider(*mid_rs.stage()))
    mid_rs.absorb(got)
    sb = [sb0, _ssd_bwd(l_cx, l_dtr[:, heads:], bias_s[1:2], alog_s[1:2], dy_ssd, l_ssd[1][1], zeros_state, True, "lat_ssd_bwd1")]
    dxbc, g_sconv_w, g_sconv_b = _conv_bwd(
        l_xbc, [dxs_skip, sb[0][0], sb[1][0], sb[0][1], sb[1][1], sb[0][2], sb[1][2]],
        lambda v: jnp.concatenate([v[0] + v[1] + v[2], v[3] + v[4], v[5] + v[6]], axis=1), conv_w_s, ssd_conv_b, True, "lat_ssd_conv_bwd")
    ddt = jnp.pad(jnp.concatenate([sb[0][3], sb[1][3]], axis=1), ((0, 0), (0, LANES - 2 * heads)))

    zeros_cy = jnp.zeros((tc, ws), F32)
    csb = [_ssd_bwd(c_cx, c_dtr[:, dirn * heads:(dirn + 1) * heads], bias_s[dirn:dirn + 1], alog_s[dirn:dirn + 1],
                    zeros_cy, c_ssd[dirn][0], sb[dirn][6], bool(dirn), f"ctx_ssd_bwd{dirn}") for dirn in (0, 1)]
    c_dxbc, cg_sconv_w, cg_sconv_b = _conv_bwd(
        c_xbc, [csb[0][0], csb[1][0], csb[0][1], csb[1][1], csb[0][2], csb[1][2]],
        lambda v: jnp.concatenate([v[0] + v[1], v[2] + v[3], v[4] + v[5]], axis=1), conv_w_s, ssd_conv_b, True, "ctx_ssd_conv_bwd")
    c_ddt = jnp.pad(jnp.concatenate([csb[0][3], csb[1][3]], axis=1), ((0, 0), (0, LANES - 2 * heads)))
    zeros_ch = jnp.zeros((tc, wl), F32)
    clb0 = _lru_bwd(c_xc, *lru_args(0), zeros_l, c_lru[0][0], zeros_ch, lb0[6], None, False, "ctx_lru_bwd0")
    clb1 = _lru_bwd(c_xc, *lru_args(1), zeros_l, c_lru[1][0], zeros_ch, lb1[6], clb0[0], True, "ctx_lru_bwd1")
    c_dxr, cg_lconv_w, cg_lconv_b = _conv_bwd(c_xr_seq, [clb1[0]], lambda v: v[0], conv_w_l, lru_conv_b, False, "ctx_lru_conv_bwd")
    dhc = _mm([(c_dxbc, w_xbc), (c_ddt, w_dt), (c_dxr, w_xr)], "nt", F32, "ctx_proj_dx")
    _, (dcsh_m, dcsc_m, cg_norm_mix) = _norm_mod_bwd(ctx2, dhc, None, norm_mix, csc_m, "ctx_norm_bwd")

    gw_z = _mm([(h, dz)], "tn", F32, "lat_z_dw")
    gw_yr = _mm([(h, dyr)], "tn", F32, "lat_yr_dw")
    gw_xbc = _mm([(h, dxbc), (hc, c_dxbc)], "tn", F32, "proj_xbc_dw")
    gw_dt = _mm([(h, ddt), (hc, c_ddt)], "tn", F32, "proj_dt_dw")[:, :2 * heads]
    gw_xr = _mm([(h, dxr), (hc, c_dxr)], "tn", F32, "proj_xr_dw")
    in_rs = _ReduceScatter([slots(jnp.concatenate([gw_z, gw_xbc, gw_dt, gw_xr, gw_yr], axis=1))], "in_rs")
    small_parts = [
        g_norm_ffn, g_sconv_w + cg_sconv_w, g_sconv_b + cg_sconv_b,
        jnp.concatenate([sb[0][5] + csb[0][5], sb[1][5] + csb[1][5]], axis=0),
        jnp.concatenate([sb[0][4] + csb[0][4], sb[1][4] + csb[1][4]], axis=0),
        g_dfull, g_ssd_norm, g_lconv_w + cg_lconv_w, g_lconv_b + cg_lconv_b,
        jnp.stack([lb0[1] + clb0[1], lb1[1] + clb1[1]]), jnp.concatenate([lb0[3] + clb0[3], lb1[3] + clb1[3]], axis=0),
        jnp.stack([lb0[2] + clb0[2], lb1[2] + clb1[2]]), jnp.concatenate([lb0[4] + clb0[4], lb1[4] + clb1[4]], axis=0),
        jnp.concatenate([lb0[5] + clb0[5], lb1[5] + clb1[5]], axis=0),
        jnp.concatenate([gb_gs, gb_gr], axis=1), g_final,
    ]
    small_ar = _AllReduce(small_parts, "small_grads", narrow=(9, 11))
    dh, bufs = None, ()
    for rnd, pairs in enumerate(([(dz, w_z), (dxbc, w_xbc), (ddt, w_dt)], [(dxr, w_xr), (dyr, w_yr)], [(dgp_s, w_gs), (dgp_r, w_gr)])):
        send, axes = in_rs.stage()
        dh, got = _mm(pairs, "nt", F32, f"lat_proj_dx{rnd}", init=dh, rider=_merge_riders(_swap_rider(send, axes), small_ar.make(rnd, bufs)))
        in_rs.absorb(got[:len(send)])
        bufs = got[len(send):]
    grad_x, (dsh_m, dsc_m, g_norm_mix) = _norm_mod_bwd(x2, dh, dx1, norm_mix, sc_m, "lat_norm_bwd")

    zero_d = jnp.zeros((1, d), F32)
    dmod_rows = jnp.concatenate([jnp.concatenate([dsh_m, dsc_m, dg_m, dsh_f, dsc_f, dg_f], axis=1),
                                 jnp.concatenate([dcsh_m, dcsc_m, zero_d, zero_d, zero_d, zero_d], axis=1)], axis=0)
    (dmod_g,) = _all_gather([dmod_rows], "gather_dmod")
    dmod_c = functools.reduce(lambda u, v: u + v, [dmod_g[i, 1] for i in range(N_DEV)])
    dmod9 = jnp.concatenate([dmod_g[:, 0], dmod_c[None, :], jnp.zeros((16 - N_DEV - 1, N_MOD * d), F32)], axis=0)
    g_b_ada = jnp.sum(dmod9, axis=0, keepdims=True)
    dmod_l = lax.dynamic_slice_in_dim(dmod9, me * ada_cols, ada_cols, 1)
    g_w_ada = _mm([(cond_act, dmod_l)], "tn", F32, "ada_dw")
    dcond_l = _mm([(dmod_l, w_ada[0])], "nt", F32, "ada_dx")

    (s_nffn, s_sconv_w, s_sconv_b, s_dtb, s_da, s_dfull, s_snorm, s_lconv_w, s_lconv_b,
     s_wa, s_ba, s_wx, s_bx, s_dsp, s_bgate, s_final) = small_ar.finish(bufs)
    s_cctx, s_nmix = _all_reduce([dcond_l[N_DEV:N_DEV + 1], g_norm_mix + cg_norm_mix], "last_grads")
    shard = lambda a, n_: lax.dynamic_slice_in_dim(a, me * n_, n_, a.ndim - 1)
    grads = {
        "c_ctx": (s_cctx * _dsilu(c_ctx[None, :]))[0],
        "w_ada": g_w_ada[None], "b_ada": g_b_ada, "norm_mix": s_nmix, "norm_ffn": s_nffn,
        "ssd_conv_w": shard(s_sconv_w, ssd_conv_w.shape[2])[None], "ssd_conv_b": s_sconv_b,
        "ssd_dt_bias": s_dtb[None], "ssd_a_log": (s_da * (-jnp.exp(alog_s)))[None],
        "ssd_d": jnp.sum(s_dfull.reshape(heads, SSD_HEAD_DIM), axis=1)[None], "ssd_norm": s_snorm,
        "lru_conv_w": shard(s_lconv_w, lru_conv_w.shape[2])[None], "lru_conv_b": s_lconv_b,
        "lru_w_a": s_wa[None], "lru_b_a": shard(s_ba, lru_b_a.shape[2])[None],
        "lru_w_x": s_wx[None], "lru_b_x": shard(s_bx, lru_b_x.shape[2])[None],
        "lru_lambda": shard(s_dsp * (-_sigmoid(-lru_lam)), lru_lambda.shape[2])[None],
        "b_gate": s_bgate, "final_norm": s_final[0],
    }

    for n_, g in zip(["w_in", "w_out_ssd", "w_out_lru", "w_gate", "w_o", "ffn_w13", "ffn_w2"],
                     in_rs.result() + mid_rs.result() + ffn_rs.result()):
        grads[n_] = g[None]

    big_names = ["w_ada", "w_in", "w_out_ssd", "w_out_lru", "w_gate", "w_o", "ffn_w13", "ffn_w2", "lru_w_a", "lru_w_x"]
    small_names = [n_ for n_ in names if n_ not in big_names]
    delta, new_m, new_v = {}, {}, {}
    for n_ in big_names:
        delta[n_], new_m[n_], new_v[n_] = _adamw(wts[n_], grads[n_], mom1[n_], mom2[n_], "adamw_" + n_)
    packed = [_pack([src[n_] for n_ in small_names]) for src in (wts, grads, mom1, mom2)]
    outs = _adamw(*packed, "adamw_small")
    like = [wts[n_] for n_ in small_names]
    for dst, buf in zip((delta, new_m, new_v), outs):
        dst.update(zip(small_names, _unpack(buf, like)))

    return (loss, grad_x[None], *[grads[n_].reshape(wts[n_].shape) for n_ in names], *[delta[n_] for n_ in names],
            *[new_m[n_] for n_ in names], *[new_v[n_] for n_ in names])
```

```python
import functools

import numpy as np
import jax
import jax.numpy as jnp
from jax import lax
from jax.experimental import pallas as pl
from jax.experimental.pallas import tpu as pltpu

F32, BF16 = jnp.float32, jnp.bfloat16
MESH = pl.DeviceIdType.MESH
N_DEV = 8
VMEM_LIMIT_BYTES = 48 * 1024 * 1024
ROW_BUDGET_BYTES = 12 * 1024 * 1024
LANES, SUBLANES = 128, 8

EPS = 1e-6
GRID_W = 64
SSD_HEAD_DIM = 64
SSD_GROUPS = 4
SSD_STATE = 128
SSD_CHUNK = 128
CONV_W = 4
CONV_LEFT = 2
LRU_HEAD_DIM = 128
LRU_C = 8.0
LRU_BLOCK = 128
N_MOD = 6
ADAM_LR, ADAM_B1, ADAM_B2, ADAM_EPS, ADAM_WD, ADAM_STEP = 0.001, 0.9, 0.999, 1e-08, 0.01, 10
NEG_BIG = -1e30


def _params(sem):
    return pltpu.CompilerParams(dimension_semantics=sem, vmem_limit_bytes=VMEM_LIMIT_BYTES)


def _tile(dim, pref, align):
    if dim <= pref:
        return dim
    t = (pref // align) * align
    while t >= align:
        if dim % t == 0:
            return t
        t -= align
    return dim


class _Rider:
    def __init__(self, ins, out_shape, aliases, sems, start, wait):
        self.ins, self.out_shape, self.aliases, self.sems, self.start, self.wait = ins, out_shape, aliases, sems, start, wait


def _merge_riders(a, b):
    ai, ao, asem = len(a.ins), len(a.out_shape), len(a.sems)
    aliases = dict(a.aliases)
    aliases.update({ai + i: ao + o for i, o in b.aliases.items()})

    def both(fa, fb):
        def run(ins, outs, sems):
            fa(ins[:ai], outs[:ao], sems[:asem])
            fb(ins[ai:], outs[ao:], sems[asem:])
        return run

    return _Rider(list(a.ins) + list(b.ins), list(a.out_shape) + list(b.out_shape), aliases, list(a.sems) + list(b.sems),
                  both(a.start, b.start), both(a.wait, b.wait))


def _call(body, name, grid, in_specs, out_specs, out_shape, scratch_shapes, sem, operands, rider=None):
    if rider is None:
        return pl.pallas_call(body, name=name, grid=grid, in_specs=in_specs, out_specs=out_specs, out_shape=out_shape,
                              scratch_shapes=scratch_shapes, compiler_params=_params(sem))(*operands)
    n_in, n_out, n_scr = len(in_specs), len(out_specs), len(scratch_shapes)
    ri, ro = len(rider.ins), len(rider.out_shape)
    a, b = n_in, n_in + ri
    c, d = b + n_out, b + n_out + ro
    e = d + n_scr

    def carried(*refs):
        ids = [pl.program_id(i) for i in range(len(grid))]
        first = functools.reduce(lambda u, v: u & v, [i == 0 for i in ids])
        last = functools.reduce(lambda u, v: u & v, [i == g - 1 for i, g in zip(ids, grid)])

        @pl.when(first)
        def _():
            rider.start(refs[a:b], refs[c:d], refs[e:])

        body(*refs[:a], *refs[b:c], *refs[d:e])

        @pl.when(last)
        def _():
            rider.wait(refs[a:b], refs[c:d], refs[e:])

    any_spec = pl.BlockSpec(memory_space=pl.ANY)
    res = pl.pallas_call(
        carried, name=name, grid=grid, in_specs=list(in_specs) + [any_spec] * ri, out_specs=list(out_specs) + [any_spec] * ro,
        out_shape=list(out_shape) + list(rider.out_shape), scratch_shapes=list(scratch_shapes) + list(rider.sems),
        input_output_aliases={n_in + i: n_out + o for i, o in rider.aliases.items()},
        compiler_params=_params(("arbitrary",) * len(grid)),
    )(*operands, *rider.ins)
    return res[:n_out], res[n_out:]


def _bdot(a, b):
    return jnp.dot(a.astype(BF16), b.astype(BF16), preferred_element_type=F32)


def _bdot_nt(a, b):
    return lax.dot_general(a.astype(BF16), b.astype(BF16), (((1,), (1,)), ((), ())), preferred_element_type=F32)


def _bdot_tn(a, b):
    return lax.dot_general(a.astype(BF16), b.astype(BF16), (((0,), (0,)), ((), ())), preferred_element_type=F32)


def _hdot(a, b):
    return jnp.dot(a, b, preferred_element_type=F32, precision=lax.Precision.HIGHEST)


def _softplus(v):
    return jnp.maximum(v, 0.0) + jnp.log(1.0 + jnp.exp(-jnp.abs(v)))


def _sigmoid(v):
    return 0.5 * jnp.tanh(0.5 * v) + 0.5


def _silu(v):
    return v * _sigmoid(v)


def _dsilu(v):
    s = _sigmoid(v)
    return s * (1.0 + v * (1.0 - s))


GELU_C0 = 0.7978845608028654
GELU_C1 = 0.044715


def _gelu(v):
    return 0.5 * v * (1.0 + jnp.tanh(GELU_C0 * (v + GELU_C1 * v * v * v)))


def _dgelu(v):
    t = jnp.tanh(GELU_C0 * (v + GELU_C1 * v * v * v))
    return 0.5 * (1.0 + t) + 0.5 * v * (1.0 - t * t) * GELU_C0 * (1.0 + 3.0 * GELU_C1 * v * v)


MM_TM, MM_TN = 1024, 1408
MM_TK = {1: 2816, 2: 1536}


def _mm(pairs, mode, out_dtype, name, init=None, rider=None, slots=None):
    a0, b0 = pairs[0]
    if mode == "nn":
        m, n, ks = a0.shape[0], b0.shape[1], [a.shape[1] for a, _ in pairs]
    elif mode == "nt":
        m, n, ks = a0.shape[0], b0.shape[0], [a.shape[1] for a, _ in pairs]
    else:
        m, n, ks = a0.shape[1], b0.shape[1], [a.shape[0] for a, _ in pairs]
    tm = _tile(m, MM_TM, LANES if mode == "tn" else 16)
    tn = _tile(n, MM_TN, LANES) if slots is None else slots[0]
    tk = MM_TK.get(sum(k > 512 for k in ks), 512)
    tks = [_tile(k, tk, 16 if mode == "tn" else LANES) for k in ks]
    steps = [k // t for k, t in zip(ks, tks)]
    offs = [sum(steps[:i]) for i in range(len(pairs))]
    total = sum(steps)
    npairs = len(pairs)

    def kk(i):
        return lambda k: jnp.clip(k - offs[i], 0, steps[i] - 1)

    in_specs, operands = [], []
    for i, (a, b) in enumerate(pairs):
        kf = kk(i)
        if mode == "nn":
            in_specs.append(pl.BlockSpec((tm, tks[i]), lambda mi, ni, k, kf=kf: (mi, kf(k))))
            in_specs.append(pl.BlockSpec((tks[i], tn), lambda mi, ni, k, kf=kf: (kf(k), ni)))
        elif mode == "nt":
            in_specs.append(pl.BlockSpec((tm, tks[i]), lambda mi, ni, k, kf=kf: (mi, kf(k))))
            in_specs.append(pl.BlockSpec((tn, tks[i]), lambda mi, ni, k, kf=kf: (ni, kf(k))))
        else:
            in_specs.append(pl.BlockSpec((tks[i], tm), lambda mi, ni, k, kf=kf: (kf(k), mi)))
            in_specs.append(pl.BlockSpec((tks[i], tn), lambda mi, ni, k, kf=kf: (kf(k), ni)))
        operands += [a, b]
    dot = {"nn": _bdot, "nt": _bdot_nt, "tn": _bdot_tn}[mode]
    if init is not None:
        in_specs.append(pl.BlockSpec((tm, tn), lambda mi, ni, k: (mi, ni)))
        operands.append(init)
    assert out_dtype == F32 or total == 1

    def body(*refs):
        out_ref = refs[-1]
        k = pl.program_id(2)

        def first_step():
            p = dot(refs[0][...], refs[1][...])
            return p if init is None else p + refs[2 * npairs][...]

        if total == 1:
            out_ref[...] = first_step().astype(out_dtype)
            return

        @pl.when(k == 0)
        def _():
            out_ref[...] = first_step()

        for i in range(npairs):
            first = 1 if i == 0 else offs[i]
            if offs[i] + steps[i] > first:
                @pl.when((k >= first) & (k < offs[i] + steps[i]))
                def _(i=i):
                    out_ref[...] += dot(refs[2 * i][...], refs[2 * i + 1][...])

    if slots is not None:
        width, n_slots, first_slot, into = slots
        assert rider is None and init is None and n % width == 0
        aliases = {}
        if into is not None:
            in_specs.append(pl.BlockSpec(memory_space=pl.ANY))
            operands.append(into)
            aliases = {len(operands) - 1: 0}
        return pl.pallas_call(
            body, name=name, grid=(m // tm, n // tn, total), in_specs=in_specs,
            out_specs=pl.BlockSpec((None, tm, tn), lambda mi, ni, k: (ni + first_slot, mi, 0)),
            out_shape=jax.ShapeDtypeStruct((n_slots, m, width), out_dtype), input_output_aliases=aliases,
            compiler_params=_params(("parallel", "parallel", "arbitrary")))(*operands)
    res = _call(body, name, (m // tm, n // tn, total), in_specs, [pl.BlockSpec((tm, tn), lambda mi, ni, k: (mi, ni))],
                [jax.ShapeDtypeStruct((m, n), out_dtype)], [], ("parallel", "parallel", "arbitrary"), operands, rider)
    return res[0] if rider is None else (res[0][0], res[1])


def _rowwise(fn, rows, small, outs, accs, name):
    t = rows[0].shape[0]
    per_row = sum(r.shape[1] * r.dtype.itemsize for r in rows) + sum(c * jnp.dtype(d).itemsize for c, d in outs)
    tm = SUBLANES
    while tm * 2 <= 1024 and t % (tm * 2) == 0 and tm * 2 * per_row <= ROW_BUDGET_BYTES:
        tm *= 2
    if t % tm:
        tm = t
    nrow, nsmall, nout, nacc = len(rows), len(small), len(outs), len(accs)

    def body(*refs):
        rv = [r[...].astype(F32) for r in refs[:nrow]]
        sv = [r[...] for r in refs[nrow:nrow + nsmall]]
        o_refs = refs[nrow + nsmall:nrow + nsmall + nout]
        a_refs = refs[nrow + nsmall + nout:]
        ov, av = fn(rv, sv)
        for r, v in zip(o_refs, ov):
            r[...] = v.astype(r.dtype)
        if nacc:
            @pl.when(pl.program_id(0) == 0)
            def _():
                for r in a_refs:
                    r[...] = jnp.zeros_like(r)
            for r, v in zip(a_refs, av):
                r[...] += v

    in_specs = [pl.BlockSpec((tm, r.shape[1]), lambda i: (i, 0)) for r in rows]
    in_specs += [pl.BlockSpec(s.shape, lambda i: (0, 0)) for s in small]
    out_specs = [pl.BlockSpec((tm, c), lambda i: (i, 0)) for c, _ in outs]
    out_specs += [pl.BlockSpec((1, c), lambda i: (0, 0)) for c in accs]
    out_shape = [jax.ShapeDtypeStruct((t, c), d) for c, d in outs]
    out_shape += [jax.ShapeDtypeStruct((1, c), F32) for c in accs]
    res = pl.pallas_call(
        body, name=name, grid=(t // tm,), in_specs=in_specs, out_specs=out_specs, out_shape=out_shape,
        compiler_params=_params(("arbitrary",)),
    )(*rows, *small)
    return res[:nout], res[nout:]


def _colsum(v):
    return jnp.sum(v, axis=0, keepdims=True)


def _rowmean(v):
    return jnp.mean(v, axis=1, keepdims=True)


def _conv_tiles(x):
    t, c = x.shape
    tm = _tile(t, 256, SUBLANES)
    nt = t // tm
    r8 = tm // SUBLANES
    specs = [
        pl.BlockSpec((SUBLANES, c), lambda i: (jnp.maximum(i * r8 - 1, 0), 0)),
        pl.BlockSpec((tm, c), lambda i: (i, 0)),
        pl.BlockSpec((SUBLANES, c), lambda i: (jnp.minimum((i + 1) * r8, t // SUBLANES - 1), 0)),
    ]
    return tm, nt, specs


def _conv_taps(prev_ref, cur_ref, next_ref, tm, nt, left):
    i = pl.program_id(0)
    p = jnp.where(i > 0, prev_ref[...], 0.0)
    q = jnp.where(i < nt - 1, next_ref[...], 0.0)
    ext = jnp.concatenate([p, cur_ref[...], q], axis=0)
    n = tm + 2 * SUBLANES
    taps = []
    for k in range(CONV_W):
        s = (left - k) % n
        r = pltpu.roll(ext, s, axis=0) if s else ext
        taps.append(r[SUBLANES:SUBLANES + tm])
    return taps


def _conv(x, w, b, left, act, name, out_dtype=F32):
    t, c = x.shape
    tm, nt, specs = _conv_tiles(x)

    def body(prev_ref, cur_ref, next_ref, w_ref, b_ref, o_ref):
        taps = _conv_taps(prev_ref, cur_ref, next_ref, tm, nt, left)
        pre = b_ref[...] + sum(w_ref[k:k + 1, :] * taps[k] for k in range(CONV_W))
        o_ref[...] = (_silu(pre) if act else pre).astype(out_dtype)

    return pl.pallas_call(
        body, name=name, grid=(nt,),
        in_specs=specs + [pl.BlockSpec((CONV_W, c), lambda i: (0, 0)), pl.BlockSpec((1, c), lambda i: (0, 0))],
        out_specs=pl.BlockSpec((tm, c), lambda i: (i, 0)),
        out_shape=jax.ShapeDtypeStruct((t, c), out_dtype), compiler_params=_params(("arbitrary",)),
    )(x, x, x, w, b)


def _conv_bwd(x, dys, combine, w, b, act, name):
    t, c = x.shape
    tm, nt, specs = _conv_tiles(x)
    nd = len(dys)

    def body(*refs):
        prev_ref, cur_ref, next_ref = refs[:3]
        w_ref, b_ref, dpre_ref, dw0, dw1, dw2, dw3, db_ref = refs[3 + nd:]
        dy = combine([r[...] for r in refs[3:3 + nd]])
        taps = _conv_taps(prev_ref, cur_ref, next_ref, tm, nt, CONV_LEFT)
        if act:
            pre = b_ref[...] + sum(w_ref[k:k + 1, :] * taps[k] for k in range(CONV_W))
            dpre = dy * _dsilu(pre)
        else:
            dpre = dy
        dpre_ref[...] = dpre

        @pl.when(pl.program_id(0) == 0)
        def _():
            for r in (dw0, dw1, dw2, dw3, db_ref):
                r[...] = jnp.zeros_like(r)

        for k, r in enumerate((dw0, dw1, dw2, dw3)):
            r[...] += _colsum(dpre * taps[k])
        db_ref[...] += _colsum(dpre)

    row = pl.BlockSpec((1, c), lambda i: (0, 0))
    res = pl.pallas_call(
        body, name=name + "_pre", grid=(nt,),
        in_specs=specs + [pl.BlockSpec((tm, a.shape[1]), lambda i: (i, 0)) for a in dys] + [pl.BlockSpec((CONV_W, c), lambda i: (0, 0)), row],
        out_specs=[pl.BlockSpec((tm, c), lambda i: (i, 0))] + [row] * 5,
        out_shape=[jax.ShapeDtypeStruct((t, c), F32)] + [jax.ShapeDtypeStruct((1, c), F32)] * 5,
        compiler_params=_params(("arbitrary",)),
    )(x, x, x, *dys, w, b)
    dpre, dw, db = res[0], jnp.concatenate(res[1:5], axis=0), res[5]
    dx = _conv(dpre, w[::-1], jnp.zeros_like(b), CONV_W - 1 - CONV_LEFT, False, name + "_dx", BF16)
    return dx, dw, db


def _ssd_dims(xbc):
    t = xbc.shape[0]
    gn = SSD_GROUPS * SSD_STATE
    ws = xbc.shape[1] - 2 * gn
    heads = ws // SSD_HEAD_DIM
    hpg = heads // SSD_GROUPS
    assert hpg % 2 == 0 and ws % gn == 0 and t % SSD_CHUNK == 0
    return t, ws, heads, hpg, heads // 2, t // SSD_CHUNK


def _ssd_chunk_terms(dtr_ref, dtrt_ref, bias_ref, biast_ref, alog_ref, alogt_ref, reverse):
    q = SSD_CHUNK
    a_neg = -jnp.exp(alog_ref[...])
    dt = _softplus(dtr_ref[...] + bias_ref[...])
    dtt = _softplus(dtrt_ref[...] + biast_ref[...])
    a = dt * a_neg
    at = dtt * (-jnp.exp(alogt_ref[...]))
    ii = lax.broadcasted_iota(jnp.int32, (q, q), 0)
    jj = lax.broadcasted_iota(jnp.int32, (q, q), 1)
    mask = (ii <= jj) if reverse else (ii >= jj)
    tri = mask.astype(F32)
    trit = ((jj <= ii) if reverse else (jj >= ii)).astype(F32)
    before = (jj > ii) if reverse else (jj < ii)
    ac = _hdot(tri, a)
    act = _hdot(at, trit)
    tot = _colsum(a)
    return a_neg, dt, a, ac, act, tot, mask, trit, before


def _pair_terms(lo, h1, dt, ac, tot):
    q = SSD_CHUNK

    def colb(v, h):
        return jnp.broadcast_to(v[:, h:h + 1], (v.shape[0], LANES))

    def pairb(v):
        return jnp.where(lo[:v.shape[0]], colb(v, h1), colb(v, h1 + 1))

    dt_p = pairb(dt)
    ac_p = pairb(ac)
    eac_p = jnp.exp(ac_p)
    dte_p = jnp.exp(pairb(jnp.broadcast_to(tot, (q, tot.shape[1]))) - ac_p)
    etot_p = jnp.exp(pairb(tot))
    return dt_p, eac_p, dte_p, etot_p


def _decay_mats(mask, ac, act, cb, h1):
    q = SSD_CHUNK
    out = []
    for h in (h1, h1 + 1):
        diff = jnp.broadcast_to(ac[:, h:h + 1], (q, q)) - jnp.broadcast_to(act[h:h + 1, :], (q, q))
        lmat = jnp.exp(jnp.where(mask, diff, NEG_BIG))
        out.append((lmat, cb * lmat))
    return out


def _ssd_specs(t, ws, heads, npairs, nc, reverse):
    q, gn = SSD_CHUNK, SSD_GROUPS * SSD_STATE
    ci = (lambda k: nc - 1 - k) if reverse else (lambda k: k)
    small = lambda shape: pl.BlockSpec(shape, lambda k: (0,) * len(shape))
    seq = [
        pl.BlockSpec((q, ws), lambda k: (ci(k), 0)),
        pl.BlockSpec((q, gn), lambda k: (ci(k), ws // gn)),
        pl.BlockSpec((q, gn), lambda k: (ci(k), ws // gn + 1)),
        pl.BlockSpec((q, heads), lambda k: (ci(k), 0)),
        pl.BlockSpec((heads, q), lambda k: (0, ci(k))),
        small((1, heads)), small((heads, 1)), small((1, heads)), small((heads, 1)),
    ]
    return ci, small, seq


def _ssd_fwd(xbc, dtr, bias, alog, s0, reverse, name, rider=None):
    t, ws, heads, hpg, npairs, nc = _ssd_dims(xbc)
    q, ns = SSD_CHUNK, SSD_STATE
    ci, small, seq = _ssd_specs(t, ws, heads, npairs, nc, reverse)

    def body(xs_ref, b_ref, c_ref, dtr_ref, dtrt_ref, bias_ref, biast_ref, alog_ref, alogt_ref, s0_ref,
             y_ref, se_ref, s_ref):
        @pl.when(pl.program_id(0) == 0)
        def _():
            s_ref[...] = s0_ref[...]

        _, dt, _, ac, act, tot, mask, _, _ = _ssd_chunk_terms(dtr_ref, dtrt_ref, bias_ref, biast_ref, alog_ref, alogt_ref, reverse)
        lo = lax.broadcasted_iota(jnp.int32, (q, LANES), 1) < SSD_HEAD_DIM
        for g in range(SSD_GROUPS):
            bg = b_ref[:, g * ns:(g + 1) * ns]
            cg = c_ref[:, g * ns:(g + 1) * ns]
            cb = _bdot_nt(cg, bg)
            bgt = bg.T
            for pq in range(hpg // 2):
                pp = g * (hpg // 2) + pq
                h1 = 2 * pp
                cs = slice(pp * LANES, (pp + 1) * LANES)
                (_, m1), (_, m2) = _decay_mats(mask, ac, act, cb, h1)
                dt_p, eac_p, dte_p, etot_p = _pair_terms(lo, h1, dt, ac, tot)
                xp = xs_ref[:, cs] * dt_p
                sp = s_ref[pp]
                se_ref[0, pp] = sp
                y_ref[:, cs] = jnp.where(lo, _bdot(m1, xp), _bdot(m2, xp)) + eac_p * _bdot(cg, sp)
                s_ref[pp] = etot_p * sp + _bdot(bgt, xp * dte_p)

    st = (npairs, ns, LANES)
    return _call(
        body, name, (nc,), seq + [small(st)],
        [pl.BlockSpec((q, ws), lambda k: (ci(k), 0)), pl.BlockSpec((1,) + st, lambda k: (ci(k), 0, 0, 0)), small(st)],
        [jax.ShapeDtypeStruct((t, ws), F32), jax.ShapeDtypeStruct((nc,) + st, F32), jax.ShapeDtypeStruct(st, F32)],
        [], ("arbitrary",), (xbc, xbc, xbc, dtr, dtr.T, bias, bias.T, alog, alog.T, s0), rider)


def _ssd_bwd(xbc, dtr, bias, alog, dy, s_enter, lam0, reverse, name, rider=None):
    t, ws, heads, hpg, npairs, nc = _ssd_dims(xbc)
    q, ns, gn = SSD_CHUNK, SSD_STATE, SSD_GROUPS * SSD_STATE
    ci, small, seq = _ssd_specs(t, ws, heads, npairs, nc, not reverse)
    st = (npairs, ns, LANES)

    e_heads = jnp.asarray(np.arange(ws)[:, None] // SSD_HEAD_DIM == np.arange(heads)[None, :], BF16)

    def body(xs_ref, b_ref, c_ref, dtr_ref, dtrt_ref, bias_ref, biast_ref, alog_ref, alogt_ref, dy_ref, se_ref, lam0_ref, e_ref,
             dxs_ref, db_ref, dc_ref, ddtr_ref, da_ref, dbias_ref, lam_ref, st_s, ts_s, ss_s, xx_s):
        @pl.when(pl.program_id(0) == 0)
        def _():
            lam_ref[...] = lam0_ref[...]
            da_ref[...] = jnp.zeros_like(da_ref)
            dbias_ref[...] = jnp.zeros_like(dbias_ref)

        a_neg, dt, _, ac, act, tot, mask, trit, before = _ssd_chunk_terms(dtr_ref, dtrt_ref, bias_ref, biast_ref, alog_ref, alogt_ref, reverse)
        lo = lax.broadcasted_iota(jnp.int32, (q, LANES), 1) < SSD_HEAD_DIM
        head_col = lax.broadcasted_iota(jnp.int32, (heads, 1), 0)
        ii = lax.broadcasted_iota(jnp.int32, (q, q), 0)
        jj = lax.broadcasted_iota(jnp.int32, (q, q), 1)
        before_t = ((ii > jj) if reverse else (ii < jj)).astype(F32)

        def per_head(v):
            hi = v.astype(BF16)
            lo = (v - hi.astype(F32)).astype(BF16)
            sel = e_ref[...]
            return jnp.dot(hi, sel, preferred_element_type=F32) + jnp.dot(lo, sel, preferred_element_type=F32)

        da_tt = jnp.zeros((heads, q), F32)
        for g in range(SSD_GROUPS):
            gs = slice(g * ns, (g + 1) * ns)
            bg, cg = b_ref[:, gs], c_ref[:, gs]
            cb = _bdot_nt(cg, bg)
            cgt = cg.T
            dcb = jnp.zeros((q, q), F32)
            dbg = jnp.zeros((q, ns), F32)
            dcg = jnp.zeros((q, ns), F32)
            for pq in range(hpg // 2):
                pp = g * (hpg // 2) + pq
                h1 = 2 * pp
                cs = slice(pp * LANES, (pp + 1) * LANES)
                (l1, m1), (l2, m2) = _decay_mats(mask, ac, act, cb, h1)
                dt_p, eac_p, dte_p, etot_p = _pair_terms(lo, h1, dt, ac, tot)
                xsp = xs_ref[:, cs]
                xp = xsp * dt_p
                dyp = dy_ref[:, cs]
                lam = lam_ref[pp]
                sp = se_ref[0, pp]
                dx_state = dte_p * _bdot(bg, lam)
                dxdt = jnp.where(lo, _bdot(m1.T, dyp), _bdot(m2.T, dyp)) + dx_state
                dyx1 = _bdot_nt(jnp.where(lo, dyp, 0.0), xp)
                dyx2 = _bdot_nt(jnp.where(lo, 0.0, dyp), xp)
                dcb = dcb + l1 * dyx1 + l2 * dyx2
                for h, pair in ((h1, m1 * dyx1), (h1 + 1, m2 * dyx2)):
                    cross = _colsum(jnp.where(mask, _bdot(pair, before_t), 0.0))
                    da_tt = da_tt + (head_col == h).astype(F32) * cross
                st_s[:, cs] = dyp * (eac_p * _bdot(cg, sp))
                ts_s[:, cs] = dx_state * xp
                ss_s[:, cs] = jnp.broadcast_to(_colsum(lam * sp) * etot_p, (SUBLANES, LANES))
                xx_s[:, cs] = dxdt * xsp
                edy = eac_p * dyp
                xe = dte_p * xp
                dcg = dcg + _bdot_nt(edy, sp)
                dbg = dbg + _bdot_nt(xe, lam)
                lam_ref[pp] = etot_p * lam + _bdot(cgt, edy)
                dxs_ref[:, cs] = dxdt * dt_p
            db_ref[:, gs] = dbg + _bdot(dcb.T, cg)
            dc_ref[:, gs] = dcg + _bdot(dcb, bg)
        da = da_tt.T + _hdot(trit, per_head(st_s[...])) + _hdot(before.astype(F32), per_head(ts_s[...])) + per_head(ss_s[...])[0:1]
        ddtr = (da * a_neg + per_head(xx_s[...])) * _sigmoid(dtr_ref[...] + bias_ref[...])
        ddtr_ref[...] = ddtr
        da_ref[...] += _colsum(da * dt)
        dbias_ref[...] += _colsum(ddtr)

    row = lambda c: pl.BlockSpec((q, c), lambda k: (ci(k), 0))
    return _call(
        body, name, (nc,),
        seq + [row(ws), pl.BlockSpec((1,) + st, lambda k: (ci(k), 0, 0, 0)), small(st), small((ws, heads))],
        [row(ws), row(gn), row(gn), row(heads), small((1, heads)), small((1, heads)), small(st)],
        [jax.ShapeDtypeStruct((t, ws), F32), jax.ShapeDtypeStruct((t, gn), F32), jax.ShapeDtypeStruct((t, gn), F32),
         jax.ShapeDtypeStruct((t, heads), F32), jax.ShapeDtypeStruct((1, heads), F32),
         jax.ShapeDtypeStruct((1, heads), F32), jax.ShapeDtypeStruct(st, F32)],
        [pltpu.VMEM((q, ws), F32), pltpu.VMEM((q, ws), F32), pltpu.VMEM((SUBLANES, ws), F32), pltpu.VMEM((q, ws), F32)],
        ("arbitrary",), (xbc, xbc, xbc, dtr, dtr.T, bias, bias.T, alog, alog.T, dy, s_enter, lam0, e_heads), rider)


def _tile_scan(a_ref, u_ref, out_ref, carry_ref, ntiles, reverse):
    w = a_ref.shape[1]
    row = lax.broadcasted_iota(jnp.int32, (SUBLANES, w), 0)

    def step(j, _):
        r0 = pl.multiple_of((ntiles - 1 - j if reverse else j) * SUBLANES, SUBLANES)
        a = a_ref[pl.ds(r0, SUBLANES), :]
        u = u_ref[pl.ds(r0, SUBLANES), :]
        for d in (1, 2, 4):
            keep = (row < SUBLANES - d) if reverse else (row >= d)
            s = SUBLANES - d if reverse else d
            a_sh = jnp.where(keep, pltpu.roll(a, s, axis=0), 1.0)
            u_sh = jnp.where(keep, pltpu.roll(u, s, axis=0), 0.0)
            u = a * u_sh + u
            a = a * a_sh
        h = a * carry_ref[...] + u
        out_ref[pl.ds(r0, SUBLANES), :] = h
        last = h[0:1, :] if reverse else h[SUBLANES - 1:SUBLANES, :]
        carry_ref[...] = jnp.broadcast_to(last, (SUBLANES, w))
        return 0

    lax.fori_loop(0, ntiles, step, 0)


def _lru_gates(xh, wa, wx, ba, bx, lam):
    r = _sigmoid(_bdot(xh, wa) + ba)
    i = _sigmoid(_bdot(xh, wx) + bx)
    sp = _softplus(-lam)
    a = jnp.exp(-LRU_C * r * sp)
    return r, i, sp, a


def _lru_fwd(xc, wa, wx, ba, bx, lam, h0, addend, reverse, name, rider=None):
    t, w = xc.shape
    tb = LRU_BLOCK
    nb, nh = t // tb, w // LRU_HEAD_DIM
    bi = (lambda k: nb - 1 - k) if reverse else (lambda k: k)
    has_add = addend is not None

    def body(*refs):
        xc_ref, wa_ref, wx_ref, ba_ref, bx_ref, lam_ref, h0_ref = refs[:7]
        add_ref = refs[7] if has_add else None
        h_ref, hl_ref = refs[7 + has_add], refs[8 + has_add]
        sum_ref = refs[9 + has_add] if has_add else None
        a_s, u_s, carry = refs[-3:]

        @pl.when(pl.program_id(0) == 0)
        def _():
            carry[...] = jnp.broadcast_to(h0_ref[...], carry.shape)

        for hd in range(nh):
            hs = slice(hd * LRU_HEAD_DIM, (hd + 1) * LRU_HEAD_DIM)
            xh = xc_ref[:, hs]
            _, i, _, a = _lru_gates(xh, wa_ref[hd], wx_ref[hd], ba_ref[:, hs], bx_ref[:, hs], lam_ref[:, hs])
            a_s[:, hs] = a
            u_s[:, hs] = jnp.sqrt(1.0 - a * a) * (i * xh)
        _tile_scan(a_s, u_s, h_ref, carry, tb // SUBLANES, reverse)
        hl_ref[...] = carry[0:1, :]
        if has_add:
            sum_ref[...] = h_ref[...] + add_ref[...]

    blk = pl.BlockSpec((tb, w), lambda k: (bi(k), 0))
    vec = pl.BlockSpec((1, w), lambda k: (0, 0))
    wsp = pl.BlockSpec((nh, LRU_HEAD_DIM, LRU_HEAD_DIM), lambda k: (0, 0, 0))
    out_specs = [blk, vec] + ([blk] if has_add else [])
    out_shape = [jax.ShapeDtypeStruct((t, w), F32), jax.ShapeDtypeStruct((1, w), F32)] + ([jax.ShapeDtypeStruct((t, w), F32)] if has_add else [])
    return _call(
        body, name, (nb,), [blk, wsp, wsp, vec, vec, vec, vec] + ([blk] if has_add else []), out_specs, out_shape,
        [pltpu.VMEM((tb, w), F32), pltpu.VMEM((tb, w), F32), pltpu.VMEM((SUBLANES, w), F32)], ("arbitrary",),
        (xc, wa, wx, ba, bx, lam, h0, *([addend] if has_add else [])), rider)


def _lru_bwd(xc, wa, wx, ba, bx, lam, h0, h, dh, dfin, addend, reverse, name, rider=None):
    t, w = xc.shape
    tb = LRU_BLOCK
    nb, nh = t // tb, w // LRU_HEAD_DIM
    r8 = tb // SUBLANES
    bi = (lambda k: k) if reverse else (lambda k: nb - 1 - k)
    if reverse:
        halo = lambda k: (jnp.minimum((bi(k) + 1) * r8, t // SUBLANES - 1), 0)
    else:
        halo = lambda k: (jnp.maximum(bi(k) * r8 - 1, 0), 0)
    has_add = addend is not None
    n_ext = tb + SUBLANES

    def body(*refs):
        xc_ref, wa_ref, wx_ref, ba_ref, bx_ref, lam_ref, h0_ref, h_ref, halo_ref, dh_ref, dfin_ref = refs[:11]
        add_ref = refs[11] if has_add else None
        dxc_ref, dwa_ref, dwx_ref, dba_ref, dbx_ref, dsp_ref, dh0_ref = refs[11 + has_add:18 + has_add]
        a_s, v_s, mu_s, r_s, i_s, carry = refs[-6:]
        k = pl.program_id(0)

        @pl.when(k == 0)
        def _():
            carry[...] = jnp.broadcast_to(dfin_ref[...], carry.shape)
            for r in (dwa_ref, dwx_ref, dba_ref, dbx_ref, dsp_ref):
                r[...] = jnp.zeros_like(r)

        for hd in range(nh):
            hs = slice(hd * LRU_HEAD_DIM, (hd + 1) * LRU_HEAD_DIM)
            r, i, _, a = _lru_gates(xc_ref[:, hs], wa_ref[hd], wx_ref[hd], ba_ref[:, hs], bx_ref[:, hs], lam_ref[:, hs])
            a_s[:, hs] = a
            r_s[:, hs] = r
            i_s[:, hs] = i
            v_s[:, hs] = a * dh_ref[:, hs]
        mu_in = carry[...]
        _tile_scan(a_s, v_s, mu_s, carry, r8, not reverse)
        dh0_ref[...] = carry[0:1, :]
        first = (bi(k) == nb - 1) if reverse else (bi(k) == 0)
        edge = jnp.where(first, jnp.broadcast_to(h0_ref[...], (SUBLANES, w)), halo_ref[...])
        for hd in range(nh):
            hs = slice(hd * LRU_HEAD_DIM, (hd + 1) * LRU_HEAD_DIM)
            if reverse:
                mu_sh = pltpu.roll(jnp.concatenate([mu_in[:, hs], mu_s[:, hs]], axis=0), 1, axis=0)[SUBLANES:]
                h_sh = pltpu.roll(jnp.concatenate([h_ref[:, hs], edge[:, hs]], axis=0), n_ext - 1, axis=0)[:tb]
            else:
                mu_sh = pltpu.roll(jnp.concatenate([mu_s[:, hs], mu_in[:, hs]], axis=0), n_ext - 1, axis=0)[:tb]
                h_sh = pltpu.roll(jnp.concatenate([edge[:, hs], h_ref[:, hs]], axis=0), 1, axis=0)[SUBLANES:]
            xh = xc_ref[:, hs]
            r, i, a = r_s[:, hs], i_s[:, hs], a_s[:, hs]
            sp = _softplus(-lam_ref[:, hs])
            om = 1.0 - a * a
            rs = lax.rsqrt(om)
            lam_t = dh_ref[:, hs] + mu_sh
            dla = lam_t * h_sh * a - lam_t * (i * xh) * (a * a) * rs
            dix = lam_t * (om * rs)
            dpr = dla * (-LRU_C * sp) * r * (1.0 - r)
            dpi = dix * xh * i * (1.0 - i)
            dsp_ref[:, hs] += _colsum(dla * (-LRU_C * r))
            dba_ref[:, hs] += _colsum(dpr)
            dbx_ref[:, hs] += _colsum(dpi)
            dx = dix * i + _bdot_nt(dpr, wa_ref[hd]) + _bdot_nt(dpi, wx_ref[hd])
            xt = xh.T
            dwa_ref[hd] += _bdot(xt, dpr)
            dwx_ref[hd] += _bdot(xt, dpi)
            dxc_ref[:, hs] = dx + add_ref[:, hs] if has_add else dx

    blk = pl.BlockSpec((tb, w), lambda k: (bi(k), 0))
    vec = pl.BlockSpec((1, w), lambda k: (0, 0))
    wsp = pl.BlockSpec((nh, LRU_HEAD_DIM, LRU_HEAD_DIM), lambda k: (0, 0, 0))
    wshape = jax.ShapeDtypeStruct((nh, LRU_HEAD_DIM, LRU_HEAD_DIM), F32)
    vshape = jax.ShapeDtypeStruct((1, w), F32)
    return _call(
        body, name, (nb,),
        [blk, wsp, wsp, vec, vec, vec, vec, blk, pl.BlockSpec((SUBLANES, w), halo), blk, vec] + ([blk] if has_add else []),
        [blk, wsp, wsp, vec, vec, vec, vec],
        [jax.ShapeDtypeStruct((t, w), F32), wshape, wshape, vshape, vshape, vshape, vshape],
        [pltpu.VMEM((tb, w), F32)] * 5 + [pltpu.VMEM((SUBLANES, w), F32)], ("arbitrary",),
        (xc, wa, wx, ba, bx, lam, h0, h, h, dh, dfin, *([addend] if has_add else [])), rider)


def _coords():
    return lax.axis_index("x"), lax.axis_index("y"), lax.axis_index("c")


HALF_ROWS_ALIGN = 32


def _all_gather(shards, name, begin_only=()):
    n, m = len(shards), len(begin_only)
    start, wait, sems = _gather_rounds(shards)
    start2, wait2, sems2 = _gather_rounds(begin_only) if m else (None, None, [])

    def body(*refs):
        ins, ins2 = refs[:n], refs[n:n + m]
        outs, outs2 = refs[n + m:2 * n + m], refs[2 * n + m:2 * (n + m)]
        s1, s2 = refs[2 * (n + m):2 * (n + m) + len(sems)], refs[2 * (n + m) + len(sems):]
        start(0, ins, outs, s1)
        if m:
            start2(0, ins2, outs2, s2)
        wait(0, ins, outs, s1)
        if m:
            wait2(0, ins2, outs2, s2)
        for rnd in (1, 2):
            start(rnd, ins, outs, s1)
            wait(rnd, ins, outs, s1)

    any_spec = pl.BlockSpec(memory_space=pl.ANY)
    return pl.pallas_call(
        body, name=name, in_specs=[any_spec] * (n + m), out_specs=[any_spec] * (n + m),
        out_shape=[jax.ShapeDtypeStruct((N_DEV,) + s.shape, s.dtype) for s in list(shards) + list(begin_only)],
        scratch_shapes=list(sems) + list(sems2),
    )(*shards, *begin_only)


def _gather_riders(shards):
    n = len(shards)
    start, wait, sems = _gather_rounds(shards)
    out_shape = [jax.ShapeDtypeStruct((N_DEV,) + s.shape, s.dtype) for s in shards]

    def make(rnd, bufs=()):
        return _Rider(list(shards) + list(bufs), out_shape, {n + i: i for i in range(len(bufs))}, sems,
                      lambda r_in, r_out, s: start(rnd, r_in[:n], r_out, s), lambda r_in, r_out, s: wait(rnd, r_in[:n], r_out, s))
    return make


def _gather_rounds(shards):
    n = len(shards)
    split = [s.ndim >= 2 and s.shape[0] % HALF_ROWS_ALIGN == 0 for s in shards]
    n_copies = 10

    def plan(ins, outs, sem_refs):
        send_sems, recv_sems, local_sems = sem_refs
        x, y, c = _coords()
        slot = lambda p: 4 * p[0] + 2 * p[1] + p[2]
        me, xp, yp, dg = (x, y, c), (1 - x, y, c), (x, 1 - y, c), (1 - x, 1 - y, c)
        other_core = lambda p: (p[0], p[1], 1 - p[2])

        def rows(ref, i, half):
            r = shards[i].shape[0]
            if half == "all" or not split[i]:
                return ref
            return ref.at[pl.ds(0, r // 2)] if half == "a" else ref.at[pl.ds(r // 2, r // 2)]

        def copy(i, j, holder, half, to, own=False):
            dst = rows(outs[i].at[slot(holder)], i, half)
            src = rows(ins[i], i, half) if own else dst
            return pltpu.make_async_remote_copy(src_ref=src, dst_ref=dst, send_sem=send_sems.at[i, j],
                                                recv_sem=recv_sems.at[i, j], device_id=to, device_id_type=MESH)

        rounds = [
            ([(0, me, "a", xp, True), (1, me, "b", yp, True)],
             [(0, xp, "a"), (1, yp, "b")]),
            ([(2, me, "a", yp, True), (3, xp, "a", yp, False), (4, me, "b", xp, True), (5, yp, "b", xp, False)],
             [(2, yp, "a"), (3, dg, "a"), (4, xp, "b"), (5, dg, "b")]),
            ([(6, me, "all", other_core(me), True)] + [(7 + q, p, "all", other_core(me), False) for q, p in enumerate((xp, yp, dg))],
             [(6 + q, other_core(p), "all") for q, p in enumerate((me, xp, yp, dg))]),
        ]
        local = lambda: [pltpu.make_async_copy(ins[i], outs[i].at[slot(me)], local_sems.at[i]) for i in range(n)]
        travels = lambda i, half: half != "b" or split[i]
        return rounds, copy, local, travels, me

    def start(rnd, ins, outs, sem_refs):
        rounds, copy, local, travels, _ = plan(ins, outs, sem_refs)
        if rnd == 0:
            for cp in local():
                cp.start()
        for j, holder, half, to, own in rounds[rnd][0]:
            for i in range(n):
                if travels(i, half):
                    copy(i, j, holder, half, to, own).start()

    def wait(rnd, ins, outs, sem_refs):
        rounds, copy, local, travels, me = plan(ins, outs, sem_refs)
        for j, holder, half in rounds[rnd][1]:
            for i in range(n):
                if travels(i, half):
                    copy(i, j, holder, half, me).wait_recv()
        for j, holder, half, to, own in rounds[rnd][0]:
            for i in range(n):
                if travels(i, half):
                    copy(i, j, holder, half, to, own).wait_send()
        if rnd == 0:
            for cp in local():
                cp.wait()

    sems = [pltpu.SemaphoreType.DMA((n, n_copies)), pltpu.SemaphoreType.DMA((n, n_copies)), pltpu.SemaphoreType.DMA((n,))]
    return start, wait, sems


def _swap_rider(bufs, axes):
    n = len(bufs)

    def copies(ins, outs, sem_refs):
        x, y, c = _coords()
        to = {"x": (1 - x, y, c), "y": (x, 1 - y, c), "c": (x, y, 1 - c)}
        return [pltpu.make_async_remote_copy(src_ref=ins[i], dst_ref=outs[i], send_sem=sem_refs[0].at[i], recv_sem=sem_refs[1].at[i],
                                             device_id=to[axes[i]], device_id_type=MESH) for i in range(n)]

    def start(ins, outs, sem_refs):
        for cp in copies(ins, outs, sem_refs):
            cp.start()

    def wait(ins, outs, sem_refs):
        for cp in copies(ins, outs, sem_refs):
            cp.wait()

    return _Rider(list(bufs), [jax.ShapeDtypeStruct(b.shape, b.dtype) for b in bufs], {},
                  [pltpu.SemaphoreType.DMA((n,)), pltpu.SemaphoreType.DMA((n,))], start, wait)


def _swap(bufs, axes, name):
    n = len(bufs)
    rider = _swap_rider(bufs, axes)

    def body(*refs):
        rider.start(refs[:n], refs[n:2 * n], refs[2 * n:])
        rider.wait(refs[:n], refs[n:2 * n], refs[2 * n:])

    any_spec = pl.BlockSpec(memory_space=pl.ANY)
    return pl.pallas_call(body, name=name, in_specs=[any_spec] * n, out_specs=[any_spec] * n, out_shape=rider.out_shape,
                          scratch_shapes=rider.sems)(*bufs)


def _as2d(a):
    return a.reshape(-1, a.shape[-1])


def _add_f32(a, b, name):
    (out,), _ = _rowwise(lambda r, s: ([r[0].astype(F32) + r[1].astype(F32)], []), [_as2d(a), _as2d(b)], [],
                         [(a.shape[-1], F32)], [], name)
    return out.reshape(a.shape)


class _ReduceScatter:
    def __init__(self, parts, name):
        self.n, self.name, self.rnd = len(parts), name, 0
        self.cur = [p.reshape((2, 2, 2) + p.shape[1:]) for p in parts]

    def stage(self):
        x, y, c = _coords()
        n = self.n
        if self.rnd == 0:
            bufs, idxs, dims, axes = self.cur, [c] * n, [2] * n, ["c"] * n
        elif self.rnd == 1:
            bufs = [p[:, :, :p.shape[2] // 2] for p in self.cur] + [p[:, :, p.shape[2] // 2:] for p in self.cur]
            idxs, dims, axes = [y] * n + [x] * n, [1] * n + [0] * n, ["y"] * n + ["x"] * n
        else:
            bufs, idxs, dims, axes = self.cur, [x] * n + [y] * n, [0] * (2 * n), ["x"] * n + ["y"] * n
        pick = lambda p, idx, dim: lax.dynamic_index_in_dim(p, idx, dim, keepdims=False)
        self.keep = [pick(p, i_, d_) for p, i_, d_ in zip(bufs, idxs, dims)]
        return [pick(p, 1 - i_, d_).astype(BF16) for p, i_, d_ in zip(bufs, idxs, dims)], axes

    def absorb(self, got):
        self.cur = [_add_f32(k, g, f"{self.name}_add{self.rnd}_{i}") for i, (k, g) in enumerate(zip(self.keep, got))]
        self.rnd += 1

    def result(self):
        return [jnp.concatenate([self.cur[i], self.cur[self.n + i]], axis=0) for i in range(self.n)]


def _reduce_scatter(parts, name):
    rs = _ReduceScatter(parts, name)
    for rnd in range(3):
        send, axes = rs.stage()
        rs.absorb(_swap(send, axes, f"{name}_swap{rnd}"))
    return rs.result()


def _pack(arrs):
    flat = jnp.concatenate([a.reshape(-1).astype(F32) for a in arrs])
    rows = -(-flat.shape[0] // LANES)
    rows = -(-rows // 512) * 512
    return jnp.pad(flat, (0, rows * LANES - flat.shape[0])).reshape(rows, LANES)


def _unpack(buf, like):
    flat, out, off = buf.reshape(-1), [], 0
    for a in like:
        out.append(flat[off:off + a.size].reshape(a.shape))
        off += a.size
    return out


class _AllReduce:
    def __init__(self, arrs, name, narrow=()):
        self.n, self.name, self.narrow = len(arrs), name, narrow
        self.wide = [a for i, a in enumerate(arrs) if i not in narrow]
        self.thin = [a for i, a in enumerate(arrs) if i in narrow]
        self.packs = [_pack(self.wide)] + ([_pack(self.thin).astype(BF16)] if self.thin else [])
        self.make = _gather_riders(self.packs)

    def finish(self, gathered):
        sums = []
        for j, g in enumerate(gathered):
            (tot,), _ = _rowwise(lambda r, s: ([functools.reduce(lambda u, v: u + v, r)], []), [g[i] for i in range(N_DEV)], [],
                                 [(LANES, F32)], [], f"{self.name}_sum{j}")
            sums.append(tot)
        out_wide = iter(_unpack(sums[0], self.wide))
        out_thin = iter(_unpack(sums[1], self.thin) if self.thin else [])
        return [next(out_thin) if i in self.narrow else next(out_wide) for i in range(self.n)]


def _all_reduce(arrs, name, narrow=()):
    ar = _AllReduce(arrs, name, narrow)
    return ar.finish(_all_gather(ar.packs, name + "_gather"))


def _adamw(w, g, m, v, name):
    def fn(r, s):
        w_, g_, m_, v_ = r
        m_ = ADAM_B1 * m_ + (1.0 - ADAM_B1) * g_
        v_ = ADAM_B2 * v_ + (1.0 - ADAM_B2) * jnp.square(g_)
        m_hat = m_ / (1.0 - ADAM_B1 ** ADAM_STEP)
        v_hat = v_ / (1.0 - ADAM_B2 ** ADAM_STEP)
        return [-ADAM_LR * (m_hat / (jnp.sqrt(v_hat) + ADAM_EPS) + ADAM_WD * w_), m_, v_], []

    c = w.shape[-1]
    (d, nm, nv), _ = _rowwise(fn, [_as2d(a) for a in (w, g, m, v)], [], [(c, F32)] * 3, [], name)
    return d.reshape(w.shape), nm.reshape(w.shape), nv.reshape(w.shape)


def _norm_mod(x, gain, sc, sh, name):
    def fn(r, s):
        xv = r[0]
        xn = xv * lax.rsqrt(_rowmean(xv * xv) + EPS)
        return [xn * s[0] * (1.0 + s[1]) + s[2]], []
    (h,), _ = _rowwise(fn, [x], [gain, sc, sh], [(x.shape[1], BF16)], [], name)
    return h


def _norm_mod_bwd(x, dh, dres, gain, sc, name):
    d = x.shape[1]

    def fn(r, s):
        xv, dhv = r[0], r[1]
        rs = lax.rsqrt(_rowmean(xv * xv) + EPS)
        xn = xv * rs
        dxn = dhv * (1.0 + s[1]) * s[0]
        dx = rs * (dxn - xn * _rowmean(dxn * xn))
        if dres is not None:
            dx = dx + r[2]
        return [dx], [_colsum(dhv), _colsum(dhv * xn * s[0]), _colsum(dhv * (1.0 + s[1]) * xn)]
    (dx,), accs = _rowwise(fn, [x, dh] + ([dres] if dres is not None else []), [gain, sc], [(d, F32)], [d, d, d], name)
    return dx, accs


def _ada_rows(mod_row, d):
    return [mod_row[:, i * d:(i + 1) * d] for i in range(N_MOD)]


def _to_col_major(t):
    n, c = t.shape
    return t.reshape(n // GRID_W, GRID_W, c).transpose(1, 0, 2).reshape(n, c)


def _to_row_major(t):
    n, c = t.shape
    return t.reshape(GRID_W, n // GRID_W, c).transpose(1, 0, 2).reshape(n, c)


def kernel(x, c, ctx, c_ctx, w_ada, b_ada, norm_mix, norm_ffn, w_in, ssd_conv_w, ssd_conv_b, ssd_dt_bias, ssd_a_log, ssd_d, ssd_norm, w_out_ssd, lru_conv_w, lru_conv_b, lru_w_a, lru_b_a, lru_w_x, lru_b_x, lru_lambda, w_out_lru, w_gate, b_gate, w_o, ffn_w13, ffn_w2, final_norm, loss_target, m_c_ctx, m_w_ada, m_b_ada, m_norm_mix, m_norm_ffn, m_w_in, m_ssd_conv_w, m_ssd_conv_b, m_ssd_dt_bias, m_ssd_a_log, m_ssd_d, m_ssd_norm, m_w_out_ssd, m_lru_conv_w, m_lru_conv_b, m_lru_w_a, m_lru_b_a, m_lru_w_x, m_lru_b_x, m_lru_lambda, m_w_out_lru, m_w_gate, m_b_gate, m_w_o, m_ffn_w13, m_ffn_w2, m_final_norm, v_c_ctx, v_w_ada, v_b_ada, v_norm_mix, v_norm_ffn, v_w_in, v_ssd_conv_w, v_ssd_conv_b, v_ssd_dt_bias, v_ssd_a_log, v_ssd_d, v_ssd_norm, v_w_out_ssd, v_lru_conv_w, v_lru_conv_b, v_lru_w_a, v_lru_b_a, v_lru_w_x, v_lru_b_x, v_lru_lambda, v_w_out_lru, v_w_gate, v_b_gate, v_w_o, v_ffn_w13, v_ffn_w2, v_final_norm):
    names = ["c_ctx", "w_ada", "b_ada", "norm_mix", "norm_ffn", "w_in", "ssd_conv_w", "ssd_conv_b", "ssd_dt_bias", "ssd_a_log",
             "ssd_d", "ssd_norm", "w_out_ssd", "lru_conv_w", "lru_conv_b", "lru_w_a", "lru_b_a", "lru_w_x", "lru_b_x", "lru_lambda",
             "w_out_lru", "w_gate", "b_gate", "w_o", "ffn_w13", "ffn_w2", "final_norm"]
    wts = dict(zip(names, (c_ctx, w_ada, b_ada, norm_mix, norm_ffn, w_in, ssd_conv_w, ssd_conv_b, ssd_dt_bias, ssd_a_log, ssd_d, ssd_norm, w_out_ssd, lru_conv_w, lru_conv_b, lru_w_a, lru_b_a, lru_w_x, lru_b_x, lru_lambda, w_out_lru, w_gate, b_gate, w_o, ffn_w13, ffn_w2, final_norm)))
    mom1 = dict(zip(names, (m_c_ctx, m_w_ada, m_b_ada, m_norm_mix, m_norm_ffn, m_w_in, m_ssd_conv_w, m_ssd_conv_b, m_ssd_dt_bias, m_ssd_a_log, m_ssd_d, m_ssd_norm, m_w_out_ssd, m_lru_conv_w, m_lru_conv_b, m_lru_w_a, m_lru_b_a, m_lru_w_x, m_lru_b_x, m_lru_lambda, m_w_out_lru, m_w_gate, m_b_gate, m_w_o, m_ffn_w13, m_ffn_w2, m_final_norm)))
    mom2 = dict(zip(names, (v_c_ctx, v_w_ada, v_b_ada, v_norm_mix, v_norm_ffn, v_w_in, v_ssd_conv_w, v_ssd_conv_b, v_ssd_dt_bias, v_ssd_a_log, v_ssd_d, v_ssd_norm, v_w_out_ssd, v_lru_conv_w, v_lru_conv_b, v_lru_w_a, v_lru_b_a, v_lru_w_x, v_lru_b_x, v_lru_lambda, v_w_out_lru, v_w_gate, v_b_gate, v_w_o, v_ffn_w13, v_ffn_w2, v_final_norm)))

    xi, yi, ci_ = _coords()
    me = 4 * xi + 2 * yi + ci_
    x2, tgt, ctx2 = x[0], loss_target[0], ctx[0]
    t, d = x2.shape
    tc = ctx2.shape[0]
    ws = N_DEV * w_out_ssd.shape[1]
    wl = N_DEV * w_out_lru.shape[1]
    heads = ws // SSD_HEAD_DIM
    gn = SSD_GROUPS * SSD_STATE
    nh = wl // LRU_HEAD_DIM
    ff = N_DEV * ffn_w2.shape[1]
    ada_cols = w_ada.shape[2]

    smalls = [c, ssd_conv_w[0], lru_conv_w[0], lru_b_a[0], lru_b_x[0], lru_lambda[0]]
    gate_shard = [w_gate[0].astype(BF16)]
    got = _all_gather(smalls + [w_in[0].astype(BF16)], "gather_weights", begin_only=gate_shard)
    gate_g, gate_bufs = _gather_riders(gate_shard), got[7:]
    late_a = _gather_riders([ffn_w13[0].astype(BF16)])
    late_b = _gather_riders([w.astype(BF16) for w in (ffn_w2[0], w_out_ssd[0], w_out_lru[0], w_o[0])])
    c_all = got[0].reshape(N_DEV, d)
    colcat = lambda g: jnp.moveaxis(g, 0, -2).reshape(g.shape[1:-1] + (N_DEV * g.shape[-1],))
    conv_w_s, conv_w_l, lru_ba, lru_bx, lru_lam = (colcat(g) for g in got[1:6])
    w_in_f = colcat(got[6])
    o_z, o_xbc, o_dt, o_xr, o_yr = 0, ws, 2 * ws + 2 * gn, 2 * ws + 2 * gn + 2 * heads, 2 * ws + 2 * gn + 2 * heads + wl
    w_z, w_xbc = w_in_f[:, o_z:o_xbc], w_in_f[:, o_xbc:o_dt]
    w_dt = jnp.pad(w_in_f[:, o_dt:o_xr], ((0, 0), (0, LANES - 2 * heads)))
    w_xr, w_yr = w_in_f[:, o_xr:o_yr], w_in_f[:, o_yr:]

    cond = jnp.concatenate([c_all, c_ctx[None, :], jnp.zeros((16 - N_DEV - 1, d), F32)], axis=0)
    cond_act = _silu(cond)
    mod_l = _mm([(cond_act, w_ada[0])], "nn", F32, "ada_fwd")
    (mod_g,) = _all_gather([mod_l], "gather_mod")
    mod_full = colcat(mod_g) + b_ada
    sh_m, sc_m, g_m, sh_f, sc_f, g_f = _ada_rows(lax.dynamic_slice_in_dim(mod_full, me, 1, 0), d)
    csh_m, csc_m = _ada_rows(mod_full[N_DEV:N_DEV + 1], d)[:2]

    bias_s, alog_s = ssd_dt_bias[0], ssd_a_log[0]
    d_full = jnp.repeat(ssd_d[0], SSD_HEAD_DIM)[None, :]
    zeros_state = jnp.zeros((heads // 2, SSD_STATE, LANES), F32)
    zeros_l = jnp.zeros((1, wl), F32)
    wa, wx = lru_w_a[0], lru_w_x[0]

    def mixer_inputs(hh, tag, full, riders=None):
        xbc = _mm([(hh, w_xbc)], "nn", F32, f"{tag}_proj_xbc", rider=riders and riders[0])
        dtr = _mm([(hh, w_dt)], "nn", F32, f"{tag}_proj_dt")[:, :2 * heads]
        xr = _mm([(hh, w_xr)], "nn", F32, f"{tag}_proj_xr", rider=riders and riders[1])
        if riders:
            (xbc, rode0), (xr, rode1) = xbc, xr
            rode = list(rode0) + list(rode1)
        cx = _conv(xbc, conv_w_s, ssd_conv_b, CONV_LEFT, True, f"{tag}_ssd_conv")
        xr_seq = _to_col_major(xr) if full else xr
        xc = _conv(xr_seq, conv_w_l, lru_conv_b, CONV_LEFT, False, f"{tag}_lru_conv")
        return ((xbc, dtr, xr_seq, cx, xc), rode) if riders else (xbc, dtr, xr_seq, cx, xc)

    def lru_args(dirn):
        return wa[dirn], wx[dirn], lru_ba[dirn:dirn + 1], lru_bx[dirn:dirn + 1], lru_lam[dirn:dirn + 1]

    hc = _norm_mod(ctx2, norm_mix, csc_m, csh_m, "ctx_norm")
    c_xbc, c_dtr, c_xr_seq, c_cx, c_xc = mixer_inputs(hc, "ctx", False)
    c_ssd = []
    for dirn in (0, 1):
        _, se, sf = _ssd_fwd(c_cx, c_dtr[:, dirn * heads:(dirn + 1) * heads], bias_s[dirn:dirn + 1], alog_s[dirn:dirn + 1],
                             zeros_state, bool(dirn), f"ctx_ssd_fwd{dirn}")
        c_ssd.append((se, sf))
    c_lru = []
    for dirn in (0, 1):
        res = _lru_fwd(c_xc, *lru_args(dirn), zeros_l, None, bool(dirn), f"ctx_lru_fwd{dirn}")
        c_lru.append((res[0], res[1]))

    h = _norm_mod(x2, norm_mix, sc_m, sh_m, "lat_norm")
    z, gate_bufs = _mm([(h, w_z)], "nn", BF16, "lat_proj_z", rider=gate_g(1, gate_bufs))
    yr, (wg_gate,) = _mm([(h, w_yr)], "nn", BF16, "lat_proj_yr", rider=gate_g(2, gate_bufs))
    w_gf = colcat(wg_gate)
    w_gs, w_gr = w_gf[:, :d], w_gf[:, d:]
    gp_s = _mm([(h, w_gs)], "nn", BF16, "lat_proj_gs")
    gp_r = _mm([(h, w_gr)], "nn", BF16, "lat_proj_gr")
    (l_xbc, l_dtr, l_xr_seq, l_cx, l_xc), late = mixer_inputs(h, "lat", True, riders=(late_a(0), late_b(0)))
    got_a, got_b = late[:1], late[1:]
    ssd_args = lambda dirn: (l_cx, l_dtr[:, dirn * heads:(dirn + 1) * heads], bias_s[dirn:dirn + 1], alog_s[dirn:dirn + 1], c_ssd[dirn][1], bool(dirn))
    (yv0, se0, _), got_a = _ssd_fwd(*ssd_args(0), "lat_ssd_fwd0", rider=late_a(1, got_a))
    (yv1, se1, _), got_b = _ssd_fwd(*ssd_args(1), "lat_ssd_fwd1", rider=late_b(1, got_b))
    l_ssd = [(yv0, se0), (yv1, se1)]
    (hf, _), late = _lru_fwd(l_xc, *lru_args(0), c_lru[0][1], None, False, "lat_lru_fwd0", rider=_merge_riders(late_a(2, got_a), late_b(2, got_b)))
    (wg_13,), (wg_2, wg_os, wg_ol, wg_o) = late[:1], late[1:]
    hb, _, r_seq = _lru_fwd(l_xc, *lru_args(1), c_lru[1][1], hf, True, "lat_lru_fwd1")
    w_os, w_ol, w_oo, w_2 = wg_os.reshape(ws, d), wg_ol.reshape(wl, d), wg_o.reshape(d, d), wg_2.reshape(ff, d)
    w_13f = colcat(wg_13)
    w_1, w_3 = w_13f[:, :ff], w_13f[:, ff:]

    def ssd_out(r, s):
        yf, yb, cx_, z_ = r
        yv = yf + yb + s[0] * cx_[:, :ws]
        yz = yv * _silu(z_)
        return [yz * lax.rsqrt(_rowmean(yz * yz) + EPS) * s[1]], []
    (ys,), _ = _rowwise(ssd_out, [l_ssd[0][0], l_ssd[1][0], l_cx, z], [d_full, ssd_norm], [(ws, BF16)], [], "lat_ssd_out")
    o_s = _mm([(ys, w_os)], "nn", BF16, "lat_out_ssd")

    r_out = _to_row_major(r_seq)
    (gr,), _ = _rowwise(lambda r, s: ([r[0] * _gelu(r[1])], []), [r_out, yr], [], [(wl, BF16)], [], "lat_lru_out")
    o_r = _mm([(gr, w_ol)], "nn", BF16, "lat_out_lru")

    bg_s, bg_r = b_gate[:, :d], b_gate[:, d:]
    (mix,), _ = _rowwise(lambda r, s: ([_sigmoid(r[0] + s[0]) * r[2] + _sigmoid(r[1] + s[1]) * r[3]], []),
                         [gp_s, gp_r, o_s, o_r], [bg_s, bg_r], [(d, BF16)], [], "lat_merge")
    mixo = _mm([(mix, w_oo)], "nn", F32, "lat_out_o")

    def resid_norm(r, s):
        x1v = r[0] + s[0] * r[1]
        xn = x1v * lax.rsqrt(_rowmean(x1v * x1v) + EPS)
        return [x1v, xn * s[1] * (1.0 + s[2]) + s[3]], []
    (x1, h2), _ = _rowwise(resid_norm, [x2, mixo], [g_m, norm_ffn, sc_f, sh_f], [(d, F32), (d, BF16)], [], "lat_resid1")
    fg = _mm([(h2, w_1)], "nn", BF16, "ffn_gate")
    fu = _mm([(h2, w_3)], "nn", BF16, "ffn_up")
    (act,), _ = _rowwise(lambda r, s: ([_silu(r[0]) * r[1]], []), [fg, fu], [], [(ff, BF16)], [], "ffn_act")
    fo = _mm([(act, w_2)], "nn", F32, "ffn_down")

    fn_gain = final_norm[None, :]

    def head(r, s):
        x1v, fv, tv = r
        xv = x1v + s[0] * fv
        rs = lax.rsqrt(_rowmean(xv * xv) + EPS)
        xn = xv * rs
        err = xn * s[1] - tv
        dyv = err * (1.0 / d)
        dxn = dyv * s[1]
        dxv = rs * (dxn - xn * _rowmean(dxn * xn))
        return [dxv, dxv * s[0]], [_colsum(err * err), _colsum(dyv * xn), _colsum(dxv * fv)]
    (dx2, dfo), (sq, g_final, dg_f) = _rowwise(head, [x1, fo, tgt], [g_f, fn_gain], [(d, F32), (d, BF16)], [d, d, d], "loss_head")
    loss = lax.psum(0.5 * jnp.sum(sq) / d, ("x", "y", "c"))

    dact = _mm([(dfo, w_2)], "nt", BF16, "ffn_down_dx")
    gw_2 = _mm([(act, dfo)], "tn", F32, "ffn_down_dw")
    (dfg, dfu), _ = _rowwise(lambda r, s: ([r[0] * r[2] * _dsilu(r[1]), r[0] * _silu(r[1])], []), [dact, fg, fu], [],
                             [(ff, BF16), (ff, BF16)], [], "ffn_act_bwd")
    gw_13 = _mm([(h2, dfg)], "tn", F32, "ffn_gate_dw", slots=(ff // 4, N_DEV, 0, None))
    gw_13 = _mm([(h2, dfu)], "tn", F32, "ffn_up_dw", slots=(ff // 4, N_DEV, N_DEV // 2, gw_13))
    slots = lambda g: jnp.moveaxis(g.reshape(g.shape[0], N_DEV, g.shape[1] // N_DEV), 1, 0)
    rslots = lambda g: g.reshape(N_DEV, g.shape[0] // N_DEV, g.shape[1])
    ffn_rs = _ReduceScatter([gw_13, rslots(gw_2)], "ffn_rs")
    dh2 = _mm([(dfg, w_1)], "nt", F32, "ffn_dx0")
    dh2, got = _mm([(dfu, w_3)], "nt", F32, "ffn_dx1", init=dh2, rider=_swap_rider(*ffn_rs.stage()))
    ffn_rs.absorb(got)
    dx1, (dsh_f, dsc_f, g_norm_ffn) = _norm_mod_bwd(x1, dh2, dx2, norm_ffn, sc_f, "lat_norm_ffn_bwd")

    (dmixo,), (dg_m,) = _rowwise(lambda r, s: ([s[0] * r[0]], [_colsum(r[0] * r[1])]), [dx1, mixo], [g_m], [(d, BF16)], [d], "lat_resid1_bwd")
    dmix = _mm([(dmixo, w_oo)], "nt", BF16, "lat_out_o_dx")
    gw_o = _mm([(mix, dmixo)], "tn", F32, "lat_out_o_dw")

    def merge_bwd(r, s):
        dm, gps, gpr, os_, or_ = r
        g1, g2 = _sigmoid(gps + s[0]), _sigmoid(gpr + s[1])
        dg1, dg2 = dm * os_ * g1 * (1.0 - g1), dm * or_ * g2 * (1.0 - g2)
        return [dm * g1, dm * g2, dg1, dg2], [_colsum(dg1), _colsum(dg2)]
    (do_s, do_r, dgp_s, dgp_r), (gb_gs, gb_gr) = _rowwise(merge_bwd, [dmix, gp_s, gp_r, o_s, o_r], [bg_s, bg_r],
                                                            [(d, BF16)] * 4, [d, d], "lat_merge_bwd")
    gw_gate = _mm([(h, dgp_s)], "tn", F32, "lat_gs_dw", slots=(d // 4, N_DEV, 0, None))
    gw_gate = _mm([(h, dgp_r)], "tn", F32, "lat_gr_dw", slots=(d // 4, N_DEV, N_DEV // 2, gw_gate))

    dgr = _mm([(do_r, w_ol)], "nt", BF16, "lat_out_lru_dx")
    gw_ol = _mm([(gr, do_r)], "tn", F32, "lat_out_lru_dw")
    (dr_out, dyr), _ = _rowwise(lambda r, s: ([r[0] * _gelu(r[2]), r[0] * r[1] * _dgelu(r[2])], []), [dgr, r_out, yr], [],
                                [(wl, F32), (wl, BF16)], [], "lat_lru_out_bwd")
    dr_seq = _to_col_major(dr_out)
    dys = _mm([(do_s, w_os)], "nt", BF16, "lat_out_ssd_dx")
    gw_os = _mm([(ys, do_s)], "tn", F32, "lat_out_ssd_dw")
    mid_rs = _ReduceScatter([rslots(gw_os), rslots(gw_ol), gw_gate, rslots(gw_o)], "mid_rs")
    n_ffn = 2 * ffn_rs.n
    lb0, got = _lru_bwd(l_xc, *lru_args(0), c_lru[0][1], hf, dr_seq, zeros_l, None, False, "lat_lru_bwd0",
                        rider=_merge_riders(_swap_rider(*ffn_rs.stage()), _swap_rider(*mid_rs.stage())))
    ffn_rs.absorb(got[:n_ffn])
    mid_rs.absorb(got[n_ffn:])
    lb1, got = _lru_bwd(l_xc, *lru_args(1), c_lru[1][1], hb, dr_seq, zeros_l, lb0[0], True, "lat_lru_bwd1",
                        rider=_merge_riders(_swap_rider(*ffn_rs.stage()), _swap_rider(*mid_rs.stage())))
    ffn_rs.absorb(got[:n_ffn])
    mid_rs.absorb(got[n_ffn:])
    dxr_seq, g_lconv_w, g_lconv_b = _conv_bwd(l_xr_seq, [lb1[0]], lambda v: v[0], conv_w_l, lru_conv_b, False, "lat_lru_conv_bwd")
    dxr = _to_row_major(dxr_seq)


    def ssd_out_bwd(r, s):
        yf, yb, cx_, z_, dyn = r
        xs_ = cx_[:, :ws]
        yv = yf + yb + s[0] * xs_
        sz = _silu(z_)
        yz = yv * sz
        rs = lax.rsqrt(_rowmean(yz * yz) + EPS)
        yzn = yz * rs
        dyzn = dyn * s[1]
        dyz = rs * (dyzn - yzn * _rowmean(dyzn * yzn))
        dyv = dyz * sz
        return [dyv, dyz * yv * _dsilu(z_), dyv * s[0]], [_colsum(dyv * xs_), _colsum(dyn * yzn)]
    (dy_ssd, dz, dxs_skip), (g_dfull, g_ssd_norm) = _rowwise(ssd_out_bwd, [l_ssd[0][0], l_ssd[1][0], l_cx, z, dys], [d_full, ssd_norm],
                                                              [(ws, F32), (ws, BF16), (ws, F32)], [ws, ws], "lat_ssd_out_bwd")
    sb0, got = _ssd_bwd(l_cx, l_dtr[:, :heads], bias_s[0:1], alog_s[0:1], dy_ssd, l_ssd[0][1], zeros_state, False, "lat_ssd_bwd0",
                        rider=_swap_rider(*mid_rs.stage()))
    mid_rs.absorb(got)
    sb = [sb0, _ssd_bwd(l_cx, l_dtr[:, heads:], bias_s[1:2], alog_s[1:2], dy_ssd, l_ssd[1][1], zeros_state, True, "lat_ssd_bwd1")]
    dxbc, g_sconv_w, g_sconv_b = _conv_bwd(
        l_xbc, [dxs_skip, sb[0][0], sb[1][0], sb[0][1], sb[1][1], sb[0][2], sb[1][2]],
        lambda v: jnp.concatenate([v[0] + v[1] + v[2], v[3] + v[4], v[5] + v[6]], axis=1), conv_w_s, ssd_conv_b, True, "lat_ssd_conv_bwd")
    ddt = jnp.pad(jnp.concatenate([sb[0][3], sb[1][3]], axis=1), ((0, 0), (0, LANES - 2 * heads)))

    zeros_cy = jnp.zeros((tc, ws), F32)
    csb = [_ssd_bwd(c_cx, c_dtr[:, dirn * heads:(dirn + 1) * heads], bias_s[dirn:dirn + 1], alog_s[dirn:dirn + 1],
                    zeros_cy, c_ssd[dirn][0], sb[dirn][6], bool(dirn), f"ctx_ssd_bwd{dirn}") for dirn in (0, 1)]
    c_dxbc, cg_sconv_w, cg_sconv_b = _conv_bwd(
        c_xbc, [csb[0][0], csb[1][0], csb[0][1], csb[1][1], csb[0][2], csb[1][2]],
        lambda v: jnp.concatenate([v[0] + v[1], v[2] + v[3], v[4] + v[5]], axis=1), conv_w_s, ssd_conv_b, True, "ctx_ssd_conv_bwd")
    c_ddt = jnp.pad(jnp.concatenate([csb[0][3], csb[1][3]], axis=1), ((0, 0), (0, LANES - 2 * heads)))
    zeros_ch = jnp.zeros((tc, wl), F32)
    clb0 = _lru_bwd(c_xc, *lru_args(0), zeros_l, c_lru[0][0], zeros_ch, lb0[6], None, False, "ctx_lru_bwd0")
    clb1 = _lru_bwd(c_xc, *lru_args(1), zeros_l, c_lru[1][0], zeros_ch, lb1[6], clb0[0], True, "ctx_lru_bwd1")
    c_dxr, cg_lconv_w, cg_lconv_b = _conv_bwd(c_xr_seq, [clb1[0]], lambda v: v[0], conv_w_l, lru_conv_b, False, "ctx_lru_conv_bwd")
    dhc = _mm([(c_dxbc, w_xbc), (c_ddt, w_dt), (c_dxr, w_xr)], "nt", F32, "ctx_proj_dx")
    _, (dcsh_m, dcsc_m, cg_norm_mix) = _norm_mod_bwd(ctx2, dhc, None, norm_mix, csc_m, "ctx_norm_bwd")

    gw_z = _mm([(h, dz)], "tn", F32, "lat_z_dw")
    gw_yr = _mm([(h, dyr)], "tn", F32, "lat_yr_dw")
    gw_xbc = _mm([(h, dxbc), (hc, c_dxbc)], "tn", F32, "proj_xbc_dw")
    gw_dt = _mm([(h, ddt), (hc, c_ddt)], "tn", F32, "proj_dt_dw")[:, :2 * heads]
    gw_xr = _mm([(h, dxr), (hc, c_dxr)], "tn", F32, "proj_xr_dw")
    in_rs = _ReduceScatter([slots(jnp.concatenate([gw_z, gw_xbc, gw_dt, gw_xr, gw_yr], axis=1))], "in_rs")
    small_parts = [
        g_norm_ffn, g_sconv_w + cg_sconv_w, g_sconv_b + cg_sconv_b,
        jnp.concatenate([sb[0][5] + csb[0][5], sb[1][5] + csb[1][5]], axis=0),
        jnp.concatenate([sb[0][4] + csb[0][4], sb[1][4] + csb[1][4]], axis=0),
        g_dfull, g_ssd_norm, g_lconv_w + cg_lconv_w, g_lconv_b + cg_lconv_b,
        jnp.stack([lb0[1] + clb0[1], lb1[1] + clb1[1]]), jnp.concatenate([lb0[3] + clb0[3], lb1[3] + clb1[3]], axis=0),
        jnp.stack([lb0[2] + clb0[2], lb1[2] + clb1[2]]), jnp.concatenate([lb0[4] + clb0[4], lb1[4] + clb1[4]], axis=0),
        jnp.concatenate([lb0[5] + clb0[5], lb1[5] + clb1[5]], axis=0),
        jnp.concatenate([gb_gs, gb_gr], axis=1), g_final,
    ]
    small_ar = _AllReduce(small_parts, "small_grads", narrow=(9, 11))
    dh, bufs = None, ()
    chain = ([(dz, w_z), (dxbc, w_xbc), (ddt, w_dt)], [(dxr, w_xr)], [(dyr, w_yr)], [(dgp_s, w_gs)], [(dgp_r, w_gr)])
    for step, pairs in enumerate(chain):
        rnd = {0: 0, 1: 1, 3: 2}.get(step)
        if rnd is None:
            dh = _mm(pairs, "nt", F32, f"lat_proj_dx{step}", init=dh)
            continue
        send, axes = in_rs.stage()
        dh, got = _mm(pairs, "nt", F32, f"lat_proj_dx{step}", init=dh, rider=_merge_riders(_swap_rider(send, axes), small_ar.make(rnd, bufs)))
        in_rs.absorb(got[:len(send)])
        bufs = got[len(send):]
    grad_x, (dsh_m, dsc_m, g_norm_mix) = _norm_mod_bwd(x2, dh, dx1, norm_mix, sc_m, "lat_norm_bwd")

    zero_d = jnp.zeros((1, d), F32)
    dmod_rows = jnp.concatenate([jnp.concatenate([dsh_m, dsc_m, dg_m, dsh_f, dsc_f, dg_f], axis=1),
                                 jnp.concatenate([dcsh_m, dcsc_m, zero_d, zero_d, zero_d, zero_d], axis=1)], axis=0)
    (dmod_g,) = _all_gather([dmod_rows], "gather_dmod")
    dmod_c = functools.reduce(lambda u, v: u + v, [dmod_g[i, 1] for i in range(N_DEV)])
    dmod9 = jnp.concatenate([dmod_g[:, 0], dmod_c[None, :], jnp.zeros((16 - N_DEV - 1, N_MOD * d), F32)], axis=0)
    g_b_ada = jnp.sum(dmod9, axis=0, keepdims=True)
    dmod_l = lax.dynamic_slice_in_dim(dmod9, me * ada_cols, ada_cols, 1)
    g_w_ada = _mm([(cond_act, dmod_l)], "tn", F32, "ada_dw")
    dcond_l = _mm([(dmod_l, w_ada[0])], "nt", F32, "ada_dx")

    (s_nffn, s_sconv_w, s_sconv_b, s_dtb, s_da, s_dfull, s_snorm, s_lconv_w, s_lconv_b,
     s_wa, s_ba, s_wx, s_bx, s_dsp, s_bgate, s_final) = small_ar.finish(bufs)
    s_cctx, s_nmix = _all_reduce([dcond_l[N_DEV:N_DEV + 1], g_norm_mix + cg_norm_mix], "last_grads")
    shard = lambda a, n_: lax.dynamic_slice_in_dim(a, me * n_, n_, a.ndim - 1)
    grads = {
        "c_ctx": (s_cctx * _dsilu(c_ctx[None, :]))[0],
        "w_ada": g_w_ada[None], "b_ada": g_b_ada, "norm_mix": s_nmix, "norm_ffn": s_nffn,
        "ssd_conv_w": shard(s_sconv_w, ssd_conv_w.shape[2])[None], "ssd_conv_b": s_sconv_b,
        "ssd_dt_bias": s_dtb[None], "ssd_a_log": (s_da * (-jnp.exp(alog_s)))[None],
        "ssd_d": jnp.sum(s_dfull.reshape(heads, SSD_HEAD_DIM), axis=1)[None], "ssd_norm": s_snorm,
        "lru_conv_w": shard(s_lconv_w, lru_conv_w.shape[2])[None], "lru_conv_b": s_lconv_b,
        "lru_w_a": s_wa[None], "lru_b_a": shard(s_ba, lru_b_a.shape[2])[None],
        "lru_w_x": s_wx[None], "lru_b_x": shard(s_bx, lru_b_x.shape[2])[None],
        "lru_lambda": shard(s_dsp * (-_sigmoid(-lru_lam)), lru_lambda.shape[2])[None],
        "b_gate": s_bgate, "final_norm": s_final[0],
    }

    for n_, g in zip(["w_in", "w_out_ssd", "w_out_lru", "w_gate", "w_o", "ffn_w13", "ffn_w2"],
                     in_rs.result() + mid_rs.result() + ffn_rs.result()):
        grads[n_] = g[None]

    big_names = ["w_ada", "w_in", "w_out_ssd", "w_out_lru", "w_gate", "w_o", "ffn_w13", "ffn_w2", "lru_w_a", "lru_w_x"]
    small_names = [n_ for n_ in names if n_ not in big_names]
    delta, new_m, new_v = {}, {}, {}
    for n_ in big_names:
        delta[n_], new_m[n_], new_v[n_] = _adamw(wts[n_], grads[n_], mom1[n_], mom2[n_], "adamw_" + n_)
    packed = [_pack([src[n_] for n_ in small_names]) for src in (wts, grads, mom1, mom2)]
    outs = _adamw(*packed, "adamw_small")
    like = [wts[n_] for n_ in small_names]
    for dst, buf in zip((delta, new_m, new_v), outs):
        dst.update(zip(small_names, _unpack(buf, like)))

    return (loss, grad_x[None], *[grads[n_].reshape(wts[n_].shape) for n_ in names], *[delta[n_] for n_ in names],
            *[new_m[n_] for n_ in names], *[new_v[n_] for n_ in names])
```

```python
import functools

import numpy as np
import jax
import jax.numpy as jnp
from jax import lax
from jax.experimental import pallas as pl
from jax.experimental.pallas import tpu as pltpu

F32, BF16 = jnp.float32, jnp.bfloat16
MESH = pl.DeviceIdType.MESH
N_DEV = 8
VMEM_LIMIT_BYTES = 48 * 1024 * 1024
ROW_BUDGET_BYTES = 12 * 1024 * 1024
LANES, SUBLANES = 128, 8

EPS = 1e-6
GRID_W = 64
SSD_HEAD_DIM = 64
SSD_GROUPS = 4
SSD_STATE = 128
SSD_CHUNK = 128
CONV_W = 4
CONV_LEFT = 2
LRU_HEAD_DIM = 128
LRU_C = 8.0
LRU_BLOCK = 128
N_MOD = 6
ADAM_LR, ADAM_B1, ADAM_B2, ADAM_EPS, ADAM_WD, ADAM_STEP = 0.001, 0.9, 0.999, 1e-08, 0.01, 10
NEG_BIG = -1e30


def _params(sem):
    return pltpu.CompilerParams(dimension_semantics=sem, vmem_limit_bytes=VMEM_LIMIT_BYTES)


def _tile(dim, pref, align):
    if dim <= pref:
        return dim
    t = (pref // align) * align
    while t >= align:
        if dim % t == 0:
            return t
        t -= align
    return dim


class _Rider:
    def __init__(self, ins, out_shape, aliases, sems, start, wait):
        self.ins, self.out_shape, self.aliases, self.sems, self.start, self.wait = ins, out_shape, aliases, sems, start, wait


def _merge_riders(a, b):
    ai, ao, asem = len(a.ins), len(a.out_shape), len(a.sems)
    aliases = dict(a.aliases)
    aliases.update({ai + i: ao + o for i, o in b.aliases.items()})

    def both(fa, fb):
        def run(ins, outs, sems):
            fa(ins[:ai], outs[:ao], sems[:asem])
            fb(ins[ai:], outs[ao:], sems[asem:])
        return run

    return _Rider(list(a.ins) + list(b.ins), list(a.out_shape) + list(b.out_shape), aliases, list(a.sems) + list(b.sems),
                  both(a.start, b.start), both(a.wait, b.wait))


def _call(body, name, grid, in_specs, out_specs, out_shape, scratch_shapes, sem, operands, rider=None):
    if rider is None:
        return pl.pallas_call(body, name=name, grid=grid, in_specs=in_specs, out_specs=out_specs, out_shape=out_shape,
                              scratch_shapes=scratch_shapes, compiler_params=_params(sem))(*operands)
    n_in, n_out, n_scr = len(in_specs), len(out_specs), len(scratch_shapes)
    ri, ro = len(rider.ins), len(rider.out_shape)
    a, b = n_in, n_in + ri
    c, d = b + n_out, b + n_out + ro
    e = d + n_scr

    def carried(*refs):
        ids = [pl.program_id(i) for i in range(len(grid))]
        first = functools.reduce(lambda u, v: u & v, [i == 0 for i in ids])
        last = functools.reduce(lambda u, v: u & v, [i == g - 1 for i, g in zip(ids, grid)])

        @pl.when(first)
        def _():
            rider.start(refs[a:b], refs[c:d], refs[e:])

        body(*refs[:a], *refs[b:c], *refs[d:e])

        @pl.when(last)
        def _():
            rider.wait(refs[a:b], refs[c:d], refs[e:])

    any_spec = pl.BlockSpec(memory_space=pl.ANY)
    res = pl.pallas_call(
        carried, name=name, grid=grid, in_specs=list(in_specs) + [any_spec] * ri, out_specs=list(out_specs) + [any_spec] * ro,
        out_shape=list(out_shape) + list(rider.out_shape), scratch_shapes=list(scratch_shapes) + list(rider.sems),
        input_output_aliases={n_in + i: n_out + o for i, o in rider.aliases.items()},
        compiler_params=_params(("arbitrary",) * len(grid)),
    )(*operands, *rider.ins)
    return res[:n_out], res[n_out:]


def _bdot(a, b):
    return jnp.dot(a.astype(BF16), b.astype(BF16), preferred_element_type=F32)


def _bdot_nt(a, b):
    return lax.dot_general(a.astype(BF16), b.astype(BF16), (((1,), (1,)), ((), ())), preferred_element_type=F32)


def _bdot_tn(a, b):
    return lax.dot_general(a.astype(BF16), b.astype(BF16), (((0,), (0,)), ((), ())), preferred_element_type=F32)


def _hdot(a, b):
    return jnp.dot(a, b, preferred_element_type=F32, precision=lax.Precision.HIGHEST)


def _softplus(v):
    return jnp.maximum(v, 0.0) + jnp.log(1.0 + jnp.exp(-jnp.abs(v)))


def _sigmoid(v):
    return 0.5 * jnp.tanh(0.5 * v) + 0.5


def _silu(v):
    return v * _sigmoid(v)


def _dsilu(v):
    s = _sigmoid(v)
    return s * (1.0 + v * (1.0 - s))


GELU_C0 = 0.7978845608028654
GELU_C1 = 0.044715


def _gelu(v):
    return 0.5 * v * (1.0 + jnp.tanh(GELU_C0 * (v + GELU_C1 * v * v * v)))


def _dgelu(v):
    t = jnp.tanh(GELU_C0 * (v + GELU_C1 * v * v * v))
    return 0.5 * (1.0 + t) + 0.5 * v * (1.0 - t * t) * GELU_C0 * (1.0 + 3.0 * GELU_C1 * v * v)


MM_TM, MM_TN = 1024, 1408
MM_TK = {1: 2816, 2: 1536}


def _mm(pairs, mode, out_dtype, name, init=None, rider=None, slots=None):
    a0, b0 = pairs[0]
    if mode == "nn":
        m, n, ks = a0.shape[0], b0.shape[1], [a.shape[1] for a, _ in pairs]
    elif mode == "nt":
        m, n, ks = a0.shape[0], b0.shape[0], [a.shape[1] for a, _ in pairs]
    else:
        m, n, ks = a0.shape[1], b0.shape[1], [a.shape[0] for a, _ in pairs]
    tm = _tile(m, MM_TM, LANES if mode == "tn" else 16)
    tn = _tile(n, MM_TN, LANES) if slots is None else slots[0]
    tk = MM_TK.get(sum(k > 512 for k in ks), 512)
    tks = [_tile(k, tk, 16 if mode == "tn" else LANES) for k in ks]
    steps = [k // t for k, t in zip(ks, tks)]
    offs = [sum(steps[:i]) for i in range(len(pairs))]
    total = sum(steps)
    npairs = len(pairs)

    def kk(i):
        return lambda k: jnp.clip(k - offs[i], 0, steps[i] - 1)

    in_specs, operands = [], []
    for i, (a, b) in enumerate(pairs):
        kf = kk(i)
        if mode == "nn":
            in_specs.append(pl.BlockSpec((tm, tks[i]), lambda mi, ni, k, kf=kf: (mi, kf(k))))
            in_specs.append(pl.BlockSpec((tks[i], tn), lambda mi, ni, k, kf=kf: (kf(k), ni)))
        elif mode == "nt":
            in_specs.append(pl.BlockSpec((tm, tks[i]), lambda mi, ni, k, kf=kf: (mi, kf(k))))
            in_specs.append(pl.BlockSpec((tn, tks[i]), lambda mi, ni, k, kf=kf: (ni, kf(k))))
        else:
            in_specs.append(pl.BlockSpec((tks[i], tm), lambda mi, ni, k, kf=kf: (kf(k), mi)))
            in_specs.append(pl.BlockSpec((tks[i], tn), lambda mi, ni, k, kf=kf: (kf(k), ni)))
        operands += [a, b]
    dot = {"nn": _bdot, "nt": _bdot_nt, "tn": _bdot_tn}[mode]
    if init is not None:
        in_specs.append(pl.BlockSpec((tm, tn), lambda mi, ni, k: (mi, ni)))
        operands.append(init)
    assert out_dtype == F32 or total == 1

    def body(*refs):
        out_ref = refs[-1]
        k = pl.program_id(2)

        def first_step():
            p = dot(refs[0][...], refs[1][...])
            return p if init is None else p + refs[2 * npairs][...]

        if total == 1:
            out_ref[...] = first_step().astype(out_dtype)
            return

        @pl.when(k == 0)
        def _():
            out_ref[...] = first_step()

        for i in range(npairs):
            first = 1 if i == 0 else offs[i]
            if offs[i] + steps[i] > first:
                @pl.when((k >= first) & (k < offs[i] + steps[i]))
                def _(i=i):
                    out_ref[...] += dot(refs[2 * i][...], refs[2 * i + 1][...])

    if slots is not None:
        width, n_slots, first_slot, into = slots
        assert rider is None and init is None and n % width == 0
        aliases = {}
        if into is not None:
            in_specs.append(pl.BlockSpec(memory_space=pl.ANY))
            operands.append(into)
            aliases = {len(operands) - 1: 0}
        return pl.pallas_call(
            body, name=name, grid=(m // tm, n // tn, total), in_specs=in_specs,
            out_specs=pl.BlockSpec((None, tm, tn), lambda mi, ni, k: (ni + first_slot, mi, 0)),
            out_shape=jax.ShapeDtypeStruct((n_slots, m, width), out_dtype), input_output_aliases=aliases,
            compiler_params=_params(("parallel", "parallel", "arbitrary")))(*operands)
    res = _call(body, name, (m // tm, n // tn, total), in_specs, [pl.BlockSpec((tm, tn), lambda mi, ni, k: (mi, ni))],
                [jax.ShapeDtypeStruct((m, n), out_dtype)], [], ("parallel", "parallel", "arbitrary"), operands, rider)
    return res[0] if rider is None else (res[0][0], res[1])


def _rowwise(fn, rows, small, outs, accs, name):
    t = rows[0].shape[0]
    per_row = sum(r.shape[1] * r.dtype.itemsize for r in rows) + sum(c * jnp.dtype(d).itemsize for c, d in outs)
    tm = SUBLANES
    while tm * 2 <= 1024 and t % (tm * 2) == 0 and tm * 2 * per_row <= ROW_BUDGET_BYTES:
        tm *= 2
    if t % tm:
        tm = t
    nrow, nsmall, nout, nacc = len(rows), len(small), len(outs), len(accs)

    def body(*refs):
        rv = [r[...].astype(F32) for r in refs[:nrow]]
        sv = [r[...] for r in refs[nrow:nrow + nsmall]]
        o_refs = refs[nrow + nsmall:nrow + nsmall + nout]
        a_refs = refs[nrow + nsmall + nout:]
        ov, av = fn(rv, sv)
        for r, v in zip(o_refs, ov):
            r[...] = v.astype(r.dtype)
        if nacc:
            @pl.when(pl.program_id(0) == 0)
            def _():
                for r in a_refs:
                    r[...] = jnp.zeros_like(r)
            for r, v in zip(a_refs, av):
                r[...] += v

    in_specs = [pl.BlockSpec((tm, r.shape[1]), lambda i: (i, 0)) for r in rows]
    in_specs += [pl.BlockSpec(s.shape, lambda i: (0, 0)) for s in small]
    out_specs = [pl.BlockSpec((tm, c), lambda i: (i, 0)) for c, _ in outs]
    out_specs += [pl.BlockSpec((1, c), lambda i: (0, 0)) for c in accs]
    out_shape = [jax.ShapeDtypeStruct((t, c), d) for c, d in outs]
    out_shape += [jax.ShapeDtypeStruct((1, c), F32) for c in accs]
    res = pl.pallas_call(
        body, name=name, grid=(t // tm,), in_specs=in_specs, out_specs=out_specs, out_shape=out_shape,
        compiler_params=_params(("arbitrary",)),
    )(*rows, *small)
    return res[:nout], res[nout:]


def _colsum(v):
    return jnp.sum(v, axis=0, keepdims=True)


def _rowmean(v):
    return jnp.mean(v, axis=1, keepdims=True)


def _conv_tiles(x):
    t, c = x.shape
    tm = _tile(t, 256, SUBLANES)
    nt = t // tm
    r8 = tm // SUBLANES
    specs = [
        pl.BlockSpec((SUBLANES, c), lambda i: (jnp.maximum(i * r8 - 1, 0), 0)),
        pl.BlockSpec((tm, c), lambda i: (i, 0)),
        pl.BlockSpec((SUBLANES, c), lambda i: (jnp.minimum((i + 1) * r8, t // SUBLANES - 1), 0)),
    ]
    return tm, nt, specs


def _conv_taps(prev_ref, cur_ref, next_ref, tm, nt, left):
    i = pl.program_id(0)
    p = jnp.where(i > 0, prev_ref[...], 0.0)
    q = jnp.where(i < nt - 1, next_ref[...], 0.0)
    ext = jnp.concatenate([p, cur_ref[...], q], axis=0)
    n = tm + 2 * SUBLANES
    taps = []
    for k in range(CONV_W):
        s = (left - k) % n
        r = pltpu.roll(ext, s, axis=0) if s else ext
        taps.append(r[SUBLANES:SUBLANES + tm])
    return taps


def _conv(x, w, b, left, act, name, out_dtype=F32):
    t, c = x.shape
    tm, nt, specs = _conv_tiles(x)

    def body(prev_ref, cur_ref, next_ref, w_ref, b_ref, o_ref):
        taps = _conv_taps(prev_ref, cur_ref, next_ref, tm, nt, left)
        pre = b_ref[...] + sum(w_ref[k:k + 1, :] * taps[k] for k in range(CONV_W))
        o_ref[...] = (_silu(pre) if act else pre).astype(out_dtype)

    return pl.pallas_call(
        body, name=name, grid=(nt,),
        in_specs=specs + [pl.BlockSpec((CONV_W, c), lambda i: (0, 0)), pl.BlockSpec((1, c), lambda i: (0, 0))],
        out_specs=pl.BlockSpec((tm, c), lambda i: (i, 0)),
        out_shape=jax.ShapeDtypeStruct((t, c), out_dtype), compiler_params=_params(("arbitrary",)),
    )(x, x, x, w, b)


def _conv_bwd(x, dys, combine, w, b, act, name):
    t, c = x.shape
    tm, nt, specs = _conv_tiles(x)
    nd = len(dys)

    def body(*refs):
        prev_ref, cur_ref, next_ref = refs[:3]
        w_ref, b_ref, dpre_ref, dw0, dw1, dw2, dw3, db_ref = refs[3 + nd:]
        dy = combine([r[...] for r in refs[3:3 + nd]])
        taps = _conv_taps(prev_ref, cur_ref, next_ref, tm, nt, CONV_LEFT)
        if act:
            pre = b_ref[...] + sum(w_ref[k:k + 1, :] * taps[k] for k in range(CONV_W))
            dpre = dy * _dsilu(pre)
        else:
            dpre = dy
        dpre_ref[...] = dpre

        @pl.when(pl.program_id(0) == 0)
        def _():
            for r in (dw0, dw1, dw2, dw3, db_ref):
                r[...] = jnp.zeros_like(r)

        for k, r in enumerate((dw0, dw1, dw2, dw3)):
            r[...] += _colsum(dpre * taps[k])
        db_ref[...] += _colsum(dpre)

    row = pl.BlockSpec((1, c), lambda i: (0, 0))
    res = pl.pallas_call(
        body, name=name + "_pre", grid=(nt,),
        in_specs=specs + [pl.BlockSpec((tm, a.shape[1]), lambda i: (i, 0)) for a in dys] + [pl.BlockSpec((CONV_W, c), lambda i: (0, 0)), row],
        out_specs=[pl.BlockSpec((tm, c), lambda i: (i, 0))] + [row] * 5,
        out_shape=[jax.ShapeDtypeStruct((t, c), F32)] + [jax.ShapeDtypeStruct((1, c), F32)] * 5,
        compiler_params=_params(("arbitrary",)),
    )(x, x, x, *dys, w, b)
    dpre, dw, db = res[0], jnp.concatenate(res[1:5], axis=0), res[5]
    dx = _conv(dpre, w[::-1], jnp.zeros_like(b), CONV_W - 1 - CONV_LEFT, False, name + "_dx", BF16)
    return dx, dw, db


def _ssd_dims(xbc):
    t = xbc.shape[0]
    gn = SSD_GROUPS * SSD_STATE
    ws = xbc.shape[1] - 2 * gn
    heads = ws // SSD_HEAD_DIM
    hpg = heads // SSD_GROUPS
    assert hpg % 2 == 0 and ws % gn == 0 and t % SSD_CHUNK == 0
    return t, ws, heads, hpg, heads // 2, t // SSD_CHUNK


def _ssd_chunk_terms(dtr_ref, dtrt_ref, bias_ref, biast_ref, alog_ref, alogt_ref, reverse):
    q = SSD_CHUNK
    a_neg = -jnp.exp(alog_ref[...])
    dt = _softplus(dtr_ref[...] + bias_ref[...])
    dtt = _softplus(dtrt_ref[...] + biast_ref[...])
    a = dt * a_neg
    at = dtt * (-jnp.exp(alogt_ref[...]))
    ii = lax.broadcasted_iota(jnp.int32, (q, q), 0)
    jj = lax.broadcasted_iota(jnp.int32, (q, q), 1)
    mask = (ii <= jj) if reverse else (ii >= jj)
    tri = mask.astype(F32)
    trit = ((jj <= ii) if reverse else (jj >= ii)).astype(F32)
    before = (jj > ii) if reverse else (jj < ii)
    ac = _hdot(tri, a)
    act = _hdot(at, trit)
    tot = _colsum(a)
    return a_neg, dt, a, ac, act, tot, mask, trit, before


def _pair_terms(lo, h1, dt, ac, tot):
    q = SSD_CHUNK

    def colb(v, h):
        return jnp.broadcast_to(v[:, h:h + 1], (v.shape[0], LANES))

    def pairb(v):
        return jnp.where(lo[:v.shape[0]], colb(v, h1), colb(v, h1 + 1))

    dt_p = pairb(dt)
    ac_p = pairb(ac)
    eac_p = jnp.exp(ac_p)
    dte_p = jnp.exp(pairb(jnp.broadcast_to(tot, (q, tot.shape[1]))) - ac_p)
    etot_p = jnp.exp(pairb(tot))
    return dt_p, eac_p, dte_p, etot_p


def _decay_mats(mask, ac, act, cb, h1):
    q = SSD_CHUNK
    out = []
    for h in (h1, h1 + 1):
        diff = jnp.broadcast_to(ac[:, h:h + 1], (q, q)) - jnp.broadcast_to(act[h:h + 1, :], (q, q))
        lmat = jnp.exp(jnp.where(mask, diff, NEG_BIG))
        out.append((lmat, cb * lmat))
    return out


def _ssd_specs(t, ws, heads, npairs, nc, reverse):
    q, gn = SSD_CHUNK, SSD_GROUPS * SSD_STATE
    ci = (lambda k: nc - 1 - k) if reverse else (lambda k: k)
    small = lambda shape: pl.BlockSpec(shape, lambda k: (0,) * len(shape))
    seq = [
        pl.BlockSpec((q, ws), lambda k: (ci(k), 0)),
        pl.BlockSpec((q, gn), lambda k: (ci(k), ws // gn)),
        pl.BlockSpec((q, gn), lambda k: (ci(k), ws // gn + 1)),
        pl.BlockSpec((q, heads), lambda k: (ci(k), 0)),
        pl.BlockSpec((heads, q), lambda k: (0, ci(k))),
        small((1, heads)), small((heads, 1)), small((1, heads)), small((heads, 1)),
    ]
    return ci, small, seq


def _ssd_fwd(xbc, dtr, bias, alog, s0, reverse, name, rider=None):
    t, ws, heads, hpg, npairs, nc = _ssd_dims(xbc)
    q, ns = SSD_CHUNK, SSD_STATE
    ci, small, seq = _ssd_specs(t, ws, heads, npairs, nc, reverse)

    def body(xs_ref, b_ref, c_ref, dtr_ref, dtrt_ref, bias_ref, biast_ref, alog_ref, alogt_ref, s0_ref,
             y_ref, se_ref, s_ref):
        @pl.when(pl.program_id(0) == 0)
        def _():
            s_ref[...] = s0_ref[...]

        _, dt, _, ac, act, tot, mask, _, _ = _ssd_chunk_terms(dtr_ref, dtrt_ref, bias_ref, biast_ref, alog_ref, alogt_ref, reverse)
        lo = lax.broadcasted_iota(jnp.int32, (q, LANES), 1) < SSD_HEAD_DIM
        for g in range(SSD_GROUPS):
            bg = b_ref[:, g * ns:(g + 1) * ns]
            cg = c_ref[:, g * ns:(g + 1) * ns]
            cb = _bdot_nt(cg, bg)
            bgt = bg.T
            for pq in range(hpg // 2):
                pp = g * (hpg // 2) + pq
                h1 = 2 * pp
                cs = slice(pp * LANES, (pp + 1) * LANES)
                (_, m1), (_, m2) = _decay_mats(mask, ac, act, cb, h1)
                dt_p, eac_p, dte_p, etot_p = _pair_terms(lo, h1, dt, ac, tot)
                xp = xs_ref[:, cs] * dt_p
                sp = s_ref[pp]
                se_ref[0, pp] = sp
                y_ref[:, cs] = jnp.where(lo, _bdot(m1, xp), _bdot(m2, xp)) + eac_p * _bdot(cg, sp)
                s_ref[pp] = etot_p * sp + _bdot(bgt, xp * dte_p)

    st = (npairs, ns, LANES)
    return _call(
        body, name, (nc,), seq + [small(st)],
        [pl.BlockSpec((q, ws), lambda k: (ci(k), 0)), pl.BlockSpec((1,) + st, lambda k: (ci(k), 0, 0, 0)), small(st)],
        [jax.ShapeDtypeStruct((t, ws), F32), jax.ShapeDtypeStruct((nc,) + st, F32), jax.ShapeDtypeStruct(st, F32)],
        [], ("arbitrary",), (xbc, xbc, xbc, dtr, dtr.T, bias, bias.T, alog, alog.T, s0), rider)


def _ssd_bwd(xbc, dtr, bias, alog, dy, s_enter, lam0, reverse, name, rider=None):
    t, ws, heads, hpg, npairs, nc = _ssd_dims(xbc)
    q, ns, gn = SSD_CHUNK, SSD_STATE, SSD_GROUPS * SSD_STATE
    ci, small, seq = _ssd_specs(t, ws, heads, npairs, nc, not reverse)
    st = (npairs, ns, LANES)

    e_heads = jnp.asarray(np.arange(ws)[:, None] // SSD_HEAD_DIM == np.arange(heads)[None, :], BF16)

    def body(xs_ref, b_ref, c_ref, dtr_ref, dtrt_ref, bias_ref, biast_ref, alog_ref, alogt_ref, dy_ref, se_ref, lam0_ref, e_ref,
             dxs_ref, db_ref, dc_ref, ddtr_ref, da_ref, dbias_ref, lam_ref, st_s, ts_s, ss_s, xx_s):
        @pl.when(pl.program_id(0) == 0)
        def _():
            lam_ref[...] = lam0_ref[...]
            da_ref[...] = jnp.zeros_like(da_ref)
            dbias_ref[...] = jnp.zeros_like(dbias_ref)

        a_neg, dt, _, ac, act, tot, mask, trit, before = _ssd_chunk_terms(dtr_ref, dtrt_ref, bias_ref, biast_ref, alog_ref, alogt_ref, reverse)
        lo = lax.broadcasted_iota(jnp.int32, (q, LANES), 1) < SSD_HEAD_DIM
        head_col = lax.broadcasted_iota(jnp.int32, (heads, 1), 0)
        ii = lax.broadcasted_iota(jnp.int32, (q, q), 0)
        jj = lax.broadcasted_iota(jnp.int32, (q, q), 1)
        before_t = ((ii > jj) if reverse else (ii < jj)).astype(F32)

        def per_head(v):
            hi = v.astype(BF16)
            lo = (v - hi.astype(F32)).astype(BF16)
            sel = e_ref[...]
            return jnp.dot(hi, sel, preferred_element_type=F32) + jnp.dot(lo, sel, preferred_element_type=F32)

        da_tt = jnp.zeros((heads, q), F32)
        for g in range(SSD_GROUPS):
            gs = slice(g * ns, (g + 1) * ns)
            bg, cg = b_ref[:, gs], c_ref[:, gs]
            cb = _bdot_nt(cg, bg)
            cgt = cg.T
            dcb = jnp.zeros((q, q), F32)
            dbg = jnp.zeros((q, ns), F32)
            dcg = jnp.zeros((q, ns), F32)
            for pq in range(hpg // 2):
                pp = g * (hpg // 2) + pq
                h1 = 2 * pp
                cs = slice(pp * LANES, (pp + 1) * LANES)
                (l1, m1), (l2, m2) = _decay_mats(mask, ac, act, cb, h1)
                dt_p, eac_p, dte_p, etot_p = _pair_terms(lo, h1, dt, ac, tot)
                xsp = xs_ref[:, cs]
                xp = xsp * dt_p
                dyp = dy_ref[:, cs]
                lam = lam_ref[pp]
                sp = se_ref[0, pp]
                dx_state = dte_p * _bdot(bg, lam)
                dxdt = jnp.where(lo, _bdot(m1.T, dyp), _bdot(m2.T, dyp)) + dx_state
                dyx1 = _bdot_nt(jnp.where(lo, dyp, 0.0), xp)
                dyx2 = _bdot_nt(jnp.where(lo, 0.0, dyp), xp)
                dcb = dcb + l1 * dyx1 + l2 * dyx2
                for h, pair in ((h1, m1 * dyx1), (h1 + 1, m2 * dyx2)):
                    cross = _colsum(jnp.where(mask, _bdot(pair, before_t), 0.0))
                    da_tt = da_tt + (head_col == h).astype(F32) * cross
                st_s[:, cs] = dyp * (eac_p * _bdot(cg, sp))
                ts_s[:, cs] = dx_state * xp
                ss_s[:, cs] = jnp.broadcast_to(_colsum(lam * sp) * etot_p, (SUBLANES, LANES))
                xx_s[:, cs] = dxdt * xsp
                edy = eac_p * dyp
                xe = dte_p * xp
                dcg = dcg + _bdot_nt(edy, sp)
                dbg = dbg + _bdot_nt(xe, lam)
                lam_ref[pp] = etot_p * lam + _bdot(cgt, edy)
                dxs_ref[:, cs] = dxdt * dt_p
            db_ref[:, gs] = dbg + _bdot(dcb.T, cg)
            dc_ref[:, gs] = dcg + _bdot(dcb, bg)
        da = da_tt.T + _hdot(trit, per_head(st_s[...])) + _hdot(before.astype(F32), per_head(ts_s[...])) + per_head(ss_s[...])[0:1]
        ddtr = (da * a_neg + per_head(xx_s[...])) * _sigmoid(dtr_ref[...] + bias_ref[...])
        ddtr_ref[...] = ddtr
        da_ref[...] += _colsum(da * dt)
        dbias_ref[...] += _colsum(ddtr)

    row = lambda c: pl.BlockSpec((q, c), lambda k: (ci(k), 0))
    return _call(
        body, name, (nc,),
        seq + [row(ws), pl.BlockSpec((1,) + st, lambda k: (ci(k), 0, 0, 0)), small(st), small((ws, heads))],
        [row(ws), row(gn), row(gn), row(heads), small((1, heads)), small((1, heads)), small(st)],
        [jax.ShapeDtypeStruct((t, ws), F32), jax.ShapeDtypeStruct((t, gn), F32), jax.ShapeDtypeStruct((t, gn), F32),
         jax.ShapeDtypeStruct((t, heads), F32), jax.ShapeDtypeStruct((1, heads), F32),
         jax.ShapeDtypeStruct((1, heads), F32), jax.ShapeDtypeStruct(st, F32)],
        [pltpu.VMEM((q, ws), F32), pltpu.VMEM((q, ws), F32), pltpu.VMEM((SUBLANES, ws), F32), pltpu.VMEM((q, ws), F32)],
        ("arbitrary",), (xbc, xbc, xbc, dtr, dtr.T, bias, bias.T, alog, alog.T, dy, s_enter, lam0, e_heads), rider)


def _tile_scan(a_ref, u_ref, out_ref, carry_ref, ntiles, reverse):
    w = a_ref.shape[1]
    row = lax.broadcasted_iota(jnp.int32, (SUBLANES, w), 0)

    def step(j, _):
        r0 = pl.multiple_of((ntiles - 1 - j if reverse else j) * SUBLANES, SUBLANES)
        a = a_ref[pl.ds(r0, SUBLANES), :]
        u = u_ref[pl.ds(r0, SUBLANES), :]
        for d in (1, 2, 4):
            keep = (row < SUBLANES - d) if reverse else (row >= d)
            s = SUBLANES - d if reverse else d
            a_sh = jnp.where(keep, pltpu.roll(a, s, axis=0), 1.0)
            u_sh = jnp.where(keep, pltpu.roll(u, s, axis=0), 0.0)
            u = a * u_sh + u
            a = a * a_sh
        h = a * carry_ref[...] + u
        out_ref[pl.ds(r0, SUBLANES), :] = h
        last = h[0:1, :] if reverse else h[SUBLANES - 1:SUBLANES, :]
        carry_ref[...] = jnp.broadcast_to(last, (SUBLANES, w))
        return 0

    lax.fori_loop(0, ntiles, step, 0)


def _lru_gates(xh, wa, wx, ba, bx, lam):
    r = _sigmoid(_bdot(xh, wa) + ba)
    i = _sigmoid(_bdot(xh, wx) + bx)
    sp = _softplus(-lam)
    a = jnp.exp(-LRU_C * r * sp)
    return r, i, sp, a


def _lru_fwd(xc, wa, wx, ba, bx, lam, h0, addend, reverse, name, rider=None):
    t, w = xc.shape
    tb = LRU_BLOCK
    nb, nh = t // tb, w // LRU_HEAD_DIM
    bi = (lambda k: nb - 1 - k) if reverse else (lambda k: k)
    has_add = addend is not None

    def body(*refs):
        xc_ref, wa_ref, wx_ref, ba_ref, bx_ref, lam_ref, h0_ref = refs[:7]
        add_ref = refs[7] if has_add else None
        h_ref, hl_ref = refs[7 + has_add], refs[8 + has_add]
        sum_ref = refs[9 + has_add] if has_add else None
        a_s, u_s, carry = refs[-3:]

        @pl.when(pl.program_id(0) == 0)
        def _():
            carry[...] = jnp.broadcast_to(h0_ref[...], carry.shape)

        for hd in range(nh):
            hs = slice(hd * LRU_HEAD_DIM, (hd + 1) * LRU_HEAD_DIM)
            xh = xc_ref[:, hs]
            _, i, _, a = _lru_gates(xh, wa_ref[hd], wx_ref[hd], ba_ref[:, hs], bx_ref[:, hs], lam_ref[:, hs])
            a_s[:, hs] = a
            u_s[:, hs] = jnp.sqrt(1.0 - a * a) * (i * xh)
        _tile_scan(a_s, u_s, h_ref, carry, tb // SUBLANES, reverse)
        hl_ref[...] = carry[0:1, :]
        if has_add:
            sum_ref[...] = h_ref[...] + add_ref[...]

    blk = pl.BlockSpec((tb, w), lambda k: (bi(k), 0))
    vec = pl.BlockSpec((1, w), lambda k: (0, 0))
    wsp = pl.BlockSpec((nh, LRU_HEAD_DIM, LRU_HEAD_DIM), lambda k: (0, 0, 0))
    out_specs = [blk, vec] + ([blk] if has_add else [])
    out_shape = [jax.ShapeDtypeStruct((t, w), F32), jax.ShapeDtypeStruct((1, w), F32)] + ([jax.ShapeDtypeStruct((t, w), F32)] if has_add else [])
    return _call(
        body, name, (nb,), [blk, wsp, wsp, vec, vec, vec, vec] + ([blk] if has_add else []), out_specs, out_shape,
        [pltpu.VMEM((tb, w), F32), pltpu.VMEM((tb, w), F32), pltpu.VMEM((SUBLANES, w), F32)], ("arbitrary",),
        (xc, wa, wx, ba, bx, lam, h0, *([addend] if has_add else [])), rider)


def _lru_bwd(xc, wa, wx, ba, bx, lam, h0, h, dh, dfin, addend, reverse, name, rider=None):
    t, w = xc.shape
    tb = LRU_BLOCK
    nb, nh = t // tb, w // LRU_HEAD_DIM
    r8 = tb // SUBLANES
    bi = (lambda k: k) if reverse else (lambda k: nb - 1 - k)
    if reverse:
        halo = lambda k: (jnp.minimum((bi(k) + 1) * r8, t // SUBLANES - 1), 0)
    else:
        halo = lambda k: (jnp.maximum(bi(k) * r8 - 1, 0), 0)
    has_add = addend is not None
    n_ext = tb + SUBLANES

    def body(*refs):
        xc_ref, wa_ref, wx_ref, ba_ref, bx_ref, lam_ref, h0_ref, h_ref, halo_ref, dh_ref, dfin_ref = refs[:11]
        add_ref = refs[11] if has_add else None
        dxc_ref, dwa_ref, dwx_ref, dba_ref, dbx_ref, dsp_ref, dh0_ref = refs[11 + has_add:18 + has_add]
        a_s, v_s, mu_s, r_s, i_s, carry = refs[-6:]
        k = pl.program_id(0)

        @pl.when(k == 0)
        def _():
            carry[...] = jnp.broadcast_to(dfin_ref[...], carry.shape)
            for r in (dwa_ref, dwx_ref, dba_ref, dbx_ref, dsp_ref):
                r[...] = jnp.zeros_like(r)

        for hd in range(nh):
            hs = slice(hd * LRU_HEAD_DIM, (hd + 1) * LRU_HEAD_DIM)
            r, i, _, a = _lru_gates(xc_ref[:, hs], wa_ref[hd], wx_ref[hd], ba_ref[:, hs], bx_ref[:, hs], lam_ref[:, hs])
            a_s[:, hs] = a
            r_s[:, hs] = r
            i_s[:, hs] = i
            v_s[:, hs] = a * dh_ref[:, hs]
        mu_in = carry[...]
        _tile_scan(a_s, v_s, mu_s, carry, r8, not reverse)
        dh0_ref[...] = carry[0:1, :]
        first = (bi(k) == nb - 1) if reverse else (bi(k) == 0)
        edge = jnp.where(first, jnp.broadcast_to(h0_ref[...], (SUBLANES, w)), halo_ref[...])
        for hd in range(nh):
            hs = slice(hd * LRU_HEAD_DIM, (hd + 1) * LRU_HEAD_DIM)
            if reverse:
                mu_sh = pltpu.roll(jnp.concatenate([mu_in[:, hs], mu_s[:, hs]], axis=0), 1, axis=0)[SUBLANES:]
                h_sh = pltpu.roll(jnp.concatenate([h_ref[:, hs], edge[:, hs]], axis=0), n_ext - 1, axis=0)[:tb]
            else:
                mu_sh = pltpu.roll(jnp.concatenate([mu_s[:, hs], mu_in[:, hs]], axis=0), n_ext - 1, axis=0)[:tb]
                h_sh = pltpu.roll(jnp.concatenate([edge[:, hs], h_ref[:, hs]], axis=0), 1, axis=0)[SUBLANES:]
            xh = xc_ref[:, hs]
            r, i, a = r_s[:, hs], i_s[:, hs], a_s[:, hs]
            sp = _softplus(-lam_ref[:, hs])
            om = 1.0 - a * a
            rs = lax.rsqrt(om)
            lam_t = dh_ref[:, hs] + mu_sh
            dla = lam_t * h_sh * a - lam_t * (i * xh) * (a * a) * rs
            dix = lam_t * (om * rs)
            dpr = dla * (-LRU_C * sp) * r * (1.0 - r)
            dpi = dix * xh * i * (1.0 - i)
            dsp_ref[:, hs] += _colsum(dla * (-LRU_C * r))
            dba_ref[:, hs] += _colsum(dpr)
            dbx_ref[:, hs] += _colsum(dpi)
            dx = dix * i + _bdot_nt(dpr, wa_ref[hd]) + _bdot_nt(dpi, wx_ref[hd])
            xt = xh.T
            dwa_ref[hd] += _bdot(xt, dpr)
            dwx_ref[hd] += _bdot(xt, dpi)
            dxc_ref[:, hs] = dx + add_ref[:, hs] if has_add else dx

    blk = pl.BlockSpec((tb, w), lambda k: (bi(k), 0))
    vec = pl.BlockSpec((1, w), lambda k: (0, 0))
    wsp = pl.BlockSpec((nh, LRU_HEAD_DIM, LRU_HEAD_DIM), lambda k: (0, 0, 0))
    wshape = jax.ShapeDtypeStruct((nh, LRU_HEAD_DIM, LRU_HEAD_DIM), F32)
    vshape = jax.ShapeDtypeStruct((1, w), F32)
    return _call(
        body, name, (nb,),
        [blk, wsp, wsp, vec, vec, vec, vec, blk, pl.BlockSpec((SUBLANES, w), halo), blk, vec] + ([blk] if has_add else []),
        [blk, wsp, wsp, vec, vec, vec, vec],
        [jax.ShapeDtypeStruct((t, w), F32), wshape, wshape, vshape, vshape, vshape, vshape],
        [pltpu.VMEM((tb, w), F32)] * 5 + [pltpu.VMEM((SUBLANES, w), F32)], ("arbitrary",),
        (xc, wa, wx, ba, bx, lam, h0, h, h, dh, dfin, *([addend] if has_add else [])), rider)


def _coords():
    return lax.axis_index("x"), lax.axis_index("y"), lax.axis_index("c")


HALF_ROWS_ALIGN = 32


def _all_gather(shards, name, begin_only=()):
    n, m = len(shards), len(begin_only)
    start, wait, sems = _gather_rounds(shards)
    start2, wait2, sems2 = _gather_rounds(begin_only) if m else (None, None, [])

    def body(*refs):
        ins, ins2 = refs[:n], refs[n:n + m]
        outs, outs2 = refs[n + m:2 * n + m], refs[2 * n + m:2 * (n + m)]
        s1, s2 = refs[2 * (n + m):2 * (n + m) + len(sems)], refs[2 * (n + m) + len(sems):]
        start(0, ins, outs, s1)
        if m:
            start2(0, ins2, outs2, s2)
        wait(0, ins, outs, s1)
        if m:
            wait2(0, ins2, outs2, s2)
        for rnd in (1, 2):
            start(rnd, ins, outs, s1)
            wait(rnd, ins, outs, s1)

    any_spec = pl.BlockSpec(memory_space=pl.ANY)
    return pl.pallas_call(
        body, name=name, in_specs=[any_spec] * (n + m), out_specs=[any_spec] * (n + m),
        out_shape=[jax.ShapeDtypeStruct((N_DEV,) + s.shape, s.dtype) for s in list(shards) + list(begin_only)],
        scratch_shapes=list(sems) + list(sems2),
    )(*shards, *begin_only)


def _gather_riders(shards):
    n = len(shards)
    start, wait, sems = _gather_rounds(shards)
    out_shape = [jax.ShapeDtypeStruct((N_DEV,) + s.shape, s.dtype) for s in shards]

    def make(rnd, bufs=()):
        return _Rider(list(shards) + list(bufs), out_shape, {n + i: i for i in range(len(bufs))}, sems,
                      lambda r_in, r_out, s: start(rnd, r_in[:n], r_out, s), lambda r_in, r_out, s: wait(rnd, r_in[:n], r_out, s))
    return make


def _gather_rounds(shards):
    n = len(shards)
    split = [s.ndim >= 2 and s.shape[0] % HALF_ROWS_ALIGN == 0 for s in shards]
    n_copies = 10

    def plan(ins, outs, sem_refs):
        send_sems, recv_sems, local_sems = sem_refs
        x, y, c = _coords()
        slot = lambda p: 4 * p[0] + 2 * p[1] + p[2]
        me, xp, yp, dg = (x, y, c), (1 - x, y, c), (x, 1 - y, c), (1 - x, 1 - y, c)
        other_core = lambda p: (p[0], p[1], 1 - p[2])

        def rows(ref, i, half):
            r = shards[i].shape[0]
            if half == "all" or not split[i]:
                return ref
            return ref.at[pl.ds(0, r // 2)] if half == "a" else ref.at[pl.ds(r // 2, r // 2)]

        def copy(i, j, holder, half, to, own=False):
            dst = rows(outs[i].at[slot(holder)], i, half)
            src = rows(ins[i], i, half) if own else dst
            return pltpu.make_async_remote_copy(src_ref=src, dst_ref=dst, send_sem=send_sems.at[i, j],
                                                recv_sem=recv_sems.at[i, j], device_id=to, device_id_type=MESH)

        rounds = [
            ([(0, me, "a", xp, True), (1, me, "b", yp, True)],
             [(0, xp, "a"), (1, yp, "b")]),
            ([(2, me, "a", yp, True), (3, xp, "a", yp, False), (4, me, "b", xp, True), (5, yp, "b", xp, False)],
             [(2, yp, "a"), (3, dg, "a"), (4, xp, "b"), (5, dg, "b")]),
            ([(6, me, "all", other_core(me), True)] + [(7 + q, p, "all", other_core(me), False) for q, p in enumerate((xp, yp, dg))],
             [(6 + q, other_core(p), "all") for q, p in enumerate((me, xp, yp, dg))]),
        ]
        local = lambda: [pltpu.make_async_copy(ins[i], outs[i].at[slot(me)], local_sems.at[i]) for i in range(n)]
        travels = lambda i, half: half != "b" or split[i]
        return rounds, copy, local, travels, me

    def start(rnd, ins, outs, sem_refs):
        rounds, copy, local, travels, _ = plan(ins, outs, sem_refs)
        if rnd == 0:
            for cp in local():
                cp.start()
        for j, holder, half, to, own in rounds[rnd][0]:
            for i in range(n):
                if travels(i, half):
                    copy(i, j, holder, half, to, own).start()

    def wait(rnd, ins, outs, sem_refs):
        rounds, copy, local, travels, me = plan(ins, outs, sem_refs)
        for j, holder, half in rounds[rnd][1]:
            for i in range(n):
                if travels(i, half):
                    copy(i, j, holder, half, me).wait_recv()
        for j, holder, half, to, own in rounds[rnd][0]:
            for i in range(n):
                if travels(i, half):
                    copy(i, j, holder, half, to, own).wait_send()
        if rnd == 0:
            for cp in local():
                cp.wait()

    sems = [pltpu.SemaphoreType.DMA((n, n_copies)), pltpu.SemaphoreType.DMA((n, n_copies)), pltpu.SemaphoreType.DMA((n,))]
    return start, wait, sems


def _swap_rider(bufs, axes):
    n = len(bufs)

    def copies(ins, outs, sem_refs):
        x, y, c = _coords()
        to = {"x": (1 - x, y, c), "y": (x, 1 - y, c), "c": (x, y, 1 - c)}
        return [pltpu.make_async_remote_copy(src_ref=ins[i], dst_ref=outs[i], send_sem=sem_refs[0].at[i], recv_sem=sem_refs[1].at[i],
                                             device_id=to[axes[i]], device_id_type=MESH) for i in range(n)]

    def start(ins, outs, sem_refs):
        for cp in copies(ins, outs, sem_refs):
            cp.start()

    def wait(ins, outs, sem_refs):
        for cp in copies(ins, outs, sem_refs):
            cp.wait()

    return _Rider(list(bufs), [jax.ShapeDtypeStruct(b.shape, b.dtype) for b in bufs], {},
                  [pltpu.SemaphoreType.DMA((n,)), pltpu.SemaphoreType.DMA((n,))], start, wait)


def _swap(bufs, axes, name):
    n = len(bufs)
    rider = _swap_rider(bufs, axes)

    def body(*refs):
        rider.start(refs[:n], refs[n:2 * n], refs[2 * n:])
        rider.wait(refs[:n], refs[n:2 * n], refs[2 * n:])

    any_spec = pl.BlockSpec(memory_space=pl.ANY)
    return pl.pallas_call(body, name=name, in_specs=[any_spec] * n, out_specs=[any_spec] * n, out_shape=rider.out_shape,
                          scratch_shapes=rider.sems)(*bufs)


def _as2d(a):
    return a.reshape(-1, a.shape[-1])


def _add_f32(a, b, name):
    (out,), _ = _rowwise(lambda r, s: ([r[0].astype(F32) + r[1].astype(F32)], []), [_as2d(a), _as2d(b)], [],
                         [(a.shape[-1], F32)], [], name)
    return out.reshape(a.shape)


class _ReduceScatter:
    def __init__(self, parts, name):
        self.n, self.name, self.rnd = len(parts), name, 0
        self.cur = [p.reshape((2, 2, 2) + p.shape[1:]) for p in parts]

    def stage(self):
        x, y, c = _coords()
        n = self.n
        if self.rnd == 0:
            bufs, idxs, dims, axes = self.cur, [c] * n, [2] * n, ["c"] * n
        elif self.rnd == 1:
            bufs = [p[:, :, :p.shape[2] // 2] for p in self.cur] + [p[:, :, p.shape[2] // 2:] for p in self.cur]
            idxs, dims, axes = [y] * n + [x] * n, [1] * n + [0] * n, ["y"] * n + ["x"] * n
        else:
            bufs, idxs, dims, axes = self.cur, [x] * n + [y] * n, [0] * (2 * n), ["x"] * n + ["y"] * n
        pick = lambda p, idx, dim: lax.dynamic_index_in_dim(p, idx, dim, keepdims=False)
        self.keep = [pick(p, i_, d_) for p, i_, d_ in zip(bufs, idxs, dims)]
        return [pick(p, 1 - i_, d_).astype(BF16) for p, i_, d_ in zip(bufs, idxs, dims)], axes

    def absorb(self, got):
        self.cur = [_add_f32(k, g, f"{self.name}_add{self.rnd}_{i}") for i, (k, g) in enumerate(zip(self.keep, got))]
        self.rnd += 1

    def result(self):
        return [jnp.concatenate([self.cur[i], self.cur[self.n + i]], axis=0) for i in range(self.n)]


def _reduce_scatter(parts, name):
    rs = _ReduceScatter(parts, name)
    for rnd in range(3):
        send, axes = rs.stage()
        rs.absorb(_swap(send, axes, f"{name}_swap{rnd}"))
    return rs.result()


def _pack(arrs):
    flat = jnp.concatenate([a.reshape(-1).astype(F32) for a in arrs])
    rows = -(-flat.shape[0] // LANES)
    rows = -(-rows // 512) * 512
    return jnp.pad(flat, (0, rows * LANES - flat.shape[0])).reshape(rows, LANES)


def _unpack(buf, like):
    flat, out, off = buf.reshape(-1), [], 0
    for a in like:
        out.append(flat[off:off + a.size].reshape(a.shape))
        off += a.size
    return out


class _AllReduce:
    def __init__(self, arrs, name, narrow=()):
        self.n, self.name, self.narrow = len(arrs), name, narrow
        self.wide = [a for i, a in enumerate(arrs) if i not in narrow]
        self.thin = [a for i, a in enumerate(arrs) if i in narrow]
        self.packs = [_pack(self.wide)] + ([_pack(self.thin).astype(BF16)] if self.thin else [])
        self.make = _gather_riders(self.packs)

    def finish(self, gathered):
        sums = []
        for j, g in enumerate(gathered):
            (tot,), _ = _rowwise(lambda r, s: ([functools.reduce(lambda u, v: u + v, r)], []), [g[i] for i in range(N_DEV)], [],
                                 [(LANES, F32)], [], f"{self.name}_sum{j}")
            sums.append(tot)
        out_wide = iter(_unpack(sums[0], self.wide))
        out_thin = iter(_unpack(sums[1], self.thin) if self.thin else [])
        return [next(out_thin) if i in self.narrow else next(out_wide) for i in range(self.n)]


def _all_reduce(arrs, name, narrow=()):
    ar = _AllReduce(arrs, name, narrow)
    return ar.finish(_all_gather(ar.packs, name + "_gather"))


def _adamw(w, g, m, v, name):
    def fn(r, s):
        w_, g_, m_, v_ = r
        m_ = ADAM_B1 * m_ + (1.0 - ADAM_B1) * g_
        v_ = ADAM_B2 * v_ + (1.0 - ADAM_B2) * jnp.square(g_)
        m_hat = m_ / (1.0 - ADAM_B1 ** ADAM_STEP)
        v_hat = v_ / (1.0 - ADAM_B2 ** ADAM_STEP)
        return [-ADAM_LR * (m_hat / (jnp.sqrt(v_hat) + ADAM_EPS) + ADAM_WD * w_), m_, v_], []

    c = w.shape[-1]
    (d, nm, nv), _ = _rowwise(fn, [_as2d(a) for a in (w, g, m, v)], [], [(c, F32)] * 3, [], name)
    return d.reshape(w.shape), nm.reshape(w.shape), nv.reshape(w.shape)


def _norm_mod(x, gain, sc, sh, name):
    def fn(r, s):
        xv = r[0]
        xn = xv * lax.rsqrt(_rowmean(xv * xv) + EPS)
        return [xn * s[0] * (1.0 + s[1]) + s[2]], []
    (h,), _ = _rowwise(fn, [x], [gain, sc, sh], [(x.shape[1], BF16)], [], name)
    return h


def _norm_mod_bwd(x, dh, dres, gain, sc, name):
    d = x.shape[1]

    def fn(r, s):
        xv, dhv = r[0], r[1]
        rs = lax.rsqrt(_rowmean(xv * xv) + EPS)
        xn = xv * rs
        dxn = dhv * (1.0 + s[1]) * s[0]
        dx = rs * (dxn - xn * _rowmean(dxn * xn))
        if dres is not None:
            dx = dx + r[2]
        return [dx], [_colsum(dhv), _colsum(dhv * xn * s[0]), _colsum(dhv * (1.0 + s[1]) * xn)]
    (dx,), accs = _rowwise(fn, [x, dh] + ([dres] if dres is not None else []), [gain, sc], [(d, F32)], [d, d, d], name)
    return dx, accs


def _ada_rows(mod_row, d):
    return [mod_row[:, i * d:(i + 1) * d] for i in range(N_MOD)]


def _to_col_major(t):
    n, c = t.shape
    return t.reshape(n // GRID_W, GRID_W, c).transpose(1, 0, 2).reshape(n, c)


def _to_row_major(t):
    n, c = t.shape
    return t.reshape(GRID_W, n // GRID_W, c).transpose(1, 0, 2).reshape(n, c)


def kernel(x, c, ctx, c_ctx, w_ada, b_ada, norm_mix, norm_ffn, w_in, ssd_conv_w, ssd_conv_b, ssd_dt_bias, ssd_a_log, ssd_d, ssd_norm, w_out_ssd, lru_conv_w, lru_conv_b, lru_w_a, lru_b_a, lru_w_x, lru_b_x, lru_lambda, w_out_lru, w_gate, b_gate, w_o, ffn_w13, ffn_w2, final_norm, loss_target, m_c_ctx, m_w_ada, m_b_ada, m_norm_mix, m_norm_ffn, m_w_in, m_ssd_conv_w, m_ssd_conv_b, m_ssd_dt_bias, m_ssd_a_log, m_ssd_d, m_ssd_norm, m_w_out_ssd, m_lru_conv_w, m_lru_conv_b, m_lru_w_a, m_lru_b_a, m_lru_w_x, m_lru_b_x, m_lru_lambda, m_w_out_lru, m_w_gate, m_b_gate, m_w_o, m_ffn_w13, m_ffn_w2, m_final_norm, v_c_ctx, v_w_ada, v_b_ada, v_norm_mix, v_norm_ffn, v_w_in, v_ssd_conv_w, v_ssd_conv_b, v_ssd_dt_bias, v_ssd_a_log, v_ssd_d, v_ssd_norm, v_w_out_ssd, v_lru_conv_w, v_lru_conv_b, v_lru_w_a, v_lru_b_a, v_lru_w_x, v_lru_b_x, v_lru_lambda, v_w_out_lru, v_w_gate, v_b_gate, v_w_o, v_ffn_w13, v_ffn_w2, v_final_norm):
    names = ["c_ctx", "w_ada", "b_ada", "norm_mix", "norm_ffn", "w_in", "ssd_conv_w", "ssd_conv_b", "ssd_dt_bias", "ssd_a_log",
             "ssd_d", "ssd_norm", "w_out_ssd", "lru_conv_w", "lru_conv_b", "lru_w_a", "lru_b_a", "lru_w_x", "lru_b_x", "lru_lambda",
             "w_out_lru", "w_gate", "b_gate", "w_o", "ffn_w13", "ffn_w2", "final_norm"]
    wts = dict(zip(names, (c_ctx, w_ada, b_ada, norm_mix, norm_ffn, w_in, ssd_conv_w, ssd_conv_b, ssd_dt_bias, ssd_a_log, ssd_d, ssd_norm, w_out_ssd, lru_conv_w, lru_conv_b, lru_w_a, lru_b_a, lru_w_x, lru_b_x, lru_lambda, w_out_lru, w_gate, b_gate, w_o, ffn_w13, ffn_w2, final_norm)))
    mom1 = dict(zip(names, (m_c_ctx, m_w_ada, m_b_ada, m_norm_mix, m_norm_ffn, m_w_in, m_ssd_conv_w, m_ssd_conv_b, m_ssd_dt_bias, m_ssd_a_log, m_ssd_d, m_ssd_norm, m_w_out_ssd, m_lru_conv_w, m_lru_conv_b, m_lru_w_a, m_lru_b_a, m_lru_w_x, m_lru_b_x, m_lru_lambda, m_w_out_lru, m_w_gate, m_b_gate, m_w_o, m_ffn_w13, m_ffn_w2, m_final_norm)))
    mom2 = dict(zip(names, (v_c_ctx, v_w_ada, v_b_ada, v_norm_mix, v_norm_ffn, v_w_in, v_ssd_conv_w, v_ssd_conv_b, v_ssd_dt_bias, v_ssd_a_log, v_ssd_d, v_ssd_norm, v_w_out_ssd, v_lru_conv_w, v_lru_conv_b, v_lru_w_a, v_lru_b_a, v_lru_w_x, v_lru_b_x, v_lru_lambda, v_w_out_lru, v_w_gate, v_b_gate, v_w_o, v_ffn_w13, v_ffn_w2, v_final_norm)))

    xi, yi, ci_ = _coords()
    me = 4 * xi + 2 * yi + ci_
    x2, tgt, ctx2 = x[0], loss_target[0], ctx[0]
    t, d = x2.shape
    tc = ctx2.shape[0]
    ws = N_DEV * w_out_ssd.shape[1]
    wl = N_DEV * w_out_lru.shape[1]
    heads = ws // SSD_HEAD_DIM
    gn = SSD_GROUPS * SSD_STATE
    nh = wl // LRU_HEAD_DIM
    ff = N_DEV * ffn_w2.shape[1]
    ada_cols = w_ada.shape[2]

    smalls = [c, ssd_conv_w[0], lru_conv_w[0], lru_b_a[0], lru_b_x[0], lru_lambda[0]]
    gate_shard = [w_gate[0].astype(BF16)]
    got = _all_gather(smalls + [w_in[0].astype(BF16)], "gather_weights", begin_only=gate_shard)
    gate_g, gate_bufs = _gather_riders(gate_shard), got[7:]
    late_a = _gather_riders([ffn_w13[0].astype(BF16)])
    late_b = _gather_riders([w.astype(BF16) for w in (ffn_w2[0], w_out_ssd[0], w_out_lru[0], w_o[0])])
    c_all = got[0].reshape(N_DEV, d)
    colcat = lambda g: jnp.moveaxis(g, 0, -2).reshape(g.shape[1:-1] + (N_DEV * g.shape[-1],))
    conv_w_s, conv_w_l, lru_ba, lru_bx, lru_lam = (colcat(g) for g in got[1:6])
    w_in_f = colcat(got[6])
    o_z, o_xbc, o_dt, o_xr, o_yr = 0, ws, 2 * ws + 2 * gn, 2 * ws + 2 * gn + 2 * heads, 2 * ws + 2 * gn + 2 * heads + wl
    w_z, w_xbc = w_in_f[:, o_z:o_xbc], w_in_f[:, o_xbc:o_dt]
    w_dt = jnp.pad(w_in_f[:, o_dt:o_xr], ((0, 0), (0, LANES - 2 * heads)))
    w_xr, w_yr = w_in_f[:, o_xr:o_yr], w_in_f[:, o_yr:]

    cond = jnp.concatenate([c_all, c_ctx[None, :], jnp.zeros((16 - N_DEV - 1, d), F32)], axis=0)
    cond_act = _silu(cond)
    mod_l = _mm([(cond_act, w_ada[0])], "nn", F32, "ada_fwd")
    (mod_g,) = _all_gather([mod_l], "gather_mod")
    mod_full = colcat(mod_g) + b_ada
    sh_m, sc_m, g_m, sh_f, sc_f, g_f = _ada_rows(lax.dynamic_slice_in_dim(mod_full, me, 1, 0), d)
    csh_m, csc_m = _ada_rows(mod_full[N_DEV:N_DEV + 1], d)[:2]

    bias_s, alog_s = ssd_dt_bias[0], ssd_a_log[0]
    d_full = jnp.repeat(ssd_d[0], SSD_HEAD_DIM)[None, :]
    zeros_state = jnp.zeros((heads // 2, SSD_STATE, LANES), F32)
    zeros_l = jnp.zeros((1, wl), F32)
    wa, wx = lru_w_a[0], lru_w_x[0]

    def mixer_inputs(hh, tag, full, riders=None):
        xbc = _mm([(hh, w_xbc)], "nn", F32, f"{tag}_proj_xbc", rider=riders and riders[0])
        dtr = _mm([(hh, w_dt)], "nn", F32, f"{tag}_proj_dt")[:, :2 * heads]
        xr = _mm([(hh, w_xr)], "nn", F32, f"{tag}_proj_xr", rider=riders and riders[1])
        if riders:
            (xbc, rode0), (xr, rode1) = xbc, xr
            rode = list(rode0) + list(rode1)
        cx = _conv(xbc, conv_w_s, ssd_conv_b, CONV_LEFT, True, f"{tag}_ssd_conv")
        xr_seq = _to_col_major(xr) if full else xr
        xc = _conv(xr_seq, conv_w_l, lru_conv_b, CONV_LEFT, False, f"{tag}_lru_conv")
        return ((xbc, dtr, xr_seq, cx, xc), rode) if riders else (xbc, dtr, xr_seq, cx, xc)

    def lru_args(dirn):
        return wa[dirn], wx[dirn], lru_ba[dirn:dirn + 1], lru_bx[dirn:dirn + 1], lru_lam[dirn:dirn + 1]

    hc = _norm_mod(ctx2, norm_mix, csc_m, csh_m, "ctx_norm")
    c_xbc, c_dtr, c_xr_seq, c_cx, c_xc = mixer_inputs(hc, "ctx", False)
    c_ssd = []
    for dirn in (0, 1):
        _, se, sf = _ssd_fwd(c_cx, c_dtr[:, dirn * heads:(dirn + 1) * heads], bias_s[dirn:dirn + 1], alog_s[dirn:dirn + 1],
                             zeros_state, bool(dirn), f"ctx_ssd_fwd{dirn}")
        c_ssd.append((se, sf))
    c_lru = []
    for dirn in (0, 1):
        res = _lru_fwd(c_xc, *lru_args(dirn), zeros_l, None, bool(dirn), f"ctx_lru_fwd{dirn}")
        c_lru.append((res[0], res[1]))

    h = _norm_mod(x2, norm_mix, sc_m, sh_m, "lat_norm")
    z, gate_bufs = _mm([(h, w_z)], "nn", BF16, "lat_proj_z", rider=gate_g(1, gate_bufs))
    yr, (wg_gate,) = _mm([(h, w_yr)], "nn", BF16, "lat_proj_yr", rider=gate_g(2, gate_bufs))
    w_gf = colcat(wg_gate)
    w_gs, w_gr = w_gf[:, :d], w_gf[:, d:]
    gp_s = _mm([(h, w_gs)], "nn", BF16, "lat_proj_gs")
    gp_r = _mm([(h, w_gr)], "nn", BF16, "lat_proj_gr")
    (l_xbc, l_dtr, l_xr_seq, l_cx, l_xc), late = mixer_inputs(h, "lat", True, riders=(late_a(0), late_b(0)))
    got_a, got_b = late[:1], late[1:]
    ssd_args = lambda dirn: (l_cx, l_dtr[:, dirn * heads:(dirn + 1) * heads], bias_s[dirn:dirn + 1], alog_s[dirn:dirn + 1], c_ssd[dirn][1], bool(dirn))
    (yv0, se0, _), got_a = _ssd_fwd(*ssd_args(0), "lat_ssd_fwd0", rider=late_a(1, got_a))
    (yv1, se1, _), got_b = _ssd_fwd(*ssd_args(1), "lat_ssd_fwd1", rider=late_b(1, got_b))
    l_ssd = [(yv0, se0), (yv1, se1)]
    (hf, _), late = _lru_fwd(l_xc, *lru_args(0), c_lru[0][1], None, False, "lat_lru_fwd0", rider=_merge_riders(late_a(2, got_a), late_b(2, got_b)))
    (wg_13,), (wg_2, wg_os, wg_ol, wg_o) = late[:1], late[1:]
    hb, _, r_seq = _lru_fwd(l_xc, *lru_args(1), c_lru[1][1], hf, True, "lat_lru_fwd1")
    w_os, w_ol, w_oo, w_2 = wg_os.reshape(ws, d), wg_ol.reshape(wl, d), wg_o.reshape(d, d), wg_2.reshape(ff, d)
    w_13f = colcat(wg_13)
    w_1, w_3 = w_13f[:, :ff], w_13f[:, ff:]

    def ssd_out(r, s):
        yf, yb, cx_, z_ = r
        yv = yf + yb + s[0] * cx_[:, :ws]
        yz = yv * _silu(z_)
        return [yz * lax.rsqrt(_rowmean(yz * yz) + EPS) * s[1]], []
    (ys,), _ = _rowwise(ssd_out, [l_ssd[0][0], l_ssd[1][0], l_cx, z], [d_full, ssd_norm], [(ws, BF16)], [], "lat_ssd_out")
    o_s = _mm([(ys, w_os)], "nn", BF16, "lat_out_ssd")

    r_out = _to_row_major(r_seq)
    (gr,), _ = _rowwise(lambda r, s: ([r[0] * _gelu(r[1])], []), [r_out, yr], [], [(wl, BF16)], [], "lat_lru_out")
    o_r = _mm([(gr, w_ol)], "nn", BF16, "lat_out_lru")

    bg_s, bg_r = b_gate[:, :d], b_gate[:, d:]
    (mix,), _ = _rowwise(lambda r, s: ([_sigmoid(r[0] + s[0]) * r[2] + _sigmoid(r[1] + s[1]) * r[3]], []),
                         [gp_s, gp_r, o_s, o_r], [bg_s, bg_r], [(d, BF16)], [], "lat_merge")
    mixo = _mm([(mix, w_oo)], "nn", F32, "lat_out_o")

    def resid_norm(r, s):
        x1v = r[0] + s[0] * r[1]
        xn = x1v * lax.rsqrt(_rowmean(x1v * x1v) + EPS)
        return [x1v, xn * s[1] * (1.0 + s[2]) + s[3]], []
    (x1, h2), _ = _rowwise(resid_norm, [x2, mixo], [g_m, norm_ffn, sc_f, sh_f], [(d, F32), (d, BF16)], [], "lat_resid1")
    fg = _mm([(h2, w_1)], "nn", BF16, "ffn_gate")
    fu = _mm([(h2, w_3)], "nn", BF16, "ffn_up")
    (act,), _ = _rowwise(lambda r, s: ([_silu(r[0]) * r[1]], []), [fg, fu], [], [(ff, BF16)], [], "ffn_act")
    fo = _mm([(act, w_2)], "nn", F32, "ffn_down")

    fn_gain = final_norm[None, :]

    def head(r, s):
        x1v, fv, tv = r
        xv = x1v + s[0] * fv
        rs = lax.rsqrt(_rowmean(xv * xv) + EPS)
        xn = xv * rs
        err = xn * s[1] - tv
        dyv = err * (1.0 / d)
        dxn = dyv * s[1]
        dxv = rs * (dxn - xn * _rowmean(dxn * xn))
        return [dxv, dxv * s[0]], [_colsum(err * err), _colsum(dyv * xn), _colsum(dxv * fv)]
    (dx2, dfo), (sq, g_final, dg_f) = _rowwise(head, [x1, fo, tgt], [g_f, fn_gain], [(d, F32), (d, BF16)], [d, d, d], "loss_head")
    loss = lax.psum(0.5 * jnp.sum(sq) / d, ("x", "y", "c"))

    dact = _mm([(dfo, w_2)], "nt", BF16, "ffn_down_dx")
    gw_2 = _mm([(act, dfo)], "tn", F32, "ffn_down_dw")
    (dfg, dfu), _ = _rowwise(lambda r, s: ([r[0] * r[2] * _dsilu(r[1]), r[0] * _silu(r[1])], []), [dact, fg, fu], [],
                             [(ff, BF16), (ff, BF16)], [], "ffn_act_bwd")
    gw_13 = _mm([(h2, dfg)], "tn", F32, "ffn_gate_dw", slots=(ff // 4, N_DEV, 0, None))
    gw_13 = _mm([(h2, dfu)], "tn", F32, "ffn_up_dw", slots=(ff // 4, N_DEV, N_DEV // 2, gw_13))
    slots = lambda g: jnp.moveaxis(g.reshape(g.shape[0], N_DEV, g.shape[1] // N_DEV), 1, 0)
    rslots = lambda g: g.reshape(N_DEV, g.shape[0] // N_DEV, g.shape[1])
    ffn_rs = _ReduceScatter([gw_13, rslots(gw_2)], "ffn_rs")
    dh2 = _mm([(dfg, w_1)], "nt", F32, "ffn_dx0")
    dh2, got = _mm([(dfu, w_3)], "nt", F32, "ffn_dx1", init=dh2, rider=_swap_rider(*ffn_rs.stage()))
    ffn_rs.absorb(got)
    dx1, (dsh_f, dsc_f, g_norm_ffn) = _norm_mod_bwd(x1, dh2, dx2, norm_ffn, sc_f, "lat_norm_ffn_bwd")

    (dmixo,), (dg_m,) = _rowwise(lambda r, s: ([s[0] * r[0]], [_colsum(r[0] * r[1])]), [dx1, mixo], [g_m], [(d, BF16)], [d], "lat_resid1_bwd")
    dmix = _mm([(dmixo, w_oo)], "nt", BF16, "lat_out_o_dx")
    gw_o = _mm([(mix, dmixo)], "tn", F32, "lat_out_o_dw")

    def merge_bwd(r, s):
        dm, gps, gpr, os_, or_ = r
        g1, g2 = _sigmoid(gps + s[0]), _sigmoid(gpr + s[1])
        dg1, dg2 = dm * os_ * g1 * (1.0 - g1), dm * or_ * g2 * (1.0 - g2)
        return [dm * g1, dm * g2, dg1, dg2], [_colsum(dg1), _colsum(dg2)]
    (do_s, do_r, dgp_s, dgp_r), (gb_gs, gb_gr) = _rowwise(merge_bwd, [dmix, gp_s, gp_r, o_s, o_r], [bg_s, bg_r],
                                                            [(d, BF16)] * 4, [d, d], "lat_merge_bwd")
    gw_gate = _mm([(h, dgp_s)], "tn", F32, "lat_gs_dw", slots=(d // 4, N_DEV, 0, None))
    gw_gate = _mm([(h, dgp_r)], "tn", F32, "lat_gr_dw", slots=(d // 4, N_DEV, N_DEV // 2, gw_gate))

    dgr = _mm([(do_r, w_ol)], "nt", BF16, "lat_out_lru_dx")
    gw_ol = _mm([(gr, do_r)], "tn", F32, "lat_out_lru_dw")
    (dr_out, dyr), _ = _rowwise(lambda r, s: ([r[0] * _gelu(r[2]), r[0] * r[1] * _dgelu(r[2])], []), [dgr, r_out, yr], [],
                                [(wl, F32), (wl, BF16)], [], "lat_lru_out_bwd")
    dr_seq = _to_col_major(dr_out)
    dys = _mm([(do_s, w_os)], "nt", BF16, "lat_out_ssd_dx")
    gw_os = _mm([(ys, do_s)], "tn", F32, "lat_out_ssd_dw")
    mid_rs = _ReduceScatter([rslots(gw_os), rslots(gw_ol), gw_gate, rslots(gw_o)], "mid_rs")
    n_ffn = 2 * ffn_rs.n
    lb0, got = _lru_bwd(l_xc, *lru_args(0), c_lru[0][1], hf, dr_seq, zeros_l, None, False, "lat_lru_bwd0",
                        rider=_merge_riders(_swap_rider(*ffn_rs.stage()), _swap_rider(*mid_rs.stage())))
    ffn_rs.absorb(got[:n_ffn])
    mid_rs.absorb(got[n_ffn:])
    lb1, got = _lru_bwd(l_xc, *lru_args(1), c_lru[1][1], hb, dr_seq, zeros_l, lb0[0], True, "lat_lru_bwd1",
                        rider=_merge_riders(_swap_rider(*ffn_rs.stage()), _swap_rider(*mid_rs.stage())))
    ffn_rs.absorb(got[:n_ffn])
    mid_rs.absorb(got[n_ffn:])
    dxr_seq, g_lconv_w, g_lconv_b = _conv_bwd(l_xr_seq, [lb1[0]], lambda v: v[0], conv_w_l, lru_conv_b, False, "lat_lru_conv_bwd")
    dxr = _to_row_major(dxr_seq)


    def ssd_out_bwd(r, s):
        yf, yb, cx_, z_, dyn = r
        xs_ = cx_[:, :ws]
        yv = yf + yb + s[0] * xs_
        sz = _silu(z_)
        yz = yv * sz
        rs = lax.rsqrt(_rowmean(yz * yz) + EPS)
        yzn = yz * rs
        dyzn = dyn * s[1]
        dyz = rs * (dyzn - yzn * _rowmean(dyzn * yzn))
        dyv = dyz * sz
        return [dyv, dyz * yv * _dsilu(z_), dyv * s[0]], [_colsum(dyv * xs_), _colsum(dyn * yzn)]
    (dy_ssd, dz, dxs_skip), (g_dfull, g_ssd_norm) = _rowwise(ssd_out_bwd, [l_ssd[0][0], l_ssd[1][0], l_cx, z, dys], [d_full, ssd_norm],
                                                              [(ws, F32), (ws, BF16), (ws, F32)], [ws, ws], "lat_ssd_out_bwd")
    sb0, got = _ssd_bwd(l_cx, l_dtr[:, :heads], bias_s[0:1], alog_s[0:1], dy_ssd, l_ssd[0][1], zeros_state, False, "lat_ssd_bwd0",
                        rider=_swap_rider(*mid_rs.stage()))
    mid_rs.absorb(got)
    sb = [sb0, _ssd_bwd(l_cx, l_dtr[:, heads:], bias_s[1:2], alog_s[1:2], dy_ssd, l_ssd[1][1], zeros_state, True, "lat_ssd_bwd1")]
    dxbc, g_sconv_w, g_sconv_b = _conv_bwd(
        l_xbc, [dxs_skip, sb[0][0], sb[1][0], sb[0][1], sb[1][1], sb[0][2], sb[1][2]],
        lambda v: jnp.concatenate([v[0] + v[1] + v[2], v[3] + v[4], v[5] + v[6]], axis=1), conv_w_s, ssd_conv_b, True, "lat_ssd_conv_bwd")
    ddt = jnp.pad(jnp.concatenate([sb[0][3], sb[1][3]], axis=1), ((0, 0), (0, LANES - 2 * heads)))

    zeros_cy = jnp.zeros((tc, ws), F32)
    csb = [_ssd_bwd(c_cx, c_dtr[:, dirn * heads:(dirn + 1) * heads], bias_s[dirn:dirn + 1], alog_s[dirn:dirn + 1],
                    zeros_cy, c_ssd[dirn][0], sb[dirn][6], bool(dirn), f"ctx_ssd_bwd{dirn}") for dirn in (0, 1)]
    c_dxbc, cg_sconv_w, cg_sconv_b = _conv_bwd(
        c_xbc, [csb[0][0], csb[1][0], csb[0][1], csb[1][1], csb[0][2], csb[1][2]],
        lambda v: jnp.concatenate([v[0] + v[1], v[2] + v[3], v[4] + v[5]], axis=1), conv_w_s, ssd_conv_b, True, "ctx_ssd_conv_bwd")
    c_ddt = jnp.pad(jnp.concatenate([csb[0][3], csb[1][3]], axis=1), ((0, 0), (0, LANES - 2 * heads)))
    zeros_ch = jnp.zeros((tc, wl), F32)
    clb0 = _lru_bwd(c_xc, *lru_args(0), zeros_l, c_lru[0][0], zeros_ch, lb0[6], None, False, "ctx_lru_bwd0")
    clb1 = _lru_bwd(c_xc, *lru_args(1), zeros_l, c_lru[1][0], zeros_ch, lb1[6], clb0[0], True, "ctx_lru_bwd1")
    c_dxr, cg_lconv_w, cg_lconv_b = _conv_bwd(c_xr_seq, [clb1[0]], lambda v: v[0], conv_w_l, lru_conv_b, False, "ctx_lru_conv_bwd")
    dhc = _mm([(c_dxbc, w_xbc), (c_ddt, w_dt), (c_dxr, w_xr)], "nt", F32, "ctx_proj_dx")
    _, (dcsh_m, dcsc_m, cg_norm_mix) = _norm_mod_bwd(ctx2, dhc, None, norm_mix, csc_m, "ctx_norm_bwd")

    gw_z = _mm([(h, dz)], "tn", F32, "lat_z_dw")
    gw_yr = _mm([(h, dyr)], "tn", F32, "lat_yr_dw")
    gw_xbc = _mm([(h, dxbc), (hc, c_dxbc)], "tn", F32, "proj_xbc_dw")
    gw_dt = _mm([(h, ddt), (hc, c_ddt)], "tn", F32, "proj_dt_dw")[:, :2 * heads]
    gw_xr = _mm([(h, dxr), (hc, c_dxr)], "tn", F32, "proj_xr_dw")
    in_rs = _ReduceScatter([slots(jnp.concatenate([gw_z, gw_xbc, gw_dt, gw_xr, gw_yr], axis=1))], "in_rs")
    small_parts = [
        g_norm_ffn, g_sconv_w + cg_sconv_w, g_sconv_b + cg_sconv_b,
        jnp.concatenate([sb[0][5] + csb[0][5], sb[1][5] + csb[1][5]], axis=0),
        jnp.concatenate([sb[0][4] + csb[0][4], sb[1][4] + csb[1][4]], axis=0),
        g_dfull, g_ssd_norm, g_lconv_w + cg_lconv_w, g_lconv_b + cg_lconv_b,
        jnp.stack([lb0[1] + clb0[1], lb1[1] + clb1[1]]), jnp.concatenate([lb0[3] + clb0[3], lb1[3] + clb1[3]], axis=0),
        jnp.stack([lb0[2] + clb0[2], lb1[2] + clb1[2]]), jnp.concatenate([lb0[4] + clb0[4], lb1[4] + clb1[4]], axis=0),
        jnp.concatenate([lb0[5] + clb0[5], lb1[5] + clb1[5]], axis=0),
        jnp.concatenate([gb_gs, gb_gr], axis=1), g_final,
    ]
    small_ar = _AllReduce(small_parts, "small_grads", narrow=(9, 11))
    dh, bufs = None, ()
    chain = ([(dxr, w_xr)], [(dz, w_z), (dxbc, w_xbc), (ddt, w_dt)], [(dyr, w_yr)], [(dgp_s, w_gs)], [(dgp_r, w_gr)])
    for step, pairs in enumerate(chain):
        rnd = {0: 0, 1: 1, 3: 2}.get(step)
        if rnd is None:
            dh = _mm(pairs, "nt", F32, f"lat_proj_dx{step}", init=dh)
            continue
        send, axes = in_rs.stage()
        dh, got = _mm(pairs, "nt", F32, f"lat_proj_dx{step}", init=dh, rider=_merge_riders(_swap_rider(send, axes), small_ar.make(rnd, bufs)))
        in_rs.absorb(got[:len(send)])
        bufs = got[len(send):]
    grad_x, (dsh_m, dsc_m, g_norm_mix) = _norm_mod_bwd(x2, dh, dx1, norm_mix, sc_m, "lat_norm_bwd")

    zero_d = jnp.zeros((1, d), F32)
    dmod_rows = jnp.concatenate([jnp.concatenate([dsh_m, dsc_m, dg_m, dsh_f, dsc_f, dg_f], axis=1),
                                 jnp.concatenate([dcsh_m, dcsc_m, zero_d, zero_d, zero_d, zero_d], axis=1)], axis=0)
    (dmod_g,) = _all_gather([dmod_rows], "gather_dmod")
    dmod_c = functools.reduce(lambda u, v: u + v, [dmod_g[i, 1] for i in range(N_DEV)])
    dmod9 = jnp.concatenate([dmod_g[:, 0], dmod_c[None, :], jnp.zeros((16 - N_DEV - 1, N_MOD * d), F32)], axis=0)
    g_b_ada = jnp.sum(dmod9, axis=0, keepdims=True)
    dmod_l = lax.dynamic_slice_in_dim(dmod9, me * ada_cols, ada_cols, 1)
    g_w_ada = _mm([(cond_act, dmod_l)], "tn", F32, "ada_dw")
    dcond_l = _mm([(dmod_l, w_ada[0])], "nt", F32, "ada_dx")

    (s_nffn, s_sconv_w, s_sconv_b, s_dtb, s_da, s_dfull, s_snorm, s_lconv_w, s_lconv_b,
     s_wa, s_ba, s_wx, s_bx, s_dsp, s_bgate, s_final) = small_ar.finish(bufs)
    s_cctx, s_nmix = _all_reduce([dcond_l[N_DEV:N_DEV + 1], g_norm_mix + cg_norm_mix], "last_grads")
    shard = lambda a, n_: lax.dynamic_slice_in_dim(a, me * n_, n_, a.ndim - 1)
    grads = {
        "c_ctx": (s_cctx * _dsilu(c_ctx[None, :]))[0],
        "w_ada": g_w_ada[None], "b_ada": g_b_ada, "norm_mix": s_nmix, "norm_ffn": s_nffn,
        "ssd_conv_w": shard(s_sconv_w, ssd_conv_w.shape[2])[None], "ssd_conv_b": s_sconv_b,
        "ssd_dt_bias": s_dtb[None], "ssd_a_log": (s_da * (-jnp.exp(alog_s)))[None],
        "ssd_d": jnp.sum(s_dfull.reshape(heads, SSD_HEAD_DIM), axis=1)[None], "ssd_norm": s_snorm,
        "lru_conv_w": shard(s_lconv_w, lru_conv_w.shape[2])[None], "lru_conv_b": s_lconv_b,
        "lru_w_a": s_wa[None], "lru_b_a": shard(s_ba, lru_b_a.shape[2])[None],
        "lru_w_x": s_wx[None], "lru_b_x": shard(s_bx, lru_b_x.shape[2])[None],
        "lru_lambda": shard(s_dsp * (-_sigmoid(-lru_lam)), lru_lambda.shape[2])[None],
        "b_gate": s_bgate, "final_norm": s_final[0],
    }

    for n_, g in zip(["w_in", "w_out_ssd", "w_out_lru", "w_gate", "w_o", "ffn_w13", "ffn_w2"],
                     in_rs.result() + mid_rs.result() + ffn_rs.result()):
        grads[n_] = g[None]

    big_names = ["w_ada", "w_in", "w_out_ssd", "w_out_lru", "w_gate", "w_o", "ffn_w13", "ffn_w2", "lru_w_a", "lru_w_x"]
    small_names = [n_ for n_ in names if n_ not in big_names]
    delta, new_m, new_v = {}, {}, {}
    for n_ in big_names:
        delta[n_], new_m[n_], new_v[n_] = _adamw(wts[n_], grads[n_], mom1[n_], mom2[n_], "adamw_" + n_)
    packed = [_pack([src[n_] for n_ in small_names]) for src in (wts, grads, mom1, mom2)]
    outs = _adamw(*packed, "adamw_small")
    like = [wts[n_] for n_ in small_names]
    for dst, buf in zip((delta, new_m, new_v), outs):
        dst.update(zip(small_names, _unpack(buf, like)))

    return (loss, grad_x[None], *[grads[n_].reshape(wts[n_].shape) for n_ in names], *[delta[n_] for n_ in names],
            *[new_m[n_] for n_ in names], *[new_v[n_] for n_ in names])
```
